```python
import math
import jax, jax.numpy as jnp
from jax import lax
import numpy as np

D_MODEL = 1024
BATCH = 8
SEQ = 4096
DEPTH = 2

D_MIX = D_MODEL
MLA_HEADS = 8
QK_NOPE_DIM = 64
QK_ROPE_DIM = 32
V_HEAD_DIM = 64
Q_LORA_RANK = 256
KV_LORA_RANK = 128
MLA_WIDTH = MLA_HEADS * V_HEAD_DIM
ROPE_THETA = 10000.0
Q_BLOCK = 128
SSM_WIDTH = D_MIX - MLA_WIDTH
SSM_GROUP = 16
SSM_GROUPS = SSM_WIDTH // SSM_GROUP
SSM_STATE = 64
DT_MIN = 0.001
DT_MAX = 0.1
IN_WIDTH = Q_LORA_RANK + KV_LORA_RANK + QK_ROPE_DIM + SSM_WIDTH
MEM_LEN = 256
X_HEADS = 4
X_HEAD_DIM = D_MODEL // X_HEADS
D_FF = -(-8 * D_MODEL // (3 * 256)) * 256
EPS = 1e-6

kernel_name = 'hybrid_mla_s5_memory_decoder'


def rmsnorm(x, g):
    xf = x.astype(jnp.float32)
    y = xf * lax.rsqrt(jnp.mean(xf * xf, axis=-1, keepdims=True) + EPS)
    return (y * g.astype(jnp.float32)).astype(x.dtype)


def apply_rope(x, cos, sin):
    xf = x.astype(jnp.float32)
    half = xf.shape[-1] // 2
    x1, x2 = xf[..., :half], xf[..., half:]
    return jnp.concatenate([x1 * cos - x2 * sin, x2 * cos + x1 * sin], axis=-1).astype(x.dtype)


def mla_group(c_q, c_kv, k_r, q_norm_g, w_uq, kv_norm_g, w_ukv, cos, sin):
    B, S, _ = c_q.shape
    q = (rmsnorm(c_q, q_norm_g) @ w_uq).reshape(B, S, MLA_HEADS, QK_NOPE_DIM + QK_ROPE_DIM)
    q_nope = q[..., :QK_NOPE_DIM]
    q_rope = apply_rope(q[..., QK_NOPE_DIM:], cos[:, :, None, :], sin[:, :, None, :])
    kv = (rmsnorm(c_kv, kv_norm_g) @ w_ukv).reshape(B, S, MLA_HEADS, QK_NOPE_DIM + V_HEAD_DIM)
    k_nope = kv[..., :QK_NOPE_DIM].transpose(0, 2, 1, 3)
    v = kv[..., QK_NOPE_DIM:].transpose(0, 2, 1, 3)
    k_rope = apply_rope(k_r, cos, sin)
    scale = (QK_NOPE_DIM + QK_ROPE_DIM) ** -0.5
    qn = (q_nope * scale).transpose(0, 2, 1, 3)
    qr = (q_rope * scale).transpose(0, 2, 1, 3)
    outs = []
    for i in range(S // Q_BLOCK):
        lo, hi = i * Q_BLOCK, (i + 1) * Q_BLOCK
        s = (jnp.einsum('bhqd,bhkd->bhqk', qn[:, :, lo:hi], k_nope[:, :, :hi])
             + jnp.einsum('bhqr,bkr->bhqk', qr[:, :, lo:hi], k_rope[:, :hi])).astype(jnp.float32)
        mask = jnp.arange(hi)[None, :] <= jnp.arange(lo, hi)[:, None]
        s = jnp.where(mask, s, -jnp.inf)
        p = jax.nn.softmax(s, axis=-1).astype(v.dtype)
        outs.append(jnp.einsum('bhqk,bhkd->bqhd', p, v[:, :, :hi]))
    o = jnp.concatenate(outs, axis=1)
    return o.reshape(B, S, MLA_WIDTH)


def s5_group(u, lam_re, lam_im, log_dt, b_re, b_im, c_re, c_im, d, w_glu, b_glu):
    B, S, _ = u.shape
    f32 = jnp.float32
    uf = u.astype(f32)
    ug = uf.reshape(B, S, SSM_GROUPS, SSM_GROUP)
    lam = lax.complex(lam_re.astype(f32), lam_im.astype(f32))
    dt = jnp.exp(log_dt.astype(f32))[:, None]
    a_bar = jnp.exp(lam * dt)
    b_mat = lax.complex(b_re.astype(f32), b_im.astype(f32))
    b_bar = ((a_bar - 1.0) / lam)[..., None] * b_mat
    bu = jnp.einsum('bsgc,gpc->bsgp', ug.astype(jnp.complex64), b_bar)
    a_elems = jnp.broadcast_to(a_bar, bu.shape)

    def combine(e1, e2):
        a1, x1 = e1
        a2, x2 = e2
        return a1 * a2, a2 * x1 + x2

    _, states = lax.associative_scan(combine, (a_elems, bu), axis=1)
    c_mat = lax.complex(c_re.astype(f32), c_im.astype(f32))
    y = jnp.einsum('bsgp,gcp->bsgc', states, c_mat).real.reshape(B, S, SSM_WIDTH)
    y = y + d.astype(f32) * uf
    g = jax.nn.gelu(y)
    y = y * jax.nn.sigmoid(g @ w_glu.astype(f32) + b_glu.astype(f32))
    return y.astype(u.dtype)


def memory_cross_attention(hn, memn, w_xq, w_xkv, w_xo):
    B, S, _ = hn.shape
    M = memn.shape[1]
    q = (hn @ w_xq).reshape(B, S, X_HEADS, X_HEAD_DIM)
    kv = (memn @ w_xkv).reshape(B, M, 2, X_HEADS, X_HEAD_DIM)
    k, v = kv[:, :, 0], kv[:, :, 1]
    s = jnp.einsum('bshd,bmhd->bhsm', q, k).astype(jnp.float32) * (X_HEAD_DIM ** -0.5)
    p = jax.nn.softmax(s, axis=-1).astype(v.dtype)
    o = jnp.einsum('bhsm,bmhd->bshd', p, v).reshape(B, S, D_MODEL)
    return o @ w_xo


def swiglu(hn, w_gate, w_up, w_down):
    return (jax.nn.silu(hn @ w_gate) * (hn @ w_up)) @ w_down


def _fwd_setup_inputs(seed: int = 0) -> dict:
    key = jax.random.key(seed)
    ks = jax.random.split(key, 40)
    f32 = jnp.float32

    def nrm(k, shape, fan_in):
        return jax.random.normal(k, shape, f32) * (fan_in ** -0.5)

    def gain(k, shape):
        return 1.0 + 0.05 * jax.random.normal(k, shape, f32)

    L = DEPTH
    x = jax.random.normal(ks[0], (BATCH, SEQ, D_MODEL), f32)
    mem = jax.random.normal(ks[1], (BATCH, MEM_LEN, D_MODEL), f32)
    start = jax.random.randint(ks[2], (BATCH, 1), 0, 1024, dtype=jnp.int32)
    positions = start + jnp.arange(SEQ, dtype=jnp.int32)[None, :]
    n_idx = jnp.arange(SSM_STATE, dtype=f32)
    ssm_lambda_re = -0.5 * jnp.exp(0.05 * jax.random.normal(ks[9], (L, SSM_GROUPS, SSM_STATE), f32))
    ssm_lambda_im = jnp.pi * n_idx + 0.01 * jax.random.normal(ks[10], (L, SSM_GROUPS, SSM_STATE), f32)
    ssm_log_dt = jax.random.uniform(ks[11], (L, SSM_GROUPS), f32, math.log(DT_MIN), math.log(DT_MAX))
    return {
        'x': x,
        'mem': mem,
        'positions': positions,
        'norm_mix_g': gain(ks[3], (L, D_MODEL)),
        'w_in': nrm(ks[4], (L, D_MODEL, IN_WIDTH), D_MODEL),
        'q_norm_g': gain(ks[5], (L, Q_LORA_RANK)),
        'w_uq': nrm(ks[6], (L, Q_LORA_RANK, MLA_HEADS * (QK_NOPE_DIM + QK_ROPE_DIM)), Q_LORA_RANK),
        'kv_norm_g': gain(ks[7], (L, KV_LORA_RANK)),
        'w_ukv': nrm(ks[8], (L, KV_LORA_RANK, MLA_HEADS * (QK_NOPE_DIM + V_HEAD_DIM)), KV_LORA_RANK),
        'ssm_lambda_re': ssm_lambda_re,
        'ssm_lambda_im': ssm_lambda_im,
        'ssm_log_dt': ssm_log_dt,
        'ssm_b_re': nrm(ks[12], (L, SSM_GROUPS, SSM_STATE, SSM_GROUP), 2 * SSM_GROUP),
        'ssm_b_im': nrm(ks[13], (L, SSM_GROUPS, SSM_STATE, SSM_GROUP), 2 * SSM_GROUP),
        'ssm_c_re': nrm(ks[14], (L, SSM_GROUPS, SSM_GROUP, SSM_STATE), 2 * SSM_STATE),
        'ssm_c_im': nrm(ks[15], (L, SSM_GROUPS, SSM_GROUP, SSM_STATE), 2 * SSM_STATE),
        'ssm_d': jax.random.normal(ks[16], (L, SSM_WIDTH), f32),
        'ssm_w_glu': nrm(ks[17], (L, SSM_WIDTH, SSM_WIDTH), SSM_WIDTH),
        'ssm_b_glu': 0.01 * jax.random.normal(ks[18], (L, SSM_WIDTH), f32),
        'attn_out_g': gain(ks[19], (L, MLA_WIDTH)),
        'ssm_out_g': gain(ks[20], (L, SSM_WIDTH)),
        'w_out': nrm(ks[21], (L, D_MIX, D_MODEL), D_MIX),
        'norm_x_g': gain(ks[22], (L, D_MODEL)),
        'mem_norm_g': gain(ks[23], (L, D_MODEL)),
        'w_xq': nrm(ks[24], (L, D_MODEL, D_MODEL), D_MODEL),
        'w_xkv': nrm(ks[25], (L, D_MODEL, 2 * D_MODEL), D_MODEL),
        'w_xo': nrm(ks[26], (L, D_MODEL, D_MODEL), D_MODEL),
        'norm_ffn_g': gain(ks[27], (L, D_MODEL)),
        'w_gate': nrm(ks[28], (L, D_MODEL, D_FF), D_MODEL),
        'w_up': nrm(ks[29], (L, D_MODEL, D_FF), D_MODEL),
        'w_down': nrm(ks[30], (L, D_FF, D_MODEL), D_FF),
        'final_norm_g': gain(ks[31], (D_MODEL,)),
    }


def _fwd_reference(x, mem, positions, norm_mix_g, w_in, q_norm_g, w_uq, kv_norm_g, w_ukv,
              ssm_lambda_re, ssm_lambda_im, ssm_log_dt, ssm_b_re, ssm_b_im, ssm_c_re, ssm_c_im,
              ssm_d, ssm_w_glu, ssm_b_glu, attn_out_g, ssm_out_g, w_out, norm_x_g, mem_norm_g,
              w_xq, w_xkv, w_xo, norm_ffn_g, w_gate, w_up, w_down, final_norm_g):
    freqs = ROPE_THETA ** (-jnp.arange(0, QK_ROPE_DIM, 2, dtype=jnp.float32) / QK_ROPE_DIM)
    ang = positions.astype(jnp.float32)[..., None] * freqs
    cos, sin = jnp.cos(ang), jnp.sin(ang)
    split_at = [Q_LORA_RANK, Q_LORA_RANK + KV_LORA_RANK, Q_LORA_RANK + KV_LORA_RANK + QK_ROPE_DIM]
    h = x
    for l in range(DEPTH):
        xn = rmsnorm(h, norm_mix_g[l])
        proj = xn @ w_in[l]
        c_q, c_kv, k_r, u = jnp.split(proj, split_at, axis=-1)
        a_out = mla_group(c_q, c_kv, k_r, q_norm_g[l], w_uq[l], kv_norm_g[l], w_ukv[l], cos, sin)
        s_out = s5_group(u, ssm_lambda_re[l], ssm_lambda_im[l], ssm_log_dt[l], ssm_b_re[l], ssm_b_im[l],
                         ssm_c_re[l], ssm_c_im[l], ssm_d[l], ssm_w_glu[l], ssm_b_glu[l])
        mixed = jnp.concatenate([rmsnorm(a_out, attn_out_g[l]), rmsnorm(s_out, ssm_out_g[l])], axis=-1)
        h = h + mixed @ w_out[l]
        h = h + memory_cross_attention(rmsnorm(h, norm_x_g[l]), rmsnorm(mem, mem_norm_g[l]),
                                       w_xq[l], w_xkv[l], w_xo[l])
        h = h + swiglu(rmsnorm(h, norm_ffn_g[l]), w_gate[l], w_up[l], w_down[l])
    return rmsnorm(h, final_norm_g)


import jax as _jax
import jax.numpy as _jnp

TWIN_FORMAT = 'train_step'
FWD_PARAMS = ['x', 'mem', 'positions', 'norm_mix_g', 'w_in', 'q_norm_g', 'w_uq', 'kv_norm_g', 'w_ukv', 'ssm_lambda_re', 'ssm_lambda_im', 'ssm_log_dt', 'ssm_b_re', 'ssm_b_im', 'ssm_c_re', 'ssm_c_im', 'ssm_d', 'ssm_w_glu', 'ssm_b_glu', 'attn_out_g', 'ssm_out_g', 'w_out', 'norm_x_g', 'mem_norm_g', 'w_xq', 'w_xkv', 'w_xo', 'norm_ffn_g', 'w_gate', 'w_up', 'w_down', 'final_norm_g']
TWIN_WEIGHTS = ['norm_mix_g', 'w_in', 'q_norm_g', 'w_uq', 'kv_norm_g', 'w_ukv', 'ssm_lambda_re', 'ssm_lambda_im', 'ssm_log_dt', 'ssm_b_re', 'ssm_b_im', 'ssm_c_re', 'ssm_c_im', 'ssm_d', 'ssm_w_glu', 'ssm_b_glu', 'attn_out_g', 'ssm_out_g', 'w_out', 'norm_x_g', 'mem_norm_g', 'w_xq', 'w_xkv', 'w_xo', 'norm_ffn_g', 'w_gate', 'w_up', 'w_down', 'final_norm_g']
TWIN_DIFF_INPUT = 'x'
TWIN_INPUTS = ['x', 'mem', 'positions', 'norm_mix_g', 'w_in', 'q_norm_g', 'w_uq', 'kv_norm_g', 'w_ukv', 'ssm_lambda_re', 'ssm_lambda_im', 'ssm_log_dt', 'ssm_b_re', 'ssm_b_im', 'ssm_c_re', 'ssm_c_im', 'ssm_d', 'ssm_w_glu', 'ssm_b_glu', 'attn_out_g', 'ssm_out_g', 'w_out', 'norm_x_g', 'mem_norm_g', 'w_xq', 'w_xkv', 'w_xo', 'norm_ffn_g', 'w_gate', 'w_up', 'w_down', 'final_norm_g', 'loss_target', 'm_norm_mix_g', 'm_w_in', 'm_q_norm_g', 'm_w_uq', 'm_kv_norm_g', 'm_w_ukv', 'm_ssm_lambda_re', 'm_ssm_lambda_im', 'm_ssm_log_dt', 'm_ssm_b_re', 'm_ssm_b_im', 'm_ssm_c_re', 'm_ssm_c_im', 'm_ssm_d', 'm_ssm_w_glu', 'm_ssm_b_glu', 'm_attn_out_g', 'm_ssm_out_g', 'm_w_out', 'm_norm_x_g', 'm_mem_norm_g', 'm_w_xq', 'm_w_xkv', 'm_w_xo', 'm_norm_ffn_g', 'm_w_gate', 'm_w_up', 'm_w_down', 'm_final_norm_g', 'v_norm_mix_g', 'v_w_in', 'v_q_norm_g', 'v_w_uq', 'v_kv_norm_g', 'v_w_ukv', 'v_ssm_lambda_re', 'v_ssm_lambda_im', 'v_ssm_log_dt', 'v_ssm_b_re', 'v_ssm_b_im', 'v_ssm_c_re', 'v_ssm_c_im', 'v_ssm_d', 'v_ssm_w_glu', 'v_ssm_b_glu', 'v_attn_out_g', 'v_ssm_out_g', 'v_w_out', 'v_norm_x_g', 'v_mem_norm_g', 'v_w_xq', 'v_w_xkv', 'v_w_xo', 'v_norm_ffn_g', 'v_w_gate', 'v_w_up', 'v_w_down', 'v_final_norm_g']
TWIN_OUTPUTS = ['loss', 'grad_x', 'grad_norm_mix_g', 'grad_w_in', 'grad_q_norm_g', 'grad_w_uq', 'grad_kv_norm_g', 'grad_w_ukv', 'grad_ssm_lambda_re', 'grad_ssm_lambda_im', 'grad_ssm_log_dt', 'grad_ssm_b_re', 'grad_ssm_b_im', 'grad_ssm_c_re', 'grad_ssm_c_im', 'grad_ssm_d', 'grad_ssm_w_glu', 'grad_ssm_b_glu', 'grad_attn_out_g', 'grad_ssm_out_g', 'grad_w_out', 'grad_norm_x_g', 'grad_mem_norm_g', 'grad_w_xq', 'grad_w_xkv', 'grad_w_xo', 'grad_norm_ffn_g', 'grad_w_gate', 'grad_w_up', 'grad_w_down', 'grad_final_norm_g', 'delta_norm_mix_g', 'delta_w_in', 'delta_q_norm_g', 'delta_w_uq', 'delta_kv_norm_g', 'delta_w_ukv', 'delta_ssm_lambda_re', 'delta_ssm_lambda_im', 'delta_ssm_log_dt', 'delta_ssm_b_re', 'delta_ssm_b_im', 'delta_ssm_c_re', 'delta_ssm_c_im', 'delta_ssm_d', 'delta_ssm_w_glu', 'delta_ssm_b_glu', 'delta_attn_out_g', 'delta_ssm_out_g', 'delta_w_out', 'delta_norm_x_g', 'delta_mem_norm_g', 'delta_w_xq', 'delta_w_xkv', 'delta_w_xo', 'delta_norm_ffn_g', 'delta_w_gate', 'delta_w_up', 'delta_w_down', 'delta_final_norm_g', 'new_m_norm_mix_g', 'new_m_w_in', 'new_m_q_norm_g', 'new_m_w_uq', 'new_m_kv_norm_g', 'new_m_w_ukv', 'new_m_ssm_lambda_re', 'new_m_ssm_lambda_im', 'new_m_ssm_log_dt', 'new_m_ssm_b_re', 'new_m_ssm_b_im', 'new_m_ssm_c_re', 'new_m_ssm_c_im', 'new_m_ssm_d', 'new_m_ssm_w_glu', 'new_m_ssm_b_glu', 'new_m_attn_out_g', 'new_m_ssm_out_g', 'new_m_w_out', 'new_m_norm_x_g', 'new_m_mem_norm_g', 'new_m_w_xq', 'new_m_w_xkv', 'new_m_w_xo', 'new_m_norm_ffn_g', 'new_m_w_gate', 'new_m_w_up', 'new_m_w_down', 'new_m_final_norm_g', 'new_v_norm_mix_g', 'new_v_w_in', 'new_v_q_norm_g', 'new_v_w_uq', 'new_v_kv_norm_g', 'new_v_w_ukv', 'new_v_ssm_lambda_re', 'new_v_ssm_lambda_im', 'new_v_ssm_log_dt', 'new_v_ssm_b_re', 'new_v_ssm_b_im', 'new_v_ssm_c_re', 'new_v_ssm_c_im', 'new_v_ssm_d', 'new_v_ssm_w_glu', 'new_v_ssm_b_glu', 'new_v_attn_out_g', 'new_v_ssm_out_g', 'new_v_w_out', 'new_v_norm_x_g', 'new_v_mem_norm_g', 'new_v_w_xq', 'new_v_w_xkv', 'new_v_w_xo', 'new_v_norm_ffn_g', 'new_v_w_gate', 'new_v_w_up', 'new_v_w_down', 'new_v_final_norm_g']
TWIN_LEAF_KINDS = {'loss': 'loss', 'grad_x': 'grad_x', 'grad_norm_mix_g': 'grad_w', 'grad_w_in': 'grad_w', 'grad_q_norm_g': 'grad_w', 'grad_w_uq': 'grad_w', 'grad_kv_norm_g': 'grad_w', 'grad_w_ukv': 'grad_w', 'grad_ssm_lambda_re': 'grad_w', 'grad_ssm_lambda_im': 'grad_w', 'grad_ssm_log_dt': 'grad_w', 'grad_ssm_b_re': 'grad_w', 'grad_ssm_b_im': 'grad_w', 'grad_ssm_c_re': 'grad_w', 'grad_ssm_c_im': 'grad_w', 'grad_ssm_d': 'grad_w', 'grad_ssm_w_glu': 'grad_w', 'grad_ssm_b_glu': 'grad_w', 'grad_attn_out_g': 'grad_w', 'grad_ssm_out_g': 'grad_w', 'grad_w_out': 'grad_w', 'grad_norm_x_g': 'grad_w', 'grad_mem_norm_g': 'grad_w', 'grad_w_xq': 'grad_w', 'grad_w_xkv': 'grad_w', 'grad_w_xo': 'grad_w', 'grad_norm_ffn_g': 'grad_w', 'grad_w_gate': 'grad_w', 'grad_w_up': 'grad_w', 'grad_w_down': 'grad_w', 'grad_final_norm_g': 'grad_w', 'delta_norm_mix_g': 'delta_w', 'delta_w_in': 'delta_w', 'delta_q_norm_g': 'delta_w', 'delta_w_uq': 'delta_w', 'delta_kv_norm_g': 'delta_w', 'delta_w_ukv': 'delta_w', 'delta_ssm_lambda_re': 'delta_w', 'delta_ssm_lambda_im': 'delta_w', 'delta_ssm_log_dt': 'delta_w', 'delta_ssm_b_re': 'delta_w', 'delta_ssm_b_im': 'delta_w', 'delta_ssm_c_re': 'delta_w', 'delta_ssm_c_im': 'delta_w', 'delta_ssm_d': 'delta_w', 'delta_ssm_w_glu': 'delta_w', 'delta_ssm_b_glu': 'delta_w', 'delta_attn_out_g': 'delta_w', 'delta_ssm_out_g': 'delta_w', 'delta_w_out': 'delta_w', 'delta_norm_x_g': 'delta_w', 'delta_mem_norm_g': 'delta_w', 'delta_w_xq': 'delta_w', 'delta_w_xkv': 'delta_w', 'delta_w_xo': 'delta_w', 'delta_norm_ffn_g': 'delta_w', 'delta_w_gate': 'delta_w', 'delta_w_up': 'delta_w', 'delta_w_down': 'delta_w', 'delta_final_norm_g': 'delta_w', 'new_m_norm_mix_g': 'new_m', 'new_m_w_in': 'new_m', 'new_m_q_norm_g': 'new_m', 'new_m_w_uq': 'new_m', 'new_m_kv_norm_g': 'new_m', 'new_m_w_ukv': 'new_m', 'new_m_ssm_lambda_re': 'new_m', 'new_m_ssm_lambda_im': 'new_m', 'new_m_ssm_log_dt': 'new_m', 'new_m_ssm_b_re': 'new_m', 'new_m_ssm_b_im': 'new_m', 'new_m_ssm_c_re': 'new_m', 'new_m_ssm_c_im': 'new_m', 'new_m_ssm_d': 'new_m', 'new_m_ssm_w_glu': 'new_m', 'new_m_ssm_b_glu': 'new_m', 'new_m_attn_out_g': 'new_m', 'new_m_ssm_out_g': 'new_m', 'new_m_w_out': 'new_m', 'new_m_norm_x_g': 'new_m', 'new_m_mem_norm_g': 'new_m', 'new_m_w_xq': 'new_m', 'new_m_w_xkv': 'new_m', 'new_m_w_xo': 'new_m', 'new_m_norm_ffn_g': 'new_m', 'new_m_w_gate': 'new_m', 'new_m_w_up': 'new_m', 'new_m_w_down': 'new_m', 'new_m_final_norm_g': 'new_m', 'new_v_norm_mix_g': 'new_v', 'new_v_w_in': 'new_v', 'new_v_q_norm_g': 'new_v', 'new_v_w_uq': 'new_v', 'new_v_kv_norm_g': 'new_v', 'new_v_w_ukv': 'new_v', 'new_v_ssm_lambda_re': 'new_v', 'new_v_ssm_lambda_im': 'new_v', 'new_v_ssm_log_dt': 'new_v', 'new_v_ssm_b_re': 'new_v', 'new_v_ssm_b_im': 'new_v', 'new_v_ssm_c_re': 'new_v', 'new_v_ssm_c_im': 'new_v', 'new_v_ssm_d': 'new_v', 'new_v_ssm_w_glu': 'new_v', 'new_v_ssm_b_glu': 'new_v', 'new_v_attn_out_g': 'new_v', 'new_v_ssm_out_g': 'new_v', 'new_v_w_out': 'new_v', 'new_v_norm_x_g': 'new_v', 'new_v_mem_norm_g': 'new_v', 'new_v_w_xq': 'new_v', 'new_v_w_xkv': 'new_v', 'new_v_w_xo': 'new_v', 'new_v_norm_ffn_g': 'new_v', 'new_v_w_gate': 'new_v', 'new_v_w_up': 'new_v', 'new_v_w_down': 'new_v', 'new_v_final_norm_g': 'new_v'}


def _forward(args):
    return _fwd_reference(*[args[k] for k in FWD_PARAMS])


def _output_shape():
    out = _jax.eval_shape(lambda: _forward(_fwd_setup_inputs(0)))
    return out.shape, out.dtype

N_MICROBATCH = 1
ADAM_LR = 0.001
ADAM_B1 = 0.9
ADAM_B2 = 0.999
ADAM_EPS = 1e-08
ADAM_WD = 0.01
ADAM_STEP = 10
PER_EXAMPLE_BATCH_AXIS = {'x': 0, 'mem': 0, 'positions': 0, 'loss_target': 0}
SHARED_INPUTS = []
_WEIGHT_DTYPES = {'norm_mix_g': _jnp.float32, 'w_in': _jnp.float32, 'q_norm_g': _jnp.float32, 'w_uq': _jnp.float32, 'kv_norm_g': _jnp.float32, 'w_ukv': _jnp.float32, 'ssm_lambda_re': _jnp.float32, 'ssm_lambda_im': _jnp.float32, 'ssm_log_dt': _jnp.float32, 'ssm_b_re': _jnp.float32, 'ssm_b_im': _jnp.float32, 'ssm_c_re': _jnp.float32, 'ssm_c_im': _jnp.float32, 'ssm_d': _jnp.float32, 'ssm_w_glu': _jnp.float32, 'ssm_b_glu': _jnp.float32, 'attn_out_g': _jnp.float32, 'ssm_out_g': _jnp.float32, 'w_out': _jnp.float32, 'norm_x_g': _jnp.float32, 'mem_norm_g': _jnp.float32, 'w_xq': _jnp.float32, 'w_xkv': _jnp.float32, 'w_xo': _jnp.float32, 'norm_ffn_g': _jnp.float32, 'w_gate': _jnp.float32, 'w_up': _jnp.float32, 'w_down': _jnp.float32, 'final_norm_g': _jnp.float32}
MOMENT_SCALE = {'norm_mix_g': 1.667397e-01, 'w_in': 1.759334e-01, 'q_norm_g': 1.409921e-01, 'w_uq': 8.221826e-02, 'kv_norm_g': 4.508722e-01, 'w_ukv': 1.444105e-01, 'ssm_lambda_re': 9.579459e-03, 'ssm_lambda_im': 9.008666e-03, 'ssm_log_dt': 8.288379e+00, 'ssm_b_re': 5.042748e-03, 'ssm_b_im': 5.205782e-03, 'ssm_c_re': 1.032396e-02, 'ssm_c_im': 1.017149e-02, 'ssm_d': 1.347646e-01, 'ssm_w_glu': 3.619467e-02, 'ssm_b_glu': 5.753978e-02, 'attn_out_g': 1.904937e-01, 'ssm_out_g': 1.239171e-01, 'w_out': 1.583474e-01, 'norm_x_g': 1.453784e-02, 'mem_norm_g': 2.855576e-02, 'w_xq': 1.331468e-02, 'w_xkv': 1.928316e-02, 'w_xo': 2.343586e-02, 'norm_ffn_g': 9.404174e-02, 'w_gate': 4.026987e-02, 'w_up': 3.931735e-02, 'w_down': 6.528663e-02, 'final_norm_g': 3.219848e+01}


def _to_microbatches(a, axis):
    t = _jnp.moveaxis(a, axis, 0)
    t = t.reshape((N_MICROBATCH, t.shape[0] // N_MICROBATCH) + t.shape[1:])
    return _jnp.moveaxis(t, 1, axis + 1)


def setup_inputs(seed: int = 0) -> dict:
    inp = _fwd_setup_inputs(seed)
    key = _jax.random.fold_in(_jax.random.key(seed), 7919)
    shape, _ = _output_shape()
    out = dict(inp)
    out["loss_target"] = _jax.random.normal(_jax.random.fold_in(key, 0), shape, _jnp.float32)
    for i, name in enumerate(TWIN_WEIGHTS):
        w = inp[name].astype(_jnp.float32)
        if MOMENT_SCALE is None:
            s = _jnp.sqrt(_jnp.mean(_jnp.square(w)) + 1e-30)
        else:
            s = MOMENT_SCALE[name]
        km, kv = _jax.random.split(_jax.random.fold_in(key, i + 1))
        out[name] = w
        out["m_" + name] = s * _jax.random.normal(km, w.shape, _jnp.float32)
        out["v_" + name] = (s * s) * _jax.random.uniform(kv, w.shape, _jnp.float32, 0.5, 1.5)
    if N_MICROBATCH > 1:
        for name, axis in PER_EXAMPLE_BATCH_AXIS.items():
            out[name] = _to_microbatches(out[name], axis)
    return {'x': out['x'], 'mem': out['mem'], 'positions': out['positions'], 'norm_mix_g': out['norm_mix_g'], 'w_in': out['w_in'], 'q_norm_g': out['q_norm_g'], 'w_uq': out['w_uq'], 'kv_norm_g': out['kv_norm_g'], 'w_ukv': out['w_ukv'], 'ssm_lambda_re': out['ssm_lambda_re'], 'ssm_lambda_im': out['ssm_lambda_im'], 'ssm_log_dt': out['ssm_log_dt'], 'ssm_b_re': out['ssm_b_re'], 'ssm_b_im': out['ssm_b_im'], 'ssm_c_re': out['ssm_c_re'], 'ssm_c_im': out['ssm_c_im'], 'ssm_d': out['ssm_d'], 'ssm_w_glu': out['ssm_w_glu'], 'ssm_b_glu': out['ssm_b_glu'], 'attn_out_g': out['attn_out_g'], 'ssm_out_g': out['ssm_out_g'], 'w_out': out['w_out'], 'norm_x_g': out['norm_x_g'], 'mem_norm_g': out['mem_norm_g'], 'w_xq': out['w_xq'], 'w_xkv': out['w_xkv'], 'w_xo': out['w_xo'], 'norm_ffn_g': out['norm_ffn_g'], 'w_gate': out['w_gate'], 'w_up': out['w_up'], 'w_down': out['w_down'], 'final_norm_g': out['final_norm_g'], 'loss_target': out['loss_target'], 'm_norm_mix_g': out['m_norm_mix_g'], 'm_w_in': out['m_w_in'], 'm_q_norm_g': out['m_q_norm_g'], 'm_w_uq': out['m_w_uq'], 'm_kv_norm_g': out['m_kv_norm_g'], 'm_w_ukv': out['m_w_ukv'], 'm_ssm_lambda_re': out['m_ssm_lambda_re'], 'm_ssm_lambda_im': out['m_ssm_lambda_im'], 'm_ssm_log_dt': out['m_ssm_log_dt'], 'm_ssm_b_re': out['m_ssm_b_re'], 'm_ssm_b_im': out['m_ssm_b_im'], 'm_ssm_c_re': out['m_ssm_c_re'], 'm_ssm_c_im': out['m_ssm_c_im'], 'm_ssm_d': out['m_ssm_d'], 'm_ssm_w_glu': out['m_ssm_w_glu'], 'm_ssm_b_glu': out['m_ssm_b_glu'], 'm_attn_out_g': out['m_attn_out_g'], 'm_ssm_out_g': out['m_ssm_out_g'], 'm_w_out': out['m_w_out'], 'm_norm_x_g': out['m_norm_x_g'], 'm_mem_norm_g': out['m_mem_norm_g'], 'm_w_xq': out['m_w_xq'], 'm_w_xkv': out['m_w_xkv'], 'm_w_xo': out['m_w_xo'], 'm_norm_ffn_g': out['m_norm_ffn_g'], 'm_w_gate': out['m_w_gate'], 'm_w_up': out['m_w_up'], 'm_w_down': out['m_w_down'], 'm_final_norm_g': out['m_final_norm_g'], 'v_norm_mix_g': out['v_norm_mix_g'], 'v_w_in': out['v_w_in'], 'v_q_norm_g': out['v_q_norm_g'], 'v_w_uq': out['v_w_uq'], 'v_kv_norm_g': out['v_kv_norm_g'], 'v_w_ukv': out['v_w_ukv'], 'v_ssm_lambda_re': out['v_ssm_lambda_re'], 'v_ssm_lambda_im': out['v_ssm_lambda_im'], 'v_ssm_log_dt': out['v_ssm_log_dt'], 'v_ssm_b_re': out['v_ssm_b_re'], 'v_ssm_b_im': out['v_ssm_b_im'], 'v_ssm_c_re': out['v_ssm_c_re'], 'v_ssm_c_im': out['v_ssm_c_im'], 'v_ssm_d': out['v_ssm_d'], 'v_ssm_w_glu': out['v_ssm_w_glu'], 'v_ssm_b_glu': out['v_ssm_b_glu'], 'v_attn_out_g': out['v_attn_out_g'], 'v_ssm_out_g': out['v_ssm_out_g'], 'v_w_out': out['v_w_out'], 'v_norm_x_g': out['v_norm_x_g'], 'v_mem_norm_g': out['v_mem_norm_g'], 'v_w_xq': out['v_w_xq'], 'v_w_xkv': out['v_w_xkv'], 'v_w_xo': out['v_w_xo'], 'v_norm_ffn_g': out['v_norm_ffn_g'], 'v_w_gate': out['v_w_gate'], 'v_w_up': out['v_w_up'], 'v_w_down': out['v_w_down'], 'v_final_norm_g': out['v_final_norm_g']}


def _loss(weights, diff, rest, loss_target):
    with _jax.named_scope("forward"):
        args = {**rest, TWIN_DIFF_INPUT: diff, **{k: w.astype(_WEIGHT_DTYPES[k]) for k, w in weights.items()}}
        y = _forward(args)
    with _jax.named_scope("loss_head"):
        err = _jnp.square(y.astype(_jnp.float32) - loss_target)
        return 0.5 * _jnp.sum(_jnp.mean(err, axis=-1)) if err.ndim else 0.5 * err


def _adamw(w, g, m, v):
    m = ADAM_B1 * m + (1.0 - ADAM_B1) * g
    v = ADAM_B2 * v + (1.0 - ADAM_B2) * _jnp.square(g)
    m_hat = m / (1.0 - ADAM_B1 ** ADAM_STEP)
    v_hat = v / (1.0 - ADAM_B2 ** ADAM_STEP)
    delta = -ADAM_LR * (m_hat / (_jnp.sqrt(v_hat) + ADAM_EPS) + ADAM_WD * w)
    return delta, m, v


def reference(x, mem, positions, norm_mix_g, w_in, q_norm_g, w_uq, kv_norm_g, w_ukv, ssm_lambda_re, ssm_lambda_im, ssm_log_dt, ssm_b_re, ssm_b_im, ssm_c_re, ssm_c_im, ssm_d, ssm_w_glu, ssm_b_glu, attn_out_g, ssm_out_g, w_out, norm_x_g, mem_norm_g, w_xq, w_xkv, w_xo, norm_ffn_g, w_gate, w_up, w_down, final_norm_g, loss_target, m_norm_mix_g, m_w_in, m_q_norm_g, m_w_uq, m_kv_norm_g, m_w_ukv, m_ssm_lambda_re, m_ssm_lambda_im, m_ssm_log_dt, m_ssm_b_re, m_ssm_b_im, m_ssm_c_re, m_ssm_c_im, m_ssm_d, m_ssm_w_glu, m_ssm_b_glu, m_attn_out_g, m_ssm_out_g, m_w_out, m_norm_x_g, m_mem_norm_g, m_w_xq, m_w_xkv, m_w_xo, m_norm_ffn_g, m_w_gate, m_w_up, m_w_down, m_final_norm_g, v_norm_mix_g, v_w_in, v_q_norm_g, v_w_uq, v_kv_norm_g, v_w_ukv, v_ssm_lambda_re, v_ssm_lambda_im, v_ssm_log_dt, v_ssm_b_re, v_ssm_b_im, v_ssm_c_re, v_ssm_c_im, v_ssm_d, v_ssm_w_glu, v_ssm_b_glu, v_attn_out_g, v_ssm_out_g, v_w_out, v_norm_x_g, v_mem_norm_g, v_w_xq, v_w_xkv, v_w_xo, v_norm_ffn_g, v_w_gate, v_w_up, v_w_down, v_final_norm_g):
    given = dict(x=x, mem=mem, positions=positions, norm_mix_g=norm_mix_g, w_in=w_in, q_norm_g=q_norm_g, w_uq=w_uq, kv_norm_g=kv_norm_g, w_ukv=w_ukv, ssm_lambda_re=ssm_lambda_re, ssm_lambda_im=ssm_lambda_im, ssm_log_dt=ssm_log_dt, ssm_b_re=ssm_b_re, ssm_b_im=ssm_b_im, ssm_c_re=ssm_c_re, ssm_c_im=ssm_c_im, ssm_d=ssm_d, ssm_w_glu=ssm_w_glu, ssm_b_glu=ssm_b_glu, attn_out_g=attn_out_g, ssm_out_g=ssm_out_g, w_out=w_out, norm_x_g=norm_x_g, mem_norm_g=mem_norm_g, w_xq=w_xq, w_xkv=w_xkv, w_xo=w_xo, norm_ffn_g=norm_ffn_g, w_gate=w_gate, w_up=w_up, w_down=w_down, final_norm_g=final_norm_g, loss_target=loss_target, m_norm_mix_g=m_norm_mix_g, m_w_in=m_w_in, m_q_norm_g=m_q_norm_g, m_w_uq=m_w_uq, m_kv_norm_g=m_kv_norm_g, m_w_ukv=m_w_ukv, m_ssm_lambda_re=m_ssm_lambda_re, m_ssm_lambda_im=m_ssm_lambda_im, m_ssm_log_dt=m_ssm_log_dt, m_ssm_b_re=m_ssm_b_re, m_ssm_b_im=m_ssm_b_im, m_ssm_c_re=m_ssm_c_re, m_ssm_c_im=m_ssm_c_im, m_ssm_d=m_ssm_d, m_ssm_w_glu=m_ssm_w_glu, m_ssm_b_glu=m_ssm_b_glu, m_attn_out_g=m_attn_out_g, m_ssm_out_g=m_ssm_out_g, m_w_out=m_w_out, m_norm_x_g=m_norm_x_g, m_mem_norm_g=m_mem_norm_g, m_w_xq=m_w_xq, m_w_xkv=m_w_xkv, m_w_xo=m_w_xo, m_norm_ffn_g=m_norm_ffn_g, m_w_gate=m_w_gate, m_w_up=m_w_up, m_w_down=m_w_down, m_final_norm_g=m_final_norm_g, v_norm_mix_g=v_norm_mix_g, v_w_in=v_w_in, v_q_norm_g=v_q_norm_g, v_w_uq=v_w_uq, v_kv_norm_g=v_kv_norm_g, v_w_ukv=v_w_ukv, v_ssm_lambda_re=v_ssm_lambda_re, v_ssm_lambda_im=v_ssm_lambda_im, v_ssm_log_dt=v_ssm_log_dt, v_ssm_b_re=v_ssm_b_re, v_ssm_b_im=v_ssm_b_im, v_ssm_c_re=v_ssm_c_re, v_ssm_c_im=v_ssm_c_im, v_ssm_d=v_ssm_d, v_ssm_w_glu=v_ssm_w_glu, v_ssm_b_glu=v_ssm_b_glu, v_attn_out_g=v_attn_out_g, v_ssm_out_g=v_ssm_out_g, v_w_out=v_w_out, v_norm_x_g=v_norm_x_g, v_mem_norm_g=v_mem_norm_g, v_w_xq=v_w_xq, v_w_xkv=v_w_xkv, v_w_xo=v_w_xo, v_norm_ffn_g=v_norm_ffn_g, v_w_gate=v_w_gate, v_w_up=v_w_up, v_w_down=v_w_down, v_final_norm_g=v_final_norm_g)
    weights = {n: given[n] for n in TWIN_WEIGHTS}
    shared = {n: given[n] for n in SHARED_INPUTS}
    per_example = {n: given[n] for n in ['x', 'mem', 'positions']}
    grad_fn = _jax.value_and_grad(_loss, argnums=(0, 1))

    def one_microbatch(ex, loss_target):
        ex = dict(ex)
        diff = ex.pop(TWIN_DIFF_INPUT)
        return grad_fn(weights, diff, {**shared, **ex}, loss_target)

    if N_MICROBATCH == 1:
        loss, (grad_w, grad_x) = one_microbatch(per_example, given["loss_target"])
    else:
        def body(carry, xs):
            loss_sum, grad_sum = carry
            l_k, (gw_k, gx_k) = one_microbatch(xs[0], xs[1])
            with _jax.named_scope("update"):
                return (loss_sum + l_k, _jax.tree.map(_jnp.add, grad_sum, gw_k)), gx_k

        init = (_jnp.zeros((), _jnp.float32), _jax.tree.map(_jnp.zeros_like, weights))
        (loss, grad_w), grad_x = _jax.lax.scan(body, init, (per_example, given["loss_target"]))
    with _jax.named_scope("update"):
        delta_w, new_m, new_v = {}, {}, {}
        for n in TWIN_WEIGHTS:
            delta_w[n], new_m[n], new_v[n] = _adamw(weights[n], grad_w[n], given["m_" + n], given["v_" + n])
    return (loss, grad_x, *[grad_w[n] for n in TWIN_WEIGHTS], *[delta_w[n] for n in TWIN_WEIGHTS],
            *[new_m[n] for n in TWIN_WEIGHTS], *[new_v[n] for n in TWIN_WEIGHTS])
```

```python
import functools

import jax
import jax.numpy as jnp
from jax import lax
from jax.experimental import pallas as pl
from jax.experimental.pallas import tpu as pltpu

F32 = jnp.float32
MXU = jnp.bfloat16
HI = lax.Precision.HIGHEST

D_MODEL = 1024
MLA_HEADS = 8
QK_NOPE = 64
QK_ROPE = 32
V_DIM = 64
Q_LORA = 256
KV_LORA = 128
SSM_WIDTH = 512
SSM_GROUPS = 32
SSM_GROUP = 16
SSM_STATE = 64
X_HEADS = 4
X_HEAD_DIM = 256
D_FF = 2816
FF_BLK = D_FF // 8
ROPE_THETA = 10000.0
EPS = 1e-6
DEPTH = 2
N_DEV = 8
LANES = 128
HEAD_W = 128
MLA_SCALE = (QK_NOPE + QK_ROPE) ** -0.5
X_SCALE = X_HEAD_DIM ** -0.5
ADAM_LR, ADAM_B1, ADAM_B2, ADAM_EPS, ADAM_WD, ADAM_STEP = 0.001, 0.9, 0.999, 1e-08, 0.01, 10
VMEM_LIMIT = 56 * 1024 * 1024
MESH = pl.DeviceIdType.MESH

SHARDED = ['w_in', 'w_uq', 'w_ukv', 'ssm_w_glu', 'w_out', 'w_xq', 'w_xkv', 'w_xo', 'w_gate', 'w_up', 'w_down']
REPL_L = ['norm_mix_g', 'q_norm_g', 'kv_norm_g', 'ssm_lambda_re', 'ssm_lambda_im', 'ssm_log_dt', 'ssm_b_re',
          'ssm_b_im', 'ssm_c_re', 'ssm_c_im', 'ssm_d', 'ssm_b_glu', 'attn_out_g', 'ssm_out_g', 'norm_x_g',
          'mem_norm_g', 'norm_ffn_g']
WEIGHTS = ['norm_mix_g', 'w_in', 'q_norm_g', 'w_uq', 'kv_norm_g', 'w_ukv', 'ssm_lambda_re', 'ssm_lambda_im',
           'ssm_log_dt', 'ssm_b_re', 'ssm_b_im', 'ssm_c_re', 'ssm_c_im', 'ssm_d', 'ssm_w_glu', 'ssm_b_glu',
           'attn_out_g', 'ssm_out_g', 'w_out', 'norm_x_g', 'mem_norm_g', 'w_xq', 'w_xkv', 'w_xo', 'norm_ffn_g',
           'w_gate', 'w_up', 'w_down', 'final_norm_g']


def _pcall(body, **kw):
    return pl.pallas_call(body, **kw)


def _mm(a, b):
    return jnp.dot(a.astype(MXU), b.astype(MXU), preferred_element_type=F32)


def _mm_nt(a, b):
    return lax.dot_general(a.astype(MXU), b.astype(MXU), (((1,), (1,)), ((), ())), preferred_element_type=F32)


def _mm_tn(a, b):
    return lax.dot_general(a.astype(MXU), b.astype(MXU), (((0,), (0,)), ((), ())), preferred_element_type=F32)


def _mm_hi(a, b):
    return jnp.dot(a.astype(F32), b.astype(F32), precision=HI, preferred_element_type=F32)


def _rms(x, g):
    r = lax.rsqrt(jnp.mean(x * x, axis=-1, keepdims=True) + EPS)
    return x * r * g, r


def _rms_bwd(x, g, r, dy):
    dyg = dy * g
    dx = r * dyg - x * (r * r * r) * jnp.mean(dyg * x, axis=-1, keepdims=True)
    return dx, jnp.sum(dy * x * r, axis=0, keepdims=True)


def _rope(x, cos, sin, p_ref):
    return x * cos + _mm_hi(x, p_ref[...]) * sin


def _rope_t(g, cos, sin, pt_ref):
    return g * cos + _mm_hi(g * sin, pt_ref[...])


def _softmax(s):
    m = jnp.max(s, axis=-1, keepdims=True)
    e = jnp.exp(s - m)
    return e / jnp.sum(e, axis=-1, keepdims=True)


def _lanes(x, j, w):
    return x[:, j * w:(j + 1) * w]


def _rows(name, fn, n, tm, ins, outs):
    def spec(shape, kind):
        nd = len(shape)
        if kind == 'f':
            return pl.BlockSpec(shape, lambda i, _nd=nd: (0,) * _nd, pipeline_mode=pl.Buffered(1))
        if kind == 'a':
            return pl.BlockSpec(shape, lambda i, _nd=nd: (0,) * _nd)
        ax = int(kind[1])
        blk = tuple(tm if d == ax else s for d, s in enumerate(shape))
        return pl.BlockSpec(blk, lambda i, _ax=ax, _nd=nd: tuple(i if d == _ax else 0 for d in range(_nd)))

    n_in = len(ins)

    def body(*refs):
        args = [r if k == 'f' else r[...] for r, (_, k) in zip(refs[:n_in], ins)]
        res = fn(*args)
        i = pl.program_id(0)
        for r, (_, dt, k), v in zip(refs[n_in:], outs, res):
            if k == 'a':
                _accumulate(r, v.astype(dt), i)
            else:
                r[...] = v.astype(dt)

    return _pcall(
        body, name=name, grid=(n // tm,),
        in_specs=[spec(a.shape, k) for a, k in ins],
        out_specs=[spec(s, k) for s, _, k in outs],
        out_shape=[jax.ShapeDtypeStruct(s, dt) for s, dt, _ in outs],
        compiler_params=pltpu.CompilerParams(dimension_semantics=("arbitrary",), vmem_limit_bytes=VMEM_LIMIT),
    )(*[a for a, _ in ins])


def _accumulate(ref, v, i):
    @pl.when(i == 0)
    def _():
        ref[...] = v

    @pl.when(i != 0)
    def _():
        ref[...] += v


def _mm_tn_call(name, a, b):
    ba, s, k = a.shape
    bb, _, n = b.shape
    nb = max(ba, bb)
    ts = min(512, s)

    def body(a_ref, b_ref, o_ref):
        _accumulate(o_ref, _mm_tn(a_ref[...], b_ref[...]), pl.program_id(1))

    return _pcall(
        body, name=name, grid=(nb, s // ts),
        in_specs=[pl.BlockSpec((None, ts, k), (lambda i, j: (i, j, 0)) if ba > 1 else (lambda i, j: (0, j, 0))),
                  pl.BlockSpec((None, ts, n), (lambda i, j: (i, j, 0)) if bb > 1 else (lambda i, j: (0, j, 0)))],
        out_specs=pl.BlockSpec((None, k, n), lambda i, j: (i, 0, 0)),
        out_shape=jax.ShapeDtypeStruct((nb, k, n), F32),
        compiler_params=pltpu.CompilerParams(dimension_semantics=("arbitrary", "arbitrary"),
                                             vmem_limit_bytes=VMEM_LIMIT),
    )(a, b)


def _flash_fwd(q, k, v):
    nh, s, w = q.shape
    t = min(256, s)

    def body(q_ref, k_ref, v_ref, o_ref, lse_ref):
        qi = pl.program_id(1)
        rows = qi * t + lax.broadcasted_iota(jnp.int32, (t, t), 0)
        o_acc = jnp.zeros((t, w), F32)
        for hh in range(2):
            qh = q_ref[hh]

            def step(j, carry, hh=hh, qh=qh):
                m, l, acc = carry
                sl = pl.ds(pl.multiple_of(j * t, t), t)
                sc = _mm_nt(qh, k_ref[hh, sl, :])
                cols = j * t + lax.broadcasted_iota(jnp.int32, (t, t), 1)
                sc = jnp.where(cols <= rows, sc, -1e30)
                m_new = jnp.maximum(m, jnp.max(sc, axis=1, keepdims=True))
                p = jnp.exp(sc - m_new)
                alpha = jnp.exp(m - m_new)
                l = alpha * l + jnp.sum(p, axis=1, keepdims=True)
                acc = alpha * acc + _mm(p, v_ref[hh, sl, :])
                return m_new, l, acc

            m, l, acc = lax.fori_loop(0, qi + 1, step, (jnp.full((t, 1), -1e30, F32), jnp.zeros((t, 1), F32),
                                                        jnp.zeros((t, w), F32)))
            o_acc = o_acc + acc / l
            lse_ref[hh] = jnp.broadcast_to(m + jnp.log(l), (t, w))
        o_ref[...] = o_acc

    return _pcall(
        body, name="mla_flash_fwd", grid=(nh // 2, s // t),
        in_specs=[pl.BlockSpec((2, t, w), lambda p, i: (p, i, 0)),
                  pl.BlockSpec((2, s, w), lambda p, i: (p, 0, 0)),
                  pl.BlockSpec((2, s, w), lambda p, i: (p, 0, 0))],
        out_specs=[pl.BlockSpec((t, w), lambda p, i: (i, p)),
                   pl.BlockSpec((2, t, w), lambda p, i: (p, i, 0))],
        out_shape=[jax.ShapeDtypeStruct((s, (nh // 2) * w), F32), jax.ShapeDtypeStruct((nh, s, w), F32)],
        compiler_params=pltpu.CompilerParams(dimension_semantics=("arbitrary", "arbitrary"),
                                             vmem_limit_bytes=VMEM_LIMIT),
    )(q, k, v)


def _flash_bwd(q, k, v, o, lse, do):
    nh, s, w = q.shape
    t = min(256, s)
    nq = s // t

    def body(q_ref, k_ref, v_ref, o_ref, lse_ref, do_ref, dq_ref, dk_ref, dv_ref):
        j = pl.program_id(1)

        @pl.when(j == 0)
        def _():
            dq_ref[...] = jnp.zeros(dq_ref.shape, F32)

        cols = j * t + lax.broadcasted_iota(jnp.int32, (t, t), 1)
        lane = lax.broadcasted_iota(jnp.int32, (t, w), 1)
        for hh in range(2):
            kh = k_ref[hh]
            vh = v_ref[hh]
            head = jnp.logical_and(lane >= hh * V_DIM, lane < (hh + 1) * V_DIM)

            def step(i, carry, hh=hh, kh=kh, vh=vh, head=head):
                dk, dv = carry
                sl = pl.ds(pl.multiple_of(i * t, t), t)
                qh = q_ref[hh, sl, :]
                dout = jnp.where(head, do_ref[sl, :], 0.0)
                rows = i * t + lax.broadcasted_iota(jnp.int32, (t, t), 0)
                sc = jnp.where(cols <= rows, _mm_nt(qh, kh), -1e30)
                p = jnp.exp(sc - lse_ref[hh, sl, 0:1])
                dv = dv + _mm_tn(p, dout)
                dp = _mm_nt(dout, vh)
                delta = jnp.sum(dout * o_ref[sl, :], axis=1, keepdims=True)
                ds = p * (dp - delta)
                dk = dk + _mm_tn(ds, qh)
                dq_ref[hh, sl, :] += _mm(ds, kh)
                return dk, dv

            dk, dv = lax.fori_loop(j, nq, step, (jnp.zeros((t, w), F32), jnp.zeros((t, w), F32)))
            dk_ref[hh] = dk
            dv_ref[hh] = jnp.where(head, dv, 0.0)

    return _pcall(
        body, name="mla_flash_bwd", grid=(nh // 2, nq),
        in_specs=[pl.BlockSpec((2, s, w), lambda p, j: (p, 0, 0)),
                  pl.BlockSpec((2, t, w), lambda p, j: (p, j, 0)),
                  pl.BlockSpec((2, t, w), lambda p, j: (p, j, 0)),
                  pl.BlockSpec((s, w), lambda p, j: (0, p)),
                  pl.BlockSpec((2, s, w), lambda p, j: (p, 0, 0)),
                  pl.BlockSpec((s, w), lambda p, j: (0, p))],
        out_specs=[pl.BlockSpec((2, s, w), lambda p, j: (p, 0, 0)),
                   pl.BlockSpec((2, t, w), lambda p, j: (p, j, 0)),
                   pl.BlockSpec((2, t, w), lambda p, j: (p, j, 0))],
        out_shape=[jax.ShapeDtypeStruct((nh, s, w), F32)] * 3,
        compiler_params=pltpu.CompilerParams(dimension_semantics=("arbitrary", "arbitrary"),
                                             vmem_limit_bytes=VMEM_LIMIT),
    )(q, k, v, o, lse, do)


def _scan(b_re, b_im, a_re, a_im, reverse):
    nb, s, w = b_re.shape
    ch = s // 8
    assert ch & (ch - 1) == 0
    grp = 2

    def cmul(ar, ai, xr, xi):
        return ar * xr - ai * xi, ar * xi + ai * xr

    def body(br_ref, bi_ref, ar_ref, ai_ref, xr_ref, xi_ref):
        sub = lax.broadcasted_iota(jnp.int32, (8, w), 0)

        def shift(x, k):
            if reverse:
                return jnp.where(sub < 8 - k, pltpu.roll(x, 8 - k, 0), 0.0)
            return jnp.where(sub >= k, pltpu.roll(x, k, 0), 0.0)

        ar = [jnp.broadcast_to(ar_ref[g], (8, w)) for g in range(grp)]
        ai = [jnp.broadcast_to(ai_ref[g], (8, w)) for g in range(grp)]

        def tsl(i):
            return pl.ds((ch - 1 - i) if reverse else i, 8, stride=ch)

        def local(i, carry):
            out = []
            for g in range(grp):
                xr, xi = carry[2 * g], carry[2 * g + 1]
                pr, pi = cmul(ar[g], ai[g], xr, xi)
                nr = pr + br_ref[g, tsl(i), :]
                ni = pi + bi_ref[g, tsl(i), :]
                xr_ref[g, tsl(i), :] = nr
                xi_ref[g, tsl(i), :] = ni
                out += [nr, ni]
            return tuple(out)

        fin = lax.fori_loop(0, ch, local, (jnp.zeros((8, w), F32),) * (2 * grp))

        carry_in = []
        for g in range(grp):
            pr, pi = ar[g], ai[g]
            for _ in range(ch.bit_length() - 1):
                pr, pi = cmul(pr, pi, pr, pi)
            fr, fi = fin[2 * g], fin[2 * g + 1]
            for kk in (1, 2, 4):
                sr, si = cmul(pr, pi, shift(fr, kk), shift(fi, kk))
                fr, fi = fr + sr, fi + si
                pr, pi = cmul(pr, pi, pr, pi)
            carry_in += [shift(fr, 1), shift(fi, 1)]

        def fix(i, pw):
            out = []
            for g in range(grp):
                pr, pi = pw[2 * g], pw[2 * g + 1]
                cr, ci = cmul(pr, pi, carry_in[2 * g], carry_in[2 * g + 1])
                xr_ref[g, tsl(i), :] = xr_ref[g, tsl(i), :] + cr
                xi_ref[g, tsl(i), :] = xi_ref[g, tsl(i), :] + ci
                nr, ni = cmul(pr, pi, ar[g], ai[g])
                out += [nr, ni]
            return tuple(out)

        lax.fori_loop(0, ch, fix, tuple(x for g in range(grp) for x in (ar[g], ai[g])))

    blk = pl.BlockSpec((grp, s, w), lambda i: (i, 0, 0))
    ablk = pl.BlockSpec((grp, 1, w), lambda i: (i, 0, 0))
    return _pcall(
        body, name="s5_scan_rev" if reverse else "s5_scan", grid=(nb // grp,),
        in_specs=[blk, blk, ablk, ablk], out_specs=[blk, blk],
        out_shape=[jax.ShapeDtypeStruct((nb, s, w), F32)] * 2,
        compiler_params=pltpu.CompilerParams(dimension_semantics=("arbitrary",), vmem_limit_bytes=VMEM_LIMIT),
    )(b_re, b_im, a_re, a_im)


def _all_gather(name, arrs):
    n = len(arrs)

    def body(*refs):
        ins, outs = refs[:n], refs[n:2 * n]
        send_sems, recv_sems, local_sems = refs[2 * n:]
        x, y, c = lax.axis_index("x"), lax.axis_index("y"), lax.axis_index("c")
        me, sibling = (x, y, c), (x, y, 1 - c)
        chips = [(1 - x, y), (x, 1 - y), (1 - x, 1 - y)]

        def copy(a, k, block, to, src=None):
            dst = outs[a].at[4 * block[0] + 2 * block[1] + block[2]]
            return pltpu.make_async_remote_copy(
                src_ref=dst if src is None else src, dst_ref=dst,
                send_sem=send_sems.at[a, k], recv_sem=recv_sems.at[a, k], device_id=to, device_id_type=MESH)

        mine = [pltpu.make_async_copy(ins[a], outs[a].at[4 * x + 2 * y + c], local_sems.at[a]) for a in range(n)]
        for cp in mine:
            cp.start()
        first = []
        for a in range(n):
            first.append(copy(a, 0, me, sibling, src=ins[a]))
            first += [copy(a, 1 + j, me, (*chip, c), src=ins[a]) for j, chip in enumerate(chips)]
        for cp in first:
            cp.start()
        passed = []
        for j, chip in enumerate(chips):
            for a in range(n):
                copy(a, 1 + j, (*chip, c), me).wait_recv()
                fwd = copy(a, 4 + j, (*chip, c), sibling)
                fwd.start()
                passed.append(fwd)
        for a in range(n):
            copy(a, 0, sibling, me).wait_recv()
            for j, chip in enumerate(chips):
                copy(a, 4 + j, (*chip, 1 - c), me).wait_recv()
        for cp in first + passed:
            cp.wait_send()
        for cp in mine:
            cp.wait()

    any_spec = pl.BlockSpec(memory_space=pl.ANY)
    return _pcall(
        body, name=name,
        in_specs=[any_spec] * n, out_specs=[any_spec] * n,
        out_shape=[jax.ShapeDtypeStruct((N_DEV,) + a.shape, a.dtype) for a in arrs],
        scratch_shapes=[pltpu.SemaphoreType.DMA((n, 7)), pltpu.SemaphoreType.DMA((n, 7)),
                        pltpu.SemaphoreType.DMA((n,))],
    )(*arrs)


def _all_to_all(name, g):
    def body(g_ref, r_ref, send_sems, recv_sems, local_sem):
        x, y, c = lax.axis_index("x"), lax.axis_index("y"), lax.axis_index("c")
        me = 4 * x + 2 * y + c
        own = pltpu.make_async_copy(g_ref.at[me], r_ref.at[me], local_sem)
        own.start()
        copies = []
        for k in range(1, N_DEV):
            fx, fy, fc = (k >> 2) & 1, (k >> 1) & 1, k & 1
            px, py, pc = x ^ fx, y ^ fy, c ^ fc
            cp = pltpu.make_async_remote_copy(
                src_ref=g_ref.at[4 * px + 2 * py + pc], dst_ref=r_ref.at[me],
                send_sem=send_sems.at[k - 1], recv_sem=recv_sems.at[k - 1],
                device_id=(px, py, pc), device_id_type=MESH)
            cp.start()
            copies.append((cp, 4 * px + 2 * py + pc))
        for k, (cp, peer) in enumerate(copies):
            pltpu.make_async_remote_copy(
                src_ref=g_ref.at[me], dst_ref=r_ref.at[peer], send_sem=send_sems.at[k], recv_sem=recv_sems.at[k],
                device_id=(x, y, c), device_id_type=MESH).wait_recv()
        for cp, _ in copies:
            cp.wait_send()
        own.wait()

    any_spec = pl.BlockSpec(memory_space=pl.ANY)
    return _pcall(
        body, name=name, in_specs=[any_spec], out_specs=any_spec,
        out_shape=jax.ShapeDtypeStruct(g.shape, g.dtype),
        scratch_shapes=[pltpu.SemaphoreType.DMA((N_DEV - 1,)), pltpu.SemaphoreType.DMA((N_DEV - 1,)),
                        pltpu.SemaphoreType.DMA(())],
    )(g)


def _adamw(name, parts, w, m, v, tr):
    r = w.shape[0]

    def fn(p, w_, m_, v_):
        g = p[0].astype(F32)
        for j in range(1, N_DEV):
            g = g + p[j].astype(F32)
        m_new = ADAM_B1 * m_ + (1.0 - ADAM_B1) * g
        v_new = ADAM_B2 * v_ + (1.0 - ADAM_B2) * (g * g)
        m_hat = m_new / (1.0 - ADAM_B1 ** ADAM_STEP)
        v_hat = v_new / (1.0 - ADAM_B2 ** ADAM_STEP)
        delta = -ADAM_LR * (m_hat / (jnp.sqrt(v_hat) + ADAM_EPS) + ADAM_WD * w_)
        return g, delta, m_new, v_new

    return _rows(name, fn, r, tr, [(parts, 'r1'), (w, 'r0'), (m, 'r0'), (v, 'r0')],
                 [((r, LANES), F32, 'r0')] * 4)


def _pack(arrs, rows_mult):
    lead = arrs[0].shape[0]
    flat = jnp.concatenate([a.reshape(lead, -1) for a in arrs], axis=1)
    per = rows_mult * LANES
    pad = (-flat.shape[1]) % per
    flat = jnp.pad(flat, ((0, 0), (0, pad)))
    return flat.reshape(lead, -1, LANES)


def _unpack(packed, shapes):
    flat = packed.reshape(-1)
    out, off = [], 0
    for shp in shapes:
        size = 1
        for d in shp:
            size *= d
        out.append(flat[off:off + size].reshape(shp))
        off += size
    return out


def _s5_params(lam_re, lam_im, log_dt, b_re, b_im, c_re, c_im):
    dt = jnp.exp(log_dt)[:, None]
    e = jnp.exp(lam_re * dt)
    ang = lam_im * dt
    a_re, a_im = e * jnp.cos(ang), e * jnp.sin(ang)
    nr, ni = a_re - 1.0, a_im
    den = lam_re * lam_re + lam_im * lam_im
    cr = ((nr * lam_re + ni * lam_im) / den)[..., None]
    ci = ((ni * lam_re - nr * lam_im) / den)[..., None]
    bb_re = cr * b_re - ci * b_im
    bb_im = cr * b_im + ci * b_re
    eye = jnp.eye(8, dtype=F32)[None, :, None, :, None]

    def bblk(bb):
        t = jnp.transpose(bb.reshape(4, 8, SSM_STATE, SSM_GROUP), (0, 3, 1, 2))
        return (eye * t[:, None]).reshape(4, 8 * SSM_GROUP, 8 * SSM_STATE)

    def cblk(cc):
        t = jnp.transpose(cc.reshape(4, 8, SSM_GROUP, SSM_STATE), (0, 3, 1, 2))
        return (eye * t[:, None]).reshape(4, 8 * SSM_STATE, 8 * SSM_GROUP)

    nb = SSM_GROUPS * SSM_STATE // LANES
    return (a_re.reshape(nb, 1, LANES), a_im.reshape(nb, 1, LANES), bblk(bb_re), bblk(bb_im),
            cblk(c_re), -cblk(c_im))


def _cat_blocks(x3, j):
    return jnp.concatenate([x3[4 * j + k] for k in range(4)], axis=-1)


def _layer_fwd(h, memx, tabs, wl, pl_):
    s = h.shape[0]
    tm = min(256, s)
    cos, sin, pmat, pmat_t = tabs
    sv = {}

    def f_mix_in(h_, g, w):
        xn, _ = _rms(h_, g[...])
        return (_mm(xn, w[...]),)
    proj, = _rows("mix_in", f_mix_in, s, tm, [(h, 'r0'), (pl_['norm_mix_g'], 'f'), (wl['w_in'], 'f')],
                  [((s, D_MODEL), F32, 'r0')])

    def f_qkv(pr, cos_, sin_, gq, gkv, wq, wk, wv, pm):
        cqn = _rms(pr[:, 0:Q_LORA], gq[...])[0].astype(MXU)
        kvn = _rms(pr[:, Q_LORA:Q_LORA + KV_LORA], gkv[...])[0].astype(MXU)
        krr = _rope(pr[:, 384:512], cos_, sin_, pm)
        qs, ks, vs = [], [], []
        for hd in range(MLA_HEADS):
            qs.append(_rope(_mm(cqn, wq[hd]), cos_, sin_, pm) * MLA_SCALE)
            ks.append(_mm(kvn, wk[hd]) + krr)
            vs.append(_mm(kvn, wv[hd]))
        return jnp.stack(qs), jnp.stack(ks), jnp.stack(vs)
    hshape = (MLA_HEADS, s, HEAD_W)
    q, k, v = _rows("mla_qkv", f_qkv, s, tm,
                    [(proj, 'r0'), (cos, 'r0'), (sin, 'r0'), (pl_['q_norm_g'], 'f'), (pl_['kv_norm_g'], 'f'),
                     (wl['w_uq'], 'f'), (wl['w_k'], 'f'), (wl['w_v'], 'f'), (pmat, 'f')],
                    [(hshape, MXU, 'r1')] * 3)
    a_out, lse = _flash_fwd(q, k, v)

    def f_s5_in(pr, bre, bim):
        u = pr[:, 512:1024]
        outs_r, outs_i = [], []
        for j in range(4):
            uj = _lanes(u, j, LANES)
            rr, ri = _mm_hi(uj, bre[j]), _mm_hi(uj, bim[j])
            outs_r += [_lanes(rr, kk, LANES) for kk in range(4)]
            outs_i += [_lanes(ri, kk, LANES) for kk in range(4)]
        return jnp.stack(outs_r), jnp.stack(outs_i)
    xshape = (16, s, LANES)
    bu_re, bu_im = _rows("s5_in", f_s5_in, s, tm, [(proj, 'r0'), (pl_['b_re'], 'f'), (pl_['b_im'], 'f')],
                         [(xshape, F32, 'r1')] * 2)
    x_re, x_im = _scan(bu_re, bu_im, pl_['a_re'], pl_['a_im'], False)

    def f_s5_out(xr, xi, pr, cre, cim, d, wglu, bglu):
        u = pr[:, 512:1024]
        y = jnp.concatenate([_mm_hi(_cat_blocks(xr, j), cre[j]) + _mm_hi(_cat_blocks(xi, j), cim[j])
                             for j in range(4)], axis=-1) + d[...] * u
        z = _mm(jax.nn.gelu(y), wglu[...]) + bglu[...]
        return y, y * jax.nn.sigmoid(z)
    y_ssm, s_out = _rows("s5_out", f_s5_out, s, tm,
                         [(x_re, 'r1'), (x_im, 'r1'), (proj, 'r0'), (pl_['c_re'], 'f'), (pl_['c_im'], 'f'),
                          (pl_['ssm_d'], 'f'), (wl['ssm_w_glu'], 'f'), (pl_['ssm_b_glu'], 'f')],
                         [((s, SSM_WIDTH), F32, 'r0')] * 2)

    def f_mix_out(h_, a, so, ga, gs, w):
        an = _rms(a, ga[...])[0]
        sn = _rms(so, gs[...])[0]
        return (h_ + _mm(jnp.concatenate([an, sn], axis=-1), w[...]),)
    h1, = _rows("mix_out", f_mix_out, s, tm,
                [(h, 'r0'), (a_out, 'r0'), (s_out, 'r0'), (pl_['attn_out_g'], 'f'), (pl_['ssm_out_g'], 'f'),
                 (wl['w_out'], 'f')], [((s, D_MODEL), F32, 'r0')])

    m_len = memx.shape[0]

    def f_memkv(mm_, g, w):
        mn = _rms(mm_, g[...])[0].astype(MXU)
        return (jnp.stack([_mm(mn, w[d]) for d in range(N_DEV)]),)
    kvm, = _rows("mem_kv", f_memkv, m_len, m_len, [(memx, 'r0'), (pl_['mem_norm_g'], 'f'), (wl['w_xkv'], 'f')],
                 [((N_DEV, m_len, X_HEAD_DIM), MXU, 'r1')])

    def f_xattn(h_, g, wq, kv_, wo):
        hn = _rms(h_, g[...])[0].astype(MXU)
        out = jnp.zeros(h_.shape, F32)
        for hd in range(X_HEADS):
            cs = pl.ds(hd * X_HEAD_DIM, X_HEAD_DIM)
            qh = _mm(hn, wq[:, cs])
            p = _softmax(_mm_nt(qh, kv_[hd]) * X_SCALE)
            out = out + _mm(_mm(p, kv_[X_HEADS + hd]), wo[cs, :])
        return (h_ + out,)
    h2, = _rows("xattn", f_xattn, s, tm, [(h1, 'r0'), (pl_['norm_x_g'], 'f'), (wl['w_xq'], 'f'), (kvm, 'f'),
                                          (wl['w_xo'], 'f')], [((s, D_MODEL), F32, 'r0')])

    def f_ffn(h_, g, wg, wu, wd):
        hn = _rms(h_, g[...])[0].astype(MXU)
        y = jnp.zeros(h_.shape, F32)
        for d in range(N_DEV):
            gate = _mm(hn, wg[d])
            y = y + _mm(gate * jax.nn.sigmoid(gate) * _mm(hn, wu[d]), wd[d])
        return (h_ + y,)
    h3, = _rows("ffn", f_ffn, s, tm, [(h2, 'r0'), (pl_['norm_ffn_g'], 'f'), (wl['w_gate'], 'f'),
                                      (wl['w_up'], 'f'), (wl['w_down'], 'f')], [((s, D_MODEL), F32, 'r0')])
    sv.update(h=h, proj=proj, q=q, k=k, v=v, a_out=a_out, lse=lse, x_re=x_re, x_im=x_im, y_ssm=y_ssm,
              s_out=s_out, h1=h1, kvm=kvm, h2=h2)
    return h3, sv


def _layer_bwd(dh3, sv, memx, tabs, wl, pl_):
    s = dh3.shape[0]
    tm = min(256, s)
    cos, sin, pmat, pmat_t = tabs
    gr = {}
    act_shape = (N_DEV, s, FF_BLK)

    def f_ffn_bwd(h_, dy, g, wg, wu, wd):
        hn, r = _rms(h_, g[...])
        hb = hn.astype(MXU)
        dyb = dy.astype(MXU)
        dhn = jnp.zeros(h_.shape, F32)
        acts, dgs, dus = [], [], []
        for d in range(N_DEV):
            gate, up = _mm(hb, wg[d]), _mm(hb, wu[d])
            sg = jax.nn.sigmoid(gate)
            si = gate * sg
            dact = _mm_nt(dyb, wd[d])
            dgate = (dact * up * (sg * (1.0 + gate * (1.0 - sg)))).astype(MXU)
            dup = (dact * si).astype(MXU)
            dhn = dhn + _mm_nt(dgate, wg[d]) + _mm_nt(dup, wu[d])
            acts.append((si * up).astype(MXU))
            dgs.append(dgate)
            dus.append(dup)
        dh, dg = _rms_bwd(h_, g[...], r, dhn)
        return dy + dh, hb, jnp.stack(acts), jnp.stack(dgs), jnp.stack(dus), dg
    dh2, hn_f, act, dgate, dup, gr['norm_ffn_g'] = _rows(
        "ffn_bwd", f_ffn_bwd, s, tm,
        [(sv['h2'], 'r0'), (dh3, 'r0'), (pl_['norm_ffn_g'], 'f'), (wl['w_gate'], 'f'), (wl['w_up'], 'f'),
         (wl['w_down'], 'f')],
        [((s, D_MODEL), F32, 'r0'), ((s, D_MODEL), MXU, 'r0'), (act_shape, MXU, 'r1'), (act_shape, MXU, 'r1'),
         (act_shape, MXU, 'r1'), ((1, D_MODEL), F32, 'a')])
    gr['w_gate'] = _mm_tn_call("dw_gate", hn_f[None], dgate)
    gr['w_up'] = _mm_tn_call("dw_up", hn_f[None], dup)
    gr['w_down'] = _mm_tn_call("dw_down", act, dh3[None])

    m_len = memx.shape[0]

    def f_xattn_bwd(h_, dy, g, wq, kv_, wo):
        hn, r = _rms(h_, g[...])
        hb = hn.astype(MXU)
        dyb = dy.astype(MXU)
        dhn = jnp.zeros(h_.shape, F32)
        dqs, ohs, dks, dvs = [], [], [], []
        for hd in range(X_HEADS):
            cs = pl.ds(hd * X_HEAD_DIM, X_HEAD_DIM)
            kh, vh = kv_[hd], kv_[X_HEADS + hd]
            qh = _mm(hb, wq[:, cs])
            p = _softmax(_mm_nt(qh, kh) * X_SCALE)
            ohs.append(_mm(p, vh).astype(MXU))
            do = _mm_nt(dyb, wo[cs, :])
            dvs.append(_mm_tn(p, do))
            dp = _mm_nt(do, vh)
            ds = p * (dp - jnp.sum(dp * p, axis=-1, keepdims=True)) * X_SCALE
            dq = _mm(ds, kh).astype(MXU)
            dks.append(_mm_tn(ds, qh))
            dhn = dhn + _mm_nt(dq, wq[:, cs])
            dqs.append(dq)
        dh, dg = _rms_bwd(h_, g[...], r, dhn)
        return (dy + dh, hb, jnp.concatenate(dqs, axis=-1), jnp.concatenate(ohs, axis=-1),
                jnp.stack(dks + dvs), dg)
    dh1, hn_x, dq_x, oh_x, dkvm, gr['norm_x_g'] = _rows(
        "xattn_bwd", f_xattn_bwd, s, tm,
        [(sv['h1'], 'r0'), (dh2, 'r0'), (pl_['norm_x_g'], 'f'), (wl['w_xq'], 'f'), (sv['kvm'], 'f'),
         (wl['w_xo'], 'f')],
        [((s, D_MODEL), F32, 'r0'), ((s, D_MODEL), MXU, 'r0'), ((s, D_MODEL), MXU, 'r0'),
         ((s, D_MODEL), MXU, 'r0'), ((N_DEV, m_len, X_HEAD_DIM), F32, 'a'), ((1, D_MODEL), F32, 'a')])
    gr['w_xq'] = _mm_tn_call("dw_xq", hn_x[None], dq_x[None])[0]
    gr['w_xo'] = _mm_tn_call("dw_xo", oh_x[None], dh2[None])[0]

    def f_memkv_bwd(mm_, dkv, g, w):
        mn, r = _rms(mm_, g[...])
        mb = mn.astype(MXU)
        dmn = jnp.zeros(mm_.shape, F32)
        dws = []
        for d in range(N_DEV):
            dmn = dmn + _mm_nt(dkv[d], w[d])
            dws.append(_mm_tn(mb, dkv[d]))
        _, dg = _rms_bwd(mm_, g[...], r, dmn)
        return jnp.stack(dws), dg
    gr['w_xkv'], gr['mem_norm_g'] = _rows(
        "mem_kv_bwd", f_memkv_bwd, m_len, m_len,
        [(memx, 'r0'), (dkvm, 'r1'), (pl_['mem_norm_g'], 'f'), (wl['w_xkv'], 'f')],
        [((N_DEV, D_MODEL, X_HEAD_DIM), F32, 'a'), ((1, D_MODEL), F32, 'a')])

    def f_mix_out_bwd(a, so, dy, ga, gs, w):
        dmix = _mm_nt(dy, w[...])
        an, ra = _rms(a, ga[...])
        sn, rs = _rms(so, gs[...])
        da, dga = _rms_bwd(a, ga[...], ra, dmix[:, 0:512])
        dso, dgs = _rms_bwd(so, gs[...], rs, dmix[:, 512:1024])
        return da, dso, jnp.concatenate([an, sn], axis=-1), dga, dgs
    da_out, ds_out, mixed, gr['attn_out_g'], gr['ssm_out_g'] = _rows(
        "mix_out_bwd", f_mix_out_bwd, s, tm,
        [(sv['a_out'], 'r0'), (sv['s_out'], 'r0'), (dh1, 'r0'), (pl_['attn_out_g'], 'f'), (pl_['ssm_out_g'], 'f'),
         (wl['w_out'], 'f')],
        [((s, 512), F32, 'r0'), ((s, 512), F32, 'r0'), ((s, D_MODEL), MXU, 'r0'), ((1, 512), F32, 'a'),
         ((1, 512), F32, 'a')])
    gr['w_out'] = _mm_tn_call("dw_out", mixed[None], dh1[None])[0]

    def f_s5_out_bwd(xr, xi, pr, y, ds, cre, cim, d, wglu, bglu):
        u = pr[:, 512:1024]
        g, gelu_vjp = jax.vjp(jax.nn.gelu, y)
        sig = jax.nn.sigmoid(_mm(g, wglu[...]) + bglu[...])
        dz = ds * y * sig * (1.0 - sig)
        dy = ds * sig + gelu_vjp(_mm_nt(dz, wglu[...]))[0]
        dxr, dxi, dcr, dci = [], [], [], []
        for j in range(4):
            dyj = _lanes(dy, j, LANES)
            tr_, ti_ = _mm_nt(dyj, cre[j]), _mm_nt(dyj, cim[j])
            dxr += [_lanes(tr_, kk, LANES) for kk in range(4)]
            dxi += [_lanes(ti_, kk, LANES) for kk in range(4)]
            dcr.append(_mm_tn(_cat_blocks(xr, j), dyj))
            dci.append(_mm_tn(_cat_blocks(xi, j), dyj))
        return (jnp.stack(dxr), jnp.stack(dxi), dy * d[...], jnp.stack(dcr), jnp.stack(dci),
                jnp.sum(dy * u, axis=0, keepdims=True), _mm_tn(g, dz), jnp.sum(dz, axis=0, keepdims=True))
    xshape = (16, s, LANES)
    dx_re, dx_im, du_dir, gr['c_re'], gr['c_im'], gr['ssm_d'], gr['ssm_w_glu'], gr['ssm_b_glu'] = _rows(
        "s5_out_bwd", f_s5_out_bwd, s, tm,
        [(sv['x_re'], 'r1'), (sv['x_im'], 'r1'), (sv['proj'], 'r0'), (sv['y_ssm'], 'r0'), (ds_out, 'r0'),
         (pl_['c_re'], 'f'), (pl_['c_im'], 'f'), (pl_['ssm_d'], 'f'), (wl['ssm_w_glu'], 'f'),
         (pl_['ssm_b_glu'], 'f')],
        [(xshape, F32, 'r1'), (xshape, F32, 'r1'), ((s, 512), F32, 'r0'), ((4, 512, LANES), F32, 'a'),
         ((4, 512, LANES), F32, 'a'), ((1, 512), F32, 'a'), ((512, 512), F32, 'a'), ((1, 512), F32, 'a')])
    g_re, g_im = _scan(dx_re, dx_im, pl_['a_re'], -pl_['a_im'], True)
    xp_re = jnp.pad(sv['x_re'][:, :-1], ((0, 0), (1, 0), (0, 0)))
    xp_im = jnp.pad(sv['x_im'][:, :-1], ((0, 0), (1, 0), (0, 0)))

    def f_s5_in_bwd(gre, gim, xpr, xpi, pr, dud, bre, bim):
        u = pr[:, 512:1024]
        dus, dbr, dbi = [], [], []
        for j in range(4):
            gj_r, gj_i, uj = _cat_blocks(gre, j), _cat_blocks(gim, j), _lanes(u, j, LANES)
            dus.append(_mm_nt(gj_r, bre[j]) + _mm_nt(gj_i, bim[j]))
            dbr.append(_mm_tn(uj, gj_r))
            dbi.append(_mm_tn(uj, gj_i))
        da_r = jnp.sum(gre * xpr + gim * xpi, axis=1, keepdims=True)
        da_i = jnp.sum(gim * xpr - gre * xpi, axis=1, keepdims=True)
        return dud + jnp.concatenate(dus, axis=-1), jnp.stack(dbr), jnp.stack(dbi), da_r, da_i
    du, gr['b_re'], gr['b_im'], gr['a_re'], gr['a_im'] = _rows(
        "s5_in_bwd", f_s5_in_bwd, s, tm,
        [(g_re, 'r1'), (g_im, 'r1'), (xp_re, 'r1'), (xp_im, 'r1'), (sv['proj'], 'r0'), (du_dir, 'r0'),
         (pl_['b_re'], 'f'), (pl_['b_im'], 'f')],
        [((s, 512), F32, 'r0'), ((4, LANES, 512), F32, 'a'), ((4, LANES, 512), F32, 'a'),
         ((16, 1, LANES), F32, 'a'), ((16, 1, LANES), F32, 'a')])

    dq, dk, dv = _flash_bwd(sv['q'], sv['k'], sv['v'], sv['a_out'], sv['lse'], da_out)

    def f_qkv_bwd(pr, cos_, sin_, dq_, dk_, dv_, gq, gkv, wq, wk, wv, pt):
        cq, ckv = pr[:, 0:Q_LORA], pr[:, Q_LORA:Q_LORA + KV_LORA]
        cqn, rq = _rms(cq, gq[...])
        kvn, rkv = _rms(ckv, gkv[...])
        cqb, kvb = cqn.astype(MXU), kvn.astype(MXU)
        dcqn = jnp.zeros(cq.shape, F32)
        dkvn = jnp.zeros(ckv.shape, F32)
        dksum = jnp.zeros(dk_[0].shape, F32)
        dwq, dwk, dwv = [], [], []
        for hd in range(MLA_HEADS):
            dqp = (_rope_t(dq_[hd], cos_, sin_, pt) * MLA_SCALE).astype(MXU)
            dkb, dvb = dk_[hd].astype(MXU), dv_[hd].astype(MXU)
            dwq.append(_mm_tn(cqb, dqp))
            dwk.append(_mm_tn(kvb, dkb))
            dwv.append(_mm_tn(kvb, dvb))
            dcqn = dcqn + _mm_nt(dqp, wq[hd])
            dkvn = dkvn + _mm_nt(dkb, wk[hd]) + _mm_nt(dvb, wv[hd])
            dksum = dksum + dk_[hd]
        dcq, dgq = _rms_bwd(cq, gq[...], rq, dcqn)
        dckv, dgkv = _rms_bwd(ckv, gkv[...], rkv, dkvn)
        dpa = jnp.concatenate([dcq, dckv, _rope_t(dksum, cos_, sin_, pt)], axis=-1)
        return dpa, jnp.stack(dwq), jnp.stack(dwk), jnp.stack(dwv), dgq, dgkv
    dpa, gr['w_uq'], gr['w_k'], gr['w_v'], gr['q_norm_g'], gr['kv_norm_g'] = _rows(
        "mla_qkv_bwd", f_qkv_bwd, s, tm,
        [(sv['proj'], 'r0'), (cos, 'r0'), (sin, 'r0'), (dq, 'r1'), (dk, 'r1'), (dv, 'r1'), (pl_['q_norm_g'], 'f'),
         (pl_['kv_norm_g'], 'f'), (wl['w_uq'], 'f'), (wl['w_k'], 'f'), (wl['w_v'], 'f'), (pmat_t, 'f')],
        [((s, 512), F32, 'r0'), ((MLA_HEADS, Q_LORA, HEAD_W), F32, 'a'), ((MLA_HEADS, KV_LORA, HEAD_W), F32, 'a'),
         ((MLA_HEADS, KV_LORA, HEAD_W), F32, 'a'), ((1, Q_LORA), F32, 'a'), ((1, KV_LORA), F32, 'a')])

    def f_mix_in_bwd(h_, dpa_, du_, dres, g, w):
        dproj = jnp.concatenate([dpa_, du_], axis=-1).astype(MXU)
        xn, r = _rms(h_, g[...])
        dh, dg = _rms_bwd(h_, g[...], r, _mm_nt(dproj, w[...]))
        return dres + dh, xn, dproj, dg
    dh0, xn, dproj, gr['norm_mix_g'] = _rows(
        "mix_in_bwd", f_mix_in_bwd, s, tm,
        [(sv['h'], 'r0'), (dpa, 'r0'), (du, 'r0'), (dh1, 'r0'), (pl_['norm_mix_g'], 'f'), (wl['w_in'], 'f')],
        [((s, D_MODEL), F32, 'r0'), ((s, D_MODEL), MXU, 'r0'), ((s, D_MODEL), MXU, 'r0'), ((1, D_MODEL), F32, 'a')])
    gr['w_in'] = _mm_tn_call("dw_in", xn[None], dproj[None])[0]
    return dh0, gr


def _layer_weights(gw, l):
    w = {n: gw[n][:, l] for n in SHARDED}
    w_in = w['w_in'].reshape(D_MODEL, -1)
    z = lambda n: jnp.zeros((D_MODEL, n), w_in.dtype)
    wl = {'w_in': jnp.concatenate([w_in[:, :384], z(64), w_in[:, 384:416], z(32), w_in[:, 416:]], axis=1)}
    wl['w_uq'] = jnp.pad(w['w_uq'], ((0, 0), (0, 0), (0, HEAD_W - QK_NOPE - QK_ROPE)))
    wl['w_k'] = jnp.pad(w['w_ukv'][..., :QK_NOPE], ((0, 0), (0, 0), (0, HEAD_W - QK_NOPE)))
    wv = w['w_ukv'][..., QK_NOPE:]
    even = (jnp.arange(MLA_HEADS) % 2 == 0)[:, None, None]
    wl['w_v'] = jnp.concatenate([jnp.where(even, wv, 0), jnp.where(even, 0, wv)], axis=-1).astype(wv.dtype)
    wl['ssm_w_glu'] = w['ssm_w_glu'].reshape(SSM_WIDTH, SSM_WIDTH)
    for n in ('w_out', 'w_xq', 'w_xo'):
        wl[n] = w[n].reshape(D_MODEL, D_MODEL)
    for n in ('w_xkv', 'w_gate', 'w_up', 'w_down'):
        wl[n] = w[n]
    return wl


def _blocked_grads(gr):
    d = gr['w_in']
    out = {'w_in': jnp.concatenate([d[:, :384], d[:, 448:480], d[:, 512:]], axis=1).reshape(N_DEV, 128, -1)}
    out['w_uq'] = gr['w_uq'][..., :QK_NOPE + QK_ROPE]
    even = (jnp.arange(MLA_HEADS) % 2 == 0)[:, None, None]
    dv = gr['w_v']
    out['w_ukv'] = jnp.concatenate([gr['w_k'][..., :QK_NOPE], jnp.where(even, dv[..., :V_DIM], dv[..., V_DIM:])], axis=-1)
    out['ssm_w_glu'] = gr['ssm_w_glu'].reshape(N_DEV, SSM_WIDTH // N_DEV, SSM_WIDTH)
    for n in ('w_out', 'w_xq', 'w_xo'):
        out[n] = gr[n].reshape(N_DEV, D_MODEL // N_DEV, D_MODEL)
    for n in ('w_xkv', 'w_gate', 'w_up', 'w_down'):
        out[n] = gr[n]
    return out


def kernel(x, mem, positions, norm_mix_g, w_in, q_norm_g, w_uq, kv_norm_g, w_ukv, ssm_lambda_re, ssm_lambda_im, ssm_log_dt, ssm_b_re, ssm_b_im, ssm_c_re, ssm_c_im, ssm_d, ssm_w_glu, ssm_b_glu, attn_out_g, ssm_out_g, w_out, norm_x_g, mem_norm_g, w_xq, w_xkv, w_xo, norm_ffn_g, w_gate, w_up, w_down, final_norm_g, loss_target, m_norm_mix_g, m_w_in, m_q_norm_g, m_w_uq, m_kv_norm_g, m_w_ukv, m_ssm_lambda_re, m_ssm_lambda_im, m_ssm_log_dt, m_ssm_b_re, m_ssm_b_im, m_ssm_c_re, m_ssm_c_im, m_ssm_d, m_ssm_w_glu, m_ssm_b_glu, m_attn_out_g, m_ssm_out_g, m_w_out, m_norm_x_g, m_mem_norm_g, m_w_xq, m_w_xkv, m_w_xo, m_norm_ffn_g, m_w_gate, m_w_up, m_w_down, m_final_norm_g, v_norm_mix_g, v_w_in, v_q_norm_g, v_w_uq, v_kv_norm_g, v_w_ukv, v_ssm_lambda_re, v_ssm_lambda_im, v_ssm_log_dt, v_ssm_b_re, v_ssm_b_im, v_ssm_c_re, v_ssm_c_im, v_ssm_d, v_ssm_w_glu, v_ssm_b_glu, v_attn_out_g, v_ssm_out_g, v_w_out, v_norm_x_g, v_mem_norm_g, v_w_xq, v_w_xkv, v_w_xo, v_norm_ffn_g, v_w_gate, v_w_up, v_w_down, v_final_norm_g):
    args = dict(locals())
    W = {n: args[n] for n in WEIGHTS}
    M = {n: args['m_' + n] for n in WEIGHTS}
    V = {n: args['v_' + n] for n in WEIGHTS}
    s = x.shape[1]
    h = x[0]
    memx = mem[0]

    freqs = ROPE_THETA ** (-jnp.arange(0, QK_ROPE, 2, dtype=F32) / QK_ROPE)
    ang = positions[0].astype(F32)[:, None] * freqs
    c16, s16 = jnp.cos(ang), jnp.sin(ang)
    cos = jnp.concatenate([jnp.ones((s, QK_NOPE), F32), c16, c16, jnp.zeros((s, 32), F32)], axis=1)
    sin = jnp.concatenate([jnp.zeros((s, QK_NOPE), F32), s16, s16, jnp.zeros((s, 32), F32)], axis=1)
    idx = jnp.arange(QK_ROPE // 2)
    pmat = jnp.zeros((HEAD_W, HEAD_W), F32)
    pmat = pmat.at[QK_NOPE + 16 + idx, QK_NOPE + idx].set(-1.0).at[QK_NOPE + idx, QK_NOPE + 16 + idx].set(1.0)
    tabs = (cos, sin, pmat, pmat.T)

    gathered = _all_gather("gather_weights", [W[n].astype(MXU) for n in SHARDED])
    gw = dict(zip(SHARDED, gathered))

    layers = []
    for l in range(DEPTH):
        wl = _layer_weights(gw, l)
        s5_in = [W[n][l] for n in ('ssm_lambda_re', 'ssm_lambda_im', 'ssm_log_dt', 'ssm_b_re', 'ssm_b_im',
                                   'ssm_c_re', 'ssm_c_im')]
        (a_re, a_im, bre, bim, cre, cim), s5_vjp = jax.vjp(_s5_params, *s5_in)
        pl_ = {n: W[n][l][None] for n in ('norm_mix_g', 'q_norm_g', 'kv_norm_g', 'ssm_d', 'ssm_b_glu',
                                           'attn_out_g', 'ssm_out_g', 'norm_x_g', 'mem_norm_g', 'norm_ffn_g')}
        pl_.update(a_re=a_re, a_im=a_im, b_re=bre, b_im=bim, c_re=cre, c_im=cim)
        h, sv = _layer_fwd(h, memx, tabs, wl, pl_)
        layers.append((wl, pl_, sv, s5_vjp))

    def f_loss(h_, tgt, g):
        y, r = _rms(h_, g[...])
        err = y - tgt
        part = 0.5 * jnp.sum(jnp.mean(err * err, axis=-1, keepdims=True), axis=0, keepdims=True)
        dh, dg = _rms_bwd(h_, g[...], r, err / D_MODEL)
        return dh, dg, jnp.broadcast_to(part, (8, LANES))
    dh, g_final, loss_part = _rows(
        "loss_head", f_loss, s, min(256, s), [(h, 'r0'), (loss_target[0], 'r0'), (final_norm_g[None], 'f')],
        [((s, D_MODEL), F32, 'r0'), ((1, D_MODEL), F32, 'a'), ((8, LANES), F32, 'a')])
    loss = lax.psum(loss_part[0, 0], ("x", "y", "c"))

    g_sh = [None] * DEPTH
    g_rep = [None] * DEPTH
    for l in reversed(range(DEPTH)):
        wl, pl_, sv, s5_vjp = layers[l]
        dh, gr = _layer_bwd(dh, sv, memx, tabs, wl, pl_)
        g_sh[l] = _blocked_grads(gr)
        ds5 = s5_vjp((gr['a_re'], gr['a_im'], gr['b_re'], gr['b_im'], gr['c_re'], gr['c_im']))
        rep = dict(zip(('ssm_lambda_re', 'ssm_lambda_im', 'ssm_log_dt', 'ssm_b_re', 'ssm_b_im', 'ssm_c_re',
                        'ssm_c_im'), ds5))
        for n in ('norm_mix_g', 'q_norm_g', 'kv_norm_g', 'ssm_d', 'ssm_b_glu', 'attn_out_g', 'ssm_out_g',
                  'norm_x_g', 'mem_norm_g', 'norm_ffn_g'):
            rep[n] = gr[n][0]
        g_rep[l] = rep
    grad_x = dh[None]

    tr_sh = 1024
    send = _pack([jnp.stack([g_sh[l][n] for l in range(DEPTH)], axis=1).astype(MXU) for n in SHARDED], tr_sh)
    parts = _all_to_all("scatter_grads", send)
    pk = lambda d: _pack([d[n][None] for n in SHARDED], tr_sh)[0]
    res_sh = _adamw("adamw_sharded", parts, pk(W), pk(M), pk(V), tr_sh)
    shapes_sh = [W[n].shape for n in SHARDED]
    out_sh = [dict(zip(SHARDED, _unpack(r, shapes_sh))) for r in res_sh]

    tr_rp = 256
    rep_names = REPL_L + ['final_norm_g']
    g_loc = {n: jnp.stack([g_rep[l][n] for l in range(DEPTH)]) for n in REPL_L}
    g_loc['final_norm_g'] = g_final[0]
    pk_r = lambda d: _pack([d[n].reshape(1, -1) for n in rep_names], tr_rp)[0]
    parts_r, = _all_gather("gather_small_grads", [pk_r(g_loc)])
    res_rp = _adamw("adamw_replicated", parts_r, pk_r(W), pk_r(M), pk_r(V), tr_rp)
    shapes_rp = [W[n].shape for n in rep_names]
    out_rp = [dict(zip(rep_names, _unpack(r, shapes_rp))) for r in res_rp]

    outs = [loss, grad_x]
    for kind in range(4):
        for n in WEIGHTS:
            outs.append(out_sh[kind][n] if n in SHARDED else out_rp[kind][n])
    return tuple(outs)
```

```python
import functools

import jax
import jax.numpy as jnp
from jax import lax
from jax.experimental import pallas as pl
from jax.experimental.pallas import tpu as pltpu

F32 = jnp.float32
MXU = jnp.bfloat16
HI = lax.Precision.HIGHEST

D_MODEL = 1024
MLA_HEADS = 8
QK_NOPE = 64
QK_ROPE = 32
V_DIM = 64
Q_LORA = 256
KV_LORA = 128
SSM_WIDTH = 512
SSM_GROUPS = 32
SSM_GROUP = 16
SSM_STATE = 64
X_HEADS = 4
X_HEAD_DIM = 256
D_FF = 2816
FF_BLK = D_FF // 8
ROPE_THETA = 10000.0
EPS = 1e-6
DEPTH = 2
N_DEV = 8
LANES = 128
HEAD_W = 128
MLA_SCALE = (QK_NOPE + QK_ROPE) ** -0.5
X_SCALE = X_HEAD_DIM ** -0.5
ADAM_LR, ADAM_B1, ADAM_B2, ADAM_EPS, ADAM_WD, ADAM_STEP = 0.001, 0.9, 0.999, 1e-08, 0.01, 10
VMEM_LIMIT = 56 * 1024 * 1024
MESH = pl.DeviceIdType.MESH

SHARDED = ['w_in', 'w_uq', 'w_ukv', 'ssm_w_glu', 'w_out', 'w_xq', 'w_xkv', 'w_xo', 'w_gate', 'w_up', 'w_down']
REPL_L = ['norm_mix_g', 'q_norm_g', 'kv_norm_g', 'ssm_lambda_re', 'ssm_lambda_im', 'ssm_log_dt', 'ssm_b_re',
          'ssm_b_im', 'ssm_c_re', 'ssm_c_im', 'ssm_d', 'ssm_b_glu', 'attn_out_g', 'ssm_out_g', 'norm_x_g',
          'mem_norm_g', 'norm_ffn_g']
WEIGHTS = ['norm_mix_g', 'w_in', 'q_norm_g', 'w_uq', 'kv_norm_g', 'w_ukv', 'ssm_lambda_re', 'ssm_lambda_im',
           'ssm_log_dt', 'ssm_b_re', 'ssm_b_im', 'ssm_c_re', 'ssm_c_im', 'ssm_d', 'ssm_w_glu', 'ssm_b_glu',
           'attn_out_g', 'ssm_out_g', 'w_out', 'norm_x_g', 'mem_norm_g', 'w_xq', 'w_xkv', 'w_xo', 'norm_ffn_g',
           'w_gate', 'w_up', 'w_down', 'final_norm_g']


def _pcall(body, **kw):
    return pl.pallas_call(body, **kw)


def _mm(a, b):
    return jnp.dot(a.astype(MXU), b.astype(MXU), preferred_element_type=F32)


def _mm_nt(a, b):
    return lax.dot_general(a.astype(MXU), b.astype(MXU), (((1,), (1,)), ((), ())), preferred_element_type=F32)


def _mm_tn(a, b):
    return lax.dot_general(a.astype(MXU), b.astype(MXU), (((0,), (0,)), ((), ())), preferred_element_type=F32)


def _mm_hi(a, b):
    return jnp.dot(a.astype(F32), b.astype(F32), precision=HI, preferred_element_type=F32)


def _rms(x, g):
    r = lax.rsqrt(jnp.mean(x * x, axis=-1, keepdims=True) + EPS)
    return x * r * g, r


def _rms_bwd(x, g, r, dy):
    dyg = dy * g
    dx = r * dyg - x * (r * r * r) * jnp.mean(dyg * x, axis=-1, keepdims=True)
    return dx, jnp.sum(dy * x * r, axis=0, keepdims=True)


def _rope(x, cos, sin, p_ref):
    return x * cos + _mm_hi(x, p_ref[...]) * sin


def _rope_t(g, cos, sin, pt_ref):
    return g * cos + _mm_hi(g * sin, pt_ref[...])


def _softmax(s):
    m = jnp.max(s, axis=-1, keepdims=True)
    e = jnp.exp(s - m)
    return e / jnp.sum(e, axis=-1, keepdims=True)


def _lanes(x, j, w):
    return x[:, j * w:(j + 1) * w]


def _rows(name, fn, n, tm, ins, outs):
    def spec(shape, kind):
        nd = len(shape)
        if kind == 'p8':
            return pl.BlockSpec((shape[0], 8, shape[2]), lambda i: (0, jnp.maximum(i * (tm // 8) - 1, 0), 0))
        if kind == 'f':
            return pl.BlockSpec(shape, lambda i, _nd=nd: (0,) * _nd, pipeline_mode=pl.Buffered(1))
        if kind == 'a':
            return pl.BlockSpec(shape, lambda i, _nd=nd: (0,) * _nd)
        ax = int(kind[1])
        blk = tuple(tm if d == ax else s for d, s in enumerate(shape))
        return pl.BlockSpec(blk, lambda i, _ax=ax, _nd=nd: tuple(i if d == _ax else 0 for d in range(_nd)))

    n_in = len(ins)

    def body(*refs):
        args = [r if k == 'f' else r[...] for r, (_, k) in zip(refs[:n_in], ins)]
        res = fn(*args)
        i = pl.program_id(0)
        for r, (_, dt, k), v in zip(refs[n_in:], outs, res):
            if k == 'a':
                _accumulate(r, v.astype(dt), i)
            else:
                r[...] = v.astype(dt)

    return _pcall(
        body, name=name, grid=(n // tm,),
        in_specs=[spec(a.shape, k) for a, k in ins],
        out_specs=[spec(s, k) for s, _, k in outs],
        out_shape=[jax.ShapeDtypeStruct(s, dt) for s, dt, _ in outs],
        compiler_params=pltpu.CompilerParams(dimension_semantics=("arbitrary",), vmem_limit_bytes=VMEM_LIMIT),
    )(*[a for a, _ in ins])


def _accumulate(ref, v, i):
    @pl.when(i == 0)
    def _():
        ref[...] = v

    @pl.when(i != 0)
    def _():
        ref[...] += v


def _mm_tn_call(name, a, b):
    out_dtype = MXU
    ba, s, k = a.shape
    bb, _, n = b.shape
    nb = max(ba, bb)
    ts = min(512, s)
    ns = s // ts

    def body(a_ref, b_ref, o_ref, acc_ref):
        j = pl.program_id(1)
        _accumulate(acc_ref, _mm_tn(a_ref[...], b_ref[...]), j)

        @pl.when(j == ns - 1)
        def _():
            o_ref[...] = acc_ref[...].astype(out_dtype)

    return _pcall(
        body, name=name, grid=(nb, ns),
        in_specs=[pl.BlockSpec((None, ts, k), (lambda i, j: (i, j, 0)) if ba > 1 else (lambda i, j: (0, j, 0))),
                  pl.BlockSpec((None, ts, n), (lambda i, j: (i, j, 0)) if bb > 1 else (lambda i, j: (0, j, 0)))],
        out_specs=pl.BlockSpec((None, k, n), lambda i, j: (i, 0, 0)),
        out_shape=jax.ShapeDtypeStruct((nb, k, n), out_dtype),
        scratch_shapes=[pltpu.VMEM((k, n), F32)],
        compiler_params=pltpu.CompilerParams(dimension_semantics=("arbitrary", "arbitrary"),
                                             vmem_limit_bytes=VMEM_LIMIT),
    )(a, b)


def _flash_fwd(q, k, v):
    nh, s, w = q.shape
    t = min(256, s)

    def body(q_ref, k_ref, v_ref, o_ref, lse_ref):
        qi = pl.program_id(1)
        qs = [q_ref[0], q_ref[1]]
        below = lax.broadcasted_iota(jnp.int32, (t, t), 1) <= lax.broadcasted_iota(jnp.int32, (t, t), 0)

        def tile(j, carry, diagonal):
            sl = pl.ds(pl.multiple_of(j * t, t), t)
            out = []
            for hh in range(2):
                m, l, acc = carry[3 * hh:3 * hh + 3]
                sc = _mm_nt(qs[hh], k_ref[hh, sl, :])
                if diagonal:
                    sc = jnp.where(below, sc, -1e30)
                m_new = jnp.maximum(m, jnp.max(sc, axis=1, keepdims=True))
                p = jnp.exp(sc - m_new)
                alpha = jnp.exp(m - m_new)
                out += [m_new, alpha * l + jnp.sum(p, axis=1, keepdims=True), alpha * acc + _mm(p, v_ref[hh, sl, :])]
            return tuple(out)

        init = (jnp.full((t, 1), -1e30, F32), jnp.zeros((t, 1), F32), jnp.zeros((t, w), F32)) * 2
        carry = lax.fori_loop(0, qi, lambda j, c: tile(j, c, False), init)
        carry = tile(qi, carry, True)
        o_ref[...] = carry[2] / carry[1] + carry[5] / carry[4]
        for hh in range(2):
            lse_ref[hh] = jnp.broadcast_to(carry[3 * hh] + jnp.log(carry[3 * hh + 1]), (t, w))

    return _pcall(
        body, name="mla_flash_fwd", grid=(nh // 2, s // t),
        in_specs=[pl.BlockSpec((2, t, w), lambda p, i: (p, i, 0)),
                  pl.BlockSpec((2, s, w), lambda p, i: (p, 0, 0)),
                  pl.BlockSpec((2, s, w), lambda p, i: (p, 0, 0))],
        out_specs=[pl.BlockSpec((t, w), lambda p, i: (i, p)),
                   pl.BlockSpec((2, t, w), lambda p, i: (p, i, 0))],
        out_shape=[jax.ShapeDtypeStruct((s, (nh // 2) * w), F32), jax.ShapeDtypeStruct((nh, s, w), F32)],
        compiler_params=pltpu.CompilerParams(dimension_semantics=("arbitrary", "arbitrary"),
                                             vmem_limit_bytes=VMEM_LIMIT),
    )(q, k, v)


def _flash_bwd(q, k, v, o, lse, do):
    nh, s, w = q.shape
    t = min(256, s)
    nq = s // t

    def body(q_ref, k_ref, v_ref, o_ref, lse_ref, do_ref, dq_ref, dk_ref, dv_ref):
        j = pl.program_id(1)

        @pl.when(j == 0)
        def _():
            dq_ref[...] = jnp.zeros(dq_ref.shape, F32)

        below = lax.broadcasted_iota(jnp.int32, (t, t), 1) <= lax.broadcasted_iota(jnp.int32, (t, t), 0)
        lane = lax.broadcasted_iota(jnp.int32, (t, w), 1)
        heads = [jnp.logical_and(lane >= hh * V_DIM, lane < (hh + 1) * V_DIM) for hh in range(2)]
        ks = [k_ref[0], k_ref[1]]
        vs = [v_ref[0], v_ref[1]]

        def tile(i, carry, diagonal):
            sl = pl.ds(pl.multiple_of(i * t, t), t)
            dout_all, o_all = do_ref[sl, :], o_ref[sl, :]
            out = []
            for hh in range(2):
                dk, dv = carry[2 * hh], carry[2 * hh + 1]
                qh = q_ref[hh, sl, :]
                dout = jnp.where(heads[hh], dout_all, 0.0)
                sc = _mm_nt(qh, ks[hh])
                if diagonal:
                    sc = jnp.where(below, sc, -1e30)
                p = jnp.exp(sc - lse_ref[hh, sl, 0:1])
                dp = _mm_nt(dout, vs[hh])
                ds = p * (dp - jnp.sum(dout * o_all, axis=1, keepdims=True))
                dq_ref[hh, sl, :] += _mm(ds, ks[hh])
                out += [dk + _mm_tn(ds, qh), dv + _mm_tn(p, dout)]
            return tuple(out)

        carry = tile(j, (jnp.zeros((t, w), F32),) * 4, True)
        carry = lax.fori_loop(j + 1, nq, lambda i, c: tile(i, c, False), carry)
        for hh in range(2):
            dk_ref[hh] = carry[2 * hh]
            dv_ref[hh] = jnp.where(heads[hh], carry[2 * hh + 1], 0.0)

    return _pcall(
        body, name="mla_flash_bwd", grid=(nh // 2, nq),
        in_specs=[pl.BlockSpec((2, s, w), lambda p, j: (p, 0, 0)),
                  pl.BlockSpec((2, t, w), lambda p, j: (p, j, 0)),
                  pl.BlockSpec((2, t, w), lambda p, j: (p, j, 0)),
                  pl.BlockSpec((s, w), lambda p, j: (0, p)),
                  pl.BlockSpec((2, s, w), lambda p, j: (p, 0, 0)),
                  pl.BlockSpec((s, w), lambda p, j: (0, p))],
        out_specs=[pl.BlockSpec((2, s, w), lambda p, j: (p, 0, 0)),
                   pl.BlockSpec((2, t, w), lambda p, j: (p, j, 0)),
                   pl.BlockSpec((2, t, w), lambda p, j: (p, j, 0))],
        out_shape=[jax.ShapeDtypeStruct((nh, s, w), F32)] * 3,
        compiler_params=pltpu.CompilerParams(dimension_semantics=("arbitrary", "arbitrary"),
                                             vmem_limit_bytes=VMEM_LIMIT),
    )(q, k, v, o, lse, do)


def _scan(b_re, b_im, a_re, a_im, reverse):
    nb, s, w = b_re.shape
    ch = s // 8
    assert ch & (ch - 1) == 0
    grp = 2

    def cmul(ar, ai, xr, xi):
        return ar * xr - ai * xi, ar * xi + ai * xr

    def body(br_ref, bi_ref, ar_ref, ai_ref, xr_ref, xi_ref):
        sub = lax.broadcasted_iota(jnp.int32, (8, w), 0)

        def shift(x, k):
            if reverse:
                return jnp.where(sub < 8 - k, pltpu.roll(x, 8 - k, 0), 0.0)
            return jnp.where(sub >= k, pltpu.roll(x, k, 0), 0.0)

        ar = [jnp.broadcast_to(ar_ref[g], (8, w)) for g in range(grp)]
        ai = [jnp.broadcast_to(ai_ref[g], (8, w)) for g in range(grp)]

        def tsl(i):
            return pl.ds(pl.multiple_of(((ch - 1 - i) if reverse else i) * 8, 8), 8)

        def local(i, carry):
            out = []
            for g in range(grp):
                xr, xi = carry[2 * g], carry[2 * g + 1]
                pr, pi = cmul(ar[g], ai[g], xr, xi)
                nr = pr + br_ref[g, tsl(i), :]
                ni = pi + bi_ref[g, tsl(i), :]
                xr_ref[g, tsl(i), :] = nr
                xi_ref[g, tsl(i), :] = ni
                out += [nr, ni]
            return tuple(out)

        fin = lax.fori_loop(0, ch, local, (jnp.zeros((8, w), F32),) * (2 * grp))

        carry_in = []
        for g in range(grp):
            pr, pi = ar[g], ai[g]
            for _ in range(ch.bit_length() - 1):
                pr, pi = cmul(pr, pi, pr, pi)
            fr, fi = fin[2 * g], fin[2 * g + 1]
            for kk in (1, 2, 4):
                sr, si = cmul(pr, pi, shift(fr, kk), shift(fi, kk))
                fr, fi = fr + sr, fi + si
                pr, pi = cmul(pr, pi, pr, pi)
            carry_in += [shift(fr, 1), shift(fi, 1)]

        def fix(i, pw):
            out = []
            for g in range(grp):
                pr, pi = pw[2 * g], pw[2 * g + 1]
                cr, ci = cmul(pr, pi, carry_in[2 * g], carry_in[2 * g + 1])
                xr_ref[g, tsl(i), :] = xr_ref[g, tsl(i), :] + cr
                xi_ref[g, tsl(i), :] = xi_ref[g, tsl(i), :] + ci
                nr, ni = cmul(pr, pi, ar[g], ai[g])
                out += [nr, ni]
            return tuple(out)

        lax.fori_loop(0, ch, fix, tuple(x for g in range(grp) for x in (ar[g], ai[g])))

    blk = pl.BlockSpec((grp, s, w), lambda i: (i, 0, 0))
    ablk = pl.BlockSpec((grp, 1, w), lambda i: (i, 0, 0))
    return _pcall(
        body, name="s5_scan_rev" if reverse else "s5_scan", grid=(nb // grp,),
        in_specs=[blk, blk, ablk, ablk], out_specs=[blk, blk],
        out_shape=[jax.ShapeDtypeStruct((nb, s, w), F32)] * 2,
        compiler_params=pltpu.CompilerParams(dimension_semantics=("arbitrary",), vmem_limit_bytes=VMEM_LIMIT),
    )(b_re, b_im, a_re, a_im)


def _all_gather(name, arrs):
    n = len(arrs)

    def body(*refs):
        ins, outs = refs[:n], refs[n:2 * n]
        send_sems, recv_sems, local_sems = refs[2 * n:]
        x, y, c = lax.axis_index("x"), lax.axis_index("y"), lax.axis_index("c")
        me, sibling = (x, y, c), (x, y, 1 - c)
        chips = [(1 - x, y), (x, 1 - y), (1 - x, 1 - y)]

        def copy(a, k, block, to, src=None):
            dst = outs[a].at[4 * block[0] + 2 * block[1] + block[2]]
            return pltpu.make_async_remote_copy(
                src_ref=dst if src is None else src, dst_ref=dst,
                send_sem=send_sems.at[a, k], recv_sem=recv_sems.at[a, k], device_id=to, device_id_type=MESH)

        mine = [pltpu.make_async_copy(ins[a], outs[a].at[4 * x + 2 * y + c], local_sems.at[a]) for a in range(n)]
        for cp in mine:
            cp.start()
        first = []
        for a in range(n):
            first.append(copy(a, 0, me, sibling, src=ins[a]))
            first += [copy(a, 1 + j, me, (*chip, c), src=ins[a]) for j, chip in enumerate(chips)]
        for cp in first:
            cp.start()
        passed = []
        for j, chip in enumerate(chips):
            for a in range(n):
                copy(a, 1 + j, (*chip, c), me).wait_recv()
                fwd = copy(a, 4 + j, (*chip, c), sibling)
                fwd.start()
                passed.append(fwd)
        for a in range(n):
            copy(a, 0, sibling, me).wait_recv()
            for j, chip in enumerate(chips):
                copy(a, 4 + j, (*chip, 1 - c), me).wait_recv()
        for cp in first + passed:
            cp.wait_send()
        for cp in mine:
            cp.wait()

    any_spec = pl.BlockSpec(memory_space=pl.ANY)
    return _pcall(
        body, name=name,
        in_specs=[any_spec] * n, out_specs=[any_spec] * n,
        out_shape=[jax.ShapeDtypeStruct((N_DEV,) + a.shape, a.dtype) for a in arrs],
        scratch_shapes=[pltpu.SemaphoreType.DMA((n, 7)), pltpu.SemaphoreType.DMA((n, 7)),
                        pltpu.SemaphoreType.DMA((n,))],
    )(*arrs)


def _all_to_all(name, grads):
    nl, n = len(grads), len(grads[0])
    flat = [g for gl in grads for g in gl]

    def body(*refs):
        ins, outs = refs[:nl * n], refs[nl * n:nl * n + n]
        send_sems, recv_sems, local_sems = refs[nl * n + n:]
        x, y, c = lax.axis_index("x"), lax.axis_index("y"), lax.axis_index("c")
        me = 4 * x + 2 * y + c
        own, sent, arrivals = [], [], []
        for l in range(nl):
            for a in range(n):
                i = l * n + a
                own.append(pltpu.make_async_copy(ins[i].at[me], outs[a].at[l, me], local_sems.at[i]))
                for k in range(1, N_DEV):
                    px, py, pc = x ^ ((k >> 2) & 1), y ^ ((k >> 1) & 1), c ^ (k & 1)
                    peer = 4 * px + 2 * py + pc
                    sent.append(pltpu.make_async_remote_copy(
                        src_ref=ins[i].at[peer], dst_ref=outs[a].at[l, me],
                        send_sem=send_sems.at[i, k - 1], recv_sem=recv_sems.at[i, k - 1],
                        device_id=(px, py, pc), device_id_type=MESH))
                    arrivals.append(pltpu.make_async_remote_copy(
                        src_ref=ins[i].at[me], dst_ref=outs[a].at[l, peer],
                        send_sem=send_sems.at[i, k - 1], recv_sem=recv_sems.at[i, k - 1],
                        device_id=(x, y, c), device_id_type=MESH))
        for cp in own + sent:
            cp.start()
        for cp in arrivals:
            cp.wait_recv()
        for cp in sent:
            cp.wait_send()
        for cp in own:
            cp.wait()

    any_spec = pl.BlockSpec(memory_space=pl.ANY)
    return _pcall(
        body, name=name, in_specs=[any_spec] * (nl * n), out_specs=[any_spec] * n,
        out_shape=[jax.ShapeDtypeStruct((nl,) + g.shape, g.dtype) for g in grads[0]],
        scratch_shapes=[pltpu.SemaphoreType.DMA((nl * n, N_DEV - 1)), pltpu.SemaphoreType.DMA((nl * n, N_DEV - 1)),
                        pltpu.SemaphoreType.DMA((nl * n,))],
    )(*flat)


def _adam_math(g, w_, m_, v_):
    m_new = ADAM_B1 * m_ + (1.0 - ADAM_B1) * g
    v_new = ADAM_B2 * v_ + (1.0 - ADAM_B2) * (g * g)
    m_hat = m_new / (1.0 - ADAM_B1 ** ADAM_STEP)
    v_hat = v_new / (1.0 - ADAM_B2 ** ADAM_STEP)
    delta = -ADAM_LR * (m_hat / (jnp.sqrt(v_hat) + ADAM_EPS) + ADAM_WD * w_)
    return delta, m_new, v_new


def _adamw_weight(name, parts, w, m, v):
    nl, _, r, c = parts.shape

    def body(p_ref, w_ref, m_ref, v_ref, g_ref, d_ref, mo_ref, vo_ref):
        g = p_ref[0].astype(F32)
        for j in range(1, N_DEV):
            g = g + p_ref[j].astype(F32)
        g_ref[...] = g
        d_ref[...], mo_ref[...], vo_ref[...] = _adam_math(g, w_ref[...], m_ref[...], v_ref[...])

    blk = pl.BlockSpec((None, r, c), lambda l: (l, 0, 0))
    return _pcall(
        body, name=name, grid=(nl,),
        in_specs=[pl.BlockSpec((None, N_DEV, r, c), lambda l: (l, 0, 0, 0)), blk, blk, blk],
        out_specs=[blk] * 4, out_shape=[jax.ShapeDtypeStruct((nl, r, c), F32)] * 4,
        compiler_params=pltpu.CompilerParams(dimension_semantics=("arbitrary",), vmem_limit_bytes=VMEM_LIMIT),
    )(parts, w, m, v)

def _adamw(name, parts, w, m, v, tr):
    r = w.shape[0]

    def fn(p, w_, m_, v_):
        g = p[0].astype(F32)
        for j in range(1, N_DEV):
            g = g + p[j].astype(F32)
        return (g,) + _adam_math(g, w_, m_, v_)

    return _rows(name, fn, r, tr, [(parts, 'r1'), (w, 'r0'), (m, 'r0'), (v, 'r0')],
                 [((r, LANES), F32, 'r0')] * 4)


def _pack(arrs, rows_mult):
    lead = arrs[0].shape[0]
    flat = jnp.concatenate([a.reshape(lead, -1) for a in arrs], axis=1)
    per = rows_mult * LANES
    pad = (-flat.shape[1]) % per
    flat = jnp.pad(flat, ((0, 0), (0, pad)))
    return flat.reshape(lead, -1, LANES)


def _unpack(packed, shapes):
    flat = packed.reshape(-1)
    out, off = [], 0
    for shp in shapes:
        size = 1
        for d in shp:
            size *= d
        out.append(flat[off:off + size].reshape(shp))
        off += size
    return out


def _s5_params(lam_re, lam_im, log_dt, b_re, b_im, c_re, c_im):
    dt = jnp.exp(log_dt)[:, None]
    e = jnp.exp(lam_re * dt)
    ang = lam_im * dt
    a_re, a_im = e * jnp.cos(ang), e * jnp.sin(ang)
    nr, ni = a_re - 1.0, a_im
    den = lam_re * lam_re + lam_im * lam_im
    cr = ((nr * lam_re + ni * lam_im) / den)[..., None]
    ci = ((ni * lam_re - nr * lam_im) / den)[..., None]
    bb_re = cr * b_re - ci * b_im
    bb_im = cr * b_im + ci * b_re
    eye = jnp.eye(8, dtype=F32)[None, :, None, :, None]

    def bblk(bb):
        t = jnp.transpose(bb.reshape(4, 8, SSM_STATE, SSM_GROUP), (0, 3, 1, 2))
        return (eye * t[:, None]).reshape(4, 8 * SSM_GROUP, 8 * SSM_STATE)

    def cblk(cc):
        t = jnp.transpose(cc.reshape(4, 8, SSM_GROUP, SSM_STATE), (0, 3, 1, 2))
        return (eye * t[:, None]).reshape(4, 8 * SSM_STATE, 8 * SSM_GROUP)

    nb = SSM_GROUPS * SSM_STATE // LANES
    return (a_re.reshape(nb, 1, LANES), a_im.reshape(nb, 1, LANES), bblk(bb_re), bblk(bb_im),
            cblk(c_re), -cblk(c_im))


def _cat_blocks(x3, j):
    return jnp.concatenate([x3[4 * j + k] for k in range(4)], axis=-1)


def _to_chunks(a):
    s, c = a.shape
    return a.reshape(8, s // 8, c).transpose(1, 0, 2).reshape(s, c)


def _from_chunks(a):
    s, c = a.shape
    return a.reshape(s // 8, 8, c).transpose(1, 0, 2).reshape(s, c)


def _layer_fwd(h, memx, tabs, wl, pl_):
    s = h.shape[0]
    tm = min(256, s)
    cos, sin, pmat, pmat_t = tabs
    sv = {}

    def f_mix_in(h_, g, w):
        xn, _ = _rms(h_, g[...])
        return (_mm(xn, w[...]),)
    proj, = _rows("mix_in", f_mix_in, s, tm, [(h, 'r0'), (pl_['norm_mix_g'], 'f'), (wl['w_in'], 'f')],
                  [((s, D_MODEL), F32, 'r0')])

    def f_qkv(pr, cos_, sin_, gq, gkv, wq, wk, wv, pm):
        cqn = _rms(pr[:, 0:Q_LORA], gq[...])[0].astype(MXU)
        kvn = _rms(pr[:, Q_LORA:Q_LORA + KV_LORA], gkv[...])[0].astype(MXU)
        krr = _rope(pr[:, 384:512], cos_, sin_, pm)
        qs, ks, vs = [], [], []
        for hd in range(MLA_HEADS):
            qs.append(_rope(_mm(cqn, wq[hd]), cos_, sin_, pm) * MLA_SCALE)
            ks.append(_mm(kvn, wk[hd]) + krr)
            vs.append(_mm(kvn, wv[hd]))
        return jnp.stack(qs), jnp.stack(ks), jnp.stack(vs)
    hshape = (MLA_HEADS, s, HEAD_W)
    q, k, v = _rows("mla_qkv", f_qkv, s, tm,
                    [(proj, 'r0'), (cos, 'r0'), (sin, 'r0'), (pl_['q_norm_g'], 'f'), (pl_['kv_norm_g'], 'f'),
                     (wl['w_uq'], 'f'), (wl['w_k'], 'f'), (wl['w_v'], 'f'), (pmat, 'f')],
                    [(hshape, MXU, 'r1')] * 3)
    a_out, lse = _flash_fwd(q, k, v)

    u_ch = _to_chunks(proj[:, 512:1024])

    def f_s5_in(u, bre, bim):
        outs_r, outs_i = [], []
        for j in range(4):
            uj = _lanes(u, j, LANES)
            rr, ri = _mm_hi(uj, bre[j]), _mm_hi(uj, bim[j])
            outs_r += [_lanes(rr, kk, LANES) for kk in range(4)]
            outs_i += [_lanes(ri, kk, LANES) for kk in range(4)]
        return jnp.stack(outs_r), jnp.stack(outs_i)
    xshape = (16, s, LANES)
    bu_re, bu_im = _rows("s5_in", f_s5_in, s, tm, [(u_ch, 'r0'), (pl_['b_re'], 'f'), (pl_['b_im'], 'f')],
                         [(xshape, F32, 'r1')] * 2)
    x_re, x_im = _scan(bu_re, bu_im, pl_['a_re'], pl_['a_im'], False)

    def f_s5_out(xr, xi, u, cre, cim, d, wglu, bglu):
        y = jnp.concatenate([_mm_hi(_cat_blocks(xr, j), cre[j]) + _mm_hi(_cat_blocks(xi, j), cim[j])
                             for j in range(4)], axis=-1) + d[...] * u
        z = _mm(jax.nn.gelu(y), wglu[...]) + bglu[...]
        return y, y * jax.nn.sigmoid(z)
    y_ssm, s_out_ch = _rows("s5_out", f_s5_out, s, tm,
                            [(x_re, 'r1'), (x_im, 'r1'), (u_ch, 'r0'), (pl_['c_re'], 'f'), (pl_['c_im'], 'f'),
                             (pl_['ssm_d'], 'f'), (wl['ssm_w_glu'], 'f'), (pl_['ssm_b_glu'], 'f')],
                            [((s, SSM_WIDTH), F32, 'r0')] * 2)
    s_out = _from_chunks(s_out_ch)

    def f_mix_out(h_, a, so, ga, gs, w):
        an = _rms(a, ga[...])[0]
        sn = _rms(so, gs[...])[0]
        return (h_ + _mm(jnp.concatenate([an, sn], axis=-1), w[...]),)
    h1, = _rows("mix_out", f_mix_out, s, tm,
                [(h, 'r0'), (a_out, 'r0'), (s_out, 'r0'), (pl_['attn_out_g'], 'f'), (pl_['ssm_out_g'], 'f'),
                 (wl['w_out'], 'f')], [((s, D_MODEL), F32, 'r0')])

    m_len = memx.shape[0]

    def f_memkv(mm_, g, w):
        mn = _rms(mm_, g[...])[0].astype(MXU)
        return (jnp.stack([_mm(mn, w[d]) for d in range(N_DEV)]),)
    kvm, = _rows("mem_kv", f_memkv, m_len, m_len, [(memx, 'r0'), (pl_['mem_norm_g'], 'f'), (wl['w_xkv'], 'f')],
                 [((N_DEV, m_len, X_HEAD_DIM), MXU, 'r1')])

    def f_xattn(h_, g, wq, kv_, wo):
        hn = _rms(h_, g[...])[0].astype(MXU)
        out = jnp.zeros(h_.shape, F32)
        for hd in range(X_HEADS):
            cs = pl.ds(hd * X_HEAD_DIM, X_HEAD_DIM)
            qh = _mm(hn, wq[:, cs])
            p = _softmax(_mm_nt(qh, kv_[hd]) * X_SCALE)
            out = out + _mm(_mm(p, kv_[X_HEADS + hd]), wo[cs, :])
        return (h_ + out,)
    h2, = _rows("xattn", f_xattn, s, tm, [(h1, 'r0'), (pl_['norm_x_g'], 'f'), (wl['w_xq'], 'f'), (kvm, 'f'),
                                          (wl['w_xo'], 'f')], [((s, D_MODEL), F32, 'r0')])

    def f_ffn(h_, g, wg, wu, wd):
        hn = _rms(h_, g[...])[0].astype(MXU)
        y = jnp.zeros(h_.shape, F32)
        for d in range(N_DEV):
            gate = _mm(hn, wg[d])
            y = y + _mm(gate * jax.nn.sigmoid(gate) * _mm(hn, wu[d]), wd[d])
        return (h_ + y,)
    h3, = _rows("ffn", f_ffn, s, tm, [(h2, 'r0'), (pl_['norm_ffn_g'], 'f'), (wl['w_gate'], 'f'),
                                      (wl['w_up'], 'f'), (wl['w_down'], 'f')], [((s, D_MODEL), F32, 'r0')])
    sv.update(h=h, proj=proj, q=q, k=k, v=v, a_out=a_out, lse=lse, x_re=x_re, x_im=x_im, y_ssm=y_ssm,
              s_out=s_out, h1=h1, kvm=kvm, h2=h2, u_ch=u_ch)
    return h3, sv


def _layer_bwd(dh3, sv, memx, tabs, wl, pl_):
    s = dh3.shape[0]
    tm = min(256, s)
    cos, sin, pmat, pmat_t = tabs
    gr = {}
    act_shape = (N_DEV, s, FF_BLK)

    def f_ffn_bwd(h_, dy, g, wg, wu, wd):
        hn, r = _rms(h_, g[...])
        hb = hn.astype(MXU)
        dyb = dy.astype(MXU)
        dhn = jnp.zeros(h_.shape, F32)
        acts, dgs, dus = [], [], []
        for d in range(N_DEV):
            gate, up = _mm(hb, wg[d]), _mm(hb, wu[d])
            sg = jax.nn.sigmoid(gate)
            si = gate * sg
            dact = _mm_nt(dyb, wd[d])
            dgate = (dact * up * (sg * (1.0 + gate * (1.0 - sg)))).astype(MXU)
            dup = (dact * si).astype(MXU)
            dhn = dhn + _mm_nt(dgate, wg[d]) + _mm_nt(dup, wu[d])
            acts.append((si * up).astype(MXU))
            dgs.append(dgate)
            dus.append(dup)
        dh, dg = _rms_bwd(h_, g[...], r, dhn)
        return dy + dh, hb, jnp.stack(acts), jnp.stack(dgs), jnp.stack(dus), dg
    dh2, hn_f, act, dgate, dup, gr['norm_ffn_g'] = _rows(
        "ffn_bwd", f_ffn_bwd, s, tm,
        [(sv['h2'], 'r0'), (dh3, 'r0'), (pl_['norm_ffn_g'], 'f'), (wl['w_gate'], 'f'), (wl['w_up'], 'f'),
         (wl['w_down'], 'f')],
        [((s, D_MODEL), F32, 'r0'), ((s, D_MODEL), MXU, 'r0'), (act_shape, MXU, 'r1'), (act_shape, MXU, 'r1'),
         (act_shape, MXU, 'r1'), ((1, D_MODEL), F32, 'a')])
    gr['w_gate'] = _mm_tn_call("dw_gate", hn_f[None], dgate)
    gr['w_up'] = _mm_tn_call("dw_up", hn_f[None], dup)
    gr['w_down'] = _mm_tn_call("dw_down", act, dh3[None])

    m_len = memx.shape[0]

    def f_xattn_bwd(h_, dy, g, wq, kv_, wo):
        hn, r = _rms(h_, g[...])
        hb = hn.astype(MXU)
        dyb = dy.astype(MXU)
        dhn = jnp.zeros(h_.shape, F32)
        dqs, ohs, dks, dvs = [], [], [], []
        for hd in range(X_HEADS):
            cs = pl.ds(hd * X_HEAD_DIM, X_HEAD_DIM)
            kh, vh = kv_[hd], kv_[X_HEADS + hd]
            qh = _mm(hb, wq[:, cs])
            p = _softmax(_mm_nt(qh, kh) * X_SCALE)
            ohs.append(_mm(p, vh).astype(MXU))
            do = _mm_nt(dyb, wo[cs, :])
            dvs.append(_mm_tn(p, do))
            dp = _mm_nt(do, vh)
            ds = p * (dp - jnp.sum(dp * p, axis=-1, keepdims=True)) * X_SCALE
            dq = _mm(ds, kh).astype(MXU)
            dks.append(_mm_tn(ds, qh))
            dhn = dhn + _mm_nt(dq, wq[:, cs])
            dqs.append(dq)
        dh, dg = _rms_bwd(h_, g[...], r, dhn)
        return (dy + dh, hb, jnp.concatenate(dqs, axis=-1), jnp.concatenate(ohs, axis=-1),
                jnp.stack(dks + dvs), dg)
    dh1, hn_x, dq_x, oh_x, dkvm, gr['norm_x_g'] = _rows(
        "xattn_bwd", f_xattn_bwd, s, tm,
        [(sv['h1'], 'r0'), (dh2, 'r0'), (pl_['norm_x_g'], 'f'), (wl['w_xq'], 'f'), (sv['kvm'], 'f'),
         (wl['w_xo'], 'f')],
        [((s, D_MODEL), F32, 'r0'), ((s, D_MODEL), MXU, 'r0'), ((s, D_MODEL), MXU, 'r0'),
         ((s, D_MODEL), MXU, 'r0'), ((N_DEV, m_len, X_HEAD_DIM), F32, 'a'), ((1, D_MODEL), F32, 'a')])
    gr['w_xq'] = _mm_tn_call("dw_xq", hn_x[None], dq_x[None])[0]
    gr['w_xo'] = _mm_tn_call("dw_xo", oh_x[None], dh2[None])[0]

    def f_memkv_bwd(mm_, dkv, g, w):
        mn, r = _rms(mm_, g[...])
        mb = mn.astype(MXU)
        dmn = jnp.zeros(mm_.shape, F32)
        dws = []
        for d in range(N_DEV):
            dmn = dmn + _mm_nt(dkv[d], w[d])
            dws.append(_mm_tn(mb, dkv[d]))
        _, dg = _rms_bwd(mm_, g[...], r, dmn)
        return jnp.stack(dws), dg
    gr['w_xkv'], gr['mem_norm_g'] = _rows(
        "mem_kv_bwd", f_memkv_bwd, m_len, m_len,
        [(memx, 'r0'), (dkvm, 'r1'), (pl_['mem_norm_g'], 'f'), (wl['w_xkv'], 'f')],
        [((N_DEV, D_MODEL, X_HEAD_DIM), F32, 'a'), ((1, D_MODEL), F32, 'a')])

    def f_mix_out_bwd(a, so, dy, ga, gs, w):
        dmix = _mm_nt(dy, w[...])
        an, ra = _rms(a, ga[...])
        sn, rs = _rms(so, gs[...])
        da, dga = _rms_bwd(a, ga[...], ra, dmix[:, 0:512])
        dso, dgs = _rms_bwd(so, gs[...], rs, dmix[:, 512:1024])
        return da, dso, jnp.concatenate([an, sn], axis=-1), dga, dgs
    da_out, ds_out, mixed, gr['attn_out_g'], gr['ssm_out_g'] = _rows(
        "mix_out_bwd", f_mix_out_bwd, s, tm,
        [(sv['a_out'], 'r0'), (sv['s_out'], 'r0'), (dh1, 'r0'), (pl_['attn_out_g'], 'f'), (pl_['ssm_out_g'], 'f'),
         (wl['w_out'], 'f')],
        [((s, 512), F32, 'r0'), ((s, 512), F32, 'r0'), ((s, D_MODEL), MXU, 'r0'), ((1, 512), F32, 'a'),
         ((1, 512), F32, 'a')])
    gr['w_out'] = _mm_tn_call("dw_out", mixed[None], dh1[None])[0]

    def f_s5_out_bwd(xr, xi, u, y, ds, cre, cim, d, wglu, bglu):
        g, gelu_vjp = jax.vjp(jax.nn.gelu, y)
        sig = jax.nn.sigmoid(_mm(g, wglu[...]) + bglu[...])
        dz = ds * y * sig * (1.0 - sig)
        dy = ds * sig + gelu_vjp(_mm_nt(dz, wglu[...]))[0]
        dxr, dxi, dcr, dci = [], [], [], []
        for j in range(4):
            dyj = _lanes(dy, j, LANES)
            tr_, ti_ = _mm_nt(dyj, cre[j]), _mm_nt(dyj, cim[j])
            dxr += [_lanes(tr_, kk, LANES) for kk in range(4)]
            dxi += [_lanes(ti_, kk, LANES) for kk in range(4)]
            dcr.append(_mm_tn(_cat_blocks(xr, j), dyj))
            dci.append(_mm_tn(_cat_blocks(xi, j), dyj))
        return (jnp.stack(dxr), jnp.stack(dxi), dy * d[...], jnp.stack(dcr), jnp.stack(dci),
                jnp.sum(dy * u, axis=0, keepdims=True), _mm_tn(g, dz), jnp.sum(dz, axis=0, keepdims=True))
    xshape = (16, s, LANES)
    dx_re, dx_im, du_dir, gr['c_re'], gr['c_im'], gr['ssm_d'], gr['ssm_w_glu'], gr['ssm_b_glu'] = _rows(
        "s5_out_bwd", f_s5_out_bwd, s, tm,
        [(sv['x_re'], 'r1'), (sv['x_im'], 'r1'), (sv['u_ch'], 'r0'), (sv['y_ssm'], 'r0'), (_to_chunks(ds_out), 'r0'),
         (pl_['c_re'], 'f'), (pl_['c_im'], 'f'), (pl_['ssm_d'], 'f'), (wl['ssm_w_glu'], 'f'),
         (pl_['ssm_b_glu'], 'f')],
        [(xshape, F32, 'r1'), (xshape, F32, 'r1'), ((s, 512), F32, 'r0'), ((4, 512, LANES), F32, 'a'),
         ((4, 512, LANES), F32, 'a'), ((1, 512), F32, 'a'), ((512, 512), F32, 'a'), ((1, 512), F32, 'a')])
    g_re, g_im = _scan(dx_re, dx_im, pl_['a_re'], -pl_['a_im'], True)
    first_re = jnp.pad(sv['x_re'][:, s - 8:s - 1], ((0, 0), (1, 0), (0, 0)))
    first_im = jnp.pad(sv['x_im'][:, s - 8:s - 1], ((0, 0), (1, 0), (0, 0)))

    def f_s5_in_bwd(gre, gim, xr, xi, pr8, pi8, u, dud, f8r, f8i, bre, bim):
        first = pl.program_id(0) == 0
        xpr = jnp.concatenate([jnp.where(first, f8r[...], pr8), xr[:, :tm - 8]], axis=1)
        xpi = jnp.concatenate([jnp.where(first, f8i[...], pi8), xi[:, :tm - 8]], axis=1)
        dus, dbr, dbi = [], [], []
        for j in range(4):
            gj_r, gj_i, uj = _cat_blocks(gre, j), _cat_blocks(gim, j), _lanes(u, j, LANES)
            dus.append(_mm_nt(gj_r, bre[j]) + _mm_nt(gj_i, bim[j]))
            dbr.append(_mm_tn(uj, gj_r))
            dbi.append(_mm_tn(uj, gj_i))
        da_r = jnp.sum(gre * xpr + gim * xpi, axis=1, keepdims=True)
        da_i = jnp.sum(gim * xpr - gre * xpi, axis=1, keepdims=True)
        return dud + jnp.concatenate(dus, axis=-1), jnp.stack(dbr), jnp.stack(dbi), da_r, da_i
    du_ch, gr['b_re'], gr['b_im'], gr['a_re'], gr['a_im'] = _rows(
        "s5_in_bwd", f_s5_in_bwd, s, tm,
        [(g_re, 'r1'), (g_im, 'r1'), (sv['x_re'], 'r1'), (sv['x_im'], 'r1'), (sv['x_re'], 'p8'), (sv['x_im'], 'p8'),
         (sv['u_ch'], 'r0'), (du_dir, 'r0'), (first_re, 'f'), (first_im, 'f'), (pl_['b_re'], 'f'), (pl_['b_im'], 'f')],
        [((s, 512), F32, 'r0'), ((4, LANES, 512), F32, 'a'), ((4, LANES, 512), F32, 'a'),
         ((16, 1, LANES), F32, 'a'), ((16, 1, LANES), F32, 'a')])
    du = _from_chunks(du_ch)

    dq, dk, dv = _flash_bwd(sv['q'], sv['k'], sv['v'], sv['a_out'], sv['lse'], da_out)

    def f_qkv_bwd(pr, cos_, sin_, dq_, dk_, dv_, gq, gkv, wq, wk, wv, pt):
        cq, ckv = pr[:, 0:Q_LORA], pr[:, Q_LORA:Q_LORA + KV_LORA]
        cqn, rq = _rms(cq, gq[...])
        kvn, rkv = _rms(ckv, gkv[...])
        cqb, kvb = cqn.astype(MXU), kvn.astype(MXU)
        dcqn = jnp.zeros(cq.shape, F32)
        dkvn = jnp.zeros(ckv.shape, F32)
        dksum = jnp.zeros(dk_[0].shape, F32)
        dwq, dwk, dwv = [], [], []
        for hd in range(MLA_HEADS):
            dqp = (_rope_t(dq_[hd], cos_, sin_, pt) * MLA_SCALE).astype(MXU)
            dkb, dvb = dk_[hd].astype(MXU), dv_[hd].astype(MXU)
            dwq.append(_mm_tn(cqb, dqp))
            dwk.append(_mm_tn(kvb, dkb))
            dwv.append(_mm_tn(kvb, dvb))
            dcqn = dcqn + _mm_nt(dqp, wq[hd])
            dkvn = dkvn + _mm_nt(dkb, wk[hd]) + _mm_nt(dvb, wv[hd])
            dksum = dksum + dk_[hd]
        dcq, dgq = _rms_bwd(cq, gq[...], rq, dcqn)
        dckv, dgkv = _rms_bwd(ckv, gkv[...], rkv, dkvn)
        dpa = jnp.concatenate([dcq, dckv, _rope_t(dksum, cos_, sin_, pt)], axis=-1)
        return dpa, jnp.stack(dwq), jnp.stack(dwk), jnp.stack(dwv), dgq, dgkv
    dpa, gr['w_uq'], gr['w_k'], gr['w_v'], gr['q_norm_g'], gr['kv_norm_g'] = _rows(
        "mla_qkv_bwd", f_qkv_bwd, s, tm,
        [(sv['proj'], 'r0'), (cos, 'r0'), (sin, 'r0'), (dq, 'r1'), (dk, 'r1'), (dv, 'r1'), (pl_['q_norm_g'], 'f'),
         (pl_['kv_norm_g'], 'f'), (wl['w_uq'], 'f'), (wl['w_k'], 'f'), (wl['w_v'], 'f'), (pmat_t, 'f')],
        [((s, 512), F32, 'r0'), ((MLA_HEADS, Q_LORA, HEAD_W), F32, 'a'), ((MLA_HEADS, KV_LORA, HEAD_W), F32, 'a'),
         ((MLA_HEADS, KV_LORA, HEAD_W), F32, 'a'), ((1, Q_LORA), F32, 'a'), ((1, KV_LORA), F32, 'a')])

    def f_mix_in_bwd(h_, dpa_, du_, dres, g, w):
        dproj = jnp.concatenate([dpa_, du_], axis=-1).astype(MXU)
        xn, r = _rms(h_, g[...])
        dh, dg = _rms_bwd(h_, g[...], r, _mm_nt(dproj, w[...]))
        return dres + dh, xn, dproj, dg
    dh0, xn, dproj, gr['norm_mix_g'] = _rows(
        "mix_in_bwd", f_mix_in_bwd, s, tm,
        [(sv['h'], 'r0'), (dpa, 'r0'), (du, 'r0'), (dh1, 'r0'), (pl_['norm_mix_g'], 'f'), (wl['w_in'], 'f')],
        [((s, D_MODEL), F32, 'r0'), ((s, D_MODEL), MXU, 'r0'), ((s, D_MODEL), MXU, 'r0'), ((1, D_MODEL), F32, 'a')])
    gr['w_in'] = _mm_tn_call("dw_in", xn[None], dproj[None])[0]
    return dh0, gr


def _layer_weights(w):
    w_in = w['w_in'].reshape(D_MODEL, -1)
    z = lambda n: jnp.zeros((D_MODEL, n), w_in.dtype)
    wl = {'w_in': jnp.concatenate([w_in[:, :384], z(64), w_in[:, 384:416], z(32), w_in[:, 416:]], axis=1)}
    wl['w_uq'] = jnp.pad(w['w_uq'], ((0, 0), (0, 0), (0, HEAD_W - QK_NOPE - QK_ROPE)))
    wl['w_k'] = jnp.pad(w['w_ukv'][..., :QK_NOPE], ((0, 0), (0, 0), (0, HEAD_W - QK_NOPE)))
    wv = w['w_ukv'][..., QK_NOPE:]
    even = (jnp.arange(MLA_HEADS) % 2 == 0)[:, None, None]
    wl['w_v'] = jnp.concatenate([jnp.where(even, wv, 0), jnp.where(even, 0, wv)], axis=-1).astype(wv.dtype)
    wl['ssm_w_glu'] = w['ssm_w_glu'].reshape(SSM_WIDTH, SSM_WIDTH)
    for n in ('w_out', 'w_xq', 'w_xo'):
        wl[n] = w[n].reshape(D_MODEL, D_MODEL)
    for n in ('w_xkv', 'w_gate', 'w_up', 'w_down'):
        wl[n] = w[n]
    return wl


def _blocked_grads(gr):
    d = gr['w_in']
    out = {'w_in': jnp.concatenate([d[:, :384], d[:, 448:480], d[:, 512:]], axis=1).reshape(N_DEV, 128, -1)}
    out['w_uq'] = gr['w_uq'][..., :QK_NOPE + QK_ROPE]
    even = (jnp.arange(MLA_HEADS) % 2 == 0)[:, None, None]
    dv = gr['w_v']
    out['w_ukv'] = jnp.concatenate([gr['w_k'][..., :QK_NOPE], jnp.where(even, dv[..., :V_DIM], dv[..., V_DIM:])], axis=-1)
    out['ssm_w_glu'] = gr['ssm_w_glu'].reshape(N_DEV, SSM_WIDTH // N_DEV, SSM_WIDTH)
    for n in ('w_out', 'w_xq', 'w_xo'):
        out[n] = gr[n].reshape(N_DEV, D_MODEL // N_DEV, D_MODEL)
    for n in ('w_xkv', 'w_gate', 'w_up', 'w_down'):
        out[n] = gr[n]
    return [out[n].astype(MXU) for n in SHARDED]


def kernel(x, mem, positions, norm_mix_g, w_in, q_norm_g, w_uq, kv_norm_g, w_ukv, ssm_lambda_re, ssm_lambda_im, ssm_log_dt, ssm_b_re, ssm_b_im, ssm_c_re, ssm_c_im, ssm_d, ssm_w_glu, ssm_b_glu, attn_out_g, ssm_out_g, w_out, norm_x_g, mem_norm_g, w_xq, w_xkv, w_xo, norm_ffn_g, w_gate, w_up, w_down, final_norm_g, loss_target, m_norm_mix_g, m_w_in, m_q_norm_g, m_w_uq, m_kv_norm_g, m_w_ukv, m_ssm_lambda_re, m_ssm_lambda_im, m_ssm_log_dt, m_ssm_b_re, m_ssm_b_im, m_ssm_c_re, m_ssm_c_im, m_ssm_d, m_ssm_w_glu, m_ssm_b_glu, m_attn_out_g, m_ssm_out_g, m_w_out, m_norm_x_g, m_mem_norm_g, m_w_xq, m_w_xkv, m_w_xo, m_norm_ffn_g, m_w_gate, m_w_up, m_w_down, m_final_norm_g, v_norm_mix_g, v_w_in, v_q_norm_g, v_w_uq, v_kv_norm_g, v_w_ukv, v_ssm_lambda_re, v_ssm_lambda_im, v_ssm_log_dt, v_ssm_b_re, v_ssm_b_im, v_ssm_c_re, v_ssm_c_im, v_ssm_d, v_ssm_w_glu, v_ssm_b_glu, v_attn_out_g, v_ssm_out_g, v_w_out, v_norm_x_g, v_mem_norm_g, v_w_xq, v_w_xkv, v_w_xo, v_norm_ffn_g, v_w_gate, v_w_up, v_w_down, v_final_norm_g):
    args = dict(locals())
    W = {n: args[n] for n in WEIGHTS}
    M = {n: args['m_' + n] for n in WEIGHTS}
    V = {n: args['v_' + n] for n in WEIGHTS}
    s = x.shape[1]
    h = x[0]
    memx = mem[0]

    freqs = ROPE_THETA ** (-jnp.arange(0, QK_ROPE, 2, dtype=F32) / QK_ROPE)
    ang = positions[0].astype(F32)[:, None] * freqs
    c16, s16 = jnp.cos(ang), jnp.sin(ang)
    cos = jnp.concatenate([jnp.ones((s, QK_NOPE), F32), c16, c16, jnp.zeros((s, 32), F32)], axis=1)
    sin = jnp.concatenate([jnp.zeros((s, QK_NOPE), F32), s16, s16, jnp.zeros((s, 32), F32)], axis=1)
    idx = jnp.arange(QK_ROPE // 2)
    pmat = jnp.zeros((HEAD_W, HEAD_W), F32)
    pmat = pmat.at[QK_NOPE + 16 + idx, QK_NOPE + idx].set(-1.0).at[QK_NOPE + idx, QK_NOPE + 16 + idx].set(1.0)
    tabs = (cos, sin, pmat, pmat.T)

    ns = len(SHARDED)
    gathered = _all_gather("gather_weights", [W[n][l].astype(MXU) for l in range(DEPTH) for n in SHARDED])

    layers = []
    for l in range(DEPTH):
        wl = _layer_weights(dict(zip(SHARDED, gathered[l * ns:(l + 1) * ns])))
        s5_in = [W[n][l] for n in ('ssm_lambda_re', 'ssm_lambda_im', 'ssm_log_dt', 'ssm_b_re', 'ssm_b_im',
                                   'ssm_c_re', 'ssm_c_im')]
        (a_re, a_im, bre, bim, cre, cim), s5_vjp = jax.vjp(_s5_params, *s5_in)
        pl_ = {n: W[n][l][None] for n in ('norm_mix_g', 'q_norm_g', 'kv_norm_g', 'ssm_d', 'ssm_b_glu',
                                           'attn_out_g', 'ssm_out_g', 'norm_x_g', 'mem_norm_g', 'norm_ffn_g')}
        pl_.update(a_re=a_re, a_im=a_im, b_re=bre, b_im=bim, c_re=cre, c_im=cim)
        h, sv = _layer_fwd(h, memx, tabs, wl, pl_)
        layers.append((wl, pl_, sv, s5_vjp))

    def f_loss(h_, tgt, g):
        y, r = _rms(h_, g[...])
        err = y - tgt
        part = 0.5 * jnp.sum(jnp.mean(err * err, axis=-1, keepdims=True), axis=0, keepdims=True)
        dh, dg = _rms_bwd(h_, g[...], r, err / D_MODEL)
        return dh, dg, jnp.broadcast_to(part, (8, LANES))
    dh, g_final, loss_part = _rows(
        "loss_head", f_loss, s, min(256, s), [(h, 'r0'), (loss_target[0], 'r0'), (final_norm_g[None], 'f')],
        [((s, D_MODEL), F32, 'r0'), ((1, D_MODEL), F32, 'a'), ((8, LANES), F32, 'a')])
    loss = lax.psum(loss_part[0, 0], ("x", "y", "c"))

    g_sh = [None] * DEPTH
    g_rep = [None] * DEPTH
    for l in reversed(range(DEPTH)):
        wl, pl_, sv, s5_vjp = layers[l]
        dh, gr = _layer_bwd(dh, sv, memx, tabs, wl, pl_)
        g_sh[l] = _blocked_grads(gr)
        ds5 = s5_vjp((gr['a_re'], gr['a_im'], gr['b_re'], gr['b_im'], gr['c_re'], gr['c_im']))
        rep = dict(zip(('ssm_lambda_re', 'ssm_lambda_im', 'ssm_log_dt', 'ssm_b_re', 'ssm_b_im', 'ssm_c_re',
                        'ssm_c_im'), ds5))
        for n in ('norm_mix_g', 'q_norm_g', 'kv_norm_g', 'ssm_d', 'ssm_b_glu', 'attn_out_g', 'ssm_out_g',
                  'norm_x_g', 'mem_norm_g', 'norm_ffn_g'):
            rep[n] = gr[n][0]
        g_rep[l] = rep
    grad_x = dh[None]

    parts = _all_to_all("scatter_grads", g_sh)
    out_sh = [{}, {}, {}, {}]
    for n, p in zip(SHARDED, parts):
        for kind, r in enumerate(_adamw_weight("adamw_" + n, p, W[n], M[n], V[n])):
            out_sh[kind][n] = r

    tr_rp = 256
    rep_names = REPL_L + ['final_norm_g']
    g_loc = {n: jnp.stack([g_rep[l][n] for l in range(DEPTH)]) for n in REPL_L}
    g_loc['final_norm_g'] = g_final[0]
    pk_r = lambda d: _pack([d[n].reshape(1, -1) for n in rep_names], tr_rp)[0]
    parts_r, = _all_gather("gather_small_grads", [pk_r(g_loc)])
    res_rp = _adamw("adamw_replicated", parts_r, pk_r(W), pk_r(M), pk_r(V), tr_rp)
    shapes_rp = [W[n].shape for n in rep_names]
    out_rp = [dict(zip(rep_names, _unpack(r, shapes_rp))) for r in res_rp]

    outs = [loss, grad_x]
    for kind in range(4):
        for n in WEIGHTS:
            outs.append(out_sh[kind][n] if n in SHARDED else out_rp[kind][n])
    return tuple(outs)
```

```python
from typing import Callable, NamedTuple

import jax
import jax.numpy as jnp
from jax import lax
from jax.experimental import pallas as pl
from jax.experimental.pallas import tpu as pltpu

F32 = jnp.float32
MXU = jnp.bfloat16
HI = lax.Precision.HIGHEST

D_MODEL = 1024
MLA_HEADS = 8
QK_NOPE = 64
QK_ROPE = 32
V_DIM = 64
Q_LORA = 256
KV_LORA = 128
SSM_WIDTH = 512
SSM_GROUPS = 32
SSM_GROUP = 16
SSM_STATE = 64
X_HEADS = 4
X_HEAD_DIM = 256
D_FF = 2816
FF_BLK = D_FF // 8
ROPE_THETA = 10000.0
EPS = 1e-6
DEPTH = 2
N_DEV = 8
LANES = 128
HEAD_W = 128
MLA_SCALE = (QK_NOPE + QK_ROPE) ** -0.5
X_SCALE = X_HEAD_DIM ** -0.5
ADAM_LR, ADAM_B1, ADAM_B2, ADAM_EPS, ADAM_WD, ADAM_STEP = 0.001, 0.9, 0.999, 1e-08, 0.01, 10
VMEM_LIMIT = 56 * 1024 * 1024
FLASH_TILE = 512
MESH = pl.DeviceIdType.MESH

SHARDED = ['w_in', 'w_uq', 'w_ukv', 'ssm_w_glu', 'w_out', 'w_xq', 'w_xkv', 'w_xo', 'w_gate', 'w_up', 'w_down']
REPL_L = ['norm_mix_g', 'q_norm_g', 'kv_norm_g', 'ssm_lambda_re', 'ssm_lambda_im', 'ssm_log_dt', 'ssm_b_re',
          'ssm_b_im', 'ssm_c_re', 'ssm_c_im', 'ssm_d', 'ssm_b_glu', 'attn_out_g', 'ssm_out_g', 'norm_x_g',
          'mem_norm_g', 'norm_ffn_g']
WEIGHTS = ['norm_mix_g', 'w_in', 'q_norm_g', 'w_uq', 'kv_norm_g', 'w_ukv', 'ssm_lambda_re', 'ssm_lambda_im',
           'ssm_log_dt', 'ssm_b_re', 'ssm_b_im', 'ssm_c_re', 'ssm_c_im', 'ssm_d', 'ssm_w_glu', 'ssm_b_glu',
           'attn_out_g', 'ssm_out_g', 'w_out', 'norm_x_g', 'mem_norm_g', 'w_xq', 'w_xkv', 'w_xo', 'norm_ffn_g',
           'w_gate', 'w_up', 'w_down', 'final_norm_g']


def _pcall(body, **kw):
    return pl.pallas_call(body, **kw)


def _mm(a, b):
    return jnp.dot(a.astype(MXU), b.astype(MXU), preferred_element_type=F32)


def _mm_nt(a, b):
    return lax.dot_general(a.astype(MXU), b.astype(MXU), (((1,), (1,)), ((), ())), preferred_element_type=F32)


def _mm_tn(a, b):
    return lax.dot_general(a.astype(MXU), b.astype(MXU), (((0,), (0,)), ((), ())), preferred_element_type=F32)


def _mm_hi(a, b):
    return jnp.dot(a.astype(F32), b.astype(F32), precision=HI, preferred_element_type=F32)


def _rms(x, g):
    r = lax.rsqrt(jnp.mean(x * x, axis=-1, keepdims=True) + EPS)
    return x * r * g, r


def _rms_bwd(x, g, r, dy):
    dyg = dy * g
    dx = r * dyg - x * (r * r * r) * jnp.mean(dyg * x, axis=-1, keepdims=True)
    return dx, jnp.sum(dy * x * r, axis=0, keepdims=True)


def _rope(x, cos, sin, p_ref):
    return x * cos + _mm_hi(x, p_ref[...]) * sin


def _rope_t(g, cos, sin, pt_ref):
    return g * cos + _mm_hi(g * sin, pt_ref[...])


def _softmax(s):
    m = jnp.max(s, axis=-1, keepdims=True)
    e = jnp.exp(s - m)
    return e / jnp.sum(e, axis=-1, keepdims=True)


def _lanes(x, j, w):
    return x[:, j * w:(j + 1) * w]


def _rows(name, fn, n, tm, ins, outs):
    def spec(shape, kind):
        nd = len(shape)
        if kind == 'p8':
            return pl.BlockSpec((shape[0], 8, shape[2]), lambda i: (0, jnp.maximum(i * (tm // 8) - 1, 0), 0))
        if kind == 'f':
            return pl.BlockSpec(shape, lambda i, _nd=nd: (0,) * _nd, pipeline_mode=pl.Buffered(1))
        if kind == 'a':
            return pl.BlockSpec(shape, lambda i, _nd=nd: (0,) * _nd)
        ax = int(kind[1])
        blk = tuple(tm if d == ax else s for d, s in enumerate(shape))
        return pl.BlockSpec(blk, lambda i, _ax=ax, _nd=nd: tuple(i if d == _ax else 0 for d in range(_nd)))

    n_in = len(ins)

    def body(*refs):
        args = [r if k == 'f' else r[...] for r, (_, k) in zip(refs[:n_in], ins)]
        res = fn(*args)
        i = pl.program_id(0)
        for r, (_, dt, k), v in zip(refs[n_in:], outs, res):
            if k == 'a':
                _accumulate(r, v.astype(dt), i)
            else:
                r[...] = v.astype(dt)

    return _pcall(
        body, name=name, grid=(n // tm,),
        in_specs=[spec(a.shape, k) for a, k in ins],
        out_specs=[spec(s, k) for s, _, k in outs],
        out_shape=[jax.ShapeDtypeStruct(s, dt) for s, dt, _ in outs],
        compiler_params=pltpu.CompilerParams(dimension_semantics=("arbitrary",), vmem_limit_bytes=VMEM_LIMIT),
    )(*[a for a, _ in ins])


def _accumulate(ref, v, i):
    @pl.when(i == 0)
    def _():
        ref[...] = v

    @pl.when(i != 0)
    def _():
        ref[...] += v


def _mm_tn_call(name, a, b):
    out_dtype = MXU
    ba, s, k = a.shape
    bb, _, n = b.shape
    nb = max(ba, bb)
    ts = min(512, s)
    ns = s // ts

    def body(a_ref, b_ref, o_ref, acc_ref):
        j = pl.program_id(1)
        _accumulate(acc_ref, _mm_tn(a_ref[...], b_ref[...]), j)

        @pl.when(j == ns - 1)
        def _():
            o_ref[...] = acc_ref[...].astype(out_dtype)

    return _pcall(
        body, name=name, grid=(nb, ns),
        in_specs=[pl.BlockSpec((None, ts, k), (lambda i, j: (i, j, 0)) if ba > 1 else (lambda i, j: (0, j, 0))),
                  pl.BlockSpec((None, ts, n), (lambda i, j: (i, j, 0)) if bb > 1 else (lambda i, j: (0, j, 0)))],
        out_specs=pl.BlockSpec((None, k, n), lambda i, j: (i, 0, 0)),
        out_shape=jax.ShapeDtypeStruct((nb, k, n), out_dtype),
        scratch_shapes=[pltpu.VMEM((k, n), F32)],
        compiler_params=pltpu.CompilerParams(dimension_semantics=("arbitrary", "arbitrary"),
                                             vmem_limit_bytes=VMEM_LIMIT),
    )(a, b)


def _side_split(refs, n_in, n_out, side):
    if side is None:
        return refs[:n_in], refs[n_in:n_in + n_out], None
    si, so = len(side.ins), len(side.out_shapes)
    own_in, side_in = refs[:n_in], refs[n_in:n_in + si]
    own_out, side_out = refs[n_in + si:n_in + si + n_out], refs[n_in + si + n_out:n_in + si + n_out + so]
    return own_in, own_out, side.steps(side_in, side_out, refs[n_in + si + n_out + so:])


def _side_args(side):
    if side is None:
        return [], [], [], [], []
    any_spec = pl.BlockSpec(memory_space=pl.ANY)
    return ([any_spec] * len(side.ins), [any_spec] * len(side.out_shapes), list(side.out_shapes),
            list(side.sem_shapes), list(side.ins))


def _flash_fwd(q, k, v, side=None):
    nh, s, w = q.shape
    t = min(FLASH_TILE, s)
    nq = s // t
    n_steps = (nh // 2) * nq

    def body(*refs):
        (q_ref, k_ref, v_ref), (o_ref, lse_ref), steps = _side_split(refs, 3, 2, side)
        step = pl.program_id(0) * nq + pl.program_id(1)
        if steps:
            pl.when(step == 0)(steps[0])
            pl.when(step == n_steps // 2)(steps[1])
        qi = pl.program_id(1)
        qs = [q_ref[0], q_ref[1]]
        below = lax.broadcasted_iota(jnp.int32, (t, t), 1) <= lax.broadcasted_iota(jnp.int32, (t, t), 0)

        def tile(j, carry, diagonal):
            sl = pl.ds(pl.multiple_of(j * t, t), t)
            out = []
            for hh in range(2):
                m, l, acc = carry[3 * hh:3 * hh + 3]
                sc = _mm_nt(qs[hh], k_ref[hh, sl, :])
                if diagonal:
                    sc = jnp.where(below, sc, -1e30)
                m_new = jnp.maximum(m, jnp.max(sc, axis=1, keepdims=True))
                p = jnp.exp(sc - m_new)
                alpha = jnp.exp(m - m_new)
                out += [m_new, alpha * l + jnp.sum(p, axis=1, keepdims=True), alpha * acc + _mm(p, v_ref[hh, sl, :])]
            return tuple(out)

        init = (jnp.full((t, 1), -1e30, F32), jnp.zeros((t, 1), F32), jnp.zeros((t, w), F32)) * 2
        carry = lax.fori_loop(0, qi, lambda j, c: tile(j, c, False), init)
        carry = tile(qi, carry, True)
        o_ref[...] = carry[2] / carry[1] + carry[5] / carry[4]
        for hh in range(2):
            lse_ref[hh] = jnp.broadcast_to(carry[3 * hh] + jnp.log(carry[3 * hh + 1]), (t, w))
        if steps:
            pl.when(step == n_steps - 1)(steps[2])

    s_in, s_out, s_shape, s_sems, s_ops = _side_args(side)
    res = _pcall(
        body, name="mla_flash_fwd" + ("_x" if side else ""), grid=(nh // 2, nq),
        in_specs=[pl.BlockSpec((2, t, w), lambda p, i: (p, i, 0)),
                  pl.BlockSpec((2, s, w), lambda p, i: (p, 0, 0)),
                  pl.BlockSpec((2, s, w), lambda p, i: (p, 0, 0))] + s_in,
        out_specs=[pl.BlockSpec((t, w), lambda p, i: (i, p)),
                   pl.BlockSpec((2, t, w), lambda p, i: (p, i, 0))] + s_out,
        out_shape=[jax.ShapeDtypeStruct((s, (nh // 2) * w), F32), jax.ShapeDtypeStruct((nh, s, w), F32)] + s_shape,
        scratch_shapes=s_sems,
        compiler_params=pltpu.CompilerParams(dimension_semantics=("arbitrary", "arbitrary"),
                                             vmem_limit_bytes=VMEM_LIMIT),
    )(q, k, v, *s_ops)
    return res[0], res[1], res[2:]


def _flash_bwd(q, k, v, o, lse, do, side=None):
    nh, s, w = q.shape
    t = min(FLASH_TILE, s)
    nq = s // t
    n_steps = (nh // 2) * nq

    def body(*refs):
        (q_ref, k_ref, v_ref, o_ref, lse_ref, do_ref), (dq_ref, dk_ref, dv_ref), steps = _side_split(refs, 6, 3, side)
        step = pl.program_id(0) * nq + pl.program_id(1)
        if steps:
            pl.when(step == 0)(steps[0])
            pl.when(step == n_steps // 2)(steps[1])
        j = pl.program_id(1)

        @pl.when(j == 0)
        def _():
            dq_ref[...] = jnp.zeros(dq_ref.shape, F32)

        below = lax.broadcasted_iota(jnp.int32, (t, t), 1) <= lax.broadcasted_iota(jnp.int32, (t, t), 0)
        lane = lax.broadcasted_iota(jnp.int32, (t, w), 1)
        heads = [jnp.logical_and(lane >= hh * V_DIM, lane < (hh + 1) * V_DIM) for hh in range(2)]
        ks = [k_ref[0], k_ref[1]]
        vs = [v_ref[0], v_ref[1]]

        def tile(i, carry, diagonal):
            sl = pl.ds(pl.multiple_of(i * t, t), t)
            dout_all, o_all = do_ref[sl, :], o_ref[sl, :]
            out = []
            for hh in range(2):
                dk, dv = carry[2 * hh], carry[2 * hh + 1]
                qh = q_ref[hh, sl, :]
                dout = jnp.where(heads[hh], dout_all, 0.0)
                sc = _mm_nt(qh, ks[hh])
                if diagonal:
                    sc = jnp.where(below, sc, -1e30)
                p = jnp.exp(sc - lse_ref[hh, sl, 0:1])
                dp = _mm_nt(dout, vs[hh])
                ds = p * (dp - jnp.sum(dout * o_all, axis=1, keepdims=True))
                dq_ref[hh, sl, :] += _mm(ds, ks[hh])
                out += [dk + _mm_tn(ds, qh), dv + _mm_tn(p, dout)]
            return tuple(out)

        carry = tile(j, (jnp.zeros((t, w), F32),) * 4, True)
        carry = lax.fori_loop(j + 1, nq, lambda i, c: tile(i, c, False), carry)
        for hh in range(2):
            dk_ref[hh] = carry[2 * hh]
            dv_ref[hh] = jnp.where(heads[hh], carry[2 * hh + 1], 0.0)
        if steps:
            pl.when(step == n_steps - 1)(steps[2])

    s_in, s_out, s_shape, s_sems, s_ops = _side_args(side)
    res = _pcall(
        body, name="mla_flash_bwd" + ("_x" if side else ""), grid=(nh // 2, nq),
        in_specs=[pl.BlockSpec((2, s, w), lambda p, j: (p, 0, 0)),
                  pl.BlockSpec((2, t, w), lambda p, j: (p, j, 0)),
                  pl.BlockSpec((2, t, w), lambda p, j: (p, j, 0)),
                  pl.BlockSpec((s, w), lambda p, j: (0, p)),
                  pl.BlockSpec((2, s, w), lambda p, j: (p, 0, 0)),
                  pl.BlockSpec((s, w), lambda p, j: (0, p))] + s_in,
        out_specs=[pl.BlockSpec((2, s, w), lambda p, j: (p, 0, 0)),
                   pl.BlockSpec((2, t, w), lambda p, j: (p, j, 0)),
                   pl.BlockSpec((2, t, w), lambda p, j: (p, j, 0))] + s_out,
        out_shape=[jax.ShapeDtypeStruct((nh, s, w), F32)] * 3 + s_shape,
        scratch_shapes=s_sems,
        compiler_params=pltpu.CompilerParams(dimension_semantics=("arbitrary", "arbitrary"),
                                             vmem_limit_bytes=VMEM_LIMIT),
    )(q, k, v, o, lse, do, *s_ops)
    return res[0], res[1], res[2], res[3:]


def _scan(b_re, b_im, a_re, a_im, reverse):
    nb, s, w = b_re.shape
    ch = s // 8
    assert ch & (ch - 1) == 0
    grp = 2

    def cmul(ar, ai, xr, xi):
        return ar * xr - ai * xi, ar * xi + ai * xr

    def body(br_ref, bi_ref, ar_ref, ai_ref, xr_ref, xi_ref):
        sub = lax.broadcasted_iota(jnp.int32, (8, w), 0)

        def shift(x, k):
            if reverse:
                return jnp.where(sub < 8 - k, pltpu.roll(x, 8 - k, 0), 0.0)
            return jnp.where(sub >= k, pltpu.roll(x, k, 0), 0.0)

        ar = [jnp.broadcast_to(ar_ref[g], (8, w)) for g in range(grp)]
        ai = [jnp.broadcast_to(ai_ref[g], (8, w)) for g in range(grp)]

        def tsl(i):
            return pl.ds(pl.multiple_of(((ch - 1 - i) if reverse else i) * 8, 8), 8)

        def local(i, carry):
            out = []
            for g in range(grp):
                xr, xi = carry[2 * g], carry[2 * g + 1]
                pr, pi = cmul(ar[g], ai[g], xr, xi)
                nr = pr + br_ref[g, tsl(i), :]
                ni = pi + bi_ref[g, tsl(i), :]
                xr_ref[g, tsl(i), :] = nr
                xi_ref[g, tsl(i), :] = ni
                out += [nr, ni]
            return tuple(out)

        fin = lax.fori_loop(0, ch, local, (jnp.zeros((8, w), F32),) * (2 * grp))

        carry_in = []
        for g in range(grp):
            pr, pi = ar[g], ai[g]
            for _ in range(ch.bit_length() - 1):
                pr, pi = cmul(pr, pi, pr, pi)
            fr, fi = fin[2 * g], fin[2 * g + 1]
            for kk in (1, 2, 4):
                sr, si = cmul(pr, pi, shift(fr, kk), shift(fi, kk))
                fr, fi = fr + sr, fi + si
                pr, pi = cmul(pr, pi, pr, pi)
            carry_in += [shift(fr, 1), shift(fi, 1)]

        def fix(i, pw):
            out = []
            for g in range(grp):
                pr, pi = pw[2 * g], pw[2 * g + 1]
                cr, ci = cmul(pr, pi, carry_in[2 * g], carry_in[2 * g + 1])
                xr_ref[g, tsl(i), :] = xr_ref[g, tsl(i), :] + cr
                xi_ref[g, tsl(i), :] = xi_ref[g, tsl(i), :] + ci
                nr, ni = cmul(pr, pi, ar[g], ai[g])
                out += [nr, ni]
            return tuple(out)

        lax.fori_loop(0, ch, fix, tuple(x for g in range(grp) for x in (ar[g], ai[g])))

    blk = pl.BlockSpec((grp, s, w), lambda i: (i, 0, 0))
    ablk = pl.BlockSpec((grp, 1, w), lambda i: (i, 0, 0))
    return _pcall(
        body, name="s5_scan_rev" if reverse else "s5_scan", grid=(nb // grp,),
        in_specs=[blk, blk, ablk, ablk], out_specs=[blk, blk],
        out_shape=[jax.ShapeDtypeStruct((nb, s, w), F32)] * 2,
        compiler_params=pltpu.CompilerParams(dimension_semantics=("arbitrary",), vmem_limit_bytes=VMEM_LIMIT),
    )(b_re, b_im, a_re, a_im)


class _Exchange(NamedTuple):
    ins: list
    out_shapes: list
    sem_shapes: list
    steps: Callable


def _gather_steps(ins, outs, sems):
    n = len(ins)
    send_sems, recv_sems, local_sems = sems
    x, y, c = lax.axis_index("x"), lax.axis_index("y"), lax.axis_index("c")
    me, sibling = (x, y, c), (x, y, 1 - c)
    chips = [(1 - x, y), (x, 1 - y), (1 - x, 1 - y)]

    def copy(a, k, block, to, src=None):
        dst = outs[a].at[4 * block[0] + 2 * block[1] + block[2]]
        return pltpu.make_async_remote_copy(
            src_ref=dst if src is None else src, dst_ref=dst,
            send_sem=send_sems.at[a, k], recv_sem=recv_sems.at[a, k], device_id=to, device_id_type=MESH)

    mine = [pltpu.make_async_copy(ins[a], outs[a].at[4 * x + 2 * y + c], local_sems.at[a]) for a in range(n)]
    first = []
    for a in range(n):
        first.append(copy(a, 0, me, sibling, src=ins[a]))
        first += [copy(a, 1 + j, me, (*chip, c), src=ins[a]) for j, chip in enumerate(chips)]
    passed = [copy(a, 4 + j, (*chip, c), sibling) for j, chip in enumerate(chips) for a in range(n)]

    def start():
        for cp in mine + first:
            cp.start()

    def pass_on():
        i = 0
        for j, chip in enumerate(chips):
            for a in range(n):
                copy(a, 1 + j, (*chip, c), me).wait_recv()
                passed[i].start()
                i += 1

    def finish():
        for a in range(n):
            copy(a, 0, sibling, me).wait_recv()
            for j, chip in enumerate(chips):
                copy(a, 4 + j, (*chip, 1 - c), me).wait_recv()
        for cp in first + passed:
            cp.wait_send()
        for cp in mine:
            cp.wait()

    return start, pass_on, finish


def _gather(arrs):
    n = len(arrs)
    return _Exchange(list(arrs), [jax.ShapeDtypeStruct((N_DEV,) + a.shape, a.dtype) for a in arrs],
                     [pltpu.SemaphoreType.DMA((n, 7)), pltpu.SemaphoreType.DMA((n, 7)), pltpu.SemaphoreType.DMA((n,))],
                     _gather_steps)


def _scatter_steps(ins, outs, sems):
    n = len(ins)
    send_sems, recv_sems, local_sems = sems
    x, y, c = lax.axis_index("x"), lax.axis_index("y"), lax.axis_index("c")
    me = 4 * x + 2 * y + c
    own, sent, arrivals = [], [], []
    for a in range(n):
        own.append(pltpu.make_async_copy(ins[a].at[me], outs[a].at[me], local_sems.at[a]))
        for k in range(1, N_DEV):
            px, py, pc = x ^ ((k >> 2) & 1), y ^ ((k >> 1) & 1), c ^ (k & 1)
            peer = 4 * px + 2 * py + pc
            sent.append(pltpu.make_async_remote_copy(
                src_ref=ins[a].at[peer], dst_ref=outs[a].at[me],
                send_sem=send_sems.at[a, k - 1], recv_sem=recv_sems.at[a, k - 1],
                device_id=(px, py, pc), device_id_type=MESH))
            arrivals.append(pltpu.make_async_remote_copy(
                src_ref=ins[a].at[me], dst_ref=outs[a].at[peer],
                send_sem=send_sems.at[a, k - 1], recv_sem=recv_sems.at[a, k - 1],
                device_id=(x, y, c), device_id_type=MESH))

    def start():
        for cp in own + sent:
            cp.start()

    def pass_on():
        pass

    def finish():
        for cp in arrivals:
            cp.wait_recv()
        for cp in sent:
            cp.wait_send()
        for cp in own:
            cp.wait()

    return start, pass_on, finish


def _scatter(grads):
    n = len(grads)
    return _Exchange(list(grads), [jax.ShapeDtypeStruct(g.shape, g.dtype) for g in grads],
                     [pltpu.SemaphoreType.DMA((n, N_DEV - 1)), pltpu.SemaphoreType.DMA((n, N_DEV - 1)),
                      pltpu.SemaphoreType.DMA((n,))], _scatter_steps)


def _run_exchange(name, ex):
    n_in, n_out = len(ex.ins), len(ex.out_shapes)

    def body(*refs):
        for step in ex.steps(refs[:n_in], refs[n_in:n_in + n_out], refs[n_in + n_out:]):
            step()

    any_spec = pl.BlockSpec(memory_space=pl.ANY)
    return _pcall(body, name=name, in_specs=[any_spec] * n_in, out_specs=[any_spec] * n_out,
                  out_shape=list(ex.out_shapes), scratch_shapes=list(ex.sem_shapes))(*ex.ins)


def _adam_math(g, w_, m_, v_):
    m_new = ADAM_B1 * m_ + (1.0 - ADAM_B1) * g
    v_new = ADAM_B2 * v_ + (1.0 - ADAM_B2) * (g * g)
    m_hat = m_new / (1.0 - ADAM_B1 ** ADAM_STEP)
    v_hat = v_new / (1.0 - ADAM_B2 ** ADAM_STEP)
    delta = -ADAM_LR * (m_hat / (jnp.sqrt(v_hat) + ADAM_EPS) + ADAM_WD * w_)
    return delta, m_new, v_new


def _adamw_weight(name, parts, w, m, v):
    nl = len(parts)

    def body(*refs):
        p_refs = refs[:nl]
        w_ref, m_ref, v_ref, g_ref, d_ref, mo_ref, vo_ref = refs[nl:]
        for l in range(nl):
            g = p_refs[l][0].astype(F32)
            for j in range(1, N_DEV):
                g = g + p_refs[l][j].astype(F32)
            g_ref[l] = g
            d_ref[l], mo_ref[l], vo_ref[l] = _adam_math(g, w_ref[l], m_ref[l], v_ref[l])

    return _pcall(
        body, name=name, out_shape=[jax.ShapeDtypeStruct(w.shape, F32)] * 4,
        compiler_params=pltpu.CompilerParams(vmem_limit_bytes=VMEM_LIMIT),
    )(*parts, w, m, v)

def _adamw(name, parts, w, m, v, tr):
    r = w.shape[0]

    def fn(p, w_, m_, v_):
        g = p[0].astype(F32)
        for j in range(1, N_DEV):
            g = g + p[j].astype(F32)
        return (g,) + _adam_math(g, w_, m_, v_)

    return _rows(name, fn, r, tr, [(parts, 'r1'), (w, 'r0'), (m, 'r0'), (v, 'r0')],
                 [((r, LANES), F32, 'r0')] * 4)


def _pack(arrs, rows_mult):
    lead = arrs[0].shape[0]
    flat = jnp.concatenate([a.reshape(lead, -1) for a in arrs], axis=1)
    per = rows_mult * LANES
    pad = (-flat.shape[1]) % per
    flat = jnp.pad(flat, ((0, 0), (0, pad)))
    return flat.reshape(lead, -1, LANES)


def _unpack(packed, shapes):
    flat = packed.reshape(-1)
    out, off = [], 0
    for shp in shapes:
        size = 1
        for d in shp:
            size *= d
        out.append(flat[off:off + size].reshape(shp))
        off += size
    return out


def _s5_params(lam_re, lam_im, log_dt, b_re, b_im, c_re, c_im):
    dt = jnp.exp(log_dt)[:, None]
    e = jnp.exp(lam_re * dt)
    ang = lam_im * dt
    a_re, a_im = e * jnp.cos(ang), e * jnp.sin(ang)
    nr, ni = a_re - 1.0, a_im
    den = lam_re * lam_re + lam_im * lam_im
    cr = ((nr * lam_re + ni * lam_im) / den)[..., None]
    ci = ((ni * lam_re - nr * lam_im) / den)[..., None]
    bb_re = cr * b_re - ci * b_im
    bb_im = cr * b_im + ci * b_re
    eye = jnp.eye(8, dtype=F32)[None, :, None, :, None]

    def bblk(bb):
        t = jnp.transpose(bb.reshape(4, 8, SSM_STATE, SSM_GROUP), (0, 3, 1, 2))
        return (eye * t[:, None]).reshape(4, 8 * SSM_GROUP, 8 * SSM_STATE)

    def cblk(cc):
        t = jnp.transpose(cc.reshape(4, 8, SSM_GROUP, SSM_STATE), (0, 3, 1, 2))
        return (eye * t[:, None]).reshape(4, 8 * SSM_STATE, 8 * SSM_GROUP)

    nb = SSM_GROUPS * SSM_STATE // LANES
    return (a_re.reshape(nb, 1, LANES), a_im.reshape(nb, 1, LANES), bblk(bb_re), bblk(bb_im),
            cblk(c_re), -cblk(c_im))


def _cat_blocks(x3, j):
    return jnp.concatenate([x3[4 * j + k] for k in range(4)], axis=-1)


def _to_chunks(a):
    s, c = a.shape
    return a.reshape(8, s // 8, c).transpose(1, 0, 2).reshape(s, c)


def _from_chunks(a):
    s, c = a.shape
    return a.reshape(s // 8, 8, c).transpose(1, 0, 2).reshape(s, c)


def _layer_fwd(h, memx, tabs, wl, pl_, side=None):
    s = h.shape[0]
    tm = min(256, s)
    cos, sin, pmat, pmat_t = tabs
    sv = {}

    def f_mix_in(h_, g, w):
        xn, _ = _rms(h_, g[...])
        return (_mm(xn, w[...]),)
    proj, = _rows("mix_in", f_mix_in, s, tm, [(h, 'r0'), (pl_['norm_mix_g'], 'f'), (wl['w_in'], 'f')],
                  [((s, D_MODEL), F32, 'r0')])

    def f_qkv(pr, cos_, sin_, gq, gkv, wq, wk, wv, pm):
        cqn = _rms(pr[:, 0:Q_LORA], gq[...])[0].astype(MXU)
        kvn = _rms(pr[:, Q_LORA:Q_LORA + KV_LORA], gkv[...])[0].astype(MXU)
        krr = _rope(pr[:, 384:512], cos_, sin_, pm)
        qs, ks, vs = [], [], []
        for hd in range(MLA_HEADS):
            qs.append(_rope(_mm(cqn, wq[hd]), cos_, sin_, pm) * MLA_SCALE)
            ks.append(_mm(kvn, wk[hd]) + krr)
            vs.append(_mm(kvn, wv[hd]))
        return jnp.stack(qs), jnp.stack(ks), jnp.stack(vs)
    hshape = (MLA_HEADS, s, HEAD_W)
    q, k, v = _rows("mla_qkv", f_qkv, s, tm,
                    [(proj, 'r0'), (cos, 'r0'), (sin, 'r0'), (pl_['q_norm_g'], 'f'), (pl_['kv_norm_g'], 'f'),
                     (wl['w_uq'], 'f'), (wl['w_k'], 'f'), (wl['w_v'], 'f'), (pmat, 'f')],
                    [(hshape, MXU, 'r1')] * 3)
    a_out, lse, side_out = _flash_fwd(q, k, v, side)

    u_ch = _to_chunks(proj[:, 512:1024])

    def f_s5_in(u, bre, bim):
        outs_r, outs_i = [], []
        for j in range(4):
            uj = _lanes(u, j, LANES)
            rr, ri = _mm_hi(uj, bre[j]), _mm_hi(uj, bim[j])
            outs_r += [_lanes(rr, kk, LANES) for kk in range(4)]
            outs_i += [_lanes(ri, kk, LANES) for kk in range(4)]
        return jnp.stack(outs_r), jnp.stack(outs_i)
    xshape = (16, s, LANES)
    bu_re, bu_im = _rows("s5_in", f_s5_in, s, tm, [(u_ch, 'r0'), (pl_['b_re'], 'f'), (pl_['b_im'], 'f')],
                         [(xshape, F32, 'r1')] * 2)
    x_re, x_im = _scan(bu_re, bu_im, pl_['a_re'], pl_['a_im'], False)

    def f_s5_out(xr, xi, u, cre, cim, d, wglu, bglu):
        y = jnp.concatenate([_mm_hi(_cat_blocks(xr, j), cre[j]) + _mm_hi(_cat_blocks(xi, j), cim[j])
                             for j in range(4)], axis=-1) + d[...] * u
        z = _mm(jax.nn.gelu(y), wglu[...]) + bglu[...]
        return y, y * jax.nn.sigmoid(z)
    y_ssm, s_out_ch = _rows("s5_out", f_s5_out, s, tm,
                            [(x_re, 'r1'), (x_im, 'r1'), (u_ch, 'r0'), (pl_['c_re'], 'f'), (pl_['c_im'], 'f'),
                             (pl_['ssm_d'], 'f'), (wl['ssm_w_glu'], 'f'), (pl_['ssm_b_glu'], 'f')],
                            [((s, SSM_WIDTH), F32, 'r0')] * 2)
    s_out = _from_chunks(s_out_ch)

    def f_mix_out(h_, a, so, ga, gs, w):
        an = _rms(a, ga[...])[0]
        sn = _rms(so, gs[...])[0]
        return (h_ + _mm(jnp.concatenate([an, sn], axis=-1), w[...]),)
    h1, = _rows("mix_out", f_mix_out, s, tm,
                [(h, 'r0'), (a_out, 'r0'), (s_out, 'r0'), (pl_['attn_out_g'], 'f'), (pl_['ssm_out_g'], 'f'),
                 (wl['w_out'], 'f')], [((s, D_MODEL), F32, 'r0')])

    m_len = memx.shape[0]

    def f_memkv(mm_, g, w):
        mn = _rms(mm_, g[...])[0].astype(MXU)
        return (jnp.stack([_mm(mn, w[d]) for d in range(N_DEV)]),)
    kvm, = _rows("mem_kv", f_memkv, m_len, m_len, [(memx, 'r0'), (pl_['mem_norm_g'], 'f'), (wl['w_xkv'], 'f')],
                 [((N_DEV, m_len, X_HEAD_DIM), MXU, 'r1')])

    def f_xattn(h_, g, wq, kv_, wo):
        hn = _rms(h_, g[...])[0].astype(MXU)
        out = jnp.zeros(h_.shape, F32)
        for hd in range(X_HEADS):
            cs = pl.ds(hd * X_HEAD_DIM, X_HEAD_DIM)
            qh = _mm(hn, wq[:, cs])
            p = _softmax(_mm_nt(qh, kv_[hd]) * X_SCALE)
            out = out + _mm(_mm(p, kv_[X_HEADS + hd]), wo[cs, :])
        return (h_ + out,)
    h2, = _rows("xattn", f_xattn, s, tm, [(h1, 'r0'), (pl_['norm_x_g'], 'f'), (wl['w_xq'], 'f'), (kvm, 'f'),
                                          (wl['w_xo'], 'f')], [((s, D_MODEL), F32, 'r0')])

    def f_ffn(h_, g, wg, wu, wd):
        hn = _rms(h_, g[...])[0].astype(MXU)
        y = jnp.zeros(h_.shape, F32)
        for d in range(N_DEV):
            gate = _mm(hn, wg[d])
            y = y + _mm(gate * jax.nn.sigmoid(gate) * _mm(hn, wu[d]), wd[d])
        return (h_ + y,)
    h3, = _rows("ffn", f_ffn, s, tm, [(h2, 'r0'), (pl_['norm_ffn_g'], 'f'), (wl['w_gate'], 'f'),
                                      (wl['w_up'], 'f'), (wl['w_down'], 'f')], [((s, D_MODEL), F32, 'r0')])
    sv.update(h=h, proj=proj, q=q, k=k, v=v, a_out=a_out, lse=lse, x_re=x_re, x_im=x_im, y_ssm=y_ssm,
              s_out=s_out, h1=h1, kvm=kvm, h2=h2, u_ch=u_ch)
    return h3, sv, side_out


def _layer_bwd(dh3, sv, memx, tabs, wl, pl_, side=None):
    s = dh3.shape[0]
    tm = min(256, s)
    cos, sin, pmat, pmat_t = tabs
    gr = {}
    act_shape = (N_DEV, s, FF_BLK)

    def f_ffn_bwd(h_, dy, g, wg, wu, wd):
        hn, r = _rms(h_, g[...])
        hb = hn.astype(MXU)
        dyb = dy.astype(MXU)
        dhn = jnp.zeros(h_.shape, F32)
        acts, dgs, dus = [], [], []
        for d in range(N_DEV):
            gate, up = _mm(hb, wg[d]), _mm(hb, wu[d])
            sg = jax.nn.sigmoid(gate)
            si = gate * sg
            dact = _mm_nt(dyb, wd[d])
            dgate = (dact * up * (sg * (1.0 + gate * (1.0 - sg)))).astype(MXU)
            dup = (dact * si).astype(MXU)
            dhn = dhn + _mm_nt(dgate, wg[d]) + _mm_nt(dup, wu[d])
            acts.append((si * up).astype(MXU))
            dgs.append(dgate)
            dus.append(dup)
        dh, dg = _rms_bwd(h_, g[...], r, dhn)
        return dy + dh, hb, jnp.stack(acts), jnp.stack(dgs), jnp.stack(dus), dg
    dh2, hn_f, act, dgate, dup, gr['norm_ffn_g'] = _rows(
        "ffn_bwd", f_ffn_bwd, s, tm,
        [(sv['h2'], 'r0'), (dh3, 'r0'), (pl_['norm_ffn_g'], 'f'), (wl['w_gate'], 'f'), (wl['w_up'], 'f'),
         (wl['w_down'], 'f')],
        [((s, D_MODEL), F32, 'r0'), ((s, D_MODEL), MXU, 'r0'), (act_shape, MXU, 'r1'), (act_shape, MXU, 'r1'),
         (act_shape, MXU, 'r1'), ((1, D_MODEL), F32, 'a')])
    gr['w_gate'] = _mm_tn_call("dw_gate", hn_f[None], dgate)
    gr['w_up'] = _mm_tn_call("dw_up", hn_f[None], dup)
    gr['w_down'] = _mm_tn_call("dw_down", act, dh3[None])

    m_len = memx.shape[0]

    def f_xattn_bwd(h_, dy, g, wq, kv_, wo):
        hn, r = _rms(h_, g[...])
        hb = hn.astype(MXU)
        dyb = dy.astype(MXU)
        dhn = jnp.zeros(h_.shape, F32)
        dqs, ohs, dks, dvs = [], [], [], []
        for hd in range(X_HEADS):
            cs = pl.ds(hd * X_HEAD_DIM, X_HEAD_DIM)
            kh, vh = kv_[hd], kv_[X_HEADS + hd]
            qh = _mm(hb, wq[:, cs])
            p = _softmax(_mm_nt(qh, kh) * X_SCALE)
            ohs.append(_mm(p, vh).astype(MXU))
            do = _mm_nt(dyb, wo[cs, :])
            dvs.append(_mm_tn(p, do))
            dp = _mm_nt(do, vh)
            ds = p * (dp - jnp.sum(dp * p, axis=-1, keepdims=True)) * X_SCALE
            dq = _mm(ds, kh).astype(MXU)
            dks.append(_mm_tn(ds, qh))
            dhn = dhn + _mm_nt(dq, wq[:, cs])
            dqs.append(dq)
        dh, dg = _rms_bwd(h_, g[...], r, dhn)
        return (dy + dh, hb, jnp.concatenate(dqs, axis=-1), jnp.concatenate(ohs, axis=-1),
                jnp.stack(dks + dvs), dg)
    dh1, hn_x, dq_x, oh_x, dkvm, gr['norm_x_g'] = _rows(
        "xattn_bwd", f_xattn_bwd, s, tm,
        [(sv['h1'], 'r0'), (dh2, 'r0'), (pl_['norm_x_g'], 'f'), (wl['w_xq'], 'f'), (sv['kvm'], 'f'),
         (wl['w_xo'], 'f')],
        [((s, D_MODEL), F32, 'r0'), ((s, D_MODEL), MXU, 'r0'), ((s, D_MODEL), MXU, 'r0'),
         ((s, D_MODEL), MXU, 'r0'), ((N_DEV, m_len, X_HEAD_DIM), F32, 'a'), ((1, D_MODEL), F32, 'a')])
    gr['w_xq'] = _mm_tn_call("dw_xq", hn_x[None], dq_x[None])[0]
    gr['w_xo'] = _mm_tn_call("dw_xo", oh_x[None], dh2[None])[0]

    def f_memkv_bwd(mm_, dkv, g, w):
        mn, r = _rms(mm_, g[...])
        mb = mn.astype(MXU)
        dmn = jnp.zeros(mm_.shape, F32)
        dws = []
        for d in range(N_DEV):
            dmn = dmn + _mm_nt(dkv[d], w[d])
            dws.append(_mm_tn(mb, dkv[d]))
        _, dg = _rms_bwd(mm_, g[...], r, dmn)
        return jnp.stack(dws), dg
    gr['w_xkv'], gr['mem_norm_g'] = _rows(
        "mem_kv_bwd", f_memkv_bwd, m_len, m_len,
        [(memx, 'r0'), (dkvm, 'r1'), (pl_['mem_norm_g'], 'f'), (wl['w_xkv'], 'f')],
        [((N_DEV, D_MODEL, X_HEAD_DIM), F32, 'a'), ((1, D_MODEL), F32, 'a')])

    def f_mix_out_bwd(a, so, dy, ga, gs, w):
        dmix = _mm_nt(dy, w[...])
        an, ra = _rms(a, ga[...])
        sn, rs = _rms(so, gs[...])
        da, dga = _rms_bwd(a, ga[...], ra, dmix[:, 0:512])
        dso, dgs = _rms_bwd(so, gs[...], rs, dmix[:, 512:1024])
        return da, dso, jnp.concatenate([an, sn], axis=-1), dga, dgs
    da_out, ds_out, mixed, gr['attn_out_g'], gr['ssm_out_g'] = _rows(
        "mix_out_bwd", f_mix_out_bwd, s, tm,
        [(sv['a_out'], 'r0'), (sv['s_out'], 'r0'), (dh1, 'r0'), (pl_['attn_out_g'], 'f'), (pl_['ssm_out_g'], 'f'),
         (wl['w_out'], 'f')],
        [((s, 512), F32, 'r0'), ((s, 512), F32, 'r0'), ((s, D_MODEL), MXU, 'r0'), ((1, 512), F32, 'a'),
         ((1, 512), F32, 'a')])
    gr['w_out'] = _mm_tn_call("dw_out", mixed[None], dh1[None])[0]

    def f_s5_out_bwd(xr, xi, u, y, ds, cre, cim, d, wglu, bglu):
        g, gelu_vjp = jax.vjp(jax.nn.gelu, y)
        sig = jax.nn.sigmoid(_mm(g, wglu[...]) + bglu[...])
        dz = ds * y * sig * (1.0 - sig)
        dy = ds * sig + gelu_vjp(_mm_nt(dz, wglu[...]))[0]
        dxr, dxi, dcr, dci = [], [], [], []
        for j in range(4):
            dyj = _lanes(dy, j, LANES)
            tr_, ti_ = _mm_nt(dyj, cre[j]), _mm_nt(dyj, cim[j])
            dxr += [_lanes(tr_, kk, LANES) for kk in range(4)]
            dxi += [_lanes(ti_, kk, LANES) for kk in range(4)]
            dcr.append(_mm_tn(_cat_blocks(xr, j), dyj))
            dci.append(_mm_tn(_cat_blocks(xi, j), dyj))
        return (jnp.stack(dxr), jnp.stack(dxi), dy * d[...], jnp.stack(dcr), jnp.stack(dci),
                jnp.sum(dy * u, axis=0, keepdims=True), _mm_tn(g, dz), jnp.sum(dz, axis=0, keepdims=True))
    xshape = (16, s, LANES)
    dx_re, dx_im, du_dir, gr['c_re'], gr['c_im'], gr['ssm_d'], gr['ssm_w_glu'], gr['ssm_b_glu'] = _rows(
        "s5_out_bwd", f_s5_out_bwd, s, tm,
        [(sv['x_re'], 'r1'), (sv['x_im'], 'r1'), (sv['u_ch'], 'r0'), (sv['y_ssm'], 'r0'), (_to_chunks(ds_out), 'r0'),
         (pl_['c_re'], 'f'), (pl_['c_im'], 'f'), (pl_['ssm_d'], 'f'), (wl['ssm_w_glu'], 'f'),
         (pl_['ssm_b_glu'], 'f')],
        [(xshape, F32, 'r1'), (xshape, F32, 'r1'), ((s, 512), F32, 'r0'), ((4, 512, LANES), F32, 'a'),
         ((4, 512, LANES), F32, 'a'), ((1, 512), F32, 'a'), ((512, 512), F32, 'a'), ((1, 512), F32, 'a')])
    g_re, g_im = _scan(dx_re, dx_im, pl_['a_re'], -pl_['a_im'], True)
    first_re = jnp.pad(sv['x_re'][:, s - 8:s - 1], ((0, 0), (1, 0), (0, 0)))
    first_im = jnp.pad(sv['x_im'][:, s - 8:s - 1], ((0, 0), (1, 0), (0, 0)))

    def f_s5_in_bwd(gre, gim, xr, xi, pr8, pi8, u, dud, f8r, f8i, bre, bim):
        first = pl.program_id(0) == 0
        xpr = jnp.concatenate([jnp.where(first, f8r[...], pr8), xr[:, :tm - 8]], axis=1)
        xpi = jnp.concatenate([jnp.where(first, f8i[...], pi8), xi[:, :tm - 8]], axis=1)
        dus, dbr, dbi = [], [], []
        for j in range(4):
            gj_r, gj_i, uj = _cat_blocks(gre, j), _cat_blocks(gim, j), _lanes(u, j, LANES)
            dus.append(_mm_nt(gj_r, bre[j]) + _mm_nt(gj_i, bim[j]))
            dbr.append(_mm_tn(uj, gj_r))
            dbi.append(_mm_tn(uj, gj_i))
        da_r = jnp.sum(gre * xpr + gim * xpi, axis=1, keepdims=True)
        da_i = jnp.sum(gim * xpr - gre * xpi, axis=1, keepdims=True)
        return dud + jnp.concatenate(dus, axis=-1), jnp.stack(dbr), jnp.stack(dbi), da_r, da_i
    du_ch, gr['b_re'], gr['b_im'], gr['a_re'], gr['a_im'] = _rows(
        "s5_in_bwd", f_s5_in_bwd, s, tm,
        [(g_re, 'r1'), (g_im, 'r1'), (sv['x_re'], 'r1'), (sv['x_im'], 'r1'), (sv['x_re'], 'p8'), (sv['x_im'], 'p8'),
         (sv['u_ch'], 'r0'), (du_dir, 'r0'), (first_re, 'f'), (first_im, 'f'), (pl_['b_re'], 'f'), (pl_['b_im'], 'f')],
        [((s, 512), F32, 'r0'), ((4, LANES, 512), F32, 'a'), ((4, LANES, 512), F32, 'a'),
         ((16, 1, LANES), F32, 'a'), ((16, 1, LANES), F32, 'a')])
    du = _from_chunks(du_ch)

    dq, dk, dv, side_out = _flash_bwd(sv['q'], sv['k'], sv['v'], sv['a_out'], sv['lse'], da_out, side)

    def f_qkv_bwd(pr, cos_, sin_, dq_, dk_, dv_, gq, gkv, wq, wk, wv, pt):
        cq, ckv = pr[:, 0:Q_LORA], pr[:, Q_LORA:Q_LORA + KV_LORA]
        cqn, rq = _rms(cq, gq[...])
        kvn, rkv = _rms(ckv, gkv[...])
        cqb, kvb = cqn.astype(MXU), kvn.astype(MXU)
        dcqn = jnp.zeros(cq.shape, F32)
        dkvn = jnp.zeros(ckv.shape, F32)
        dksum = jnp.zeros(dk_[0].shape, F32)
        dwq, dwk, dwv = [], [], []
        for hd in range(MLA_HEADS):
            dqp = (_rope_t(dq_[hd], cos_, sin_, pt) * MLA_SCALE).astype(MXU)
            dkb, dvb = dk_[hd].astype(MXU), dv_[hd].astype(MXU)
            dwq.append(_mm_tn(cqb, dqp))
            dwk.append(_mm_tn(kvb, dkb))
            dwv.append(_mm_tn(kvb, dvb))
            dcqn = dcqn + _mm_nt(dqp, wq[hd])
            dkvn = dkvn + _mm_nt(dkb, wk[hd]) + _mm_nt(dvb, wv[hd])
            dksum = dksum + dk_[hd]
        dcq, dgq = _rms_bwd(cq, gq[...], rq, dcqn)
        dckv, dgkv = _rms_bwd(ckv, gkv[...], rkv, dkvn)
        dpa = jnp.concatenate([dcq, dckv, _rope_t(dksum, cos_, sin_, pt)], axis=-1)
        return dpa, jnp.stack(dwq), jnp.stack(dwk), jnp.stack(dwv), dgq, dgkv
    dpa, gr['w_uq'], gr['w_k'], gr['w_v'], gr['q_norm_g'], gr['kv_norm_g'] = _rows(
        "mla_qkv_bwd", f_qkv_bwd, s, tm,
        [(sv['proj'], 'r0'), (cos, 'r0'), (sin, 'r0'), (dq, 'r1'), (dk, 'r1'), (dv, 'r1'), (pl_['q_norm_g'], 'f'),
         (pl_['kv_norm_g'], 'f'), (wl['w_uq'], 'f'), (wl['w_k'], 'f'), (wl['w_v'], 'f'), (pmat_t, 'f')],
        [((s, 512), F32, 'r0'), ((MLA_HEADS, Q_LORA, HEAD_W), F32, 'a'), ((MLA_HEADS, KV_LORA, HEAD_W), F32, 'a'),
         ((MLA_HEADS, KV_LORA, HEAD_W), F32, 'a'), ((1, Q_LORA), F32, 'a'), ((1, KV_LORA), F32, 'a')])

    def f_mix_in_bwd(h_, dpa_, du_, dres, g, w):
        dproj = jnp.concatenate([dpa_, du_], axis=-1).astype(MXU)
        xn, r = _rms(h_, g[...])
        dh, dg = _rms_bwd(h_, g[...], r, _mm_nt(dproj, w[...]))
        return dres + dh, xn, dproj, dg
    dh0, xn, dproj, gr['norm_mix_g'] = _rows(
        "mix_in_bwd", f_mix_in_bwd, s, tm,
        [(sv['h'], 'r0'), (dpa, 'r0'), (du, 'r0'), (dh1, 'r0'), (pl_['norm_mix_g'], 'f'), (wl['w_in'], 'f')],
        [((s, D_MODEL), F32, 'r0'), ((s, D_MODEL), MXU, 'r0'), ((s, D_MODEL), MXU, 'r0'), ((1, D_MODEL), F32, 'a')])
    gr['w_in'] = _mm_tn_call("dw_in", xn[None], dproj[None])[0]
    return dh0, gr, side_out


def _layer_weights(w):
    w_in = w['w_in'].reshape(D_MODEL, -1)
    z = lambda n: jnp.zeros((D_MODEL, n), w_in.dtype)
    wl = {'w_in': jnp.concatenate([w_in[:, :384], z(64), w_in[:, 384:416], z(32), w_in[:, 416:]], axis=1)}
    wl['w_uq'] = jnp.pad(w['w_uq'], ((0, 0), (0, 0), (0, HEAD_W - QK_NOPE - QK_ROPE)))
    wl['w_k'] = jnp.pad(w['w_ukv'][..., :QK_NOPE], ((0, 0), (0, 0), (0, HEAD_W - QK_NOPE)))
    wv = w['w_ukv'][..., QK_NOPE:]
    even = (jnp.arange(MLA_HEADS) % 2 == 0)[:, None, None]
    wl['w_v'] = jnp.concatenate([jnp.where(even, wv, 0), jnp.where(even, 0, wv)], axis=-1).astype(wv.dtype)
    wl['ssm_w_glu'] = w['ssm_w_glu'].reshape(SSM_WIDTH, SSM_WIDTH)
    for n in ('w_out', 'w_xq', 'w_xo'):
        wl[n] = w[n].reshape(D_MODEL, D_MODEL)
    for n in ('w_xkv', 'w_gate', 'w_up', 'w_down'):
        wl[n] = w[n]
    return wl


def _blocked_grads(gr):
    d = gr['w_in']
    out = {'w_in': jnp.concatenate([d[:, :384], d[:, 448:480], d[:, 512:]], axis=1).reshape(N_DEV, 128, -1)}
    out['w_uq'] = gr['w_uq'][..., :QK_NOPE + QK_ROPE]
    even = (jnp.arange(MLA_HEADS) % 2 == 0)[:, None, None]
    dv = gr['w_v']
    out['w_ukv'] = jnp.concatenate([gr['w_k'][..., :QK_NOPE], jnp.where(even, dv[..., :V_DIM], dv[..., V_DIM:])], axis=-1)
    out['ssm_w_glu'] = gr['ssm_w_glu'].reshape(N_DEV, SSM_WIDTH // N_DEV, SSM_WIDTH)
    for n in ('w_out', 'w_xq', 'w_xo'):
        out[n] = gr[n].reshape(N_DEV, D_MODEL // N_DEV, D_MODEL)
    for n in ('w_xkv', 'w_gate', 'w_up', 'w_down'):
        out[n] = gr[n]
    return [out[n].astype(MXU) for n in SHARDED]


def kernel(x, mem, positions, norm_mix_g, w_in, q_norm_g, w_uq, kv_norm_g, w_ukv, ssm_lambda_re, ssm_lambda_im, ssm_log_dt, ssm_b_re, ssm_b_im, ssm_c_re, ssm_c_im, ssm_d, ssm_w_glu, ssm_b_glu, attn_out_g, ssm_out_g, w_out, norm_x_g, mem_norm_g, w_xq, w_xkv, w_xo, norm_ffn_g, w_gate, w_up, w_down, final_norm_g, loss_target, m_norm_mix_g, m_w_in, m_q_norm_g, m_w_uq, m_kv_norm_g, m_w_ukv, m_ssm_lambda_re, m_ssm_lambda_im, m_ssm_log_dt, m_ssm_b_re, m_ssm_b_im, m_ssm_c_re, m_ssm_c_im, m_ssm_d, m_ssm_w_glu, m_ssm_b_glu, m_attn_out_g, m_ssm_out_g, m_w_out, m_norm_x_g, m_mem_norm_g, m_w_xq, m_w_xkv, m_w_xo, m_norm_ffn_g, m_w_gate, m_w_up, m_w_down, m_final_norm_g, v_norm_mix_g, v_w_in, v_q_norm_g, v_w_uq, v_kv_norm_g, v_w_ukv, v_ssm_lambda_re, v_ssm_lambda_im, v_ssm_log_dt, v_ssm_b_re, v_ssm_b_im, v_ssm_c_re, v_ssm_c_im, v_ssm_d, v_ssm_w_glu, v_ssm_b_glu, v_attn_out_g, v_ssm_out_g, v_w_out, v_norm_x_g, v_mem_norm_g, v_w_xq, v_w_xkv, v_w_xo, v_norm_ffn_g, v_w_gate, v_w_up, v_w_down, v_final_norm_g):
    args = dict(locals())
    W = {n: args[n] for n in WEIGHTS}
    M = {n: args['m_' + n] for n in WEIGHTS}
    V = {n: args['v_' + n] for n in WEIGHTS}
    s = x.shape[1]
    h = x[0]
    memx = mem[0]

    freqs = ROPE_THETA ** (-jnp.arange(0, QK_ROPE, 2, dtype=F32) / QK_ROPE)
    ang = positions[0].astype(F32)[:, None] * freqs
    c16, s16 = jnp.cos(ang), jnp.sin(ang)
    cos = jnp.concatenate([jnp.ones((s, QK_NOPE), F32), c16, c16, jnp.zeros((s, 32), F32)], axis=1)
    sin = jnp.concatenate([jnp.zeros((s, QK_NOPE), F32), s16, s16, jnp.zeros((s, 32), F32)], axis=1)
    idx = jnp.arange(QK_ROPE // 2)
    pmat = jnp.zeros((HEAD_W, HEAD_W), F32)
    pmat = pmat.at[QK_NOPE + 16 + idx, QK_NOPE + idx].set(-1.0).at[QK_NOPE + idx, QK_NOPE + 16 + idx].set(1.0)
    tabs = (cos, sin, pmat, pmat.T)

    shards = [[W[n][l].astype(MXU) for n in SHARDED] for l in range(DEPTH)]
    gathered = _run_exchange("gather_weights", _gather(shards[0]))

    layers = []
    for l in range(DEPTH):
        wl = _layer_weights(dict(zip(SHARDED, gathered)))
        s5_in = [W[n][l] for n in ('ssm_lambda_re', 'ssm_lambda_im', 'ssm_log_dt', 'ssm_b_re', 'ssm_b_im',
                                   'ssm_c_re', 'ssm_c_im')]
        (a_re, a_im, bre, bim, cre, cim), s5_vjp = jax.vjp(_s5_params, *s5_in)
        pl_ = {n: W[n][l][None] for n in ('norm_mix_g', 'q_norm_g', 'kv_norm_g', 'ssm_d', 'ssm_b_glu',
                                           'attn_out_g', 'ssm_out_g', 'norm_x_g', 'mem_norm_g', 'norm_ffn_g')}
        pl_.update(a_re=a_re, a_im=a_im, b_re=bre, b_im=bim, c_re=cre, c_im=cim)
        h, sv, gathered = _layer_fwd(h, memx, tabs, wl, pl_, _gather(shards[l + 1]) if l + 1 < DEPTH else None)
        layers.append((wl, pl_, sv, s5_vjp))

    def f_loss(h_, tgt, g):
        y, r = _rms(h_, g[...])
        err = y - tgt
        part = 0.5 * jnp.sum(jnp.mean(err * err, axis=-1, keepdims=True), axis=0, keepdims=True)
        dh, dg = _rms_bwd(h_, g[...], r, err / D_MODEL)
        return dh, dg, jnp.broadcast_to(part, (8, LANES))
    dh, g_final, loss_part = _rows(
        "loss_head", f_loss, s, min(256, s), [(h, 'r0'), (loss_target[0], 'r0'), (final_norm_g[None], 'f')],
        [((s, D_MODEL), F32, 'r0'), ((1, D_MODEL), F32, 'a'), ((8, LANES), F32, 'a')])
    loss = lax.psum(loss_part[0, 0], ("x", "y", "c"))

    parts = [None] * DEPTH
    g_rep = [None] * DEPTH
    blocks = None
    for l in reversed(range(DEPTH)):
        wl, pl_, sv, s5_vjp = layers[l]
        dh, gr, arrived = _layer_bwd(dh, sv, memx, tabs, wl, pl_, _scatter(blocks) if blocks else None)
        if blocks:
            parts[l + 1] = arrived
        blocks = _blocked_grads(gr)
        ds5 = s5_vjp((gr['a_re'], gr['a_im'], gr['b_re'], gr['b_im'], gr['c_re'], gr['c_im']))
        rep = dict(zip(('ssm_lambda_re', 'ssm_lambda_im', 'ssm_log_dt', 'ssm_b_re', 'ssm_b_im', 'ssm_c_re',
                        'ssm_c_im'), ds5))
        for n in ('norm_mix_g', 'q_norm_g', 'kv_norm_g', 'ssm_d', 'ssm_b_glu', 'attn_out_g', 'ssm_out_g',
                  'norm_x_g', 'mem_norm_g', 'norm_ffn_g'):
            rep[n] = gr[n][0]
        g_rep[l] = rep
    grad_x = dh[None]

    parts[0] = _run_exchange("scatter_grads", _scatter(blocks))
    out_sh = [{}, {}, {}, {}]
    for a, n in enumerate(SHARDED):
        res = _adamw_weight("adamw_" + n, [parts[l][a] for l in range(DEPTH)], W[n], M[n], V[n])
        for kind, r in enumerate(res):
            out_sh[kind][n] = r

    tr_rp = 256
    rep_names = REPL_L + ['final_norm_g']
    g_loc = {n: jnp.stack([g_rep[l][n] for l in range(DEPTH)]) for n in REPL_L}
    g_loc['final_norm_g'] = g_final[0]
    pk_r = lambda d: _pack([d[n].reshape(1, -1) for n in rep_names], tr_rp)[0]
    parts_r, = _run_exchange("gather_small_grads", _gather([pk_r(g_loc)]))
    res_rp = _adamw("adamw_replicated", parts_r, pk_r(W), pk_r(M), pk_r(V), tr_rp)
    shapes_rp = [W[n].shape for n in rep_names]
    out_rp = [dict(zip(rep_names, _unpack(r, shapes_rp))) for r in res_rp]

    outs = [loss, grad_x]
    for kind in range(4):
        for n in WEIGHTS:
            outs.append(out_sh[kind][n] if n in SHARDED else out_rp[kind][n])
    return tuple(outs)
```

```python
from typing import Callable, NamedTuple

import jax
import jax.numpy as jnp
from jax import lax
from jax.experimental import pallas as pl
from jax.experimental.pallas import tpu as pltpu

F32 = jnp.float32
MXU = jnp.bfloat16
HI = lax.Precision.HIGHEST

D_MODEL = 1024
MLA_HEADS = 8
QK_NOPE = 64
QK_ROPE = 32
V_DIM = 64
Q_LORA = 256
KV_LORA = 128
SSM_WIDTH = 512
SSM_GROUPS = 32
SSM_GROUP = 16
SSM_STATE = 64
X_HEADS = 4
X_HEAD_DIM = 256
D_FF = 2816
FF_BLK = D_FF // 8
ROPE_THETA = 10000.0
EPS = 1e-6
DEPTH = 2
N_DEV = 8
LANES = 128
HEAD_W = 128
MLA_SCALE = (QK_NOPE + QK_ROPE) ** -0.5
X_SCALE = X_HEAD_DIM ** -0.5
ADAM_LR, ADAM_B1, ADAM_B2, ADAM_EPS, ADAM_WD, ADAM_STEP = 0.001, 0.9, 0.999, 1e-08, 0.01, 10
VMEM_LIMIT = 56 * 1024 * 1024
FLASH_TILE = 512
MESH = pl.DeviceIdType.MESH

SHARDED = ['w_in', 'w_uq', 'w_ukv', 'ssm_w_glu', 'w_out', 'w_xq', 'w_xkv', 'w_xo', 'w_gate', 'w_up', 'w_down']
REPL_L = ['norm_mix_g', 'q_norm_g', 'kv_norm_g', 'ssm_lambda_re', 'ssm_lambda_im', 'ssm_log_dt', 'ssm_b_re',
          'ssm_b_im', 'ssm_c_re', 'ssm_c_im', 'ssm_d', 'ssm_b_glu', 'attn_out_g', 'ssm_out_g', 'norm_x_g',
          'mem_norm_g', 'norm_ffn_g']
WEIGHTS = ['norm_mix_g', 'w_in', 'q_norm_g', 'w_uq', 'kv_norm_g', 'w_ukv', 'ssm_lambda_re', 'ssm_lambda_im',
           'ssm_log_dt', 'ssm_b_re', 'ssm_b_im', 'ssm_c_re', 'ssm_c_im', 'ssm_d', 'ssm_w_glu', 'ssm_b_glu',
           'attn_out_g', 'ssm_out_g', 'w_out', 'norm_x_g', 'mem_norm_g', 'w_xq', 'w_xkv', 'w_xo', 'norm_ffn_g',
           'w_gate', 'w_up', 'w_down', 'final_norm_g']


def _pcall(body, **kw):
    return pl.pallas_call(body, **kw)


def _mm(a, b):
    return jnp.dot(a.astype(MXU), b.astype(MXU), preferred_element_type=F32)


def _mm_nt(a, b):
    return lax.dot_general(a.astype(MXU), b.astype(MXU), (((1,), (1,)), ((), ())), preferred_element_type=F32)


def _mm_tn(a, b):
    return lax.dot_general(a.astype(MXU), b.astype(MXU), (((0,), (0,)), ((), ())), preferred_element_type=F32)


def _mm_hi(a, b):
    return jnp.dot(a.astype(F32), b.astype(F32), precision=HI, preferred_element_type=F32)


def _rms(x, g):
    r = lax.rsqrt(jnp.mean(x * x, axis=-1, keepdims=True) + EPS)
    return x * r * g, r


def _rms_bwd(x, g, r, dy):
    dyg = dy * g
    dx = r * dyg - x * (r * r * r) * jnp.mean(dyg * x, axis=-1, keepdims=True)
    return dx, jnp.sum(dy * x * r, axis=0, keepdims=True)


def _rope(x, cos, sin, p_ref):
    return x * cos + _mm_hi(x, p_ref[...]) * sin


def _rope_t(g, cos, sin, pt_ref):
    return g * cos + _mm_hi(g * sin, pt_ref[...])


def _softmax(s):
    m = jnp.max(s, axis=-1, keepdims=True)
    e = jnp.exp(s - m)
    return e / jnp.sum(e, axis=-1, keepdims=True)


def _lanes(x, j, w):
    return x[:, j * w:(j + 1) * w]


def _rows(name, fn, n, tm, ins, outs, side=None):
    def spec(shape, kind):
        nd = len(shape)
        if kind == 'p8':
            return pl.BlockSpec((shape[0], 8, shape[2]), lambda i: (0, jnp.maximum(i * (tm // 8) - 1, 0), 0))
        if kind == 'f':
            return pl.BlockSpec(shape, lambda i, _nd=nd: (0,) * _nd, pipeline_mode=pl.Buffered(1))
        if kind == 'a':
            return pl.BlockSpec(shape, lambda i, _nd=nd: (0,) * _nd)
        ax = int(kind[1])
        blk = tuple(tm if d == ax else s for d, s in enumerate(shape))
        return pl.BlockSpec(blk, lambda i, _ax=ax, _nd=nd: tuple(i if d == _ax else 0 for d in range(_nd)))

    n_in, n_out, n_steps = len(ins), len(outs), n // tm

    def body(*refs):
        in_refs, out_refs, steps = _side_split(refs, n_in, n_out, side)
        i = pl.program_id(0)
        if steps:
            pl.when(i == 0)(steps[0])
            pl.when(i == n_steps // 2)(steps[1])
        args = [r if k == 'f' else r[...] for r, (_, k) in zip(in_refs, ins)]
        res = fn(*args)
        for r, (_, dt, k), v in zip(out_refs, outs, res):
            if k == 'a':
                _accumulate(r, v.astype(dt), i)
            else:
                r[...] = v.astype(dt)
        if steps:
            pl.when(i == n_steps - 1)(steps[2])

    s_in, s_out, s_shape, s_sems, s_ops = _side_args(side)
    res = _pcall(
        body, name=name + ("_x" if side else ""), grid=(n_steps,),
        in_specs=[spec(a.shape, k) for a, k in ins] + s_in,
        out_specs=[spec(s, k) for s, _, k in outs] + s_out,
        out_shape=[jax.ShapeDtypeStruct(s, dt) for s, dt, _ in outs] + s_shape,
        scratch_shapes=s_sems,
        compiler_params=pltpu.CompilerParams(dimension_semantics=("arbitrary",), vmem_limit_bytes=VMEM_LIMIT),
    )(*[a for a, _ in ins], *s_ops)
    return _Hosted(res[:n_out], res[n_out:]) if side else res


def _accumulate(ref, v, i):
    @pl.when(i == 0)
    def _():
        ref[...] = v

    @pl.when(i != 0)
    def _():
        ref[...] += v


def _mm_tn_call(name, a, b):
    out_dtype = MXU
    ba, s, k = a.shape
    bb, _, n = b.shape
    nb = max(ba, bb)
    ts = min(512, s)
    ns = s // ts

    def body(a_ref, b_ref, o_ref, acc_ref):
        j = pl.program_id(1)
        _accumulate(acc_ref, _mm_tn(a_ref[...], b_ref[...]), j)

        @pl.when(j == ns - 1)
        def _():
            o_ref[...] = acc_ref[...].astype(out_dtype)

    return _pcall(
        body, name=name, grid=(nb, ns),
        in_specs=[pl.BlockSpec((None, ts, k), (lambda i, j: (i, j, 0)) if ba > 1 else (lambda i, j: (0, j, 0))),
                  pl.BlockSpec((None, ts, n), (lambda i, j: (i, j, 0)) if bb > 1 else (lambda i, j: (0, j, 0)))],
        out_specs=pl.BlockSpec((None, k, n), lambda i, j: (i, 0, 0)),
        out_shape=jax.ShapeDtypeStruct((nb, k, n), out_dtype),
        scratch_shapes=[pltpu.VMEM((k, n), F32)],
        compiler_params=pltpu.CompilerParams(dimension_semantics=("arbitrary", "arbitrary"),
                                             vmem_limit_bytes=VMEM_LIMIT),
    )(a, b)


def _side_split(refs, n_in, n_out, side):
    if side is None:
        return refs[:n_in], refs[n_in:n_in + n_out], None
    si, so = len(side.ins), len(side.out_shapes)
    own_in, side_in = refs[:n_in], refs[n_in:n_in + si]
    own_out, side_out = refs[n_in + si:n_in + si + n_out], refs[n_in + si + n_out:n_in + si + n_out + so]
    return own_in, own_out, side.steps(side_in, side_out, refs[n_in + si + n_out + so:])


def _side_args(side):
    if side is None:
        return [], [], [], [], []
    any_spec = pl.BlockSpec(memory_space=pl.ANY)
    return ([any_spec] * len(side.ins), [any_spec] * len(side.out_shapes), list(side.out_shapes),
            list(side.sem_shapes), list(side.ins))


class _Hosted(NamedTuple):
    results: list
    arrived: list


def _hosted(res):
    return res if isinstance(res, _Hosted) else _Hosted(res, ())


def _flash_fwd(q, k, v, side=None):
    nh, s, w = q.shape
    t = min(FLASH_TILE, s)
    nq = s // t
    n_steps = (nh // 2) * nq

    def body(*refs):
        (q_ref, k_ref, v_ref), (o_ref, lse_ref), steps = _side_split(refs, 3, 2, side)
        step = pl.program_id(0) * nq + pl.program_id(1)
        if steps:
            pl.when(step == 0)(steps[0])
            pl.when(step == n_steps // 2)(steps[1])
        qi = pl.program_id(1)
        qs = [q_ref[0], q_ref[1]]
        below = lax.broadcasted_iota(jnp.int32, (t, t), 1) <= lax.broadcasted_iota(jnp.int32, (t, t), 0)

        def tile(j, carry, diagonal):
            sl = pl.ds(pl.multiple_of(j * t, t), t)
            out = []
            for hh in range(2):
                m, l, acc = carry[3 * hh:3 * hh + 3]
                sc = _mm_nt(qs[hh], k_ref[hh, sl, :])
                if diagonal:
                    sc = jnp.where(below, sc, -1e30)
                m_new = jnp.maximum(m, jnp.max(sc, axis=1, keepdims=True))
                p = jnp.exp(sc - m_new)
                alpha = jnp.exp(m - m_new)
                out += [m_new, alpha * l + jnp.sum(p, axis=1, keepdims=True), alpha * acc + _mm(p, v_ref[hh, sl, :])]
            return tuple(out)

        init = (jnp.full((t, 1), -1e30, F32), jnp.zeros((t, 1), F32), jnp.zeros((t, w), F32)) * 2
        carry = lax.fori_loop(0, qi, lambda j, c: tile(j, c, False), init)
        carry = tile(qi, carry, True)
        o_ref[...] = carry[2] / carry[1] + carry[5] / carry[4]
        for hh in range(2):
            lse_ref[hh] = jnp.broadcast_to(carry[3 * hh] + jnp.log(carry[3 * hh + 1]), (t, w))
        if steps:
            pl.when(step == n_steps - 1)(steps[2])

    s_in, s_out, s_shape, s_sems, s_ops = _side_args(side)
    res = _pcall(
        body, name="mla_flash_fwd" + ("_x" if side else ""), grid=(nh // 2, nq),
        in_specs=[pl.BlockSpec((2, t, w), lambda p, i: (p, i, 0)),
                  pl.BlockSpec((2, s, w), lambda p, i: (p, 0, 0)),
                  pl.BlockSpec((2, s, w), lambda p, i: (p, 0, 0))] + s_in,
        out_specs=[pl.BlockSpec((t, w), lambda p, i: (i, p)),
                   pl.BlockSpec((2, t, w), lambda p, i: (p, i, 0))] + s_out,
        out_shape=[jax.ShapeDtypeStruct((s, (nh // 2) * w), F32), jax.ShapeDtypeStruct((nh, s, w), F32)] + s_shape,
        scratch_shapes=s_sems,
        compiler_params=pltpu.CompilerParams(dimension_semantics=("arbitrary", "arbitrary"),
                                             vmem_limit_bytes=VMEM_LIMIT),
    )(q, k, v, *s_ops)
    return res[0], res[1], res[2:]


def _flash_bwd(q, k, v, o, lse, do, side=None):
    nh, s, w = q.shape
    t = min(FLASH_TILE, s)
    nq = s // t
    n_steps = (nh // 2) * nq

    def body(*refs):
        (q_ref, k_ref, v_ref, o_ref, lse_ref, do_ref), (dq_ref, dk_ref, dv_ref), steps = _side_split(refs, 6, 3, side)
        step = pl.program_id(0) * nq + pl.program_id(1)
        if steps:
            pl.when(step == 0)(steps[0])
            pl.when(step == n_steps // 2)(steps[1])
        j = pl.program_id(1)

        @pl.when(j == 0)
        def _():
            dq_ref[...] = jnp.zeros(dq_ref.shape, F32)

        below = lax.broadcasted_iota(jnp.int32, (t, t), 1) <= lax.broadcasted_iota(jnp.int32, (t, t), 0)
        lane = lax.broadcasted_iota(jnp.int32, (t, w), 1)
        heads = [jnp.logical_and(lane >= hh * V_DIM, lane < (hh + 1) * V_DIM) for hh in range(2)]
        ks = [k_ref[0], k_ref[1]]
        vs = [v_ref[0], v_ref[1]]

        def tile(i, carry, diagonal):
            sl = pl.ds(pl.multiple_of(i * t, t), t)
            dout_all, o_all = do_ref[sl, :], o_ref[sl, :]
            out = []
            for hh in range(2):
                dk, dv = carry[2 * hh], carry[2 * hh + 1]
                qh = q_ref[hh, sl, :]
                dout = jnp.where(heads[hh], dout_all, 0.0)
                sc = _mm_nt(qh, ks[hh])
                if diagonal:
                    sc = jnp.where(below, sc, -1e30)
                p = jnp.exp(sc - lse_ref[hh, sl, 0:1])
                dp = _mm_nt(dout, vs[hh])
                ds = p * (dp - jnp.sum(dout * o_all, axis=1, keepdims=True))
                dq_ref[hh, sl, :] += _mm(ds, ks[hh])
                out += [dk + _mm_tn(ds, qh), dv + _mm_tn(p, dout)]
            return tuple(out)

        carry = tile(j, (jnp.zeros((t, w), F32),) * 4, True)
        carry = lax.fori_loop(j + 1, nq, lambda i, c: tile(i, c, False), carry)
        for hh in range(2):
            dk_ref[hh] = carry[2 * hh]
            dv_ref[hh] = jnp.where(heads[hh], carry[2 * hh + 1], 0.0)
        if steps:
            pl.when(step == n_steps - 1)(steps[2])

    s_in, s_out, s_shape, s_sems, s_ops = _side_args(side)
    res = _pcall(
        body, name="mla_flash_bwd" + ("_x" if side else ""), grid=(nh // 2, nq),
        in_specs=[pl.BlockSpec((2, s, w), lambda p, j: (p, 0, 0)),
                  pl.BlockSpec((2, t, w), lambda p, j: (p, j, 0)),
                  pl.BlockSpec((2, t, w), lambda p, j: (p, j, 0)),
                  pl.BlockSpec((s, w), lambda p, j: (0, p)),
                  pl.BlockSpec((2, s, w), lambda p, j: (p, 0, 0)),
                  pl.BlockSpec((s, w), lambda p, j: (0, p))] + s_in,
        out_specs=[pl.BlockSpec((2, s, w), lambda p, j: (p, 0, 0)),
                   pl.BlockSpec((2, t, w), lambda p, j: (p, j, 0)),
                   pl.BlockSpec((2, t, w), lambda p, j: (p, j, 0))] + s_out,
        out_shape=[jax.ShapeDtypeStruct((nh, s, w), F32)] * 3 + s_shape,
        scratch_shapes=s_sems,
        compiler_params=pltpu.CompilerParams(dimension_semantics=("arbitrary", "arbitrary"),
                                             vmem_limit_bytes=VMEM_LIMIT),
    )(q, k, v, o, lse, do, *s_ops)
    return res[0], res[1], res[2], res[3:]


def _scan(b_re, b_im, a_re, a_im, reverse):
    nb, s, w = b_re.shape
    ch = s // 8
    assert ch & (ch - 1) == 0
    grp = 2

    def cmul(ar, ai, xr, xi):
        return ar * xr - ai * xi, ar * xi + ai * xr

    def body(br_ref, bi_ref, ar_ref, ai_ref, xr_ref, xi_ref):
        sub = lax.broadcasted_iota(jnp.int32, (8, w), 0)

        def shift(x, k):
            if reverse:
                return jnp.where(sub < 8 - k, pltpu.roll(x, 8 - k, 0), 0.0)
            return jnp.where(sub >= k, pltpu.roll(x, k, 0), 0.0)

        ar = [jnp.broadcast_to(ar_ref[g], (8, w)) for g in range(grp)]
        ai = [jnp.broadcast_to(ai_ref[g], (8, w)) for g in range(grp)]

        def tsl(i):
            return pl.ds(pl.multiple_of(((ch - 1 - i) if reverse else i) * 8, 8), 8)

        def local(i, carry):
            out = []
            for g in range(grp):
                xr, xi = carry[2 * g], carry[2 * g + 1]
                pr, pi = cmul(ar[g], ai[g], xr, xi)
                nr = pr + br_ref[g, tsl(i), :]
                ni = pi + bi_ref[g, tsl(i), :]
                xr_ref[g, tsl(i), :] = nr
                xi_ref[g, tsl(i), :] = ni
                out += [nr, ni]
            return tuple(out)

        fin = lax.fori_loop(0, ch, local, (jnp.zeros((8, w), F32),) * (2 * grp))

        carry_in = []
        for g in range(grp):
            pr, pi = ar[g], ai[g]
            for _ in range(ch.bit_length() - 1):
                pr, pi = cmul(pr, pi, pr, pi)
            fr, fi = fin[2 * g], fin[2 * g + 1]
            for kk in (1, 2, 4):
                sr, si = cmul(pr, pi, shift(fr, kk), shift(fi, kk))
                fr, fi = fr + sr, fi + si
                pr, pi = cmul(pr, pi, pr, pi)
            carry_in += [shift(fr, 1), shift(fi, 1)]

        def fix(i, pw):
            out = []
            for g in range(grp):
                pr, pi = pw[2 * g], pw[2 * g + 1]
                cr, ci = cmul(pr, pi, carry_in[2 * g], carry_in[2 * g + 1])
                xr_ref[g, tsl(i), :] = xr_ref[g, tsl(i), :] + cr
                xi_ref[g, tsl(i), :] = xi_ref[g, tsl(i), :] + ci
                nr, ni = cmul(pr, pi, ar[g], ai[g])
                out += [nr, ni]
            return tuple(out)

        lax.fori_loop(0, ch, fix, tuple(x for g in range(grp) for x in (ar[g], ai[g])))

    blk = pl.BlockSpec((grp, s, w), lambda i: (i, 0, 0))
    ablk = pl.BlockSpec((grp, 1, w), lambda i: (i, 0, 0))
    return _pcall(
        body, name="s5_scan_rev" if reverse else "s5_scan", grid=(nb // grp,),
        in_specs=[blk, blk, ablk, ablk], out_specs=[blk, blk],
        out_shape=[jax.ShapeDtypeStruct((nb, s, w), F32)] * 2,
        compiler_params=pltpu.CompilerParams(dimension_semantics=("arbitrary",), vmem_limit_bytes=VMEM_LIMIT),
    )(b_re, b_im, a_re, a_im)


class _Exchange(NamedTuple):
    ins: list
    out_shapes: list
    sem_shapes: list
    steps: Callable


def _gather_steps(ins, outs, sems):
    n = len(ins)
    send_sems, recv_sems, local_sems = sems
    x, y, c = lax.axis_index("x"), lax.axis_index("y"), lax.axis_index("c")
    me, sibling = (x, y, c), (x, y, 1 - c)
    chips = [(1 - x, y), (x, 1 - y), (1 - x, 1 - y)]

    def copy(a, k, block, to, src=None):
        dst = outs[a].at[4 * block[0] + 2 * block[1] + block[2]]
        return pltpu.make_async_remote_copy(
            src_ref=dst if src is None else src, dst_ref=dst,
            send_sem=send_sems.at[a, k], recv_sem=recv_sems.at[a, k], device_id=to, device_id_type=MESH)

    mine = [pltpu.make_async_copy(ins[a], outs[a].at[4 * x + 2 * y + c], local_sems.at[a]) for a in range(n)]
    first = []
    for a in range(n):
        first.append(copy(a, 0, me, sibling, src=ins[a]))
        first += [copy(a, 1 + j, me, (*chip, c), src=ins[a]) for j, chip in enumerate(chips)]
    passed = [copy(a, 4 + j, (*chip, c), sibling) for j, chip in enumerate(chips) for a in range(n)]

    def start():
        for cp in mine + first:
            cp.start()

    def pass_on():
        i = 0
        for j, chip in enumerate(chips):
            for a in range(n):
                copy(a, 1 + j, (*chip, c), me).wait_recv()
                passed[i].start()
                i += 1

    def finish():
        for a in range(n):
            copy(a, 0, sibling, me).wait_recv()
            for j, chip in enumerate(chips):
                copy(a, 4 + j, (*chip, 1 - c), me).wait_recv()
        for cp in first + passed:
            cp.wait_send()
        for cp in mine:
            cp.wait()

    return start, pass_on, finish


def _gather(arrs):
    n = len(arrs)
    return _Exchange(list(arrs), [jax.ShapeDtypeStruct((N_DEV,) + a.shape, a.dtype) for a in arrs],
                     [pltpu.SemaphoreType.DMA((n, 7)), pltpu.SemaphoreType.DMA((n, 7)), pltpu.SemaphoreType.DMA((n,))],
                     _gather_steps)


def _scatter_steps(ins, outs, sems):
    n = len(ins)
    send_sems, recv_sems, local_sems = sems
    x, y, c = lax.axis_index("x"), lax.axis_index("y"), lax.axis_index("c")
    me = 4 * x + 2 * y + c
    own, sent, arrivals = [], [], []
    for a in range(n):
        own.append(pltpu.make_async_copy(ins[a].at[me], outs[a].at[me], local_sems.at[a]))
        for k in range(1, N_DEV):
            px, py, pc = x ^ ((k >> 2) & 1), y ^ ((k >> 1) & 1), c ^ (k & 1)
            peer = 4 * px + 2 * py + pc
            sent.append(pltpu.make_async_remote_copy(
                src_ref=ins[a].at[peer], dst_ref=outs[a].at[me],
                send_sem=send_sems.at[a, k - 1], recv_sem=recv_sems.at[a, k - 1],
                device_id=(px, py, pc), device_id_type=MESH))
            arrivals.append(pltpu.make_async_remote_copy(
                src_ref=ins[a].at[me], dst_ref=outs[a].at[peer],
                send_sem=send_sems.at[a, k - 1], recv_sem=recv_sems.at[a, k - 1],
                device_id=(x, y, c), device_id_type=MESH))

    def start():
        for cp in own + sent:
            cp.start()

    def pass_on():
        pass

    def finish():
        for cp in arrivals:
            cp.wait_recv()
        for cp in sent:
            cp.wait_send()
        for cp in own:
            cp.wait()

    return start, pass_on, finish


def _scatter(grads):
    n = len(grads)
    return _Exchange(list(grads), [jax.ShapeDtypeStruct(g.shape, g.dtype) for g in grads],
                     [pltpu.SemaphoreType.DMA((n, N_DEV - 1)), pltpu.SemaphoreType.DMA((n, N_DEV - 1)),
                      pltpu.SemaphoreType.DMA((n,))], _scatter_steps)


def _run_exchange(name, ex):
    n_in, n_out = len(ex.ins), len(ex.out_shapes)

    def body(*refs):
        for step in ex.steps(refs[:n_in], refs[n_in:n_in + n_out], refs[n_in + n_out:]):
            step()

    any_spec = pl.BlockSpec(memory_space=pl.ANY)
    return _pcall(body, name=name, in_specs=[any_spec] * n_in, out_specs=[any_spec] * n_out,
                  out_shape=list(ex.out_shapes), scratch_shapes=list(ex.sem_shapes))(*ex.ins)


def _adam_math(g, w_, m_, v_):
    m_new = ADAM_B1 * m_ + (1.0 - ADAM_B1) * g
    v_new = ADAM_B2 * v_ + (1.0 - ADAM_B2) * (g * g)
    m_hat = m_new / (1.0 - ADAM_B1 ** ADAM_STEP)
    v_hat = v_new / (1.0 - ADAM_B2 ** ADAM_STEP)
    delta = -ADAM_LR * (m_hat / (jnp.sqrt(v_hat) + ADAM_EPS) + ADAM_WD * w_)
    return delta, m_new, v_new


def _adamw_weight(name, parts, w, m, v):
    nl = len(parts)

    def body(*refs):
        p_refs = refs[:nl]
        w_ref, m_ref, v_ref, g_ref, d_ref, mo_ref, vo_ref = refs[nl:]
        for l in range(nl):
            g = p_refs[l][0].astype(F32)
            for j in range(1, N_DEV):
                g = g + p_refs[l][j].astype(F32)
            g_ref[l] = g
            d_ref[l], mo_ref[l], vo_ref[l] = _adam_math(g, w_ref[l], m_ref[l], v_ref[l])

    return _pcall(
        body, name=name, out_shape=[jax.ShapeDtypeStruct(w.shape, F32)] * 4,
        compiler_params=pltpu.CompilerParams(vmem_limit_bytes=VMEM_LIMIT),
    )(*parts, w, m, v)

def _adamw(name, parts, w, m, v, tr):
    r = w.shape[0]

    def fn(p, w_, m_, v_):
        g = p[0].astype(F32)
        for j in range(1, N_DEV):
            g = g + p[j].astype(F32)
        return (g,) + _adam_math(g, w_, m_, v_)

    return _rows(name, fn, r, tr, [(parts, 'r1'), (w, 'r0'), (m, 'r0'), (v, 'r0')],
                 [((r, LANES), F32, 'r0')] * 4)


def _pack(arrs, rows_mult):
    lead = arrs[0].shape[0]
    flat = jnp.concatenate([a.reshape(lead, -1) for a in arrs], axis=1)
    per = rows_mult * LANES
    pad = (-flat.shape[1]) % per
    flat = jnp.pad(flat, ((0, 0), (0, pad)))
    return flat.reshape(lead, -1, LANES)


def _unpack(packed, shapes):
    flat = packed.reshape(-1)
    out, off = [], 0
    for shp in shapes:
        size = 1
        for d in shp:
            size *= d
        out.append(flat[off:off + size].reshape(shp))
        off += size
    return out


def _s5_params(lam_re, lam_im, log_dt, b_re, b_im, c_re, c_im):
    dt = jnp.exp(log_dt)[:, None]
    e = jnp.exp(lam_re * dt)
    ang = lam_im * dt
    a_re, a_im = e * jnp.cos(ang), e * jnp.sin(ang)
    nr, ni = a_re - 1.0, a_im
    den = lam_re * lam_re + lam_im * lam_im
    cr = ((nr * lam_re + ni * lam_im) / den)[..., None]
    ci = ((ni * lam_re - nr * lam_im) / den)[..., None]
    bb_re = cr * b_re - ci * b_im
    bb_im = cr * b_im + ci * b_re
    eye = jnp.eye(8, dtype=F32)[None, :, None, :, None]

    def bblk(bb):
        t = jnp.transpose(bb.reshape(4, 8, SSM_STATE, SSM_GROUP), (0, 3, 1, 2))
        return (eye * t[:, None]).reshape(4, 8 * SSM_GROUP, 8 * SSM_STATE)

    def cblk(cc):
        t = jnp.transpose(cc.reshape(4, 8, SSM_GROUP, SSM_STATE), (0, 3, 1, 2))
        return (eye * t[:, None]).reshape(4, 8 * SSM_STATE, 8 * SSM_GROUP)

    nb = SSM_GROUPS * SSM_STATE // LANES
    return (a_re.reshape(nb, 1, LANES), a_im.reshape(nb, 1, LANES), bblk(bb_re), bblk(bb_im),
            cblk(c_re), -cblk(c_im))


def _cat_blocks(x3, j):
    return jnp.concatenate([x3[4 * j + k] for k in range(4)], axis=-1)


def _to_chunks(a):
    s, c = a.shape
    return a.reshape(8, s // 8, c).transpose(1, 0, 2).reshape(s, c)


def _from_chunks(a):
    s, c = a.shape
    return a.reshape(s // 8, 8, c).transpose(1, 0, 2).reshape(s, c)


EARLY = ['w_in', 'w_uq', 'w_ukv']
MIDDLE = ['ssm_w_glu', 'w_out', 'w_xq', 'w_xkv', 'w_xo']
FEED = ['w_gate', 'w_up', 'w_down']


def _named(names, d):
    return [d[n] for n in names]


def _layer_fwd(h, memx, tabs, wl, pl_, late=None, nxt=None):
    s = h.shape[0]
    tm = min(256, s)
    cos, sin, pmat, pmat_t = tabs
    sv = {}
    wl = dict(wl)

    def f_mix_in(h_, g, w):
        xn, _ = _rms(h_, g[...])
        return (_mm(xn, w[...]),)
    proj, = _rows("mix_in", f_mix_in, s, tm, [(h, 'r0'), (pl_['norm_mix_g'], 'f'), (wl['w_in'], 'f')],
                  [((s, D_MODEL), F32, 'r0')])

    def f_qkv(pr, cos_, sin_, gq, gkv, wq, wk, wv, pm):
        cqn = _rms(pr[:, 0:Q_LORA], gq[...])[0].astype(MXU)
        kvn = _rms(pr[:, Q_LORA:Q_LORA + KV_LORA], gkv[...])[0].astype(MXU)
        krr = _rope(pr[:, 384:512], cos_, sin_, pm)
        qs, ks, vs = [], [], []
        for hd in range(MLA_HEADS):
            qs.append(_rope(_mm(cqn, wq[hd]), cos_, sin_, pm) * MLA_SCALE)
            ks.append(_mm(kvn, wk[hd]) + krr)
            vs.append(_mm(kvn, wv[hd]))
        return jnp.stack(qs), jnp.stack(ks), jnp.stack(vs)
    hshape = (MLA_HEADS, s, HEAD_W)
    q, k, v = _rows("mla_qkv", f_qkv, s, tm,
                    [(proj, 'r0'), (cos, 'r0'), (sin, 'r0'), (pl_['q_norm_g'], 'f'), (pl_['kv_norm_g'], 'f'),
                     (wl['w_uq'], 'f'), (wl['w_k'], 'f'), (wl['w_v'], 'f'), (pmat, 'f')],
                    [(hshape, MXU, 'r1')] * 3)
    a_out, lse, arrived = _flash_fwd(q, k, v, _gather(_named(MIDDLE + FEED, late)) if late else None)
    if late:
        wl.update(_layer_weights(dict(zip(MIDDLE + FEED, arrived))))

    u_ch = _to_chunks(proj[:, 512:1024])

    def f_s5_in(u, bre, bim):
        outs_r, outs_i = [], []
        for j in range(4):
            uj = _lanes(u, j, LANES)
            rr, ri = _mm_hi(uj, bre[j]), _mm_hi(uj, bim[j])
            outs_r += [_lanes(rr, kk, LANES) for kk in range(4)]
            outs_i += [_lanes(ri, kk, LANES) for kk in range(4)]
        return jnp.stack(outs_r), jnp.stack(outs_i)
    xshape = (16, s, LANES)
    bu_re, bu_im = _rows("s5_in", f_s5_in, s, tm, [(u_ch, 'r0'), (pl_['b_re'], 'f'), (pl_['b_im'], 'f')],
                         [(xshape, F32, 'r1')] * 2)
    x_re, x_im = _scan(bu_re, bu_im, pl_['a_re'], pl_['a_im'], False)

    def f_s5_out(xr, xi, u, cre, cim, d, wglu, bglu):
        y = jnp.concatenate([_mm_hi(_cat_blocks(xr, j), cre[j]) + _mm_hi(_cat_blocks(xi, j), cim[j])
                             for j in range(4)], axis=-1) + d[...] * u
        z = _mm(jax.nn.gelu(y), wglu[...]) + bglu[...]
        return y, y * jax.nn.sigmoid(z)
    y_ssm, s_out_ch = _rows("s5_out", f_s5_out, s, tm,
                            [(x_re, 'r1'), (x_im, 'r1'), (u_ch, 'r0'), (pl_['c_re'], 'f'), (pl_['c_im'], 'f'),
                             (pl_['ssm_d'], 'f'), (wl['ssm_w_glu'], 'f'), (pl_['ssm_b_glu'], 'f')],
                            [((s, SSM_WIDTH), F32, 'r0')] * 2)
    s_out = _from_chunks(s_out_ch)

    def f_mix_out(h_, a, so, ga, gs, w):
        an = _rms(a, ga[...])[0]
        sn = _rms(so, gs[...])[0]
        return (h_ + _mm(jnp.concatenate([an, sn], axis=-1), w[...]),)
    h1, = _rows("mix_out", f_mix_out, s, tm,
                [(h, 'r0'), (a_out, 'r0'), (s_out, 'r0'), (pl_['attn_out_g'], 'f'), (pl_['ssm_out_g'], 'f'),
                 (wl['w_out'], 'f')], [((s, D_MODEL), F32, 'r0')])

    m_len = memx.shape[0]

    def f_memkv(mm_, g, w):
        mn = _rms(mm_, g[...])[0].astype(MXU)
        return (jnp.stack([_mm(mn, w[d]) for d in range(N_DEV)]),)
    kvm, = _rows("mem_kv", f_memkv, m_len, m_len, [(memx, 'r0'), (pl_['mem_norm_g'], 'f'), (wl['w_xkv'], 'f')],
                 [((N_DEV, m_len, X_HEAD_DIM), MXU, 'r1')])

    def f_xattn(h_, g, wq, kv_, wo):
        hn = _rms(h_, g[...])[0].astype(MXU)
        out = jnp.zeros(h_.shape, F32)
        for hd in range(X_HEADS):
            cs = pl.ds(hd * X_HEAD_DIM, X_HEAD_DIM)
            qh = _mm(hn, wq[:, cs])
            p = _softmax(_mm_nt(qh, kv_[hd]) * X_SCALE)
            out = out + _mm(_mm(p, kv_[X_HEADS + hd]), wo[cs, :])
        return (h_ + out,)
    (h2,), got_a = _hosted(_rows("xattn", f_xattn, s, tm,
                                 [(h1, 'r0'), (pl_['norm_x_g'], 'f'), (wl['w_xq'], 'f'), (kvm, 'f'), (wl['w_xo'], 'f')],
                                 [((s, D_MODEL), F32, 'r0')], _gather(_named(EARLY + MIDDLE, nxt)) if nxt else None))

    def f_ffn(h_, g, wg, wu, wd):
        hn = _rms(h_, g[...])[0].astype(MXU)
        y = jnp.zeros(h_.shape, F32)
        for d in range(N_DEV):
            gate = _mm(hn, wg[d])
            y = y + _mm(gate * jax.nn.sigmoid(gate) * _mm(hn, wu[d]), wd[d])
        return (h_ + y,)
    (h3,), got_b = _hosted(_rows("ffn", f_ffn, s, tm,
                                 [(h2, 'r0'), (pl_['norm_ffn_g'], 'f'), (wl['w_gate'], 'f'), (wl['w_up'], 'f'),
                                  (wl['w_down'], 'f')],
                                 [((s, D_MODEL), F32, 'r0')], _gather(_named(FEED, nxt)) if nxt else None))
    sv.update(h=h, proj=proj, q=q, k=k, v=v, a_out=a_out, lse=lse, x_re=x_re, x_im=x_im, y_ssm=y_ssm,
              s_out=s_out, h1=h1, kvm=kvm, h2=h2, u_ch=u_ch)
    return h3, sv, wl, dict(zip(EARLY + MIDDLE + FEED, tuple(got_a) + tuple(got_b)))


def _layer_bwd(dh3, sv, memx, tabs, wl, pl_, nxt=None, own=False):
    s = dh3.shape[0]
    tm = min(256, s)
    cos, sin, pmat, pmat_t = tabs
    gr = {}
    arrived = {}
    act_shape = (N_DEV, s, FF_BLK)

    def send(*groups):
        keys = [(who, n) for who, names in groups if (nxt if who == 'nxt' else own) for n in names]
        if not keys:
            return None, keys
        return _scatter([nxt[n] if who == 'nxt' else _blocked(gr, n) for who, n in keys]), keys

    def f_ffn_bwd(h_, dy, g, wg, wu, wd):
        hn, r = _rms(h_, g[...])
        hb = hn.astype(MXU)
        dyb = dy.astype(MXU)
        dhn = jnp.zeros(h_.shape, F32)
        acts, dgs, dus = [], [], []
        for d in range(N_DEV):
            gate, up = _mm(hb, wg[d]), _mm(hb, wu[d])
            sg = jax.nn.sigmoid(gate)
            si = gate * sg
            dact = _mm_nt(dyb, wd[d])
            dgate = (dact * up * (sg * (1.0 + gate * (1.0 - sg)))).astype(MXU)
            dup = (dact * si).astype(MXU)
            dhn = dhn + _mm_nt(dgate, wg[d]) + _mm_nt(dup, wu[d])
            acts.append((si * up).astype(MXU))
            dgs.append(dgate)
            dus.append(dup)
        dh, dg = _rms_bwd(h_, g[...], r, dhn)
        return dy + dh, hb, jnp.stack(acts), jnp.stack(dgs), jnp.stack(dus), dg
    ex, keys = send(('nxt', ['w_gate', 'w_up']))
    (dh2, hn_f, act, dgate, dup, gr['norm_ffn_g']), got = _hosted(_rows(
        "ffn_bwd", f_ffn_bwd, s, tm,
        [(sv['h2'], 'r0'), (dh3, 'r0'), (pl_['norm_ffn_g'], 'f'), (wl['w_gate'], 'f'), (wl['w_up'], 'f'),
         (wl['w_down'], 'f')],
        [((s, D_MODEL), F32, 'r0'), ((s, D_MODEL), MXU, 'r0'), (act_shape, MXU, 'r1'), (act_shape, MXU, 'r1'),
         (act_shape, MXU, 'r1'), ((1, D_MODEL), F32, 'a')], ex))
    arrived.update(zip(keys, got))
    gr['w_gate'] = _mm_tn_call("dw_gate", hn_f[None], dgate)
    gr['w_up'] = _mm_tn_call("dw_up", hn_f[None], dup)
    gr['w_down'] = _mm_tn_call("dw_down", act, dh3[None])

    m_len = memx.shape[0]

    def f_xattn_bwd(h_, dy, g, wq, kv_, wo):
        hn, r = _rms(h_, g[...])
        hb = hn.astype(MXU)
        dyb = dy.astype(MXU)
        dhn = jnp.zeros(h_.shape, F32)
        dqs, ohs, dks, dvs = [], [], [], []
        for hd in range(X_HEADS):
            cs = pl.ds(hd * X_HEAD_DIM, X_HEAD_DIM)
            kh, vh = kv_[hd], kv_[X_HEADS + hd]
            qh = _mm(hb, wq[:, cs])
            p = _softmax(_mm_nt(qh, kh) * X_SCALE)
            ohs.append(_mm(p, vh).astype(MXU))
            do = _mm_nt(dyb, wo[cs, :])
            dvs.append(_mm_tn(p, do))
            dp = _mm_nt(do, vh)
            ds = p * (dp - jnp.sum(dp * p, axis=-1, keepdims=True)) * X_SCALE
            dq = _mm(ds, kh).astype(MXU)
            dks.append(_mm_tn(ds, qh))
            dhn = dhn + _mm_nt(dq, wq[:, cs])
            dqs.append(dq)
        dh, dg = _rms_bwd(h_, g[...], r, dhn)
        return (dy + dh, hb, jnp.concatenate(dqs, axis=-1), jnp.concatenate(ohs, axis=-1),
                jnp.stack(dks + dvs), dg)
    ex, keys = send(('nxt', ['w_down', 'w_xkv', 'w_xq']))
    (dh1, hn_x, dq_x, oh_x, dkvm, gr['norm_x_g']), got = _hosted(_rows(
        "xattn_bwd", f_xattn_bwd, s, tm,
        [(sv['h1'], 'r0'), (dh2, 'r0'), (pl_['norm_x_g'], 'f'), (wl['w_xq'], 'f'), (sv['kvm'], 'f'),
         (wl['w_xo'], 'f')],
        [((s, D_MODEL), F32, 'r0'), ((s, D_MODEL), MXU, 'r0'), ((s, D_MODEL), MXU, 'r0'),
         ((s, D_MODEL), MXU, 'r0'), ((N_DEV, m_len, X_HEAD_DIM), F32, 'a'), ((1, D_MODEL), F32, 'a')], ex))
    arrived.update(zip(keys, got))
    gr['w_xq'] = _mm_tn_call("dw_xq", hn_x[None], dq_x[None])[0]
    gr['w_xo'] = _mm_tn_call("dw_xo", oh_x[None], dh2[None])[0]

    def f_memkv_bwd(mm_, dkv, g, w):
        mn, r = _rms(mm_, g[...])
        mb = mn.astype(MXU)
        dmn = jnp.zeros(mm_.shape, F32)
        dws = []
        for d in range(N_DEV):
            dmn = dmn + _mm_nt(dkv[d], w[d])
            dws.append(_mm_tn(mb, dkv[d]))
        _, dg = _rms_bwd(mm_, g[...], r, dmn)
        return jnp.stack(dws), dg
    gr['w_xkv'], gr['mem_norm_g'] = _rows(
        "mem_kv_bwd", f_memkv_bwd, m_len, m_len,
        [(memx, 'r0'), (dkvm, 'r1'), (pl_['mem_norm_g'], 'f'), (wl['w_xkv'], 'f')],
        [((N_DEV, D_MODEL, X_HEAD_DIM), F32, 'a'), ((1, D_MODEL), F32, 'a')])

    def f_mix_out_bwd(a, so, dy, ga, gs, w):
        dmix = _mm_nt(dy, w[...])
        an, ra = _rms(a, ga[...])
        sn, rs = _rms(so, gs[...])
        da, dga = _rms_bwd(a, ga[...], ra, dmix[:, 0:512])
        dso, dgs = _rms_bwd(so, gs[...], rs, dmix[:, 512:1024])
        return da, dso, jnp.concatenate([an, sn], axis=-1), dga, dgs
    da_out, ds_out, mixed, gr['attn_out_g'], gr['ssm_out_g'] = _rows(
        "mix_out_bwd", f_mix_out_bwd, s, tm,
        [(sv['a_out'], 'r0'), (sv['s_out'], 'r0'), (dh1, 'r0'), (pl_['attn_out_g'], 'f'), (pl_['ssm_out_g'], 'f'),
         (wl['w_out'], 'f')],
        [((s, 512), F32, 'r0'), ((s, 512), F32, 'r0'), ((s, D_MODEL), MXU, 'r0'), ((1, 512), F32, 'a'),
         ((1, 512), F32, 'a')])
    gr['w_out'] = _mm_tn_call("dw_out", mixed[None], dh1[None])[0]

    ex, keys = send(('nxt', ['w_in', 'w_uq', 'w_ukv', 'ssm_w_glu', 'w_out', 'w_xo']), ('own', ['w_gate', 'w_up']))
    dq, dk, dv, got = _flash_bwd(sv['q'], sv['k'], sv['v'], sv['a_out'], sv['lse'], da_out, ex)
    arrived.update(zip(keys, got))

    def f_s5_out_bwd(xr, xi, u, y, ds, cre, cim, d, wglu, bglu):
        g, gelu_vjp = jax.vjp(jax.nn.gelu, y)
        sig = jax.nn.sigmoid(_mm(g, wglu[...]) + bglu[...])
        dz = ds * y * sig * (1.0 - sig)
        dy = ds * sig + gelu_vjp(_mm_nt(dz, wglu[...]))[0]
        dxr, dxi, dcr, dci = [], [], [], []
        for j in range(4):
            dyj = _lanes(dy, j, LANES)
            tr_, ti_ = _mm_nt(dyj, cre[j]), _mm_nt(dyj, cim[j])
            dxr += [_lanes(tr_, kk, LANES) for kk in range(4)]
            dxi += [_lanes(ti_, kk, LANES) for kk in range(4)]
            dcr.append(_mm_tn(_cat_blocks(xr, j), dyj))
            dci.append(_mm_tn(_cat_blocks(xi, j), dyj))
        return (jnp.stack(dxr), jnp.stack(dxi), dy * d[...], jnp.stack(dcr), jnp.stack(dci),
                jnp.sum(dy * u, axis=0, keepdims=True), _mm_tn(g, dz), jnp.sum(dz, axis=0, keepdims=True))
    xshape = (16, s, LANES)
    ex, keys = send(('own', ['w_down']))
    (dx_re, dx_im, du_dir, gr['c_re'], gr['c_im'], gr['ssm_d'], gr['ssm_w_glu'], gr['ssm_b_glu']), got = _hosted(_rows(
        "s5_out_bwd", f_s5_out_bwd, s, tm,
        [(sv['x_re'], 'r1'), (sv['x_im'], 'r1'), (sv['u_ch'], 'r0'), (sv['y_ssm'], 'r0'), (_to_chunks(ds_out), 'r0'),
         (pl_['c_re'], 'f'), (pl_['c_im'], 'f'), (pl_['ssm_d'], 'f'), (wl['ssm_w_glu'], 'f'),
         (pl_['ssm_b_glu'], 'f')],
        [(xshape, F32, 'r1'), (xshape, F32, 'r1'), ((s, 512), F32, 'r0'), ((4, 512, LANES), F32, 'a'),
         ((4, 512, LANES), F32, 'a'), ((1, 512), F32, 'a'), ((512, 512), F32, 'a'), ((1, 512), F32, 'a')], ex))
    arrived.update(zip(keys, got))
    g_re, g_im = _scan(dx_re, dx_im, pl_['a_re'], -pl_['a_im'], True)
    first_re = jnp.pad(sv['x_re'][:, s - 8:s - 1], ((0, 0), (1, 0), (0, 0)))
    first_im = jnp.pad(sv['x_im'][:, s - 8:s - 1], ((0, 0), (1, 0), (0, 0)))

    def f_s5_in_bwd(gre, gim, xr, xi, pr8, pi8, u, dud, f8r, f8i, bre, bim):
        first = pl.program_id(0) == 0
        xpr = jnp.concatenate([jnp.where(first, f8r[...], pr8), xr[:, :tm - 8]], axis=1)
        xpi = jnp.concatenate([jnp.where(first, f8i[...], pi8), xi[:, :tm - 8]], axis=1)
        dus, dbr, dbi = [], [], []
        for j in range(4):
            gj_r, gj_i, uj = _cat_blocks(gre, j), _cat_blocks(gim, j), _lanes(u, j, LANES)
            dus.append(_mm_nt(gj_r, bre[j]) + _mm_nt(gj_i, bim[j]))
            dbr.append(_mm_tn(uj, gj_r))
            dbi.append(_mm_tn(uj, gj_i))
        da_r = jnp.sum(gre * xpr + gim * xpi, axis=1, keepdims=True)
        da_i = jnp.sum(gim * xpr - gre * xpi, axis=1, keepdims=True)
        return dud + jnp.concatenate(dus, axis=-1), jnp.stack(dbr), jnp.stack(dbi), da_r, da_i
    ex, keys = send(('own', ['w_xkv']))
    (du_ch, gr['b_re'], gr['b_im'], gr['a_re'], gr['a_im']), got = _hosted(_rows(
        "s5_in_bwd", f_s5_in_bwd, s, tm,
        [(g_re, 'r1'), (g_im, 'r1'), (sv['x_re'], 'r1'), (sv['x_im'], 'r1'), (sv['x_re'], 'p8'), (sv['x_im'], 'p8'),
         (sv['u_ch'], 'r0'), (du_dir, 'r0'), (first_re, 'f'), (first_im, 'f'), (pl_['b_re'], 'f'), (pl_['b_im'], 'f')],
        [((s, 512), F32, 'r0'), ((4, LANES, 512), F32, 'a'), ((4, LANES, 512), F32, 'a'),
         ((16, 1, LANES), F32, 'a'), ((16, 1, LANES), F32, 'a')], ex))
    arrived.update(zip(keys, got))
    du = _from_chunks(du_ch)

    def f_qkv_bwd(pr, cos_, sin_, dq_, dk_, dv_, gq, gkv, wq, wk, wv, pt):
        cq, ckv = pr[:, 0:Q_LORA], pr[:, Q_LORA:Q_LORA + KV_LORA]
        cqn, rq = _rms(cq, gq[...])
        kvn, rkv = _rms(ckv, gkv[...])
        cqb, kvb = cqn.astype(MXU), kvn.astype(MXU)
        dcqn = jnp.zeros(cq.shape, F32)
        dkvn = jnp.zeros(ckv.shape, F32)
        dksum = jnp.zeros(dk_[0].shape, F32)
        dwq, dwk, dwv = [], [], []
        for hd in range(MLA_HEADS):
            dqp = (_rope_t(dq_[hd], cos_, sin_, pt) * MLA_SCALE).astype(MXU)
            dkb, dvb = dk_[hd].astype(MXU), dv_[hd].astype(MXU)
            dwq.append(_mm_tn(cqb, dqp))
            dwk.append(_mm_tn(kvb, dkb))
            dwv.append(_mm_tn(kvb, dvb))
            dcqn = dcqn + _mm_nt(dqp, wq[hd])
            dkvn = dkvn + _mm_nt(dkb, wk[hd]) + _mm_nt(dvb, wv[hd])
            dksum = dksum + dk_[hd]
        dcq, dgq = _rms_bwd(cq, gq[...], rq, dcqn)
        dckv, dgkv = _rms_bwd(ckv, gkv[...], rkv, dkvn)
        dpa = jnp.concatenate([dcq, dckv, _rope_t(dksum, cos_, sin_, pt)], axis=-1)
        return dpa, jnp.stack(dwq), jnp.stack(dwk), jnp.stack(dwv), dgq, dgkv
    ex, keys = send(('own', ['w_xq', 'w_xo']))
    (dpa, gr['w_uq'], gr['w_k'], gr['w_v'], gr['q_norm_g'], gr['kv_norm_g']), got = _hosted(_rows(
        "mla_qkv_bwd", f_qkv_bwd, s, tm,
        [(sv['proj'], 'r0'), (cos, 'r0'), (sin, 'r0'), (dq, 'r1'), (dk, 'r1'), (dv, 'r1'), (pl_['q_norm_g'], 'f'),
         (pl_['kv_norm_g'], 'f'), (wl['w_uq'], 'f'), (wl['w_k'], 'f'), (wl['w_v'], 'f'), (pmat_t, 'f')],
        [((s, 512), F32, 'r0'), ((MLA_HEADS, Q_LORA, HEAD_W), F32, 'a'), ((MLA_HEADS, KV_LORA, HEAD_W), F32, 'a'),
         ((MLA_HEADS, KV_LORA, HEAD_W), F32, 'a'), ((1, Q_LORA), F32, 'a'), ((1, KV_LORA), F32, 'a')], ex))
    arrived.update(zip(keys, got))

    def f_mix_in_bwd(h_, dpa_, du_, dres, g, w):
        dproj = jnp.concatenate([dpa_, du_], axis=-1).astype(MXU)
        xn, r = _rms(h_, g[...])
        dh, dg = _rms_bwd(h_, g[...], r, _mm_nt(dproj, w[...]))
        return dres + dh, xn, dproj, dg
    dh0, xn, dproj, gr['norm_mix_g'] = _rows(
        "mix_in_bwd", f_mix_in_bwd, s, tm,
        [(sv['h'], 'r0'), (dpa, 'r0'), (du, 'r0'), (dh1, 'r0'), (pl_['norm_mix_g'], 'f'), (wl['w_in'], 'f')],
        [((s, D_MODEL), F32, 'r0'), ((s, D_MODEL), MXU, 'r0'), ((s, D_MODEL), MXU, 'r0'), ((1, D_MODEL), F32, 'a')])
    gr['w_in'] = _mm_tn_call("dw_in", xn[None], dproj[None])[0]
    return dh0, gr, arrived


def _layer_weights(w):
    wl = {}
    if 'w_in' in w:
        w_in = w['w_in'].reshape(D_MODEL, -1)
        z = lambda n: jnp.zeros((D_MODEL, n), w_in.dtype)
        wl['w_in'] = jnp.concatenate([w_in[:, :384], z(64), w_in[:, 384:416], z(32), w_in[:, 416:]], axis=1)
    if 'w_uq' in w:
        wl['w_uq'] = jnp.pad(w['w_uq'], ((0, 0), (0, 0), (0, HEAD_W - QK_NOPE - QK_ROPE)))
    if 'w_ukv' in w:
        wl['w_k'] = jnp.pad(w['w_ukv'][..., :QK_NOPE], ((0, 0), (0, 0), (0, HEAD_W - QK_NOPE)))
        wv = w['w_ukv'][..., QK_NOPE:]
        even = (jnp.arange(MLA_HEADS) % 2 == 0)[:, None, None]
        wl['w_v'] = jnp.concatenate([jnp.where(even, wv, 0), jnp.where(even, 0, wv)], axis=-1).astype(wv.dtype)
    if 'ssm_w_glu' in w:
        wl['ssm_w_glu'] = w['ssm_w_glu'].reshape(SSM_WIDTH, SSM_WIDTH)
    for n in ('w_out', 'w_xq', 'w_xo'):
        if n in w:
            wl[n] = w[n].reshape(D_MODEL, D_MODEL)
    for n in ('w_xkv', 'w_gate', 'w_up', 'w_down'):
        if n in w:
            wl[n] = w[n]
    return wl


def _blocked(gr, n):
    if n == 'w_in':
        d = gr['w_in']
        out = jnp.concatenate([d[:, :384], d[:, 448:480], d[:, 512:]], axis=1).reshape(N_DEV, 128, -1)
    elif n == 'w_uq':
        out = gr['w_uq'][..., :QK_NOPE + QK_ROPE]
    elif n == 'w_ukv':
        even = (jnp.arange(MLA_HEADS) % 2 == 0)[:, None, None]
        dv = gr['w_v']
        out = jnp.concatenate([gr['w_k'][..., :QK_NOPE], jnp.where(even, dv[..., :V_DIM], dv[..., V_DIM:])], axis=-1)
    elif n == 'ssm_w_glu':
        out = gr['ssm_w_glu'].reshape(N_DEV, SSM_WIDTH // N_DEV, SSM_WIDTH)
    elif n in ('w_out', 'w_xq', 'w_xo'):
        out = gr[n].reshape(N_DEV, D_MODEL // N_DEV, D_MODEL)
    else:
        out = gr[n]
    return out.astype(MXU)


def kernel(x, mem, positions, norm_mix_g, w_in, q_norm_g, w_uq, kv_norm_g, w_ukv, ssm_lambda_re, ssm_lambda_im, ssm_log_dt, ssm_b_re, ssm_b_im, ssm_c_re, ssm_c_im, ssm_d, ssm_w_glu, ssm_b_glu, attn_out_g, ssm_out_g, w_out, norm_x_g, mem_norm_g, w_xq, w_xkv, w_xo, norm_ffn_g, w_gate, w_up, w_down, final_norm_g, loss_target, m_norm_mix_g, m_w_in, m_q_norm_g, m_w_uq, m_kv_norm_g, m_w_ukv, m_ssm_lambda_re, m_ssm_lambda_im, m_ssm_log_dt, m_ssm_b_re, m_ssm_b_im, m_ssm_c_re, m_ssm_c_im, m_ssm_d, m_ssm_w_glu, m_ssm_b_glu, m_attn_out_g, m_ssm_out_g, m_w_out, m_norm_x_g, m_mem_norm_g, m_w_xq, m_w_xkv, m_w_xo, m_norm_ffn_g, m_w_gate, m_w_up, m_w_down, m_final_norm_g, v_norm_mix_g, v_w_in, v_q_norm_g, v_w_uq, v_kv_norm_g, v_w_ukv, v_ssm_lambda_re, v_ssm_lambda_im, v_ssm_log_dt, v_ssm_b_re, v_ssm_b_im, v_ssm_c_re, v_ssm_c_im, v_ssm_d, v_ssm_w_glu, v_ssm_b_glu, v_attn_out_g, v_ssm_out_g, v_w_out, v_norm_x_g, v_mem_norm_g, v_w_xq, v_w_xkv, v_w_xo, v_norm_ffn_g, v_w_gate, v_w_up, v_w_down, v_final_norm_g):
    args = dict(locals())
    W = {n: args[n] for n in WEIGHTS}
    M = {n: args['m_' + n] for n in WEIGHTS}
    V = {n: args['v_' + n] for n in WEIGHTS}
    s = x.shape[1]
    h = x[0]
    memx = mem[0]

    freqs = ROPE_THETA ** (-jnp.arange(0, QK_ROPE, 2, dtype=F32) / QK_ROPE)
    ang = positions[0].astype(F32)[:, None] * freqs
    c16, s16 = jnp.cos(ang), jnp.sin(ang)
    cos = jnp.concatenate([jnp.ones((s, QK_NOPE), F32), c16, c16, jnp.zeros((s, 32), F32)], axis=1)
    sin = jnp.concatenate([jnp.zeros((s, QK_NOPE), F32), s16, s16, jnp.zeros((s, 32), F32)], axis=1)
    idx = jnp.arange(QK_ROPE // 2)
    pmat = jnp.zeros((HEAD_W, HEAD_W), F32)
    pmat = pmat.at[QK_NOPE + 16 + idx, QK_NOPE + idx].set(-1.0).at[QK_NOPE + idx, QK_NOPE + 16 + idx].set(1.0)
    tabs = (cos, sin, pmat, pmat.T)

    shards = [{n: W[n][l].astype(MXU) for n in SHARDED} for l in range(DEPTH)]
    gathered = dict(zip(EARLY, _run_exchange("gather_weights", _gather(_named(EARLY, shards[0])))))

    layers = []
    for l in range(DEPTH):
        wl = _layer_weights(gathered)
        s5_in = [W[n][l] for n in ('ssm_lambda_re', 'ssm_lambda_im', 'ssm_log_dt', 'ssm_b_re', 'ssm_b_im',
                                   'ssm_c_re', 'ssm_c_im')]
        (a_re, a_im, bre, bim, cre, cim), s5_vjp = jax.vjp(_s5_params, *s5_in)
        pl_ = {n: W[n][l][None] for n in ('norm_mix_g', 'q_norm_g', 'kv_norm_g', 'ssm_d', 'ssm_b_glu',
                                           'attn_out_g', 'ssm_out_g', 'norm_x_g', 'mem_norm_g', 'norm_ffn_g')}
        pl_.update(a_re=a_re, a_im=a_im, b_re=bre, b_im=bim, c_re=cre, c_im=cim)
        h, sv, wl, gathered = _layer_fwd(h, memx, tabs, wl, pl_, shards[0] if l == 0 else None,
                                         shards[l + 1] if l + 1 < DEPTH else None)
        layers.append((wl, pl_, sv, s5_vjp))

    def f_loss(h_, tgt, g):
        y, r = _rms(h_, g[...])
        err = y - tgt
        part = 0.5 * jnp.sum(jnp.mean(err * err, axis=-1, keepdims=True), axis=0, keepdims=True)
        dh, dg = _rms_bwd(h_, g[...], r, err / D_MODEL)
        return dh, dg, jnp.broadcast_to(part, (8, LANES))
    dh, g_final, loss_part = _rows(
        "loss_head", f_loss, s, min(256, s), [(h, 'r0'), (loss_target[0], 'r0'), (final_norm_g[None], 'f')],
        [((s, D_MODEL), F32, 'r0'), ((1, D_MODEL), F32, 'a'), ((8, LANES), F32, 'a')])
    loss = lax.psum(loss_part[0, 0], ("x", "y", "c"))

    parts = [{} for _ in range(DEPTH)]
    g_rep = [None] * DEPTH
    blocks = None
    for l in reversed(range(DEPTH)):
        wl, pl_, sv, s5_vjp = layers[l]
        dh, gr, arrived = _layer_bwd(dh, sv, memx, tabs, wl, pl_, blocks, own=(l == 0))
        for (who, n), p in arrived.items():
            parts[l + 1 if who == 'nxt' else l][n] = p
        blocks = {n: _blocked(gr, n) for n in SHARDED} if l > 0 else None
        ds5 = s5_vjp((gr['a_re'], gr['a_im'], gr['b_re'], gr['b_im'], gr['c_re'], gr['c_im']))
        rep = dict(zip(('ssm_lambda_re', 'ssm_lambda_im', 'ssm_log_dt', 'ssm_b_re', 'ssm_b_im', 'ssm_c_re',
                        'ssm_c_im'), ds5))
        for n in ('norm_mix_g', 'q_norm_g', 'kv_norm_g', 'ssm_d', 'ssm_b_glu', 'attn_out_g', 'ssm_out_g',
                  'norm_x_g', 'mem_norm_g', 'norm_ffn_g'):
            rep[n] = gr[n][0]
        g_rep[l] = rep
    grad_x = dh[None]

    rest = [n for n in SHARDED if n not in parts[0]]
    parts[0].update(zip(rest, _run_exchange("scatter_grads", _scatter([_blocked(gr, n) for n in rest]))))
    out_sh = [{}, {}, {}, {}]
    for n in SHARDED:
        res = _adamw_weight("adamw_" + n, [parts[l][n] for l in range(DEPTH)], W[n], M[n], V[n])
        for kind, r in enumerate(res):
            out_sh[kind][n] = r

    tr_rp = 256
    rep_names = REPL_L + ['final_norm_g']
    g_loc = {n: jnp.stack([g_rep[l][n] for l in range(DEPTH)]) for n in REPL_L}
    g_loc['final_norm_g'] = g_final[0]
    pk_r = lambda d: _pack([d[n].reshape(1, -1) for n in rep_names], tr_rp)[0]
    parts_r, = _run_exchange("gather_small_grads", _gather([pk_r(g_loc)]))
    res_rp = _adamw("adamw_replicated", parts_r, pk_r(W), pk_r(M), pk_r(V), tr_rp)
    shapes_rp = [W[n].shape for n in rep_names]
    out_rp = [dict(zip(rep_names, _unpack(r, shapes_rp))) for r in res_rp]

    outs = [loss, grad_x]
    for kind in range(4):
        for n in WEIGHTS:
            outs.append(out_sh[kind][n] if n in SHARDED else out_rp[kind][n])
    return tuple(outs)
```

```python
from typing import Callable, NamedTuple

import jax
import jax.numpy as jnp
from jax import lax
from jax.experimental import pallas as pl
from jax.experimental.pallas import tpu as pltpu

F32 = jnp.float32
MXU = jnp.bfloat16
HI = lax.Precision.HIGHEST

D_MODEL = 1024
MLA_HEADS = 8
QK_NOPE = 64
QK_ROPE = 32
V_DIM = 64
Q_LORA = 256
KV_LORA = 128
SSM_WIDTH = 512
SSM_GROUPS = 32
SSM_GROUP = 16
SSM_STATE = 64
X_HEADS = 4
X_HEAD_DIM = 256
D_FF = 2816
FF_BLK = D_FF // 8
ROPE_THETA = 10000.0
EPS = 1e-6
DEPTH = 2
N_DEV = 8
LANES = 128
HEAD_W = 128
MLA_SCALE = (QK_NOPE + QK_ROPE) ** -0.5
X_SCALE = X_HEAD_DIM ** -0.5
ADAM_LR, ADAM_B1, ADAM_B2, ADAM_EPS, ADAM_WD, ADAM_STEP = 0.001, 0.9, 0.999, 1e-08, 0.01, 10
VMEM_LIMIT = 56 * 1024 * 1024
FLASH_TILE = 512
MESH = pl.DeviceIdType.MESH

SHARDED = ['w_in', 'w_uq', 'w_ukv', 'ssm_w_glu', 'w_out', 'w_xq', 'w_xkv', 'w_xo', 'w_gate', 'w_up', 'w_down']
REPL_L = ['norm_mix_g', 'q_norm_g', 'kv_norm_g', 'ssm_lambda_re', 'ssm_lambda_im', 'ssm_log_dt', 'ssm_b_re',
          'ssm_b_im', 'ssm_c_re', 'ssm_c_im', 'ssm_d', 'ssm_b_glu', 'attn_out_g', 'ssm_out_g', 'norm_x_g',
          'mem_norm_g', 'norm_ffn_g']
WEIGHTS = ['norm_mix_g', 'w_in', 'q_norm_g', 'w_uq', 'kv_norm_g', 'w_ukv', 'ssm_lambda_re', 'ssm_lambda_im',
           'ssm_log_dt', 'ssm_b_re', 'ssm_b_im', 'ssm_c_re', 'ssm_c_im', 'ssm_d', 'ssm_w_glu', 'ssm_b_glu',
           'attn_out_g', 'ssm_out_g', 'w_out', 'norm_x_g', 'mem_norm_g', 'w_xq', 'w_xkv', 'w_xo', 'norm_ffn_g',
           'w_gate', 'w_up', 'w_down', 'final_norm_g']


def _pcall(body, **kw):
    return pl.pallas_call(body, **kw)


def _mm(a, b):
    return jnp.dot(a.astype(MXU), b.astype(MXU), preferred_element_type=F32)


def _mm_nt(a, b):
    return lax.dot_general(a.astype(MXU), b.astype(MXU), (((1,), (1,)), ((), ())), preferred_element_type=F32)


def _mm_tn(a, b):
    return lax.dot_general(a.astype(MXU), b.astype(MXU), (((0,), (0,)), ((), ())), preferred_element_type=F32)


def _mm_hi(a, b):
    return jnp.dot(a.astype(F32), b.astype(F32), precision=HI, preferred_element_type=F32)


def _rms(x, g):
    r = lax.rsqrt(jnp.mean(x * x, axis=-1, keepdims=True) + EPS)
    return x * r * g, r


def _rms_bwd(x, g, r, dy):
    dyg = dy * g
    dx = r * dyg - x * (r * r * r) * jnp.mean(dyg * x, axis=-1, keepdims=True)
    return dx, jnp.sum(dy * x * r, axis=0, keepdims=True)


def _rope(x, cos, sin, p_ref):
    return x * cos + _mm_hi(x, p_ref[...]) * sin


def _rope_t(g, cos, sin, pt_ref):
    return g * cos + _mm_hi(g * sin, pt_ref[...])


def _softmax(s):
    m = jnp.max(s, axis=-1, keepdims=True)
    e = jnp.exp(s - m)
    return e / jnp.sum(e, axis=-1, keepdims=True)


def _lanes(x, j, w):
    return x[:, j * w:(j + 1) * w]


def _rows(name, fn, n, tm, ins, outs, side=None):
    def spec(shape, kind):
        nd = len(shape)
        if kind == 'p8':
            return pl.BlockSpec((shape[0], 8, shape[2]), lambda i: (0, jnp.maximum(i * (tm // 8) - 1, 0), 0))
        if kind == 'f':
            return pl.BlockSpec(shape, lambda i, _nd=nd: (0,) * _nd, pipeline_mode=pl.Buffered(1))
        if kind == 'a':
            return pl.BlockSpec(shape, lambda i, _nd=nd: (0,) * _nd)
        ax = int(kind[1])
        blk = tuple(tm if d == ax else s for d, s in enumerate(shape))
        return pl.BlockSpec(blk, lambda i, _ax=ax, _nd=nd: tuple(i if d == _ax else 0 for d in range(_nd)))

    n_in, n_out, n_steps = len(ins), len(outs), n // tm

    def body(*refs):
        in_refs, out_refs, steps = _side_split(refs, n_in, n_out, side)
        i = pl.program_id(0)
        if steps:
            pl.when(i == 0)(steps[0])
            pl.when(i == _pass_on_step(n_steps))(steps[1])
        args = [r if k == 'f' else r[...] for r, (_, k) in zip(in_refs, ins)]
        res = fn(*args)
        for r, (_, dt, k), v in zip(out_refs, outs, res):
            if k == 'a':
                _accumulate(r, v.astype(dt), i)
            else:
                r[...] = v.astype(dt)
        if steps:
            pl.when(i == n_steps - 1)(steps[2])

    s_in, s_out, s_shape, s_sems, s_ops = _side_args(side)
    res = _pcall(
        body, name=name + ("_x" if side else ""), grid=(n_steps,),
        in_specs=[spec(a.shape, k) for a, k in ins] + s_in,
        out_specs=[spec(s, k) for s, _, k in outs] + s_out,
        out_shape=[jax.ShapeDtypeStruct(s, dt) for s, dt, _ in outs] + s_shape,
        scratch_shapes=s_sems,
        compiler_params=pltpu.CompilerParams(dimension_semantics=("arbitrary",), vmem_limit_bytes=VMEM_LIMIT),
    )(*[a for a, _ in ins], *s_ops)
    return _Hosted(res[:n_out], res[n_out:]) if side else res


def _accumulate(ref, v, i):
    @pl.when(i == 0)
    def _():
        ref[...] = v

    @pl.when(i != 0)
    def _():
        ref[...] += v


def _mm_tn_call(name, a, b):
    out_dtype = MXU
    ba, s, k = a.shape
    bb, _, n = b.shape
    nb = max(ba, bb)
    ts = min(512, s)
    ns = s // ts

    def body(a_ref, b_ref, o_ref, acc_ref):
        j = pl.program_id(1)
        _accumulate(acc_ref, _mm_tn(a_ref[...], b_ref[...]), j)

        @pl.when(j == ns - 1)
        def _():
            o_ref[...] = acc_ref[...].astype(out_dtype)

    return _pcall(
        body, name=name, grid=(nb, ns),
        in_specs=[pl.BlockSpec((None, ts, k), (lambda i, j: (i, j, 0)) if ba > 1 else (lambda i, j: (0, j, 0))),
                  pl.BlockSpec((None, ts, n), (lambda i, j: (i, j, 0)) if bb > 1 else (lambda i, j: (0, j, 0)))],
        out_specs=pl.BlockSpec((None, k, n), lambda i, j: (i, 0, 0)),
        out_shape=jax.ShapeDtypeStruct((nb, k, n), out_dtype),
        scratch_shapes=[pltpu.VMEM((k, n), F32)],
        compiler_params=pltpu.CompilerParams(dimension_semantics=("arbitrary", "arbitrary"),
                                             vmem_limit_bytes=VMEM_LIMIT),
    )(a, b)


def _side_split(refs, n_in, n_out, side):
    if side is None:
        return refs[:n_in], refs[n_in:n_in + n_out], None
    si, so = len(side.ins), len(side.out_shapes)
    own_in, side_in = refs[:n_in], refs[n_in:n_in + si]
    own_out, side_out = refs[n_in + si:n_in + si + n_out], refs[n_in + si + n_out:n_in + si + n_out + so]
    return own_in, own_out, side.steps(side_in, side_out, refs[n_in + si + n_out + so:])


def _pass_on_step(n_steps):
    return max(n_steps - 2, 0)


def _side_args(side):
    if side is None:
        return [], [], [], [], []
    any_spec = pl.BlockSpec(memory_space=pl.ANY)
    return ([any_spec] * len(side.ins), [any_spec] * len(side.out_shapes), list(side.out_shapes),
            list(side.sem_shapes), list(side.ins))


class _Hosted(NamedTuple):
    results: list
    arrived: list


def _hosted(res):
    return res if isinstance(res, _Hosted) else _Hosted(res, ())


def _flash_fwd(q, k, v, side=None):
    nh, s, w = q.shape
    t = min(FLASH_TILE, s)
    nq = s // t
    n_steps = (nh // 2) * nq

    def body(*refs):
        (q_ref, k_ref, v_ref), (o_ref, lse_ref), steps = _side_split(refs, 3, 2, side)
        step = pl.program_id(0) * nq + pl.program_id(1)
        if steps:
            pl.when(step == 0)(steps[0])
            pl.when(step == _pass_on_step(n_steps))(steps[1])
        qi = pl.program_id(1)
        qs = [q_ref[0], q_ref[1]]
        below = lax.broadcasted_iota(jnp.int32, (t, t), 1) <= lax.broadcasted_iota(jnp.int32, (t, t), 0)

        def tile(j, carry, diagonal):
            sl = pl.ds(pl.multiple_of(j * t, t), t)
            out = []
            for hh in range(2):
                m, l, acc = carry[3 * hh:3 * hh + 3]
                sc = _mm_nt(qs[hh], k_ref[hh, sl, :])
                if diagonal:
                    sc = jnp.where(below, sc, -1e30)
                m_new = jnp.maximum(m, jnp.max(sc, axis=1, keepdims=True))
                p = jnp.exp(sc - m_new)
                alpha = jnp.exp(m - m_new)
                out += [m_new, alpha * l + jnp.sum(p, axis=1, keepdims=True), alpha * acc + _mm(p, v_ref[hh, sl, :])]
            return tuple(out)

        init = (jnp.full((t, 1), -1e30, F32), jnp.zeros((t, 1), F32), jnp.zeros((t, w), F32)) * 2
        carry = lax.fori_loop(0, qi, lambda j, c: tile(j, c, False), init)
        carry = tile(qi, carry, True)
        o_ref[...] = carry[2] / carry[1] + carry[5] / carry[4]
        for hh in range(2):
            lse_ref[hh] = jnp.broadcast_to(carry[3 * hh] + jnp.log(carry[3 * hh + 1]), (t, w))
        if steps:
            pl.when(step == n_steps - 1)(steps[2])

    s_in, s_out, s_shape, s_sems, s_ops = _side_args(side)
    res = _pcall(
        body, name="mla_flash_fwd" + ("_x" if side else ""), grid=(nh // 2, nq),
        in_specs=[pl.BlockSpec((2, t, w), lambda p, i: (p, i, 0)),
                  pl.BlockSpec((2, s, w), lambda p, i: (p, 0, 0)),
                  pl.BlockSpec((2, s, w), lambda p, i: (p, 0, 0))] + s_in,
        out_specs=[pl.BlockSpec((t, w), lambda p, i: (i, p)),
                   pl.BlockSpec((2, t, w), lambda p, i: (p, i, 0))] + s_out,
        out_shape=[jax.ShapeDtypeStruct((s, (nh // 2) * w), F32), jax.ShapeDtypeStruct((nh, s, w), F32)] + s_shape,
        scratch_shapes=s_sems,
        compiler_params=pltpu.CompilerParams(dimension_semantics=("arbitrary", "arbitrary"),
                                             vmem_limit_bytes=VMEM_LIMIT),
    )(q, k, v, *s_ops)
    return res[0], res[1], res[2:]


def _flash_bwd(q, k, v, o, lse, do, side=None):
    nh, s, w = q.shape
    t = min(FLASH_TILE, s)
    nq = s // t
    n_steps = (nh // 2) * nq

    def body(*refs):
        (q_ref, k_ref, v_ref, o_ref, lse_ref, do_ref), (dq_ref, dk_ref, dv_ref), steps = _side_split(refs, 6, 3, side)
        step = pl.program_id(0) * nq + pl.program_id(1)
        if steps:
            pl.when(step == 0)(steps[0])
            pl.when(step == _pass_on_step(n_steps))(steps[1])
        j = pl.program_id(1)

        @pl.when(j == 0)
        def _():
            dq_ref[...] = jnp.zeros(dq_ref.shape, F32)

        below = lax.broadcasted_iota(jnp.int32, (t, t), 1) <= lax.broadcasted_iota(jnp.int32, (t, t), 0)
        lane = lax.broadcasted_iota(jnp.int32, (t, w), 1)
        heads = [jnp.logical_and(lane >= hh * V_DIM, lane < (hh + 1) * V_DIM) for hh in range(2)]
        ks = [k_ref[0], k_ref[1]]
        vs = [v_ref[0], v_ref[1]]

        def tile(i, carry, diagonal):
            sl = pl.ds(pl.multiple_of(i * t, t), t)
            dout_all, o_all = do_ref[sl, :], o_ref[sl, :]
            out = []
            for hh in range(2):
                dk, dv = carry[2 * hh], carry[2 * hh + 1]
                qh = q_ref[hh, sl, :]
                dout = jnp.where(heads[hh], dout_all, 0.0)
                sc = _mm_nt(qh, ks[hh])
                if diagonal:
                    sc = jnp.where(below, sc, -1e30)
                p = jnp.exp(sc - lse_ref[hh, sl, 0:1])
                dp = _mm_nt(dout, vs[hh])
                ds = p * (dp - jnp.sum(dout * o_all, axis=1, keepdims=True))
                dq_ref[hh, sl, :] += _mm(ds, ks[hh])
                out += [dk + _mm_tn(ds, qh), dv + _mm_tn(p, dout)]
            return tuple(out)

        carry = tile(j, (jnp.zeros((t, w), F32),) * 4, True)
        carry = lax.fori_loop(j + 1, nq, lambda i, c: tile(i, c, False), carry)
        for hh in range(2):
            dk_ref[hh] = carry[2 * hh]
            dv_ref[hh] = jnp.where(heads[hh], carry[2 * hh + 1], 0.0)
        if steps:
            pl.when(step == n_steps - 1)(steps[2])

    s_in, s_out, s_shape, s_sems, s_ops = _side_args(side)
    res = _pcall(
        body, name="mla_flash_bwd" + ("_x" if side else ""), grid=(nh // 2, nq),
        in_specs=[pl.BlockSpec((2, s, w), lambda p, j: (p, 0, 0)),
                  pl.BlockSpec((2, t, w), lambda p, j: (p, j, 0)),
                  pl.BlockSpec((2, t, w), lambda p, j: (p, j, 0)),
                  pl.BlockSpec((s, w), lambda p, j: (0, p)),
                  pl.BlockSpec((2, s, w), lambda p, j: (p, 0, 0)),
                  pl.BlockSpec((s, w), lambda p, j: (0, p))] + s_in,
        out_specs=[pl.BlockSpec((2, s, w), lambda p, j: (p, 0, 0)),
                   pl.BlockSpec((2, t, w), lambda p, j: (p, j, 0)),
                   pl.BlockSpec((2, t, w), lambda p, j: (p, j, 0))] + s_out,
        out_shape=[jax.ShapeDtypeStruct((nh, s, w), F32)] * 3 + s_shape,
        scratch_shapes=s_sems,
        compiler_params=pltpu.CompilerParams(dimension_semantics=("arbitrary", "arbitrary"),
                                             vmem_limit_bytes=VMEM_LIMIT),
    )(q, k, v, o, lse, do, *s_ops)
    return res[0], res[1], res[2], res[3:]


def _scan(b_re, b_im, a_re, a_im, reverse):
    nb, s, w = b_re.shape
    ch = s // 8
    assert ch & (ch - 1) == 0
    grp = 2

    def cmul(ar, ai, xr, xi):
        return ar * xr - ai * xi, ar * xi + ai * xr

    def body(br_ref, bi_ref, ar_ref, ai_ref, xr_ref, xi_ref):
        sub = lax.broadcasted_iota(jnp.int32, (8, w), 0)

        def shift(x, k):
            if reverse:
                return jnp.where(sub < 8 - k, pltpu.roll(x, 8 - k, 0), 0.0)
            return jnp.where(sub >= k, pltpu.roll(x, k, 0), 0.0)

        ar = [jnp.broadcast_to(ar_ref[g], (8, w)) for g in range(grp)]
        ai = [jnp.broadcast_to(ai_ref[g], (8, w)) for g in range(grp)]

        def tsl(i):
            return pl.ds(pl.multiple_of(((ch - 1 - i) if reverse else i) * 8, 8), 8)

        def local(i, carry):
            out = []
            for g in range(grp):
                xr, xi = carry[2 * g], carry[2 * g + 1]
                pr, pi = cmul(ar[g], ai[g], xr, xi)
                nr = pr + br_ref[g, tsl(i), :]
                ni = pi + bi_ref[g, tsl(i), :]
                xr_ref[g, tsl(i), :] = nr
                xi_ref[g, tsl(i), :] = ni
                out += [nr, ni]
            return tuple(out)

        fin = lax.fori_loop(0, ch, local, (jnp.zeros((8, w), F32),) * (2 * grp))

        carry_in = []
        for g in range(grp):
            pr, pi = ar[g], ai[g]
            for _ in range(ch.bit_length() - 1):
                pr, pi = cmul(pr, pi, pr, pi)
            fr, fi = fin[2 * g], fin[2 * g + 1]
            for kk in (1, 2, 4):
                sr, si = cmul(pr, pi, shift(fr, kk), shift(fi, kk))
                fr, fi = fr + sr, fi + si
                pr, pi = cmul(pr, pi, pr, pi)
            carry_in += [shift(fr, 1), shift(fi, 1)]

        def fix(i, pw):
            out = []
            for g in range(grp):
                pr, pi = pw[2 * g], pw[2 * g + 1]
                cr, ci = cmul(pr, pi, carry_in[2 * g], carry_in[2 * g + 1])
                xr_ref[g, tsl(i), :] = xr_ref[g, tsl(i), :] + cr
                xi_ref[g, tsl(i), :] = xi_ref[g, tsl(i), :] + ci
                nr, ni = cmul(pr, pi, ar[g], ai[g])
                out += [nr, ni]
            return tuple(out)

        lax.fori_loop(0, ch, fix, tuple(x for g in range(grp) for x in (ar[g], ai[g])))

    blk = pl.BlockSpec((grp, s, w), lambda i: (i, 0, 0))
    ablk = pl.BlockSpec((grp, 1, w), lambda i: (i, 0, 0))
    return _pcall(
        body, name="s5_scan_rev" if reverse else "s5_scan", grid=(nb // grp,),
        in_specs=[blk, blk, ablk, ablk], out_specs=[blk, blk],
        out_shape=[jax.ShapeDtypeStruct((nb, s, w), F32)] * 2,
        compiler_params=pltpu.CompilerParams(dimension_semantics=("arbitrary",), vmem_limit_bytes=VMEM_LIMIT),
    )(b_re, b_im, a_re, a_im)


class _Exchange(NamedTuple):
    ins: list
    out_shapes: list
    sem_shapes: list
    steps: Callable


def _gather_steps(ins, outs, sems):
    n = len(ins)
    send_sems, recv_sems, local_sems = sems
    x, y, c = lax.axis_index("x"), lax.axis_index("y"), lax.axis_index("c")
    me, sibling = (x, y, c), (x, y, 1 - c)
    chips = [(1 - x, y), (x, 1 - y), (1 - x, 1 - y)]

    def copy(a, k, block, to, src=None):
        dst = outs[a].at[4 * block[0] + 2 * block[1] + block[2]]
        return pltpu.make_async_remote_copy(
            src_ref=dst if src is None else src, dst_ref=dst,
            send_sem=send_sems.at[a, k], recv_sem=recv_sems.at[a, k], device_id=to, device_id_type=MESH)

    mine = [pltpu.make_async_copy(ins[a], outs[a].at[4 * x + 2 * y + c], local_sems.at[a]) for a in range(n)]
    first = []
    for a in range(n):
        first.append(copy(a, 0, me, sibling, src=ins[a]))
        first += [copy(a, 1 + j, me, (*chip, c), src=ins[a]) for j, chip in enumerate(chips)]
    passed = [copy(a, 4 + j, (*chip, c), sibling) for j, chip in enumerate(chips) for a in range(n)]

    def start():
        for cp in mine + first:
            cp.start()

    def pass_on():
        i = 0
        for j, chip in enumerate(chips):
            for a in range(n):
                copy(a, 1 + j, (*chip, c), me).wait_recv()
                passed[i].start()
                i += 1

    def finish():
        for a in range(n):
            copy(a, 0, sibling, me).wait_recv()
            for j, chip in enumerate(chips):
                copy(a, 4 + j, (*chip, 1 - c), me).wait_recv()
        for cp in first + passed:
            cp.wait_send()
        for cp in mine:
            cp.wait()

    return start, pass_on, finish


def _gather(arrs):
    n = len(arrs)
    return _Exchange(list(arrs), [jax.ShapeDtypeStruct((N_DEV,) + a.shape, a.dtype) for a in arrs],
                     [pltpu.SemaphoreType.DMA((n, 7)), pltpu.SemaphoreType.DMA((n, 7)), pltpu.SemaphoreType.DMA((n,))],
                     _gather_steps)


def _scatter_steps(ins, outs, sems):
    n = len(ins)
    send_sems, recv_sems, local_sems = sems
    x, y, c = lax.axis_index("x"), lax.axis_index("y"), lax.axis_index("c")
    me = 4 * x + 2 * y + c
    own, sent, arrivals = [], [], []
    for a in range(n):
        own.append(pltpu.make_async_copy(ins[a].at[me], outs[a].at[me], local_sems.at[a]))
        for k in range(1, N_DEV):
            px, py, pc = x ^ ((k >> 2) & 1), y ^ ((k >> 1) & 1), c ^ (k & 1)
            peer = 4 * px + 2 * py + pc
            sent.append(pltpu.make_async_remote_copy(
                src_ref=ins[a].at[peer], dst_ref=outs[a].at[me],
                send_sem=send_sems.at[a, k - 1], recv_sem=recv_sems.at[a, k - 1],
                device_id=(px, py, pc), device_id_type=MESH))
            arrivals.append(pltpu.make_async_remote_copy(
                src_ref=ins[a].at[me], dst_ref=outs[a].at[peer],
                send_sem=send_sems.at[a, k - 1], recv_sem=recv_sems.at[a, k - 1],
                device_id=(x, y, c), device_id_type=MESH))

    def start():
        for cp in own + sent:
            cp.start()

    def pass_on():
        pass

    def finish():
        for cp in arrivals:
            cp.wait_recv()
        for cp in sent:
            cp.wait_send()
        for cp in own:
            cp.wait()

    return start, pass_on, finish


def _scatter(grads):
    n = len(grads)
    return _Exchange(list(grads), [jax.ShapeDtypeStruct(g.shape, g.dtype) for g in grads],
                     [pltpu.SemaphoreType.DMA((n, N_DEV - 1)), pltpu.SemaphoreType.DMA((n, N_DEV - 1)),
                      pltpu.SemaphoreType.DMA((n,))], _scatter_steps)


def _together(a, b):
    def steps(ins, outs, sems):
        sa = a.steps(ins[:len(a.ins)], outs[:len(a.out_shapes)], sems[:len(a.sem_shapes)])
        sb = b.steps(ins[len(a.ins):], outs[len(a.out_shapes):], sems[len(a.sem_shapes):])

        def both(k):
            def run():
                sa[k]()
                sb[k]()
            return run
        return both(0), both(1), both(2)

    return _Exchange(a.ins + b.ins, a.out_shapes + b.out_shapes, a.sem_shapes + b.sem_shapes, steps)


def _run_exchange(name, ex):
    n_in, n_out = len(ex.ins), len(ex.out_shapes)

    def body(*refs):
        for step in ex.steps(refs[:n_in], refs[n_in:n_in + n_out], refs[n_in + n_out:]):
            step()

    any_spec = pl.BlockSpec(memory_space=pl.ANY)
    return _pcall(body, name=name, in_specs=[any_spec] * n_in, out_specs=[any_spec] * n_out,
                  out_shape=list(ex.out_shapes), scratch_shapes=list(ex.sem_shapes))(*ex.ins)


def _adam_math(g, w_, m_, v_):
    m_new = ADAM_B1 * m_ + (1.0 - ADAM_B1) * g
    v_new = ADAM_B2 * v_ + (1.0 - ADAM_B2) * (g * g)
    m_hat = m_new / (1.0 - ADAM_B1 ** ADAM_STEP)
    v_hat = v_new / (1.0 - ADAM_B2 ** ADAM_STEP)
    delta = -ADAM_LR * (m_hat / (jnp.sqrt(v_hat) + ADAM_EPS) + ADAM_WD * w_)
    return delta, m_new, v_new


def _adamw_weight(name, parts, w, m, v):
    nl = len(parts)

    def body(*refs):
        p_refs = refs[:nl]
        w_ref, m_ref, v_ref, g_ref, d_ref, mo_ref, vo_ref = refs[nl:]
        for l in range(nl):
            g = p_refs[l][0].astype(F32)
            for j in range(1, N_DEV):
                g = g + p_refs[l][j].astype(F32)
            g_ref[l] = g
            d_ref[l], mo_ref[l], vo_ref[l] = _adam_math(g, w_ref[l], m_ref[l], v_ref[l])

    return _pcall(
        body, name=name, out_shape=[jax.ShapeDtypeStruct(w.shape, F32)] * 4,
        compiler_params=pltpu.CompilerParams(vmem_limit_bytes=VMEM_LIMIT),
    )(*parts, w, m, v)

def _sum_sources(name, parts):
    r = parts.shape[1]

    def body(p_ref, g_ref):
        g = p_ref[0]
        for j in range(1, N_DEV):
            g = g + p_ref[j]
        g_ref[...] = g

    return _pcall(body, name=name, out_shape=jax.ShapeDtypeStruct((r, LANES), F32),
                  compiler_params=pltpu.CompilerParams(vmem_limit_bytes=VMEM_LIMIT))(parts)


def _adamw_small(name, g, w, m, v):
    n = len(g)

    def body(*refs):
        g_r, w_r, m_r, v_r = (refs[k * n:(k + 1) * n] for k in range(4))
        d_r, mo_r, vo_r = (refs[k * n:(k + 1) * n] for k in range(4, 7))
        for i in range(n):
            d_r[i][...], mo_r[i][...], vo_r[i][...] = _adam_math(g_r[i][...], w_r[i][...], m_r[i][...], v_r[i][...])

    res = _pcall(body, name=name, out_shape=[jax.ShapeDtypeStruct(a.shape, F32) for a in w] * 3,
                 compiler_params=pltpu.CompilerParams(vmem_limit_bytes=VMEM_LIMIT))(*g, *w, *m, *v)
    return res[:n], res[n:2 * n], res[2 * n:]


def _pack(arrs):
    flat = jnp.concatenate([a.reshape(-1) for a in arrs])
    flat = jnp.pad(flat, (0, (-flat.shape[0]) % (8 * LANES)))
    return flat.reshape(-1, LANES)


def _unpack(packed, shapes):
    flat = packed.reshape(-1)
    out, off = [], 0
    for shp in shapes:
        size = 1
        for d in shp:
            size *= d
        out.append(flat[off:off + size].reshape(shp))
        off += size
    return out


def _s5_params(lam_re, lam_im, log_dt, b_re, b_im, c_re, c_im):
    dt = jnp.exp(log_dt)[:, None]
    e = jnp.exp(lam_re * dt)
    ang = lam_im * dt
    a_re, a_im = e * jnp.cos(ang), e * jnp.sin(ang)
    nr, ni = a_re - 1.0, a_im
    den = lam_re * lam_re + lam_im * lam_im
    cr = ((nr * lam_re + ni * lam_im) / den)[..., None]
    ci = ((ni * lam_re - nr * lam_im) / den)[..., None]
    bb_re = cr * b_re - ci * b_im
    bb_im = cr * b_im + ci * b_re
    eye = jnp.eye(8, dtype=F32)[None, :, None, :, None]

    def bblk(bb):
        t = jnp.transpose(bb.reshape(4, 8, SSM_STATE, SSM_GROUP), (0, 3, 1, 2))
        return (eye * t[:, None]).reshape(4, 8 * SSM_GROUP, 8 * SSM_STATE)

    def cblk(cc):
        t = jnp.transpose(cc.reshape(4, 8, SSM_GROUP, SSM_STATE), (0, 3, 1, 2))
        return (eye * t[:, None]).reshape(4, 8 * SSM_STATE, 8 * SSM_GROUP)

    nb = SSM_GROUPS * SSM_STATE // LANES
    return (a_re.reshape(nb, 1, LANES), a_im.reshape(nb, 1, LANES), bblk(bb_re), bblk(bb_im),
            cblk(c_re), -cblk(c_im))


def _cat_blocks(x3, j):
    return jnp.concatenate([x3[4 * j + k] for k in range(4)], axis=-1)


def _to_chunks(a):
    s, c = a.shape
    return a.reshape(8, s // 8, c).transpose(1, 0, 2).reshape(s, c)


def _from_chunks(a):
    s, c = a.shape
    return a.reshape(s // 8, 8, c).transpose(1, 0, 2).reshape(s, c)


EARLY = ['w_in', 'w_uq', 'w_ukv']
MIDDLE = ['ssm_w_glu', 'w_out', 'w_xq', 'w_xkv', 'w_xo']
FEED = ['w_gate', 'w_up', 'w_down']


def _named(names, d):
    return [d[n] for n in names]


def _layer_fwd(h, memx, tabs, wl, pl_, late=None, nxt=None):
    s = h.shape[0]
    tm = min(256, s)
    cos, sin, pmat, pmat_t = tabs
    sv = {}
    wl = dict(wl)

    def f_mix_in(h_, g, w):
        xn, _ = _rms(h_, g[...])
        return (_mm(xn, w[...]),)
    proj, = _rows("mix_in", f_mix_in, s, tm, [(h, 'r0'), (pl_['norm_mix_g'], 'f'), (wl['w_in'], 'f')],
                  [((s, D_MODEL), F32, 'r0')])

    def f_qkv(pr, cos_, sin_, gq, gkv, wq, wk, wv, pm):
        cqn = _rms(pr[:, 0:Q_LORA], gq[...])[0].astype(MXU)
        kvn = _rms(pr[:, Q_LORA:Q_LORA + KV_LORA], gkv[...])[0].astype(MXU)
        krr = _rope(pr[:, 384:512], cos_, sin_, pm)
        qs, ks, vs = [], [], []
        for hd in range(MLA_HEADS):
            qs.append(_rope(_mm(cqn, wq[hd]), cos_, sin_, pm) * MLA_SCALE)
            ks.append(_mm(kvn, wk[hd]) + krr)
            vs.append(_mm(kvn, wv[hd]))
        return jnp.stack(qs), jnp.stack(ks), jnp.stack(vs)
    hshape = (MLA_HEADS, s, HEAD_W)
    q, k, v = _rows("mla_qkv", f_qkv, s, tm,
                    [(proj, 'r0'), (cos, 'r0'), (sin, 'r0'), (pl_['q_norm_g'], 'f'), (pl_['kv_norm_g'], 'f'),
                     (wl['w_uq'], 'f'), (wl['w_k'], 'f'), (wl['w_v'], 'f'), (pmat, 'f')],
                    [(hshape, MXU, 'r1')] * 3)
    def fetch(names, src):
        return _gather(_named(names, src)) if src else None

    def landed(names, got):
        return _layer_weights(dict(zip(names, got))) if got else {}

    a_out, lse, got = _flash_fwd(q, k, v, fetch(MIDDLE + FEED[:1], late))
    wl.update(landed(MIDDLE + FEED[:1], got))

    u_ch = _to_chunks(proj[:, 512:1024])

    def f_s5_in(u, bre, bim):
        outs_r, outs_i = [], []
        for j in range(4):
            uj = _lanes(u, j, LANES)
            rr, ri = _mm_hi(uj, bre[j]), _mm_hi(uj, bim[j])
            outs_r += [_lanes(rr, kk, LANES) for kk in range(4)]
            outs_i += [_lanes(ri, kk, LANES) for kk in range(4)]
        return jnp.stack(outs_r), jnp.stack(outs_i)
    xshape = (16, s, LANES)
    (bu_re, bu_im), got = _hosted(_rows("s5_in", f_s5_in, s, tm,
                                        [(u_ch, 'r0'), (pl_['b_re'], 'f'), (pl_['b_im'], 'f')],
                                        [(xshape, F32, 'r1')] * 2, fetch(FEED[1:2], late)))
    wl.update(landed(FEED[1:2], got))
    x_re, x_im = _scan(bu_re, bu_im, pl_['a_re'], pl_['a_im'], False)

    def f_s5_out(xr, xi, u, cre, cim, d, wglu, bglu):
        y = jnp.concatenate([_mm_hi(_cat_blocks(xr, j), cre[j]) + _mm_hi(_cat_blocks(xi, j), cim[j])
                             for j in range(4)], axis=-1) + d[...] * u
        z = _mm(jax.nn.gelu(y), wglu[...]) + bglu[...]
        return y, y * jax.nn.sigmoid(z)
    (y_ssm, s_out_ch), got = _hosted(_rows(
        "s5_out", f_s5_out, s, tm,
        [(x_re, 'r1'), (x_im, 'r1'), (u_ch, 'r0'), (pl_['c_re'], 'f'), (pl_['c_im'], 'f'),
         (pl_['ssm_d'], 'f'), (wl['ssm_w_glu'], 'f'), (pl_['ssm_b_glu'], 'f')],
        [((s, SSM_WIDTH), F32, 'r0')] * 2, fetch(FEED[2:3], late)))
    wl.update(landed(FEED[2:3], got))
    s_out = _from_chunks(s_out_ch)

    def f_mix_out(h_, a, so, ga, gs, w):
        an = _rms(a, ga[...])[0]
        sn = _rms(so, gs[...])[0]
        return (h_ + _mm(jnp.concatenate([an, sn], axis=-1), w[...]),)
    (h1,), got_e = _hosted(_rows("mix_out", f_mix_out, s, tm,
                                 [(h, 'r0'), (a_out, 'r0'), (s_out, 'r0'), (pl_['attn_out_g'], 'f'),
                                  (pl_['ssm_out_g'], 'f'), (wl['w_out'], 'f')],
                                 [((s, D_MODEL), F32, 'r0')], fetch(EARLY, nxt)))

    m_len = memx.shape[0]

    def f_memkv(mm_, g, w):
        mn = _rms(mm_, g[...])[0].astype(MXU)
        return (jnp.stack([_mm(mn, w[d]) for d in range(N_DEV)]),)
    kvm, = _rows("mem_kv", f_memkv, m_len, m_len, [(memx, 'r0'), (pl_['mem_norm_g'], 'f'), (wl['w_xkv'], 'f')],
                 [((N_DEV, m_len, X_HEAD_DIM), MXU, 'r1')])

    def f_xattn(h_, g, wq, kv_, wo):
        hn = _rms(h_, g[...])[0].astype(MXU)
        out = jnp.zeros(h_.shape, F32)
        for hd in range(X_HEADS):
            cs = pl.ds(hd * X_HEAD_DIM, X_HEAD_DIM)
            qh = _mm(hn, wq[:, cs])
            p = _softmax(_mm_nt(qh, kv_[hd]) * X_SCALE)
            out = out + _mm(_mm(p, kv_[X_HEADS + hd]), wo[cs, :])
        return (h_ + out,)
    (h2,), got_a = _hosted(_rows("xattn", f_xattn, s, tm,
                                 [(h1, 'r0'), (pl_['norm_x_g'], 'f'), (wl['w_xq'], 'f'), (kvm, 'f'), (wl['w_xo'], 'f')],
                                 [((s, D_MODEL), F32, 'r0')], fetch(MIDDLE, nxt)))

    def f_ffn(h_, g, wg, wu, wd):
        hn = _rms(h_, g[...])[0].astype(MXU)
        y = jnp.zeros(h_.shape, F32)
        for d in range(N_DEV):
            gate = _mm(hn, wg[d])
            y = y + _mm(gate * jax.nn.sigmoid(gate) * _mm(hn, wu[d]), wd[d])
        return (h_ + y,)
    (h3,), got_b = _hosted(_rows("ffn", f_ffn, s, tm,
                                 [(h2, 'r0'), (pl_['norm_ffn_g'], 'f'), (wl['w_gate'], 'f'), (wl['w_up'], 'f'),
                                  (wl['w_down'], 'f')],
                                 [((s, D_MODEL), F32, 'r0')], fetch(FEED, nxt)))
    sv.update(h=h, proj=proj, q=q, k=k, v=v, a_out=a_out, lse=lse, x_re=x_re, x_im=x_im, y_ssm=y_ssm,
              s_out=s_out, h1=h1, kvm=kvm, h2=h2, u_ch=u_ch)
    return h3, sv, wl, dict(zip(EARLY + MIDDLE + FEED, tuple(got_e) + tuple(got_a) + tuple(got_b)))


def _layer_bwd(dh3, sv, memx, tabs, wl, pl_, nxt=None, own=False):
    s = dh3.shape[0]
    tm = min(256, s)
    cos, sin, pmat, pmat_t = tabs
    gr = {}
    arrived = {}
    act_shape = (N_DEV, s, FF_BLK)

    def send(*groups):
        keys = [(who, n) for who, names in groups if (nxt if who == 'nxt' else own) for n in names]
        if not keys:
            return None, keys
        return _scatter([nxt[n] if who == 'nxt' else _blocked(gr, n) for who, n in keys]), keys

    def f_ffn_bwd(h_, dy, g, wg, wu, wd):
        hn, r = _rms(h_, g[...])
        hb = hn.astype(MXU)
        dyb = dy.astype(MXU)
        dhn = jnp.zeros(h_.shape, F32)
        acts, dgs, dus = [], [], []
        for d in range(N_DEV):
            gate, up = _mm(hb, wg[d]), _mm(hb, wu[d])
            sg = jax.nn.sigmoid(gate)
            si = gate * sg
            dact = _mm_nt(dyb, wd[d])
            dgate = (dact * up * (sg * (1.0 + gate * (1.0 - sg)))).astype(MXU)
            dup = (dact * si).astype(MXU)
            dhn = dhn + _mm_nt(dgate, wg[d]) + _mm_nt(dup, wu[d])
            acts.append((si * up).astype(MXU))
            dgs.append(dgate)
            dus.append(dup)
        dh, dg = _rms_bwd(h_, g[...], r, dhn)
        return dy + dh, hb, jnp.stack(acts), jnp.stack(dgs), jnp.stack(dus), dg
    ex, keys = send(('nxt', ['w_gate', 'w_up']))
    (dh2, hn_f, act, dgate, dup, gr['norm_ffn_g']), got = _hosted(_rows(
        "ffn_bwd", f_ffn_bwd, s, tm,
        [(sv['h2'], 'r0'), (dh3, 'r0'), (pl_['norm_ffn_g'], 'f'), (wl['w_gate'], 'f'), (wl['w_up'], 'f'),
         (wl['w_down'], 'f')],
        [((s, D_MODEL), F32, 'r0'), ((s, D_MODEL), MXU, 'r0'), (act_shape, MXU, 'r1'), (act_shape, MXU, 'r1'),
         (act_shape, MXU, 'r1'), ((1, D_MODEL), F32, 'a')], ex))
    arrived.update(zip(keys, got))
    gr['w_gate'] = _mm_tn_call("dw_gate", hn_f[None], dgate)
    gr['w_up'] = _mm_tn_call("dw_up", hn_f[None], dup)
    gr['w_down'] = _mm_tn_call("dw_down", act, dh3[None])

    m_len = memx.shape[0]

    def f_xattn_bwd(h_, dy, g, wq, kv_, wo):
        hn, r = _rms(h_, g[...])
        hb = hn.astype(MXU)
        dyb = dy.astype(MXU)
        dhn = jnp.zeros(h_.shape, F32)
        dqs, ohs, dks, dvs = [], [], [], []
        for hd in range(X_HEADS):
            cs = pl.ds(hd * X_HEAD_DIM, X_HEAD_DIM)
            kh, vh = kv_[hd], kv_[X_HEADS + hd]
            qh = _mm(hb, wq[:, cs])
            p = _softmax(_mm_nt(qh, kh) * X_SCALE)
            ohs.append(_mm(p, vh).astype(MXU))
            do = _mm_nt(dyb, wo[cs, :])
            dvs.append(_mm_tn(p, do))
            dp = _mm_nt(do, vh)
            ds = p * (dp - jnp.sum(dp * p, axis=-1, keepdims=True)) * X_SCALE
            dq = _mm(ds, kh).astype(MXU)
            dks.append(_mm_tn(ds, qh))
            dhn = dhn + _mm_nt(dq, wq[:, cs])
            dqs.append(dq)
        dh, dg = _rms_bwd(h_, g[...], r, dhn)
        return (dy + dh, hb, jnp.concatenate(dqs, axis=-1), jnp.concatenate(ohs, axis=-1),
                jnp.stack(dks + dvs), dg)
    ex, keys = send(('nxt', ['w_down', 'w_xkv', 'w_xq']))
    (dh1, hn_x, dq_x, oh_x, dkvm, gr['norm_x_g']), got = _hosted(_rows(
        "xattn_bwd", f_xattn_bwd, s, tm,
        [(sv['h1'], 'r0'), (dh2, 'r0'), (pl_['norm_x_g'], 'f'), (wl['w_xq'], 'f'), (sv['kvm'], 'f'),
         (wl['w_xo'], 'f')],
        [((s, D_MODEL), F32, 'r0'), ((s, D_MODEL), MXU, 'r0'), ((s, D_MODEL), MXU, 'r0'),
         ((s, D_MODEL), MXU, 'r0'), ((N_DEV, m_len, X_HEAD_DIM), F32, 'a'), ((1, D_MODEL), F32, 'a')], ex))
    arrived.update(zip(keys, got))
    gr['w_xq'] = _mm_tn_call("dw_xq", hn_x[None], dq_x[None])[0]
    gr['w_xo'] = _mm_tn_call("dw_xo", oh_x[None], dh2[None])[0]

    def f_memkv_bwd(mm_, dkv, g, w):
        mn, r = _rms(mm_, g[...])
        mb = mn.astype(MXU)
        dmn = jnp.zeros(mm_.shape, F32)
        dws = []
        for d in range(N_DEV):
            dmn = dmn + _mm_nt(dkv[d], w[d])
            dws.append(_mm_tn(mb, dkv[d]))
        _, dg = _rms_bwd(mm_, g[...], r, dmn)
        return jnp.stack(dws), dg
    gr['w_xkv'], gr['mem_norm_g'] = _rows(
        "mem_kv_bwd", f_memkv_bwd, m_len, m_len,
        [(memx, 'r0'), (dkvm, 'r1'), (pl_['mem_norm_g'], 'f'), (wl['w_xkv'], 'f')],
        [((N_DEV, D_MODEL, X_HEAD_DIM), F32, 'a'), ((1, D_MODEL), F32, 'a')])

    def f_mix_out_bwd(a, so, dy, ga, gs, w):
        dmix = _mm_nt(dy, w[...])
        an, ra = _rms(a, ga[...])
        sn, rs = _rms(so, gs[...])
        da, dga = _rms_bwd(a, ga[...], ra, dmix[:, 0:512])
        dso, dgs = _rms_bwd(so, gs[...], rs, dmix[:, 512:1024])
        return da, dso, jnp.concatenate([an, sn], axis=-1), dga, dgs
    da_out, ds_out, mixed, gr['attn_out_g'], gr['ssm_out_g'] = _rows(
        "mix_out_bwd", f_mix_out_bwd, s, tm,
        [(sv['a_out'], 'r0'), (sv['s_out'], 'r0'), (dh1, 'r0'), (pl_['attn_out_g'], 'f'), (pl_['ssm_out_g'], 'f'),
         (wl['w_out'], 'f')],
        [((s, 512), F32, 'r0'), ((s, 512), F32, 'r0'), ((s, D_MODEL), MXU, 'r0'), ((1, 512), F32, 'a'),
         ((1, 512), F32, 'a')])
    gr['w_out'] = _mm_tn_call("dw_out", mixed[None], dh1[None])[0]

    ex, keys = send(('nxt', ['w_in', 'w_uq', 'w_ukv', 'ssm_w_glu', 'w_out', 'w_xo']), ('own', ['w_gate', 'w_up']))
    dq, dk, dv, got = _flash_bwd(sv['q'], sv['k'], sv['v'], sv['a_out'], sv['lse'], da_out, ex)
    arrived.update(zip(keys, got))

    def f_s5_out_bwd(xr, xi, u, y, ds, cre, cim, d, wglu, bglu):
        g, gelu_vjp = jax.vjp(jax.nn.gelu, y)
        sig = jax.nn.sigmoid(_mm(g, wglu[...]) + bglu[...])
        dz = ds * y * sig * (1.0 - sig)
        dy = ds * sig + gelu_vjp(_mm_nt(dz, wglu[...]))[0]
        dxr, dxi, dcr, dci = [], [], [], []
        for j in range(4):
            dyj = _lanes(dy, j, LANES)
            tr_, ti_ = _mm_nt(dyj, cre[j]), _mm_nt(dyj, cim[j])
            dxr += [_lanes(tr_, kk, LANES) for kk in range(4)]
            dxi += [_lanes(ti_, kk, LANES) for kk in range(4)]
            dcr.append(_mm_tn(_cat_blocks(xr, j), dyj))
            dci.append(_mm_tn(_cat_blocks(xi, j), dyj))
        return (jnp.stack(dxr), jnp.stack(dxi), dy * d[...], jnp.stack(dcr), jnp.stack(dci),
                jnp.sum(dy * u, axis=0, keepdims=True), _mm_tn(g, dz), jnp.sum(dz, axis=0, keepdims=True))
    xshape = (16, s, LANES)
    ex, keys = send(('own', ['w_down']))
    (dx_re, dx_im, du_dir, gr['c_re'], gr['c_im'], gr['ssm_d'], gr['ssm_w_glu'], gr['ssm_b_glu']), got = _hosted(_rows(
        "s5_out_bwd", f_s5_out_bwd, s, tm,
        [(sv['x_re'], 'r1'), (sv['x_im'], 'r1'), (sv['u_ch'], 'r0'), (sv['y_ssm'], 'r0'), (_to_chunks(ds_out), 'r0'),
         (pl_['c_re'], 'f'), (pl_['c_im'], 'f'), (pl_['ssm_d'], 'f'), (wl['ssm_w_glu'], 'f'),
         (pl_['ssm_b_glu'], 'f')],
        [(xshape, F32, 'r1'), (xshape, F32, 'r1'), ((s, 512), F32, 'r0'), ((4, 512, LANES), F32, 'a'),
         ((4, 512, LANES), F32, 'a'), ((1, 512), F32, 'a'), ((512, 512), F32, 'a'), ((1, 512), F32, 'a')], ex))
    arrived.update(zip(keys, got))
    g_re, g_im = _scan(dx_re, dx_im, pl_['a_re'], -pl_['a_im'], True)
    first_re = jnp.pad(sv['x_re'][:, s - 8:s - 1], ((0, 0), (1, 0), (0, 0)))
    first_im = jnp.pad(sv['x_im'][:, s - 8:s - 1], ((0, 0), (1, 0), (0, 0)))

    def f_s5_in_bwd(gre, gim, xr, xi, pr8, pi8, u, dud, f8r, f8i, bre, bim):
        first = pl.program_id(0) == 0
        xpr = jnp.concatenate([jnp.where(first, f8r[...], pr8), xr[:, :tm - 8]], axis=1)
        xpi = jnp.concatenate([jnp.where(first, f8i[...], pi8), xi[:, :tm - 8]], axis=1)
        dus, dbr, dbi = [], [], []
        for j in range(4):
            gj_r, gj_i, uj = _cat_blocks(gre, j), _cat_blocks(gim, j), _lanes(u, j, LANES)
            dus.append(_mm_nt(gj_r, bre[j]) + _mm_nt(gj_i, bim[j]))
            dbr.append(_mm_tn(uj, gj_r))
            dbi.append(_mm_tn(uj, gj_i))
        da_r = jnp.sum(gre * xpr + gim * xpi, axis=1, keepdims=True)
        da_i = jnp.sum(gim * xpr - gre * xpi, axis=1, keepdims=True)
        return dud + jnp.concatenate(dus, axis=-1), jnp.stack(dbr), jnp.stack(dbi), da_r, da_i
    ex, keys = send(('own', ['w_xkv']))
    (du_ch, gr['b_re'], gr['b_im'], gr['a_re'], gr['a_im']), got = _hosted(_rows(
        "s5_in_bwd", f_s5_in_bwd, s, tm,
        [(g_re, 'r1'), (g_im, 'r1'), (sv['x_re'], 'r1'), (sv['x_im'], 'r1'), (sv['x_re'], 'p8'), (sv['x_im'], 'p8'),
         (sv['u_ch'], 'r0'), (du_dir, 'r0'), (first_re, 'f'), (first_im, 'f'), (pl_['b_re'], 'f'), (pl_['b_im'], 'f')],
        [((s, 512), F32, 'r0'), ((4, LANES, 512), F32, 'a'), ((4, LANES, 512), F32, 'a'),
         ((16, 1, LANES), F32, 'a'), ((16, 1, LANES), F32, 'a')], ex))
    arrived.update(zip(keys, got))
    du = _from_chunks(du_ch)

    def f_qkv_bwd(pr, cos_, sin_, dq_, dk_, dv_, gq, gkv, wq, wk, wv, pt):
        cq, ckv = pr[:, 0:Q_LORA], pr[:, Q_LORA:Q_LORA + KV_LORA]
        cqn, rq = _rms(cq, gq[...])
        kvn, rkv = _rms(ckv, gkv[...])
        cqb, kvb = cqn.astype(MXU), kvn.astype(MXU)
        dcqn = jnp.zeros(cq.shape, F32)
        dkvn = jnp.zeros(ckv.shape, F32)
        dksum = jnp.zeros(dk_[0].shape, F32)
        dwq, dwk, dwv = [], [], []
        for hd in range(MLA_HEADS):
            dqp = (_rope_t(dq_[hd], cos_, sin_, pt) * MLA_SCALE).astype(MXU)
            dkb, dvb = dk_[hd].astype(MXU), dv_[hd].astype(MXU)
            dwq.append(_mm_tn(cqb, dqp))
            dwk.append(_mm_tn(kvb, dkb))
            dwv.append(_mm_tn(kvb, dvb))
            dcqn = dcqn + _mm_nt(dqp, wq[hd])
            dkvn = dkvn + _mm_nt(dkb, wk[hd]) + _mm_nt(dvb, wv[hd])
            dksum = dksum + dk_[hd]
        dcq, dgq = _rms_bwd(cq, gq[...], rq, dcqn)
        dckv, dgkv = _rms_bwd(ckv, gkv[...], rkv, dkvn)
        dpa = jnp.concatenate([dcq, dckv, _rope_t(dksum, cos_, sin_, pt)], axis=-1)
        return dpa, jnp.stack(dwq), jnp.stack(dwk), jnp.stack(dwv), dgq, dgkv
    ex, keys = send(('own', ['w_xq', 'w_xo']))
    (dpa, gr['w_uq'], gr['w_k'], gr['w_v'], gr['q_norm_g'], gr['kv_norm_g']), got = _hosted(_rows(
        "mla_qkv_bwd", f_qkv_bwd, s, tm,
        [(sv['proj'], 'r0'), (cos, 'r0'), (sin, 'r0'), (dq, 'r1'), (dk, 'r1'), (dv, 'r1'), (pl_['q_norm_g'], 'f'),
         (pl_['kv_norm_g'], 'f'), (wl['w_uq'], 'f'), (wl['w_k'], 'f'), (wl['w_v'], 'f'), (pmat_t, 'f')],
        [((s, 512), F32, 'r0'), ((MLA_HEADS, Q_LORA, HEAD_W), F32, 'a'), ((MLA_HEADS, KV_LORA, HEAD_W), F32, 'a'),
         ((MLA_HEADS, KV_LORA, HEAD_W), F32, 'a'), ((1, Q_LORA), F32, 'a'), ((1, KV_LORA), F32, 'a')], ex))
    arrived.update(zip(keys, got))

    def f_mix_in_bwd(h_, dpa_, du_, dres, g, w):
        dproj = jnp.concatenate([dpa_, du_], axis=-1).astype(MXU)
        xn, r = _rms(h_, g[...])
        dh, dg = _rms_bwd(h_, g[...], r, _mm_nt(dproj, w[...]))
        return dres + dh, xn, dproj, dg
    ex, keys = send(('own', ['ssm_w_glu', 'w_out']))
    (dh0, xn, dproj, gr['norm_mix_g']), got = _hosted(_rows(
        "mix_in_bwd", f_mix_in_bwd, s, tm,
        [(sv['h'], 'r0'), (dpa, 'r0'), (du, 'r0'), (dh1, 'r0'), (pl_['norm_mix_g'], 'f'), (wl['w_in'], 'f')],
        [((s, D_MODEL), F32, 'r0'), ((s, D_MODEL), MXU, 'r0'), ((s, D_MODEL), MXU, 'r0'), ((1, D_MODEL), F32, 'a')],
        ex))
    arrived.update(zip(keys, got))
    gr['w_in'] = _mm_tn_call("dw_in", xn[None], dproj[None])[0]
    return dh0, gr, arrived


def _layer_weights(w):
    wl = {}
    if 'w_in' in w:
        w_in = w['w_in'].reshape(D_MODEL, -1)
        z = lambda n: jnp.zeros((D_MODEL, n), w_in.dtype)
        wl['w_in'] = jnp.concatenate([w_in[:, :384], z(64), w_in[:, 384:416], z(32), w_in[:, 416:]], axis=1)
    if 'w_uq' in w:
        wl['w_uq'] = jnp.pad(w['w_uq'], ((0, 0), (0, 0), (0, HEAD_W - QK_NOPE - QK_ROPE)))
    if 'w_ukv' in w:
        wl['w_k'] = jnp.pad(w['w_ukv'][..., :QK_NOPE], ((0, 0), (0, 0), (0, HEAD_W - QK_NOPE)))
        wv = w['w_ukv'][..., QK_NOPE:]
        even = (jnp.arange(MLA_HEADS) % 2 == 0)[:, None, None]
        wl['w_v'] = jnp.concatenate([jnp.where(even, wv, 0), jnp.where(even, 0, wv)], axis=-1).astype(wv.dtype)
    if 'ssm_w_glu' in w:
        wl['ssm_w_glu'] = w['ssm_w_glu'].reshape(SSM_WIDTH, SSM_WIDTH)
    for n in ('w_out', 'w_xq', 'w_xo'):
        if n in w:
            wl[n] = w[n].reshape(D_MODEL, D_MODEL)
    for n in ('w_xkv', 'w_gate', 'w_up', 'w_down'):
        if n in w:
            wl[n] = w[n]
    return wl


def _blocked(gr, n):
    if n == 'w_in':
        d = gr['w_in']
        out = jnp.concatenate([d[:, :384], d[:, 448:480], d[:, 512:]], axis=1).reshape(N_DEV, 128, -1)
    elif n == 'w_uq':
        out = gr['w_uq'][..., :QK_NOPE + QK_ROPE]
    elif n == 'w_ukv':
        even = (jnp.arange(MLA_HEADS) % 2 == 0)[:, None, None]
        dv = gr['w_v']
        out = jnp.concatenate([gr['w_k'][..., :QK_NOPE], jnp.where(even, dv[..., :V_DIM], dv[..., V_DIM:])], axis=-1)
    elif n == 'ssm_w_glu':
        out = gr['ssm_w_glu'].reshape(N_DEV, SSM_WIDTH // N_DEV, SSM_WIDTH)
    elif n in ('w_out', 'w_xq', 'w_xo'):
        out = gr[n].reshape(N_DEV, D_MODEL // N_DEV, D_MODEL)
    else:
        out = gr[n]
    return out.astype(MXU)


def kernel(x, mem, positions, norm_mix_g, w_in, q_norm_g, w_uq, kv_norm_g, w_ukv, ssm_lambda_re, ssm_lambda_im, ssm_log_dt, ssm_b_re, ssm_b_im, ssm_c_re, ssm_c_im, ssm_d, ssm_w_glu, ssm_b_glu, attn_out_g, ssm_out_g, w_out, norm_x_g, mem_norm_g, w_xq, w_xkv, w_xo, norm_ffn_g, w_gate, w_up, w_down, final_norm_g, loss_target, m_norm_mix_g, m_w_in, m_q_norm_g, m_w_uq, m_kv_norm_g, m_w_ukv, m_ssm_lambda_re, m_ssm_lambda_im, m_ssm_log_dt, m_ssm_b_re, m_ssm_b_im, m_ssm_c_re, m_ssm_c_im, m_ssm_d, m_ssm_w_glu, m_ssm_b_glu, m_attn_out_g, m_ssm_out_g, m_w_out, m_norm_x_g, m_mem_norm_g, m_w_xq, m_w_xkv, m_w_xo, m_norm_ffn_g, m_w_gate, m_w_up, m_w_down, m_final_norm_g, v_norm_mix_g, v_w_in, v_q_norm_g, v_w_uq, v_kv_norm_g, v_w_ukv, v_ssm_lambda_re, v_ssm_lambda_im, v_ssm_log_dt, v_ssm_b_re, v_ssm_b_im, v_ssm_c_re, v_ssm_c_im, v_ssm_d, v_ssm_w_glu, v_ssm_b_glu, v_attn_out_g, v_ssm_out_g, v_w_out, v_norm_x_g, v_mem_norm_g, v_w_xq, v_w_xkv, v_w_xo, v_norm_ffn_g, v_w_gate, v_w_up, v_w_down, v_final_norm_g):
    args = dict(locals())
    W = {n: args[n] for n in WEIGHTS}
    M = {n: args['m_' + n] for n in WEIGHTS}
    V = {n: args['v_' + n] for n in WEIGHTS}
    s = x.shape[1]
    h = x[0]
    memx = mem[0]

    freqs = ROPE_THETA ** (-jnp.arange(0, QK_ROPE, 2, dtype=F32) / QK_ROPE)
    ang = positions[0].astype(F32)[:, None] * freqs
    c16, s16 = jnp.cos(ang), jnp.sin(ang)
    cos = jnp.concatenate([jnp.ones((s, QK_NOPE), F32), c16, c16, jnp.zeros((s, 32), F32)], axis=1)
    sin = jnp.concatenate([jnp.zeros((s, QK_NOPE), F32), s16, s16, jnp.zeros((s, 32), F32)], axis=1)
    idx = jnp.arange(QK_ROPE // 2)
    pmat = jnp.zeros((HEAD_W, HEAD_W), F32)
    pmat = pmat.at[QK_NOPE + 16 + idx, QK_NOPE + idx].set(-1.0).at[QK_NOPE + idx, QK_NOPE + 16 + idx].set(1.0)
    tabs = (cos, sin, pmat, pmat.T)

    shards = [{n: W[n][l].astype(MXU) for n in SHARDED} for l in range(DEPTH)]
    gathered = dict(zip(EARLY, _run_exchange("gather_weights", _gather(_named(EARLY, shards[0])))))

    layers = []
    for l in range(DEPTH):
        wl = _layer_weights(gathered)
        s5_in = [W[n][l] for n in ('ssm_lambda_re', 'ssm_lambda_im', 'ssm_log_dt', 'ssm_b_re', 'ssm_b_im',
                                   'ssm_c_re', 'ssm_c_im')]
        (a_re, a_im, bre, bim, cre, cim), s5_vjp = jax.vjp(_s5_params, *s5_in)
        pl_ = {n: W[n][l][None] for n in ('norm_mix_g', 'q_norm_g', 'kv_norm_g', 'ssm_d', 'ssm_b_glu',
                                           'attn_out_g', 'ssm_out_g', 'norm_x_g', 'mem_norm_g', 'norm_ffn_g')}
        pl_.update(a_re=a_re, a_im=a_im, b_re=bre, b_im=bim, c_re=cre, c_im=cim)
        h, sv, wl, gathered = _layer_fwd(h, memx, tabs, wl, pl_, shards[0] if l == 0 else None,
                                         shards[l + 1] if l + 1 < DEPTH else None)
        layers.append((wl, pl_, sv, s5_vjp))

    def f_loss(h_, tgt, g):
        y, r = _rms(h_, g[...])
        err = y - tgt
        part = 0.5 * jnp.sum(jnp.mean(err * err, axis=-1, keepdims=True), axis=0, keepdims=True)
        dh, dg = _rms_bwd(h_, g[...], r, err / D_MODEL)
        return dh, dg, jnp.broadcast_to(part, (8, LANES))
    dh, g_final, loss_part = _rows(
        "loss_head", f_loss, s, min(256, s), [(h, 'r0'), (loss_target[0], 'r0'), (final_norm_g[None], 'f')],
        [((s, D_MODEL), F32, 'r0'), ((1, D_MODEL), F32, 'a'), ((8, LANES), F32, 'a')])
    loss = lax.psum(loss_part[0, 0], ("x", "y", "c"))

    parts = [{} for _ in range(DEPTH)]
    g_rep = [None] * DEPTH
    blocks = None
    for l in reversed(range(DEPTH)):
        wl, pl_, sv, s5_vjp = layers[l]
        dh, gr, arrived = _layer_bwd(dh, sv, memx, tabs, wl, pl_, blocks, own=(l == 0))
        for (who, n), p in arrived.items():
            parts[l + 1 if who == 'nxt' else l][n] = p
        blocks = {n: _blocked(gr, n) for n in SHARDED} if l > 0 else None
        ds5 = s5_vjp((gr['a_re'], gr['a_im'], gr['b_re'], gr['b_im'], gr['c_re'], gr['c_im']))
        rep = dict(zip(('ssm_lambda_re', 'ssm_lambda_im', 'ssm_log_dt', 'ssm_b_re', 'ssm_b_im', 'ssm_c_re',
                        'ssm_c_im'), ds5))
        for n in ('norm_mix_g', 'q_norm_g', 'kv_norm_g', 'ssm_d', 'ssm_b_glu', 'attn_out_g', 'ssm_out_g',
                  'norm_x_g', 'mem_norm_g', 'norm_ffn_g'):
            rep[n] = gr[n][0]
        g_rep[l] = rep
    grad_x = dh[None]

    rep_names = REPL_L + ['final_norm_g']
    g_loc = {n: jnp.stack([g_rep[l][n] for l in range(DEPTH)]) for n in REPL_L}
    g_loc['final_norm_g'] = g_final
    rest = [n for n in SHARDED if n not in parts[0]]
    last = _run_exchange("last_grads", _together(_scatter([_blocked(gr, n) for n in rest]),
                                                 _gather([_pack(_named(rep_names, g_loc))])))
    parts[0].update(zip(rest, last[:len(rest)]))

    out_sh = [{}, {}, {}, {}]
    for n in SHARDED:
        res = _adamw_weight("adamw_" + n, [parts[l][n] for l in range(DEPTH)], W[n], M[n], V[n])
        for kind, r in enumerate(res):
            out_sh[kind][n] = r

    shapes_rp = [(1,) + W[n].shape if W[n].ndim == 1 else W[n].shape for n in rep_names]
    g_rp = _unpack(_sum_sources("sum_small_grads", last[len(rest)]), shapes_rp)
    as_rows = lambda d: [d[n].reshape(shp) for n, shp in zip(rep_names, shapes_rp)]
    res_rp = (g_rp,) + _adamw_small("adamw_replicated", g_rp, as_rows(W), as_rows(M), as_rows(V))
    out_rp = [{n: a.reshape(W[n].shape) for n, a in zip(rep_names, r)} for r in res_rp]

    outs = [loss, grad_x]
    for kind in range(4):
        for n in WEIGHTS:
            outs.append(out_sh[kind][n] if n in SHARDED else out_rp[kind][n])
    return tuple(outs)
```

```python
from typing import Callable, NamedTuple

import jax
import jax.numpy as jnp
from jax import lax
from jax.experimental import pallas as pl
from jax.experimental.pallas import tpu as pltpu

F32 = jnp.float32
MXU = jnp.bfloat16
HI = lax.Precision.HIGHEST

D_MODEL = 1024
MLA_HEADS = 8
QK_NOPE = 64
QK_ROPE = 32
V_DIM = 64
Q_LORA = 256
KV_LORA = 128
SSM_WIDTH = 512
SSM_GROUPS = 32
SSM_GROUP = 16
SSM_STATE = 64
X_HEADS = 4
X_HEAD_DIM = 256
D_FF = 2816
FF_CHUNK = D_FF // 2
ROPE_THETA = 10000.0
EPS = 1e-6
DEPTH = 2
N_DEV = 8
LANES = 128
HEAD_W = 128
MLA_SCALE = (QK_NOPE + QK_ROPE) ** -0.5
X_SCALE = X_HEAD_DIM ** -0.5
ADAM_LR, ADAM_B1, ADAM_B2, ADAM_EPS, ADAM_WD, ADAM_STEP = 0.001, 0.9, 0.999, 1e-08, 0.01, 10
VMEM_LIMIT = 56 * 1024 * 1024
FLASH_TILE = 512
DW_ROWS = 2048
X_ROWS = 512
MESH = pl.DeviceIdType.MESH

SHARDED = ['w_in', 'w_uq', 'w_ukv', 'ssm_w_glu', 'w_out', 'w_xq', 'w_xkv', 'w_xo', 'w_gate', 'w_up', 'w_down']
REPL_L = ['norm_mix_g', 'q_norm_g', 'kv_norm_g', 'ssm_lambda_re', 'ssm_lambda_im', 'ssm_log_dt', 'ssm_b_re',
          'ssm_b_im', 'ssm_c_re', 'ssm_c_im', 'ssm_d', 'ssm_b_glu', 'attn_out_g', 'ssm_out_g', 'norm_x_g',
          'mem_norm_g', 'norm_ffn_g']
WEIGHTS = ['norm_mix_g', 'w_in', 'q_norm_g', 'w_uq', 'kv_norm_g', 'w_ukv', 'ssm_lambda_re', 'ssm_lambda_im',
           'ssm_log_dt', 'ssm_b_re', 'ssm_b_im', 'ssm_c_re', 'ssm_c_im', 'ssm_d', 'ssm_w_glu', 'ssm_b_glu',
           'attn_out_g', 'ssm_out_g', 'w_out', 'norm_x_g', 'mem_norm_g', 'w_xq', 'w_xkv', 'w_xo', 'norm_ffn_g',
           'w_gate', 'w_up', 'w_down', 'final_norm_g']


def _pcall(body, **kw):
    return pl.pallas_call(body, **kw)


def _mm(a, b):
    return jnp.dot(a.astype(MXU), b.astype(MXU), preferred_element_type=F32)


def _mm_nt(a, b):
    return lax.dot_general(a.astype(MXU), b.astype(MXU), (((1,), (1,)), ((), ())), preferred_element_type=F32)


def _mm_tn(a, b):
    return lax.dot_general(a.astype(MXU), b.astype(MXU), (((0,), (0,)), ((), ())), preferred_element_type=F32)


def _mm_hi(a, b):
    return jnp.dot(a.astype(F32), b.astype(F32), precision=HI, preferred_element_type=F32)


def _rms(x, g):
    r = lax.rsqrt(jnp.mean(x * x, axis=-1, keepdims=True) + EPS)
    return x * r * g, r


def _rms_bwd(x, g, r, dy):
    dyg = dy * g
    dx = r * dyg - x * (r * r * r) * jnp.mean(dyg * x, axis=-1, keepdims=True)
    return dx, jnp.sum(dy * x * r, axis=0, keepdims=True)


def _rope(x, cos, sin, p_ref):
    return x * cos + _mm_hi(x, p_ref[...]) * sin


def _rope_t(g, cos, sin, pt_ref):
    return g * cos + _mm_hi(g * sin, pt_ref[...])


def _softmax(s):
    m = jnp.max(s, axis=-1, keepdims=True)
    e = jnp.exp(s - m)
    return e / jnp.sum(e, axis=-1, keepdims=True)


def _lanes(x, j, w):
    return x[:, j * w:(j + 1) * w]


def _rows(name, fn, n, tm, ins, outs, side=None):
    def spec(shape, kind):
        nd = len(shape)
        if kind == 'p8':
            return pl.BlockSpec((shape[0], 8, shape[2]), lambda i: (0, jnp.maximum(i * (tm // 8) - 1, 0), 0))
        if kind == 'f':
            return pl.BlockSpec(shape, lambda i, _nd=nd: (0,) * _nd, pipeline_mode=pl.Buffered(1))
        if kind == 'a':
            return pl.BlockSpec(shape, lambda i, _nd=nd: (0,) * _nd)
        ax = int(kind[1])
        blk = tuple(tm if d == ax else s for d, s in enumerate(shape))
        return pl.BlockSpec(blk, lambda i, _ax=ax, _nd=nd: tuple(i if d == _ax else 0 for d in range(_nd)))

    n_in, n_out, n_steps = len(ins), len(outs), n // tm

    def body(*refs):
        in_refs, out_refs, steps = _side_split(refs, n_in, n_out, side)
        i = pl.program_id(0)
        if steps:
            pl.when(i == 0)(steps[0])
            pl.when(i == _pass_on_step(n_steps))(steps[1])
        args = [r if k == 'f' else r[...] for r, (_, k) in zip(in_refs, ins)]
        res = fn(*args)
        for r, (_, dt, k), v in zip(out_refs, outs, res):
            if k == 'a':
                _accumulate(r, v.astype(dt), i)
            else:
                r[...] = v.astype(dt)
        if steps:
            pl.when(i == n_steps - 1)(steps[2])

    s_in, s_out, s_shape, s_sems, s_ops = _side_args(side)
    res = _pcall(
        body, name=name + ("_x" if side else ""), grid=(n_steps,),
        in_specs=[spec(a.shape, k) for a, k in ins] + s_in,
        out_specs=[spec(s, k) for s, _, k in outs] + s_out,
        out_shape=[jax.ShapeDtypeStruct(s, dt) for s, dt, _ in outs] + s_shape,
        scratch_shapes=s_sems,
        compiler_params=pltpu.CompilerParams(dimension_semantics=("arbitrary",), vmem_limit_bytes=VMEM_LIMIT),
    )(*[a for a, _ in ins], *s_ops)
    return _Hosted(res[:n_out], res[n_out:]) if side else res


def _accumulate(ref, v, i):
    @pl.when(i == 0)
    def _():
        ref[...] = v

    @pl.when(i != 0)
    def _():
        ref[...] += v


def _mm_tn_call(name, a, b, tk=None, tn=None):
    out_dtype = MXU
    s, k = a.shape
    n = b.shape[1]
    tk, tn = tk or k, tn or n
    ts = min(DW_ROWS, s)
    ns = s // ts

    def body(a_ref, b_ref, o_ref, acc_ref):
        j = pl.program_id(2)
        _accumulate(acc_ref, _mm_tn(a_ref[...], b_ref[...]), j)

        @pl.when(j == ns - 1)
        def _():
            o_ref[...] = acc_ref[...].astype(out_dtype)

    return _pcall(
        body, name=name, grid=(k // tk, n // tn, ns),
        in_specs=[pl.BlockSpec((ts, tk), lambda ik, jn, j: (j, ik)),
                  pl.BlockSpec((ts, tn), lambda ik, jn, j: (j, jn))],
        out_specs=pl.BlockSpec((tk, tn), lambda ik, jn, j: (ik, jn)),
        out_shape=jax.ShapeDtypeStruct((k, n), out_dtype),
        scratch_shapes=[pltpu.VMEM((tk, tn), F32)],
        compiler_params=pltpu.CompilerParams(dimension_semantics=("arbitrary", "arbitrary", "arbitrary"),
                                             vmem_limit_bytes=VMEM_LIMIT),
    )(a, b)


def _side_split(refs, n_in, n_out, side):
    if side is None:
        return refs[:n_in], refs[n_in:n_in + n_out], None
    si, so = len(side.ins), len(side.out_shapes)
    own_in, side_in = refs[:n_in], refs[n_in:n_in + si]
    own_out, side_out = refs[n_in + si:n_in + si + n_out], refs[n_in + si + n_out:n_in + si + n_out + so]
    return own_in, own_out, side.steps(side_in, side_out, refs[n_in + si + n_out + so:])


def _pass_on_step(n_steps):
    return max(n_steps - 2, 0)


def _side_args(side):
    if side is None:
        return [], [], [], [], []
    any_spec = pl.BlockSpec(memory_space=pl.ANY)
    return ([any_spec] * len(side.ins), [any_spec] * len(side.out_shapes), list(side.out_shapes),
            list(side.sem_shapes), list(side.ins))


class _Hosted(NamedTuple):
    results: list
    arrived: list


def _hosted(res):
    return res if isinstance(res, _Hosted) else _Hosted(res, ())


def _flash_fwd(q, k, v, side=None):
    nh, s, w = q.shape
    t = min(FLASH_TILE, s)
    nq = s // t
    n_steps = (nh // 2) * nq

    def body(*refs):
        (q_ref, k_ref, v_ref), (o_ref, lse_ref), steps = _side_split(refs, 3, 2, side)
        step = pl.program_id(0) * nq + pl.program_id(1)
        if steps:
            pl.when(step == 0)(steps[0])
            pl.when(step == _pass_on_step(n_steps))(steps[1])
        qi = pl.program_id(1)
        qs = [q_ref[0], q_ref[1]]
        below = lax.broadcasted_iota(jnp.int32, (t, t), 1) <= lax.broadcasted_iota(jnp.int32, (t, t), 0)

        def tile(j, carry, diagonal):
            sl = pl.ds(pl.multiple_of(j * t, t), t)
            out = []
            for hh in range(2):
                m, l, acc = carry[3 * hh:3 * hh + 3]
                sc = _mm_nt(qs[hh], k_ref[hh, sl, :])
                if diagonal:
                    sc = jnp.where(below, sc, -1e30)
                m_new = jnp.maximum(m, jnp.max(sc, axis=1, keepdims=True))
                p = jnp.exp(sc - m_new)
                alpha = jnp.exp(m - m_new)
                out += [m_new, alpha * l + jnp.sum(p, axis=1, keepdims=True), alpha * acc + _mm(p, v_ref[hh, sl, :])]
            return tuple(out)

        init = (jnp.full((t, 1), -1e30, F32), jnp.zeros((t, 1), F32), jnp.zeros((t, w), F32)) * 2
        carry = lax.fori_loop(0, qi, lambda j, c: tile(j, c, False), init)
        carry = tile(qi, carry, True)
        o_ref[...] = carry[2] / carry[1] + carry[5] / carry[4]
        for hh in range(2):
            lse_ref[hh] = jnp.broadcast_to(carry[3 * hh] + jnp.log(carry[3 * hh + 1]), (t, w))
        if steps:
            pl.when(step == n_steps - 1)(steps[2])

    s_in, s_out, s_shape, s_sems, s_ops = _side_args(side)
    res = _pcall(
        body, name="mla_flash_fwd" + ("_x" if side else ""), grid=(nh // 2, nq),
        in_specs=[pl.BlockSpec((2, t, w), lambda p, i: (p, i, 0)),
                  pl.BlockSpec((2, s, w), lambda p, i: (p, 0, 0)),
                  pl.BlockSpec((2, s, w), lambda p, i: (p, 0, 0))] + s_in,
        out_specs=[pl.BlockSpec((t, w), lambda p, i: (i, p)),
                   pl.BlockSpec((2, t, w), lambda p, i: (p, i, 0))] + s_out,
        out_shape=[jax.ShapeDtypeStruct((s, (nh // 2) * w), F32), jax.ShapeDtypeStruct((nh, s, w), F32)] + s_shape,
        scratch_shapes=s_sems,
        compiler_params=pltpu.CompilerParams(dimension_semantics=("arbitrary", "arbitrary"),
                                             vmem_limit_bytes=VMEM_LIMIT),
    )(q, k, v, *s_ops)
    return res[0], res[1], res[2:]


def _flash_bwd(q, k, v, o, lse, do, side=None):
    nh, s, w = q.shape
    t = min(FLASH_TILE, s)
    nq = s // t
    n_steps = (nh // 2) * nq

    def body(*refs):
        (q_ref, k_ref, v_ref, o_ref, lse_ref, do_ref), (dq_ref, dk_ref, dv_ref), steps = _side_split(refs, 6, 3, side)
        step = pl.program_id(0) * nq + pl.program_id(1)
        if steps:
            pl.when(step == 0)(steps[0])
            pl.when(step == _pass_on_step(n_steps))(steps[1])
        j = pl.program_id(1)

        @pl.when(j == 0)
        def _():
            dq_ref[...] = jnp.zeros(dq_ref.shape, F32)

        below = lax.broadcasted_iota(jnp.int32, (t, t), 1) <= lax.broadcasted_iota(jnp.int32, (t, t), 0)
        lane = lax.broadcasted_iota(jnp.int32, (t, w), 1)
        heads = [jnp.logical_and(lane >= hh * V_DIM, lane < (hh + 1) * V_DIM) for hh in range(2)]
        ks = [k_ref[0], k_ref[1]]
        vs = [v_ref[0], v_ref[1]]

        def tile(i, carry, diagonal):
            sl = pl.ds(pl.multiple_of(i * t, t), t)
            dout_all, o_all = do_ref[sl, :], o_ref[sl, :]
            out = []
            for hh in range(2):
                dk, dv = carry[2 * hh], carry[2 * hh + 1]
                qh = q_ref[hh, sl, :]
                dout = jnp.where(heads[hh], dout_all, 0.0)
                sc = _mm_nt(qh, ks[hh])
                if diagonal:
                    sc = jnp.where(below, sc, -1e30)
                p = jnp.exp(sc - lse_ref[hh, sl, 0:1])
                dp = _mm_nt(dout, vs[hh])
                ds = p * (dp - jnp.sum(dout * o_all, axis=1, keepdims=True))
                dq_ref[hh, sl, :] += _mm(ds, ks[hh])
                out += [dk + _mm_tn(ds, qh), dv + _mm_tn(p, dout)]
            return tuple(out)

        carry = tile(j, (jnp.zeros((t, w), F32),) * 4, True)
        carry = lax.fori_loop(j + 1, nq, lambda i, c: tile(i, c, False), carry)
        for hh in range(2):
            dk_ref[hh] = carry[2 * hh]
            dv_ref[hh] = jnp.where(heads[hh], carry[2 * hh + 1], 0.0)
        if steps:
            pl.when(step == n_steps - 1)(steps[2])

    s_in, s_out, s_shape, s_sems, s_ops = _side_args(side)
    res = _pcall(
        body, name="mla_flash_bwd" + ("_x" if side else ""), grid=(nh // 2, nq),
        in_specs=[pl.BlockSpec((2, s, w), lambda p, j: (p, 0, 0)),
                  pl.BlockSpec((2, t, w), lambda p, j: (p, j, 0)),
                  pl.BlockSpec((2, t, w), lambda p, j: (p, j, 0)),
                  pl.BlockSpec((s, w), lambda p, j: (0, p)),
                  pl.BlockSpec((2, s, w), lambda p, j: (p, 0, 0)),
                  pl.BlockSpec((s, w), lambda p, j: (0, p))] + s_in,
        out_specs=[pl.BlockSpec((2, s, w), lambda p, j: (p, 0, 0)),
                   pl.BlockSpec((2, t, w), lambda p, j: (p, j, 0)),
                   pl.BlockSpec((2, t, w), lambda p, j: (p, j, 0))] + s_out,
        out_shape=[jax.ShapeDtypeStruct((nh, s, w), F32)] * 3 + s_shape,
        scratch_shapes=s_sems,
        compiler_params=pltpu.CompilerParams(dimension_semantics=("arbitrary", "arbitrary"),
                                             vmem_limit_bytes=VMEM_LIMIT),
    )(q, k, v, o, lse, do, *s_ops)
    return res[0], res[1], res[2], res[3:]


def _scan(b_re, b_im, a_re, a_im, reverse):
    nb, s, w = b_re.shape
    ch = s // 8
    assert ch & (ch - 1) == 0
    grp = 2

    def cmul(ar, ai, xr, xi):
        return ar * xr - ai * xi, ar * xi + ai * xr

    def body(br_ref, bi_ref, ar_ref, ai_ref, xr_ref, xi_ref):
        sub = lax.broadcasted_iota(jnp.int32, (8, w), 0)

        def shift(x, k):
            if reverse:
                return jnp.where(sub < 8 - k, pltpu.roll(x, 8 - k, 0), 0.0)
            return jnp.where(sub >= k, pltpu.roll(x, k, 0), 0.0)

        ar = [jnp.broadcast_to(ar_ref[g], (8, w)) for g in range(grp)]
        ai = [jnp.broadcast_to(ai_ref[g], (8, w)) for g in range(grp)]

        def tsl(i):
            return pl.ds(pl.multiple_of(((ch - 1 - i) if reverse else i) * 8, 8), 8)

        def local(i, carry):
            out = []
            for g in range(grp):
                xr, xi = carry[2 * g], carry[2 * g + 1]
                pr, pi = cmul(ar[g], ai[g], xr, xi)
                nr = pr + br_ref[g, tsl(i), :]
                ni = pi + bi_ref[g, tsl(i), :]
                xr_ref[g, tsl(i), :] = nr
                xi_ref[g, tsl(i), :] = ni
                out += [nr, ni]
            return tuple(out)

        fin = lax.fori_loop(0, ch, local, (jnp.zeros((8, w), F32),) * (2 * grp))

        carry_in = []
        for g in range(grp):
            pr, pi = ar[g], ai[g]
            for _ in range(ch.bit_length() - 1):
                pr, pi = cmul(pr, pi, pr, pi)
            fr, fi = fin[2 * g], fin[2 * g + 1]
            for kk in (1, 2, 4):
                sr, si = cmul(pr, pi, shift(fr, kk), shift(fi, kk))
                fr, fi = fr + sr, fi + si
                pr, pi = cmul(pr, pi, pr, pi)
            carry_in += [shift(fr, 1), shift(fi, 1)]

        def fix(i, pw):
            out = []
            for g in range(grp):
                pr, pi = pw[2 * g], pw[2 * g + 1]
                cr, ci = cmul(pr, pi, carry_in[2 * g], carry_in[2 * g + 1])
                xr_ref[g, tsl(i), :] = xr_ref[g, tsl(i), :] + cr
                xi_ref[g, tsl(i), :] = xi_ref[g, tsl(i), :] + ci
                nr, ni = cmul(pr, pi, ar[g], ai[g])
                out += [nr, ni]
            return tuple(out)

        lax.fori_loop(0, ch, fix, tuple(x for g in range(grp) for x in (ar[g], ai[g])))

    blk = pl.BlockSpec((grp, s, w), lambda i: (i, 0, 0))
    ablk = pl.BlockSpec((grp, 1, w), lambda i: (i, 0, 0))
    return _pcall(
        body, name="s5_scan_rev" if reverse else "s5_scan", grid=(nb // grp,),
        in_specs=[blk, blk, ablk, ablk], out_specs=[blk, blk],
        out_shape=[jax.ShapeDtypeStruct((nb, s, w), F32)] * 2,
        compiler_params=pltpu.CompilerParams(dimension_semantics=("arbitrary",), vmem_limit_bytes=VMEM_LIMIT),
    )(b_re, b_im, a_re, a_im)


class _Exchange(NamedTuple):
    ins: list
    out_shapes: list
    sem_shapes: list
    steps: Callable


def _gather_steps(ins, outs, sems):
    n = len(ins)
    send_sems, recv_sems, local_sems = sems
    x, y, c = lax.axis_index("x"), lax.axis_index("y"), lax.axis_index("c")
    me, sibling = (x, y, c), (x, y, 1 - c)
    chips = [(1 - x, y), (x, 1 - y), (1 - x, 1 - y)]

    def copy(a, k, block, to, src=None):
        dst = outs[a].at[4 * block[0] + 2 * block[1] + block[2]]
        return pltpu.make_async_remote_copy(
            src_ref=dst if src is None else src, dst_ref=dst,
            send_sem=send_sems.at[a, k], recv_sem=recv_sems.at[a, k], device_id=to, device_id_type=MESH)

    mine = [pltpu.make_async_copy(ins[a], outs[a].at[4 * x + 2 * y + c], local_sems.at[a]) for a in range(n)]
    first = []
    for a in range(n):
        first.append(copy(a, 0, me, sibling, src=ins[a]))
        first += [copy(a, 1 + j, me, (*chip, c), src=ins[a]) for j, chip in enumerate(chips)]
    passed = [copy(a, 4 + j, (*chip, c), sibling) for j, chip in enumerate(chips) for a in range(n)]

    def start():
        for cp in mine + first:
            cp.start()

    def pass_on():
        i = 0
        for j, chip in enumerate(chips):
            for a in range(n):
                copy(a, 1 + j, (*chip, c), me).wait_recv()
                passed[i].start()
                i += 1

    def finish():
        for a in range(n):
            copy(a, 0, sibling, me).wait_recv()
            for j, chip in enumerate(chips):
                copy(a, 4 + j, (*chip, 1 - c), me).wait_recv()
        for cp in first + passed:
            cp.wait_send()
        for cp in mine:
            cp.wait()

    return start, pass_on, finish


def _gather(arrs):
    n = len(arrs)
    return _Exchange(list(arrs), [jax.ShapeDtypeStruct((N_DEV,) + a.shape, a.dtype) for a in arrs],
                     [pltpu.SemaphoreType.DMA((n, 7)), pltpu.SemaphoreType.DMA((n, 7)), pltpu.SemaphoreType.DMA((n,))],
                     _gather_steps)


def _scatter_steps(ins, outs, sems):
    n = len(ins)
    send_sems, recv_sems, local_sems = sems
    x, y, c = lax.axis_index("x"), lax.axis_index("y"), lax.axis_index("c")
    me = 4 * x + 2 * y + c
    own, sent, arrivals = [], [], []
    for a in range(n):
        own.append(pltpu.make_async_copy(ins[a].at[me], outs[a].at[me], local_sems.at[a]))
        for k in range(1, N_DEV):
            px, py, pc = x ^ ((k >> 2) & 1), y ^ ((k >> 1) & 1), c ^ (k & 1)
            peer = 4 * px + 2 * py + pc
            sent.append(pltpu.make_async_remote_copy(
                src_ref=ins[a].at[peer], dst_ref=outs[a].at[me],
                send_sem=send_sems.at[a, k - 1], recv_sem=recv_sems.at[a, k - 1],
                device_id=(px, py, pc), device_id_type=MESH))
            arrivals.append(pltpu.make_async_remote_copy(
                src_ref=ins[a].at[me], dst_ref=outs[a].at[peer],
                send_sem=send_sems.at[a, k - 1], recv_sem=recv_sems.at[a, k - 1],
                device_id=(x, y, c), device_id_type=MESH))

    def start():
        for cp in own + sent:
            cp.start()

    def pass_on():
        pass

    def finish():
        for cp in arrivals:
            cp.wait_recv()
        for cp in sent:
            cp.wait_send()
        for cp in own:
            cp.wait()

    return start, pass_on, finish


def _scatter(grads):
    n = len(grads)
    return _Exchange(list(grads), [jax.ShapeDtypeStruct(g.shape, g.dtype) for g in grads],
                     [pltpu.SemaphoreType.DMA((n, N_DEV - 1)), pltpu.SemaphoreType.DMA((n, N_DEV - 1)),
                      pltpu.SemaphoreType.DMA((n,))], _scatter_steps)


def _together(a, b):
    def steps(ins, outs, sems):
        sa = a.steps(ins[:len(a.ins)], outs[:len(a.out_shapes)], sems[:len(a.sem_shapes)])
        sb = b.steps(ins[len(a.ins):], outs[len(a.out_shapes):], sems[len(a.sem_shapes):])

        def both(k):
            def run():
                sa[k]()
                sb[k]()
            return run
        return both(0), both(1), both(2)

    return _Exchange(a.ins + b.ins, a.out_shapes + b.out_shapes, a.sem_shapes + b.sem_shapes, steps)


def _run_exchange(name, ex):
    n_in, n_out = len(ex.ins), len(ex.out_shapes)

    def body(*refs):
        for step in ex.steps(refs[:n_in], refs[n_in:n_in + n_out], refs[n_in + n_out:]):
            step()

    any_spec = pl.BlockSpec(memory_space=pl.ANY)
    return _pcall(body, name=name, in_specs=[any_spec] * n_in, out_specs=[any_spec] * n_out,
                  out_shape=list(ex.out_shapes), scratch_shapes=list(ex.sem_shapes))(*ex.ins)


def _adam_math(g, w_, m_, v_):
    m_new = ADAM_B1 * m_ + (1.0 - ADAM_B1) * g
    v_new = ADAM_B2 * v_ + (1.0 - ADAM_B2) * (g * g)
    m_hat = m_new / (1.0 - ADAM_B1 ** ADAM_STEP)
    v_hat = v_new / (1.0 - ADAM_B2 ** ADAM_STEP)
    delta = -ADAM_LR * (m_hat / (jnp.sqrt(v_hat) + ADAM_EPS) + ADAM_WD * w_)
    return delta, m_new, v_new


def _adamw_weight(name, parts, w, m, v):
    nl = len(parts)

    def body(*refs):
        p_refs = refs[:nl]
        w_ref, m_ref, v_ref, g_ref, d_ref, mo_ref, vo_ref = refs[nl:]
        for l in range(nl):
            g = p_refs[l][0].astype(F32)
            for j in range(1, N_DEV):
                g = g + p_refs[l][j].astype(F32)
            g_ref[l] = g
            d_ref[l], mo_ref[l], vo_ref[l] = _adam_math(g, w_ref[l], m_ref[l], v_ref[l])

    return _pcall(
        body, name=name, out_shape=[jax.ShapeDtypeStruct(w.shape, F32)] * 4,
        compiler_params=pltpu.CompilerParams(vmem_limit_bytes=VMEM_LIMIT),
    )(*parts, w, m, v)

def _sum_sources(name, parts):
    r = parts.shape[1]

    def body(p_ref, g_ref):
        g = p_ref[0]
        for j in range(1, N_DEV):
            g = g + p_ref[j]
        g_ref[...] = g

    return _pcall(body, name=name, out_shape=jax.ShapeDtypeStruct((r, LANES), F32),
                  compiler_params=pltpu.CompilerParams(vmem_limit_bytes=VMEM_LIMIT))(parts)


def _adamw_small(name, g, w, m, v):
    n = len(g)

    def body(*refs):
        g_r, w_r, m_r, v_r = (refs[k * n:(k + 1) * n] for k in range(4))
        d_r, mo_r, vo_r = (refs[k * n:(k + 1) * n] for k in range(4, 7))
        for i in range(n):
            d_r[i][...], mo_r[i][...], vo_r[i][...] = _adam_math(g_r[i][...], w_r[i][...], m_r[i][...], v_r[i][...])

    res = _pcall(body, name=name, out_shape=[jax.ShapeDtypeStruct(a.shape, F32) for a in w] * 3,
                 compiler_params=pltpu.CompilerParams(vmem_limit_bytes=VMEM_LIMIT))(*g, *w, *m, *v)
    return res[:n], res[n:2 * n], res[2 * n:]


def _pack(arrs):
    flat = jnp.concatenate([a.reshape(-1) for a in arrs])
    flat = jnp.pad(flat, (0, (-flat.shape[0]) % (8 * LANES)))
    return flat.reshape(-1, LANES)


def _unpack(packed, shapes):
    flat = packed.reshape(-1)
    out, off = [], 0
    for shp in shapes:
        size = 1
        for d in shp:
            size *= d
        out.append(flat[off:off + size].reshape(shp))
        off += size
    return out


def _s5_params(lam_re, lam_im, log_dt, b_re, b_im, c_re, c_im):
    dt = jnp.exp(log_dt)[:, None]
    e = jnp.exp(lam_re * dt)
    ang = lam_im * dt
    a_re, a_im = e * jnp.cos(ang), e * jnp.sin(ang)
    nr, ni = a_re - 1.0, a_im
    den = lam_re * lam_re + lam_im * lam_im
    cr = ((nr * lam_re + ni * lam_im) / den)[..., None]
    ci = ((ni * lam_re - nr * lam_im) / den)[..., None]
    bb_re = cr * b_re - ci * b_im
    bb_im = cr * b_im + ci * b_re
    eye = jnp.eye(8, dtype=F32)[None, :, None, :, None]

    def bblk(bb):
        t = jnp.transpose(bb.reshape(4, 8, SSM_STATE, SSM_GROUP), (0, 3, 1, 2))
        return (eye * t[:, None]).reshape(4, 8 * SSM_GROUP, 8 * SSM_STATE)

    def cblk(cc):
        t = jnp.transpose(cc.reshape(4, 8, SSM_GROUP, SSM_STATE), (0, 3, 1, 2))
        return (eye * t[:, None]).reshape(4, 8 * SSM_STATE, 8 * SSM_GROUP)

    nb = SSM_GROUPS * SSM_STATE // LANES
    return (a_re.reshape(nb, 1, LANES), a_im.reshape(nb, 1, LANES), bblk(bb_re), bblk(bb_im),
            cblk(c_re), -cblk(c_im))


def _cat_blocks(x3, j):
    return jnp.concatenate([x3[4 * j + k] for k in range(4)], axis=-1)


def _to_chunks(a):
    s, c = a.shape
    return a.reshape(8, s // 8, c).transpose(1, 0, 2).reshape(s, c)


def _from_chunks(a):
    s, c = a.shape
    return a.reshape(s // 8, 8, c).transpose(1, 0, 2).reshape(s, c)


EARLY = ['w_in', 'w_uq', 'w_ukv']
MIDDLE = ['ssm_w_glu', 'w_out', 'w_xq', 'w_xkv', 'w_xo']
FEED = ['w_gate', 'w_up', 'w_down']


def _named(names, d):
    return [d[n] for n in names]


def _layer_fwd(h, memx, tabs, wl, pl_, late=None, nxt=None):
    s = h.shape[0]
    tm = min(256, s)
    cos, sin, pmat, pmat_t = tabs
    sv = {}
    wl = dict(wl)

    def f_mix_in(h_, g, w):
        xn, _ = _rms(h_, g[...])
        return (_mm(xn, w[...]),)
    proj, = _rows("mix_in", f_mix_in, s, tm, [(h, 'r0'), (pl_['norm_mix_g'], 'f'), (wl['w_in'], 'f')],
                  [((s, D_MODEL), F32, 'r0')])

    def f_qkv(pr, cos_, sin_, gq, gkv, wq, wk, wv, pm):
        cqn = _rms(pr[:, 0:Q_LORA], gq[...])[0].astype(MXU)
        kvn = _rms(pr[:, Q_LORA:Q_LORA + KV_LORA], gkv[...])[0].astype(MXU)
        krr = _rope(pr[:, 384:512], cos_, sin_, pm)
        qs, ks, vs = [], [], []
        for hd in range(MLA_HEADS):
            qs.append(_rope(_mm(cqn, wq[hd]), cos_, sin_, pm) * MLA_SCALE)
            ks.append(_mm(kvn, wk[hd]) + krr)
            vs.append(_mm(kvn, wv[hd]))
        return jnp.stack(qs), jnp.stack(ks), jnp.stack(vs)
    hshape = (MLA_HEADS, s, HEAD_W)
    q, k, v = _rows("mla_qkv", f_qkv, s, tm,
                    [(proj, 'r0'), (cos, 'r0'), (sin, 'r0'), (pl_['q_norm_g'], 'f'), (pl_['kv_norm_g'], 'f'),
                     (wl['w_uq'], 'f'), (wl['w_k'], 'f'), (wl['w_v'], 'f'), (pmat, 'f')],
                    [(hshape, MXU, 'r1')] * 3)
    def fetch(names, src):
        return _gather(_named(names, src)) if src else None

    def landed(names, got):
        return _layer_weights(dict(zip(names, got))) if got else {}

    a_out, lse, got = _flash_fwd(q, k, v, fetch(MIDDLE + FEED[:1], late))
    wl.update(landed(MIDDLE + FEED[:1], got))

    u_ch = _to_chunks(proj[:, 512:1024])

    def f_s5_in(u, bre, bim):
        outs_r, outs_i = [], []
        for j in range(4):
            uj = _lanes(u, j, LANES)
            rr, ri = _mm_hi(uj, bre[j]), _mm_hi(uj, bim[j])
            outs_r += [_lanes(rr, kk, LANES) for kk in range(4)]
            outs_i += [_lanes(ri, kk, LANES) for kk in range(4)]
        return jnp.stack(outs_r), jnp.stack(outs_i)
    xshape = (16, s, LANES)
    (bu_re, bu_im), got = _hosted(_rows("s5_in", f_s5_in, s, tm,
                                        [(u_ch, 'r0'), (pl_['b_re'], 'f'), (pl_['b_im'], 'f')],
                                        [(xshape, F32, 'r1')] * 2, fetch(FEED[1:2], late)))
    wl.update(landed(FEED[1:2], got))
    x_re, x_im = _scan(bu_re, bu_im, pl_['a_re'], pl_['a_im'], False)

    def f_s5_out(xr, xi, u, cre, cim, d, wglu, bglu):
        y = jnp.concatenate([_mm_hi(_cat_blocks(xr, j), cre[j]) + _mm_hi(_cat_blocks(xi, j), cim[j])
                             for j in range(4)], axis=-1) + d[...] * u
        z = _mm(jax.nn.gelu(y), wglu[...]) + bglu[...]
        return y, y * jax.nn.sigmoid(z)
    (y_ssm, s_out_ch), got = _hosted(_rows(
        "s5_out", f_s5_out, s, tm,
        [(x_re, 'r1'), (x_im, 'r1'), (u_ch, 'r0'), (pl_['c_re'], 'f'), (pl_['c_im'], 'f'),
         (pl_['ssm_d'], 'f'), (wl['ssm_w_glu'], 'f'), (pl_['ssm_b_glu'], 'f')],
        [((s, SSM_WIDTH), F32, 'r0')] * 2, fetch(FEED[2:3], late)))
    wl.update(landed(FEED[2:3], got))
    s_out = _from_chunks(s_out_ch)

    def f_mix_out(h_, a, so, ga, gs, w):
        an = _rms(a, ga[...])[0]
        sn = _rms(so, gs[...])[0]
        return (h_ + _mm(jnp.concatenate([an, sn], axis=-1), w[...]),)
    (h1,), got_e = _hosted(_rows("mix_out", f_mix_out, s, tm,
                                 [(h, 'r0'), (a_out, 'r0'), (s_out, 'r0'), (pl_['attn_out_g'], 'f'),
                                  (pl_['ssm_out_g'], 'f'), (wl['w_out'], 'f')],
                                 [((s, D_MODEL), F32, 'r0')], fetch(EARLY, nxt)))

    m_len = memx.shape[0]

    def f_memkv(mm_, g, w):
        mn = _rms(mm_, g[...])[0].astype(MXU)
        return (jnp.stack([_mm(mn, w[d]) for d in range(N_DEV)]),)
    kvm, = _rows("mem_kv", f_memkv, m_len, m_len, [(memx, 'r0'), (pl_['mem_norm_g'], 'f'), (wl['w_xkv'], 'f')],
                 [((N_DEV, m_len, X_HEAD_DIM), MXU, 'r1')])

    def f_xattn(h_, g, wq, kv_, wo):
        hn = _rms(h_, g[...])[0].astype(MXU)
        out = jnp.zeros(h_.shape, F32)
        for hd in range(X_HEADS):
            cs = pl.ds(hd * X_HEAD_DIM, X_HEAD_DIM)
            qh = _mm(hn, wq[:, cs])
            p = _softmax(_mm_nt(qh, kv_[hd]) * X_SCALE)
            out = out + _mm(_mm(p, kv_[X_HEADS + hd]), wo[cs, :])
        return (h_ + out,)
    (h2,), got_a = _hosted(_rows("xattn", f_xattn, s, min(X_ROWS, s),
                                 [(h1, 'r0'), (pl_['norm_x_g'], 'f'), (wl['w_xq'], 'f'), (kvm, 'f'), (wl['w_xo'], 'f')],
                                 [((s, D_MODEL), F32, 'r0')], fetch(MIDDLE, nxt)))

    def f_ffn(h_, g, wg, wu, wd):
        hn = _rms(h_, g[...])[0].astype(MXU)
        y = jnp.zeros(h_.shape, F32)
        for c in range(D_FF // FF_CHUNK):
            cs = pl.ds(c * FF_CHUNK, FF_CHUNK)
            gate = _mm(hn, wg[:, cs])
            y = y + _mm(gate * jax.nn.sigmoid(gate) * _mm(hn, wu[:, cs]), wd[cs, :])
        return (h_ + y,)
    (h3,), got_b = _hosted(_rows("ffn", f_ffn, s, tm,
                                 [(h2, 'r0'), (pl_['norm_ffn_g'], 'f'), (wl['w_gate'], 'f'), (wl['w_up'], 'f'),
                                  (wl['w_down'], 'f')],
                                 [((s, D_MODEL), F32, 'r0')], fetch(FEED, nxt)))
    sv.update(h=h, proj=proj, q=q, k=k, v=v, a_out=a_out, lse=lse, x_re=x_re, x_im=x_im, y_ssm=y_ssm,
              s_out=s_out, h1=h1, kvm=kvm, h2=h2, u_ch=u_ch)
    return h3, sv, wl, dict(zip(EARLY + MIDDLE + FEED, tuple(got_e) + tuple(got_a) + tuple(got_b)))


def _layer_bwd(dh3, sv, memx, tabs, wl, pl_, nxt=None, own=False):
    s = dh3.shape[0]
    tm = min(256, s)
    cos, sin, pmat, pmat_t = tabs
    gr = {}
    arrived = {}
    act_shape = (s, D_FF)

    def send(*groups):
        keys = [(who, n) for who, names in groups if (nxt if who == 'nxt' else own) for n in names]
        if not keys:
            return None, keys
        return _scatter([nxt[n] if who == 'nxt' else _blocked(gr, n) for who, n in keys]), keys

    def f_ffn_bwd(h_, dy, g, wg, wu, wd):
        hn, r = _rms(h_, g[...])
        hb = hn.astype(MXU)
        dyb = dy.astype(MXU)
        dhn = jnp.zeros(h_.shape, F32)
        acts, dgs, dus = [], [], []
        for c in range(D_FF // FF_CHUNK):
            cs = pl.ds(c * FF_CHUNK, FF_CHUNK)
            gate, up = _mm(hb, wg[:, cs]), _mm(hb, wu[:, cs])
            sg = jax.nn.sigmoid(gate)
            si = gate * sg
            dact = _mm_nt(dyb, wd[cs, :])
            dgate = (dact * up * (sg * (1.0 + gate * (1.0 - sg)))).astype(MXU)
            dup = (dact * si).astype(MXU)
            dhn = dhn + _mm_nt(dgate, wg[:, cs]) + _mm_nt(dup, wu[:, cs])
            acts.append((si * up).astype(MXU))
            dgs.append(dgate)
            dus.append(dup)
        dh, dg = _rms_bwd(h_, g[...], r, dhn)
        cat = lambda parts: jnp.concatenate(parts, axis=-1)
        return dy + dh, hb, cat(acts), cat(dgs), cat(dus), dg
    ex, keys = send(('nxt', ['w_gate', 'w_up']))
    (dh2, hn_f, act, dgate, dup, gr['norm_ffn_g']), got = _hosted(_rows(
        "ffn_bwd", f_ffn_bwd, s, tm,
        [(sv['h2'], 'r0'), (dh3, 'r0'), (pl_['norm_ffn_g'], 'f'), (wl['w_gate'], 'f'), (wl['w_up'], 'f'),
         (wl['w_down'], 'f')],
        [((s, D_MODEL), F32, 'r0'), ((s, D_MODEL), MXU, 'r0'), (act_shape, MXU, 'r0'), (act_shape, MXU, 'r0'),
         (act_shape, MXU, 'r0'), ((1, D_MODEL), F32, 'a')], ex))
    arrived.update(zip(keys, got))
    gr['w_gate'] = _mm_tn_call("dw_gate", hn_f, dgate, tn=FF_CHUNK)
    gr['w_up'] = _mm_tn_call("dw_up", hn_f, dup, tn=FF_CHUNK)
    gr['w_down'] = _mm_tn_call("dw_down", act, dh3, tk=FF_CHUNK)

    m_len = memx.shape[0]

    def f_xattn_bwd(h_, dy, g, wq, kv_, wo):
        hn, r = _rms(h_, g[...])
        hb = hn.astype(MXU)
        dyb = dy.astype(MXU)
        dhn = jnp.zeros(h_.shape, F32)
        dqs, ohs, dks, dvs = [], [], [], []
        for hd in range(X_HEADS):
            cs = pl.ds(hd * X_HEAD_DIM, X_HEAD_DIM)
            kh, vh = kv_[hd], kv_[X_HEADS + hd]
            qh = _mm(hb, wq[:, cs])
            p = _softmax(_mm_nt(qh, kh) * X_SCALE)
            ohs.append(_mm(p, vh).astype(MXU))
            do = _mm_nt(dyb, wo[cs, :])
            dvs.append(_mm_tn(p, do))
            dp = _mm_nt(do, vh)
            ds = p * (dp - jnp.sum(dp * p, axis=-1, keepdims=True)) * X_SCALE
            dq = _mm(ds, kh).astype(MXU)
            dks.append(_mm_tn(ds, qh))
            dhn = dhn + _mm_nt(dq, wq[:, cs])
            dqs.append(dq)
        dh, dg = _rms_bwd(h_, g[...], r, dhn)
        return (dy + dh, hb, jnp.concatenate(dqs, axis=-1), jnp.concatenate(ohs, axis=-1),
                jnp.stack(dks + dvs), dg)
    ex, keys = send(('nxt', ['w_down', 'w_xkv', 'w_xq']))
    (dh1, hn_x, dq_x, oh_x, dkvm, gr['norm_x_g']), got = _hosted(_rows(
        "xattn_bwd", f_xattn_bwd, s, min(X_ROWS, s),
        [(sv['h1'], 'r0'), (dh2, 'r0'), (pl_['norm_x_g'], 'f'), (wl['w_xq'], 'f'), (sv['kvm'], 'f'),
         (wl['w_xo'], 'f')],
        [((s, D_MODEL), F32, 'r0'), ((s, D_MODEL), MXU, 'r0'), ((s, D_MODEL), MXU, 'r0'),
         ((s, D_MODEL), MXU, 'r0'), ((N_DEV, m_len, X_HEAD_DIM), F32, 'a'), ((1, D_MODEL), F32, 'a')], ex))
    arrived.update(zip(keys, got))
    gr['w_xq'] = _mm_tn_call("dw_xq", hn_x, dq_x)
    gr['w_xo'] = _mm_tn_call("dw_xo", oh_x, dh2)

    def f_memkv_bwd(mm_, dkv, g, w):
        mn, r = _rms(mm_, g[...])
        mb = mn.astype(MXU)
        dmn = jnp.zeros(mm_.shape, F32)
        dws = []
        for d in range(N_DEV):
            dmn = dmn + _mm_nt(dkv[d], w[d])
            dws.append(_mm_tn(mb, dkv[d]))
        _, dg = _rms_bwd(mm_, g[...], r, dmn)
        return jnp.stack(dws), dg
    gr['w_xkv'], gr['mem_norm_g'] = _rows(
        "mem_kv_bwd", f_memkv_bwd, m_len, m_len,
        [(memx, 'r0'), (dkvm, 'r1'), (pl_['mem_norm_g'], 'f'), (wl['w_xkv'], 'f')],
        [((N_DEV, D_MODEL, X_HEAD_DIM), F32, 'a'), ((1, D_MODEL), F32, 'a')])

    def f_mix_out_bwd(a, so, dy, ga, gs, w):
        dmix = _mm_nt(dy, w[...])
        an, ra = _rms(a, ga[...])
        sn, rs = _rms(so, gs[...])
        da, dga = _rms_bwd(a, ga[...], ra, dmix[:, 0:512])
        dso, dgs = _rms_bwd(so, gs[...], rs, dmix[:, 512:1024])
        return da, dso, jnp.concatenate([an, sn], axis=-1), dga, dgs
    da_out, ds_out, mixed, gr['attn_out_g'], gr['ssm_out_g'] = _rows(
        "mix_out_bwd", f_mix_out_bwd, s, tm,
        [(sv['a_out'], 'r0'), (sv['s_out'], 'r0'), (dh1, 'r0'), (pl_['attn_out_g'], 'f'), (pl_['ssm_out_g'], 'f'),
         (wl['w_out'], 'f')],
        [((s, 512), F32, 'r0'), ((s, 512), F32, 'r0'), ((s, D_MODEL), MXU, 'r0'), ((1, 512), F32, 'a'),
         ((1, 512), F32, 'a')])
    gr['w_out'] = _mm_tn_call("dw_out", mixed, dh1)

    ex, keys = send(('nxt', ['w_in', 'w_uq', 'w_ukv', 'ssm_w_glu', 'w_out', 'w_xo']), ('own', ['w_gate', 'w_up']))
    dq, dk, dv, got = _flash_bwd(sv['q'], sv['k'], sv['v'], sv['a_out'], sv['lse'], da_out, ex)
    arrived.update(zip(keys, got))

    def f_s5_out_bwd(xr, xi, u, y, ds, cre, cim, d, wglu, bglu):
        g, gelu_vjp = jax.vjp(jax.nn.gelu, y)
        sig = jax.nn.sigmoid(_mm(g, wglu[...]) + bglu[...])
        dz = ds * y * sig * (1.0 - sig)
        dy = ds * sig + gelu_vjp(_mm_nt(dz, wglu[...]))[0]
        dxr, dxi, dcr, dci = [], [], [], []
        for j in range(4):
            dyj = _lanes(dy, j, LANES)
            tr_, ti_ = _mm_nt(dyj, cre[j]), _mm_nt(dyj, cim[j])
            dxr += [_lanes(tr_, kk, LANES) for kk in range(4)]
            dxi += [_lanes(ti_, kk, LANES) for kk in range(4)]
            dcr.append(_mm_tn(_cat_blocks(xr, j), dyj))
            dci.append(_mm_tn(_cat_blocks(xi, j), dyj))
        return (jnp.stack(dxr), jnp.stack(dxi), dy * d[...], jnp.stack(dcr), jnp.stack(dci),
                jnp.sum(dy * u, axis=0, keepdims=True), _mm_tn(g, dz), jnp.sum(dz, axis=0, keepdims=True))
    xshape = (16, s, LANES)
    ex, keys = send(('own', ['w_down']))
    (dx_re, dx_im, du_dir, gr['c_re'], gr['c_im'], gr['ssm_d'], gr['ssm_w_glu'], gr['ssm_b_glu']), got = _hosted(_rows(
        "s5_out_bwd", f_s5_out_bwd, s, tm,
        [(sv['x_re'], 'r1'), (sv['x_im'], 'r1'), (sv['u_ch'], 'r0'), (sv['y_ssm'], 'r0'), (_to_chunks(ds_out), 'r0'),
         (pl_['c_re'], 'f'), (pl_['c_im'], 'f'), (pl_['ssm_d'], 'f'), (wl['ssm_w_glu'], 'f'),
         (pl_['ssm_b_glu'], 'f')],
        [(xshape, F32, 'r1'), (xshape, F32, 'r1'), ((s, 512), F32, 'r0'), ((4, 512, LANES), F32, 'a'),
         ((4, 512, LANES), F32, 'a'), ((1, 512), F32, 'a'), ((512, 512), F32, 'a'), ((1, 512), F32, 'a')], ex))
    arrived.update(zip(keys, got))
    g_re, g_im = _scan(dx_re, dx_im, pl_['a_re'], -pl_['a_im'], True)
    first_re = jnp.pad(sv['x_re'][:, s - 8:s - 1], ((0, 0), (1, 0), (0, 0)))
    first_im = jnp.pad(sv['x_im'][:, s - 8:s - 1], ((0, 0), (1, 0), (0, 0)))

    def f_s5_in_bwd(gre, gim, xr, xi, pr8, pi8, u, dud, f8r, f8i, bre, bim):
        first = pl.program_id(0) == 0
        xpr = jnp.concatenate([jnp.where(first, f8r[...], pr8), xr[:, :tm - 8]], axis=1)
        xpi = jnp.concatenate([jnp.where(first, f8i[...], pi8), xi[:, :tm - 8]], axis=1)
        dus, dbr, dbi = [], [], []
        for j in range(4):
            gj_r, gj_i, uj = _cat_blocks(gre, j), _cat_blocks(gim, j), _lanes(u, j, LANES)
            dus.append(_mm_nt(gj_r, bre[j]) + _mm_nt(gj_i, bim[j]))
            dbr.append(_mm_tn(uj, gj_r))
            dbi.append(_mm_tn(uj, gj_i))
        da_r = jnp.sum(gre * xpr + gim * xpi, axis=1, keepdims=True)
        da_i = jnp.sum(gim * xpr - gre * xpi, axis=1, keepdims=True)
        return dud + jnp.concatenate(dus, axis=-1), jnp.stack(dbr), jnp.stack(dbi), da_r, da_i
    ex, keys = send(('own', ['w_xkv']))
    (du_ch, gr['b_re'], gr['b_im'], gr['a_re'], gr['a_im']), got = _hosted(_rows(
        "s5_in_bwd", f_s5_in_bwd, s, tm,
        [(g_re, 'r1'), (g_im, 'r1'), (sv['x_re'], 'r1'), (sv['x_im'], 'r1'), (sv['x_re'], 'p8'), (sv['x_im'], 'p8'),
         (sv['u_ch'], 'r0'), (du_dir, 'r0'), (first_re, 'f'), (first_im, 'f'), (pl_['b_re'], 'f'), (pl_['b_im'], 'f')],
        [((s, 512), F32, 'r0'), ((4, LANES, 512), F32, 'a'), ((4, LANES, 512), F32, 'a'),
         ((16, 1, LANES), F32, 'a'), ((16, 1, LANES), F32, 'a')], ex))
    arrived.update(zip(keys, got))
    du = _from_chunks(du_ch)

    def f_qkv_bwd(pr, cos_, sin_, dq_, dk_, dv_, gq, gkv, wq, wk, wv, pt):
        cq, ckv = pr[:, 0:Q_LORA], pr[:, Q_LORA:Q_LORA + KV_LORA]
        cqn, rq = _rms(cq, gq[...])
        kvn, rkv = _rms(ckv, gkv[...])
        cqb, kvb = cqn.astype(MXU), kvn.astype(MXU)
        dcqn = jnp.zeros(cq.shape, F32)
        dkvn = jnp.zeros(ckv.shape, F32)
        dksum = jnp.zeros(dk_[0].shape, F32)
        dwq, dwk, dwv = [], [], []
        for hd in range(MLA_HEADS):
            dqp = (_rope_t(dq_[hd], cos_, sin_, pt) * MLA_SCALE).astype(MXU)
            dkb, dvb = dk_[hd].astype(MXU), dv_[hd].astype(MXU)
            dwq.append(_mm_tn(cqb, dqp))
            dwk.append(_mm_tn(kvb, dkb))
            dwv.append(_mm_tn(kvb, dvb))
            dcqn = dcqn + _mm_nt(dqp, wq[hd])
            dkvn = dkvn + _mm_nt(dkb, wk[hd]) + _mm_nt(dvb, wv[hd])
            dksum = dksum + dk_[hd]
        dcq, dgq = _rms_bwd(cq, gq[...], rq, dcqn)
        dckv, dgkv = _rms_bwd(ckv, gkv[...], rkv, dkvn)
        dpa = jnp.concatenate([dcq, dckv, _rope_t(dksum, cos_, sin_, pt)], axis=-1)
        return dpa, jnp.stack(dwq), jnp.stack(dwk), jnp.stack(dwv), dgq, dgkv
    ex, keys = send(('own', ['w_xq', 'w_xo']))
    (dpa, gr['w_uq'], gr['w_k'], gr['w_v'], gr['q_norm_g'], gr['kv_norm_g']), got = _hosted(_rows(
        "mla_qkv_bwd", f_qkv_bwd, s, tm,
        [(sv['proj'], 'r0'), (cos, 'r0'), (sin, 'r0'), (dq, 'r1'), (dk, 'r1'), (dv, 'r1'), (pl_['q_norm_g'], 'f'),
         (pl_['kv_norm_g'], 'f'), (wl['w_uq'], 'f'), (wl['w_k'], 'f'), (wl['w_v'], 'f'), (pmat_t, 'f')],
        [((s, 512), F32, 'r0'), ((MLA_HEADS, Q_LORA, HEAD_W), F32, 'a'), ((MLA_HEADS, KV_LORA, HEAD_W), F32, 'a'),
         ((MLA_HEADS, KV_LORA, HEAD_W), F32, 'a'), ((1, Q_LORA), F32, 'a'), ((1, KV_LORA), F32, 'a')], ex))
    arrived.update(zip(keys, got))

    def f_mix_in_bwd(h_, dpa_, du_, dres, g, w):
        dproj = jnp.concatenate([dpa_, du_], axis=-1).astype(MXU)
        xn, r = _rms(h_, g[...])
        dh, dg = _rms_bwd(h_, g[...], r, _mm_nt(dproj, w[...]))
        return dres + dh, xn, dproj, dg
    ex, keys = send(('own', ['ssm_w_glu', 'w_out']))
    (dh0, xn, dproj, gr['norm_mix_g']), got = _hosted(_rows(
        "mix_in_bwd", f_mix_in_bwd, s, tm,
        [(sv['h'], 'r0'), (dpa, 'r0'), (du, 'r0'), (dh1, 'r0'), (pl_['norm_mix_g'], 'f'), (wl['w_in'], 'f')],
        [((s, D_MODEL), F32, 'r0'), ((s, D_MODEL), MXU, 'r0'), ((s, D_MODEL), MXU, 'r0'), ((1, D_MODEL), F32, 'a')],
        ex))
    arrived.update(zip(keys, got))
    gr['w_in'] = _mm_tn_call("dw_in", xn, dproj)
    return dh0, gr, arrived


def _layer_weights(w):
    wl = {}
    if 'w_in' in w:
        w_in = w['w_in'].reshape(D_MODEL, -1)
        z = lambda n: jnp.zeros((D_MODEL, n), w_in.dtype)
        wl['w_in'] = jnp.concatenate([w_in[:, :384], z(64), w_in[:, 384:416], z(32), w_in[:, 416:]], axis=1)
    if 'w_uq' in w:
        wl['w_uq'] = jnp.pad(w['w_uq'], ((0, 0), (0, 0), (0, HEAD_W - QK_NOPE - QK_ROPE)))
    if 'w_ukv' in w:
        wl['w_k'] = jnp.pad(w['w_ukv'][..., :QK_NOPE], ((0, 0), (0, 0), (0, HEAD_W - QK_NOPE)))
        wv = w['w_ukv'][..., QK_NOPE:]
        even = (jnp.arange(MLA_HEADS) % 2 == 0)[:, None, None]
        wl['w_v'] = jnp.concatenate([jnp.where(even, wv, 0), jnp.where(even, 0, wv)], axis=-1).astype(wv.dtype)
    if 'ssm_w_glu' in w:
        wl['ssm_w_glu'] = w['ssm_w_glu'].reshape(SSM_WIDTH, SSM_WIDTH)
    for n in ('w_out', 'w_xq', 'w_xo'):
        if n in w:
            wl[n] = w[n].reshape(D_MODEL, D_MODEL)
    if 'w_xkv' in w:
        wl['w_xkv'] = w['w_xkv']
    for n in ('w_gate', 'w_up'):
        if n in w:
            wl[n] = jnp.transpose(w[n], (1, 0, 2)).reshape(D_MODEL, D_FF)
    if 'w_down' in w:
        wl['w_down'] = w['w_down'].reshape(D_FF, D_MODEL)
    return wl


def _blocked(gr, n):
    if n == 'w_in':
        d = gr['w_in']
        out = jnp.concatenate([d[:, :384], d[:, 448:480], d[:, 512:]], axis=1).reshape(N_DEV, 128, -1)
    elif n == 'w_uq':
        out = gr['w_uq'][..., :QK_NOPE + QK_ROPE]
    elif n == 'w_ukv':
        even = (jnp.arange(MLA_HEADS) % 2 == 0)[:, None, None]
        dv = gr['w_v']
        out = jnp.concatenate([gr['w_k'][..., :QK_NOPE], jnp.where(even, dv[..., :V_DIM], dv[..., V_DIM:])], axis=-1)
    elif n == 'ssm_w_glu':
        out = gr['ssm_w_glu'].reshape(N_DEV, SSM_WIDTH // N_DEV, SSM_WIDTH)
    elif n in ('w_out', 'w_xq', 'w_xo'):
        out = gr[n].reshape(N_DEV, D_MODEL // N_DEV, D_MODEL)
    elif n in ('w_gate', 'w_up'):
        out = jnp.transpose(gr[n].reshape(D_MODEL, N_DEV, D_FF // N_DEV), (1, 0, 2))
    elif n == 'w_down':
        out = gr[n].reshape(N_DEV, D_FF // N_DEV, D_MODEL)
    else:
        out = gr[n]
    return out.astype(MXU)


def kernel(x, mem, positions, norm_mix_g, w_in, q_norm_g, w_uq, kv_norm_g, w_ukv, ssm_lambda_re, ssm_lambda_im, ssm_log_dt, ssm_b_re, ssm_b_im, ssm_c_re, ssm_c_im, ssm_d, ssm_w_glu, ssm_b_glu, attn_out_g, ssm_out_g, w_out, norm_x_g, mem_norm_g, w_xq, w_xkv, w_xo, norm_ffn_g, w_gate, w_up, w_down, final_norm_g, loss_target, m_norm_mix_g, m_w_in, m_q_norm_g, m_w_uq, m_kv_norm_g, m_w_ukv, m_ssm_lambda_re, m_ssm_lambda_im, m_ssm_log_dt, m_ssm_b_re, m_ssm_b_im, m_ssm_c_re, m_ssm_c_im, m_ssm_d, m_ssm_w_glu, m_ssm_b_glu, m_attn_out_g, m_ssm_out_g, m_w_out, m_norm_x_g, m_mem_norm_g, m_w_xq, m_w_xkv, m_w_xo, m_norm_ffn_g, m_w_gate, m_w_up, m_w_down, m_final_norm_g, v_norm_mix_g, v_w_in, v_q_norm_g, v_w_uq, v_kv_norm_g, v_w_ukv, v_ssm_lambda_re, v_ssm_lambda_im, v_ssm_log_dt, v_ssm_b_re, v_ssm_b_im, v_ssm_c_re, v_ssm_c_im, v_ssm_d, v_ssm_w_glu, v_ssm_b_glu, v_attn_out_g, v_ssm_out_g, v_w_out, v_norm_x_g, v_mem_norm_g, v_w_xq, v_w_xkv, v_w_xo, v_norm_ffn_g, v_w_gate, v_w_up, v_w_down, v_final_norm_g):
    args = dict(locals())
    W = {n: args[n] for n in WEIGHTS}
    M = {n: args['m_' + n] for n in WEIGHTS}
    V = {n: args['v_' + n] for n in WEIGHTS}
    s = x.shape[1]
    h = x[0]
    memx = mem[0]

    freqs = ROPE_THETA ** (-jnp.arange(0, QK_ROPE, 2, dtype=F32) / QK_ROPE)
    ang = positions[0].astype(F32)[:, None] * freqs
    c16, s16 = jnp.cos(ang), jnp.sin(ang)
    cos = jnp.concatenate([jnp.ones((s, QK_NOPE), F32), c16, c16, jnp.zeros((s, 32), F32)], axis=1)
    sin = jnp.concatenate([jnp.zeros((s, QK_NOPE), F32), s16, s16, jnp.zeros((s, 32), F32)], axis=1)
    idx = jnp.arange(QK_ROPE // 2)
    pmat = jnp.zeros((HEAD_W, HEAD_W), F32)
    pmat = pmat.at[QK_NOPE + 16 + idx, QK_NOPE + idx].set(-1.0).at[QK_NOPE + idx, QK_NOPE + 16 + idx].set(1.0)
    tabs = (cos, sin, pmat, pmat.T)

    shards = [{n: W[n][l].astype(MXU) for n in SHARDED} for l in range(DEPTH)]
    gathered = dict(zip(EARLY, _run_exchange("gather_weights", _gather(_named(EARLY, shards[0])))))

    layers = []
    for l in range(DEPTH):
        wl = _layer_weights(gathered)
        s5_in = [W[n][l] for n in ('ssm_lambda_re', 'ssm_lambda_im', 'ssm_log_dt', 'ssm_b_re', 'ssm_b_im',
                                   'ssm_c_re', 'ssm_c_im')]
        (a_re, a_im, bre, bim, cre, cim), s5_vjp = jax.vjp(_s5_params, *s5_in)
        pl_ = {n: W[n][l][None] for n in ('norm_mix_g', 'q_norm_g', 'kv_norm_g', 'ssm_d', 'ssm_b_glu',
                                           'attn_out_g', 'ssm_out_g', 'norm_x_g', 'mem_norm_g', 'norm_ffn_g')}
        pl_.update(a_re=a_re, a_im=a_im, b_re=bre, b_im=bim, c_re=cre, c_im=cim)
        h, sv, wl, gathered = _layer_fwd(h, memx, tabs, wl, pl_, shards[0] if l == 0 else None,
                                         shards[l + 1] if l + 1 < DEPTH else None)
        layers.append((wl, pl_, sv, s5_vjp))

    def f_loss(h_, tgt, g):
        y, r = _rms(h_, g[...])
        err = y - tgt
        part = 0.5 * jnp.sum(jnp.mean(err * err, axis=-1, keepdims=True), axis=0, keepdims=True)
        dh, dg = _rms_bwd(h_, g[...], r, err / D_MODEL)
        return dh, dg, jnp.broadcast_to(part, (8, LANES))
    dh, g_final, loss_part = _rows(
        "loss_head", f_loss, s, min(256, s), [(h, 'r0'), (loss_target[0], 'r0'), (final_norm_g[None], 'f')],
        [((s, D_MODEL), F32, 'r0'), ((1, D_MODEL), F32, 'a'), ((8, LANES), F32, 'a')])
    loss = lax.psum(loss_part[0, 0], ("x", "y", "c"))

    parts = [{} for _ in range(DEPTH)]
    g_rep = [None] * DEPTH
    blocks = None
    for l in reversed(range(DEPTH)):
        wl, pl_, sv, s5_vjp = layers[l]
        dh, gr, arrived = _layer_bwd(dh, sv, memx, tabs, wl, pl_, blocks, own=(l == 0))
        for (who, n), p in arrived.items():
            parts[l + 1 if who == 'nxt' else l][n] = p
        blocks = {n: _blocked(gr, n) for n in SHARDED} if l > 0 else None
        ds5 = s5_vjp((gr['a_re'], gr['a_im'], gr['b_re'], gr['b_im'], gr['c_re'], gr['c_im']))
        rep = dict(zip(('ssm_lambda_re', 'ssm_lambda_im', 'ssm_log_dt', 'ssm_b_re', 'ssm_b_im', 'ssm_c_re',
                        'ssm_c_im'), ds5))
        for n in ('norm_mix_g', 'q_norm_g', 'kv_norm_g', 'ssm_d', 'ssm_b_glu', 'attn_out_g', 'ssm_out_g',
                  'norm_x_g', 'mem_norm_g', 'norm_ffn_g'):
            rep[n] = gr[n][0]
        g_rep[l] = rep
    grad_x = dh[None]

    rep_names = REPL_L + ['final_norm_g']
    g_loc = {n: jnp.stack([g_rep[l][n] for l in range(DEPTH)]) for n in REPL_L}
    g_loc['final_norm_g'] = g_final
    rest = [n for n in SHARDED if n not in parts[0]]
    last = _run_exchange("last_grads", _together(_scatter([_blocked(gr, n) for n in rest]),
                                                 _gather([_pack(_named(rep_names, g_loc))])))
    parts[0].update(zip(rest, last[:len(rest)]))

    out_sh = [{}, {}, {}, {}]
    for n in SHARDED:
        res = _adamw_weight("adamw_" + n, [parts[l][n] for l in range(DEPTH)], W[n], M[n], V[n])
        for kind, r in enumerate(res):
            out_sh[kind][n] = r

    shapes_rp = [(1,) + W[n].shape if W[n].ndim == 1 else W[n].shape for n in rep_names]
    g_rp = _unpack(_sum_sources("sum_small_grads", last[len(rest)]), shapes_rp)
    as_rows = lambda d: [d[n].reshape(shp) for n, shp in zip(rep_names, shapes_rp)]
    res_rp = (g_rp,) + _adamw_small("adamw_replicated", g_rp, as_rows(W), as_rows(M), as_rows(V))
    out_rp = [{n: a.reshape(W[n].shape) for n, a in zip(rep_names, r)} for r in res_rp]

    outs = [loss, grad_x]
    for kind in range(4):
        for n in WEIGHTS:
            outs.append(out_sh[kind][n] if n in SHARDED else out_rp[kind][n])
    return tuple(outs)
```

```python
from typing import Callable, NamedTuple

import jax
import jax.numpy as jnp
from jax import lax
from jax.experimental import pallas as pl
from jax.experimental.pallas import tpu as pltpu

F32 = jnp.float32
MXU = jnp.bfloat16
HI = lax.Precision.HIGHEST

D_MODEL = 1024
MLA_HEADS = 8
QK_NOPE = 64
QK_ROPE = 32
V_DIM = 64
Q_LORA = 256
KV_LORA = 128
SSM_WIDTH = 512
SSM_GROUPS = 32
SSM_GROUP = 16
SSM_STATE = 64
X_HEADS = 4
X_HEAD_DIM = 256
D_FF = 2816
FF_CHUNK = D_FF // 2
ROPE_THETA = 10000.0
EPS = 1e-6
DEPTH = 2
N_DEV = 8
LANES = 128
HEAD_W = 128
MLA_SCALE = (QK_NOPE + QK_ROPE) ** -0.5
X_SCALE = X_HEAD_DIM ** -0.5
ADAM_LR, ADAM_B1, ADAM_B2, ADAM_EPS, ADAM_WD, ADAM_STEP = 0.001, 0.9, 0.999, 1e-08, 0.01, 10
VMEM_LIMIT = 56 * 1024 * 1024
FLASH_TILE = 512
DW_ROWS = 2048
X_ROWS = 512
MESH = pl.DeviceIdType.MESH

SHARDED = ['w_in', 'w_uq', 'w_ukv', 'ssm_w_glu', 'w_out', 'w_xq', 'w_xkv', 'w_xo', 'w_gate', 'w_up', 'w_down']
REPL_L = ['norm_mix_g', 'q_norm_g', 'kv_norm_g', 'ssm_lambda_re', 'ssm_lambda_im', 'ssm_log_dt', 'ssm_b_re',
          'ssm_b_im', 'ssm_c_re', 'ssm_c_im', 'ssm_d', 'ssm_b_glu', 'attn_out_g', 'ssm_out_g', 'norm_x_g',
          'mem_norm_g', 'norm_ffn_g']
WEIGHTS = ['norm_mix_g', 'w_in', 'q_norm_g', 'w_uq', 'kv_norm_g', 'w_ukv', 'ssm_lambda_re', 'ssm_lambda_im',
           'ssm_log_dt', 'ssm_b_re', 'ssm_b_im', 'ssm_c_re', 'ssm_c_im', 'ssm_d', 'ssm_w_glu', 'ssm_b_glu',
           'attn_out_g', 'ssm_out_g', 'w_out', 'norm_x_g', 'mem_norm_g', 'w_xq', 'w_xkv', 'w_xo', 'norm_ffn_g',
           'w_gate', 'w_up', 'w_down', 'final_norm_g']


def _pcall(body, **kw):
    return pl.pallas_call(body, **kw)


def _mm(a, b):
    return jnp.dot(a.astype(MXU), b.astype(MXU), preferred_element_type=F32)


def _mm_nt(a, b):
    return lax.dot_general(a.astype(MXU), b.astype(MXU), (((1,), (1,)), ((), ())), preferred_element_type=F32)


def _mm_tn(a, b):
    return lax.dot_general(a.astype(MXU), b.astype(MXU), (((0,), (0,)), ((), ())), preferred_element_type=F32)


def _mm_hi(a, b):
    return jnp.dot(a.astype(F32), b.astype(F32), precision=HI, preferred_element_type=F32)


def _rms(x, g):
    r = lax.rsqrt(jnp.mean(x * x, axis=-1, keepdims=True) + EPS)
    return x * r * g, r


def _rms_bwd(x, g, r, dy):
    dyg = dy * g
    dx = r * dyg - x * (r * r * r) * jnp.mean(dyg * x, axis=-1, keepdims=True)
    return dx, jnp.sum(dy * x * r, axis=0, keepdims=True)


def _rope(x, cos, sin, p_ref):
    return x * cos + _mm_hi(x, p_ref[...]) * sin


def _rope_t(g, cos, sin, pt_ref):
    return g * cos + _mm_hi(g * sin, pt_ref[...])


def _softmax(s):
    m = jnp.max(s, axis=-1, keepdims=True)
    e = jnp.exp(s - m)
    return e / jnp.sum(e, axis=-1, keepdims=True)


def _lanes(x, j, w):
    return x[:, j * w:(j + 1) * w]


def _rows(name, fn, n, tm, ins, outs, side=None):
    def spec(shape, kind):
        nd = len(shape)
        if kind == 'p8':
            return pl.BlockSpec((shape[0], 8, shape[2]), lambda i: (0, jnp.maximum(i * (tm // 8) - 1, 0), 0))
        if kind == 'f':
            return pl.BlockSpec(shape, lambda i, _nd=nd: (0,) * _nd, pipeline_mode=pl.Buffered(1))
        if kind == 'a':
            return pl.BlockSpec(shape, lambda i, _nd=nd: (0,) * _nd)
        ax = int(kind[1])
        blk = tuple(tm if d == ax else s for d, s in enumerate(shape))
        return pl.BlockSpec(blk, lambda i, _ax=ax, _nd=nd: tuple(i if d == _ax else 0 for d in range(_nd)))

    n_in, n_out, n_steps = len(ins), len(outs), n // tm

    def body(*refs):
        in_refs, out_refs, steps = _side_split(refs, n_in, n_out, side)
        i = pl.program_id(0)
        if steps:
            pl.when(i == 0)(steps[0])
            pl.when(i == _pass_on_step(n_steps))(steps[1])
        args = [r if k == 'f' else r[...] for r, (_, k) in zip(in_refs, ins)]
        res = fn(*args)
        for r, (_, dt, k), v in zip(out_refs, outs, res):
            if k == 'a':
                _accumulate(r, v.astype(dt), i)
            else:
                r[...] = v.astype(dt)
        if steps:
            pl.when(i == n_steps - 1)(steps[2])

    s_in, s_out, s_shape, s_sems, s_ops = _side_args(side)
    res = _pcall(
        body, name=name + ("_x" if side else ""), grid=(n_steps,),
        in_specs=[spec(a.shape, k) for a, k in ins] + s_in,
        out_specs=[spec(s, k) for s, _, k in outs] + s_out,
        out_shape=[jax.ShapeDtypeStruct(s, dt) for s, dt, _ in outs] + s_shape,
        scratch_shapes=s_sems,
        compiler_params=pltpu.CompilerParams(dimension_semantics=("arbitrary",), vmem_limit_bytes=VMEM_LIMIT),
    )(*[a for a, _ in ins], *s_ops)
    return _Hosted(res[:n_out], res[n_out:]) if side else res


def _accumulate(ref, v, i):
    @pl.when(i == 0)
    def _():
        ref[...] = v

    @pl.when(i != 0)
    def _():
        ref[...] += v


def _mm_tn_call(name, a, b, tk=None, tn=None):
    out_dtype = MXU
    s, k = a.shape
    n = b.shape[1]
    tk, tn = tk or k, tn or n
    ts = min(DW_ROWS, s)
    ns = s // ts

    def body(a_ref, b_ref, o_ref, acc_ref):
        j = pl.program_id(2)
        _accumulate(acc_ref, _mm_tn(a_ref[...], b_ref[...]), j)

        @pl.when(j == ns - 1)
        def _():
            o_ref[...] = acc_ref[...].astype(out_dtype)

    return _pcall(
        body, name=name, grid=(k // tk, n // tn, ns),
        in_specs=[pl.BlockSpec((ts, tk), lambda ik, jn, j: (j, ik)),
                  pl.BlockSpec((ts, tn), lambda ik, jn, j: (j, jn))],
        out_specs=pl.BlockSpec((tk, tn), lambda ik, jn, j: (ik, jn)),
        out_shape=jax.ShapeDtypeStruct((k, n), out_dtype),
        scratch_shapes=[pltpu.VMEM((tk, tn), F32)],
        compiler_params=pltpu.CompilerParams(dimension_semantics=("arbitrary", "arbitrary", "arbitrary"),
                                             vmem_limit_bytes=VMEM_LIMIT),
    )(a, b)


def _side_split(refs, n_in, n_out, side):
    if side is None:
        return refs[:n_in], refs[n_in:n_in + n_out], None
    si, so = len(side.ins), len(side.out_shapes)
    own_in, side_in = refs[:n_in], refs[n_in:n_in + si]
    own_out, side_out = refs[n_in + si:n_in + si + n_out], refs[n_in + si + n_out:n_in + si + n_out + so]
    return own_in, own_out, side.steps(side_in, side_out, refs[n_in + si + n_out + so:])


def _pass_on_step(n_steps):
    return max(n_steps - 2, 0)


def _side_args(side):
    if side is None:
        return [], [], [], [], []
    any_spec = pl.BlockSpec(memory_space=pl.ANY)
    return ([any_spec] * len(side.ins), [any_spec] * len(side.out_shapes), list(side.out_shapes),
            list(side.sem_shapes), list(side.ins))


class _Hosted(NamedTuple):
    results: list
    arrived: list


def _hosted(res):
    return res if isinstance(res, _Hosted) else _Hosted(res, ())


def _flash_fwd(q, k, v, side=None):
    nh, s, w = q.shape
    t = min(FLASH_TILE, s)
    nq = s // t
    n_steps = (nh // 2) * nq

    def body(*refs):
        (q_ref, k_ref, v_ref), (o_ref, lse_ref), steps = _side_split(refs, 3, 2, side)
        step = pl.program_id(0) * nq + pl.program_id(1)
        if steps:
            pl.when(step == 0)(steps[0])
            pl.when(step == _pass_on_step(n_steps))(steps[1])
        qi = pl.program_id(1)
        qs = [q_ref[0], q_ref[1]]
        below = lax.broadcasted_iota(jnp.int32, (t, t), 1) <= lax.broadcasted_iota(jnp.int32, (t, t), 0)

        def tile(j, carry, diagonal):
            sl = pl.ds(pl.multiple_of(j * t, t), t)
            out = []
            for hh in range(2):
                m, l, acc = carry[3 * hh:3 * hh + 3]
                sc = _mm_nt(qs[hh], k_ref[hh, sl, :])
                if diagonal:
                    sc = jnp.where(below, sc, -1e30)
                m_new = jnp.maximum(m, jnp.max(sc, axis=1, keepdims=True))
                p = jnp.exp(sc - m_new)
                alpha = jnp.exp(m - m_new)
                out += [m_new, alpha * l + jnp.sum(p, axis=1, keepdims=True), alpha * acc + _mm(p, v_ref[hh, sl, :])]
            return tuple(out)

        init = (jnp.full((t, 1), -1e30, F32), jnp.zeros((t, 1), F32), jnp.zeros((t, w), F32)) * 2
        carry = lax.fori_loop(0, qi, lambda j, c: tile(j, c, False), init)
        carry = tile(qi, carry, True)
        o_ref[...] = carry[2] / carry[1] + carry[5] / carry[4]
        for hh in range(2):
            lse_ref[hh] = jnp.broadcast_to(carry[3 * hh] + jnp.log(carry[3 * hh + 1]), (t, w))
        if steps:
            pl.when(step == n_steps - 1)(steps[2])

    s_in, s_out, s_shape, s_sems, s_ops = _side_args(side)
    res = _pcall(
        body, name="mla_flash_fwd" + ("_x" if side else ""), grid=(nh // 2, nq),
        in_specs=[pl.BlockSpec((2, t, w), lambda p, i: (p, i, 0)),
                  pl.BlockSpec((2, s, w), lambda p, i: (p, 0, 0)),
                  pl.BlockSpec((2, s, w), lambda p, i: (p, 0, 0))] + s_in,
        out_specs=[pl.BlockSpec((t, w), lambda p, i: (i, p)),
                   pl.BlockSpec((2, t, w), lambda p, i: (p, i, 0))] + s_out,
        out_shape=[jax.ShapeDtypeStruct((s, (nh // 2) * w), F32), jax.ShapeDtypeStruct((nh, s, w), F32)] + s_shape,
        scratch_shapes=s_sems,
        compiler_params=pltpu.CompilerParams(dimension_semantics=("arbitrary", "arbitrary"),
                                             vmem_limit_bytes=VMEM_LIMIT),
    )(q, k, v, *s_ops)
    return res[0], res[1], res[2:]


def _flash_bwd(q, k, v, o, lse, do, side=None):
    nh, s, w = q.shape
    t = min(FLASH_TILE, s)
    nq = s // t
    n_steps = (nh // 2) * nq

    def body(*refs):
        (q_ref, k_ref, v_ref, o_ref, lse_ref, do_ref), (dq_ref, dk_ref, dv_ref), steps = _side_split(refs, 6, 3, side)
        step = pl.program_id(0) * nq + pl.program_id(1)
        if steps:
            pl.when(step == 0)(steps[0])
            pl.when(step == _pass_on_step(n_steps))(steps[1])
        j = pl.program_id(1)

        @pl.when(j == 0)
        def _():
            dq_ref[...] = jnp.zeros(dq_ref.shape, F32)

        below = lax.broadcasted_iota(jnp.int32, (t, t), 1) <= lax.broadcasted_iota(jnp.int32, (t, t), 0)
        lane = lax.broadcasted_iota(jnp.int32, (t, w), 1)
        heads = [jnp.logical_and(lane >= hh * V_DIM, lane < (hh + 1) * V_DIM) for hh in range(2)]
        ks = [k_ref[0], k_ref[1]]
        vs = [v_ref[0], v_ref[1]]

        def tile(i, carry, diagonal):
            sl = pl.ds(pl.multiple_of(i * t, t), t)
            dout_all, o_all = do_ref[sl, :], o_ref[sl, :]
            out = []
            for hh in range(2):
                dk, dv = carry[2 * hh], carry[2 * hh + 1]
                qh = q_ref[hh, sl, :]
                dout = jnp.where(heads[hh], dout_all, 0.0)
                sc = _mm_nt(qh, ks[hh])
                if diagonal:
                    sc = jnp.where(below, sc, -1e30)
                p = jnp.exp(sc - lse_ref[hh, sl, 0:1])
                dp = _mm_nt(dout, vs[hh])
                ds = p * (dp - jnp.sum(dout * o_all, axis=1, keepdims=True))
                dq_ref[hh, sl, :] += _mm(ds, ks[hh])
                out += [dk + _mm_tn(ds, qh), dv + _mm_tn(p, dout)]
            return tuple(out)

        carry = tile(j, (jnp.zeros((t, w), F32),) * 4, True)
        carry = lax.fori_loop(j + 1, nq, lambda i, c: tile(i, c, False), carry)
        for hh in range(2):
            dk_ref[hh] = carry[2 * hh]
            dv_ref[hh] = jnp.where(heads[hh], carry[2 * hh + 1], 0.0)
        if steps:
            pl.when(step == n_steps - 1)(steps[2])

    s_in, s_out, s_shape, s_sems, s_ops = _side_args(side)
    res = _pcall(
        body, name="mla_flash_bwd" + ("_x" if side else ""), grid=(nh // 2, nq),
        in_specs=[pl.BlockSpec((2, s, w), lambda p, j: (p, 0, 0)),
                  pl.BlockSpec((2, t, w), lambda p, j: (p, j, 0)),
                  pl.BlockSpec((2, t, w), lambda p, j: (p, j, 0)),
                  pl.BlockSpec((s, w), lambda p, j: (0, p)),
                  pl.BlockSpec((2, s, w), lambda p, j: (p, 0, 0)),
                  pl.BlockSpec((s, w), lambda p, j: (0, p))] + s_in,
        out_specs=[pl.BlockSpec((2, s, w), lambda p, j: (p, 0, 0)),
                   pl.BlockSpec((2, t, w), lambda p, j: (p, j, 0)),
                   pl.BlockSpec((2, t, w), lambda p, j: (p, j, 0))] + s_out,
        out_shape=[jax.ShapeDtypeStruct((nh, s, w), F32)] * 3 + s_shape,
        scratch_shapes=s_sems,
        compiler_params=pltpu.CompilerParams(dimension_semantics=("arbitrary", "arbitrary"),
                                             vmem_limit_bytes=VMEM_LIMIT),
    )(q, k, v, o, lse, do, *s_ops)
    return res[0], res[1], res[2], res[3:]


def _scan(b_re, b_im, a_re, a_im, reverse):
    nb, s, w = b_re.shape
    ch = s // 8
    assert ch & (ch - 1) == 0
    grp = 2

    def cmul(ar, ai, xr, xi):
        return ar * xr - ai * xi, ar * xi + ai * xr

    def body(br_ref, bi_ref, ar_ref, ai_ref, xr_ref, xi_ref):
        sub = lax.broadcasted_iota(jnp.int32, (8, w), 0)

        def shift(x, k):
            if reverse:
                return jnp.where(sub < 8 - k, pltpu.roll(x, 8 - k, 0), 0.0)
            return jnp.where(sub >= k, pltpu.roll(x, k, 0), 0.0)

        ar = [jnp.broadcast_to(ar_ref[g], (8, w)) for g in range(grp)]
        ai = [jnp.broadcast_to(ai_ref[g], (8, w)) for g in range(grp)]

        def tsl(i):
            return pl.ds(pl.multiple_of(((ch - 1 - i) if reverse else i) * 8, 8), 8)

        def local(i, carry):
            out = []
            for g in range(grp):
                xr, xi = carry[2 * g], carry[2 * g + 1]
                pr, pi = cmul(ar[g], ai[g], xr, xi)
                nr = pr + br_ref[g, tsl(i), :]
                ni = pi + bi_ref[g, tsl(i), :]
                xr_ref[g, tsl(i), :] = nr
                xi_ref[g, tsl(i), :] = ni
                out += [nr, ni]
            return tuple(out)

        fin = lax.fori_loop(0, ch, local, (jnp.zeros((8, w), F32),) * (2 * grp))

        carry_in = []
        for g in range(grp):
            pr, pi = ar[g], ai[g]
            for _ in range(ch.bit_length() - 1):
                pr, pi = cmul(pr, pi, pr, pi)
            fr, fi = fin[2 * g], fin[2 * g + 1]
            for kk in (1, 2, 4):
                sr, si = cmul(pr, pi, shift(fr, kk), shift(fi, kk))
                fr, fi = fr + sr, fi + si
                pr, pi = cmul(pr, pi, pr, pi)
            carry_in += [shift(fr, 1), shift(fi, 1)]

        def fix(i, pw):
            out = []
            for g in range(grp):
                pr, pi = pw[2 * g], pw[2 * g + 1]
                cr, ci = cmul(pr, pi, carry_in[2 * g], carry_in[2 * g + 1])
                xr_ref[g, tsl(i), :] = xr_ref[g, tsl(i), :] + cr
                xi_ref[g, tsl(i), :] = xi_ref[g, tsl(i), :] + ci
                nr, ni = cmul(pr, pi, ar[g], ai[g])
                out += [nr, ni]
            return tuple(out)

        lax.fori_loop(0, ch, fix, tuple(x for g in range(grp) for x in (ar[g], ai[g])))

    blk = pl.BlockSpec((grp, s, w), lambda i: (i, 0, 0))
    ablk = pl.BlockSpec((grp, 1, w), lambda i: (i, 0, 0))
    return _pcall(
        body, name="s5_scan_rev" if reverse else "s5_scan", grid=(nb // grp,),
        in_specs=[blk, blk, ablk, ablk], out_specs=[blk, blk],
        out_shape=[jax.ShapeDtypeStruct((nb, s, w), F32)] * 2,
        compiler_params=pltpu.CompilerParams(dimension_semantics=("arbitrary",), vmem_limit_bytes=VMEM_LIMIT),
    )(b_re, b_im, a_re, a_im)


class _Exchange(NamedTuple):
    ins: list
    out_shapes: list
    sem_shapes: list
    steps: Callable


def _gather_steps(ins, outs, sems):
    n = len(ins)
    send_sems, recv_sems, local_sems = sems
    x, y, c = lax.axis_index("x"), lax.axis_index("y"), lax.axis_index("c")
    me, sibling = (x, y, c), (x, y, 1 - c)
    chips = [(1 - x, y), (x, 1 - y), (1 - x, 1 - y)]

    def copy(a, k, block, to, src=None):
        dst = outs[a].at[4 * block[0] + 2 * block[1] + block[2]]
        return pltpu.make_async_remote_copy(
            src_ref=dst if src is None else src, dst_ref=dst,
            send_sem=send_sems.at[a, k], recv_sem=recv_sems.at[a, k], device_id=to, device_id_type=MESH)

    mine = [pltpu.make_async_copy(ins[a], outs[a].at[4 * x + 2 * y + c], local_sems.at[a]) for a in range(n)]
    first = []
    for a in range(n):
        first.append(copy(a, 0, me, sibling, src=ins[a]))
        first += [copy(a, 1 + j, me, (*chip, c), src=ins[a]) for j, chip in enumerate(chips)]
    passed = [copy(a, 4 + j, (*chip, c), sibling) for j, chip in enumerate(chips) for a in range(n)]

    def start():
        for cp in mine + first:
            cp.start()

    def pass_on():
        i = 0
        for j, chip in enumerate(chips):
            for a in range(n):
                copy(a, 1 + j, (*chip, c), me).wait_recv()
                passed[i].start()
                i += 1

    def finish():
        for a in range(n):
            copy(a, 0, sibling, me).wait_recv()
            for j, chip in enumerate(chips):
                copy(a, 4 + j, (*chip, 1 - c), me).wait_recv()
        for cp in first + passed:
            cp.wait_send()
        for cp in mine:
            cp.wait()

    return start, pass_on, finish


def _gather(arrs):
    n = len(arrs)
    return _Exchange(list(arrs), [jax.ShapeDtypeStruct((N_DEV,) + a.shape, a.dtype) for a in arrs],
                     [pltpu.SemaphoreType.DMA((n, 7)), pltpu.SemaphoreType.DMA((n, 7)), pltpu.SemaphoreType.DMA((n,))],
                     _gather_steps)


def _scatter_steps(ins, outs, sems):
    n = len(ins)
    send_sems, recv_sems, local_sems = sems
    x, y, c = lax.axis_index("x"), lax.axis_index("y"), lax.axis_index("c")
    me = 4 * x + 2 * y + c
    own, sent, arrivals = [], [], []
    for a in range(n):
        own.append(pltpu.make_async_copy(ins[a].at[me], outs[a].at[me], local_sems.at[a]))
        for k in range(1, N_DEV):
            px, py, pc = x ^ ((k >> 2) & 1), y ^ ((k >> 1) & 1), c ^ (k & 1)
            peer = 4 * px + 2 * py + pc
            sent.append(pltpu.make_async_remote_copy(
                src_ref=ins[a].at[peer], dst_ref=outs[a].at[me],
                send_sem=send_sems.at[a, k - 1], recv_sem=recv_sems.at[a, k - 1],
                device_id=(px, py, pc), device_id_type=MESH))
            arrivals.append(pltpu.make_async_remote_copy(
                src_ref=ins[a].at[me], dst_ref=outs[a].at[peer],
                send_sem=send_sems.at[a, k - 1], recv_sem=recv_sems.at[a, k - 1],
                device_id=(x, y, c), device_id_type=MESH))

    def start():
        for cp in own + sent:
            cp.start()

    def pass_on():
        pass

    def finish():
        for cp in arrivals:
            cp.wait_recv()
        for cp in sent:
            cp.wait_send()
        for cp in own:
            cp.wait()

    return start, pass_on, finish


def _scatter(grads):
    n = len(grads)
    return _Exchange(list(grads), [jax.ShapeDtypeStruct(g.shape, g.dtype) for g in grads],
                     [pltpu.SemaphoreType.DMA((n, N_DEV - 1)), pltpu.SemaphoreType.DMA((n, N_DEV - 1)),
                      pltpu.SemaphoreType.DMA((n,))], _scatter_steps)


def _together(a, b):
    def steps(ins, outs, sems):
        sa = a.steps(ins[:len(a.ins)], outs[:len(a.out_shapes)], sems[:len(a.sem_shapes)])
        sb = b.steps(ins[len(a.ins):], outs[len(a.out_shapes):], sems[len(a.sem_shapes):])

        def both(k):
            def run():
                sa[k]()
                sb[k]()
            return run
        return both(0), both(1), both(2)

    return _Exchange(a.ins + b.ins, a.out_shapes + b.out_shapes, a.sem_shapes + b.sem_shapes, steps)


def _run_exchange(name, ex):
    n_in, n_out = len(ex.ins), len(ex.out_shapes)

    def body(*refs):
        for step in ex.steps(refs[:n_in], refs[n_in:n_in + n_out], refs[n_in + n_out:]):
            step()

    any_spec = pl.BlockSpec(memory_space=pl.ANY)
    return _pcall(body, name=name, in_specs=[any_spec] * n_in, out_specs=[any_spec] * n_out,
                  out_shape=list(ex.out_shapes), scratch_shapes=list(ex.sem_shapes))(*ex.ins)


def _adam_math(g, w_, m_, v_):
    m_new = ADAM_B1 * m_ + (1.0 - ADAM_B1) * g
    v_new = ADAM_B2 * v_ + (1.0 - ADAM_B2) * (g * g)
    m_hat = m_new / (1.0 - ADAM_B1 ** ADAM_STEP)
    v_hat = v_new / (1.0 - ADAM_B2 ** ADAM_STEP)
    delta = -ADAM_LR * (m_hat / (jnp.sqrt(v_hat) + ADAM_EPS) + ADAM_WD * w_)
    return delta, m_new, v_new


def _adamw_weight(name, parts, w, m, v):
    nl = len(parts)

    def body(*refs):
        p_refs = refs[:nl]
        w_ref, m_ref, v_ref, g_ref, d_ref, mo_ref, vo_ref = refs[nl:]
        for l in range(nl):
            g = p_refs[l][0].astype(F32)
            for j in range(1, N_DEV):
                g = g + p_refs[l][j].astype(F32)
            g_ref[l] = g
            d_ref[l], mo_ref[l], vo_ref[l] = _adam_math(g, w_ref[l], m_ref[l], v_ref[l])

    return _pcall(
        body, name=name, out_shape=[jax.ShapeDtypeStruct(w.shape, F32)] * 4,
        compiler_params=pltpu.CompilerParams(vmem_limit_bytes=VMEM_LIMIT),
    )(*parts, w, m, v)

def _sum_sources(name, parts):
    r = parts.shape[1]

    def body(p_ref, g_ref):
        g = p_ref[0]
        for j in range(1, N_DEV):
            g = g + p_ref[j]
        g_ref[...] = g

    return _pcall(body, name=name, out_shape=jax.ShapeDtypeStruct((r, LANES), F32),
                  compiler_params=pltpu.CompilerParams(vmem_limit_bytes=VMEM_LIMIT))(parts)


def _adamw_small(name, g, w, m, v):
    n = len(g)

    def body(*refs):
        g_r, w_r, m_r, v_r = (refs[k * n:(k + 1) * n] for k in range(4))
        d_r, mo_r, vo_r = (refs[k * n:(k + 1) * n] for k in range(4, 7))
        for i in range(n):
            d_r[i][...], mo_r[i][...], vo_r[i][...] = _adam_math(g_r[i][...], w_r[i][...], m_r[i][...], v_r[i][...])

    res = _pcall(body, name=name, out_shape=[jax.ShapeDtypeStruct(a.shape, F32) for a in w] * 3,
                 compiler_params=pltpu.CompilerParams(vmem_limit_bytes=VMEM_LIMIT))(*g, *w, *m, *v)
    return res[:n], res[n:2 * n], res[2 * n:]


def _pack(arrs):
    flat = jnp.concatenate([a.reshape(-1) for a in arrs])
    flat = jnp.pad(flat, (0, (-flat.shape[0]) % (8 * LANES)))
    return flat.reshape(-1, LANES)


def _unpack(packed, shapes):
    flat = packed.reshape(-1)
    out, off = [], 0
    for shp in shapes:
        size = 1
        for d in shp:
            size *= d
        out.append(flat[off:off + size].reshape(shp))
        off += size
    return out


def _s5_params(lam_re, lam_im, log_dt, b_re, b_im, c_re, c_im):
    dt = jnp.exp(log_dt)[:, None]
    e = jnp.exp(lam_re * dt)
    ang = lam_im * dt
    a_re, a_im = e * jnp.cos(ang), e * jnp.sin(ang)
    nr, ni = a_re - 1.0, a_im
    den = lam_re * lam_re + lam_im * lam_im
    cr = ((nr * lam_re + ni * lam_im) / den)[..., None]
    ci = ((ni * lam_re - nr * lam_im) / den)[..., None]
    bb_re = cr * b_re - ci * b_im
    bb_im = cr * b_im + ci * b_re
    eye = jnp.eye(8, dtype=F32)[None, :, None, :, None]

    def bblk(bb):
        t = jnp.transpose(bb.reshape(4, 8, SSM_STATE, SSM_GROUP), (0, 3, 1, 2))
        return (eye * t[:, None]).reshape(4, 8 * SSM_GROUP, 8 * SSM_STATE)

    def cblk(cc):
        t = jnp.transpose(cc.reshape(4, 8, SSM_GROUP, SSM_STATE), (0, 3, 1, 2))
        return (eye * t[:, None]).reshape(4, 8 * SSM_STATE, 8 * SSM_GROUP)

    nb = SSM_GROUPS * SSM_STATE // LANES
    return (a_re.reshape(nb, 1, LANES), a_im.reshape(nb, 1, LANES), bblk(bb_re), bblk(bb_im),
            cblk(c_re), -cblk(c_im))


def _cat_blocks(x3, j):
    return jnp.concatenate([x3[4 * j + k] for k in range(4)], axis=-1)


def _to_chunks(a):
    s, c = a.shape
    return a.reshape(8, s // 8, c).transpose(1, 0, 2).reshape(s, c)


def _from_chunks(a):
    s, c = a.shape
    return a.reshape(s // 8, 8, c).transpose(1, 0, 2).reshape(s, c)


EARLY = ['w_in', 'w_uq', 'w_ukv']

FWD_PLAN = {
    'mla_qkv': ('nxt', ['w_down']),
    'flash': ('late', ['ssm_w_glu', 'w_out', 'w_xq', 'w_xkv', 'w_xo', 'w_gate', 'w_up']),
    's5_in': ('nxt', ['w_in', 'w_uq', 'w_ukv', 'ssm_w_glu', 'w_out']),
    's5_out': ('late', ['w_down']),
    'mix_out': ('nxt', ['w_xq']),
    'xattn': ('nxt', ['w_xkv', 'w_xo']),
    'ffn': ('nxt', ['w_gate', 'w_up']),
}
BWD_PLAN = {
    'ffn_bwd': ('nxt', EARLY),
    'xattn_bwd': ('own', ['w_down']),
    'flash_bwd': ('own', ['w_gate', 'w_up']),
    's5_out_bwd': ('own', ['w_xkv']),
    's5_in_bwd': ('own', ['w_xq']),
    'mla_qkv_bwd': ('own', ['w_xo']),
    'mix_in_bwd': ('own', ['ssm_w_glu', 'w_out']),
}


def _named(names, d):
    return [d[n] for n in names]


def _layer_fwd(h, memx, tabs, wl, pl_, late=None, nxt=None):
    s = h.shape[0]
    tm = min(256, s)
    cos, sin, pmat, pmat_t = tabs
    sv = {}
    wl = dict(wl)
    nxt_got = {}

    def fetch(host):
        who, names = FWD_PLAN[host]
        src = late if who == 'late' else nxt
        return _gather(_named(names, src)) if src else None

    def landed(host, got):
        who, names = FWD_PLAN[host]
        if got and who == 'late':
            wl.update(_layer_weights(dict(zip(names, got))))
        elif got:
            nxt_got.update(zip(names, got))

    def f_mix_in(h_, g, w):
        xn, _ = _rms(h_, g[...])
        return (_mm(xn, w[...]),)
    proj, = _rows("mix_in", f_mix_in, s, tm, [(h, 'r0'), (pl_['norm_mix_g'], 'f'), (wl['w_in'], 'f')],
                  [((s, D_MODEL), F32, 'r0')])

    def f_qkv(pr, cos_, sin_, gq, gkv, wq, wk, wv, pm):
        cqn = _rms(pr[:, 0:Q_LORA], gq[...])[0].astype(MXU)
        kvn = _rms(pr[:, Q_LORA:Q_LORA + KV_LORA], gkv[...])[0].astype(MXU)
        krr = _rope(pr[:, 384:512], cos_, sin_, pm)
        qs, ks, vs = [], [], []
        for hd in range(MLA_HEADS):
            qs.append(_rope(_mm(cqn, wq[hd]), cos_, sin_, pm) * MLA_SCALE)
            ks.append(_mm(kvn, wk[hd]) + krr)
            vs.append(_mm(kvn, wv[hd]))
        return jnp.stack(qs), jnp.stack(ks), jnp.stack(vs)
    hshape = (MLA_HEADS, s, HEAD_W)
    (q, k, v), got = _hosted(_rows(
        "mla_qkv", f_qkv, s, tm,
        [(proj, 'r0'), (cos, 'r0'), (sin, 'r0'), (pl_['q_norm_g'], 'f'), (pl_['kv_norm_g'], 'f'),
         (wl['w_uq'], 'f'), (wl['w_k'], 'f'), (wl['w_v'], 'f'), (pmat, 'f')],
        [(hshape, MXU, 'r1')] * 3, fetch('mla_qkv')))
    landed('mla_qkv', got)

    a_out, lse, got = _flash_fwd(q, k, v, fetch('flash'))
    landed('flash', got)

    u_ch = _to_chunks(proj[:, 512:1024])

    def f_s5_in(u, bre, bim):
        outs_r, outs_i = [], []
        for j in range(4):
            uj = _lanes(u, j, LANES)
            rr, ri = _mm_hi(uj, bre[j]), _mm_hi(uj, bim[j])
            outs_r += [_lanes(rr, kk, LANES) for kk in range(4)]
            outs_i += [_lanes(ri, kk, LANES) for kk in range(4)]
        return jnp.stack(outs_r), jnp.stack(outs_i)
    xshape = (16, s, LANES)
    (bu_re, bu_im), got = _hosted(_rows("s5_in", f_s5_in, s, tm,
                                        [(u_ch, 'r0'), (pl_['b_re'], 'f'), (pl_['b_im'], 'f')],
                                        [(xshape, F32, 'r1')] * 2, fetch('s5_in')))
    landed('s5_in', got)
    x_re, x_im = _scan(bu_re, bu_im, pl_['a_re'], pl_['a_im'], False)

    def f_s5_out(xr, xi, u, cre, cim, d, wglu, bglu):
        y = jnp.concatenate([_mm_hi(_cat_blocks(xr, j), cre[j]) + _mm_hi(_cat_blocks(xi, j), cim[j])
                             for j in range(4)], axis=-1) + d[...] * u
        z = _mm(jax.nn.gelu(y), wglu[...]) + bglu[...]
        return y, y * jax.nn.sigmoid(z)
    (y_ssm, s_out_ch), got = _hosted(_rows(
        "s5_out", f_s5_out, s, tm,
        [(x_re, 'r1'), (x_im, 'r1'), (u_ch, 'r0'), (pl_['c_re'], 'f'), (pl_['c_im'], 'f'),
         (pl_['ssm_d'], 'f'), (wl['ssm_w_glu'], 'f'), (pl_['ssm_b_glu'], 'f')],
        [((s, SSM_WIDTH), F32, 'r0')] * 2, fetch('s5_out')))
    landed('s5_out', got)
    s_out = _from_chunks(s_out_ch)

    def f_mix_out(h_, a, so, ga, gs, w):
        an = _rms(a, ga[...])[0]
        sn = _rms(so, gs[...])[0]
        return (h_ + _mm(jnp.concatenate([an, sn], axis=-1), w[...]),)
    (h1,), got = _hosted(_rows("mix_out", f_mix_out, s, tm,
                               [(h, 'r0'), (a_out, 'r0'), (s_out, 'r0'), (pl_['attn_out_g'], 'f'),
                                (pl_['ssm_out_g'], 'f'), (wl['w_out'], 'f')],
                               [((s, D_MODEL), F32, 'r0')], fetch('mix_out')))
    landed('mix_out', got)

    m_len = memx.shape[0]

    def f_memkv(mm_, g, w):
        mn = _rms(mm_, g[...])[0].astype(MXU)
        return (jnp.stack([_mm(mn, w[d]) for d in range(N_DEV)]),)
    kvm, = _rows("mem_kv", f_memkv, m_len, m_len, [(memx, 'r0'), (pl_['mem_norm_g'], 'f'), (wl['w_xkv'], 'f')],
                 [((N_DEV, m_len, X_HEAD_DIM), MXU, 'r1')])

    def f_xattn(h_, g, wq, kv_, wo):
        hn = _rms(h_, g[...])[0].astype(MXU)
        out = jnp.zeros(h_.shape, F32)
        for hd in range(X_HEADS):
            cs = pl.ds(hd * X_HEAD_DIM, X_HEAD_DIM)
            qh = _mm(hn, wq[:, cs])
            p = _softmax(_mm_nt(qh, kv_[hd]) * X_SCALE)
            out = out + _mm(_mm(p, kv_[X_HEADS + hd]), wo[cs, :])
        return (h_ + out,)
    (h2,), got = _hosted(_rows("xattn", f_xattn, s, min(X_ROWS, s),
                               [(h1, 'r0'), (pl_['norm_x_g'], 'f'), (wl['w_xq'], 'f'), (kvm, 'f'), (wl['w_xo'], 'f')],
                               [((s, D_MODEL), F32, 'r0')], fetch('xattn')))
    landed('xattn', got)

    def f_ffn(h_, g, wg, wu, wd):
        hn = _rms(h_, g[...])[0].astype(MXU)
        y = jnp.zeros(h_.shape, F32)
        for c in range(D_FF // FF_CHUNK):
            cs = pl.ds(c * FF_CHUNK, FF_CHUNK)
            gate = _mm(hn, wg[:, cs])
            y = y + _mm(gate * jax.nn.sigmoid(gate) * _mm(hn, wu[:, cs]), wd[cs, :])
        return (h_ + y,)
    (h3,), got = _hosted(_rows("ffn", f_ffn, s, tm,
                               [(h2, 'r0'), (pl_['norm_ffn_g'], 'f'), (wl['w_gate'], 'f'), (wl['w_up'], 'f'),
                                (wl['w_down'], 'f')],
                               [((s, D_MODEL), F32, 'r0')], fetch('ffn')))
    landed('ffn', got)
    sv.update(h=h, proj=proj, q=q, k=k, v=v, a_out=a_out, lse=lse, x_re=x_re, x_im=x_im, y_ssm=y_ssm,
              s_out=s_out, h1=h1, kvm=kvm, h2=h2, u_ch=u_ch)
    return h3, sv, wl, nxt_got


def _layer_bwd(dh3, sv, memx, tabs, wl, pl_, nxt=None):
    s = dh3.shape[0]
    tm = min(256, s)
    cos, sin, pmat, pmat_t = tabs
    gr = {}
    arrived = {}
    act_shape = (s, D_FF)

    def send(host):
        who, names = BWD_PLAN[host]
        if who == 'nxt' and not nxt:
            return None, []
        return (_scatter([nxt[n] if who == 'nxt' else _blocked(gr, n) for n in names]),
                [(who, n) for n in names])

    def f_ffn_bwd(h_, dy, g, wg, wu, wd):
        hn, r = _rms(h_, g[...])
        hb = hn.astype(MXU)
        dyb = dy.astype(MXU)
        dhn = jnp.zeros(h_.shape, F32)
        acts, dgs, dus = [], [], []
        for c in range(D_FF // FF_CHUNK):
            cs = pl.ds(c * FF_CHUNK, FF_CHUNK)
            gate, up = _mm(hb, wg[:, cs]), _mm(hb, wu[:, cs])
            sg = jax.nn.sigmoid(gate)
            si = gate * sg
            dact = _mm_nt(dyb, wd[cs, :])
            dgate = (dact * up * (sg * (1.0 + gate * (1.0 - sg)))).astype(MXU)
            dup = (dact * si).astype(MXU)
            dhn = dhn + _mm_nt(dgate, wg[:, cs]) + _mm_nt(dup, wu[:, cs])
            acts.append((si * up).astype(MXU))
            dgs.append(dgate)
            dus.append(dup)
        dh, dg = _rms_bwd(h_, g[...], r, dhn)
        cat = lambda parts: jnp.concatenate(parts, axis=-1)
        return dy + dh, hb, cat(acts), cat(dgs), cat(dus), dg
    ex, keys = send('ffn_bwd')
    (dh2, hn_f, act, dgate, dup, gr['norm_ffn_g']), got = _hosted(_rows(
        "ffn_bwd", f_ffn_bwd, s, tm,
        [(sv['h2'], 'r0'), (dh3, 'r0'), (pl_['norm_ffn_g'], 'f'), (wl['w_gate'], 'f'), (wl['w_up'], 'f'),
         (wl['w_down'], 'f')],
        [((s, D_MODEL), F32, 'r0'), ((s, D_MODEL), MXU, 'r0'), (act_shape, MXU, 'r0'), (act_shape, MXU, 'r0'),
         (act_shape, MXU, 'r0'), ((1, D_MODEL), F32, 'a')], ex))
    arrived.update(zip(keys, got))
    gr['w_gate'] = _mm_tn_call("dw_gate", hn_f, dgate, tn=FF_CHUNK)
    gr['w_up'] = _mm_tn_call("dw_up", hn_f, dup, tn=FF_CHUNK)
    gr['w_down'] = _mm_tn_call("dw_down", act, dh3, tk=FF_CHUNK)

    m_len = memx.shape[0]

    def f_xattn_bwd(h_, dy, g, wq, kv_, wo):
        hn, r = _rms(h_, g[...])
        hb = hn.astype(MXU)
        dyb = dy.astype(MXU)
        dhn = jnp.zeros(h_.shape, F32)
        dqs, ohs, dks, dvs = [], [], [], []
        for hd in range(X_HEADS):
            cs = pl.ds(hd * X_HEAD_DIM, X_HEAD_DIM)
            kh, vh = kv_[hd], kv_[X_HEADS + hd]
            qh = _mm(hb, wq[:, cs])
            p = _softmax(_mm_nt(qh, kh) * X_SCALE)
            ohs.append(_mm(p, vh).astype(MXU))
            do = _mm_nt(dyb, wo[cs, :])
            dvs.append(_mm_tn(p, do))
            dp = _mm_nt(do, vh)
            ds = p * (dp - jnp.sum(dp * p, axis=-1, keepdims=True)) * X_SCALE
            dq = _mm(ds, kh).astype(MXU)
            dks.append(_mm_tn(ds, qh))
            dhn = dhn + _mm_nt(dq, wq[:, cs])
            dqs.append(dq)
        dh, dg = _rms_bwd(h_, g[...], r, dhn)
        return (dy + dh, hb, jnp.concatenate(dqs, axis=-1), jnp.concatenate(ohs, axis=-1),
                jnp.stack(dks + dvs), dg)
    ex, keys = send('xattn_bwd')
    (dh1, hn_x, dq_x, oh_x, dkvm, gr['norm_x_g']), got = _hosted(_rows(
        "xattn_bwd", f_xattn_bwd, s, min(X_ROWS, s),
        [(sv['h1'], 'r0'), (dh2, 'r0'), (pl_['norm_x_g'], 'f'), (wl['w_xq'], 'f'), (sv['kvm'], 'f'),
         (wl['w_xo'], 'f')],
        [((s, D_MODEL), F32, 'r0'), ((s, D_MODEL), MXU, 'r0'), ((s, D_MODEL), MXU, 'r0'),
         ((s, D_MODEL), MXU, 'r0'), ((N_DEV, m_len, X_HEAD_DIM), F32, 'a'), ((1, D_MODEL), F32, 'a')], ex))
    arrived.update(zip(keys, got))
    gr['w_xq'] = _mm_tn_call("dw_xq", hn_x, dq_x)
    gr['w_xo'] = _mm_tn_call("dw_xo", oh_x, dh2)

    def f_memkv_bwd(mm_, dkv, g, w):
        mn, r = _rms(mm_, g[...])
        mb = mn.astype(MXU)
        dmn = jnp.zeros(mm_.shape, F32)
        dws = []
        for d in range(N_DEV):
            dmn = dmn + _mm_nt(dkv[d], w[d])
            dws.append(_mm_tn(mb, dkv[d]))
        _, dg = _rms_bwd(mm_, g[...], r, dmn)
        return jnp.stack(dws), dg
    gr['w_xkv'], gr['mem_norm_g'] = _rows(
        "mem_kv_bwd", f_memkv_bwd, m_len, m_len,
        [(memx, 'r0'), (dkvm, 'r1'), (pl_['mem_norm_g'], 'f'), (wl['w_xkv'], 'f')],
        [((N_DEV, D_MODEL, X_HEAD_DIM), F32, 'a'), ((1, D_MODEL), F32, 'a')])

    def f_mix_out_bwd(a, so, dy, ga, gs, w):
        dmix = _mm_nt(dy, w[...])
        an, ra = _rms(a, ga[...])
        sn, rs = _rms(so, gs[...])
        da, dga = _rms_bwd(a, ga[...], ra, dmix[:, 0:512])
        dso, dgs = _rms_bwd(so, gs[...], rs, dmix[:, 512:1024])
        return da, dso, jnp.concatenate([an, sn], axis=-1), dga, dgs
    da_out, ds_out, mixed, gr['attn_out_g'], gr['ssm_out_g'] = _rows(
        "mix_out_bwd", f_mix_out_bwd, s, tm,
        [(sv['a_out'], 'r0'), (sv['s_out'], 'r0'), (dh1, 'r0'), (pl_['attn_out_g'], 'f'), (pl_['ssm_out_g'], 'f'),
         (wl['w_out'], 'f')],
        [((s, 512), F32, 'r0'), ((s, 512), F32, 'r0'), ((s, D_MODEL), MXU, 'r0'), ((1, 512), F32, 'a'),
         ((1, 512), F32, 'a')])
    gr['w_out'] = _mm_tn_call("dw_out", mixed, dh1)

    ex, keys = send('flash_bwd')
    dq, dk, dv, got = _flash_bwd(sv['q'], sv['k'], sv['v'], sv['a_out'], sv['lse'], da_out, ex)
    arrived.update(zip(keys, got))

    def f_s5_out_bwd(xr, xi, u, y, ds, cre, cim, d, wglu, bglu):
        g, gelu_vjp = jax.vjp(jax.nn.gelu, y)
        sig = jax.nn.sigmoid(_mm(g, wglu[...]) + bglu[...])
        dz = ds * y * sig * (1.0 - sig)
        dy = ds * sig + gelu_vjp(_mm_nt(dz, wglu[...]))[0]
        dxr, dxi, dcr, dci = [], [], [], []
        for j in range(4):
            dyj = _lanes(dy, j, LANES)
            tr_, ti_ = _mm_nt(dyj, cre[j]), _mm_nt(dyj, cim[j])
            dxr += [_lanes(tr_, kk, LANES) for kk in range(4)]
            dxi += [_lanes(ti_, kk, LANES) for kk in range(4)]
            dcr.append(_mm_tn(_cat_blocks(xr, j), dyj))
            dci.append(_mm_tn(_cat_blocks(xi, j), dyj))
        return (jnp.stack(dxr), jnp.stack(dxi), dy * d[...], jnp.stack(dcr), jnp.stack(dci),
                jnp.sum(dy * u, axis=0, keepdims=True), _mm_tn(g, dz), jnp.sum(dz, axis=0, keepdims=True))
    xshape = (16, s, LANES)
    ex, keys = send('s5_out_bwd')
    (dx_re, dx_im, du_dir, gr['c_re'], gr['c_im'], gr['ssm_d'], gr['ssm_w_glu'], gr['ssm_b_glu']), got = _hosted(_rows(
        "s5_out_bwd", f_s5_out_bwd, s, tm,
        [(sv['x_re'], 'r1'), (sv['x_im'], 'r1'), (sv['u_ch'], 'r0'), (sv['y_ssm'], 'r0'), (_to_chunks(ds_out), 'r0'),
         (pl_['c_re'], 'f'), (pl_['c_im'], 'f'), (pl_['ssm_d'], 'f'), (wl['ssm_w_glu'], 'f'),
         (pl_['ssm_b_glu'], 'f')],
        [(xshape, F32, 'r1'), (xshape, F32, 'r1'), ((s, 512), F32, 'r0'), ((4, 512, LANES), F32, 'a'),
         ((4, 512, LANES), F32, 'a'), ((1, 512), F32, 'a'), ((512, 512), F32, 'a'), ((1, 512), F32, 'a')], ex))
    arrived.update(zip(keys, got))
    g_re, g_im = _scan(dx_re, dx_im, pl_['a_re'], -pl_['a_im'], True)
    first_re = jnp.pad(sv['x_re'][:, s - 8:s - 1], ((0, 0), (1, 0), (0, 0)))
    first_im = jnp.pad(sv['x_im'][:, s - 8:s - 1], ((0, 0), (1, 0), (0, 0)))

    def f_s5_in_bwd(gre, gim, xr, xi, pr8, pi8, u, dud, f8r, f8i, bre, bim):
        first = pl.program_id(0) == 0
        xpr = jnp.concatenate([jnp.where(first, f8r[...], pr8), xr[:, :tm - 8]], axis=1)
        xpi = jnp.concatenate([jnp.where(first, f8i[...], pi8), xi[:, :tm - 8]], axis=1)
        dus, dbr, dbi = [], [], []
        for j in range(4):
            gj_r, gj_i, uj = _cat_blocks(gre, j), _cat_blocks(gim, j), _lanes(u, j, LANES)
            dus.append(_mm_nt(gj_r, bre[j]) + _mm_nt(gj_i, bim[j]))
            dbr.append(_mm_tn(uj, gj_r))
            dbi.append(_mm_tn(uj, gj_i))
        da_r = jnp.sum(gre * xpr + gim * xpi, axis=1, keepdims=True)
        da_i = jnp.sum(gim * xpr - gre * xpi, axis=1, keepdims=True)
        return dud + jnp.concatenate(dus, axis=-1), jnp.stack(dbr), jnp.stack(dbi), da_r, da_i
    ex, keys = send('s5_in_bwd')
    (du_ch, gr['b_re'], gr['b_im'], gr['a_re'], gr['a_im']), got = _hosted(_rows(
        "s5_in_bwd", f_s5_in_bwd, s, tm,
        [(g_re, 'r1'), (g_im, 'r1'), (sv['x_re'], 'r1'), (sv['x_im'], 'r1'), (sv['x_re'], 'p8'), (sv['x_im'], 'p8'),
         (sv['u_ch'], 'r0'), (du_dir, 'r0'), (first_re, 'f'), (first_im, 'f'), (pl_['b_re'], 'f'), (pl_['b_im'], 'f')],
        [((s, 512), F32, 'r0'), ((4, LANES, 512), F32, 'a'), ((4, LANES, 512), F32, 'a'),
         ((16, 1, LANES), F32, 'a'), ((16, 1, LANES), F32, 'a')], ex))
    arrived.update(zip(keys, got))
    du = _from_chunks(du_ch)

    def f_qkv_bwd(pr, cos_, sin_, dq_, dk_, dv_, gq, gkv, wq, wk, wv, pt):
        cq, ckv = pr[:, 0:Q_LORA], pr[:, Q_LORA:Q_LORA + KV_LORA]
        cqn, rq = _rms(cq, gq[...])
        kvn, rkv = _rms(ckv, gkv[...])
        cqb, kvb = cqn.astype(MXU), kvn.astype(MXU)
        dcqn = jnp.zeros(cq.shape, F32)
        dkvn = jnp.zeros(ckv.shape, F32)
        dksum = jnp.zeros(dk_[0].shape, F32)
        dwq, dwk, dwv = [], [], []
        for hd in range(MLA_HEADS):
            dqp = (_rope_t(dq_[hd], cos_, sin_, pt) * MLA_SCALE).astype(MXU)
            dkb, dvb = dk_[hd].astype(MXU), dv_[hd].astype(MXU)
            dwq.append(_mm_tn(cqb, dqp))
            dwk.append(_mm_tn(kvb, dkb))
            dwv.append(_mm_tn(kvb, dvb))
            dcqn = dcqn + _mm_nt(dqp, wq[hd])
            dkvn = dkvn + _mm_nt(dkb, wk[hd]) + _mm_nt(dvb, wv[hd])
            dksum = dksum + dk_[hd]
        dcq, dgq = _rms_bwd(cq, gq[...], rq, dcqn)
        dckv, dgkv = _rms_bwd(ckv, gkv[...], rkv, dkvn)
        dpa = jnp.concatenate([dcq, dckv, _rope_t(dksum, cos_, sin_, pt)], axis=-1)
        return dpa, jnp.stack(dwq), jnp.stack(dwk), jnp.stack(dwv), dgq, dgkv
    ex, keys = send('mla_qkv_bwd')
    (dpa, gr['w_uq'], gr['w_k'], gr['w_v'], gr['q_norm_g'], gr['kv_norm_g']), got = _hosted(_rows(
        "mla_qkv_bwd", f_qkv_bwd, s, tm,
        [(sv['proj'], 'r0'), (cos, 'r0'), (sin, 'r0'), (dq, 'r1'), (dk, 'r1'), (dv, 'r1'), (pl_['q_norm_g'], 'f'),
         (pl_['kv_norm_g'], 'f'), (wl['w_uq'], 'f'), (wl['w_k'], 'f'), (wl['w_v'], 'f'), (pmat_t, 'f')],
        [((s, 512), F32, 'r0'), ((MLA_HEADS, Q_LORA, HEAD_W), F32, 'a'), ((MLA_HEADS, KV_LORA, HEAD_W), F32, 'a'),
         ((MLA_HEADS, KV_LORA, HEAD_W), F32, 'a'), ((1, Q_LORA), F32, 'a'), ((1, KV_LORA), F32, 'a')], ex))
    arrived.update(zip(keys, got))

    def f_mix_in_bwd(h_, dpa_, du_, dres, g, w):
        dproj = jnp.concatenate([dpa_, du_], axis=-1).astype(MXU)
        xn, r = _rms(h_, g[...])
        dh, dg = _rms_bwd(h_, g[...], r, _mm_nt(dproj, w[...]))
        return dres + dh, xn, dproj, dg
    ex, keys = send('mix_in_bwd')
    (dh0, xn, dproj, gr['norm_mix_g']), got = _hosted(_rows(
        "mix_in_bwd", f_mix_in_bwd, s, tm,
        [(sv['h'], 'r0'), (dpa, 'r0'), (du, 'r0'), (dh1, 'r0'), (pl_['norm_mix_g'], 'f'), (wl['w_in'], 'f')],
        [((s, D_MODEL), F32, 'r0'), ((s, D_MODEL), MXU, 'r0'), ((s, D_MODEL), MXU, 'r0'), ((1, D_MODEL), F32, 'a')],
        ex))
    arrived.update(zip(keys, got))
    gr['w_in'] = _mm_tn_call("dw_in", xn, dproj)
    return dh0, gr, arrived


def _layer_weights(w):
    wl = {}
    if 'w_in' in w:
        w_in = w['w_in'].reshape(D_MODEL, -1)
        z = lambda n: jnp.zeros((D_MODEL, n), w_in.dtype)
        wl['w_in'] = jnp.concatenate([w_in[:, :384], z(64), w_in[:, 384:416], z(32), w_in[:, 416:]], axis=1)
    if 'w_uq' in w:
        wl['w_uq'] = jnp.pad(w['w_uq'], ((0, 0), (0, 0), (0, HEAD_W - QK_NOPE - QK_ROPE)))
    if 'w_ukv' in w:
        wl['w_k'] = jnp.pad(w['w_ukv'][..., :QK_NOPE], ((0, 0), (0, 0), (0, HEAD_W - QK_NOPE)))
        wv = w['w_ukv'][..., QK_NOPE:]
        even = (jnp.arange(MLA_HEADS) % 2 == 0)[:, None, None]
        wl['w_v'] = jnp.concatenate([jnp.where(even, wv, 0), jnp.where(even, 0, wv)], axis=-1).astype(wv.dtype)
    if 'ssm_w_glu' in w:
        wl['ssm_w_glu'] = w['ssm_w_glu'].reshape(SSM_WIDTH, SSM_WIDTH)
    for n in ('w_out', 'w_xq', 'w_xo'):
        if n in w:
            wl[n] = w[n].reshape(D_MODEL, D_MODEL)
    if 'w_xkv' in w:
        wl['w_xkv'] = w['w_xkv']
    for n in ('w_gate', 'w_up'):
        if n in w:
            wl[n] = jnp.transpose(w[n], (1, 0, 2)).reshape(D_MODEL, D_FF)
    if 'w_down' in w:
        wl['w_down'] = w['w_down'].reshape(D_FF, D_MODEL)
    return wl


def _blocked(gr, n):
    if n == 'w_in':
        d = gr['w_in']
        out = jnp.concatenate([d[:, :384], d[:, 448:480], d[:, 512:]], axis=1).reshape(N_DEV, 128, -1)
    elif n == 'w_uq':
        out = gr['w_uq'][..., :QK_NOPE + QK_ROPE]
    elif n == 'w_ukv':
        even = (jnp.arange(MLA_HEADS) % 2 == 0)[:, None, None]
        dv = gr['w_v']
        out = jnp.concatenate([gr['w_k'][..., :QK_NOPE], jnp.where(even, dv[..., :V_DIM], dv[..., V_DIM:])], axis=-1)
    elif n == 'ssm_w_glu':
        out = gr['ssm_w_glu'].reshape(N_DEV, SSM_WIDTH // N_DEV, SSM_WIDTH)
    elif n in ('w_out', 'w_xq', 'w_xo'):
        out = gr[n].reshape(N_DEV, D_MODEL // N_DEV, D_MODEL)
    elif n in ('w_gate', 'w_up'):
        out = jnp.transpose(gr[n].reshape(D_MODEL, N_DEV, D_FF // N_DEV), (1, 0, 2))
    elif n == 'w_down':
        out = gr[n].reshape(N_DEV, D_FF // N_DEV, D_MODEL)
    else:
        out = gr[n]
    return out.astype(MXU)


def kernel(x, mem, positions, norm_mix_g, w_in, q_norm_g, w_uq, kv_norm_g, w_ukv, ssm_lambda_re, ssm_lambda_im, ssm_log_dt, ssm_b_re, ssm_b_im, ssm_c_re, ssm_c_im, ssm_d, ssm_w_glu, ssm_b_glu, attn_out_g, ssm_out_g, w_out, norm_x_g, mem_norm_g, w_xq, w_xkv, w_xo, norm_ffn_g, w_gate, w_up, w_down, final_norm_g, loss_target, m_norm_mix_g, m_w_in, m_q_norm_g, m_w_uq, m_kv_norm_g, m_w_ukv, m_ssm_lambda_re, m_ssm_lambda_im, m_ssm_log_dt, m_ssm_b_re, m_ssm_b_im, m_ssm_c_re, m_ssm_c_im, m_ssm_d, m_ssm_w_glu, m_ssm_b_glu, m_attn_out_g, m_ssm_out_g, m_w_out, m_norm_x_g, m_mem_norm_g, m_w_xq, m_w_xkv, m_w_xo, m_norm_ffn_g, m_w_gate, m_w_up, m_w_down, m_final_norm_g, v_norm_mix_g, v_w_in, v_q_norm_g, v_w_uq, v_kv_norm_g, v_w_ukv, v_ssm_lambda_re, v_ssm_lambda_im, v_ssm_log_dt, v_ssm_b_re, v_ssm_b_im, v_ssm_c_re, v_ssm_c_im, v_ssm_d, v_ssm_w_glu, v_ssm_b_glu, v_attn_out_g, v_ssm_out_g, v_w_out, v_norm_x_g, v_mem_norm_g, v_w_xq, v_w_xkv, v_w_xo, v_norm_ffn_g, v_w_gate, v_w_up, v_w_down, v_final_norm_g):
    args = dict(locals())
    W = {n: args[n] for n in WEIGHTS}
    M = {n: args['m_' + n] for n in WEIGHTS}
    V = {n: args['v_' + n] for n in WEIGHTS}
    s = x.shape[1]
    h = x[0]
    memx = mem[0]

    freqs = ROPE_THETA ** (-jnp.arange(0, QK_ROPE, 2, dtype=F32) / QK_ROPE)
    ang = positions[0].astype(F32)[:, None] * freqs
    c16, s16 = jnp.cos(ang), jnp.sin(ang)
    cos = jnp.concatenate([jnp.ones((s, QK_NOPE), F32), c16, c16, jnp.zeros((s, 32), F32)], axis=1)
    sin = jnp.concatenate([jnp.zeros((s, QK_NOPE), F32), s16, s16, jnp.zeros((s, 32), F32)], axis=1)
    idx = jnp.arange(QK_ROPE // 2)
    pmat = jnp.zeros((HEAD_W, HEAD_W), F32)
    pmat = pmat.at[QK_NOPE + 16 + idx, QK_NOPE + idx].set(-1.0).at[QK_NOPE + idx, QK_NOPE + 16 + idx].set(1.0)
    tabs = (cos, sin, pmat, pmat.T)

    shards = [{n: W[n][l].astype(MXU) for n in SHARDED} for l in range(DEPTH)]
    gathered = dict(zip(EARLY, _run_exchange("gather_weights", _gather(_named(EARLY, shards[0])))))

    layers = []
    for l in range(DEPTH):
        wl = _layer_weights(gathered)
        s5_in = [W[n][l] for n in ('ssm_lambda_re', 'ssm_lambda_im', 'ssm_log_dt', 'ssm_b_re', 'ssm_b_im',
                                   'ssm_c_re', 'ssm_c_im')]
        (a_re, a_im, bre, bim, cre, cim), s5_vjp = jax.vjp(_s5_params, *s5_in)
        pl_ = {n: W[n][l][None] for n in ('norm_mix_g', 'q_norm_g', 'kv_norm_g', 'ssm_d', 'ssm_b_glu',
                                           'attn_out_g', 'ssm_out_g', 'norm_x_g', 'mem_norm_g', 'norm_ffn_g')}
        pl_.update(a_re=a_re, a_im=a_im, b_re=bre, b_im=bim, c_re=cre, c_im=cim)
        h, sv, wl, gathered = _layer_fwd(h, memx, tabs, wl, pl_, shards[0] if l == 0 else None,
                                         shards[l + 1] if l + 1 < DEPTH else None)
        layers.append((wl, pl_, sv, s5_vjp))

    def f_loss(h_, tgt, g):
        y, r = _rms(h_, g[...])
        err = y - tgt
        part = 0.5 * jnp.sum(jnp.mean(err * err, axis=-1, keepdims=True), axis=0, keepdims=True)
        dh, dg = _rms_bwd(h_, g[...], r, err / D_MODEL)
        return dh, dg, jnp.broadcast_to(part, (8, LANES))
    dh, g_final, loss_part = _rows(
        "loss_head", f_loss, s, min(256, s), [(h, 'r0'), (loss_target[0], 'r0'), (final_norm_g[None], 'f')],
        [((s, D_MODEL), F32, 'r0'), ((1, D_MODEL), F32, 'a'), ((8, LANES), F32, 'a')])
    loss = lax.psum(loss_part[0, 0], ("x", "y", "c"))

    parts = [{} for _ in range(DEPTH)]
    g_rep = [None] * DEPTH
    blocks = None
    for l in reversed(range(DEPTH)):
        wl, pl_, sv, s5_vjp = layers[l]
        dh, gr, arrived = _layer_bwd(dh, sv, memx, tabs, wl, pl_, blocks)
        for (who, n), p in arrived.items():
            parts[l + 1 if who == 'nxt' else l][n] = p
        blocks = {n: _blocked(gr, n) for n in EARLY}
        ds5 = s5_vjp((gr['a_re'], gr['a_im'], gr['b_re'], gr['b_im'], gr['c_re'], gr['c_im']))
        rep = dict(zip(('ssm_lambda_re', 'ssm_lambda_im', 'ssm_log_dt', 'ssm_b_re', 'ssm_b_im', 'ssm_c_re',
                        'ssm_c_im'), ds5))
        for n in ('norm_mix_g', 'q_norm_g', 'kv_norm_g', 'ssm_d', 'ssm_b_glu', 'attn_out_g', 'ssm_out_g',
                  'norm_x_g', 'mem_norm_g', 'norm_ffn_g'):
            rep[n] = gr[n][0]
        g_rep[l] = rep
    grad_x = dh[None]

    rep_names = REPL_L + ['final_norm_g']
    g_loc = {n: jnp.stack([g_rep[l][n] for l in range(DEPTH)]) for n in REPL_L}
    g_loc['final_norm_g'] = g_final
    rest = [n for n in SHARDED if n not in parts[0]]
    last = _run_exchange("last_grads", _together(_scatter(_named(rest, blocks)),
                                                 _gather([_pack(_named(rep_names, g_loc))])))
    parts[0].update(zip(rest, last[:len(rest)]))

    out_sh = [{}, {}, {}, {}]
    for n in SHARDED:
        res = _adamw_weight("adamw_" + n, [parts[l][n] for l in range(DEPTH)], W[n], M[n], V[n])
        for kind, r in enumerate(res):
            out_sh[kind][n] = r

    shapes_rp = [(1,) + W[n].shape if W[n].ndim == 1 else W[n].shape for n in rep_names]
    g_rp = _unpack(_sum_sources("sum_small_grads", last[len(rest)]), shapes_rp)
    as_rows = lambda d: [d[n].reshape(shp) for n, shp in zip(rep_names, shapes_rp)]
    res_rp = (g_rp,) + _adamw_small("adamw_replicated", g_rp, as_rows(W), as_rows(M), as_rows(V))
    out_rp = [{n: a.reshape(W[n].shape) for n, a in zip(rep_names, r)} for r in res_rp]

    outs = [loss, grad_x]
    for kind in range(4):
        for n in WEIGHTS:
            outs.append(out_sh[kind][n] if n in SHARDED else out_rp[kind][n])
    return tuple(outs)
```

```python
from typing import Callable, NamedTuple

import jax
import jax.numpy as jnp
from jax import lax
from jax.experimental import pallas as pl
from jax.experimental.pallas import tpu as pltpu

F32 = jnp.float32
MXU = jnp.bfloat16
HI = lax.Precision.HIGHEST

D_MODEL = 1024
MLA_HEADS = 8
QK_NOPE = 64
QK_ROPE = 32
V_DIM = 64
Q_LORA = 256
KV_LORA = 128
SSM_WIDTH = 512
SSM_GROUPS = 32
SSM_GROUP = 16
SSM_STATE = 64
X_HEADS = 4
X_HEAD_DIM = 256
D_FF = 2816
FF_CHUNK = D_FF // 2
ROPE_THETA = 10000.0
EPS = 1e-6
DEPTH = 2
N_DEV = 8
LANES = 128
HEAD_W = 128
MLA_SCALE = (QK_NOPE + QK_ROPE) ** -0.5
X_SCALE = X_HEAD_DIM ** -0.5
ADAM_LR, ADAM_B1, ADAM_B2, ADAM_EPS, ADAM_WD, ADAM_STEP = 0.001, 0.9, 0.999, 1e-08, 0.01, 10
VMEM_LIMIT = 56 * 1024 * 1024
FLASH_TILE = 512
DW_ROWS = 2048
X_ROWS = 512
MESH = pl.DeviceIdType.MESH

SHARDED = ['w_in', 'w_uq', 'w_ukv', 'ssm_w_glu', 'w_out', 'w_xq', 'w_xkv', 'w_xo', 'w_gate', 'w_up', 'w_down']
REPL_L = ['norm_mix_g', 'q_norm_g', 'kv_norm_g', 'ssm_lambda_re', 'ssm_lambda_im', 'ssm_log_dt', 'ssm_b_re',
          'ssm_b_im', 'ssm_c_re', 'ssm_c_im', 'ssm_d', 'ssm_b_glu', 'attn_out_g', 'ssm_out_g', 'norm_x_g',
          'mem_norm_g', 'norm_ffn_g']
WEIGHTS = ['norm_mix_g', 'w_in', 'q_norm_g', 'w_uq', 'kv_norm_g', 'w_ukv', 'ssm_lambda_re', 'ssm_lambda_im',
           'ssm_log_dt', 'ssm_b_re', 'ssm_b_im', 'ssm_c_re', 'ssm_c_im', 'ssm_d', 'ssm_w_glu', 'ssm_b_glu',
           'attn_out_g', 'ssm_out_g', 'w_out', 'norm_x_g', 'mem_norm_g', 'w_xq', 'w_xkv', 'w_xo', 'norm_ffn_g',
           'w_gate', 'w_up', 'w_down', 'final_norm_g']


def _pcall(body, **kw):
    return pl.pallas_call(body, **kw)


def _mm(a, b):
    return jnp.dot(a.astype(MXU), b.astype(MXU), preferred_element_type=F32)


def _mm_nt(a, b):
    return lax.dot_general(a.astype(MXU), b.astype(MXU), (((1,), (1,)), ((), ())), preferred_element_type=F32)


def _mm_tn(a, b):
    return lax.dot_general(a.astype(MXU), b.astype(MXU), (((0,), (0,)), ((), ())), preferred_element_type=F32)


def _mm_hi(a, b):
    return jnp.dot(a.astype(F32), b.astype(F32), precision=HI, preferred_element_type=F32)


def _rms(x, g):
    r = lax.rsqrt(jnp.mean(x * x, axis=-1, keepdims=True) + EPS)
    return x * r * g, r


def _rms_bwd(x, g, r, dy):
    dyg = dy * g
    dx = r * dyg - x * (r * r * r) * jnp.mean(dyg * x, axis=-1, keepdims=True)
    return dx, jnp.sum(dy * x * r, axis=0, keepdims=True)


def _rope(x, cos, sin, p_ref):
    return x * cos + _mm_hi(x, p_ref[...]) * sin


def _rope_t(g, cos, sin, pt_ref):
    return g * cos + _mm_hi(g * sin, pt_ref[...])


def _softmax(s):
    m = jnp.max(s, axis=-1, keepdims=True)
    e = jnp.exp(s - m)
    return e / jnp.sum(e, axis=-1, keepdims=True)


def _lanes(x, j, w):
    return x[:, j * w:(j + 1) * w]


def _rows(name, fn, n, tm, ins, outs, side=None):
    def spec(shape, kind):
        nd = len(shape)
        if kind == 'p8':
            return pl.BlockSpec((shape[0], 8, shape[2]), lambda i: (0, jnp.maximum(i * (tm // 8) - 1, 0), 0))
        if kind == 'f':
            return pl.BlockSpec(shape, lambda i, _nd=nd: (0,) * _nd, pipeline_mode=pl.Buffered(1))
        if kind == 'a':
            return pl.BlockSpec(shape, lambda i, _nd=nd: (0,) * _nd)
        ax = int(kind[1])
        blk = tuple(tm if d == ax else s for d, s in enumerate(shape))
        return pl.BlockSpec(blk, lambda i, _ax=ax, _nd=nd: tuple(i if d == _ax else 0 for d in range(_nd)))

    n_in, n_out, n_steps = len(ins), len(outs), n // tm

    def body(*refs):
        in_refs, out_refs, steps = _side_split(refs, n_in, n_out, side)
        i = pl.program_id(0)
        if steps:
            pl.when(i == 0)(steps[0])
            pl.when(i == _pass_on_step(n_steps))(steps[1])
        args = [r if k == 'f' else r[...] for r, (_, k) in zip(in_refs, ins)]
        res = fn(*args)
        for r, (_, dt, k), v in zip(out_refs, outs, res):
            if k == 'a':
                _accumulate(r, v.astype(dt), i)
            else:
                r[...] = v.astype(dt)
        if steps:
            pl.when(i == n_steps - 1)(steps[2])

    s_in, s_out, s_shape, s_sems, s_ops = _side_args(side)
    res = _pcall(
        body, name=name + ("_x" if side else ""), grid=(n_steps,),
        in_specs=[spec(a.shape, k) for a, k in ins] + s_in,
        out_specs=[spec(s, k) for s, _, k in outs] + s_out,
        out_shape=[jax.ShapeDtypeStruct(s, dt) for s, dt, _ in outs] + s_shape,
        scratch_shapes=s_sems,
        compiler_params=pltpu.CompilerParams(dimension_semantics=("arbitrary",), vmem_limit_bytes=VMEM_LIMIT),
    )(*[a for a, _ in ins], *s_ops)
    return _Hosted(res[:n_out], res[n_out:]) if side else res


def _accumulate(ref, v, i):
    @pl.when(i == 0)
    def _():
        ref[...] = v

    @pl.when(i != 0)
    def _():
        ref[...] += v


def _mm_tn_call(name, a, b, tk=None, tn=None):
    out_dtype = MXU
    s, k = a.shape
    n = b.shape[1]
    tk, tn = tk or k, tn or n
    ts = min(DW_ROWS, s)
    ns = s // ts

    def body(a_ref, b_ref, o_ref, acc_ref):
        j = pl.program_id(2)
        _accumulate(acc_ref, _mm_tn(a_ref[...], b_ref[...]), j)

        @pl.when(j == ns - 1)
        def _():
            o_ref[...] = acc_ref[...].astype(out_dtype)

    return _pcall(
        body, name=name, grid=(k // tk, n // tn, ns),
        in_specs=[pl.BlockSpec((ts, tk), lambda ik, jn, j: (j, ik)),
                  pl.BlockSpec((ts, tn), lambda ik, jn, j: (j, jn))],
        out_specs=pl.BlockSpec((tk, tn), lambda ik, jn, j: (ik, jn)),
        out_shape=jax.ShapeDtypeStruct((k, n), out_dtype),
        scratch_shapes=[pltpu.VMEM((tk, tn), F32)],
        compiler_params=pltpu.CompilerParams(dimension_semantics=("arbitrary", "arbitrary", "arbitrary"),
                                             vmem_limit_bytes=VMEM_LIMIT),
    )(a, b)


def _side_split(refs, n_in, n_out, side):
    if side is None:
        return refs[:n_in], refs[n_in:n_in + n_out], None
    si, so = len(side.ins), len(side.out_shapes)
    own_in, side_in = refs[:n_in], refs[n_in:n_in + si]
    own_out, side_out = refs[n_in + si:n_in + si + n_out], refs[n_in + si + n_out:n_in + si + n_out + so]
    return own_in, own_out, side.steps(side_in, side_out, refs[n_in + si + n_out + so:])


def _pass_on_step(n_steps):
    return max(n_steps - 2, 0)


def _side_args(side):
    if side is None:
        return [], [], [], [], []
    any_spec = pl.BlockSpec(memory_space=pl.ANY)
    return ([any_spec] * len(side.ins), [any_spec] * len(side.out_shapes), list(side.out_shapes),
            list(side.sem_shapes), list(side.ins))


class _Hosted(NamedTuple):
    results: list
    arrived: list


def _hosted(res):
    return res if isinstance(res, _Hosted) else _Hosted(res, ())


def _flash_fwd(q, k, v, side=None):
    nh, s, w = q.shape
    t = min(FLASH_TILE, s)
    nq = s // t
    n_steps = (nh // 2) * nq

    def body(*refs):
        (q_ref, k_ref, v_ref), (o_ref, lse_ref), steps = _side_split(refs, 3, 2, side)
        step = pl.program_id(0) * nq + pl.program_id(1)
        if steps:
            pl.when(step == 0)(steps[0])
            pl.when(step == _pass_on_step(n_steps))(steps[1])
        qi = pl.program_id(1)
        qs = [q_ref[0], q_ref[1]]
        below = lax.broadcasted_iota(jnp.int32, (t, t), 1) <= lax.broadcasted_iota(jnp.int32, (t, t), 0)

        def tile(j, carry, diagonal):
            sl = pl.ds(pl.multiple_of(j * t, t), t)
            out = []
            for hh in range(2):
                m, l, acc = carry[3 * hh:3 * hh + 3]
                sc = _mm_nt(qs[hh], k_ref[hh, sl, :])
                if diagonal:
                    sc = jnp.where(below, sc, -1e30)
                m_new = jnp.maximum(m, jnp.max(sc, axis=1, keepdims=True))
                p = jnp.exp(sc - m_new)
                alpha = jnp.exp(m - m_new)
                out += [m_new, alpha * l + jnp.sum(p, axis=1, keepdims=True), alpha * acc + _mm(p, v_ref[hh, sl, :])]
            return tuple(out)

        init = (jnp.full((t, 1), -1e30, F32), jnp.zeros((t, 1), F32), jnp.zeros((t, w), F32)) * 2
        carry = lax.fori_loop(0, qi, lambda j, c: tile(j, c, False), init)
        carry = tile(qi, carry, True)
        o_ref[...] = carry[2] / carry[1] + carry[5] / carry[4]
        for hh in range(2):
            lse_ref[hh] = jnp.broadcast_to(carry[3 * hh] + jnp.log(carry[3 * hh + 1]), (t, w))
        if steps:
            pl.when(step == n_steps - 1)(steps[2])

    s_in, s_out, s_shape, s_sems, s_ops = _side_args(side)
    res = _pcall(
        body, name="mla_flash_fwd" + ("_x" if side else ""), grid=(nh // 2, nq),
        in_specs=[pl.BlockSpec((2, t, w), lambda p, i: (p, i, 0)),
                  pl.BlockSpec((2, s, w), lambda p, i: (p, 0, 0)),
                  pl.BlockSpec((2, s, w), lambda p, i: (p, 0, 0))] + s_in,
        out_specs=[pl.BlockSpec((t, w), lambda p, i: (i, p)),
                   pl.BlockSpec((2, t, w), lambda p, i: (p, i, 0))] + s_out,
        out_shape=[jax.ShapeDtypeStruct((s, (nh // 2) * w), F32), jax.ShapeDtypeStruct((nh, s, w), F32)] + s_shape,
        scratch_shapes=s_sems,
        compiler_params=pltpu.CompilerParams(dimension_semantics=("arbitrary", "arbitrary"),
                                             vmem_limit_bytes=VMEM_LIMIT),
    )(q, k, v, *s_ops)
    return res[0], res[1], res[2:]


def _flash_bwd(q, k, v, o, lse, do, side=None):
    nh, s, w = q.shape
    t = min(FLASH_TILE, s)
    nq = s // t
    n_steps = (nh // 2) * nq

    def body(*refs):
        (q_ref, k_ref, v_ref, o_ref, lse_ref, do_ref), (dq_ref, dk_ref, dv_ref), steps = _side_split(refs, 6, 3, side)
        step = pl.program_id(0) * nq + pl.program_id(1)
        if steps:
            pl.when(step == 0)(steps[0])
            pl.when(step == _pass_on_step(n_steps))(steps[1])
        j = pl.program_id(1)

        @pl.when(j == 0)
        def _():
            dq_ref[...] = jnp.zeros(dq_ref.shape, F32)

        below = lax.broadcasted_iota(jnp.int32, (t, t), 1) <= lax.broadcasted_iota(jnp.int32, (t, t), 0)
        lane = lax.broadcasted_iota(jnp.int32, (t, w), 1)
        heads = [jnp.logical_and(lane >= hh * V_DIM, lane < (hh + 1) * V_DIM) for hh in range(2)]
        ks = [k_ref[0], k_ref[1]]
        vs = [v_ref[0], v_ref[1]]

        def tile(i, carry, diagonal):
            sl = pl.ds(pl.multiple_of(i * t, t), t)
            dout_all, o_all = do_ref[sl, :], o_ref[sl, :]
            out = []
            for hh in range(2):
                dk, dv = carry[2 * hh], carry[2 * hh + 1]
                qh = q_ref[hh, sl, :]
                dout = jnp.where(heads[hh], dout_all, 0.0)
                sc = _mm_nt(qh, ks[hh])
                if diagonal:
                    sc = jnp.where(below, sc, -1e30)
                p = jnp.exp(sc - lse_ref[hh, sl, 0:1])
                dp = _mm_nt(dout, vs[hh])
                ds = p * (dp - jnp.sum(dout * o_all, axis=1, keepdims=True))
                dq_ref[hh, sl, :] += _mm(ds, ks[hh])
                out += [dk + _mm_tn(ds, qh), dv + _mm_tn(p, dout)]
            return tuple(out)

        carry = tile(j, (jnp.zeros((t, w), F32),) * 4, True)
        carry = lax.fori_loop(j + 1, nq, lambda i, c: tile(i, c, False), carry)
        for hh in range(2):
            dk_ref[hh] = carry[2 * hh]
            dv_ref[hh] = jnp.where(heads[hh], carry[2 * hh + 1], 0.0)
        if steps:
            pl.when(step == n_steps - 1)(steps[2])

    s_in, s_out, s_shape, s_sems, s_ops = _side_args(side)
    res = _pcall(
        body, name="mla_flash_bwd" + ("_x" if side else ""), grid=(nh // 2, nq),
        in_specs=[pl.BlockSpec((2, s, w), lambda p, j: (p, 0, 0)),
                  pl.BlockSpec((2, t, w), lambda p, j: (p, j, 0)),
                  pl.BlockSpec((2, t, w), lambda p, j: (p, j, 0)),
                  pl.BlockSpec((s, w), lambda p, j: (0, p)),
                  pl.BlockSpec((2, s, w), lambda p, j: (p, 0, 0)),
                  pl.BlockSpec((s, w), lambda p, j: (0, p))] + s_in,
        out_specs=[pl.BlockSpec((2, s, w), lambda p, j: (p, 0, 0)),
                   pl.BlockSpec((2, t, w), lambda p, j: (p, j, 0)),
                   pl.BlockSpec((2, t, w), lambda p, j: (p, j, 0))] + s_out,
        out_shape=[jax.ShapeDtypeStruct((nh, s, w), F32)] * 3 + s_shape,
        scratch_shapes=s_sems,
        compiler_params=pltpu.CompilerParams(dimension_semantics=("arbitrary", "arbitrary"),
                                             vmem_limit_bytes=VMEM_LIMIT),
    )(q, k, v, o, lse, do, *s_ops)
    return res[0], res[1], res[2], res[3:]


def _scan(src, w_re, w_im, a_re, a_im, reverse):
    s = src.shape[0]
    nb, w = a_re.shape[0], LANES
    ch = s // 8
    assert ch & (ch - 1) == 0
    grp = 2
    tr = min(512, s)

    def cmul(ar, ai, xr, xi):
        return ar * xr - ai * xi, ar * xi + ai * xr

    def body(src_ref, wr_ref, wi_ref, ar_ref, ai_ref, xr_ref, xi_ref):
        def project(c, carry):
            rows = pl.ds(pl.multiple_of(c * tr, tr), tr)
            u = src_ref[rows, :]
            if reverse:
                br, bi = _mm_nt(u, wr_ref[...]), _mm_nt(u, wi_ref[...])
            else:
                br, bi = _mm_hi(u, wr_ref[...]), _mm_hi(u, wi_ref[...])
            for g in range(grp):
                xr_ref[g, rows, :] = _lanes(br, g, w)
                xi_ref[g, rows, :] = _lanes(bi, g, w)
            return carry

        lax.fori_loop(0, s // tr, project, 0)
        sub = lax.broadcasted_iota(jnp.int32, (8, w), 0)

        def shift(x, k):
            if reverse:
                return jnp.where(sub < 8 - k, pltpu.roll(x, 8 - k, 0), 0.0)
            return jnp.where(sub >= k, pltpu.roll(x, k, 0), 0.0)

        ar = [jnp.broadcast_to(ar_ref[g], (8, w)) for g in range(grp)]
        ai = [jnp.broadcast_to(ai_ref[g], (8, w)) for g in range(grp)]

        def tsl(i):
            return pl.ds(pl.multiple_of(((ch - 1 - i) if reverse else i) * 8, 8), 8)

        def local(i, carry):
            out = []
            for g in range(grp):
                xr, xi = carry[2 * g], carry[2 * g + 1]
                pr, pi = cmul(ar[g], ai[g], xr, xi)
                nr = pr + xr_ref[g, tsl(i), :]
                ni = pi + xi_ref[g, tsl(i), :]
                xr_ref[g, tsl(i), :] = nr
                xi_ref[g, tsl(i), :] = ni
                out += [nr, ni]
            return tuple(out)

        fin = lax.fori_loop(0, ch, local, (jnp.zeros((8, w), F32),) * (2 * grp))

        carry_in = []
        for g in range(grp):
            pr, pi = ar[g], ai[g]
            for _ in range(ch.bit_length() - 1):
                pr, pi = cmul(pr, pi, pr, pi)
            fr, fi = fin[2 * g], fin[2 * g + 1]
            for kk in (1, 2, 4):
                sr, si = cmul(pr, pi, shift(fr, kk), shift(fi, kk))
                fr, fi = fr + sr, fi + si
                pr, pi = cmul(pr, pi, pr, pi)
            carry_in += [shift(fr, 1), shift(fi, 1)]

        def fix(i, pw):
            out = []
            for g in range(grp):
                pr, pi = pw[2 * g], pw[2 * g + 1]
                cr, ci = cmul(pr, pi, carry_in[2 * g], carry_in[2 * g + 1])
                xr_ref[g, tsl(i), :] = xr_ref[g, tsl(i), :] + cr
                xi_ref[g, tsl(i), :] = xi_ref[g, tsl(i), :] + ci
                nr, ni = cmul(pr, pi, ar[g], ai[g])
                out += [nr, ni]
            return tuple(out)

        lax.fori_loop(0, ch, fix, tuple(x for g in range(grp) for x in (ar[g], ai[g])))

    per_j = 4 // grp
    blk = pl.BlockSpec((grp, s, w), lambda i: (i, 0, 0))
    ablk = pl.BlockSpec((grp, 1, w), lambda i: (i, 0, 0))
    sblk = pl.BlockSpec((s, w), lambda i: (0, i // per_j))
    if reverse:
        wblk = pl.BlockSpec((None, grp * w, w), lambda i: (i // per_j, i % per_j, 0))
    else:
        wblk = pl.BlockSpec((None, w, grp * w), lambda i: (i // per_j, 0, i % per_j))
    return _pcall(
        body, name="s5_scan_rev" if reverse else "s5_scan", grid=(nb // grp,),
        in_specs=[sblk, wblk, wblk, ablk, ablk], out_specs=[blk, blk],
        out_shape=[jax.ShapeDtypeStruct((nb, s, w), F32)] * 2,
        compiler_params=pltpu.CompilerParams(dimension_semantics=("arbitrary",), vmem_limit_bytes=VMEM_LIMIT),
    )(src, w_re, w_im, a_re, a_im)


class _Exchange(NamedTuple):
    ins: list
    out_shapes: list
    sem_shapes: list
    steps: Callable


def _gather_steps(ins, outs, sems):
    n = len(ins)
    send_sems, recv_sems, local_sems = sems
    x, y, c = lax.axis_index("x"), lax.axis_index("y"), lax.axis_index("c")
    me, sibling = (x, y, c), (x, y, 1 - c)
    chips = [(1 - x, y), (x, 1 - y), (1 - x, 1 - y)]

    def copy(a, k, block, to, src=None):
        dst = outs[a].at[4 * block[0] + 2 * block[1] + block[2]]
        return pltpu.make_async_remote_copy(
            src_ref=dst if src is None else src, dst_ref=dst,
            send_sem=send_sems.at[a, k], recv_sem=recv_sems.at[a, k], device_id=to, device_id_type=MESH)

    mine = [pltpu.make_async_copy(ins[a], outs[a].at[4 * x + 2 * y + c], local_sems.at[a]) for a in range(n)]
    first = []
    for a in range(n):
        first.append(copy(a, 0, me, sibling, src=ins[a]))
        first += [copy(a, 1 + j, me, (*chip, c), src=ins[a]) for j, chip in enumerate(chips)]
    passed = [copy(a, 4 + j, (*chip, c), sibling) for j, chip in enumerate(chips) for a in range(n)]

    def start():
        for cp in mine + first:
            cp.start()

    def pass_on():
        i = 0
        for j, chip in enumerate(chips):
            for a in range(n):
                copy(a, 1 + j, (*chip, c), me).wait_recv()
                passed[i].start()
                i += 1

    def finish():
        for a in range(n):
            copy(a, 0, sibling, me).wait_recv()
            for j, chip in enumerate(chips):
                copy(a, 4 + j, (*chip, 1 - c), me).wait_recv()
        for cp in first + passed:
            cp.wait_send()
        for cp in mine:
            cp.wait()

    return start, pass_on, finish


def _gather(arrs):
    n = len(arrs)
    return _Exchange(list(arrs), [jax.ShapeDtypeStruct((N_DEV,) + a.shape, a.dtype) for a in arrs],
                     [pltpu.SemaphoreType.DMA((n, 7)), pltpu.SemaphoreType.DMA((n, 7)), pltpu.SemaphoreType.DMA((n,))],
                     _gather_steps)


def _scatter_steps(ins, outs, sems):
    n = len(ins)
    send_sems, recv_sems, local_sems = sems
    x, y, c = lax.axis_index("x"), lax.axis_index("y"), lax.axis_index("c")
    me = 4 * x + 2 * y + c
    own, sent, arrivals = [], [], []
    for a in range(n):
        own.append(pltpu.make_async_copy(ins[a].at[me], outs[a].at[me], local_sems.at[a]))
        for k in range(1, N_DEV):
            px, py, pc = x ^ ((k >> 2) & 1), y ^ ((k >> 1) & 1), c ^ (k & 1)
            peer = 4 * px + 2 * py + pc
            sent.append(pltpu.make_async_remote_copy(
                src_ref=ins[a].at[peer], dst_ref=outs[a].at[me],
                send_sem=send_sems.at[a, k - 1], recv_sem=recv_sems.at[a, k - 1],
                device_id=(px, py, pc), device_id_type=MESH))
            arrivals.append(pltpu.make_async_remote_copy(
                src_ref=ins[a].at[me], dst_ref=outs[a].at[peer],
                send_sem=send_sems.at[a, k - 1], recv_sem=recv_sems.at[a, k - 1],
                device_id=(x, y, c), device_id_type=MESH))

    def start():
        for cp in own + sent:
            cp.start()

    def pass_on():
        pass

    def finish():
        for cp in arrivals:
            cp.wait_recv()
        for cp in sent:
            cp.wait_send()
        for cp in own:
            cp.wait()

    return start, pass_on, finish


def _scatter(grads):
    n = len(grads)
    return _Exchange(list(grads), [jax.ShapeDtypeStruct(g.shape, g.dtype) for g in grads],
                     [pltpu.SemaphoreType.DMA((n, N_DEV - 1)), pltpu.SemaphoreType.DMA((n, N_DEV - 1)),
                      pltpu.SemaphoreType.DMA((n,))], _scatter_steps)


def _together(a, b):
    def steps(ins, outs, sems):
        sa = a.steps(ins[:len(a.ins)], outs[:len(a.out_shapes)], sems[:len(a.sem_shapes)])
        sb = b.steps(ins[len(a.ins):], outs[len(a.out_shapes):], sems[len(a.sem_shapes):])

        def both(k):
            def run():
                sa[k]()
                sb[k]()
            return run
        return both(0), both(1), both(2)

    return _Exchange(a.ins + b.ins, a.out_shapes + b.out_shapes, a.sem_shapes + b.sem_shapes, steps)


def _run_exchange(name, ex):
    n_in, n_out = len(ex.ins), len(ex.out_shapes)

    def body(*refs):
        for step in ex.steps(refs[:n_in], refs[n_in:n_in + n_out], refs[n_in + n_out:]):
            step()

    any_spec = pl.BlockSpec(memory_space=pl.ANY)
    return _pcall(body, name=name, in_specs=[any_spec] * n_in, out_specs=[any_spec] * n_out,
                  out_shape=list(ex.out_shapes), scratch_shapes=list(ex.sem_shapes))(*ex.ins)


def _adam_math(g, w_, m_, v_):
    m_new = ADAM_B1 * m_ + (1.0 - ADAM_B1) * g
    v_new = ADAM_B2 * v_ + (1.0 - ADAM_B2) * (g * g)
    m_hat = m_new / (1.0 - ADAM_B1 ** ADAM_STEP)
    v_hat = v_new / (1.0 - ADAM_B2 ** ADAM_STEP)
    delta = -ADAM_LR * (m_hat / (jnp.sqrt(v_hat) + ADAM_EPS) + ADAM_WD * w_)
    return delta, m_new, v_new


def _adamw_weight(name, parts, w, m, v):
    nl = len(parts)

    def body(*refs):
        p_refs = refs[:nl]
        w_ref, m_ref, v_ref, g_ref, d_ref, mo_ref, vo_ref = refs[nl:]
        for l in range(nl):
            g = p_refs[l][0].astype(F32)
            for j in range(1, N_DEV):
                g = g + p_refs[l][j].astype(F32)
            g_ref[l] = g
            d_ref[l], mo_ref[l], vo_ref[l] = _adam_math(g, w_ref[l], m_ref[l], v_ref[l])

    return _pcall(
        body, name=name, out_shape=[jax.ShapeDtypeStruct(w.shape, F32)] * 4,
        compiler_params=pltpu.CompilerParams(vmem_limit_bytes=VMEM_LIMIT),
    )(*parts, w, m, v)

def _sum_sources(name, parts):
    r = parts.shape[1]

    def body(p_ref, g_ref):
        g = p_ref[0]
        for j in range(1, N_DEV):
            g = g + p_ref[j]
        g_ref[...] = g

    return _pcall(body, name=name, out_shape=jax.ShapeDtypeStruct((r, LANES), F32),
                  compiler_params=pltpu.CompilerParams(vmem_limit_bytes=VMEM_LIMIT))(parts)


def _adamw_small(name, g, w, m, v):
    n = len(g)

    def body(*refs):
        g_r, w_r, m_r, v_r = (refs[k * n:(k + 1) * n] for k in range(4))
        d_r, mo_r, vo_r = (refs[k * n:(k + 1) * n] for k in range(4, 7))
        for i in range(n):
            d_r[i][...], mo_r[i][...], vo_r[i][...] = _adam_math(g_r[i][...], w_r[i][...], m_r[i][...], v_r[i][...])

    res = _pcall(body, name=name, out_shape=[jax.ShapeDtypeStruct(a.shape, F32) for a in w] * 3,
                 compiler_params=pltpu.CompilerParams(vmem_limit_bytes=VMEM_LIMIT))(*g, *w, *m, *v)
    return res[:n], res[n:2 * n], res[2 * n:]


def _pack(arrs):
    flat = jnp.concatenate([a.reshape(-1) for a in arrs])
    flat = jnp.pad(flat, (0, (-flat.shape[0]) % (8 * LANES)))
    return flat.reshape(-1, LANES)


def _unpack(packed, shapes):
    flat = packed.reshape(-1)
    out, off = [], 0
    for shp in shapes:
        size = 1
        for d in shp:
            size *= d
        out.append(flat[off:off + size].reshape(shp))
        off += size
    return out


def _s5_params(lam_re, lam_im, log_dt, b_re, b_im, c_re, c_im):
    dt = jnp.exp(log_dt)[:, None]
    e = jnp.exp(lam_re * dt)
    ang = lam_im * dt
    a_re, a_im = e * jnp.cos(ang), e * jnp.sin(ang)
    nr, ni = a_re - 1.0, a_im
    den = lam_re * lam_re + lam_im * lam_im
    cr = ((nr * lam_re + ni * lam_im) / den)[..., None]
    ci = ((ni * lam_re - nr * lam_im) / den)[..., None]
    bb_re = cr * b_re - ci * b_im
    bb_im = cr * b_im + ci * b_re
    eye = jnp.eye(8, dtype=F32)[None, :, None, :, None]

    def bblk(bb):
        t = jnp.transpose(bb.reshape(4, 8, SSM_STATE, SSM_GROUP), (0, 3, 1, 2))
        return (eye * t[:, None]).reshape(4, 8 * SSM_GROUP, 8 * SSM_STATE)

    def cblk(cc):
        t = jnp.transpose(cc.reshape(4, 8, SSM_GROUP, SSM_STATE), (0, 3, 1, 2))
        return (eye * t[:, None]).reshape(4, 8 * SSM_STATE, 8 * SSM_GROUP)

    nb = SSM_GROUPS * SSM_STATE // LANES
    return (a_re.reshape(nb, 1, LANES), a_im.reshape(nb, 1, LANES), bblk(bb_re), bblk(bb_im),
            cblk(c_re), -cblk(c_im))


def _cat_blocks(x3, j):
    return jnp.concatenate([x3[4 * j + k] for k in range(4)], axis=-1)


def _to_chunks(a):
    s, c = a.shape
    return a.reshape(8, s // 8, c).transpose(1, 0, 2).reshape(s, c)


def _from_chunks(a):
    s, c = a.shape
    return a.reshape(s // 8, 8, c).transpose(1, 0, 2).reshape(s, c)


EARLY = ['w_in', 'w_uq', 'w_ukv']

FWD_PLAN = {
    'mla_qkv': ('nxt', ['w_in', 'w_uq', 'w_ukv', 'ssm_w_glu', 'w_out']),
    'flash': ('late', ['ssm_w_glu', 'w_out', 'w_xq', 'w_xkv', 'w_xo', 'w_gate', 'w_up', 'w_down']),
    's5_out': ('nxt', ['w_down']),
    'mix_out': ('nxt', ['w_xq']),
    'xattn': ('nxt', ['w_xkv', 'w_xo']),
    'ffn': ('nxt', ['w_gate', 'w_up']),
}
BWD_PLAN = {
    'ffn_bwd': ('nxt', EARLY),
    'xattn_bwd': ('own', ['w_down']),
    'flash_bwd': ('own', ['w_gate', 'w_up']),
    's5_out_bwd': ('own', ['w_xkv']),
    's5_in_bwd': ('own', ['w_xq']),
    'mla_qkv_bwd': ('own', ['w_xo']),
    'mix_in_bwd': ('own', ['ssm_w_glu', 'w_out']),
}


def _named(names, d):
    return [d[n] for n in names]


def _layer_fwd(h, memx, tabs, wl, pl_, late=None, nxt=None):
    s = h.shape[0]
    tm = min(256, s)
    cos, sin, pmat, pmat_t = tabs
    sv = {}
    wl = dict(wl)
    nxt_got = {}

    def fetch(host):
        who, names = FWD_PLAN[host]
        src = late if who == 'late' else nxt
        return _gather(_named(names, src)) if src else None

    def landed(host, got):
        who, names = FWD_PLAN[host]
        if got and who == 'late':
            wl.update(_layer_weights(dict(zip(names, got))))
        elif got:
            nxt_got.update(zip(names, got))

    def f_mix_in(h_, g, w):
        xn, _ = _rms(h_, g[...])
        return (_mm(xn, w[...]),)
    proj, = _rows("mix_in", f_mix_in, s, tm, [(h, 'r0'), (pl_['norm_mix_g'], 'f'), (wl['w_in'], 'f')],
                  [((s, D_MODEL), F32, 'r0')])

    def f_qkv(pr, cos_, sin_, gq, gkv, wq, wk, wv, pm):
        cqn = _rms(pr[:, 0:Q_LORA], gq[...])[0].astype(MXU)
        kvn = _rms(pr[:, Q_LORA:Q_LORA + KV_LORA], gkv[...])[0].astype(MXU)
        krr = _rope(pr[:, 384:512], cos_, sin_, pm)
        qs, ks, vs = [], [], []
        for hd in range(MLA_HEADS):
            qs.append(_rope(_mm(cqn, wq[hd]), cos_, sin_, pm) * MLA_SCALE)
            ks.append(_mm(kvn, wk[hd]) + krr)
            vs.append(_mm(kvn, wv[hd]))
        return jnp.stack(qs), jnp.stack(ks), jnp.stack(vs)
    hshape = (MLA_HEADS, s, HEAD_W)
    (q, k, v), got = _hosted(_rows(
        "mla_qkv", f_qkv, s, tm,
        [(proj, 'r0'), (cos, 'r0'), (sin, 'r0'), (pl_['q_norm_g'], 'f'), (pl_['kv_norm_g'], 'f'),
         (wl['w_uq'], 'f'), (wl['w_k'], 'f'), (wl['w_v'], 'f'), (pmat, 'f')],
        [(hshape, MXU, 'r1')] * 3, fetch('mla_qkv')))
    landed('mla_qkv', got)

    a_out, lse, got = _flash_fwd(q, k, v, fetch('flash'))
    landed('flash', got)

    u_ch = _to_chunks(proj[:, 512:1024])

    x_re, x_im = _scan(u_ch, pl_['b_re'], pl_['b_im'], pl_['a_re'], pl_['a_im'], False)

    def f_s5_out(xr, xi, u, cre, cim, d, wglu, bglu):
        y = jnp.concatenate([_mm_hi(_cat_blocks(xr, j), cre[j]) + _mm_hi(_cat_blocks(xi, j), cim[j])
                             for j in range(4)], axis=-1) + d[...] * u
        z = _mm(jax.nn.gelu(y), wglu[...]) + bglu[...]
        return y, y * jax.nn.sigmoid(z)
    (y_ssm, s_out_ch), got = _hosted(_rows(
        "s5_out", f_s5_out, s, tm,
        [(x_re, 'r1'), (x_im, 'r1'), (u_ch, 'r0'), (pl_['c_re'], 'f'), (pl_['c_im'], 'f'),
         (pl_['ssm_d'], 'f'), (wl['ssm_w_glu'], 'f'), (pl_['ssm_b_glu'], 'f')],
        [((s, SSM_WIDTH), F32, 'r0')] * 2, fetch('s5_out')))
    landed('s5_out', got)
    s_out = _from_chunks(s_out_ch)

    def f_mix_out(h_, a, so, ga, gs, w):
        an = _rms(a, ga[...])[0]
        sn = _rms(so, gs[...])[0]
        return (h_ + _mm(jnp.concatenate([an, sn], axis=-1), w[...]),)
    (h1,), got = _hosted(_rows("mix_out", f_mix_out, s, tm,
                               [(h, 'r0'), (a_out, 'r0'), (s_out, 'r0'), (pl_['attn_out_g'], 'f'),
                                (pl_['ssm_out_g'], 'f'), (wl['w_out'], 'f')],
                               [((s, D_MODEL), F32, 'r0')], fetch('mix_out')))
    landed('mix_out', got)

    m_len = memx.shape[0]

    def f_memkv(mm_, g, w):
        mn = _rms(mm_, g[...])[0].astype(MXU)
        return (jnp.stack([_mm(mn, w[d]) for d in range(N_DEV)]),)
    kvm, = _rows("mem_kv", f_memkv, m_len, m_len, [(memx, 'r0'), (pl_['mem_norm_g'], 'f'), (wl['w_xkv'], 'f')],
                 [((N_DEV, m_len, X_HEAD_DIM), MXU, 'r1')])

    def f_xattn(h_, g, wq, kv_, wo):
        hn = _rms(h_, g[...])[0].astype(MXU)
        out = jnp.zeros(h_.shape, F32)
        for hd in range(X_HEADS):
            cs = pl.ds(hd * X_HEAD_DIM, X_HEAD_DIM)
            qh = _mm(hn, wq[:, cs])
            p = _softmax(_mm_nt(qh, kv_[hd]) * X_SCALE)
            out = out + _mm(_mm(p, kv_[X_HEADS + hd]), wo[cs, :])
        return (h_ + out,)
    (h2,), got = _hosted(_rows("xattn", f_xattn, s, min(X_ROWS, s),
                               [(h1, 'r0'), (pl_['norm_x_g'], 'f'), (wl['w_xq'], 'f'), (kvm, 'f'), (wl['w_xo'], 'f')],
                               [((s, D_MODEL), F32, 'r0')], fetch('xattn')))
    landed('xattn', got)

    def f_ffn(h_, g, wg, wu, wd):
        hn = _rms(h_, g[...])[0].astype(MXU)
        y = jnp.zeros(h_.shape, F32)
        for c in range(D_FF // FF_CHUNK):
            cs = pl.ds(c * FF_CHUNK, FF_CHUNK)
            gate = _mm(hn, wg[:, cs])
            y = y + _mm(gate * jax.nn.sigmoid(gate) * _mm(hn, wu[:, cs]), wd[cs, :])
        return (h_ + y,)
    (h3,), got = _hosted(_rows("ffn", f_ffn, s, tm,
                               [(h2, 'r0'), (pl_['norm_ffn_g'], 'f'), (wl['w_gate'], 'f'), (wl['w_up'], 'f'),
                                (wl['w_down'], 'f')],
                               [((s, D_MODEL), F32, 'r0')], fetch('ffn')))
    landed('ffn', got)
    sv.update(h=h, proj=proj, q=q, k=k, v=v, a_out=a_out, lse=lse, x_re=x_re, x_im=x_im, y_ssm=y_ssm,
              s_out=s_out, h1=h1, kvm=kvm, h2=h2, u_ch=u_ch)
    return h3, sv, wl, nxt_got


def _layer_bwd(dh3, sv, memx, tabs, wl, pl_, nxt=None):
    s = dh3.shape[0]
    tm = min(256, s)
    cos, sin, pmat, pmat_t = tabs
    gr = {}
    arrived = {}
    act_shape = (s, D_FF)

    def send(host):
        who, names = BWD_PLAN[host]
        if who == 'nxt' and not nxt:
            return None, []
        return (_scatter([nxt[n] if who == 'nxt' else _blocked(gr, n) for n in names]),
                [(who, n) for n in names])

    def f_ffn_bwd(h_, dy, g, wg, wu, wd):
        hn, r = _rms(h_, g[...])
        hb = hn.astype(MXU)
        dyb = dy.astype(MXU)
        dhn = jnp.zeros(h_.shape, F32)
        acts, dgs, dus = [], [], []
        for c in range(D_FF // FF_CHUNK):
            cs = pl.ds(c * FF_CHUNK, FF_CHUNK)
            gate, up = _mm(hb, wg[:, cs]), _mm(hb, wu[:, cs])
            sg = jax.nn.sigmoid(gate)
            si = gate * sg
            dact = _mm_nt(dyb, wd[cs, :])
            dgate = (dact * up * (sg * (1.0 + gate * (1.0 - sg)))).astype(MXU)
            dup = (dact * si).astype(MXU)
            dhn = dhn + _mm_nt(dgate, wg[:, cs]) + _mm_nt(dup, wu[:, cs])
            acts.append((si * up).astype(MXU))
            dgs.append(dgate)
            dus.append(dup)
        dh, dg = _rms_bwd(h_, g[...], r, dhn)
        cat = lambda parts: jnp.concatenate(parts, axis=-1)
        return dy + dh, hb, cat(acts), cat(dgs), cat(dus), dg
    ex, keys = send('ffn_bwd')
    (dh2, hn_f, act, dgate, dup, gr['norm_ffn_g']), got = _hosted(_rows(
        "ffn_bwd", f_ffn_bwd, s, tm,
        [(sv['h2'], 'r0'), (dh3, 'r0'), (pl_['norm_ffn_g'], 'f'), (wl['w_gate'], 'f'), (wl['w_up'], 'f'),
         (wl['w_down'], 'f')],
        [((s, D_MODEL), F32, 'r0'), ((s, D_MODEL), MXU, 'r0'), (act_shape, MXU, 'r0'), (act_shape, MXU, 'r0'),
         (act_shape, MXU, 'r0'), ((1, D_MODEL), F32, 'a')], ex))
    arrived.update(zip(keys, got))
    gr['w_gate'] = _mm_tn_call("dw_gate", hn_f, dgate, tn=FF_CHUNK)
    gr['w_up'] = _mm_tn_call("dw_up", hn_f, dup, tn=FF_CHUNK)
    gr['w_down'] = _mm_tn_call("dw_down", act, dh3, tk=FF_CHUNK)

    m_len = memx.shape[0]

    def f_xattn_bwd(h_, dy, g, wq, kv_, wo):
        hn, r = _rms(h_, g[...])
        hb = hn.astype(MXU)
        dyb = dy.astype(MXU)
        dhn = jnp.zeros(h_.shape, F32)
        dqs, ohs, dks, dvs = [], [], [], []
        for hd in range(X_HEADS):
            cs = pl.ds(hd * X_HEAD_DIM, X_HEAD_DIM)
            kh, vh = kv_[hd], kv_[X_HEADS + hd]
            qh = _mm(hb, wq[:, cs])
            p = _softmax(_mm_nt(qh, kh) * X_SCALE)
            ohs.append(_mm(p, vh).astype(MXU))
            do = _mm_nt(dyb, wo[cs, :])
            dvs.append(_mm_tn(p, do))
            dp = _mm_nt(do, vh)
            ds = p * (dp - jnp.sum(dp * p, axis=-1, keepdims=True)) * X_SCALE
            dq = _mm(ds, kh).astype(MXU)
            dks.append(_mm_tn(ds, qh))
            dhn = dhn + _mm_nt(dq, wq[:, cs])
            dqs.append(dq)
        dh, dg = _rms_bwd(h_, g[...], r, dhn)
        return (dy + dh, hb, jnp.concatenate(dqs, axis=-1), jnp.concatenate(ohs, axis=-1),
                jnp.stack(dks + dvs), dg)
    ex, keys = send('xattn_bwd')
    (dh1, hn_x, dq_x, oh_x, dkvm, gr['norm_x_g']), got = _hosted(_rows(
        "xattn_bwd", f_xattn_bwd, s, min(X_ROWS, s),
        [(sv['h1'], 'r0'), (dh2, 'r0'), (pl_['norm_x_g'], 'f'), (wl['w_xq'], 'f'), (sv['kvm'], 'f'),
         (wl['w_xo'], 'f')],
        [((s, D_MODEL), F32, 'r0'), ((s, D_MODEL), MXU, 'r0'), ((s, D_MODEL), MXU, 'r0'),
         ((s, D_MODEL), MXU, 'r0'), ((N_DEV, m_len, X_HEAD_DIM), F32, 'a'), ((1, D_MODEL), F32, 'a')], ex))
    arrived.update(zip(keys, got))
    gr['w_xq'] = _mm_tn_call("dw_xq", hn_x, dq_x)
    gr['w_xo'] = _mm_tn_call("dw_xo", oh_x, dh2)

    def f_memkv_bwd(mm_, dkv, g, w):
        mn, r = _rms(mm_, g[...])
        mb = mn.astype(MXU)
        dmn = jnp.zeros(mm_.shape, F32)
        dws = []
        for d in range(N_DEV):
            dmn = dmn + _mm_nt(dkv[d], w[d])
            dws.append(_mm_tn(mb, dkv[d]))
        _, dg = _rms_bwd(mm_, g[...], r, dmn)
        return jnp.stack(dws), dg
    gr['w_xkv'], gr['mem_norm_g'] = _rows(
        "mem_kv_bwd", f_memkv_bwd, m_len, m_len,
        [(memx, 'r0'), (dkvm, 'r1'), (pl_['mem_norm_g'], 'f'), (wl['w_xkv'], 'f')],
        [((N_DEV, D_MODEL, X_HEAD_DIM), F32, 'a'), ((1, D_MODEL), F32, 'a')])

    def f_mix_out_bwd(a, so, dy, ga, gs, w):
        dmix = _mm_nt(dy, w[...])
        an, ra = _rms(a, ga[...])
        sn, rs = _rms(so, gs[...])
        da, dga = _rms_bwd(a, ga[...], ra, dmix[:, 0:512])
        dso, dgs = _rms_bwd(so, gs[...], rs, dmix[:, 512:1024])
        return da, dso, jnp.concatenate([an, sn], axis=-1), dga, dgs
    da_out, ds_out, mixed, gr['attn_out_g'], gr['ssm_out_g'] = _rows(
        "mix_out_bwd", f_mix_out_bwd, s, tm,
        [(sv['a_out'], 'r0'), (sv['s_out'], 'r0'), (dh1, 'r0'), (pl_['attn_out_g'], 'f'), (pl_['ssm_out_g'], 'f'),
         (wl['w_out'], 'f')],
        [((s, 512), F32, 'r0'), ((s, 512), F32, 'r0'), ((s, D_MODEL), MXU, 'r0'), ((1, 512), F32, 'a'),
         ((1, 512), F32, 'a')])
    gr['w_out'] = _mm_tn_call("dw_out", mixed, dh1)

    ex, keys = send('flash_bwd')
    dq, dk, dv, got = _flash_bwd(sv['q'], sv['k'], sv['v'], sv['a_out'], sv['lse'], da_out, ex)
    arrived.update(zip(keys, got))

    def f_s5_out_bwd(xr, xi, u, y, ds, cre, cim, d, wglu, bglu):
        g, gelu_vjp = jax.vjp(jax.nn.gelu, y)
        sig = jax.nn.sigmoid(_mm(g, wglu[...]) + bglu[...])
        dz = ds * y * sig * (1.0 - sig)
        dy = ds * sig + gelu_vjp(_mm_nt(dz, wglu[...]))[0]
        dcr, dci = [], []
        for j in range(4):
            dyj = _lanes(dy, j, LANES)
            dcr.append(_mm_tn(_cat_blocks(xr, j), dyj))
            dci.append(_mm_tn(_cat_blocks(xi, j), dyj))
        return (dy, dy * d[...], jnp.stack(dcr), jnp.stack(dci),
                jnp.sum(dy * u, axis=0, keepdims=True), _mm_tn(g, dz), jnp.sum(dz, axis=0, keepdims=True))
    ex, keys = send('s5_out_bwd')
    (dy_ssm, du_dir, gr['c_re'], gr['c_im'], gr['ssm_d'], gr['ssm_w_glu'], gr['ssm_b_glu']), got = _hosted(_rows(
        "s5_out_bwd", f_s5_out_bwd, s, tm,
        [(sv['x_re'], 'r1'), (sv['x_im'], 'r1'), (sv['u_ch'], 'r0'), (sv['y_ssm'], 'r0'), (_to_chunks(ds_out), 'r0'),
         (pl_['c_re'], 'f'), (pl_['c_im'], 'f'), (pl_['ssm_d'], 'f'), (wl['ssm_w_glu'], 'f'),
         (pl_['ssm_b_glu'], 'f')],
        [((s, 512), F32, 'r0'), ((s, 512), F32, 'r0'), ((4, 512, LANES), F32, 'a'),
         ((4, 512, LANES), F32, 'a'), ((1, 512), F32, 'a'), ((512, 512), F32, 'a'), ((1, 512), F32, 'a')], ex))
    arrived.update(zip(keys, got))
    g_re, g_im = _scan(dy_ssm, pl_['c_re'], pl_['c_im'], pl_['a_re'], -pl_['a_im'], True)
    first_re = jnp.pad(sv['x_re'][:, s - 8:s - 1], ((0, 0), (1, 0), (0, 0)))
    first_im = jnp.pad(sv['x_im'][:, s - 8:s - 1], ((0, 0), (1, 0), (0, 0)))

    def f_s5_in_bwd(gre, gim, xr, xi, pr8, pi8, u, dud, f8r, f8i, bre, bim):
        first = pl.program_id(0) == 0
        xpr = jnp.concatenate([jnp.where(first, f8r[...], pr8), xr[:, :tm - 8]], axis=1)
        xpi = jnp.concatenate([jnp.where(first, f8i[...], pi8), xi[:, :tm - 8]], axis=1)
        dus, dbr, dbi = [], [], []
        for j in range(4):
            gj_r, gj_i, uj = _cat_blocks(gre, j), _cat_blocks(gim, j), _lanes(u, j, LANES)
            dus.append(_mm_nt(gj_r, bre[j]) + _mm_nt(gj_i, bim[j]))
            dbr.append(_mm_tn(uj, gj_r))
            dbi.append(_mm_tn(uj, gj_i))
        da_r = jnp.sum(gre * xpr + gim * xpi, axis=1, keepdims=True)
        da_i = jnp.sum(gim * xpr - gre * xpi, axis=1, keepdims=True)
        return dud + jnp.concatenate(dus, axis=-1), jnp.stack(dbr), jnp.stack(dbi), da_r, da_i
    ex, keys = send('s5_in_bwd')
    (du_ch, gr['b_re'], gr['b_im'], gr['a_re'], gr['a_im']), got = _hosted(_rows(
        "s5_in_bwd", f_s5_in_bwd, s, tm,
        [(g_re, 'r1'), (g_im, 'r1'), (sv['x_re'], 'r1'), (sv['x_im'], 'r1'), (sv['x_re'], 'p8'), (sv['x_im'], 'p8'),
         (sv['u_ch'], 'r0'), (du_dir, 'r0'), (first_re, 'f'), (first_im, 'f'), (pl_['b_re'], 'f'), (pl_['b_im'], 'f')],
        [((s, 512), F32, 'r0'), ((4, LANES, 512), F32, 'a'), ((4, LANES, 512), F32, 'a'),
         ((16, 1, LANES), F32, 'a'), ((16, 1, LANES), F32, 'a')], ex))
    arrived.update(zip(keys, got))
    du = _from_chunks(du_ch)

    def f_qkv_bwd(pr, cos_, sin_, dq_, dk_, dv_, gq, gkv, wq, wk, wv, pt):
        cq, ckv = pr[:, 0:Q_LORA], pr[:, Q_LORA:Q_LORA + KV_LORA]
        cqn, rq = _rms(cq, gq[...])
        kvn, rkv = _rms(ckv, gkv[...])
        cqb, kvb = cqn.astype(MXU), kvn.astype(MXU)
        dcqn = jnp.zeros(cq.shape, F32)
        dkvn = jnp.zeros(ckv.shape, F32)
        dksum = jnp.zeros(dk_[0].shape, F32)
        dwq, dwk, dwv = [], [], []
        for hd in range(MLA_HEADS):
            dqp = (_rope_t(dq_[hd], cos_, sin_, pt) * MLA_SCALE).astype(MXU)
            dkb, dvb = dk_[hd].astype(MXU), dv_[hd].astype(MXU)
            dwq.append(_mm_tn(cqb, dqp))
            dwk.append(_mm_tn(kvb, dkb))
            dwv.append(_mm_tn(kvb, dvb))
            dcqn = dcqn + _mm_nt(dqp, wq[hd])
            dkvn = dkvn + _mm_nt(dkb, wk[hd]) + _mm_nt(dvb, wv[hd])
            dksum = dksum + dk_[hd]
        dcq, dgq = _rms_bwd(cq, gq[...], rq, dcqn)
        dckv, dgkv = _rms_bwd(ckv, gkv[...], rkv, dkvn)
        dpa = jnp.concatenate([dcq, dckv, _rope_t(dksum, cos_, sin_, pt)], axis=-1)
        return dpa, jnp.stack(dwq), jnp.stack(dwk), jnp.stack(dwv), dgq, dgkv
    ex, keys = send('mla_qkv_bwd')
    (dpa, gr['w_uq'], gr['w_k'], gr['w_v'], gr['q_norm_g'], gr['kv_norm_g']), got = _hosted(_rows(
        "mla_qkv_bwd", f_qkv_bwd, s, tm,
        [(sv['proj'], 'r0'), (cos, 'r0'), (sin, 'r0'), (dq, 'r1'), (dk, 'r1'), (dv, 'r1'), (pl_['q_norm_g'], 'f'),
         (pl_['kv_norm_g'], 'f'), (wl['w_uq'], 'f'), (wl['w_k'], 'f'), (wl['w_v'], 'f'), (pmat_t, 'f')],
        [((s, 512), F32, 'r0'), ((MLA_HEADS, Q_LORA, HEAD_W), F32, 'a'), ((MLA_HEADS, KV_LORA, HEAD_W), F32, 'a'),
         ((MLA_HEADS, KV_LORA, HEAD_W), F32, 'a'), ((1, Q_LORA), F32, 'a'), ((1, KV_LORA), F32, 'a')], ex))
    arrived.update(zip(keys, got))

    def f_mix_in_bwd(h_, dpa_, du_, dres, g, w):
        dproj = jnp.concatenate([dpa_, du_], axis=-1).astype(MXU)
        xn, r = _rms(h_, g[...])
        dh, dg = _rms_bwd(h_, g[...], r, _mm_nt(dproj, w[...]))
        return dres + dh, xn, dproj, dg
    ex, keys = send('mix_in_bwd')
    (dh0, xn, dproj, gr['norm_mix_g']), got = _hosted(_rows(
        "mix_in_bwd", f_mix_in_bwd, s, tm,
        [(sv['h'], 'r0'), (dpa, 'r0'), (du, 'r0'), (dh1, 'r0'), (pl_['norm_mix_g'], 'f'), (wl['w_in'], 'f')],
        [((s, D_MODEL), F32, 'r0'), ((s, D_MODEL), MXU, 'r0'), ((s, D_MODEL), MXU, 'r0'), ((1, D_MODEL), F32, 'a')],
        ex))
    arrived.update(zip(keys, got))
    gr['w_in'] = _mm_tn_call("dw_in", xn, dproj)
    return dh0, gr, arrived


def _layer_weights(w):
    wl = {}
    if 'w_in' in w:
        w_in = w['w_in'].reshape(D_MODEL, -1)
        z = lambda n: jnp.zeros((D_MODEL, n), w_in.dtype)
        wl['w_in'] = jnp.concatenate([w_in[:, :384], z(64), w_in[:, 384:416], z(32), w_in[:, 416:]], axis=1)
    if 'w_uq' in w:
        wl['w_uq'] = jnp.pad(w['w_uq'], ((0, 0), (0, 0), (0, HEAD_W - QK_NOPE - QK_ROPE)))
    if 'w_ukv' in w:
        wl['w_k'] = jnp.pad(w['w_ukv'][..., :QK_NOPE], ((0, 0), (0, 0), (0, HEAD_W - QK_NOPE)))
        wv = w['w_ukv'][..., QK_NOPE:]
        even = (jnp.arange(MLA_HEADS) % 2 == 0)[:, None, None]
        wl['w_v'] = jnp.concatenate([jnp.where(even, wv, 0), jnp.where(even, 0, wv)], axis=-1).astype(wv.dtype)
    if 'ssm_w_glu' in w:
        wl['ssm_w_glu'] = w['ssm_w_glu'].reshape(SSM_WIDTH, SSM_WIDTH)
    for n in ('w_out', 'w_xq', 'w_xo'):
        if n in w:
            wl[n] = w[n].reshape(D_MODEL, D_MODEL)
    if 'w_xkv' in w:
        wl['w_xkv'] = w['w_xkv']
    for n in ('w_gate', 'w_up'):
        if n in w:
            wl[n] = jnp.transpose(w[n], (1, 0, 2)).reshape(D_MODEL, D_FF)
    if 'w_down' in w:
        wl['w_down'] = w['w_down'].reshape(D_FF, D_MODEL)
    return wl


def _blocked(gr, n):
    if n == 'w_in':
        d = gr['w_in']
        out = jnp.concatenate([d[:, :384], d[:, 448:480], d[:, 512:]], axis=1).reshape(N_DEV, 128, -1)
    elif n == 'w_uq':
        out = gr['w_uq'][..., :QK_NOPE + QK_ROPE]
    elif n == 'w_ukv':
        even = (jnp.arange(MLA_HEADS) % 2 == 0)[:, None, None]
        dv = gr['w_v']
        out = jnp.concatenate([gr['w_k'][..., :QK_NOPE], jnp.where(even, dv[..., :V_DIM], dv[..., V_DIM:])], axis=-1)
    elif n == 'ssm_w_glu':
        out = gr['ssm_w_glu'].reshape(N_DEV, SSM_WIDTH // N_DEV, SSM_WIDTH)
    elif n in ('w_out', 'w_xq', 'w_xo'):
        out = gr[n].reshape(N_DEV, D_MODEL // N_DEV, D_MODEL)
    elif n in ('w_gate', 'w_up'):
        out = jnp.transpose(gr[n].reshape(D_MODEL, N_DEV, D_FF // N_DEV), (1, 0, 2))
    elif n == 'w_down':
        out = gr[n].reshape(N_DEV, D_FF // N_DEV, D_MODEL)
    else:
        out = gr[n]
    return out.astype(MXU)


def kernel(x, mem, positions, norm_mix_g, w_in, q_norm_g, w_uq, kv_norm_g, w_ukv, ssm_lambda_re, ssm_lambda_im, ssm_log_dt, ssm_b_re, ssm_b_im, ssm_c_re, ssm_c_im, ssm_d, ssm_w_glu, ssm_b_glu, attn_out_g, ssm_out_g, w_out, norm_x_g, mem_norm_g, w_xq, w_xkv, w_xo, norm_ffn_g, w_gate, w_up, w_down, final_norm_g, loss_target, m_norm_mix_g, m_w_in, m_q_norm_g, m_w_uq, m_kv_norm_g, m_w_ukv, m_ssm_lambda_re, m_ssm_lambda_im, m_ssm_log_dt, m_ssm_b_re, m_ssm_b_im, m_ssm_c_re, m_ssm_c_im, m_ssm_d, m_ssm_w_glu, m_ssm_b_glu, m_attn_out_g, m_ssm_out_g, m_w_out, m_norm_x_g, m_mem_norm_g, m_w_xq, m_w_xkv, m_w_xo, m_norm_ffn_g, m_w_gate, m_w_up, m_w_down, m_final_norm_g, v_norm_mix_g, v_w_in, v_q_norm_g, v_w_uq, v_kv_norm_g, v_w_ukv, v_ssm_lambda_re, v_ssm_lambda_im, v_ssm_log_dt, v_ssm_b_re, v_ssm_b_im, v_ssm_c_re, v_ssm_c_im, v_ssm_d, v_ssm_w_glu, v_ssm_b_glu, v_attn_out_g, v_ssm_out_g, v_w_out, v_norm_x_g, v_mem_norm_g, v_w_xq, v_w_xkv, v_w_xo, v_norm_ffn_g, v_w_gate, v_w_up, v_w_down, v_final_norm_g):
    args = dict(locals())
    W = {n: args[n] for n in WEIGHTS}
    M = {n: args['m_' + n] for n in WEIGHTS}
    V = {n: args['v_' + n] for n in WEIGHTS}
    s = x.shape[1]
    h = x[0]
    memx = mem[0]

    freqs = ROPE_THETA ** (-jnp.arange(0, QK_ROPE, 2, dtype=F32) / QK_ROPE)
    ang = positions[0].astype(F32)[:, None] * freqs
    c16, s16 = jnp.cos(ang), jnp.sin(ang)
    cos = jnp.concatenate([jnp.ones((s, QK_NOPE), F32), c16, c16, jnp.zeros((s, 32), F32)], axis=1)
    sin = jnp.concatenate([jnp.zeros((s, QK_NOPE), F32), s16, s16, jnp.zeros((s, 32), F32)], axis=1)
    idx = jnp.arange(QK_ROPE // 2)
    pmat = jnp.zeros((HEAD_W, HEAD_W), F32)
    pmat = pmat.at[QK_NOPE + 16 + idx, QK_NOPE + idx].set(-1.0).at[QK_NOPE + idx, QK_NOPE + 16 + idx].set(1.0)
    tabs = (cos, sin, pmat, pmat.T)

    shards = [{n: W[n][l].astype(MXU) for n in SHARDED} for l in range(DEPTH)]
    gathered = dict(zip(EARLY, _run_exchange("gather_weights", _gather(_named(EARLY, shards[0])))))

    layers = []
    for l in range(DEPTH):
        wl = _layer_weights(gathered)
        s5_in = [W[n][l] for n in ('ssm_lambda_re', 'ssm_lambda_im', 'ssm_log_dt', 'ssm_b_re', 'ssm_b_im',
                                   'ssm_c_re', 'ssm_c_im')]
        (a_re, a_im, bre, bim, cre, cim), s5_vjp = jax.vjp(_s5_params, *s5_in)
        pl_ = {n: W[n][l][None] for n in ('norm_mix_g', 'q_norm_g', 'kv_norm_g', 'ssm_d', 'ssm_b_glu',
                                           'attn_out_g', 'ssm_out_g', 'norm_x_g', 'mem_norm_g', 'norm_ffn_g')}
        pl_.update(a_re=a_re, a_im=a_im, b_re=bre, b_im=bim, c_re=cre, c_im=cim)
        h, sv, wl, gathered = _layer_fwd(h, memx, tabs, wl, pl_, shards[0] if l == 0 else None,
                                         shards[l + 1] if l + 1 < DEPTH else None)
        layers.append((wl, pl_, sv, s5_vjp))

    def f_loss(h_, tgt, g):
        y, r = _rms(h_, g[...])
        err = y - tgt
        part = 0.5 * jnp.sum(jnp.mean(err * err, axis=-1, keepdims=True), axis=0, keepdims=True)
        dh, dg = _rms_bwd(h_, g[...], r, err / D_MODEL)
        return dh, dg, jnp.broadcast_to(part, (8, LANES))
    dh, g_final, loss_part = _rows(
        "loss_head", f_loss, s, min(256, s), [(h, 'r0'), (loss_target[0], 'r0'), (final_norm_g[None], 'f')],
        [((s, D_MODEL), F32, 'r0'), ((1, D_MODEL), F32, 'a'), ((8, LANES), F32, 'a')])
    loss = lax.psum(loss_part[0, 0], ("x", "y", "c"))

    parts = [{} for _ in range(DEPTH)]
    g_rep = [None] * DEPTH
    blocks = None
    for l in reversed(range(DEPTH)):
        wl, pl_, sv, s5_vjp = layers[l]
        dh, gr, arrived = _layer_bwd(dh, sv, memx, tabs, wl, pl_, blocks)
        for (who, n), p in arrived.items():
            parts[l + 1 if who == 'nxt' else l][n] = p
        blocks = {n: _blocked(gr, n) for n in EARLY}
        ds5 = s5_vjp((gr['a_re'], gr['a_im'], gr['b_re'], gr['b_im'], gr['c_re'], gr['c_im']))
        rep = dict(zip(('ssm_lambda_re', 'ssm_lambda_im', 'ssm_log_dt', 'ssm_b_re', 'ssm_b_im', 'ssm_c_re',
                        'ssm_c_im'), ds5))
        for n in ('norm_mix_g', 'q_norm_g', 'kv_norm_g', 'ssm_d', 'ssm_b_glu', 'attn_out_g', 'ssm_out_g',
                  'norm_x_g', 'mem_norm_g', 'norm_ffn_g'):
            rep[n] = gr[n][0]
        g_rep[l] = rep
    grad_x = dh[None]

    rep_names = REPL_L + ['final_norm_g']
    g_loc = {n: jnp.stack([g_rep[l][n] for l in range(DEPTH)]) for n in REPL_L}
    g_loc['final_norm_g'] = g_final
    rest = [n for n in SHARDED if n not in parts[0]]
    last = _run_exchange("last_grads", _together(_scatter(_named(rest, blocks)),
                                                 _gather([_pack(_named(rep_names, g_loc))])))
    parts[0].update(zip(rest, last[:len(rest)]))

    out_sh = [{}, {}, {}, {}]
    for n in SHARDED:
        res = _adamw_weight("adamw_" + n, [parts[l][n] for l in range(DEPTH)], W[n], M[n], V[n])
        for kind, r in enumerate(res):
            out_sh[kind][n] = r

    shapes_rp = [(1,) + W[n].shape if W[n].ndim == 1 else W[n].shape for n in rep_names]
    g_rp = _unpack(_sum_sources("sum_small_grads", last[len(rest)]), shapes_rp)
    as_rows = lambda d: [d[n].reshape(shp) for n, shp in zip(rep_names, shapes_rp)]
    res_rp = (g_rp,) + _adamw_small("adamw_replicated", g_rp, as_rows(W), as_rows(M), as_rows(V))
    out_rp = [{n: a.reshape(W[n].shape) for n, a in zip(rep_names, r)} for r in res_rp]

    outs = [loss, grad_x]
    for kind in range(4):
        for n in WEIGHTS:
            outs.append(out_sh[kind][n] if n in SHARDED else out_rp[kind][n])
    return tuple(outs)
```

```python
from typing import Callable, NamedTuple

import jax
import jax.numpy as jnp
from jax import lax
from jax.experimental import pallas as pl
from jax.experimental.pallas import tpu as pltpu

F32 = jnp.float32
MXU = jnp.bfloat16
HI = lax.Precision.HIGHEST

D_MODEL = 1024
MLA_HEADS = 8
QK_NOPE = 64
QK_ROPE = 32
V_DIM = 64
Q_LORA = 256
KV_LORA = 128
SSM_WIDTH = 512
SSM_GROUPS = 32
SSM_GROUP = 16
SSM_STATE = 64
X_HEADS = 4
X_HEAD_DIM = 256
D_FF = 2816
FF_CHUNK = D_FF // 2
ROPE_THETA = 10000.0
EPS = 1e-6
DEPTH = 2
N_DEV = 8
LANES = 128
HEAD_W = 128
MLA_SCALE = (QK_NOPE + QK_ROPE) ** -0.5
X_SCALE = X_HEAD_DIM ** -0.5
ADAM_LR, ADAM_B1, ADAM_B2, ADAM_EPS, ADAM_WD, ADAM_STEP = 0.001, 0.9, 0.999, 1e-08, 0.01, 10
VMEM_LIMIT = 56 * 1024 * 1024
FLASH_TILE = 512
DW_ROWS = 2048
X_ROWS = 512
MESH = pl.DeviceIdType.MESH

SHARDED = ['w_in', 'w_uq', 'w_ukv', 'ssm_w_glu', 'w_out', 'w_xq', 'w_xkv', 'w_xo', 'w_gate', 'w_up', 'w_down']
REPL_L = ['norm_mix_g', 'q_norm_g', 'kv_norm_g', 'ssm_lambda_re', 'ssm_lambda_im', 'ssm_log_dt', 'ssm_b_re',
          'ssm_b_im', 'ssm_c_re', 'ssm_c_im', 'ssm_d', 'ssm_b_glu', 'attn_out_g', 'ssm_out_g', 'norm_x_g',
          'mem_norm_g', 'norm_ffn_g']
WEIGHTS = ['norm_mix_g', 'w_in', 'q_norm_g', 'w_uq', 'kv_norm_g', 'w_ukv', 'ssm_lambda_re', 'ssm_lambda_im',
           'ssm_log_dt', 'ssm_b_re', 'ssm_b_im', 'ssm_c_re', 'ssm_c_im', 'ssm_d', 'ssm_w_glu', 'ssm_b_glu',
           'attn_out_g', 'ssm_out_g', 'w_out', 'norm_x_g', 'mem_norm_g', 'w_xq', 'w_xkv', 'w_xo', 'norm_ffn_g',
           'w_gate', 'w_up', 'w_down', 'final_norm_g']


def _pcall(body, **kw):
    return pl.pallas_call(body, **kw)


def _mm(a, b):
    return jnp.dot(a.astype(MXU), b.astype(MXU), preferred_element_type=F32)


def _mm_nt(a, b):
    return lax.dot_general(a.astype(MXU), b.astype(MXU), (((1,), (1,)), ((), ())), preferred_element_type=F32)


def _mm_tn(a, b):
    return lax.dot_general(a.astype(MXU), b.astype(MXU), (((0,), (0,)), ((), ())), preferred_element_type=F32)


def _mm_hi(a, b):
    return jnp.dot(a.astype(F32), b.astype(F32), precision=HI, preferred_element_type=F32)


def _rms(x, g):
    r = lax.rsqrt(jnp.mean(x * x, axis=-1, keepdims=True) + EPS)
    return x * r * g, r


def _rms_bwd(x, g, r, dy):
    dyg = dy * g
    dx = r * dyg - x * (r * r * r) * jnp.mean(dyg * x, axis=-1, keepdims=True)
    return dx, jnp.sum(dy * x * r, axis=0, keepdims=True)


def _rope(x, cos, sin, p_ref):
    return x * cos + _mm_hi(x, p_ref[...]) * sin


def _rope_t(g, cos, sin, pt_ref):
    return g * cos + _mm_hi(g * sin, pt_ref[...])


def _softmax(s):
    m = jnp.max(s, axis=-1, keepdims=True)
    e = jnp.exp(s - m)
    return e / jnp.sum(e, axis=-1, keepdims=True)


def _lanes(x, j, w):
    return x[:, j * w:(j + 1) * w]


def _rows(name, fn, n, tm, ins, outs, side=None):
    def spec(shape, kind):
        nd = len(shape)
        if kind == 'p8':
            return pl.BlockSpec((shape[0], 8, shape[2]), lambda i: (0, jnp.maximum(i * (tm // 8) - 1, 0), 0))
        if kind == 'f':
            return pl.BlockSpec(shape, lambda i, _nd=nd: (0,) * _nd, pipeline_mode=pl.Buffered(1))
        if kind == 'a':
            return pl.BlockSpec(shape, lambda i, _nd=nd: (0,) * _nd)
        ax = int(kind[1])
        blk = tuple(tm if d == ax else s for d, s in enumerate(shape))
        return pl.BlockSpec(blk, lambda i, _ax=ax, _nd=nd: tuple(i if d == _ax else 0 for d in range(_nd)))

    n_in, n_out, n_steps = len(ins), len(outs), n // tm

    def body(*refs):
        in_refs, out_refs, steps = _side_split(refs, n_in, n_out, side)
        i = pl.program_id(0)
        if steps:
            pl.when(i == 0)(steps[0])
            pl.when(i == _pass_on_step(n_steps))(steps[1])
        args = [r if k == 'f' else r[...] for r, (_, k) in zip(in_refs, ins)]
        res = fn(*args)
        for r, (_, dt, k), v in zip(out_refs, outs, res):
            if k == 'a':
                _accumulate(r, v.astype(dt), i)
            else:
                r[...] = v.astype(dt)
        if steps:
            pl.when(i == n_steps - 1)(steps[2])

    s_in, s_out, s_shape, s_sems, s_ops = _side_args(side)
    res = _pcall(
        body, name=name + ("_x" if side else ""), grid=(n_steps,),
        in_specs=[spec(a.shape, k) for a, k in ins] + s_in,
        out_specs=[spec(s, k) for s, _, k in outs] + s_out,
        out_shape=[jax.ShapeDtypeStruct(s, dt) for s, dt, _ in outs] + s_shape,
        scratch_shapes=s_sems,
        compiler_params=pltpu.CompilerParams(dimension_semantics=("arbitrary",), vmem_limit_bytes=VMEM_LIMIT),
    )(*[a for a, _ in ins], *s_ops)
    return _Hosted(res[:n_out], res[n_out:]) if side else res


def _accumulate(ref, v, i):
    @pl.when(i == 0)
    def _():
        ref[...] = v

    @pl.when(i != 0)
    def _():
        ref[...] += v


def _mm_tn_call(name, a, b, tk=None, tn=None):
    out_dtype = MXU
    s, k = a.shape
    n = b.shape[1]
    tk, tn = tk or k, tn or n
    ts = min(DW_ROWS, s)
    ns = s // ts

    def body(a_ref, b_ref, o_ref, acc_ref):
        j = pl.program_id(2)
        _accumulate(acc_ref, _mm_tn(a_ref[...], b_ref[...]), j)

        @pl.when(j == ns - 1)
        def _():
            o_ref[...] = acc_ref[...].astype(out_dtype)

    return _pcall(
        body, name=name, grid=(k // tk, n // tn, ns),
        in_specs=[pl.BlockSpec((ts, tk), lambda ik, jn, j: (j, ik)),
                  pl.BlockSpec((ts, tn), lambda ik, jn, j: (j, jn))],
        out_specs=pl.BlockSpec((tk, tn), lambda ik, jn, j: (ik, jn)),
        out_shape=jax.ShapeDtypeStruct((k, n), out_dtype),
        scratch_shapes=[pltpu.VMEM((tk, tn), F32)],
        compiler_params=pltpu.CompilerParams(dimension_semantics=("arbitrary", "arbitrary", "arbitrary"),
                                             vmem_limit_bytes=VMEM_LIMIT),
    )(a, b)


def _side_split(refs, n_in, n_out, side):
    if side is None:
        return refs[:n_in], refs[n_in:n_in + n_out], None
    si, so = len(side.ins), len(side.out_shapes)
    own_in, side_in = refs[:n_in], refs[n_in:n_in + si]
    own_out, side_out = refs[n_in + si:n_in + si + n_out], refs[n_in + si + n_out:n_in + si + n_out + so]
    return own_in, own_out, side.steps(side_in, side_out, refs[n_in + si + n_out + so:])


def _pass_on_step(n_steps):
    return max(n_steps - 2, 0)


def _side_args(side):
    if side is None:
        return [], [], [], [], []
    any_spec = pl.BlockSpec(memory_space=pl.ANY)
    return ([any_spec] * len(side.ins), [any_spec] * len(side.out_shapes), list(side.out_shapes),
            list(side.sem_shapes), list(side.ins))


class _Hosted(NamedTuple):
    results: list
    arrived: list


def _hosted(res):
    return res if isinstance(res, _Hosted) else _Hosted(res, ())


def _flash_fwd(q, k, v, side=None):
    nh, s, w = q.shape
    t = min(FLASH_TILE, s)
    nq = s // t
    n_steps = (nh // 2) * nq

    def body(*refs):
        (q_ref, k_ref, v_ref), (o_ref, lse_ref), steps = _side_split(refs, 3, 2, side)
        step = pl.program_id(0) * nq + pl.program_id(1)
        if steps:
            pl.when(step == 0)(steps[0])
            pl.when(step == _pass_on_step(n_steps))(steps[1])
        qi = pl.program_id(1)
        qs = [q_ref[0], q_ref[1]]
        below = lax.broadcasted_iota(jnp.int32, (t, t), 1) <= lax.broadcasted_iota(jnp.int32, (t, t), 0)

        def tile(j, carry, diagonal):
            sl = pl.ds(pl.multiple_of(j * t, t), t)
            out = []
            for hh in range(2):
                m, l, acc = carry[3 * hh:3 * hh + 3]
                sc = _mm_nt(qs[hh], k_ref[hh, sl, :])
                if diagonal:
                    sc = jnp.where(below, sc, -1e30)
                m_new = jnp.maximum(m, jnp.max(sc, axis=1, keepdims=True))
                p = jnp.exp(sc - m_new)
                alpha = jnp.exp(m - m_new)
                out += [m_new, alpha * l + jnp.sum(p, axis=1, keepdims=True), alpha * acc + _mm(p, v_ref[hh, sl, :])]
            return tuple(out)

        init = (jnp.full((t, 1), -1e30, F32), jnp.zeros((t, 1), F32), jnp.zeros((t, w), F32)) * 2
        carry = lax.fori_loop(0, qi, lambda j, c: tile(j, c, False), init)
        carry = tile(qi, carry, True)
        o_ref[...] = carry[2] / carry[1] + carry[5] / carry[4]
        for hh in range(2):
            lse_ref[hh] = jnp.broadcast_to(carry[3 * hh] + jnp.log(carry[3 * hh + 1]), (t, w))
        if steps:
            pl.when(step == n_steps - 1)(steps[2])

    s_in, s_out, s_shape, s_sems, s_ops = _side_args(side)
    res = _pcall(
        body, name="mla_flash_fwd" + ("_x" if side else ""), grid=(nh // 2, nq),
        in_specs=[pl.BlockSpec((2, t, w), lambda p, i: (p, i, 0)),
                  pl.BlockSpec((2, s, w), lambda p, i: (p, 0, 0)),
                  pl.BlockSpec((2, s, w), lambda p, i: (p, 0, 0))] + s_in,
        out_specs=[pl.BlockSpec((t, w), lambda p, i: (i, p)),
                   pl.BlockSpec((2, t, w), lambda p, i: (p, i, 0))] + s_out,
        out_shape=[jax.ShapeDtypeStruct((s, (nh // 2) * w), F32), jax.ShapeDtypeStruct((nh, s, w), F32)] + s_shape,
        scratch_shapes=s_sems,
        compiler_params=pltpu.CompilerParams(dimension_semantics=("arbitrary", "arbitrary"),
                                             vmem_limit_bytes=VMEM_LIMIT),
    )(q, k, v, *s_ops)
    return res[0], res[1], res[2:]


def _flash_bwd(q, k, v, o, lse, do, side=None):
    nh, s, w = q.shape
    t = min(FLASH_TILE, s)
    nq = s // t
    n_steps = (nh // 2) * nq

    def body(*refs):
        (q_ref, k_ref, v_ref, o_ref, lse_ref, do_ref), (dq_ref, dk_ref, dv_ref), steps = _side_split(refs, 6, 3, side)
        step = pl.program_id(0) * nq + pl.program_id(1)
        if steps:
            pl.when(step == 0)(steps[0])
            pl.when(step == _pass_on_step(n_steps))(steps[1])
        j = pl.program_id(1)

        @pl.when(j == 0)
        def _():
            dq_ref[...] = jnp.zeros(dq_ref.shape, F32)

        below = lax.broadcasted_iota(jnp.int32, (t, t), 1) <= lax.broadcasted_iota(jnp.int32, (t, t), 0)
        lane = lax.broadcasted_iota(jnp.int32, (t, w), 1)
        heads = [jnp.logical_and(lane >= hh * V_DIM, lane < (hh + 1) * V_DIM) for hh in range(2)]
        ks = [k_ref[0], k_ref[1]]
        vs = [v_ref[0], v_ref[1]]

        def tile(i, carry, diagonal):
            sl = pl.ds(pl.multiple_of(i * t, t), t)
            dout_all, o_all = do_ref[sl, :], o_ref[sl, :]
            out = []
            for hh in range(2):
                dk, dv = carry[2 * hh], carry[2 * hh + 1]
                qh = q_ref[hh, sl, :]
                dout = jnp.where(heads[hh], dout_all, 0.0)
                sc = _mm_nt(qh, ks[hh])
                if diagonal:
                    sc = jnp.where(below, sc, -1e30)
                p = jnp.exp(sc - lse_ref[hh, sl, 0:1])
                dp = _mm_nt(dout, vs[hh])
                ds = p * (dp - jnp.sum(dout * o_all, axis=1, keepdims=True))
                dq_ref[hh, sl, :] += _mm(ds, ks[hh])
                out += [dk + _mm_tn(ds, qh), dv + _mm_tn(p, dout)]
            return tuple(out)

        carry = tile(j, (jnp.zeros((t, w), F32),) * 4, True)
        carry = lax.fori_loop(j + 1, nq, lambda i, c: tile(i, c, False), carry)
        for hh in range(2):
            dk_ref[hh] = carry[2 * hh]
            dv_ref[hh] = jnp.where(heads[hh], carry[2 * hh + 1], 0.0)
        if steps:
            pl.when(step == n_steps - 1)(steps[2])

    s_in, s_out, s_shape, s_sems, s_ops = _side_args(side)
    res = _pcall(
        body, name="mla_flash_bwd" + ("_x" if side else ""), grid=(nh // 2, nq),
        in_specs=[pl.BlockSpec((2, s, w), lambda p, j: (p, 0, 0)),
                  pl.BlockSpec((2, t, w), lambda p, j: (p, j, 0)),
                  pl.BlockSpec((2, t, w), lambda p, j: (p, j, 0)),
                  pl.BlockSpec((s, w), lambda p, j: (0, p)),
                  pl.BlockSpec((2, s, w), lambda p, j: (p, 0, 0)),
                  pl.BlockSpec((s, w), lambda p, j: (0, p))] + s_in,
        out_specs=[pl.BlockSpec((2, s, w), lambda p, j: (p, 0, 0)),
                   pl.BlockSpec((2, t, w), lambda p, j: (p, j, 0)),
                   pl.BlockSpec((2, t, w), lambda p, j: (p, j, 0))] + s_out,
        out_shape=[jax.ShapeDtypeStruct((nh, s, w), F32)] * 3 + s_shape,
        scratch_shapes=s_sems,
        compiler_params=pltpu.CompilerParams(dimension_semantics=("arbitrary", "arbitrary"),
                                             vmem_limit_bytes=VMEM_LIMIT),
    )(q, k, v, o, lse, do, *s_ops)
    return res[0], res[1], res[2], res[3:]


def _scan(src, w_re, w_im, a_re, a_im, reverse):
    s = src.shape[0]
    nb, w = a_re.shape[0], LANES
    ch = s // 8
    assert ch & (ch - 1) == 0
    grp = 4
    tr = min(512, s)

    def cmul(ar, ai, xr, xi):
        return ar * xr - ai * xi, ar * xi + ai * xr

    def body(src_ref, wr_ref, wi_ref, ar_ref, ai_ref, xr_ref, xi_ref):
        def project(c, carry):
            rows = pl.ds(pl.multiple_of(c * tr, tr), tr)
            u = src_ref[rows, :]
            if reverse:
                br, bi = _mm_nt(u, wr_ref[...]), _mm_nt(u, wi_ref[...])
            else:
                br, bi = _mm(u, wr_ref[...]), _mm(u, wi_ref[...])
            for g in range(grp):
                xr_ref[g, rows, :] = _lanes(br, g, w)
                xi_ref[g, rows, :] = _lanes(bi, g, w)
            return carry

        lax.fori_loop(0, s // tr, project, 0)
        sub = lax.broadcasted_iota(jnp.int32, (8, w), 0)

        def shift(x, k):
            if reverse:
                return jnp.where(sub < 8 - k, pltpu.roll(x, 8 - k, 0), 0.0)
            return jnp.where(sub >= k, pltpu.roll(x, k, 0), 0.0)

        ar = [jnp.broadcast_to(ar_ref[g], (8, w)) for g in range(grp)]
        ai = [jnp.broadcast_to(ai_ref[g], (8, w)) for g in range(grp)]

        def tsl(i):
            return pl.ds(pl.multiple_of(((ch - 1 - i) if reverse else i) * 8, 8), 8)

        def local(i, carry):
            out = []
            for g in range(grp):
                xr, xi = carry[2 * g], carry[2 * g + 1]
                pr, pi = cmul(ar[g], ai[g], xr, xi)
                nr = pr + xr_ref[g, tsl(i), :]
                ni = pi + xi_ref[g, tsl(i), :]
                xr_ref[g, tsl(i), :] = nr
                xi_ref[g, tsl(i), :] = ni
                out += [nr, ni]
            return tuple(out)

        fin = lax.fori_loop(0, ch, local, (jnp.zeros((8, w), F32),) * (2 * grp))

        carry_in = []
        for g in range(grp):
            pr, pi = ar[g], ai[g]
            for _ in range(ch.bit_length() - 1):
                pr, pi = cmul(pr, pi, pr, pi)
            fr, fi = fin[2 * g], fin[2 * g + 1]
            for kk in (1, 2, 4):
                sr, si = cmul(pr, pi, shift(fr, kk), shift(fi, kk))
                fr, fi = fr + sr, fi + si
                pr, pi = cmul(pr, pi, pr, pi)
            carry_in += [shift(fr, 1), shift(fi, 1)]

        def fix(i, pw):
            out = []
            for g in range(grp):
                pr, pi = pw[2 * g], pw[2 * g + 1]
                cr, ci = cmul(pr, pi, carry_in[2 * g], carry_in[2 * g + 1])
                xr_ref[g, tsl(i), :] = xr_ref[g, tsl(i), :] + cr
                xi_ref[g, tsl(i), :] = xi_ref[g, tsl(i), :] + ci
                nr, ni = cmul(pr, pi, ar[g], ai[g])
                out += [nr, ni]
            return tuple(out)

        lax.fori_loop(0, ch, fix, tuple(x for g in range(grp) for x in (ar[g], ai[g])))

    per_j = 4 // grp
    blk = pl.BlockSpec((grp, s, w), lambda i: (i, 0, 0))
    ablk = pl.BlockSpec((grp, 1, w), lambda i: (i, 0, 0))
    sblk = pl.BlockSpec((s, w), lambda i: (0, i // per_j))
    if reverse:
        wblk = pl.BlockSpec((None, grp * w, w), lambda i: (i // per_j, i % per_j, 0))
    else:
        wblk = pl.BlockSpec((None, w, grp * w), lambda i: (i // per_j, 0, i % per_j))
    return _pcall(
        body, name="s5_scan_rev" if reverse else "s5_scan", grid=(nb // grp,),
        in_specs=[sblk, wblk, wblk, ablk, ablk], out_specs=[blk, blk],
        out_shape=[jax.ShapeDtypeStruct((nb, s, w), F32)] * 2,
        compiler_params=pltpu.CompilerParams(dimension_semantics=("arbitrary",), vmem_limit_bytes=VMEM_LIMIT),
    )(src, w_re, w_im, a_re, a_im)


class _Exchange(NamedTuple):
    ins: list
    out_shapes: list
    sem_shapes: list
    steps: Callable


def _gather_steps(ins, outs, sems):
    n = len(ins)
    send_sems, recv_sems, local_sems = sems
    x, y, c = lax.axis_index("x"), lax.axis_index("y"), lax.axis_index("c")
    me, sibling = (x, y, c), (x, y, 1 - c)
    chips = [(1 - x, y), (x, 1 - y), (1 - x, 1 - y)]

    def copy(a, k, block, to, src=None):
        dst = outs[a].at[4 * block[0] + 2 * block[1] + block[2]]
        return pltpu.make_async_remote_copy(
            src_ref=dst if src is None else src, dst_ref=dst,
            send_sem=send_sems.at[a, k], recv_sem=recv_sems.at[a, k], device_id=to, device_id_type=MESH)

    mine = [pltpu.make_async_copy(ins[a], outs[a].at[4 * x + 2 * y + c], local_sems.at[a]) for a in range(n)]
    first = []
    for a in range(n):
        first.append(copy(a, 0, me, sibling, src=ins[a]))
        first += [copy(a, 1 + j, me, (*chip, c), src=ins[a]) for j, chip in enumerate(chips)]
    passed = [copy(a, 4 + j, (*chip, c), sibling) for j, chip in enumerate(chips) for a in range(n)]

    def start():
        for cp in mine + first:
            cp.start()

    def pass_on():
        i = 0
        for j, chip in enumerate(chips):
            for a in range(n):
                copy(a, 1 + j, (*chip, c), me).wait_recv()
                passed[i].start()
                i += 1

    def finish():
        for a in range(n):
            copy(a, 0, sibling, me).wait_recv()
            for j, chip in enumerate(chips):
                copy(a, 4 + j, (*chip, 1 - c), me).wait_recv()
        for cp in first + passed:
            cp.wait_send()
        for cp in mine:
            cp.wait()

    return start, pass_on, finish


def _gather(arrs):
    n = len(arrs)
    return _Exchange(list(arrs), [jax.ShapeDtypeStruct((N_DEV,) + a.shape, a.dtype) for a in arrs],
                     [pltpu.SemaphoreType.DMA((n, 7)), pltpu.SemaphoreType.DMA((n, 7)), pltpu.SemaphoreType.DMA((n,))],
                     _gather_steps)


def _scatter_steps(ins, outs, sems):
    n = len(ins)
    send_sems, recv_sems, local_sems = sems
    x, y, c = lax.axis_index("x"), lax.axis_index("y"), lax.axis_index("c")
    me = 4 * x + 2 * y + c
    own, sent, arrivals = [], [], []
    for a in range(n):
        own.append(pltpu.make_async_copy(ins[a].at[me], outs[a].at[me], local_sems.at[a]))
        for k in range(1, N_DEV):
            px, py, pc = x ^ ((k >> 2) & 1), y ^ ((k >> 1) & 1), c ^ (k & 1)
            peer = 4 * px + 2 * py + pc
            sent.append(pltpu.make_async_remote_copy(
                src_ref=ins[a].at[peer], dst_ref=outs[a].at[me],
                send_sem=send_sems.at[a, k - 1], recv_sem=recv_sems.at[a, k - 1],
                device_id=(px, py, pc), device_id_type=MESH))
            arrivals.append(pltpu.make_async_remote_copy(
                src_ref=ins[a].at[me], dst_ref=outs[a].at[peer],
                send_sem=send_sems.at[a, k - 1], recv_sem=recv_sems.at[a, k - 1],
                device_id=(x, y, c), device_id_type=MESH))

    def start():
        for cp in own + sent:
            cp.start()

    def pass_on():
        pass

    def finish():
        for cp in arrivals:
            cp.wait_recv()
        for cp in sent:
            cp.wait_send()
        for cp in own:
            cp.wait()

    return start, pass_on, finish


def _scatter(grads):
    n = len(grads)
    return _Exchange(list(grads), [jax.ShapeDtypeStruct(g.shape, g.dtype) for g in grads],
                     [pltpu.SemaphoreType.DMA((n, N_DEV - 1)), pltpu.SemaphoreType.DMA((n, N_DEV - 1)),
                      pltpu.SemaphoreType.DMA((n,))], _scatter_steps)


def _together(a, b):
    def steps(ins, outs, sems):
        sa = a.steps(ins[:len(a.ins)], outs[:len(a.out_shapes)], sems[:len(a.sem_shapes)])
        sb = b.steps(ins[len(a.ins):], outs[len(a.out_shapes):], sems[len(a.sem_shapes):])

        def both(k):
            def run():
                sa[k]()
                sb[k]()
            return run
        return both(0), both(1), both(2)

    return _Exchange(a.ins + b.ins, a.out_shapes + b.out_shapes, a.sem_shapes + b.sem_shapes, steps)


def _run_exchange(name, ex):
    n_in, n_out = len(ex.ins), len(ex.out_shapes)

    def body(*refs):
        for step in ex.steps(refs[:n_in], refs[n_in:n_in + n_out], refs[n_in + n_out:]):
            step()

    any_spec = pl.BlockSpec(memory_space=pl.ANY)
    return _pcall(body, name=name, in_specs=[any_spec] * n_in, out_specs=[any_spec] * n_out,
                  out_shape=list(ex.out_shapes), scratch_shapes=list(ex.sem_shapes))(*ex.ins)


def _adam_math(g, w_, m_, v_):
    m_new = ADAM_B1 * m_ + (1.0 - ADAM_B1) * g
    v_new = ADAM_B2 * v_ + (1.0 - ADAM_B2) * (g * g)
    m_hat = m_new / (1.0 - ADAM_B1 ** ADAM_STEP)
    v_hat = v_new / (1.0 - ADAM_B2 ** ADAM_STEP)
    delta = -ADAM_LR * (m_hat / (jnp.sqrt(v_hat) + ADAM_EPS) + ADAM_WD * w_)
    return delta, m_new, v_new


def _adamw_weight(name, parts, w, m, v):
    nl = len(parts)

    def body(*refs):
        p_refs = refs[:nl]
        w_ref, m_ref, v_ref, g_ref, d_ref, mo_ref, vo_ref = refs[nl:]
        for l in range(nl):
            g = p_refs[l][0].astype(F32)
            for j in range(1, N_DEV):
                g = g + p_refs[l][j].astype(F32)
            g_ref[l] = g
            d_ref[l], mo_ref[l], vo_ref[l] = _adam_math(g, w_ref[l], m_ref[l], v_ref[l])

    return _pcall(
        body, name=name, out_shape=[jax.ShapeDtypeStruct(w.shape, F32)] * 4,
        compiler_params=pltpu.CompilerParams(vmem_limit_bytes=VMEM_LIMIT),
    )(*parts, w, m, v)

def _sum_sources(name, parts):
    r = parts.shape[1]

    def body(p_ref, g_ref):
        g = p_ref[0]
        for j in range(1, N_DEV):
            g = g + p_ref[j]
        g_ref[...] = g

    return _pcall(body, name=name, out_shape=jax.ShapeDtypeStruct((r, LANES), F32),
                  compiler_params=pltpu.CompilerParams(vmem_limit_bytes=VMEM_LIMIT))(parts)


def _adamw_small(name, g, w, m, v):
    n = len(g)

    def body(*refs):
        g_r, w_r, m_r, v_r = (refs[k * n:(k + 1) * n] for k in range(4))
        d_r, mo_r, vo_r = (refs[k * n:(k + 1) * n] for k in range(4, 7))
        for i in range(n):
            d_r[i][...], mo_r[i][...], vo_r[i][...] = _adam_math(g_r[i][...], w_r[i][...], m_r[i][...], v_r[i][...])

    res = _pcall(body, name=name, out_shape=[jax.ShapeDtypeStruct(a.shape, F32) for a in w] * 3,
                 compiler_params=pltpu.CompilerParams(vmem_limit_bytes=VMEM_LIMIT))(*g, *w, *m, *v)
    return res[:n], res[n:2 * n], res[2 * n:]


def _pack(arrs):
    flat = jnp.concatenate([a.reshape(-1) for a in arrs])
    flat = jnp.pad(flat, (0, (-flat.shape[0]) % (8 * LANES)))
    return flat.reshape(-1, LANES)


def _unpack(packed, shapes):
    flat = packed.reshape(-1)
    out, off = [], 0
    for shp in shapes:
        size = 1
        for d in shp:
            size *= d
        out.append(flat[off:off + size].reshape(shp))
        off += size
    return out


def _s5_params(lam_re, lam_im, log_dt, b_re, b_im, c_re, c_im):
    dt = jnp.exp(log_dt)[:, None]
    e = jnp.exp(lam_re * dt)
    ang = lam_im * dt
    a_re, a_im = e * jnp.cos(ang), e * jnp.sin(ang)
    nr, ni = a_re - 1.0, a_im
    den = lam_re * lam_re + lam_im * lam_im
    cr = ((nr * lam_re + ni * lam_im) / den)[..., None]
    ci = ((ni * lam_re - nr * lam_im) / den)[..., None]
    bb_re = cr * b_re - ci * b_im
    bb_im = cr * b_im + ci * b_re
    eye = jnp.eye(8, dtype=F32)[None, :, None, :, None]

    def bblk(bb):
        t = jnp.transpose(bb.reshape(4, 8, SSM_STATE, SSM_GROUP), (0, 3, 1, 2))
        return (eye * t[:, None]).reshape(4, 8 * SSM_GROUP, 8 * SSM_STATE)

    def cblk(cc):
        t = jnp.transpose(cc.reshape(4, 8, SSM_GROUP, SSM_STATE), (0, 3, 1, 2))
        return (eye * t[:, None]).reshape(4, 8 * SSM_STATE, 8 * SSM_GROUP)

    nb = SSM_GROUPS * SSM_STATE // LANES
    return (a_re.reshape(nb, 1, LANES), a_im.reshape(nb, 1, LANES), bblk(bb_re), bblk(bb_im),
            cblk(c_re), -cblk(c_im))


def _cat_blocks(x3, j):
    return jnp.concatenate([x3[4 * j + k] for k in range(4)], axis=-1)


def _to_chunks(a):
    s, c = a.shape
    return a.reshape(8, s // 8, c).transpose(1, 0, 2).reshape(s, c)


def _from_chunks(a):
    s, c = a.shape
    return a.reshape(s // 8, 8, c).transpose(1, 0, 2).reshape(s, c)


EARLY = ['w_in', 'w_uq', 'w_ukv']

FWD_PLAN = {
    'mla_qkv': ('nxt', ['w_in', 'w_uq', 'w_ukv', 'ssm_w_glu', 'w_out']),
    'flash': ('late', ['ssm_w_glu', 'w_out', 'w_xq', 'w_xkv', 'w_xo', 'w_gate', 'w_up', 'w_down']),
    's5_out': ('nxt', ['w_down']),
    'mix_out': ('nxt', ['w_xq']),
    'xattn': ('nxt', ['w_xkv', 'w_xo']),
    'ffn': ('nxt', ['w_gate', 'w_up']),
}
BWD_PLAN = {
    'ffn_bwd': ('nxt', EARLY),
    'xattn_bwd': ('own', ['w_down']),
    'flash_bwd': ('own', ['w_gate', 'w_up']),
    's5_out_bwd': ('own', ['w_xkv']),
    's5_in_bwd': ('own', ['w_xq']),
    'mla_qkv_bwd': ('own', ['w_xo']),
    'mix_in_bwd': ('own', ['ssm_w_glu', 'w_out']),
}


def _named(names, d):
    return [d[n] for n in names]


def _layer_fwd(h, memx, tabs, wl, pl_, late=None, nxt=None):
    s = h.shape[0]
    tm = min(256, s)
    cos, sin, pmat, pmat_t = tabs
    sv = {}
    wl = dict(wl)
    nxt_got = {}

    def fetch(host):
        who, names = FWD_PLAN[host]
        src = late if who == 'late' else nxt
        return _gather(_named(names, src)) if src else None

    def landed(host, got):
        who, names = FWD_PLAN[host]
        if got and who == 'late':
            wl.update(_layer_weights(dict(zip(names, got))))
        elif got:
            nxt_got.update(zip(names, got))

    def f_mix_in(h_, g, w):
        xn, _ = _rms(h_, g[...])
        return (_mm(xn, w[...]),)
    proj, = _rows("mix_in", f_mix_in, s, tm, [(h, 'r0'), (pl_['norm_mix_g'], 'f'), (wl['w_in'], 'f')],
                  [((s, D_MODEL), F32, 'r0')])

    def f_qkv(pr, cos_, sin_, gq, gkv, wq, wk, wv, pm):
        cqn = _rms(pr[:, 0:Q_LORA], gq[...])[0].astype(MXU)
        kvn = _rms(pr[:, Q_LORA:Q_LORA + KV_LORA], gkv[...])[0].astype(MXU)
        krr = _rope(pr[:, 384:512], cos_, sin_, pm)
        qs, ks, vs = [], [], []
        for hd in range(MLA_HEADS):
            qs.append(_rope(_mm(cqn, wq[hd]), cos_, sin_, pm) * MLA_SCALE)
            ks.append(_mm(kvn, wk[hd]) + krr)
            vs.append(_mm(kvn, wv[hd]))
        return jnp.stack(qs), jnp.stack(ks), jnp.stack(vs)
    hshape = (MLA_HEADS, s, HEAD_W)
    (q, k, v), got = _hosted(_rows(
        "mla_qkv", f_qkv, s, tm,
        [(proj, 'r0'), (cos, 'r0'), (sin, 'r0'), (pl_['q_norm_g'], 'f'), (pl_['kv_norm_g'], 'f'),
         (wl['w_uq'], 'f'), (wl['w_k'], 'f'), (wl['w_v'], 'f'), (pmat, 'f')],
        [(hshape, MXU, 'r1')] * 3, fetch('mla_qkv')))
    landed('mla_qkv', got)

    a_out, lse, got = _flash_fwd(q, k, v, fetch('flash'))
    landed('flash', got)

    u_ch = _to_chunks(proj[:, 512:1024])

    x_re, x_im = _scan(u_ch, pl_['b_re'], pl_['b_im'], pl_['a_re'], pl_['a_im'], False)

    def f_s5_out(xr, xi, u, cre, cim, d, wglu, bglu):
        y = jnp.concatenate([_mm(_cat_blocks(xr, j), cre[j]) + _mm(_cat_blocks(xi, j), cim[j])
                             for j in range(4)], axis=-1) + d[...] * u
        z = _mm(jax.nn.gelu(y), wglu[...]) + bglu[...]
        return y, y * jax.nn.sigmoid(z)
    (y_ssm, s_out_ch), got = _hosted(_rows(
        "s5_out", f_s5_out, s, tm,
        [(x_re, 'r1'), (x_im, 'r1'), (u_ch, 'r0'), (pl_['c_re'], 'f'), (pl_['c_im'], 'f'),
         (pl_['ssm_d'], 'f'), (wl['ssm_w_glu'], 'f'), (pl_['ssm_b_glu'], 'f')],
        [((s, SSM_WIDTH), F32, 'r0')] * 2, fetch('s5_out')))
    landed('s5_out', got)
    s_out = _from_chunks(s_out_ch)

    def f_mix_out(h_, a, so, ga, gs, w):
        an = _rms(a, ga[...])[0]
        sn = _rms(so, gs[...])[0]
        return (h_ + _mm(jnp.concatenate([an, sn], axis=-1), w[...]),)
    (h1,), got = _hosted(_rows("mix_out", f_mix_out, s, tm,
                               [(h, 'r0'), (a_out, 'r0'), (s_out, 'r0'), (pl_['attn_out_g'], 'f'),
                                (pl_['ssm_out_g'], 'f'), (wl['w_out'], 'f')],
                               [((s, D_MODEL), F32, 'r0')], fetch('mix_out')))
    landed('mix_out', got)

    m_len = memx.shape[0]

    def f_memkv(mm_, g, w):
        mn = _rms(mm_, g[...])[0].astype(MXU)
        return (jnp.stack([_mm(mn, w[d]) for d in range(N_DEV)]),)
    kvm, = _rows("mem_kv", f_memkv, m_len, m_len, [(memx, 'r0'), (pl_['mem_norm_g'], 'f'), (wl['w_xkv'], 'f')],
                 [((N_DEV, m_len, X_HEAD_DIM), MXU, 'r1')])

    def f_xattn(h_, g, wq, kv_, wo):
        hn = _rms(h_, g[...])[0].astype(MXU)
        out = jnp.zeros(h_.shape, F32)
        for hd in range(X_HEADS):
            cs = pl.ds(hd * X_HEAD_DIM, X_HEAD_DIM)
            qh = _mm(hn, wq[:, cs])
            p = _softmax(_mm_nt(qh, kv_[hd]) * X_SCALE)
            out = out + _mm(_mm(p, kv_[X_HEADS + hd]), wo[cs, :])
        return (h_ + out,)
    (h2,), got = _hosted(_rows("xattn", f_xattn, s, min(X_ROWS, s),
                               [(h1, 'r0'), (pl_['norm_x_g'], 'f'), (wl['w_xq'], 'f'), (kvm, 'f'), (wl['w_xo'], 'f')],
                               [((s, D_MODEL), F32, 'r0')], fetch('xattn')))
    landed('xattn', got)

    def f_ffn(h_, g, wg, wu, wd):
        hn = _rms(h_, g[...])[0].astype(MXU)
        y = jnp.zeros(h_.shape, F32)
        for c in range(D_FF // FF_CHUNK):
            cs = pl.ds(c * FF_CHUNK, FF_CHUNK)
            gate = _mm(hn, wg[:, cs])
            y = y + _mm(gate * jax.nn.sigmoid(gate) * _mm(hn, wu[:, cs]), wd[cs, :])
        return (h_ + y,)
    (h3,), got = _hosted(_rows("ffn", f_ffn, s, tm,
                               [(h2, 'r0'), (pl_['norm_ffn_g'], 'f'), (wl['w_gate'], 'f'), (wl['w_up'], 'f'),
                                (wl['w_down'], 'f')],
                               [((s, D_MODEL), F32, 'r0')], fetch('ffn')))
    landed('ffn', got)
    sv.update(h=h, proj=proj, q=q, k=k, v=v, a_out=a_out, lse=lse, x_re=x_re, x_im=x_im, y_ssm=y_ssm,
              s_out=s_out, h1=h1, kvm=kvm, h2=h2, u_ch=u_ch)
    return h3, sv, wl, nxt_got


def _layer_bwd(dh3, sv, memx, tabs, wl, pl_, nxt=None):
    s = dh3.shape[0]
    tm = min(256, s)
    cos, sin, pmat, pmat_t = tabs
    gr = {}
    arrived = {}
    act_shape = (s, D_FF)

    def send(host):
        who, names = BWD_PLAN[host]
        if who == 'nxt' and not nxt:
            return None, []
        return (_scatter([nxt[n] if who == 'nxt' else _blocked(gr, n) for n in names]),
                [(who, n) for n in names])

    def f_ffn_bwd(h_, dy, g, wg, wu, wd):
        hn, r = _rms(h_, g[...])
        hb = hn.astype(MXU)
        dyb = dy.astype(MXU)
        dhn = jnp.zeros(h_.shape, F32)
        acts, dgs, dus = [], [], []
        for c in range(D_FF // FF_CHUNK):
            cs = pl.ds(c * FF_CHUNK, FF_CHUNK)
            gate, up = _mm(hb, wg[:, cs]), _mm(hb, wu[:, cs])
            sg = jax.nn.sigmoid(gate)
            si = gate * sg
            dact = _mm_nt(dyb, wd[cs, :])
            dgate = (dact * up * (sg * (1.0 + gate * (1.0 - sg)))).astype(MXU)
            dup = (dact * si).astype(MXU)
            dhn = dhn + _mm_nt(dgate, wg[:, cs]) + _mm_nt(dup, wu[:, cs])
            acts.append((si * up).astype(MXU))
            dgs.append(dgate)
            dus.append(dup)
        dh, dg = _rms_bwd(h_, g[...], r, dhn)
        cat = lambda parts: jnp.concatenate(parts, axis=-1)
        return dy + dh, hb, cat(acts), cat(dgs), cat(dus), dg
    ex, keys = send('ffn_bwd')
    (dh2, hn_f, act, dgate, dup, gr['norm_ffn_g']), got = _hosted(_rows(
        "ffn_bwd", f_ffn_bwd, s, tm,
        [(sv['h2'], 'r0'), (dh3, 'r0'), (pl_['norm_ffn_g'], 'f'), (wl['w_gate'], 'f'), (wl['w_up'], 'f'),
         (wl['w_down'], 'f')],
        [((s, D_MODEL), F32, 'r0'), ((s, D_MODEL), MXU, 'r0'), (act_shape, MXU, 'r0'), (act_shape, MXU, 'r0'),
         (act_shape, MXU, 'r0'), ((1, D_MODEL), F32, 'a')], ex))
    arrived.update(zip(keys, got))
    gr['w_gate'] = _mm_tn_call("dw_gate", hn_f, dgate, tn=FF_CHUNK)
    gr['w_up'] = _mm_tn_call("dw_up", hn_f, dup, tn=FF_CHUNK)
    gr['w_down'] = _mm_tn_call("dw_down", act, dh3, tk=FF_CHUNK)

    m_len = memx.shape[0]

    def f_xattn_bwd(h_, dy, g, wq, kv_, wo):
        hn, r = _rms(h_, g[...])
        hb = hn.astype(MXU)
        dyb = dy.astype(MXU)
        dhn = jnp.zeros(h_.shape, F32)
        dqs, ohs, dks, dvs = [], [], [], []
        for hd in range(X_HEADS):
            cs = pl.ds(hd * X_HEAD_DIM, X_HEAD_DIM)
            kh, vh = kv_[hd], kv_[X_HEADS + hd]
            qh = _mm(hb, wq[:, cs])
            p = _softmax(_mm_nt(qh, kh) * X_SCALE)
            ohs.append(_mm(p, vh).astype(MXU))
            do = _mm_nt(dyb, wo[cs, :])
            dvs.append(_mm_tn(p, do))
            dp = _mm_nt(do, vh)
            ds = p * (dp - jnp.sum(dp * p, axis=-1, keepdims=True)) * X_SCALE
            dq = _mm(ds, kh).astype(MXU)
            dks.append(_mm_tn(ds, qh))
            dhn = dhn + _mm_nt(dq, wq[:, cs])
            dqs.append(dq)
        dh, dg = _rms_bwd(h_, g[...], r, dhn)
        return (dy + dh, hb, jnp.concatenate(dqs, axis=-1), jnp.concatenate(ohs, axis=-1),
                jnp.stack(dks + dvs), dg)
    ex, keys = send('xattn_bwd')
    (dh1, hn_x, dq_x, oh_x, dkvm, gr['norm_x_g']), got = _hosted(_rows(
        "xattn_bwd", f_xattn_bwd, s, min(X_ROWS, s),
        [(sv['h1'], 'r0'), (dh2, 'r0'), (pl_['norm_x_g'], 'f'), (wl['w_xq'], 'f'), (sv['kvm'], 'f'),
         (wl['w_xo'], 'f')],
        [((s, D_MODEL), F32, 'r0'), ((s, D_MODEL), MXU, 'r0'), ((s, D_MODEL), MXU, 'r0'),
         ((s, D_MODEL), MXU, 'r0'), ((N_DEV, m_len, X_HEAD_DIM), F32, 'a'), ((1, D_MODEL), F32, 'a')], ex))
    arrived.update(zip(keys, got))
    gr['w_xq'] = _mm_tn_call("dw_xq", hn_x, dq_x)
    gr['w_xo'] = _mm_tn_call("dw_xo", oh_x, dh2)

    def f_memkv_bwd(mm_, dkv, g, w):
        mn, r = _rms(mm_, g[...])
        mb = mn.astype(MXU)
        dmn = jnp.zeros(mm_.shape, F32)
        dws = []
        for d in range(N_DEV):
            dmn = dmn + _mm_nt(dkv[d], w[d])
            dws.append(_mm_tn(mb, dkv[d]))
        _, dg = _rms_bwd(mm_, g[...], r, dmn)
        return jnp.stack(dws), dg
    gr['w_xkv'], gr['mem_norm_g'] = _rows(
        "mem_kv_bwd", f_memkv_bwd, m_len, m_len,
        [(memx, 'r0'), (dkvm, 'r1'), (pl_['mem_norm_g'], 'f'), (wl['w_xkv'], 'f')],
        [((N_DEV, D_MODEL, X_HEAD_DIM), F32, 'a'), ((1, D_MODEL), F32, 'a')])

    def f_mix_out_bwd(a, so, dy, ga, gs, w):
        dmix = _mm_nt(dy, w[...])
        an, ra = _rms(a, ga[...])
        sn, rs = _rms(so, gs[...])
        da, dga = _rms_bwd(a, ga[...], ra, dmix[:, 0:512])
        dso, dgs = _rms_bwd(so, gs[...], rs, dmix[:, 512:1024])
        return da, dso, jnp.concatenate([an, sn], axis=-1), dga, dgs
    da_out, ds_out, mixed, gr['attn_out_g'], gr['ssm_out_g'] = _rows(
        "mix_out_bwd", f_mix_out_bwd, s, tm,
        [(sv['a_out'], 'r0'), (sv['s_out'], 'r0'), (dh1, 'r0'), (pl_['attn_out_g'], 'f'), (pl_['ssm_out_g'], 'f'),
         (wl['w_out'], 'f')],
        [((s, 512), F32, 'r0'), ((s, 512), F32, 'r0'), ((s, D_MODEL), MXU, 'r0'), ((1, 512), F32, 'a'),
         ((1, 512), F32, 'a')])
    gr['w_out'] = _mm_tn_call("dw_out", mixed, dh1)

    ex, keys = send('flash_bwd')
    dq, dk, dv, got = _flash_bwd(sv['q'], sv['k'], sv['v'], sv['a_out'], sv['lse'], da_out, ex)
    arrived.update(zip(keys, got))

    def f_s5_out_bwd(xr, xi, u, y, ds, cre, cim, d, wglu, bglu):
        g, gelu_vjp = jax.vjp(jax.nn.gelu, y)
        sig = jax.nn.sigmoid(_mm(g, wglu[...]) + bglu[...])
        dz = ds * y * sig * (1.0 - sig)
        dy = ds * sig + gelu_vjp(_mm_nt(dz, wglu[...]))[0]
        dcr, dci = [], []
        for j in range(4):
            dyj = _lanes(dy, j, LANES)
            dcr.append(_mm_tn(_cat_blocks(xr, j), dyj))
            dci.append(_mm_tn(_cat_blocks(xi, j), dyj))
        return (dy, dy * d[...], jnp.stack(dcr), jnp.stack(dci),
                jnp.sum(dy * u, axis=0, keepdims=True), _mm_tn(g, dz), jnp.sum(dz, axis=0, keepdims=True))
    ex, keys = send('s5_out_bwd')
    (dy_ssm, du_dir, gr['c_re'], gr['c_im'], gr['ssm_d'], gr['ssm_w_glu'], gr['ssm_b_glu']), got = _hosted(_rows(
        "s5_out_bwd", f_s5_out_bwd, s, tm,
        [(sv['x_re'], 'r1'), (sv['x_im'], 'r1'), (sv['u_ch'], 'r0'), (sv['y_ssm'], 'r0'), (_to_chunks(ds_out), 'r0'),
         (pl_['c_re'], 'f'), (pl_['c_im'], 'f'), (pl_['ssm_d'], 'f'), (wl['ssm_w_glu'], 'f'),
         (pl_['ssm_b_glu'], 'f')],
        [((s, 512), F32, 'r0'), ((s, 512), F32, 'r0'), ((4, 512, LANES), F32, 'a'),
         ((4, 512, LANES), F32, 'a'), ((1, 512), F32, 'a'), ((512, 512), F32, 'a'), ((1, 512), F32, 'a')], ex))
    arrived.update(zip(keys, got))
    g_re, g_im = _scan(dy_ssm, pl_['c_re'], pl_['c_im'], pl_['a_re'], -pl_['a_im'], True)
    first_re = jnp.pad(sv['x_re'][:, s - 8:s - 1], ((0, 0), (1, 0), (0, 0)))
    first_im = jnp.pad(sv['x_im'][:, s - 8:s - 1], ((0, 0), (1, 0), (0, 0)))

    def f_s5_in_bwd(gre, gim, xr, xi, pr8, pi8, u, dud, f8r, f8i, bre, bim):
        first = pl.program_id(0) == 0
        xpr = jnp.concatenate([jnp.where(first, f8r[...], pr8), xr[:, :tm - 8]], axis=1)
        xpi = jnp.concatenate([jnp.where(first, f8i[...], pi8), xi[:, :tm - 8]], axis=1)
        dus, dbr, dbi = [], [], []
        for j in range(4):
            gj_r, gj_i, uj = _cat_blocks(gre, j), _cat_blocks(gim, j), _lanes(u, j, LANES)
            dus.append(_mm_nt(gj_r, bre[j]) + _mm_nt(gj_i, bim[j]))
            dbr.append(_mm_tn(uj, gj_r))
            dbi.append(_mm_tn(uj, gj_i))
        da_r = jnp.sum(gre * xpr + gim * xpi, axis=1, keepdims=True)
        da_i = jnp.sum(gim * xpr - gre * xpi, axis=1, keepdims=True)
        return dud + jnp.concatenate(dus, axis=-1), jnp.stack(dbr), jnp.stack(dbi), da_r, da_i
    ex, keys = send('s5_in_bwd')
    (du_ch, gr['b_re'], gr['b_im'], gr['a_re'], gr['a_im']), got = _hosted(_rows(
        "s5_in_bwd", f_s5_in_bwd, s, tm,
        [(g_re, 'r1'), (g_im, 'r1'), (sv['x_re'], 'r1'), (sv['x_im'], 'r1'), (sv['x_re'], 'p8'), (sv['x_im'], 'p8'),
         (sv['u_ch'], 'r0'), (du_dir, 'r0'), (first_re, 'f'), (first_im, 'f'), (pl_['b_re'], 'f'), (pl_['b_im'], 'f')],
        [((s, 512), F32, 'r0'), ((4, LANES, 512), F32, 'a'), ((4, LANES, 512), F32, 'a'),
         ((16, 1, LANES), F32, 'a'), ((16, 1, LANES), F32, 'a')], ex))
    arrived.update(zip(keys, got))
    du = _from_chunks(du_ch)

    def f_qkv_bwd(pr, cos_, sin_, dq_, dk_, dv_, gq, gkv, wq, wk, wv, pt):
        cq, ckv = pr[:, 0:Q_LORA], pr[:, Q_LORA:Q_LORA + KV_LORA]
        cqn, rq = _rms(cq, gq[...])
        kvn, rkv = _rms(ckv, gkv[...])
        cqb, kvb = cqn.astype(MXU), kvn.astype(MXU)
        dcqn = jnp.zeros(cq.shape, F32)
        dkvn = jnp.zeros(ckv.shape, F32)
        dksum = jnp.zeros(dk_[0].shape, F32)
        dwq, dwk, dwv = [], [], []
        for hd in range(MLA_HEADS):
            dqp = (_rope_t(dq_[hd], cos_, sin_, pt) * MLA_SCALE).astype(MXU)
            dkb, dvb = dk_[hd].astype(MXU), dv_[hd].astype(MXU)
            dwq.append(_mm_tn(cqb, dqp))
            dwk.append(_mm_tn(kvb, dkb))
            dwv.append(_mm_tn(kvb, dvb))
            dcqn = dcqn + _mm_nt(dqp, wq[hd])
            dkvn = dkvn + _mm_nt(dkb, wk[hd]) + _mm_nt(dvb, wv[hd])
            dksum = dksum + dk_[hd]
        dcq, dgq = _rms_bwd(cq, gq[...], rq, dcqn)
        dckv, dgkv = _rms_bwd(ckv, gkv[...], rkv, dkvn)
        dpa = jnp.concatenate([dcq, dckv, _rope_t(dksum, cos_, sin_, pt)], axis=-1)
        return dpa, jnp.stack(dwq), jnp.stack(dwk), jnp.stack(dwv), dgq, dgkv
    ex, keys = send('mla_qkv_bwd')
    (dpa, gr['w_uq'], gr['w_k'], gr['w_v'], gr['q_norm_g'], gr['kv_norm_g']), got = _hosted(_rows(
        "mla_qkv_bwd", f_qkv_bwd, s, tm,
        [(sv['proj'], 'r0'), (cos, 'r0'), (sin, 'r0'), (dq, 'r1'), (dk, 'r1'), (dv, 'r1'), (pl_['q_norm_g'], 'f'),
         (pl_['kv_norm_g'], 'f'), (wl['w_uq'], 'f'), (wl['w_k'], 'f'), (wl['w_v'], 'f'), (pmat_t, 'f')],
        [((s, 512), F32, 'r0'), ((MLA_HEADS, Q_LORA, HEAD_W), F32, 'a'), ((MLA_HEADS, KV_LORA, HEAD_W), F32, 'a'),
         ((MLA_HEADS, KV_LORA, HEAD_W), F32, 'a'), ((1, Q_LORA), F32, 'a'), ((1, KV_LORA), F32, 'a')], ex))
    arrived.update(zip(keys, got))

    def f_mix_in_bwd(h_, dpa_, du_, dres, g, w):
        dproj = jnp.concatenate([dpa_, du_], axis=-1).astype(MXU)
        xn, r = _rms(h_, g[...])
        dh, dg = _rms_bwd(h_, g[...], r, _mm_nt(dproj, w[...]))
        return dres + dh, xn, dproj, dg
    ex, keys = send('mix_in_bwd')
    (dh0, xn, dproj, gr['norm_mix_g']), got = _hosted(_rows(
        "mix_in_bwd", f_mix_in_bwd, s, tm,
        [(sv['h'], 'r0'), (dpa, 'r0'), (du, 'r0'), (dh1, 'r0'), (pl_['norm_mix_g'], 'f'), (wl['w_in'], 'f')],
        [((s, D_MODEL), F32, 'r0'), ((s, D_MODEL), MXU, 'r0'), ((s, D_MODEL), MXU, 'r0'), ((1, D_MODEL), F32, 'a')],
        ex))
    arrived.update(zip(keys, got))
    gr['w_in'] = _mm_tn_call("dw_in", xn, dproj)
    return dh0, gr, arrived


def _layer_weights(w):
    wl = {}
    if 'w_in' in w:
        w_in = w['w_in'].reshape(D_MODEL, -1)
        z = lambda n: jnp.zeros((D_MODEL, n), w_in.dtype)
        wl['w_in'] = jnp.concatenate([w_in[:, :384], z(64), w_in[:, 384:416], z(32), w_in[:, 416:]], axis=1)
    if 'w_uq' in w:
        wl['w_uq'] = jnp.pad(w['w_uq'], ((0, 0), (0, 0), (0, HEAD_W - QK_NOPE - QK_ROPE)))
    if 'w_ukv' in w:
        wl['w_k'] = jnp.pad(w['w_ukv'][..., :QK_NOPE], ((0, 0), (0, 0), (0, HEAD_W - QK_NOPE)))
        wv = w['w_ukv'][..., QK_NOPE:]
        even = (jnp.arange(MLA_HEADS) % 2 == 0)[:, None, None]
        wl['w_v'] = jnp.concatenate([jnp.where(even, wv, 0), jnp.where(even, 0, wv)], axis=-1).astype(wv.dtype)
    if 'ssm_w_glu' in w:
        wl['ssm_w_glu'] = w['ssm_w_glu'].reshape(SSM_WIDTH, SSM_WIDTH)
    for n in ('w_out', 'w_xq', 'w_xo'):
        if n in w:
            wl[n] = w[n].reshape(D_MODEL, D_MODEL)
    if 'w_xkv' in w:
        wl['w_xkv'] = w['w_xkv']
    for n in ('w_gate', 'w_up'):
        if n in w:
            wl[n] = jnp.transpose(w[n], (1, 0, 2)).reshape(D_MODEL, D_FF)
    if 'w_down' in w:
        wl['w_down'] = w['w_down'].reshape(D_FF, D_MODEL)
    return wl


def _blocked(gr, n):
    if n == 'w_in':
        d = gr['w_in']
        out = jnp.concatenate([d[:, :384], d[:, 448:480], d[:, 512:]], axis=1).reshape(N_DEV, 128, -1)
    elif n == 'w_uq':
        out = gr['w_uq'][..., :QK_NOPE + QK_ROPE]
    elif n == 'w_ukv':
        even = (jnp.arange(MLA_HEADS) % 2 == 0)[:, None, None]
        dv = gr['w_v']
        out = jnp.concatenate([gr['w_k'][..., :QK_NOPE], jnp.where(even, dv[..., :V_DIM], dv[..., V_DIM:])], axis=-1)
    elif n == 'ssm_w_glu':
        out = gr['ssm_w_glu'].reshape(N_DEV, SSM_WIDTH // N_DEV, SSM_WIDTH)
    elif n in ('w_out', 'w_xq', 'w_xo'):
        out = gr[n].reshape(N_DEV, D_MODEL // N_DEV, D_MODEL)
    elif n in ('w_gate', 'w_up'):
        out = jnp.transpose(gr[n].reshape(D_MODEL, N_DEV, D_FF // N_DEV), (1, 0, 2))
    elif n == 'w_down':
        out = gr[n].reshape(N_DEV, D_FF // N_DEV, D_MODEL)
    else:
        out = gr[n]
    return out.astype(MXU)


def kernel(x, mem, positions, norm_mix_g, w_in, q_norm_g, w_uq, kv_norm_g, w_ukv, ssm_lambda_re, ssm_lambda_im, ssm_log_dt, ssm_b_re, ssm_b_im, ssm_c_re, ssm_c_im, ssm_d, ssm_w_glu, ssm_b_glu, attn_out_g, ssm_out_g, w_out, norm_x_g, mem_norm_g, w_xq, w_xkv, w_xo, norm_ffn_g, w_gate, w_up, w_down, final_norm_g, loss_target, m_norm_mix_g, m_w_in, m_q_norm_g, m_w_uq, m_kv_norm_g, m_w_ukv, m_ssm_lambda_re, m_ssm_lambda_im, m_ssm_log_dt, m_ssm_b_re, m_ssm_b_im, m_ssm_c_re, m_ssm_c_im, m_ssm_d, m_ssm_w_glu, m_ssm_b_glu, m_attn_out_g, m_ssm_out_g, m_w_out, m_norm_x_g, m_mem_norm_g, m_w_xq, m_w_xkv, m_w_xo, m_norm_ffn_g, m_w_gate, m_w_up, m_w_down, m_final_norm_g, v_norm_mix_g, v_w_in, v_q_norm_g, v_w_uq, v_kv_norm_g, v_w_ukv, v_ssm_lambda_re, v_ssm_lambda_im, v_ssm_log_dt, v_ssm_b_re, v_ssm_b_im, v_ssm_c_re, v_ssm_c_im, v_ssm_d, v_ssm_w_glu, v_ssm_b_glu, v_attn_out_g, v_ssm_out_g, v_w_out, v_norm_x_g, v_mem_norm_g, v_w_xq, v_w_xkv, v_w_xo, v_norm_ffn_g, v_w_gate, v_w_up, v_w_down, v_final_norm_g):
    args = dict(locals())
    W = {n: args[n] for n in WEIGHTS}
    M = {n: args['m_' + n] for n in WEIGHTS}
    V = {n: args['v_' + n] for n in WEIGHTS}
    s = x.shape[1]
    h = x[0]
    memx = mem[0]

    freqs = ROPE_THETA ** (-jnp.arange(0, QK_ROPE, 2, dtype=F32) / QK_ROPE)
    ang = positions[0].astype(F32)[:, None] * freqs
    c16, s16 = jnp.cos(ang), jnp.sin(ang)
    cos = jnp.concatenate([jnp.ones((s, QK_NOPE), F32), c16, c16, jnp.zeros((s, 32), F32)], axis=1)
    sin = jnp.concatenate([jnp.zeros((s, QK_NOPE), F32), s16, s16, jnp.zeros((s, 32), F32)], axis=1)
    idx = jnp.arange(QK_ROPE // 2)
    pmat = jnp.zeros((HEAD_W, HEAD_W), F32)
    pmat = pmat.at[QK_NOPE + 16 + idx, QK_NOPE + idx].set(-1.0).at[QK_NOPE + idx, QK_NOPE + 16 + idx].set(1.0)
    tabs = (cos, sin, pmat, pmat.T)

    shards = [{n: W[n][l].astype(MXU) for n in SHARDED} for l in range(DEPTH)]
    gathered = dict(zip(EARLY, _run_exchange("gather_weights", _gather(_named(EARLY, shards[0])))))

    layers = []
    for l in range(DEPTH):
        wl = _layer_weights(gathered)
        s5_in = [W[n][l] for n in ('ssm_lambda_re', 'ssm_lambda_im', 'ssm_log_dt', 'ssm_b_re', 'ssm_b_im',
                                   'ssm_c_re', 'ssm_c_im')]
        (a_re, a_im, bre, bim, cre, cim), s5_vjp = jax.vjp(_s5_params, *s5_in)
        pl_ = {n: W[n][l][None] for n in ('norm_mix_g', 'q_norm_g', 'kv_norm_g', 'ssm_d', 'ssm_b_glu',
                                           'attn_out_g', 'ssm_out_g', 'norm_x_g', 'mem_norm_g', 'norm_ffn_g')}
        pl_.update(a_re=a_re, a_im=a_im, b_re=bre, b_im=bim, c_re=cre, c_im=cim)
        h, sv, wl, gathered = _layer_fwd(h, memx, tabs, wl, pl_, shards[0] if l == 0 else None,
                                         shards[l + 1] if l + 1 < DEPTH else None)
        layers.append((wl, pl_, sv, s5_vjp))

    def f_loss(h_, tgt, g):
        y, r = _rms(h_, g[...])
        err = y - tgt
        part = 0.5 * jnp.sum(jnp.mean(err * err, axis=-1, keepdims=True), axis=0, keepdims=True)
        dh, dg = _rms_bwd(h_, g[...], r, err / D_MODEL)
        return dh, dg, jnp.broadcast_to(part, (8, LANES))
    dh, g_final, loss_part = _rows(
        "loss_head", f_loss, s, min(256, s), [(h, 'r0'), (loss_target[0], 'r0'), (final_norm_g[None], 'f')],
        [((s, D_MODEL), F32, 'r0'), ((1, D_MODEL), F32, 'a'), ((8, LANES), F32, 'a')])
    loss = lax.psum(loss_part[0, 0], ("x", "y", "c"))

    parts = [{} for _ in range(DEPTH)]
    g_rep = [None] * DEPTH
    blocks = None
    for l in reversed(range(DEPTH)):
        wl, pl_, sv, s5_vjp = layers[l]
        dh, gr, arrived = _layer_bwd(dh, sv, memx, tabs, wl, pl_, blocks)
        for (who, n), p in arrived.items():
            parts[l + 1 if who == 'nxt' else l][n] = p
        blocks = {n: _blocked(gr, n) for n in EARLY}
        ds5 = s5_vjp((gr['a_re'], gr['a_im'], gr['b_re'], gr['b_im'], gr['c_re'], gr['c_im']))
        rep = dict(zip(('ssm_lambda_re', 'ssm_lambda_im', 'ssm_log_dt', 'ssm_b_re', 'ssm_b_im', 'ssm_c_re',
                        'ssm_c_im'), ds5))
        for n in ('norm_mix_g', 'q_norm_g', 'kv_norm_g', 'ssm_d', 'ssm_b_glu', 'attn_out_g', 'ssm_out_g',
                  'norm_x_g', 'mem_norm_g', 'norm_ffn_g'):
            rep[n] = gr[n][0]
        g_rep[l] = rep
    grad_x = dh[None]

    rep_names = REPL_L + ['final_norm_g']
    g_loc = {n: jnp.stack([g_rep[l][n] for l in range(DEPTH)]) for n in REPL_L}
    g_loc['final_norm_g'] = g_final
    rest = [n for n in SHARDED if n not in parts[0]]
    last = _run_exchange("last_grads", _together(_scatter(_named(rest, blocks)),
                                                 _gather([_pack(_named(rep_names, g_loc))])))
    parts[0].update(zip(rest, last[:len(rest)]))

    out_sh = [{}, {}, {}, {}]
    for n in SHARDED:
        res = _adamw_weight("adamw_" + n, [parts[l][n] for l in range(DEPTH)], W[n], M[n], V[n])
        for kind, r in enumerate(res):
            out_sh[kind][n] = r

    shapes_rp = [(1,) + W[n].shape if W[n].ndim == 1 else W[n].shape for n in rep_names]
    g_rp = _unpack(_sum_sources("sum_small_grads", last[len(rest)]), shapes_rp)
    as_rows = lambda d: [d[n].reshape(shp) for n, shp in zip(rep_names, shapes_rp)]
    res_rp = (g_rp,) + _adamw_small("adamw_replicated", g_rp, as_rows(W), as_rows(M), as_rows(V))
    out_rp = [{n: a.reshape(W[n].shape) for n, a in zip(rep_names, r)} for r in res_rp]

    outs = [loss, grad_x]
    for kind in range(4):
        for n in WEIGHTS:
            outs.append(out_sh[kind][n] if n in SHARDED else out_rp[kind][n])
    return tuple(outs)
```

```python
from typing import Callable, NamedTuple

import jax
import jax.numpy as jnp
from jax import lax
from jax.experimental import pallas as pl
from jax.experimental.pallas import tpu as pltpu

F32 = jnp.float32
MXU = jnp.bfloat16
HI = lax.Precision.HIGHEST

D_MODEL = 1024
MLA_HEADS = 8
QK_NOPE = 64
QK_ROPE = 32
V_DIM = 64
Q_LORA = 256
KV_LORA = 128
SSM_WIDTH = 512
SSM_GROUPS = 32
SSM_GROUP = 16
SSM_STATE = 64
X_HEADS = 4
X_HEAD_DIM = 256
D_FF = 2816
FF_CHUNK = D_FF // 2
ROPE_THETA = 10000.0
EPS = 1e-6
DEPTH = 2
N_DEV = 8
LANES = 128
HEAD_W = 128
MLA_SCALE = (QK_NOPE + QK_ROPE) ** -0.5
X_SCALE = X_HEAD_DIM ** -0.5
ADAM_LR, ADAM_B1, ADAM_B2, ADAM_EPS, ADAM_WD, ADAM_STEP = 0.001, 0.9, 0.999, 1e-08, 0.01, 10
VMEM_LIMIT = 56 * 1024 * 1024
FLASH_TILE = 512
DW_ROWS = 2048
X_ROWS = 512
MESH = pl.DeviceIdType.MESH

SHARDED = ['w_in', 'w_uq', 'w_ukv', 'ssm_w_glu', 'w_out', 'w_xq', 'w_xkv', 'w_xo', 'w_gate', 'w_up', 'w_down']
REPL_L = ['norm_mix_g', 'q_norm_g', 'kv_norm_g', 'ssm_lambda_re', 'ssm_lambda_im', 'ssm_log_dt', 'ssm_b_re',
          'ssm_b_im', 'ssm_c_re', 'ssm_c_im', 'ssm_d', 'ssm_b_glu', 'attn_out_g', 'ssm_out_g', 'norm_x_g',
          'mem_norm_g', 'norm_ffn_g']
WEIGHTS = ['norm_mix_g', 'w_in', 'q_norm_g', 'w_uq', 'kv_norm_g', 'w_ukv', 'ssm_lambda_re', 'ssm_lambda_im',
           'ssm_log_dt', 'ssm_b_re', 'ssm_b_im', 'ssm_c_re', 'ssm_c_im', 'ssm_d', 'ssm_w_glu', 'ssm_b_glu',
           'attn_out_g', 'ssm_out_g', 'w_out', 'norm_x_g', 'mem_norm_g', 'w_xq', 'w_xkv', 'w_xo', 'norm_ffn_g',
           'w_gate', 'w_up', 'w_down', 'final_norm_g']


def _pcall(body, **kw):
    return pl.pallas_call(body, **kw)


def _mm(a, b):
    return jnp.dot(a.astype(MXU), b.astype(MXU), preferred_element_type=F32)


def _mm_nt(a, b):
    return lax.dot_general(a.astype(MXU), b.astype(MXU), (((1,), (1,)), ((), ())), preferred_element_type=F32)


def _mm_tn(a, b):
    return lax.dot_general(a.astype(MXU), b.astype(MXU), (((0,), (0,)), ((), ())), preferred_element_type=F32)


def _mm_hi(a, b):
    return jnp.dot(a.astype(F32), b.astype(F32), precision=HI, preferred_element_type=F32)


def _rms(x, g):
    r = lax.rsqrt(jnp.mean(x * x, axis=-1, keepdims=True) + EPS)
    return x * r * g, r


def _rms_bwd(x, g, r, dy):
    dyg = dy * g
    dx = r * dyg - x * (r * r * r) * jnp.mean(dyg * x, axis=-1, keepdims=True)
    return dx, jnp.sum(dy * x * r, axis=0, keepdims=True)


def _rope(x, cos, sin, p_ref):
    return x * cos + _mm_hi(x, p_ref[...]) * sin


def _rope_t(g, cos, sin, pt_ref):
    return g * cos + _mm_hi(g * sin, pt_ref[...])


def _softmax(s):
    m = jnp.max(s, axis=-1, keepdims=True)
    e = jnp.exp(s - m)
    return e / jnp.sum(e, axis=-1, keepdims=True)


def _lanes(x, j, w):
    return x[:, j * w:(j + 1) * w]


def _rows(name, fn, n, tm, ins, outs, side=None):
    def spec(shape, kind):
        nd = len(shape)
        if kind == 'p8':
            return pl.BlockSpec((shape[0], 8, shape[2]), lambda i: (0, jnp.maximum(i * (tm // 8) - 1, 0), 0))
        if kind == 'f':
            return pl.BlockSpec(shape, lambda i, _nd=nd: (0,) * _nd, pipeline_mode=pl.Buffered(1))
        if kind == 'a':
            return pl.BlockSpec(shape, lambda i, _nd=nd: (0,) * _nd)
        ax = int(kind[1])
        blk = tuple(tm if d == ax else s for d, s in enumerate(shape))
        return pl.BlockSpec(blk, lambda i, _ax=ax, _nd=nd: tuple(i if d == _ax else 0 for d in range(_nd)))

    n_in, n_out, n_steps = len(ins), len(outs), n // tm

    def body(*refs):
        in_refs, out_refs, steps = _side_split(refs, n_in, n_out, side)
        i = pl.program_id(0)
        if steps:
            pl.when(i == 0)(steps[0])
            pl.when(i == _pass_on_step(n_steps))(steps[1])
        args = [r if k == 'f' else r[...] for r, (_, k) in zip(in_refs, ins)]
        res = fn(*args)
        for r, (_, dt, k), v in zip(out_refs, outs, res):
            if k == 'a':
                _accumulate(r, v.astype(dt), i)
            else:
                r[...] = v.astype(dt)
        if steps:
            pl.when(i == n_steps - 1)(steps[2])

    s_in, s_out, s_shape, s_sems, s_ops = _side_args(side)
    res = _pcall(
        body, name=name + ("_x" if side else ""), grid=(n_steps,),
        in_specs=[spec(a.shape, k) for a, k in ins] + s_in,
        out_specs=[spec(s, k) for s, _, k in outs] + s_out,
        out_shape=[jax.ShapeDtypeStruct(s, dt) for s, dt, _ in outs] + s_shape,
        scratch_shapes=s_sems,
        compiler_params=pltpu.CompilerParams(dimension_semantics=("arbitrary",), vmem_limit_bytes=VMEM_LIMIT),
    )(*[a for a, _ in ins], *s_ops)
    return _Hosted(res[:n_out], res[n_out:]) if side else res


def _accumulate(ref, v, i):
    @pl.when(i == 0)
    def _():
        ref[...] = v

    @pl.when(i != 0)
    def _():
        ref[...] += v


def _mm_tn_call(name, a, b, tk=None, tn=None):
    out_dtype = MXU
    s, k = a.shape
    n = b.shape[1]
    tk, tn = tk or k, tn or n
    ts = min(DW_ROWS, s)
    ns = s // ts

    def body(a_ref, b_ref, o_ref, acc_ref):
        j = pl.program_id(2)
        _accumulate(acc_ref, _mm_tn(a_ref[...], b_ref[...]), j)

        @pl.when(j == ns - 1)
        def _():
            o_ref[...] = acc_ref[...].astype(out_dtype)

    return _pcall(
        body, name=name, grid=(k // tk, n // tn, ns),
        in_specs=[pl.BlockSpec((ts, tk), lambda ik, jn, j: (j, ik)),
                  pl.BlockSpec((ts, tn), lambda ik, jn, j: (j, jn))],
        out_specs=pl.BlockSpec((tk, tn), lambda ik, jn, j: (ik, jn)),
        out_shape=jax.ShapeDtypeStruct((k, n), out_dtype),
        scratch_shapes=[pltpu.VMEM((tk, tn), F32)],
        compiler_params=pltpu.CompilerParams(dimension_semantics=("arbitrary", "arbitrary", "arbitrary"),
                                             vmem_limit_bytes=VMEM_LIMIT),
    )(a, b)


def _side_split(refs, n_in, n_out, side):
    if side is None:
        return refs[:n_in], refs[n_in:n_in + n_out], None
    si, so = len(side.ins), len(side.out_shapes)
    own_in, side_in = refs[:n_in], refs[n_in:n_in + si]
    own_out, side_out = refs[n_in + si:n_in + si + n_out], refs[n_in + si + n_out:n_in + si + n_out + so]
    return own_in, own_out, side.steps(side_in, side_out, refs[n_in + si + n_out + so:])


def _pass_on_step(n_steps):
    return max(n_steps - 2, 0)


def _side_args(side):
    if side is None:
        return [], [], [], [], []
    any_spec = pl.BlockSpec(memory_space=pl.ANY)
    return ([any_spec] * len(side.ins), [any_spec] * len(side.out_shapes), list(side.out_shapes),
            list(side.sem_shapes), list(side.ins))


class _Hosted(NamedTuple):
    results: list
    arrived: list


def _hosted(res):
    return res if isinstance(res, _Hosted) else _Hosted(res, ())


def _flash_fwd(q, k, v, side=None):
    nh, s, w = q.shape
    t = min(FLASH_TILE, s)
    nq = s // t
    n_steps = (nh // 2) * nq

    def body(*refs):
        (q_ref, k_ref, v_ref), (o_ref, lse_ref), steps = _side_split(refs, 3, 2, side)
        step = pl.program_id(0) * nq + pl.program_id(1)
        if steps:
            pl.when(step == 0)(steps[0])
            pl.when(step == _pass_on_step(n_steps))(steps[1])
        qi = pl.program_id(1)
        qs = [q_ref[0], q_ref[1]]
        below = lax.broadcasted_iota(jnp.int32, (t, t), 1) <= lax.broadcasted_iota(jnp.int32, (t, t), 0)

        def tile(j, carry, diagonal):
            sl = pl.ds(pl.multiple_of(j * t, t), t)
            out = []
            for hh in range(2):
                m, l, acc = carry[3 * hh:3 * hh + 3]
                sc = _mm_nt(qs[hh], k_ref[hh, sl, :])
                if diagonal:
                    sc = jnp.where(below, sc, -1e30)
                m_new = jnp.maximum(m, jnp.max(sc, axis=1, keepdims=True))
                p = jnp.exp(sc - m_new)
                alpha = jnp.exp(m - m_new)
                out += [m_new, alpha * l + jnp.sum(p, axis=1, keepdims=True), alpha * acc + _mm(p, v_ref[hh, sl, :])]
            return tuple(out)

        init = (jnp.full((t, 1), -1e30, F32), jnp.zeros((t, 1), F32), jnp.zeros((t, w), F32)) * 2
        carry = lax.fori_loop(0, qi, lambda j, c: tile(j, c, False), init)
        carry = tile(qi, carry, True)
        o_ref[...] = carry[2] / carry[1] + carry[5] / carry[4]
        for hh in range(2):
            lse_ref[hh] = jnp.broadcast_to(carry[3 * hh] + jnp.log(carry[3 * hh + 1]), (t, w))
        if steps:
            pl.when(step == n_steps - 1)(steps[2])

    s_in, s_out, s_shape, s_sems, s_ops = _side_args(side)
    res = _pcall(
        body, name="mla_flash_fwd" + ("_x" if side else ""), grid=(nh // 2, nq),
        in_specs=[pl.BlockSpec((2, t, w), lambda p, i: (p, i, 0)),
                  pl.BlockSpec((2, s, w), lambda p, i: (p, 0, 0)),
                  pl.BlockSpec((2, s, w), lambda p, i: (p, 0, 0))] + s_in,
        out_specs=[pl.BlockSpec((t, w), lambda p, i: (i, p)),
                   pl.BlockSpec((2, t, w), lambda p, i: (p, i, 0))] + s_out,
        out_shape=[jax.ShapeDtypeStruct((s, (nh // 2) * w), F32), jax.ShapeDtypeStruct((nh, s, w), F32)] + s_shape,
        scratch_shapes=s_sems,
        compiler_params=pltpu.CompilerParams(dimension_semantics=("arbitrary", "arbitrary"),
                                             vmem_limit_bytes=VMEM_LIMIT),
    )(q, k, v, *s_ops)
    return res[0], res[1], res[2:]


def _flash_bwd(q, k, v, o, lse, do, side=None):
    nh, s, w = q.shape
    t = min(FLASH_TILE, s)
    nq = s // t
    n_steps = (nh // 2) * nq

    def body(*refs):
        (q_ref, k_ref, v_ref, o_ref, lse_ref, do_ref), (dq_ref, dk_ref, dv_ref), steps = _side_split(refs, 6, 3, side)
        step = pl.program_id(0) * nq + pl.program_id(1)
        if steps:
            pl.when(step == 0)(steps[0])
            pl.when(step == _pass_on_step(n_steps))(steps[1])
        j = pl.program_id(1)

        @pl.when(j == 0)
        def _():
            dq_ref[...] = jnp.zeros(dq_ref.shape, F32)

        below = lax.broadcasted_iota(jnp.int32, (t, t), 1) <= lax.broadcasted_iota(jnp.int32, (t, t), 0)
        lane = lax.broadcasted_iota(jnp.int32, (t, w), 1)
        heads = [jnp.logical_and(lane >= hh * V_DIM, lane < (hh + 1) * V_DIM) for hh in range(2)]
        ks = [k_ref[0], k_ref[1]]
        vs = [v_ref[0], v_ref[1]]

        def tile(i, carry, diagonal):
            sl = pl.ds(pl.multiple_of(i * t, t), t)
            dout_all, o_all = do_ref[sl, :], o_ref[sl, :]
            out = []
            for hh in range(2):
                dk, dv = carry[2 * hh], carry[2 * hh + 1]
                qh = q_ref[hh, sl, :]
                dout = jnp.where(heads[hh], dout_all, 0.0)
                sc = _mm_nt(qh, ks[hh])
                if diagonal:
                    sc = jnp.where(below, sc, -1e30)
                p = jnp.exp(sc - lse_ref[hh, sl, 0:1])
                dp = _mm_nt(dout, vs[hh])
                ds = p * (dp - jnp.sum(dout * o_all, axis=1, keepdims=True))
                dq_ref[hh, sl, :] += _mm(ds, ks[hh])
                out += [dk + _mm_tn(ds, qh), dv + _mm_tn(p, dout)]
            return tuple(out)

        carry = tile(j, (jnp.zeros((t, w), F32),) * 4, True)
        carry = lax.fori_loop(j + 1, nq, lambda i, c: tile(i, c, False), carry)
        for hh in range(2):
            dk_ref[hh] = carry[2 * hh]
            dv_ref[hh] = jnp.where(heads[hh], carry[2 * hh + 1], 0.0)
        if steps:
            pl.when(step == n_steps - 1)(steps[2])

    s_in, s_out, s_shape, s_sems, s_ops = _side_args(side)
    res = _pcall(
        body, name="mla_flash_bwd" + ("_x" if side else ""), grid=(nh // 2, nq),
        in_specs=[pl.BlockSpec((2, s, w), lambda p, j: (p, 0, 0)),
                  pl.BlockSpec((2, t, w), lambda p, j: (p, j, 0)),
                  pl.BlockSpec((2, t, w), lambda p, j: (p, j, 0)),
                  pl.BlockSpec((s, w), lambda p, j: (0, p)),
                  pl.BlockSpec((2, s, w), lambda p, j: (p, 0, 0)),
                  pl.BlockSpec((s, w), lambda p, j: (0, p))] + s_in,
        out_specs=[pl.BlockSpec((2, s, w), lambda p, j: (p, 0, 0)),
                   pl.BlockSpec((2, t, w), lambda p, j: (p, j, 0)),
                   pl.BlockSpec((2, t, w), lambda p, j: (p, j, 0))] + s_out,
        out_shape=[jax.ShapeDtypeStruct((nh, s, w), F32)] * 3 + s_shape,
        scratch_shapes=s_sems,
        compiler_params=pltpu.CompilerParams(dimension_semantics=("arbitrary", "arbitrary"),
                                             vmem_limit_bytes=VMEM_LIMIT),
    )(q, k, v, o, lse, do, *s_ops)
    return res[0], res[1], res[2], res[3:]


def _scan(src, w_re, w_im, a_re, a_im, reverse):
    s = src.shape[0]
    nb, w = a_re.shape[0], LANES
    ch = s // 8
    assert ch & (ch - 1) == 0
    grp = 4
    tr = min(512, s)

    def cmul(ar, ai, xr, xi):
        return ar * xr - ai * xi, ar * xi + ai * xr

    def body(src_ref, wr_ref, wi_ref, ar_ref, ai_ref, xr_ref, xi_ref):
        def project(c, carry):
            rows = pl.ds(pl.multiple_of(c * tr, tr), tr)
            u = src_ref[rows, :]
            if reverse:
                br, bi = _mm_nt(u, wr_ref[...]), _mm_nt(u, wi_ref[...])
            else:
                br, bi = _mm(u, wr_ref[...]), _mm(u, wi_ref[...])
            for g in range(grp):
                xr_ref[g, rows, :] = _lanes(br, g, w)
                xi_ref[g, rows, :] = _lanes(bi, g, w)
            return carry

        lax.fori_loop(0, s // tr, project, 0)
        sub = lax.broadcasted_iota(jnp.int32, (8, w), 0)

        def shift(x, k):
            if reverse:
                return jnp.where(sub < 8 - k, pltpu.roll(x, 8 - k, 0), 0.0)
            return jnp.where(sub >= k, pltpu.roll(x, k, 0), 0.0)

        ar = [jnp.broadcast_to(ar_ref[g], (8, w)) for g in range(grp)]
        ai = [jnp.broadcast_to(ai_ref[g], (8, w)) for g in range(grp)]

        def tsl(i):
            return pl.ds(pl.multiple_of(((ch - 1 - i) if reverse else i) * 8, 8), 8)

        def local(i, carry):
            out = []
            for g in range(grp):
                xr, xi = carry[2 * g], carry[2 * g + 1]
                pr, pi = cmul(ar[g], ai[g], xr, xi)
                nr = pr + xr_ref[g, tsl(i), :]
                ni = pi + xi_ref[g, tsl(i), :]
                xr_ref[g, tsl(i), :] = nr
                xi_ref[g, tsl(i), :] = ni
                out += [nr, ni]
            return tuple(out)

        fin = lax.fori_loop(0, ch, local, (jnp.zeros((8, w), F32),) * (2 * grp))

        carry_in = []
        for g in range(grp):
            pr, pi = ar[g], ai[g]
            for _ in range(ch.bit_length() - 1):
                pr, pi = cmul(pr, pi, pr, pi)
            fr, fi = fin[2 * g], fin[2 * g + 1]
            for kk in (1, 2, 4):
                sr, si = cmul(pr, pi, shift(fr, kk), shift(fi, kk))
                fr, fi = fr + sr, fi + si
                pr, pi = cmul(pr, pi, pr, pi)
            carry_in += [shift(fr, 1), shift(fi, 1)]

        def fix(i, pw):
            out = []
            for g in range(grp):
                pr, pi = pw[2 * g], pw[2 * g + 1]
                cr, ci = cmul(pr, pi, carry_in[2 * g], carry_in[2 * g + 1])
                xr_ref[g, tsl(i), :] = xr_ref[g, tsl(i), :] + cr
                xi_ref[g, tsl(i), :] = xi_ref[g, tsl(i), :] + ci
                nr, ni = cmul(pr, pi, ar[g], ai[g])
                out += [nr, ni]
            return tuple(out)

        lax.fori_loop(0, ch, fix, tuple(x for g in range(grp) for x in (ar[g], ai[g])))

    per_j = 4 // grp
    blk = pl.BlockSpec((grp, s, w), lambda i: (i, 0, 0))
    ablk = pl.BlockSpec((grp, 1, w), lambda i: (i, 0, 0))
    sblk = pl.BlockSpec((s, w), lambda i: (0, i // per_j))
    if reverse:
        wblk = pl.BlockSpec((None, grp * w, w), lambda i: (i // per_j, i % per_j, 0))
    else:
        wblk = pl.BlockSpec((None, w, grp * w), lambda i: (i // per_j, 0, i % per_j))
    return _pcall(
        body, name="s5_scan_rev" if reverse else "s5_scan", grid=(nb // grp,),
        in_specs=[sblk, wblk, wblk, ablk, ablk], out_specs=[blk, blk],
        out_shape=[jax.ShapeDtypeStruct((nb, s, w), F32)] * 2,
        compiler_params=pltpu.CompilerParams(dimension_semantics=("arbitrary",), vmem_limit_bytes=VMEM_LIMIT),
    )(src, w_re, w_im, a_re, a_im)


class _Exchange(NamedTuple):
    ins: list
    out_shapes: list
    sem_shapes: list
    steps: Callable


def _gather_steps(ins, outs, sems):
    n = len(ins)
    send_sems, recv_sems, local_sems = sems
    x, y, c = lax.axis_index("x"), lax.axis_index("y"), lax.axis_index("c")
    me, sibling = (x, y, c), (x, y, 1 - c)
    chips = [(1 - x, y), (x, 1 - y), (1 - x, 1 - y)]

    def copy(a, k, block, to, src=None):
        dst = outs[a].at[4 * block[0] + 2 * block[1] + block[2]]
        return pltpu.make_async_remote_copy(
            src_ref=dst if src is None else src, dst_ref=dst,
            send_sem=send_sems.at[a, k], recv_sem=recv_sems.at[a, k], device_id=to, device_id_type=MESH)

    mine = [pltpu.make_async_copy(ins[a], outs[a].at[4 * x + 2 * y + c], local_sems.at[a]) for a in range(n)]
    first = []
    for a in range(n):
        first.append(copy(a, 0, me, sibling, src=ins[a]))
        first += [copy(a, 1 + j, me, (*chip, c), src=ins[a]) for j, chip in enumerate(chips)]
    passed = [copy(a, 4 + j, (*chip, c), sibling) for j, chip in enumerate(chips) for a in range(n)]

    def start():
        for cp in mine + first:
            cp.start()

    def pass_on():
        i = 0
        for j, chip in enumerate(chips):
            for a in range(n):
                copy(a, 1 + j, (*chip, c), me).wait_recv()
                passed[i].start()
                i += 1

    def finish():
        for a in range(n):
            copy(a, 0, sibling, me).wait_recv()
            for j, chip in enumerate(chips):
                copy(a, 4 + j, (*chip, 1 - c), me).wait_recv()
        for cp in first + passed:
            cp.wait_send()
        for cp in mine:
            cp.wait()

    return start, pass_on, finish


def _gather(arrs):
    n = len(arrs)
    return _Exchange(list(arrs), [jax.ShapeDtypeStruct((N_DEV,) + a.shape, a.dtype) for a in arrs],
                     [pltpu.SemaphoreType.DMA((n, 7)), pltpu.SemaphoreType.DMA((n, 7)), pltpu.SemaphoreType.DMA((n,))],
                     _gather_steps)


def _scatter_steps(ins, outs, sems):
    n = len(ins)
    send_sems, recv_sems, local_sems = sems
    x, y, c = lax.axis_index("x"), lax.axis_index("y"), lax.axis_index("c")
    me = 4 * x + 2 * y + c
    own, sent, arrivals = [], [], []
    for a in range(n):
        own.append(pltpu.make_async_copy(ins[a].at[me], outs[a].at[me], local_sems.at[a]))
        for k in range(1, N_DEV):
            px, py, pc = x ^ ((k >> 2) & 1), y ^ ((k >> 1) & 1), c ^ (k & 1)
            peer = 4 * px + 2 * py + pc
            sent.append(pltpu.make_async_remote_copy(
                src_ref=ins[a].at[peer], dst_ref=outs[a].at[me],
                send_sem=send_sems.at[a, k - 1], recv_sem=recv_sems.at[a, k - 1],
                device_id=(px, py, pc), device_id_type=MESH))
            arrivals.append(pltpu.make_async_remote_copy(
                src_ref=ins[a].at[me], dst_ref=outs[a].at[peer],
                send_sem=send_sems.at[a, k - 1], recv_sem=recv_sems.at[a, k - 1],
                device_id=(x, y, c), device_id_type=MESH))

    def start():
        for cp in own + sent:
            cp.start()

    def pass_on():
        pass

    def finish():
        for cp in arrivals:
            cp.wait_recv()
        for cp in sent:
            cp.wait_send()
        for cp in own:
            cp.wait()

    return start, pass_on, finish


def _scatter(grads):
    n = len(grads)
    return _Exchange(list(grads), [jax.ShapeDtypeStruct(g.shape, g.dtype) for g in grads],
                     [pltpu.SemaphoreType.DMA((n, N_DEV - 1)), pltpu.SemaphoreType.DMA((n, N_DEV - 1)),
                      pltpu.SemaphoreType.DMA((n,))], _scatter_steps)


def _together(a, b):
    def steps(ins, outs, sems):
        sa = a.steps(ins[:len(a.ins)], outs[:len(a.out_shapes)], sems[:len(a.sem_shapes)])
        sb = b.steps(ins[len(a.ins):], outs[len(a.out_shapes):], sems[len(a.sem_shapes):])

        def both(k):
            def run():
                sa[k]()
                sb[k]()
            return run
        return both(0), both(1), both(2)

    return _Exchange(a.ins + b.ins, a.out_shapes + b.out_shapes, a.sem_shapes + b.sem_shapes, steps)


def _run_exchange(name, ex):
    n_in, n_out = len(ex.ins), len(ex.out_shapes)

    def body(*refs):
        for step in ex.steps(refs[:n_in], refs[n_in:n_in + n_out], refs[n_in + n_out:]):
            step()

    any_spec = pl.BlockSpec(memory_space=pl.ANY)
    return _pcall(body, name=name, in_specs=[any_spec] * n_in, out_specs=[any_spec] * n_out,
                  out_shape=list(ex.out_shapes), scratch_shapes=list(ex.sem_shapes))(*ex.ins)


def _adam_math(g, w_, m_, v_):
    m_new = ADAM_B1 * m_ + (1.0 - ADAM_B1) * g
    v_new = ADAM_B2 * v_ + (1.0 - ADAM_B2) * (g * g)
    m_hat = m_new / (1.0 - ADAM_B1 ** ADAM_STEP)
    v_hat = v_new / (1.0 - ADAM_B2 ** ADAM_STEP)
    delta = -ADAM_LR * (m_hat / (jnp.sqrt(v_hat) + ADAM_EPS) + ADAM_WD * w_)
    return delta, m_new, v_new


def _adamw_weight(name, parts, w, m, v):
    nl = len(parts)

    def body(*refs):
        p_refs = refs[:nl]
        w_ref, m_ref, v_ref, g_ref, d_ref, mo_ref, vo_ref = refs[nl:]
        for l in range(nl):
            g = p_refs[l][0].astype(F32)
            for j in range(1, N_DEV):
                g = g + p_refs[l][j].astype(F32)
            g_ref[l] = g
            d_ref[l], mo_ref[l], vo_ref[l] = _adam_math(g, w_ref[l], m_ref[l], v_ref[l])

    return _pcall(
        body, name=name, out_shape=[jax.ShapeDtypeStruct(w.shape, F32)] * 4,
        compiler_params=pltpu.CompilerParams(vmem_limit_bytes=VMEM_LIMIT),
    )(*parts, w, m, v)

def _sum_sources(name, parts):
    r = parts.shape[1]

    def body(p_ref, g_ref):
        g = p_ref[0]
        for j in range(1, N_DEV):
            g = g + p_ref[j]
        g_ref[...] = g

    return _pcall(body, name=name, out_shape=jax.ShapeDtypeStruct((r, LANES), F32),
                  compiler_params=pltpu.CompilerParams(vmem_limit_bytes=VMEM_LIMIT))(parts)


def _adamw_small(name, g, w, m, v):
    n = len(g)

    def body(*refs):
        g_r, w_r, m_r, v_r = (refs[k * n:(k + 1) * n] for k in range(4))
        d_r, mo_r, vo_r = (refs[k * n:(k + 1) * n] for k in range(4, 7))
        for i in range(n):
            d_r[i][...], mo_r[i][...], vo_r[i][...] = _adam_math(g_r[i][...], w_r[i][...], m_r[i][...], v_r[i][...])

    res = _pcall(body, name=name, out_shape=[jax.ShapeDtypeStruct(a.shape, F32) for a in w] * 3,
                 compiler_params=pltpu.CompilerParams(vmem_limit_bytes=VMEM_LIMIT))(*g, *w, *m, *v)
    return res[:n], res[n:2 * n], res[2 * n:]


def _pack(arrs):
    flat = jnp.concatenate([a.reshape(-1) for a in arrs])
    flat = jnp.pad(flat, (0, (-flat.shape[0]) % (8 * LANES)))
    return flat.reshape(-1, LANES)


def _unpack(packed, shapes):
    flat = packed.reshape(-1)
    out, off = [], 0
    for shp in shapes:
        size = 1
        for d in shp:
            size *= d
        out.append(flat[off:off + size].reshape(shp))
        off += size
    return out


def _s5_params(lam_re, lam_im, log_dt, b_re, b_im, c_re, c_im):
    dt = jnp.exp(log_dt)[:, None]
    e = jnp.exp(lam_re * dt)
    ang = lam_im * dt
    a_re, a_im = e * jnp.cos(ang), e * jnp.sin(ang)
    nr, ni = a_re - 1.0, a_im
    den = lam_re * lam_re + lam_im * lam_im
    cr = ((nr * lam_re + ni * lam_im) / den)[..., None]
    ci = ((ni * lam_re - nr * lam_im) / den)[..., None]
    bb_re = cr * b_re - ci * b_im
    bb_im = cr * b_im + ci * b_re
    eye = jnp.eye(8, dtype=F32)[None, :, None, :, None]

    def bblk(bb):
        t = jnp.transpose(bb.reshape(4, 8, SSM_STATE, SSM_GROUP), (0, 3, 1, 2))
        return (eye * t[:, None]).reshape(4, 8 * SSM_GROUP, 8 * SSM_STATE)

    def cblk(cc):
        t = jnp.transpose(cc.reshape(4, 8, SSM_GROUP, SSM_STATE), (0, 3, 1, 2))
        return (eye * t[:, None]).reshape(4, 8 * SSM_STATE, 8 * SSM_GROUP)

    nb = SSM_GROUPS * SSM_STATE // LANES
    return (a_re.reshape(nb, 1, LANES), a_im.reshape(nb, 1, LANES), bblk(bb_re), bblk(bb_im),
            cblk(c_re), -cblk(c_im))


def _cat_blocks(x3, j):
    return jnp.concatenate([x3[4 * j + k] for k in range(4)], axis=-1)


def _to_chunks(a):
    s, c = a.shape
    return a.reshape(8, s // 8, c).transpose(1, 0, 2).reshape(s, c)


def _from_chunks(a):
    s, c = a.shape
    return a.reshape(s // 8, 8, c).transpose(1, 0, 2).reshape(s, c)


EARLY = ['w_in', 'w_uq', 'w_ukv']

FWD_PLAN = {
    'mla_qkv': ('nxt', ['w_in', 'w_uq', 'w_ukv', 'ssm_w_glu', 'w_out']),
    'flash': ('late', ['ssm_w_glu', 'w_out', 'w_xq', 'w_xkv', 'w_xo', 'w_gate', 'w_up', 'w_down']),
    's5_out': ('nxt', ['w_down']),
    'xattn': ('nxt', ['w_xq', 'w_xkv', 'w_xo']),
    'ffn': ('nxt', ['w_gate', 'w_up']),
}
BWD_PLAN = {
    'ffn_bwd': ('nxt', EARLY),
    'xattn_bwd': ('own', ['w_down']),
    'flash_bwd': ('own', ['w_gate', 'w_up', 'w_xq', 'w_xkv', 'w_xo']),
    'mix_in_bwd': ('own', ['ssm_w_glu', 'w_out']),
}


def _named(names, d):
    return [d[n] for n in names]


def _layer_fwd(h, memx, tabs, wl, pl_, late=None, nxt=None):
    s = h.shape[0]
    tm = min(256, s)
    cos, sin, pmat, pmat_t = tabs
    sv = {}
    wl = dict(wl)
    nxt_got = {}

    def fetch(host):
        who, names = FWD_PLAN.get(host, (None, []))
        src = late if who == 'late' else nxt if who == 'nxt' else None
        return _gather(_named(names, src)) if src else None

    def landed(host, got):
        who, names = FWD_PLAN.get(host, (None, []))
        if got and who == 'late':
            wl.update(_layer_weights(dict(zip(names, got))))
        elif got:
            nxt_got.update(zip(names, got))

    def f_mix_in(h_, g, w):
        xn, _ = _rms(h_, g[...])
        return (_mm(xn, w[...]),)
    proj, = _rows("mix_in", f_mix_in, s, tm, [(h, 'r0'), (pl_['norm_mix_g'], 'f'), (wl['w_in'], 'f')],
                  [((s, D_MODEL), F32, 'r0')])

    def f_qkv(pr, cos_, sin_, gq, gkv, wq, wk, wv, pm):
        cqn = _rms(pr[:, 0:Q_LORA], gq[...])[0].astype(MXU)
        kvn = _rms(pr[:, Q_LORA:Q_LORA + KV_LORA], gkv[...])[0].astype(MXU)
        krr = _rope(pr[:, 384:512], cos_, sin_, pm)
        qs, ks, vs = [], [], []
        for hd in range(MLA_HEADS):
            qs.append(_rope(_mm(cqn, wq[hd]), cos_, sin_, pm) * MLA_SCALE)
            ks.append(_mm(kvn, wk[hd]) + krr)
            vs.append(_mm(kvn, wv[hd]))
        return jnp.stack(qs), jnp.stack(ks), jnp.stack(vs)
    hshape = (MLA_HEADS, s, HEAD_W)
    (q, k, v), got = _hosted(_rows(
        "mla_qkv", f_qkv, s, tm,
        [(proj, 'r0'), (cos, 'r0'), (sin, 'r0'), (pl_['q_norm_g'], 'f'), (pl_['kv_norm_g'], 'f'),
         (wl['w_uq'], 'f'), (wl['w_k'], 'f'), (wl['w_v'], 'f'), (pmat, 'f')],
        [(hshape, MXU, 'r1')] * 3, fetch('mla_qkv')))
    landed('mla_qkv', got)

    a_out, lse, got = _flash_fwd(q, k, v, fetch('flash'))
    landed('flash', got)

    u_ch = _to_chunks(proj[:, 512:1024])

    x_re, x_im = _scan(u_ch, pl_['b_re'], pl_['b_im'], pl_['a_re'], pl_['a_im'], False)

    def f_s5_out(xr, xi, u, cre, cim, d, wglu, bglu):
        y = jnp.concatenate([_mm(_cat_blocks(xr, j), cre[j]) + _mm(_cat_blocks(xi, j), cim[j])
                             for j in range(4)], axis=-1) + d[...] * u
        z = _mm(jax.nn.gelu(y), wglu[...]) + bglu[...]
        return y, y * jax.nn.sigmoid(z)
    (y_ssm, s_out_ch), got = _hosted(_rows(
        "s5_out", f_s5_out, s, tm,
        [(x_re, 'r1'), (x_im, 'r1'), (u_ch, 'r0'), (pl_['c_re'], 'f'), (pl_['c_im'], 'f'),
         (pl_['ssm_d'], 'f'), (wl['ssm_w_glu'], 'f'), (pl_['ssm_b_glu'], 'f')],
        [((s, SSM_WIDTH), F32, 'r0')] * 2, fetch('s5_out')))
    landed('s5_out', got)
    s_out = _from_chunks(s_out_ch)

    def f_mix_out(h_, a, so, ga, gs, w):
        an = _rms(a, ga[...])[0]
        sn = _rms(so, gs[...])[0]
        return (h_ + _mm(jnp.concatenate([an, sn], axis=-1), w[...]),)
    (h1,), got = _hosted(_rows("mix_out", f_mix_out, s, tm,
                               [(h, 'r0'), (a_out, 'r0'), (s_out, 'r0'), (pl_['attn_out_g'], 'f'),
                                (pl_['ssm_out_g'], 'f'), (wl['w_out'], 'f')],
                               [((s, D_MODEL), F32, 'r0')], fetch('mix_out')))
    landed('mix_out', got)

    m_len = memx.shape[0]

    def f_memkv(mm_, g, w):
        mn = _rms(mm_, g[...])[0].astype(MXU)
        return (jnp.stack([_mm(mn, w[d]) for d in range(N_DEV)]),)
    kvm, = _rows("mem_kv", f_memkv, m_len, m_len, [(memx, 'r0'), (pl_['mem_norm_g'], 'f'), (wl['w_xkv'], 'f')],
                 [((N_DEV, m_len, X_HEAD_DIM), MXU, 'r1')])

    def f_xattn(h_, g, wq, kv_, wo):
        hn = _rms(h_, g[...])[0].astype(MXU)
        out = jnp.zeros(h_.shape, F32)
        for hd in range(X_HEADS):
            cs = pl.ds(hd * X_HEAD_DIM, X_HEAD_DIM)
            qh = _mm(hn, wq[:, cs])
            p = _softmax(_mm_nt(qh, kv_[hd]) * X_SCALE)
            out = out + _mm(_mm(p, kv_[X_HEADS + hd]), wo[cs, :])
        return (h_ + out,)
    (h2,), got = _hosted(_rows("xattn", f_xattn, s, min(X_ROWS, s),
                               [(h1, 'r0'), (pl_['norm_x_g'], 'f'), (wl['w_xq'], 'f'), (kvm, 'f'), (wl['w_xo'], 'f')],
                               [((s, D_MODEL), F32, 'r0')], fetch('xattn')))
    landed('xattn', got)

    def f_ffn(h_, g, wg, wu, wd):
        hn = _rms(h_, g[...])[0].astype(MXU)
        y = jnp.zeros(h_.shape, F32)
        gates, ups = [], []
        for c in range(D_FF // FF_CHUNK):
            cs = pl.ds(c * FF_CHUNK, FF_CHUNK)
            gate, up = _mm(hn, wg[:, cs]), _mm(hn, wu[:, cs])
            y = y + _mm(gate * jax.nn.sigmoid(gate) * up, wd[cs, :])
            gates.append(gate)
            ups.append(up)
        return h_ + y, jnp.concatenate(gates, axis=-1), jnp.concatenate(ups, axis=-1)
    (h3, gate_f, up_f), got = _hosted(_rows(
        "ffn", f_ffn, s, tm,
        [(h2, 'r0'), (pl_['norm_ffn_g'], 'f'), (wl['w_gate'], 'f'), (wl['w_up'], 'f'), (wl['w_down'], 'f')],
        [((s, D_MODEL), F32, 'r0'), ((s, D_FF), MXU, 'r0'), ((s, D_FF), MXU, 'r0')], fetch('ffn')))
    landed('ffn', got)
    sv.update(h=h, proj=proj, q=q, k=k, v=v, a_out=a_out, lse=lse, x_re=x_re, x_im=x_im, y_ssm=y_ssm,
              s_out=s_out, h1=h1, kvm=kvm, h2=h2, u_ch=u_ch, gate=gate_f, up=up_f)
    return h3, sv, wl, nxt_got


def _layer_bwd(dh3, sv, memx, tabs, wl, pl_, nxt=None):
    s = dh3.shape[0]
    tm = min(256, s)
    cos, sin, pmat, pmat_t = tabs
    gr = {}
    arrived = {}
    act_shape = (s, D_FF)

    def send(host):
        who, names = BWD_PLAN.get(host, (None, []))
        if who is None or (who == 'nxt' and not nxt):
            return None, []
        return (_scatter([nxt[n] if who == 'nxt' else _blocked(gr, n) for n in names]),
                [(who, n) for n in names])

    def f_ffn_bwd(h_, dy, gate_, up_, g, wg, wu, wd):
        hn, r = _rms(h_, g[...])
        hb = hn.astype(MXU)
        dyb = dy.astype(MXU)
        dhn = jnp.zeros(h_.shape, F32)
        acts, dgs, dus = [], [], []
        for c in range(D_FF // FF_CHUNK):
            cs = pl.ds(c * FF_CHUNK, FF_CHUNK)
            gate = _lanes(gate_, c, FF_CHUNK).astype(F32)
            up = _lanes(up_, c, FF_CHUNK).astype(F32)
            sg = jax.nn.sigmoid(gate)
            si = gate * sg
            dact = _mm_nt(dyb, wd[cs, :])
            dgate = (dact * up * (sg * (1.0 + gate * (1.0 - sg)))).astype(MXU)
            dup = (dact * si).astype(MXU)
            dhn = dhn + _mm_nt(dgate, wg[:, cs]) + _mm_nt(dup, wu[:, cs])
            acts.append((si * up).astype(MXU))
            dgs.append(dgate)
            dus.append(dup)
        dh, dg = _rms_bwd(h_, g[...], r, dhn)
        cat = lambda parts: jnp.concatenate(parts, axis=-1)
        return dy + dh, hb, cat(acts), cat(dgs), cat(dus), dg
    ex, keys = send('ffn_bwd')
    (dh2, hn_f, act, dgate, dup, gr['norm_ffn_g']), got = _hosted(_rows(
        "ffn_bwd", f_ffn_bwd, s, tm,
        [(sv['h2'], 'r0'), (dh3, 'r0'), (sv['gate'], 'r0'), (sv['up'], 'r0'), (pl_['norm_ffn_g'], 'f'),
         (wl['w_gate'], 'f'), (wl['w_up'], 'f'), (wl['w_down'], 'f')],
        [((s, D_MODEL), F32, 'r0'), ((s, D_MODEL), MXU, 'r0'), (act_shape, MXU, 'r0'), (act_shape, MXU, 'r0'),
         (act_shape, MXU, 'r0'), ((1, D_MODEL), F32, 'a')], ex))
    arrived.update(zip(keys, got))
    gr['w_gate'] = _mm_tn_call("dw_gate", hn_f, dgate, tn=FF_CHUNK)
    gr['w_up'] = _mm_tn_call("dw_up", hn_f, dup, tn=FF_CHUNK)
    gr['w_down'] = _mm_tn_call("dw_down", act, dh3, tk=FF_CHUNK)

    m_len = memx.shape[0]

    def f_xattn_bwd(h_, dy, g, wq, kv_, wo):
        hn, r = _rms(h_, g[...])
        hb = hn.astype(MXU)
        dyb = dy.astype(MXU)
        dhn = jnp.zeros(h_.shape, F32)
        dqs, ohs, dks, dvs = [], [], [], []
        for hd in range(X_HEADS):
            cs = pl.ds(hd * X_HEAD_DIM, X_HEAD_DIM)
            kh, vh = kv_[hd], kv_[X_HEADS + hd]
            qh = _mm(hb, wq[:, cs])
            p = _softmax(_mm_nt(qh, kh) * X_SCALE)
            ohs.append(_mm(p, vh).astype(MXU))
            do = _mm_nt(dyb, wo[cs, :])
            dvs.append(_mm_tn(p, do))
            dp = _mm_nt(do, vh)
            ds = p * (dp - jnp.sum(dp * p, axis=-1, keepdims=True)) * X_SCALE
            dq = _mm(ds, kh).astype(MXU)
            dks.append(_mm_tn(ds, qh))
            dhn = dhn + _mm_nt(dq, wq[:, cs])
            dqs.append(dq)
        dh, dg = _rms_bwd(h_, g[...], r, dhn)
        return (dy + dh, hb, jnp.concatenate(dqs, axis=-1), jnp.concatenate(ohs, axis=-1),
                jnp.stack(dks + dvs), dg)
    ex, keys = send('xattn_bwd')
    (dh1, hn_x, dq_x, oh_x, dkvm, gr['norm_x_g']), got = _hosted(_rows(
        "xattn_bwd", f_xattn_bwd, s, min(X_ROWS, s),
        [(sv['h1'], 'r0'), (dh2, 'r0'), (pl_['norm_x_g'], 'f'), (wl['w_xq'], 'f'), (sv['kvm'], 'f'),
         (wl['w_xo'], 'f')],
        [((s, D_MODEL), F32, 'r0'), ((s, D_MODEL), MXU, 'r0'), ((s, D_MODEL), MXU, 'r0'),
         ((s, D_MODEL), MXU, 'r0'), ((N_DEV, m_len, X_HEAD_DIM), F32, 'a'), ((1, D_MODEL), F32, 'a')], ex))
    arrived.update(zip(keys, got))
    gr['w_xq'] = _mm_tn_call("dw_xq", hn_x, dq_x)
    gr['w_xo'] = _mm_tn_call("dw_xo", oh_x, dh2)

    def f_memkv_bwd(mm_, dkv, g, w):
        mn, r = _rms(mm_, g[...])
        mb = mn.astype(MXU)
        dmn = jnp.zeros(mm_.shape, F32)
        dws = []
        for d in range(N_DEV):
            dmn = dmn + _mm_nt(dkv[d], w[d])
            dws.append(_mm_tn(mb, dkv[d]))
        _, dg = _rms_bwd(mm_, g[...], r, dmn)
        return jnp.stack(dws), dg
    gr['w_xkv'], gr['mem_norm_g'] = _rows(
        "mem_kv_bwd", f_memkv_bwd, m_len, m_len,
        [(memx, 'r0'), (dkvm, 'r1'), (pl_['mem_norm_g'], 'f'), (wl['w_xkv'], 'f')],
        [((N_DEV, D_MODEL, X_HEAD_DIM), F32, 'a'), ((1, D_MODEL), F32, 'a')])

    def f_mix_out_bwd(a, so, dy, ga, gs, w):
        dmix = _mm_nt(dy, w[...])
        an, ra = _rms(a, ga[...])
        sn, rs = _rms(so, gs[...])
        da, dga = _rms_bwd(a, ga[...], ra, dmix[:, 0:512])
        dso, dgs = _rms_bwd(so, gs[...], rs, dmix[:, 512:1024])
        return da, dso, jnp.concatenate([an, sn], axis=-1), dga, dgs
    da_out, ds_out, mixed, gr['attn_out_g'], gr['ssm_out_g'] = _rows(
        "mix_out_bwd", f_mix_out_bwd, s, tm,
        [(sv['a_out'], 'r0'), (sv['s_out'], 'r0'), (dh1, 'r0'), (pl_['attn_out_g'], 'f'), (pl_['ssm_out_g'], 'f'),
         (wl['w_out'], 'f')],
        [((s, 512), F32, 'r0'), ((s, 512), F32, 'r0'), ((s, D_MODEL), MXU, 'r0'), ((1, 512), F32, 'a'),
         ((1, 512), F32, 'a')])
    gr['w_out'] = _mm_tn_call("dw_out", mixed, dh1)

    ex, keys = send('flash_bwd')
    dq, dk, dv, got = _flash_bwd(sv['q'], sv['k'], sv['v'], sv['a_out'], sv['lse'], da_out, ex)
    arrived.update(zip(keys, got))

    def f_s5_out_bwd(xr, xi, u, y, ds, cre, cim, d, wglu, bglu):
        g, gelu_vjp = jax.vjp(jax.nn.gelu, y)
        sig = jax.nn.sigmoid(_mm(g, wglu[...]) + bglu[...])
        dz = ds * y * sig * (1.0 - sig)
        dy = ds * sig + gelu_vjp(_mm_nt(dz, wglu[...]))[0]
        dcr, dci = [], []
        for j in range(4):
            dyj = _lanes(dy, j, LANES)
            dcr.append(_mm_tn(_cat_blocks(xr, j), dyj))
            dci.append(_mm_tn(_cat_blocks(xi, j), dyj))
        return (dy, dy * d[...], jnp.stack(dcr), jnp.stack(dci),
                jnp.sum(dy * u, axis=0, keepdims=True), _mm_tn(g, dz), jnp.sum(dz, axis=0, keepdims=True))
    ex, keys = send('s5_out_bwd')
    (dy_ssm, du_dir, gr['c_re'], gr['c_im'], gr['ssm_d'], gr['ssm_w_glu'], gr['ssm_b_glu']), got = _hosted(_rows(
        "s5_out_bwd", f_s5_out_bwd, s, tm,
        [(sv['x_re'], 'r1'), (sv['x_im'], 'r1'), (sv['u_ch'], 'r0'), (sv['y_ssm'], 'r0'), (_to_chunks(ds_out), 'r0'),
         (pl_['c_re'], 'f'), (pl_['c_im'], 'f'), (pl_['ssm_d'], 'f'), (wl['ssm_w_glu'], 'f'),
         (pl_['ssm_b_glu'], 'f')],
        [((s, 512), F32, 'r0'), ((s, 512), F32, 'r0'), ((4, 512, LANES), F32, 'a'),
         ((4, 512, LANES), F32, 'a'), ((1, 512), F32, 'a'), ((512, 512), F32, 'a'), ((1, 512), F32, 'a')], ex))
    arrived.update(zip(keys, got))
    g_re, g_im = _scan(dy_ssm, pl_['c_re'], pl_['c_im'], pl_['a_re'], -pl_['a_im'], True)
    first_re = jnp.pad(sv['x_re'][:, s - 8:s - 1], ((0, 0), (1, 0), (0, 0)))
    first_im = jnp.pad(sv['x_im'][:, s - 8:s - 1], ((0, 0), (1, 0), (0, 0)))

    def f_s5_in_bwd(gre, gim, xr, xi, pr8, pi8, u, dud, f8r, f8i, bre, bim):
        first = pl.program_id(0) == 0
        xpr = jnp.concatenate([jnp.where(first, f8r[...], pr8), xr[:, :tm - 8]], axis=1)
        xpi = jnp.concatenate([jnp.where(first, f8i[...], pi8), xi[:, :tm - 8]], axis=1)
        dus, dbr, dbi = [], [], []
        for j in range(4):
            gj_r, gj_i, uj = _cat_blocks(gre, j), _cat_blocks(gim, j), _lanes(u, j, LANES)
            dus.append(_mm_nt(gj_r, bre[j]) + _mm_nt(gj_i, bim[j]))
            dbr.append(_mm_tn(uj, gj_r))
            dbi.append(_mm_tn(uj, gj_i))
        da_r = jnp.sum(gre * xpr + gim * xpi, axis=1, keepdims=True)
        da_i = jnp.sum(gim * xpr - gre * xpi, axis=1, keepdims=True)
        return dud + jnp.concatenate(dus, axis=-1), jnp.stack(dbr), jnp.stack(dbi), da_r, da_i
    ex, keys = send('s5_in_bwd')
    (du_ch, gr['b_re'], gr['b_im'], gr['a_re'], gr['a_im']), got = _hosted(_rows(
        "s5_in_bwd", f_s5_in_bwd, s, tm,
        [(g_re, 'r1'), (g_im, 'r1'), (sv['x_re'], 'r1'), (sv['x_im'], 'r1'), (sv['x_re'], 'p8'), (sv['x_im'], 'p8'),
         (sv['u_ch'], 'r0'), (du_dir, 'r0'), (first_re, 'f'), (first_im, 'f'), (pl_['b_re'], 'f'), (pl_['b_im'], 'f')],
        [((s, 512), F32, 'r0'), ((4, LANES, 512), F32, 'a'), ((4, LANES, 512), F32, 'a'),
         ((16, 1, LANES), F32, 'a'), ((16, 1, LANES), F32, 'a')], ex))
    arrived.update(zip(keys, got))
    du = _from_chunks(du_ch)

    def f_qkv_bwd(pr, cos_, sin_, dq_, dk_, dv_, gq, gkv, wq, wk, wv, pt):
        cq, ckv = pr[:, 0:Q_LORA], pr[:, Q_LORA:Q_LORA + KV_LORA]
        cqn, rq = _rms(cq, gq[...])
        kvn, rkv = _rms(ckv, gkv[...])
        cqb, kvb = cqn.astype(MXU), kvn.astype(MXU)
        dcqn = jnp.zeros(cq.shape, F32)
        dkvn = jnp.zeros(ckv.shape, F32)
        dksum = jnp.zeros(dk_[0].shape, F32)
        dwq, dwk, dwv = [], [], []
        for hd in range(MLA_HEADS):
            dqp = (_rope_t(dq_[hd], cos_, sin_, pt) * MLA_SCALE).astype(MXU)
            dkb, dvb = dk_[hd].astype(MXU), dv_[hd].astype(MXU)
            dwq.append(_mm_tn(cqb, dqp))
            dwk.append(_mm_tn(kvb, dkb))
            dwv.append(_mm_tn(kvb, dvb))
            dcqn = dcqn + _mm_nt(dqp, wq[hd])
            dkvn = dkvn + _mm_nt(dkb, wk[hd]) + _mm_nt(dvb, wv[hd])
            dksum = dksum + dk_[hd]
        dcq, dgq = _rms_bwd(cq, gq[...], rq, dcqn)
        dckv, dgkv = _rms_bwd(ckv, gkv[...], rkv, dkvn)
        dpa = jnp.concatenate([dcq, dckv, _rope_t(dksum, cos_, sin_, pt)], axis=-1)
        return dpa, jnp.stack(dwq), jnp.stack(dwk), jnp.stack(dwv), dgq, dgkv
    ex, keys = send('mla_qkv_bwd')
    (dpa, gr['w_uq'], gr['w_k'], gr['w_v'], gr['q_norm_g'], gr['kv_norm_g']), got = _hosted(_rows(
        "mla_qkv_bwd", f_qkv_bwd, s, tm,
        [(sv['proj'], 'r0'), (cos, 'r0'), (sin, 'r0'), (dq, 'r1'), (dk, 'r1'), (dv, 'r1'), (pl_['q_norm_g'], 'f'),
         (pl_['kv_norm_g'], 'f'), (wl['w_uq'], 'f'), (wl['w_k'], 'f'), (wl['w_v'], 'f'), (pmat_t, 'f')],
        [((s, 512), F32, 'r0'), ((MLA_HEADS, Q_LORA, HEAD_W), F32, 'a'), ((MLA_HEADS, KV_LORA, HEAD_W), F32, 'a'),
         ((MLA_HEADS, KV_LORA, HEAD_W), F32, 'a'), ((1, Q_LORA), F32, 'a'), ((1, KV_LORA), F32, 'a')], ex))
    arrived.update(zip(keys, got))

    def f_mix_in_bwd(h_, dpa_, du_, dres, g, w):
        dproj = jnp.concatenate([dpa_, du_], axis=-1).astype(MXU)
        xn, r = _rms(h_, g[...])
        dh, dg = _rms_bwd(h_, g[...], r, _mm_nt(dproj, w[...]))
        return dres + dh, xn, dproj, dg
    ex, keys = send('mix_in_bwd')
    (dh0, xn, dproj, gr['norm_mix_g']), got = _hosted(_rows(
        "mix_in_bwd", f_mix_in_bwd, s, tm,
        [(sv['h'], 'r0'), (dpa, 'r0'), (du, 'r0'), (dh1, 'r0'), (pl_['norm_mix_g'], 'f'), (wl['w_in'], 'f')],
        [((s, D_MODEL), F32, 'r0'), ((s, D_MODEL), MXU, 'r0'), ((s, D_MODEL), MXU, 'r0'), ((1, D_MODEL), F32, 'a')],
        ex))
    arrived.update(zip(keys, got))
    gr['w_in'] = _mm_tn_call("dw_in", xn, dproj)
    return dh0, gr, arrived


def _layer_weights(w):
    wl = {}
    if 'w_in' in w:
        w_in = w['w_in'].reshape(D_MODEL, -1)
        z = lambda n: jnp.zeros((D_MODEL, n), w_in.dtype)
        wl['w_in'] = jnp.concatenate([w_in[:, :384], z(64), w_in[:, 384:416], z(32), w_in[:, 416:]], axis=1)
    if 'w_uq' in w:
        wl['w_uq'] = jnp.pad(w['w_uq'], ((0, 0), (0, 0), (0, HEAD_W - QK_NOPE - QK_ROPE)))
    if 'w_ukv' in w:
        wl['w_k'] = jnp.pad(w['w_ukv'][..., :QK_NOPE], ((0, 0), (0, 0), (0, HEAD_W - QK_NOPE)))
        wv = w['w_ukv'][..., QK_NOPE:]
        even = (jnp.arange(MLA_HEADS) % 2 == 0)[:, None, None]
        wl['w_v'] = jnp.concatenate([jnp.where(even, wv, 0), jnp.where(even, 0, wv)], axis=-1).astype(wv.dtype)
    if 'ssm_w_glu' in w:
        wl['ssm_w_glu'] = w['ssm_w_glu'].reshape(SSM_WIDTH, SSM_WIDTH)
    for n in ('w_out', 'w_xq', 'w_xo'):
        if n in w:
            wl[n] = w[n].reshape(D_MODEL, D_MODEL)
    if 'w_xkv' in w:
        wl['w_xkv'] = w['w_xkv']
    for n in ('w_gate', 'w_up'):
        if n in w:
            wl[n] = jnp.transpose(w[n], (1, 0, 2)).reshape(D_MODEL, D_FF)
    if 'w_down' in w:
        wl['w_down'] = w['w_down'].reshape(D_FF, D_MODEL)
    return wl


def _blocked(gr, n):
    if n == 'w_in':
        d = gr['w_in']
        out = jnp.concatenate([d[:, :384], d[:, 448:480], d[:, 512:]], axis=1).reshape(N_DEV, 128, -1)
    elif n == 'w_uq':
        out = gr['w_uq'][..., :QK_NOPE + QK_ROPE]
    elif n == 'w_ukv':
        even = (jnp.arange(MLA_HEADS) % 2 == 0)[:, None, None]
        dv = gr['w_v']
        out = jnp.concatenate([gr['w_k'][..., :QK_NOPE], jnp.where(even, dv[..., :V_DIM], dv[..., V_DIM:])], axis=-1)
    elif n == 'ssm_w_glu':
        out = gr['ssm_w_glu'].reshape(N_DEV, SSM_WIDTH // N_DEV, SSM_WIDTH)
    elif n in ('w_out', 'w_xq', 'w_xo'):
        out = gr[n].reshape(N_DEV, D_MODEL // N_DEV, D_MODEL)
    elif n in ('w_gate', 'w_up'):
        out = jnp.transpose(gr[n].reshape(D_MODEL, N_DEV, D_FF // N_DEV), (1, 0, 2))
    elif n == 'w_down':
        out = gr[n].reshape(N_DEV, D_FF // N_DEV, D_MODEL)
    else:
        out = gr[n]
    return out.astype(MXU)


def kernel(x, mem, positions, norm_mix_g, w_in, q_norm_g, w_uq, kv_norm_g, w_ukv, ssm_lambda_re, ssm_lambda_im, ssm_log_dt, ssm_b_re, ssm_b_im, ssm_c_re, ssm_c_im, ssm_d, ssm_w_glu, ssm_b_glu, attn_out_g, ssm_out_g, w_out, norm_x_g, mem_norm_g, w_xq, w_xkv, w_xo, norm_ffn_g, w_gate, w_up, w_down, final_norm_g, loss_target, m_norm_mix_g, m_w_in, m_q_norm_g, m_w_uq, m_kv_norm_g, m_w_ukv, m_ssm_lambda_re, m_ssm_lambda_im, m_ssm_log_dt, m_ssm_b_re, m_ssm_b_im, m_ssm_c_re, m_ssm_c_im, m_ssm_d, m_ssm_w_glu, m_ssm_b_glu, m_attn_out_g, m_ssm_out_g, m_w_out, m_norm_x_g, m_mem_norm_g, m_w_xq, m_w_xkv, m_w_xo, m_norm_ffn_g, m_w_gate, m_w_up, m_w_down, m_final_norm_g, v_norm_mix_g, v_w_in, v_q_norm_g, v_w_uq, v_kv_norm_g, v_w_ukv, v_ssm_lambda_re, v_ssm_lambda_im, v_ssm_log_dt, v_ssm_b_re, v_ssm_b_im, v_ssm_c_re, v_ssm_c_im, v_ssm_d, v_ssm_w_glu, v_ssm_b_glu, v_attn_out_g, v_ssm_out_g, v_w_out, v_norm_x_g, v_mem_norm_g, v_w_xq, v_w_xkv, v_w_xo, v_norm_ffn_g, v_w_gate, v_w_up, v_w_down, v_final_norm_g):
    args = dict(locals())
    W = {n: args[n] for n in WEIGHTS}
    M = {n: args['m_' + n] for n in WEIGHTS}
    V = {n: args['v_' + n] for n in WEIGHTS}
    s = x.shape[1]
    h = x[0]
    memx = mem[0]

    freqs = ROPE_THETA ** (-jnp.arange(0, QK_ROPE, 2, dtype=F32) / QK_ROPE)
    ang = positions[0].astype(F32)[:, None] * freqs
    c16, s16 = jnp.cos(ang), jnp.sin(ang)
    cos = jnp.concatenate([jnp.ones((s, QK_NOPE), F32), c16, c16, jnp.zeros((s, 32), F32)], axis=1)
    sin = jnp.concatenate([jnp.zeros((s, QK_NOPE), F32), s16, s16, jnp.zeros((s, 32), F32)], axis=1)
    idx = jnp.arange(QK_ROPE // 2)
    pmat = jnp.zeros((HEAD_W, HEAD_W), F32)
    pmat = pmat.at[QK_NOPE + 16 + idx, QK_NOPE + idx].set(-1.0).at[QK_NOPE + idx, QK_NOPE + 16 + idx].set(1.0)
    tabs = (cos, sin, pmat, pmat.T)

    shards = [{n: W[n][l].astype(MXU) for n in SHARDED} for l in range(DEPTH)]
    gathered = dict(zip(EARLY, _run_exchange("gather_weights", _gather(_named(EARLY, shards[0])))))

    layers = []
    for l in range(DEPTH):
        wl = _layer_weights(gathered)
        s5_in = [W[n][l] for n in ('ssm_lambda_re', 'ssm_lambda_im', 'ssm_log_dt', 'ssm_b_re', 'ssm_b_im',
                                   'ssm_c_re', 'ssm_c_im')]
        (a_re, a_im, bre, bim, cre, cim), s5_vjp = jax.vjp(_s5_params, *s5_in)
        pl_ = {n: W[n][l][None] for n in ('norm_mix_g', 'q_norm_g', 'kv_norm_g', 'ssm_d', 'ssm_b_glu',
                                           'attn_out_g', 'ssm_out_g', 'norm_x_g', 'mem_norm_g', 'norm_ffn_g')}
        pl_.update(a_re=a_re, a_im=a_im, b_re=bre, b_im=bim, c_re=cre, c_im=cim)
        h, sv, wl, gathered = _layer_fwd(h, memx, tabs, wl, pl_, shards[0] if l == 0 else None,
                                         shards[l + 1] if l + 1 < DEPTH else None)
        layers.append((wl, pl_, sv, s5_vjp))

    def f_loss(h_, tgt, g):
        y, r = _rms(h_, g[...])
        err = y - tgt
        part = 0.5 * jnp.sum(jnp.mean(err * err, axis=-1, keepdims=True), axis=0, keepdims=True)
        dh, dg = _rms_bwd(h_, g[...], r, err / D_MODEL)
        return dh, dg, jnp.broadcast_to(part, (8, LANES))
    dh, g_final, loss_part = _rows(
        "loss_head", f_loss, s, min(256, s), [(h, 'r0'), (loss_target[0], 'r0'), (final_norm_g[None], 'f')],
        [((s, D_MODEL), F32, 'r0'), ((1, D_MODEL), F32, 'a'), ((8, LANES), F32, 'a')])
    loss = lax.psum(loss_part[0, 0], ("x", "y", "c"))

    parts = [{} for _ in range(DEPTH)]
    g_rep = [None] * DEPTH
    blocks = None
    for l in reversed(range(DEPTH)):
        wl, pl_, sv, s5_vjp = layers[l]
        dh, gr, arrived = _layer_bwd(dh, sv, memx, tabs, wl, pl_, blocks)
        for (who, n), p in arrived.items():
            parts[l + 1 if who == 'nxt' else l][n] = p
        blocks = {n: _blocked(gr, n) for n in EARLY}
        ds5 = s5_vjp((gr['a_re'], gr['a_im'], gr['b_re'], gr['b_im'], gr['c_re'], gr['c_im']))
        rep = dict(zip(('ssm_lambda_re', 'ssm_lambda_im', 'ssm_log_dt', 'ssm_b_re', 'ssm_b_im', 'ssm_c_re',
                        'ssm_c_im'), ds5))
        for n in ('norm_mix_g', 'q_norm_g', 'kv_norm_g', 'ssm_d', 'ssm_b_glu', 'attn_out_g', 'ssm_out_g',
                  'norm_x_g', 'mem_norm_g', 'norm_ffn_g'):
            rep[n] = gr[n][0]
        g_rep[l] = rep
    grad_x = dh[None]

    rep_names = REPL_L + ['final_norm_g']
    g_loc = {n: jnp.stack([g_rep[l][n] for l in range(DEPTH)]) for n in REPL_L}
    g_loc['final_norm_g'] = g_final
    rest = [n for n in SHARDED if n not in parts[0]]
    last = _run_exchange("last_grads", _together(_scatter(_named(rest, blocks)),
                                                 _gather([_pack(_named(rep_names, g_loc))])))
    parts[0].update(zip(rest, last[:len(rest)]))

    out_sh = [{}, {}, {}, {}]
    for n in SHARDED:
        res = _adamw_weight("adamw_" + n, [parts[l][n] for l in range(DEPTH)], W[n], M[n], V[n])
        for kind, r in enumerate(res):
            out_sh[kind][n] = r

    shapes_rp = [(1,) + W[n].shape if W[n].ndim == 1 else W[n].shape for n in rep_names]
    g_rp = _unpack(_sum_sources("sum_small_grads", last[len(rest)]), shapes_rp)
    as_rows = lambda d: [d[n].reshape(shp) for n, shp in zip(rep_names, shapes_rp)]
    res_rp = (g_rp,) + _adamw_small("adamw_replicated", g_rp, as_rows(W), as_rows(M), as_rows(V))
    out_rp = [{n: a.reshape(W[n].shape) for n, a in zip(rep_names, r)} for r in res_rp]

    outs = [loss, grad_x]
    for kind in range(4):
        for n in WEIGHTS:
            outs.append(out_sh[kind][n] if n in SHARDED else out_rp[kind][n])
    return tuple(outs)
```

```python
from typing import Callable, NamedTuple

import jax
import jax.numpy as jnp
from jax import lax
from jax.experimental import pallas as pl
from jax.experimental.pallas import tpu as pltpu

F32 = jnp.float32
MXU = jnp.bfloat16
HI = lax.Precision.HIGHEST

D_MODEL = 1024
MLA_HEADS = 8
QK_NOPE = 64
QK_ROPE = 32
V_DIM = 64
Q_LORA = 256
KV_LORA = 128
SSM_WIDTH = 512
SSM_GROUPS = 32
SSM_GROUP = 16
SSM_STATE = 64
X_HEADS = 4
X_HEAD_DIM = 256
D_FF = 2816
FF_CHUNK = D_FF // 2
ROPE_THETA = 10000.0
EPS = 1e-6
DEPTH = 2
N_DEV = 8
LANES = 128
HEAD_W = 128
MLA_SCALE = (QK_NOPE + QK_ROPE) ** -0.5
X_SCALE = X_HEAD_DIM ** -0.5
ADAM_LR, ADAM_B1, ADAM_B2, ADAM_EPS, ADAM_WD, ADAM_STEP = 0.001, 0.9, 0.999, 1e-08, 0.01, 10
VMEM_LIMIT = 56 * 1024 * 1024
FLASH_TILE = 512
DW_ROWS = 2048
X_ROWS = 512
MESH = pl.DeviceIdType.MESH

SHARDED = ['w_in', 'w_uq', 'w_ukv', 'ssm_w_glu', 'w_out', 'w_xq', 'w_xkv', 'w_xo', 'w_gate', 'w_up', 'w_down']
REPL_L = ['norm_mix_g', 'q_norm_g', 'kv_norm_g', 'ssm_lambda_re', 'ssm_lambda_im', 'ssm_log_dt', 'ssm_b_re',
          'ssm_b_im', 'ssm_c_re', 'ssm_c_im', 'ssm_d', 'ssm_b_glu', 'attn_out_g', 'ssm_out_g', 'norm_x_g',
          'mem_norm_g', 'norm_ffn_g']
WEIGHTS = ['norm_mix_g', 'w_in', 'q_norm_g', 'w_uq', 'kv_norm_g', 'w_ukv', 'ssm_lambda_re', 'ssm_lambda_im',
           'ssm_log_dt', 'ssm_b_re', 'ssm_b_im', 'ssm_c_re', 'ssm_c_im', 'ssm_d', 'ssm_w_glu', 'ssm_b_glu',
           'attn_out_g', 'ssm_out_g', 'w_out', 'norm_x_g', 'mem_norm_g', 'w_xq', 'w_xkv', 'w_xo', 'norm_ffn_g',
           'w_gate', 'w_up', 'w_down', 'final_norm_g']


def _pcall(body, **kw):
    return pl.pallas_call(body, **kw)


def _mm(a, b):
    return jnp.dot(a.astype(MXU), b.astype(MXU), preferred_element_type=F32)


def _mm_nt(a, b):
    return lax.dot_general(a.astype(MXU), b.astype(MXU), (((1,), (1,)), ((), ())), preferred_element_type=F32)


def _mm_tn(a, b):
    return lax.dot_general(a.astype(MXU), b.astype(MXU), (((0,), (0,)), ((), ())), preferred_element_type=F32)


def _mm_hi(a, b):
    return jnp.dot(a.astype(F32), b.astype(F32), precision=HI, preferred_element_type=F32)


def _rms(x, g):
    r = lax.rsqrt(jnp.mean(x * x, axis=-1, keepdims=True) + EPS)
    return x * r * g, r


def _rms_bwd(x, g, r, dy):
    dyg = dy * g
    dx = r * dyg - x * (r * r * r) * jnp.mean(dyg * x, axis=-1, keepdims=True)
    return dx, jnp.sum(dy * x * r, axis=0, keepdims=True)


def _rope(x, cos, sin, p_ref):
    return x * cos + _mm_hi(x, p_ref[...]) * sin


def _rope_t(g, cos, sin, pt_ref):
    return g * cos + _mm_hi(g * sin, pt_ref[...])


def _softmax(s):
    m = jnp.max(s, axis=-1, keepdims=True)
    e = jnp.exp(s - m)
    return e / jnp.sum(e, axis=-1, keepdims=True)


def _lanes(x, j, w):
    return x[:, j * w:(j + 1) * w]


def _rows(name, fn, n, tm, ins, outs, side=None):
    def spec(shape, kind):
        nd = len(shape)
        if kind == 'p8':
            return pl.BlockSpec((shape[0], 8, shape[2]), lambda i: (0, jnp.maximum(i * (tm // 8) - 1, 0), 0))
        if kind == 'f':
            return pl.BlockSpec(shape, lambda i, _nd=nd: (0,) * _nd, pipeline_mode=pl.Buffered(1))
        if kind == 'a':
            return pl.BlockSpec(shape, lambda i, _nd=nd: (0,) * _nd)
        ax = int(kind[1])
        blk = tuple(tm if d == ax else s for d, s in enumerate(shape))
        return pl.BlockSpec(blk, lambda i, _ax=ax, _nd=nd: tuple(i if d == _ax else 0 for d in range(_nd)))

    n_in, n_out, n_steps = len(ins), len(outs), n // tm

    def body(*refs):
        in_refs, out_refs, steps = _side_split(refs, n_in, n_out, side)
        i = pl.program_id(0)
        if steps:
            pl.when(i == 0)(steps[0])
            pl.when(i == _pass_on_step(n_steps))(steps[1])
        args = [r if k == 'f' else r[...] for r, (_, k) in zip(in_refs, ins)]
        res = fn(*args)
        for r, (_, dt, k), v in zip(out_refs, outs, res):
            if k == 'a':
                _accumulate(r, v.astype(dt), i)
            else:
                r[...] = v.astype(dt)
        if steps:
            pl.when(i == n_steps - 1)(steps[2])

    s_in, s_out, s_shape, s_sems, s_ops = _side_args(side)
    res = _pcall(
        body, name=name + ("_x" if side else ""), grid=(n_steps,),
        in_specs=[spec(a.shape, k) for a, k in ins] + s_in,
        out_specs=[spec(s, k) for s, _, k in outs] + s_out,
        out_shape=[jax.ShapeDtypeStruct(s, dt) for s, dt, _ in outs] + s_shape,
        scratch_shapes=s_sems,
        compiler_params=pltpu.CompilerParams(dimension_semantics=("arbitrary",), vmem_limit_bytes=VMEM_LIMIT),
    )(*[a for a, _ in ins], *s_ops)
    return _Hosted(res[:n_out], res[n_out:]) if side else res


def _accumulate(ref, v, i):
    @pl.when(i == 0)
    def _():
        ref[...] = v

    @pl.when(i != 0)
    def _():
        ref[...] += v


def _mm_tn_call(name, a, b, tk=None, tn=None):
    out_dtype = MXU
    s, k = a.shape
    n = b.shape[1]
    tk, tn = tk or k, tn or n
    ts = min(DW_ROWS, s)
    ns = s // ts

    def body(a_ref, b_ref, o_ref, acc_ref):
        j = pl.program_id(2)
        _accumulate(acc_ref, _mm_tn(a_ref[...], b_ref[...]), j)

        @pl.when(j == ns - 1)
        def _():
            o_ref[...] = acc_ref[...].astype(out_dtype)

    return _pcall(
        body, name=name, grid=(k // tk, n // tn, ns),
        in_specs=[pl.BlockSpec((ts, tk), lambda ik, jn, j: (j, ik)),
                  pl.BlockSpec((ts, tn), lambda ik, jn, j: (j, jn))],
        out_specs=pl.BlockSpec((tk, tn), lambda ik, jn, j: (ik, jn)),
        out_shape=jax.ShapeDtypeStruct((k, n), out_dtype),
        scratch_shapes=[pltpu.VMEM((tk, tn), F32)],
        compiler_params=pltpu.CompilerParams(dimension_semantics=("arbitrary", "arbitrary", "arbitrary"),
                                             vmem_limit_bytes=VMEM_LIMIT),
    )(a, b)


def _side_split(refs, n_in, n_out, side):
    if side is None:
        return refs[:n_in], refs[n_in:n_in + n_out], None
    si, so = len(side.ins), len(side.out_shapes)
    own_in, side_in = refs[:n_in], refs[n_in:n_in + si]
    own_out, side_out = refs[n_in + si:n_in + si + n_out], refs[n_in + si + n_out:n_in + si + n_out + so]
    return own_in, own_out, side.steps(side_in, side_out, refs[n_in + si + n_out + so:])


def _pass_on_step(n_steps):
    return max(n_steps - 2, 0)


def _side_args(side):
    if side is None:
        return [], [], [], [], []
    any_spec = pl.BlockSpec(memory_space=pl.ANY)
    return ([any_spec] * len(side.ins), [any_spec] * len(side.out_shapes), list(side.out_shapes),
            list(side.sem_shapes), list(side.ins))


class _Hosted(NamedTuple):
    results: list
    arrived: list


def _hosted(res):
    return res if isinstance(res, _Hosted) else _Hosted(res, ())


def _flash_fwd(q, k, v, side=None):
    nh, s, w = q.shape
    t = min(FLASH_TILE, s)
    nq = s // t
    n_steps = (nh // 2) * nq

    def body(*refs):
        (q_ref, k_ref, v_ref), (o_ref, lse_ref), steps = _side_split(refs, 3, 2, side)
        step = pl.program_id(0) * nq + pl.program_id(1)
        if steps:
            pl.when(step == 0)(steps[0])
            pl.when(step == _pass_on_step(n_steps))(steps[1])
        qi = pl.program_id(1)
        qs = [q_ref[0], q_ref[1]]
        below = lax.broadcasted_iota(jnp.int32, (t, t), 1) <= lax.broadcasted_iota(jnp.int32, (t, t), 0)

        def tile(j, carry, diagonal):
            sl = pl.ds(pl.multiple_of(j * t, t), t)
            out = []
            for hh in range(2):
                m, l, acc = carry[3 * hh:3 * hh + 3]
                sc = _mm_nt(qs[hh], k_ref[hh, sl, :])
                if diagonal:
                    sc = jnp.where(below, sc, -1e30)
                m_new = jnp.maximum(m, jnp.max(sc, axis=1, keepdims=True))
                p = jnp.exp(sc - m_new)
                alpha = jnp.exp(m - m_new)
                out += [m_new, alpha * l + jnp.sum(p, axis=1, keepdims=True), alpha * acc + _mm(p, v_ref[hh, sl, :])]
            return tuple(out)

        init = (jnp.full((t, 1), -1e30, F32), jnp.zeros((t, 1), F32), jnp.zeros((t, w), F32)) * 2
        carry = lax.fori_loop(0, qi, lambda j, c: tile(j, c, False), init)
        carry = tile(qi, carry, True)
        o_ref[...] = carry[2] / carry[1] + carry[5] / carry[4]
        for hh in range(2):
            lse_ref[hh] = jnp.broadcast_to(carry[3 * hh] + jnp.log(carry[3 * hh + 1]), (t, w))
        if steps:
            pl.when(step == n_steps - 1)(steps[2])

    s_in, s_out, s_shape, s_sems, s_ops = _side_args(side)
    res = _pcall(
        body, name="mla_flash_fwd" + ("_x" if side else ""), grid=(nh // 2, nq),
        in_specs=[pl.BlockSpec((2, t, w), lambda p, i: (p, i, 0)),
                  pl.BlockSpec((2, s, w), lambda p, i: (p, 0, 0)),
                  pl.BlockSpec((2, s, w), lambda p, i: (p, 0, 0))] + s_in,
        out_specs=[pl.BlockSpec((t, w), lambda p, i: (i, p)),
                   pl.BlockSpec((2, t, w), lambda p, i: (p, i, 0))] + s_out,
        out_shape=[jax.ShapeDtypeStruct((s, (nh // 2) * w), F32), jax.ShapeDtypeStruct((nh, s, w), F32)] + s_shape,
        scratch_shapes=s_sems,
        compiler_params=pltpu.CompilerParams(dimension_semantics=("arbitrary", "arbitrary"),
                                             vmem_limit_bytes=VMEM_LIMIT),
    )(q, k, v, *s_ops)
    return res[0], res[1], res[2:]


def _flash_bwd(q, k, v, o, lse, do, side=None):
    nh, s, w = q.shape
    t = min(FLASH_TILE, s)
    nq = s // t
    n_steps = (nh // 2) * nq

    def body(*refs):
        (q_ref, k_ref, v_ref, o_ref, lse_ref, do_ref), (dq_ref, dk_ref, dv_ref), steps = _side_split(refs, 6, 3, side)
        step = pl.program_id(0) * nq + pl.program_id(1)
        if steps:
            pl.when(step == 0)(steps[0])
            pl.when(step == _pass_on_step(n_steps))(steps[1])
        j = pl.program_id(1)

        @pl.when(j == 0)
        def _():
            dq_ref[...] = jnp.zeros(dq_ref.shape, F32)

        below = lax.broadcasted_iota(jnp.int32, (t, t), 1) <= lax.broadcasted_iota(jnp.int32, (t, t), 0)
        lane = lax.broadcasted_iota(jnp.int32, (t, w), 1)
        heads = [jnp.logical_and(lane >= hh * V_DIM, lane < (hh + 1) * V_DIM) for hh in range(2)]
        ks = [k_ref[0], k_ref[1]]
        vs = [v_ref[0], v_ref[1]]

        def tile(i, carry, diagonal):
            sl = pl.ds(pl.multiple_of(i * t, t), t)
            dout_all, o_all = do_ref[sl, :], o_ref[sl, :]
            out = []
            for hh in range(2):
                dk, dv = carry[2 * hh], carry[2 * hh + 1]
                qh = q_ref[hh, sl, :]
                dout = jnp.where(heads[hh], dout_all, 0.0)
                sc = _mm_nt(qh, ks[hh])
                if diagonal:
                    sc = jnp.where(below, sc, -1e30)
                p = jnp.exp(sc - lse_ref[hh, sl, 0:1])
                dp = _mm_nt(dout, vs[hh])
                ds = p * (dp - jnp.sum(dout * o_all, axis=1, keepdims=True))
                dq_ref[hh, sl, :] += _mm(ds, ks[hh])
                out += [dk + _mm_tn(ds, qh), dv + _mm_tn(p, dout)]
            return tuple(out)

        carry = tile(j, (jnp.zeros((t, w), F32),) * 4, True)
        carry = lax.fori_loop(j + 1, nq, lambda i, c: tile(i, c, False), carry)
        for hh in range(2):
            dk_ref[hh] = carry[2 * hh]
            dv_ref[hh] = jnp.where(heads[hh], carry[2 * hh + 1], 0.0)
        if steps:
            pl.when(step == n_steps - 1)(steps[2])

    s_in, s_out, s_shape, s_sems, s_ops = _side_args(side)
    res = _pcall(
        body, name="mla_flash_bwd" + ("_x" if side else ""), grid=(nh // 2, nq),
        in_specs=[pl.BlockSpec((2, s, w), lambda p, j: (p, 0, 0)),
                  pl.BlockSpec((2, t, w), lambda p, j: (p, j, 0)),
                  pl.BlockSpec((2, t, w), lambda p, j: (p, j, 0)),
                  pl.BlockSpec((s, w), lambda p, j: (0, p)),
                  pl.BlockSpec((2, s, w), lambda p, j: (p, 0, 0)),
                  pl.BlockSpec((s, w), lambda p, j: (0, p))] + s_in,
        out_specs=[pl.BlockSpec((2, s, w), lambda p, j: (p, 0, 0)),
                   pl.BlockSpec((2, t, w), lambda p, j: (p, j, 0)),
                   pl.BlockSpec((2, t, w), lambda p, j: (p, j, 0))] + s_out,
        out_shape=[jax.ShapeDtypeStruct((nh, s, w), F32)] * 3 + s_shape,
        scratch_shapes=s_sems,
        compiler_params=pltpu.CompilerParams(dimension_semantics=("arbitrary", "arbitrary"),
                                             vmem_limit_bytes=VMEM_LIMIT),
    )(q, k, v, o, lse, do, *s_ops)
    return res[0], res[1], res[2], res[3:]


def _scan(src, w_re, w_im, a_re, a_im, reverse):
    s = src.shape[0]
    nb, w = a_re.shape[0], LANES
    ch = s // 8
    assert ch & (ch - 1) == 0
    grp = 4
    tr = min(512, s)

    def cmul(ar, ai, xr, xi):
        return ar * xr - ai * xi, ar * xi + ai * xr

    def body(src_ref, wr_ref, wi_ref, ar_ref, ai_ref, xr_ref, xi_ref):
        def project(c, carry):
            rows = pl.ds(pl.multiple_of(c * tr, tr), tr)
            u = src_ref[rows, :]
            if reverse:
                br, bi = _mm_nt(u, wr_ref[...]), _mm_nt(u, wi_ref[...])
            else:
                br, bi = _mm(u, wr_ref[...]), _mm(u, wi_ref[...])
            for g in range(grp):
                xr_ref[g, rows, :] = _lanes(br, g, w)
                xi_ref[g, rows, :] = _lanes(bi, g, w)
            return carry

        lax.fori_loop(0, s // tr, project, 0)
        sub = lax.broadcasted_iota(jnp.int32, (8, w), 0)

        def shift(x, k):
            if reverse:
                return jnp.where(sub < 8 - k, pltpu.roll(x, 8 - k, 0), 0.0)
            return jnp.where(sub >= k, pltpu.roll(x, k, 0), 0.0)

        ar = [jnp.broadcast_to(ar_ref[g], (8, w)) for g in range(grp)]
        ai = [jnp.broadcast_to(ai_ref[g], (8, w)) for g in range(grp)]

        def tsl(i):
            return pl.ds(pl.multiple_of(((ch - 1 - i) if reverse else i) * 8, 8), 8)

        def local(i, carry):
            out = []
            for g in range(grp):
                xr, xi = carry[2 * g], carry[2 * g + 1]
                pr, pi = cmul(ar[g], ai[g], xr, xi)
                nr = pr + xr_ref[g, tsl(i), :]
                ni = pi + xi_ref[g, tsl(i), :]
                xr_ref[g, tsl(i), :] = nr
                xi_ref[g, tsl(i), :] = ni
                out += [nr, ni]
            return tuple(out)

        fin = lax.fori_loop(0, ch, local, (jnp.zeros((8, w), F32),) * (2 * grp))

        carry_in = []
        for g in range(grp):
            pr, pi = ar[g], ai[g]
            for _ in range(ch.bit_length() - 1):
                pr, pi = cmul(pr, pi, pr, pi)
            fr, fi = fin[2 * g], fin[2 * g + 1]
            for kk in (1, 2, 4):
                sr, si = cmul(pr, pi, shift(fr, kk), shift(fi, kk))
                fr, fi = fr + sr, fi + si
                pr, pi = cmul(pr, pi, pr, pi)
            carry_in += [shift(fr, 1), shift(fi, 1)]

        def fix(i, pw):
            out = []
            for g in range(grp):
                pr, pi = pw[2 * g], pw[2 * g + 1]
                cr, ci = cmul(pr, pi, carry_in[2 * g], carry_in[2 * g + 1])
                xr_ref[g, tsl(i), :] = xr_ref[g, tsl(i), :] + cr
                xi_ref[g, tsl(i), :] = xi_ref[g, tsl(i), :] + ci
                nr, ni = cmul(pr, pi, ar[g], ai[g])
                out += [nr, ni]
            return tuple(out)

        lax.fori_loop(0, ch, fix, tuple(x for g in range(grp) for x in (ar[g], ai[g])))

    per_j = 4 // grp
    blk = pl.BlockSpec((grp, s, w), lambda i: (i, 0, 0))
    ablk = pl.BlockSpec((grp, 1, w), lambda i: (i, 0, 0))
    sblk = pl.BlockSpec((s, w), lambda i: (0, i // per_j))
    if reverse:
        wblk = pl.BlockSpec((None, grp * w, w), lambda i: (i // per_j, i % per_j, 0))
    else:
        wblk = pl.BlockSpec((None, w, grp * w), lambda i: (i // per_j, 0, i % per_j))
    return _pcall(
        body, name="s5_scan_rev" if reverse else "s5_scan", grid=(nb // grp,),
        in_specs=[sblk, wblk, wblk, ablk, ablk], out_specs=[blk, blk],
        out_shape=[jax.ShapeDtypeStruct((nb, s, w), F32)] * 2,
        compiler_params=pltpu.CompilerParams(dimension_semantics=("arbitrary",), vmem_limit_bytes=VMEM_LIMIT),
    )(src, w_re, w_im, a_re, a_im)


class _Exchange(NamedTuple):
    ins: list
    out_shapes: list
    sem_shapes: list
    steps: Callable


def _gather_steps(ins, outs, sems):
    n = len(ins)
    send_sems, recv_sems, local_sems = sems
    x, y, c = lax.axis_index("x"), lax.axis_index("y"), lax.axis_index("c")
    me, sibling = (x, y, c), (x, y, 1 - c)
    chips = [(1 - x, y), (x, 1 - y), (1 - x, 1 - y)]

    def copy(a, k, block, to, src=None):
        dst = outs[a].at[4 * block[0] + 2 * block[1] + block[2]]
        return pltpu.make_async_remote_copy(
            src_ref=dst if src is None else src, dst_ref=dst,
            send_sem=send_sems.at[a, k], recv_sem=recv_sems.at[a, k], device_id=to, device_id_type=MESH)

    mine = [pltpu.make_async_copy(ins[a], outs[a].at[4 * x + 2 * y + c], local_sems.at[a]) for a in range(n)]
    first = []
    for a in range(n):
        first.append(copy(a, 0, me, sibling, src=ins[a]))
        first += [copy(a, 1 + j, me, (*chip, c), src=ins[a]) for j, chip in enumerate(chips)]
    passed = [copy(a, 4 + j, (*chip, c), sibling) for j, chip in enumerate(chips) for a in range(n)]

    def start():
        for cp in mine + first:
            cp.start()

    def pass_on():
        i = 0
        for j, chip in enumerate(chips):
            for a in range(n):
                copy(a, 1 + j, (*chip, c), me).wait_recv()
                passed[i].start()
                i += 1

    def finish():
        for a in range(n):
            copy(a, 0, sibling, me).wait_recv()
            for j, chip in enumerate(chips):
                copy(a, 4 + j, (*chip, 1 - c), me).wait_recv()
        for cp in first + passed:
            cp.wait_send()
        for cp in mine:
            cp.wait()

    return start, pass_on, finish


def _gather(arrs):
    n = len(arrs)
    return _Exchange(list(arrs), [jax.ShapeDtypeStruct((N_DEV,) + a.shape, a.dtype) for a in arrs],
                     [pltpu.SemaphoreType.DMA((n, 7)), pltpu.SemaphoreType.DMA((n, 7)), pltpu.SemaphoreType.DMA((n,))],
                     _gather_steps)


def _scatter_steps(ins, outs, sems):
    n = len(ins)
    send_sems, recv_sems, local_sems = sems
    x, y, c = lax.axis_index("x"), lax.axis_index("y"), lax.axis_index("c")
    me = 4 * x + 2 * y + c
    own, sent, arrivals = [], [], []
    for a in range(n):
        own.append(pltpu.make_async_copy(ins[a].at[me], outs[a].at[me], local_sems.at[a]))
        for k in range(1, N_DEV):
            px, py, pc = x ^ ((k >> 2) & 1), y ^ ((k >> 1) & 1), c ^ (k & 1)
            peer = 4 * px + 2 * py + pc
            sent.append(pltpu.make_async_remote_copy(
                src_ref=ins[a].at[peer], dst_ref=outs[a].at[me],
                send_sem=send_sems.at[a, k - 1], recv_sem=recv_sems.at[a, k - 1],
                device_id=(px, py, pc), device_id_type=MESH))
            arrivals.append(pltpu.make_async_remote_copy(
                src_ref=ins[a].at[me], dst_ref=outs[a].at[peer],
                send_sem=send_sems.at[a, k - 1], recv_sem=recv_sems.at[a, k - 1],
                device_id=(x, y, c), device_id_type=MESH))

    def start():
        for cp in own + sent:
            cp.start()

    def pass_on():
        pass

    def finish():
        for cp in arrivals:
            cp.wait_recv()
        for cp in sent:
            cp.wait_send()
        for cp in own:
            cp.wait()

    return start, pass_on, finish


def _scatter(grads):
    n = len(grads)
    return _Exchange(list(grads), [jax.ShapeDtypeStruct(g.shape, g.dtype) for g in grads],
                     [pltpu.SemaphoreType.DMA((n, N_DEV - 1)), pltpu.SemaphoreType.DMA((n, N_DEV - 1)),
                      pltpu.SemaphoreType.DMA((n,))], _scatter_steps)


def _together(a, b):
    def steps(ins, outs, sems):
        sa = a.steps(ins[:len(a.ins)], outs[:len(a.out_shapes)], sems[:len(a.sem_shapes)])
        sb = b.steps(ins[len(a.ins):], outs[len(a.out_shapes):], sems[len(a.sem_shapes):])

        def both(k):
            def run():
                sa[k]()
                sb[k]()
            return run
        return both(0), both(1), both(2)

    return _Exchange(a.ins + b.ins, a.out_shapes + b.out_shapes, a.sem_shapes + b.sem_shapes, steps)


def _run_exchange(name, ex):
    n_in, n_out = len(ex.ins), len(ex.out_shapes)

    def body(*refs):
        for step in ex.steps(refs[:n_in], refs[n_in:n_in + n_out], refs[n_in + n_out:]):
            step()

    any_spec = pl.BlockSpec(memory_space=pl.ANY)
    return _pcall(body, name=name, in_specs=[any_spec] * n_in, out_specs=[any_spec] * n_out,
                  out_shape=list(ex.out_shapes), scratch_shapes=list(ex.sem_shapes))(*ex.ins)


def _adam_math(g, w_, m_, v_):
    m_new = ADAM_B1 * m_ + (1.0 - ADAM_B1) * g
    v_new = ADAM_B2 * v_ + (1.0 - ADAM_B2) * (g * g)
    m_hat = m_new / (1.0 - ADAM_B1 ** ADAM_STEP)
    v_hat = v_new / (1.0 - ADAM_B2 ** ADAM_STEP)
    delta = -ADAM_LR * (m_hat / (jnp.sqrt(v_hat) + ADAM_EPS) + ADAM_WD * w_)
    return delta, m_new, v_new


def _adamw_weight(name, parts, w, m, v):
    nl = len(parts)

    def body(*refs):
        p_refs = refs[:nl]
        w_ref, m_ref, v_ref, g_ref, d_ref, mo_ref, vo_ref = refs[nl:]
        for l in range(nl):
            g = p_refs[l][0].astype(F32)
            for j in range(1, N_DEV):
                g = g + p_refs[l][j].astype(F32)
            g_ref[l] = g
            d_ref[l], mo_ref[l], vo_ref[l] = _adam_math(g, w_ref[l], m_ref[l], v_ref[l])

    return _pcall(
        body, name=name, out_shape=[jax.ShapeDtypeStruct(w.shape, F32)] * 4,
        compiler_params=pltpu.CompilerParams(vmem_limit_bytes=VMEM_LIMIT),
    )(*parts, w, m, v)

def _sum_sources(name, parts):
    r = parts.shape[1]

    def body(p_ref, g_ref):
        g = p_ref[0]
        for j in range(1, N_DEV):
            g = g + p_ref[j]
        g_ref[...] = g

    return _pcall(body, name=name, out_shape=jax.ShapeDtypeStruct((r, LANES), F32),
                  compiler_params=pltpu.CompilerParams(vmem_limit_bytes=VMEM_LIMIT))(parts)


def _adamw_small(name, g, w, m, v):
    n = len(g)

    def body(*refs):
        g_r, w_r, m_r, v_r = (refs[k * n:(k + 1) * n] for k in range(4))
        d_r, mo_r, vo_r = (refs[k * n:(k + 1) * n] for k in range(4, 7))
        for i in range(n):
            d_r[i][...], mo_r[i][...], vo_r[i][...] = _adam_math(g_r[i][...], w_r[i][...], m_r[i][...], v_r[i][...])

    res = _pcall(body, name=name, out_shape=[jax.ShapeDtypeStruct(a.shape, F32) for a in w] * 3,
                 compiler_params=pltpu.CompilerParams(vmem_limit_bytes=VMEM_LIMIT))(*g, *w, *m, *v)
    return res[:n], res[n:2 * n], res[2 * n:]


def _pack(arrs):
    flat = jnp.concatenate([a.reshape(-1) for a in arrs])
    flat = jnp.pad(flat, (0, (-flat.shape[0]) % (8 * LANES)))
    return flat.reshape(-1, LANES)


def _unpack(packed, shapes):
    flat = packed.reshape(-1)
    out, off = [], 0
    for shp in shapes:
        size = 1
        for d in shp:
            size *= d
        out.append(flat[off:off + size].reshape(shp))
        off += size
    return out


def _s5_params(lam_re, lam_im, log_dt, b_re, b_im, c_re, c_im):
    dt = jnp.exp(log_dt)[:, None]
    e = jnp.exp(lam_re * dt)
    ang = lam_im * dt
    a_re, a_im = e * jnp.cos(ang), e * jnp.sin(ang)
    nr, ni = a_re - 1.0, a_im
    den = lam_re * lam_re + lam_im * lam_im
    cr = ((nr * lam_re + ni * lam_im) / den)[..., None]
    ci = ((ni * lam_re - nr * lam_im) / den)[..., None]
    bb_re = cr * b_re - ci * b_im
    bb_im = cr * b_im + ci * b_re
    eye = jnp.eye(8, dtype=F32)[None, :, None, :, None]

    def bblk(bb):
        t = jnp.transpose(bb.reshape(4, 8, SSM_STATE, SSM_GROUP), (0, 3, 1, 2))
        return (eye * t[:, None]).reshape(4, 8 * SSM_GROUP, 8 * SSM_STATE)

    def cblk(cc):
        t = jnp.transpose(cc.reshape(4, 8, SSM_GROUP, SSM_STATE), (0, 3, 1, 2))
        return (eye * t[:, None]).reshape(4, 8 * SSM_STATE, 8 * SSM_GROUP)

    nb = SSM_GROUPS * SSM_STATE // LANES
    return (a_re.reshape(nb, 1, LANES), a_im.reshape(nb, 1, LANES), bblk(bb_re), bblk(bb_im),
            cblk(c_re), -cblk(c_im))


def _cat_blocks(x3, j):
    return jnp.concatenate([x3[4 * j + k] for k in range(4)], axis=-1)


def _to_chunks(a):
    s, c = a.shape
    return a.reshape(8, s // 8, c).transpose(1, 0, 2).reshape(s, c)


def _from_chunks(a):
    s, c = a.shape
    return a.reshape(s // 8, 8, c).transpose(1, 0, 2).reshape(s, c)


EARLY = ['w_in', 'w_uq', 'w_ukv']

FWD_PLAN = {
    'flash': ('late', ['ssm_w_glu', 'w_out', 'w_xq', 'w_xkv', 'w_xo', 'w_gate', 'w_up', 'w_down']),
    'ffn': ('nxt', EARLY),
}
BWD_PLAN = {
    'ffn_bwd': ('nxt', EARLY),
    'xattn_bwd': ('own', ['w_down']),
    'flash_bwd': ('own', ['w_gate', 'w_up', 'w_xq', 'w_xkv', 'w_xo']),
    'mix_in_bwd': ('own', ['ssm_w_glu', 'w_out']),
}


def _named(names, d):
    return [d[n] for n in names]


def _layer_fwd(h, memx, tabs, wl, pl_, late=None, nxt=None):
    s = h.shape[0]
    tm = min(256, s)
    cos, sin, pmat, pmat_t = tabs
    sv = {}
    wl = dict(wl)
    nxt_got = {}

    def fetch(host):
        who, names = FWD_PLAN.get(host, (None, []))
        src = late if who == 'late' else nxt if who == 'nxt' else None
        return _gather(_named(names, src)) if src else None

    def landed(host, got):
        who, names = FWD_PLAN.get(host, (None, []))
        if got and who == 'late':
            wl.update(_layer_weights(dict(zip(names, got))))
        elif got:
            nxt_got.update(zip(names, got))

    def f_mix_in(h_, g, w):
        xn, _ = _rms(h_, g[...])
        return (_mm(xn, w[...]),)
    proj, = _rows("mix_in", f_mix_in, s, tm, [(h, 'r0'), (pl_['norm_mix_g'], 'f'), (wl['w_in'], 'f')],
                  [((s, D_MODEL), F32, 'r0')])

    def f_qkv(pr, cos_, sin_, gq, gkv, wq, wk, wv, pm):
        cqn = _rms(pr[:, 0:Q_LORA], gq[...])[0].astype(MXU)
        kvn = _rms(pr[:, Q_LORA:Q_LORA + KV_LORA], gkv[...])[0].astype(MXU)
        krr = _rope(pr[:, 384:512], cos_, sin_, pm)
        qs, ks, vs = [], [], []
        for hd in range(MLA_HEADS):
            qs.append(_rope(_mm(cqn, wq[hd]), cos_, sin_, pm) * MLA_SCALE)
            ks.append(_mm(kvn, wk[hd]) + krr)
            vs.append(_mm(kvn, wv[hd]))
        return jnp.stack(qs), jnp.stack(ks), jnp.stack(vs)
    hshape = (MLA_HEADS, s, HEAD_W)
    (q, k, v), got = _hosted(_rows(
        "mla_qkv", f_qkv, s, tm,
        [(proj, 'r0'), (cos, 'r0'), (sin, 'r0'), (pl_['q_norm_g'], 'f'), (pl_['kv_norm_g'], 'f'),
         (wl['w_uq'], 'f'), (wl['w_k'], 'f'), (wl['w_v'], 'f'), (pmat, 'f')],
        [(hshape, MXU, 'r1')] * 3, fetch('mla_qkv')))
    landed('mla_qkv', got)

    a_out, lse, got = _flash_fwd(q, k, v, fetch('flash'))
    landed('flash', got)

    u_ch = _to_chunks(proj[:, 512:1024])

    x_re, x_im = _scan(u_ch, pl_['b_re'], pl_['b_im'], pl_['a_re'], pl_['a_im'], False)

    def f_s5_out(xr, xi, u, cre, cim, d, wglu, bglu):
        y = jnp.concatenate([_mm(_cat_blocks(xr, j), cre[j]) + _mm(_cat_blocks(xi, j), cim[j])
                             for j in range(4)], axis=-1) + d[...] * u
        z = _mm(jax.nn.gelu(y), wglu[...]) + bglu[...]
        return y, y * jax.nn.sigmoid(z)
    (y_ssm, s_out_ch), got = _hosted(_rows(
        "s5_out", f_s5_out, s, tm,
        [(x_re, 'r1'), (x_im, 'r1'), (u_ch, 'r0'), (pl_['c_re'], 'f'), (pl_['c_im'], 'f'),
         (pl_['ssm_d'], 'f'), (wl['ssm_w_glu'], 'f'), (pl_['ssm_b_glu'], 'f')],
        [((s, SSM_WIDTH), F32, 'r0')] * 2, fetch('s5_out')))
    landed('s5_out', got)
    s_out = _from_chunks(s_out_ch)

    def f_mix_out(h_, a, so, ga, gs, w):
        an = _rms(a, ga[...])[0]
        sn = _rms(so, gs[...])[0]
        return (h_ + _mm(jnp.concatenate([an, sn], axis=-1), w[...]),)
    (h1,), got = _hosted(_rows("mix_out", f_mix_out, s, tm,
                               [(h, 'r0'), (a_out, 'r0'), (s_out, 'r0'), (pl_['attn_out_g'], 'f'),
                                (pl_['ssm_out_g'], 'f'), (wl['w_out'], 'f')],
                               [((s, D_MODEL), F32, 'r0')], fetch('mix_out')))
    landed('mix_out', got)

    m_len = memx.shape[0]

    def f_memkv(mm_, g, w):
        mn = _rms(mm_, g[...])[0].astype(MXU)
        return (jnp.stack([_mm(mn, w[d]) for d in range(N_DEV)]),)
    kvm, = _rows("mem_kv", f_memkv, m_len, m_len, [(memx, 'r0'), (pl_['mem_norm_g'], 'f'), (wl['w_xkv'], 'f')],
                 [((N_DEV, m_len, X_HEAD_DIM), MXU, 'r1')])

    def f_xattn(h_, g, wq, kv_, wo):
        hn = _rms(h_, g[...])[0].astype(MXU)
        out = jnp.zeros(h_.shape, F32)
        for hd in range(X_HEADS):
            cs = pl.ds(hd * X_HEAD_DIM, X_HEAD_DIM)
            qh = _mm(hn, wq[:, cs])
            p = _softmax(_mm_nt(qh, kv_[hd]) * X_SCALE)
            out = out + _mm(_mm(p, kv_[X_HEADS + hd]), wo[cs, :])
        return (h_ + out,)
    (h2,), got = _hosted(_rows("xattn", f_xattn, s, min(X_ROWS, s),
                               [(h1, 'r0'), (pl_['norm_x_g'], 'f'), (wl['w_xq'], 'f'), (kvm, 'f'), (wl['w_xo'], 'f')],
                               [((s, D_MODEL), F32, 'r0')], fetch('xattn')))
    landed('xattn', got)

    def f_ffn(h_, g, wg, wu, wd):
        hn = _rms(h_, g[...])[0].astype(MXU)
        y = jnp.zeros(h_.shape, F32)
        gates, ups = [], []
        for c in range(D_FF // FF_CHUNK):
            cs = pl.ds(c * FF_CHUNK, FF_CHUNK)
            gate, up = _mm(hn, wg[:, cs]), _mm(hn, wu[:, cs])
            y = y + _mm(gate * jax.nn.sigmoid(gate) * up, wd[cs, :])
            gates.append(gate)
            ups.append(up)
        return h_ + y, jnp.concatenate(gates, axis=-1), jnp.concatenate(ups, axis=-1)
    (h3, gate_f, up_f), got = _hosted(_rows(
        "ffn", f_ffn, s, tm,
        [(h2, 'r0'), (pl_['norm_ffn_g'], 'f'), (wl['w_gate'], 'f'), (wl['w_up'], 'f'), (wl['w_down'], 'f')],
        [((s, D_MODEL), F32, 'r0'), ((s, D_FF), MXU, 'r0'), ((s, D_FF), MXU, 'r0')], fetch('ffn')))
    landed('ffn', got)
    sv.update(h=h, proj=proj, q=q, k=k, v=v, a_out=a_out, lse=lse, x_re=x_re, x_im=x_im, y_ssm=y_ssm,
              s_out=s_out, h1=h1, kvm=kvm, h2=h2, u_ch=u_ch, gate=gate_f, up=up_f)
    return h3, sv, wl, nxt_got


def _layer_bwd(dh3, sv, memx, tabs, wl, pl_, nxt=None):
    s = dh3.shape[0]
    tm = min(256, s)
    cos, sin, pmat, pmat_t = tabs
    gr = {}
    arrived = {}
    act_shape = (s, D_FF)

    def send(host):
        who, names = BWD_PLAN.get(host, (None, []))
        if who is None or (who == 'nxt' and not nxt):
            return None, []
        return (_scatter([nxt[n] if who == 'nxt' else _blocked(gr, n) for n in names]),
                [(who, n) for n in names])

    def f_ffn_bwd(h_, dy, gate_, up_, g, wg, wu, wd):
        hn, r = _rms(h_, g[...])
        hb = hn.astype(MXU)
        dyb = dy.astype(MXU)
        dhn = jnp.zeros(h_.shape, F32)
        acts, dgs, dus = [], [], []
        for c in range(D_FF // FF_CHUNK):
            cs = pl.ds(c * FF_CHUNK, FF_CHUNK)
            gate = _lanes(gate_, c, FF_CHUNK).astype(F32)
            up = _lanes(up_, c, FF_CHUNK).astype(F32)
            sg = jax.nn.sigmoid(gate)
            si = gate * sg
            dact = _mm_nt(dyb, wd[cs, :])
            dgate = (dact * up * (sg * (1.0 + gate * (1.0 - sg)))).astype(MXU)
            dup = (dact * si).astype(MXU)
            dhn = dhn + _mm_nt(dgate, wg[:, cs]) + _mm_nt(dup, wu[:, cs])
            acts.append((si * up).astype(MXU))
            dgs.append(dgate)
            dus.append(dup)
        dh, dg = _rms_bwd(h_, g[...], r, dhn)
        cat = lambda parts: jnp.concatenate(parts, axis=-1)
        return dy + dh, hb, cat(acts), cat(dgs), cat(dus), dg
    ex, keys = send('ffn_bwd')
    (dh2, hn_f, act, dgate, dup, gr['norm_ffn_g']), got = _hosted(_rows(
        "ffn_bwd", f_ffn_bwd, s, tm,
        [(sv['h2'], 'r0'), (dh3, 'r0'), (sv['gate'], 'r0'), (sv['up'], 'r0'), (pl_['norm_ffn_g'], 'f'),
         (wl['w_gate'], 'f'), (wl['w_up'], 'f'), (wl['w_down'], 'f')],
        [((s, D_MODEL), F32, 'r0'), ((s, D_MODEL), MXU, 'r0'), (act_shape, MXU, 'r0'), (act_shape, MXU, 'r0'),
         (act_shape, MXU, 'r0'), ((1, D_MODEL), F32, 'a')], ex))
    arrived.update(zip(keys, got))
    gr['w_gate'] = _mm_tn_call("dw_gate", hn_f, dgate, tn=FF_CHUNK)
    gr['w_up'] = _mm_tn_call("dw_up", hn_f, dup, tn=FF_CHUNK)
    gr['w_down'] = _mm_tn_call("dw_down", act, dh3, tk=FF_CHUNK)

    m_len = memx.shape[0]

    def f_xattn_bwd(h_, dy, g, wq, kv_, wo):
        hn, r = _rms(h_, g[...])
        hb = hn.astype(MXU)
        dyb = dy.astype(MXU)
        dhn = jnp.zeros(h_.shape, F32)
        dqs, ohs, dks, dvs = [], [], [], []
        for hd in range(X_HEADS):
            cs = pl.ds(hd * X_HEAD_DIM, X_HEAD_DIM)
            kh, vh = kv_[hd], kv_[X_HEADS + hd]
            qh = _mm(hb, wq[:, cs])
            p = _softmax(_mm_nt(qh, kh) * X_SCALE)
            ohs.append(_mm(p, vh).astype(MXU))
            do = _mm_nt(dyb, wo[cs, :])
            dvs.append(_mm_tn(p, do))
            dp = _mm_nt(do, vh)
            ds = p * (dp - jnp.sum(dp * p, axis=-1, keepdims=True)) * X_SCALE
            dq = _mm(ds, kh).astype(MXU)
            dks.append(_mm_tn(ds, qh))
            dhn = dhn + _mm_nt(dq, wq[:, cs])
            dqs.append(dq)
        dh, dg = _rms_bwd(h_, g[...], r, dhn)
        return (dy + dh, hb, jnp.concatenate(dqs, axis=-1), jnp.concatenate(ohs, axis=-1),
                jnp.stack(dks + dvs), dg)
    ex, keys = send('xattn_bwd')
    (dh1, hn_x, dq_x, oh_x, dkvm, gr['norm_x_g']), got = _hosted(_rows(
        "xattn_bwd", f_xattn_bwd, s, min(X_ROWS, s),
        [(sv['h1'], 'r0'), (dh2, 'r0'), (pl_['norm_x_g'], 'f'), (wl['w_xq'], 'f'), (sv['kvm'], 'f'),
         (wl['w_xo'], 'f')],
        [((s, D_MODEL), F32, 'r0'), ((s, D_MODEL), MXU, 'r0'), ((s, D_MODEL), MXU, 'r0'),
         ((s, D_MODEL), MXU, 'r0'), ((N_DEV, m_len, X_HEAD_DIM), F32, 'a'), ((1, D_MODEL), F32, 'a')], ex))
    arrived.update(zip(keys, got))
    gr['w_xq'] = _mm_tn_call("dw_xq", hn_x, dq_x)
    gr['w_xo'] = _mm_tn_call("dw_xo", oh_x, dh2)

    def f_memkv_bwd(mm_, dkv, g, w):
        mn, r = _rms(mm_, g[...])
        mb = mn.astype(MXU)
        dmn = jnp.zeros(mm_.shape, F32)
        dws = []
        for d in range(N_DEV):
            dmn = dmn + _mm_nt(dkv[d], w[d])
            dws.append(_mm_tn(mb, dkv[d]))
        _, dg = _rms_bwd(mm_, g[...], r, dmn)
        return jnp.stack(dws), dg
    gr['w_xkv'], gr['mem_norm_g'] = _rows(
        "mem_kv_bwd", f_memkv_bwd, m_len, m_len,
        [(memx, 'r0'), (dkvm, 'r1'), (pl_['mem_norm_g'], 'f'), (wl['w_xkv'], 'f')],
        [((N_DEV, D_MODEL, X_HEAD_DIM), F32, 'a'), ((1, D_MODEL), F32, 'a')])

    def f_mix_out_bwd(a, so, dy, ga, gs, w):
        dmix = _mm_nt(dy, w[...])
        an, ra = _rms(a, ga[...])
        sn, rs = _rms(so, gs[...])
        da, dga = _rms_bwd(a, ga[...], ra, dmix[:, 0:512])
        dso, dgs = _rms_bwd(so, gs[...], rs, dmix[:, 512:1024])
        return da, dso, jnp.concatenate([an, sn], axis=-1), dga, dgs
    da_out, ds_out, mixed, gr['attn_out_g'], gr['ssm_out_g'] = _rows(
        "mix_out_bwd", f_mix_out_bwd, s, tm,
        [(sv['a_out'], 'r0'), (sv['s_out'], 'r0'), (dh1, 'r0'), (pl_['attn_out_g'], 'f'), (pl_['ssm_out_g'], 'f'),
         (wl['w_out'], 'f')],
        [((s, 512), F32, 'r0'), ((s, 512), F32, 'r0'), ((s, D_MODEL), MXU, 'r0'), ((1, 512), F32, 'a'),
         ((1, 512), F32, 'a')])
    gr['w_out'] = _mm_tn_call("dw_out", mixed, dh1)

    ex, keys = send('flash_bwd')
    dq, dk, dv, got = _flash_bwd(sv['q'], sv['k'], sv['v'], sv['a_out'], sv['lse'], da_out, ex)
    arrived.update(zip(keys, got))

    def f_s5_out_bwd(xr, xi, u, y, ds, cre, cim, d, wglu, bglu):
        g, gelu_vjp = jax.vjp(jax.nn.gelu, y)
        sig = jax.nn.sigmoid(_mm(g, wglu[...]) + bglu[...])
        dz = ds * y * sig * (1.0 - sig)
        dy = ds * sig + gelu_vjp(_mm_nt(dz, wglu[...]))[0]
        dcr, dci = [], []
        for j in range(4):
            dyj = _lanes(dy, j, LANES)
            dcr.append(_mm_tn(_cat_blocks(xr, j), dyj))
            dci.append(_mm_tn(_cat_blocks(xi, j), dyj))
        return (dy, dy * d[...], jnp.stack(dcr), jnp.stack(dci),
                jnp.sum(dy * u, axis=0, keepdims=True), _mm_tn(g, dz), jnp.sum(dz, axis=0, keepdims=True))
    ex, keys = send('s5_out_bwd')
    (dy_ssm, du_dir, gr['c_re'], gr['c_im'], gr['ssm_d'], gr['ssm_w_glu'], gr['ssm_b_glu']), got = _hosted(_rows(
        "s5_out_bwd", f_s5_out_bwd, s, tm,
        [(sv['x_re'], 'r1'), (sv['x_im'], 'r1'), (sv['u_ch'], 'r0'), (sv['y_ssm'], 'r0'), (_to_chunks(ds_out), 'r0'),
         (pl_['c_re'], 'f'), (pl_['c_im'], 'f'), (pl_['ssm_d'], 'f'), (wl['ssm_w_glu'], 'f'),
         (pl_['ssm_b_glu'], 'f')],
        [((s, 512), F32, 'r0'), ((s, 512), F32, 'r0'), ((4, 512, LANES), F32, 'a'),
         ((4, 512, LANES), F32, 'a'), ((1, 512), F32, 'a'), ((512, 512), F32, 'a'), ((1, 512), F32, 'a')], ex))
    arrived.update(zip(keys, got))
    g_re, g_im = _scan(dy_ssm, pl_['c_re'], pl_['c_im'], pl_['a_re'], -pl_['a_im'], True)
    first_re = jnp.pad(sv['x_re'][:, s - 8:s - 1], ((0, 0), (1, 0), (0, 0)))
    first_im = jnp.pad(sv['x_im'][:, s - 8:s - 1], ((0, 0), (1, 0), (0, 0)))

    def f_s5_in_bwd(gre, gim, xr, xi, pr8, pi8, u, dud, f8r, f8i, bre, bim):
        first = pl.program_id(0) == 0
        xpr = jnp.concatenate([jnp.where(first, f8r[...], pr8), xr[:, :tm - 8]], axis=1)
        xpi = jnp.concatenate([jnp.where(first, f8i[...], pi8), xi[:, :tm - 8]], axis=1)
        dus, dbr, dbi = [], [], []
        for j in range(4):
            gj_r, gj_i, uj = _cat_blocks(gre, j), _cat_blocks(gim, j), _lanes(u, j, LANES)
            dus.append(_mm_nt(gj_r, bre[j]) + _mm_nt(gj_i, bim[j]))
            dbr.append(_mm_tn(uj, gj_r))
            dbi.append(_mm_tn(uj, gj_i))
        da_r = jnp.sum(gre * xpr + gim * xpi, axis=1, keepdims=True)
        da_i = jnp.sum(gim * xpr - gre * xpi, axis=1, keepdims=True)
        return dud + jnp.concatenate(dus, axis=-1), jnp.stack(dbr), jnp.stack(dbi), da_r, da_i
    ex, keys = send('s5_in_bwd')
    (du_ch, gr['b_re'], gr['b_im'], gr['a_re'], gr['a_im']), got = _hosted(_rows(
        "s5_in_bwd", f_s5_in_bwd, s, tm,
        [(g_re, 'r1'), (g_im, 'r1'), (sv['x_re'], 'r1'), (sv['x_im'], 'r1'), (sv['x_re'], 'p8'), (sv['x_im'], 'p8'),
         (sv['u_ch'], 'r0'), (du_dir, 'r0'), (first_re, 'f'), (first_im, 'f'), (pl_['b_re'], 'f'), (pl_['b_im'], 'f')],
        [((s, 512), F32, 'r0'), ((4, LANES, 512), F32, 'a'), ((4, LANES, 512), F32, 'a'),
         ((16, 1, LANES), F32, 'a'), ((16, 1, LANES), F32, 'a')], ex))
    arrived.update(zip(keys, got))
    du = _from_chunks(du_ch)

    def f_qkv_bwd(pr, cos_, sin_, dq_, dk_, dv_, gq, gkv, wq, wk, wv, pt):
        cq, ckv = pr[:, 0:Q_LORA], pr[:, Q_LORA:Q_LORA + KV_LORA]
        cqn, rq = _rms(cq, gq[...])
        kvn, rkv = _rms(ckv, gkv[...])
        cqb, kvb = cqn.astype(MXU), kvn.astype(MXU)
        dcqn = jnp.zeros(cq.shape, F32)
        dkvn = jnp.zeros(ckv.shape, F32)
        dksum = jnp.zeros(dk_[0].shape, F32)
        dwq, dwk, dwv = [], [], []
        for hd in range(MLA_HEADS):
            dqp = (_rope_t(dq_[hd], cos_, sin_, pt) * MLA_SCALE).astype(MXU)
            dkb, dvb = dk_[hd].astype(MXU), dv_[hd].astype(MXU)
            dwq.append(_mm_tn(cqb, dqp))
            dwk.append(_mm_tn(kvb, dkb))
            dwv.append(_mm_tn(kvb, dvb))
            dcqn = dcqn + _mm_nt(dqp, wq[hd])
            dkvn = dkvn + _mm_nt(dkb, wk[hd]) + _mm_nt(dvb, wv[hd])
            dksum = dksum + dk_[hd]
        dcq, dgq = _rms_bwd(cq, gq[...], rq, dcqn)
        dckv, dgkv = _rms_bwd(ckv, gkv[...], rkv, dkvn)
        dpa = jnp.concatenate([dcq, dckv, _rope_t(dksum, cos_, sin_, pt)], axis=-1)
        return dpa, jnp.stack(dwq), jnp.stack(dwk), jnp.stack(dwv), dgq, dgkv
    ex, keys = send('mla_qkv_bwd')
    (dpa, gr['w_uq'], gr['w_k'], gr['w_v'], gr['q_norm_g'], gr['kv_norm_g']), got = _hosted(_rows(
        "mla_qkv_bwd", f_qkv_bwd, s, tm,
        [(sv['proj'], 'r0'), (cos, 'r0'), (sin, 'r0'), (dq, 'r1'), (dk, 'r1'), (dv, 'r1'), (pl_['q_norm_g'], 'f'),
         (pl_['kv_norm_g'], 'f'), (wl['w_uq'], 'f'), (wl['w_k'], 'f'), (wl['w_v'], 'f'), (pmat_t, 'f')],
        [((s, 512), F32, 'r0'), ((MLA_HEADS, Q_LORA, HEAD_W), F32, 'a'), ((MLA_HEADS, KV_LORA, HEAD_W), F32, 'a'),
         ((MLA_HEADS, KV_LORA, HEAD_W), F32, 'a'), ((1, Q_LORA), F32, 'a'), ((1, KV_LORA), F32, 'a')], ex))
    arrived.update(zip(keys, got))

    def f_mix_in_bwd(h_, dpa_, du_, dres, g, w):
        dproj = jnp.concatenate([dpa_, du_], axis=-1).astype(MXU)
        xn, r = _rms(h_, g[...])
        dh, dg = _rms_bwd(h_, g[...], r, _mm_nt(dproj, w[...]))
        return dres + dh, xn, dproj, dg
    ex, keys = send('mix_in_bwd')
    (dh0, xn, dproj, gr['norm_mix_g']), got = _hosted(_rows(
        "mix_in_bwd", f_mix_in_bwd, s, tm,
        [(sv['h'], 'r0'), (dpa, 'r0'), (du, 'r0'), (dh1, 'r0'), (pl_['norm_mix_g'], 'f'), (wl['w_in'], 'f')],
        [((s, D_MODEL), F32, 'r0'), ((s, D_MODEL), MXU, 'r0'), ((s, D_MODEL), MXU, 'r0'), ((1, D_MODEL), F32, 'a')],
        ex))
    arrived.update(zip(keys, got))
    gr['w_in'] = _mm_tn_call("dw_in", xn, dproj)
    return dh0, gr, arrived


def _layer_weights(w):
    wl = {}
    if 'w_in' in w:
        w_in = w['w_in'].reshape(D_MODEL, -1)
        z = lambda n: jnp.zeros((D_MODEL, n), w_in.dtype)
        wl['w_in'] = jnp.concatenate([w_in[:, :384], z(64), w_in[:, 384:416], z(32), w_in[:, 416:]], axis=1)
    if 'w_uq' in w:
        wl['w_uq'] = jnp.pad(w['w_uq'], ((0, 0), (0, 0), (0, HEAD_W - QK_NOPE - QK_ROPE)))
    if 'w_ukv' in w:
        wl['w_k'] = jnp.pad(w['w_ukv'][..., :QK_NOPE], ((0, 0), (0, 0), (0, HEAD_W - QK_NOPE)))
        wv = w['w_ukv'][..., QK_NOPE:]
        even = (jnp.arange(MLA_HEADS) % 2 == 0)[:, None, None]
        wl['w_v'] = jnp.concatenate([jnp.where(even, wv, 0), jnp.where(even, 0, wv)], axis=-1).astype(wv.dtype)
    if 'ssm_w_glu' in w:
        wl['ssm_w_glu'] = w['ssm_w_glu'].reshape(SSM_WIDTH, SSM_WIDTH)
    for n in ('w_out', 'w_xq', 'w_xo'):
        if n in w:
            wl[n] = w[n].reshape(D_MODEL, D_MODEL)
    if 'w_xkv' in w:
        wl['w_xkv'] = w['w_xkv']
    for n in ('w_gate', 'w_up'):
        if n in w:
            wl[n] = jnp.transpose(w[n], (1, 0, 2)).reshape(D_MODEL, D_FF)
    if 'w_down' in w:
        wl['w_down'] = w['w_down'].reshape(D_FF, D_MODEL)
    return wl


def _blocked(gr, n):
    if n == 'w_in':
        d = gr['w_in']
        out = jnp.concatenate([d[:, :384], d[:, 448:480], d[:, 512:]], axis=1).reshape(N_DEV, 128, -1)
    elif n == 'w_uq':
        out = gr['w_uq'][..., :QK_NOPE + QK_ROPE]
    elif n == 'w_ukv':
        even = (jnp.arange(MLA_HEADS) % 2 == 0)[:, None, None]
        dv = gr['w_v']
        out = jnp.concatenate([gr['w_k'][..., :QK_NOPE], jnp.where(even, dv[..., :V_DIM], dv[..., V_DIM:])], axis=-1)
    elif n == 'ssm_w_glu':
        out = gr['ssm_w_glu'].reshape(N_DEV, SSM_WIDTH // N_DEV, SSM_WIDTH)
    elif n in ('w_out', 'w_xq', 'w_xo'):
        out = gr[n].reshape(N_DEV, D_MODEL // N_DEV, D_MODEL)
    elif n in ('w_gate', 'w_up'):
        out = jnp.transpose(gr[n].reshape(D_MODEL, N_DEV, D_FF // N_DEV), (1, 0, 2))
    elif n == 'w_down':
        out = gr[n].reshape(N_DEV, D_FF // N_DEV, D_MODEL)
    else:
        out = gr[n]
    return out.astype(MXU)


def kernel(x, mem, positions, norm_mix_g, w_in, q_norm_g, w_uq, kv_norm_g, w_ukv, ssm_lambda_re, ssm_lambda_im, ssm_log_dt, ssm_b_re, ssm_b_im, ssm_c_re, ssm_c_im, ssm_d, ssm_w_glu, ssm_b_glu, attn_out_g, ssm_out_g, w_out, norm_x_g, mem_norm_g, w_xq, w_xkv, w_xo, norm_ffn_g, w_gate, w_up, w_down, final_norm_g, loss_target, m_norm_mix_g, m_w_in, m_q_norm_g, m_w_uq, m_kv_norm_g, m_w_ukv, m_ssm_lambda_re, m_ssm_lambda_im, m_ssm_log_dt, m_ssm_b_re, m_ssm_b_im, m_ssm_c_re, m_ssm_c_im, m_ssm_d, m_ssm_w_glu, m_ssm_b_glu, m_attn_out_g, m_ssm_out_g, m_w_out, m_norm_x_g, m_mem_norm_g, m_w_xq, m_w_xkv, m_w_xo, m_norm_ffn_g, m_w_gate, m_w_up, m_w_down, m_final_norm_g, v_norm_mix_g, v_w_in, v_q_norm_g, v_w_uq, v_kv_norm_g, v_w_ukv, v_ssm_lambda_re, v_ssm_lambda_im, v_ssm_log_dt, v_ssm_b_re, v_ssm_b_im, v_ssm_c_re, v_ssm_c_im, v_ssm_d, v_ssm_w_glu, v_ssm_b_glu, v_attn_out_g, v_ssm_out_g, v_w_out, v_norm_x_g, v_mem_norm_g, v_w_xq, v_w_xkv, v_w_xo, v_norm_ffn_g, v_w_gate, v_w_up, v_w_down, v_final_norm_g):
    args = dict(locals())
    W = {n: args[n] for n in WEIGHTS}
    M = {n: args['m_' + n] for n in WEIGHTS}
    V = {n: args['v_' + n] for n in WEIGHTS}
    s = x.shape[1]
    h = x[0]
    memx = mem[0]

    freqs = ROPE_THETA ** (-jnp.arange(0, QK_ROPE, 2, dtype=F32) / QK_ROPE)
    ang = positions[0].astype(F32)[:, None] * freqs
    c16, s16 = jnp.cos(ang), jnp.sin(ang)
    cos = jnp.concatenate([jnp.ones((s, QK_NOPE), F32), c16, c16, jnp.zeros((s, 32), F32)], axis=1)
    sin = jnp.concatenate([jnp.zeros((s, QK_NOPE), F32), s16, s16, jnp.zeros((s, 32), F32)], axis=1)
    idx = jnp.arange(QK_ROPE // 2)
    pmat = jnp.zeros((HEAD_W, HEAD_W), F32)
    pmat = pmat.at[QK_NOPE + 16 + idx, QK_NOPE + idx].set(-1.0).at[QK_NOPE + idx, QK_NOPE + 16 + idx].set(1.0)
    tabs = (cos, sin, pmat, pmat.T)

    shards = [{n: W[n][l].astype(MXU) for n in SHARDED} for l in range(DEPTH)]
    gathered = dict(zip(EARLY, _run_exchange("gather_weights", _gather(_named(EARLY, shards[0])))))

    layers = []
    for l in range(DEPTH):
        wl = _layer_weights(gathered)
        s5_in = [W[n][l] for n in ('ssm_lambda_re', 'ssm_lambda_im', 'ssm_log_dt', 'ssm_b_re', 'ssm_b_im',
                                   'ssm_c_re', 'ssm_c_im')]
        (a_re, a_im, bre, bim, cre, cim), s5_vjp = jax.vjp(_s5_params, *s5_in)
        pl_ = {n: W[n][l][None] for n in ('norm_mix_g', 'q_norm_g', 'kv_norm_g', 'ssm_d', 'ssm_b_glu',
                                           'attn_out_g', 'ssm_out_g', 'norm_x_g', 'mem_norm_g', 'norm_ffn_g')}
        pl_.update(a_re=a_re, a_im=a_im, b_re=bre, b_im=bim, c_re=cre, c_im=cim)
        h, sv, wl, gathered = _layer_fwd(h, memx, tabs, wl, pl_, shards[l], shards[l + 1] if l + 1 < DEPTH else None)
        layers.append((wl, pl_, sv, s5_vjp))

    def f_loss(h_, tgt, g):
        y, r = _rms(h_, g[...])
        err = y - tgt
        part = 0.5 * jnp.sum(jnp.mean(err * err, axis=-1, keepdims=True), axis=0, keepdims=True)
        dh, dg = _rms_bwd(h_, g[...], r, err / D_MODEL)
        return dh, dg, jnp.broadcast_to(part, (8, LANES))
    dh, g_final, loss_part = _rows(
        "loss_head", f_loss, s, min(256, s), [(h, 'r0'), (loss_target[0], 'r0'), (final_norm_g[None], 'f')],
        [((s, D_MODEL), F32, 'r0'), ((1, D_MODEL), F32, 'a'), ((8, LANES), F32, 'a')])
    loss = lax.psum(loss_part[0, 0], ("x", "y", "c"))

    parts = [{} for _ in range(DEPTH)]
    g_rep = [None] * DEPTH
    blocks = None
    for l in reversed(range(DEPTH)):
        wl, pl_, sv, s5_vjp = layers[l]
        dh, gr, arrived = _layer_bwd(dh, sv, memx, tabs, wl, pl_, blocks)
        for (who, n), p in arrived.items():
            parts[l + 1 if who == 'nxt' else l][n] = p
        blocks = {n: _blocked(gr, n) for n in EARLY}
        ds5 = s5_vjp((gr['a_re'], gr['a_im'], gr['b_re'], gr['b_im'], gr['c_re'], gr['c_im']))
        rep = dict(zip(('ssm_lambda_re', 'ssm_lambda_im', 'ssm_log_dt', 'ssm_b_re', 'ssm_b_im', 'ssm_c_re',
                        'ssm_c_im'), ds5))
        for n in ('norm_mix_g', 'q_norm_g', 'kv_norm_g', 'ssm_d', 'ssm_b_glu', 'attn_out_g', 'ssm_out_g',
                  'norm_x_g', 'mem_norm_g', 'norm_ffn_g'):
            rep[n] = gr[n][0]
        g_rep[l] = rep
    grad_x = dh[None]

    rep_names = REPL_L + ['final_norm_g']
    g_loc = {n: jnp.stack([g_rep[l][n] for l in range(DEPTH)]) for n in REPL_L}
    g_loc['final_norm_g'] = g_final
    rest = [n for n in SHARDED if n not in parts[0]]
    last = _run_exchange("last_grads", _together(_scatter(_named(rest, blocks)),
                                                 _gather([_pack(_named(rep_names, g_loc))])))
    parts[0].update(zip(rest, last[:len(rest)]))

    out_sh = [{}, {}, {}, {}]
    for n in SHARDED:
        res = _adamw_weight("adamw_" + n, [parts[l][n] for l in range(DEPTH)], W[n], M[n], V[n])
        for kind, r in enumerate(res):
            out_sh[kind][n] = r

    shapes_rp = [(1,) + W[n].shape if W[n].ndim == 1 else W[n].shape for n in rep_names]
    g_rp = _unpack(_sum_sources("sum_small_grads", last[len(rest)]), shapes_rp)
    as_rows = lambda d: [d[n].reshape(shp) for n, shp in zip(rep_names, shapes_rp)]
    res_rp = (g_rp,) + _adamw_small("adamw_replicated", g_rp, as_rows(W), as_rows(M), as_rows(V))
    out_rp = [{n: a.reshape(W[n].shape) for n, a in zip(rep_names, r)} for r in res_rp]

    outs = [loss, grad_x]
    for kind in range(4):
        for n in WEIGHTS:
            outs.append(out_sh[kind][n] if n in SHARDED else out_rp[kind][n])
    return tuple(outs)
```

```python
from typing import Callable, NamedTuple

import jax
import jax.numpy as jnp
from jax import lax
from jax.experimental import pallas as pl
from jax.experimental.pallas import tpu as pltpu

F32 = jnp.float32
MXU = jnp.bfloat16

D_MODEL = 1024
MLA_HEADS = 8
QK_NOPE = 64
QK_ROPE = 32
V_DIM = 64
Q_LORA = 256
KV_LORA = 128
SSM_WIDTH = 512
SSM_GROUPS = 32
SSM_GROUP = 16
SSM_STATE = 64
X_HEADS = 4
X_HEAD_DIM = 256
D_FF = 2816
FF_CHUNK = D_FF // 2
ROPE_THETA = 10000.0
EPS = 1e-6
DEPTH = 2
N_DEV = 8
LANES = 128
HEAD_W = 128
MLA_SCALE = (QK_NOPE + QK_ROPE) ** -0.5
X_SCALE = X_HEAD_DIM ** -0.5
ADAM_LR, ADAM_B1, ADAM_B2, ADAM_EPS, ADAM_WD, ADAM_STEP = 0.001, 0.9, 0.999, 1e-08, 0.01, 10
VMEM_LIMIT = 56 * 1024 * 1024
FLASH_TILE = 512
DW_ROWS = 2048
X_ROWS = 512
MESH = pl.DeviceIdType.MESH

SHARDED = ['w_in', 'w_uq', 'w_ukv', 'ssm_w_glu', 'w_out', 'w_xq', 'w_xkv', 'w_xo', 'w_gate', 'w_up', 'w_down']
REPL_L = ['norm_mix_g', 'q_norm_g', 'kv_norm_g', 'ssm_lambda_re', 'ssm_lambda_im', 'ssm_log_dt', 'ssm_b_re',
          'ssm_b_im', 'ssm_c_re', 'ssm_c_im', 'ssm_d', 'ssm_b_glu', 'attn_out_g', 'ssm_out_g', 'norm_x_g',
          'mem_norm_g', 'norm_ffn_g']
WEIGHTS = ['norm_mix_g', 'w_in', 'q_norm_g', 'w_uq', 'kv_norm_g', 'w_ukv', 'ssm_lambda_re', 'ssm_lambda_im',
           'ssm_log_dt', 'ssm_b_re', 'ssm_b_im', 'ssm_c_re', 'ssm_c_im', 'ssm_d', 'ssm_w_glu', 'ssm_b_glu',
           'attn_out_g', 'ssm_out_g', 'w_out', 'norm_x_g', 'mem_norm_g', 'w_xq', 'w_xkv', 'w_xo', 'norm_ffn_g',
           'w_gate', 'w_up', 'w_down', 'final_norm_g']


def _pcall(body, **kw):
    return pl.pallas_call(body, **kw)


def _mm(a, b):
    return jnp.dot(a.astype(MXU), b.astype(MXU), preferred_element_type=F32)


def _mm_nt(a, b):
    return lax.dot_general(a.astype(MXU), b.astype(MXU), (((1,), (1,)), ((), ())), preferred_element_type=F32)


def _mm_tn(a, b):
    return lax.dot_general(a.astype(MXU), b.astype(MXU), (((0,), (0,)), ((), ())), preferred_element_type=F32)


def _rms(x, g):
    r = lax.rsqrt(jnp.mean(x * x, axis=-1, keepdims=True) + EPS)
    return x * r * g, r


def _rms_bwd(x, g, r, dy):
    dyg = dy * g
    dx = r * dyg - x * (r * r * r) * jnp.mean(dyg * x, axis=-1, keepdims=True)
    return dx, jnp.sum(dy * x * r, axis=0, keepdims=True)


ROT = QK_ROPE // 2


def _rope(x, cos, sin_lo, sin_hi):
    return x * cos + pltpu.roll(x, HEAD_W - ROT, 1) * sin_lo + pltpu.roll(x, ROT, 1) * sin_hi


def _rope_t(g, cos, sin_lo, sin_hi):
    return g * cos + pltpu.roll(g * sin_lo, ROT, 1) + pltpu.roll(g * sin_hi, HEAD_W - ROT, 1)


def _softmax(s):
    m = jnp.max(s, axis=-1, keepdims=True)
    e = jnp.exp(s - m)
    return e / jnp.sum(e, axis=-1, keepdims=True)


def _lanes(x, j, w):
    return x[:, j * w:(j + 1) * w]


def _rows(name, fn, n, tm, ins, outs, side=None):
    def spec(shape, kind):
        nd = len(shape)
        if kind == 'p8':
            return pl.BlockSpec((shape[0], 8, shape[2]), lambda i: (0, jnp.maximum(i * (tm // 8) - 1, 0), 0))
        if kind == 'f':
            return pl.BlockSpec(shape, lambda i, _nd=nd: (0,) * _nd, pipeline_mode=pl.Buffered(1))
        if kind == 'a':
            return pl.BlockSpec(shape, lambda i, _nd=nd: (0,) * _nd)
        ax = int(kind[1])
        blk = tuple(tm if d == ax else s for d, s in enumerate(shape))
        return pl.BlockSpec(blk, lambda i, _ax=ax, _nd=nd: tuple(i if d == _ax else 0 for d in range(_nd)))

    n_in, n_out, n_steps = len(ins), len(outs), n // tm

    def body(*refs):
        in_refs, out_refs, steps = _side_split(refs, n_in, n_out, side)
        i = pl.program_id(0)
        if steps:
            pl.when(i == 0)(steps[0])
            pl.when(i == _pass_on_step(n_steps))(steps[1])
        args = [r if k == 'f' else r[...] for r, (_, k) in zip(in_refs, ins)]
        res = fn(*args)
        for r, (_, dt, k), v in zip(out_refs, outs, res):
            if k == 'a':
                _accumulate(r, v.astype(dt), i)
            else:
                r[...] = v.astype(dt)
        if steps:
            pl.when(i == n_steps - 1)(steps[2])

    s_in, s_out, s_shape, s_sems, s_ops = _side_args(side)
    res = _pcall(
        body, name=name + ("_x" if side else ""), grid=(n_steps,),
        in_specs=[spec(a.shape, k) for a, k in ins] + s_in,
        out_specs=[spec(s, k) for s, _, k in outs] + s_out,
        out_shape=[jax.ShapeDtypeStruct(s, dt) for s, dt, _ in outs] + s_shape,
        scratch_shapes=s_sems,
        compiler_params=pltpu.CompilerParams(dimension_semantics=("arbitrary",), vmem_limit_bytes=VMEM_LIMIT),
    )(*[a for a, _ in ins], *s_ops)
    return _Hosted(res[:n_out], res[n_out:]) if side else res


def _accumulate(ref, v, i):
    @pl.when(i == 0)
    def _():
        ref[...] = v

    @pl.when(i != 0)
    def _():
        ref[...] += v


def _mm_tn_call(name, a, b, tk=None, tn=None):
    out_dtype = MXU
    s, k = a.shape
    n = b.shape[1]
    tk, tn = tk or k, tn or n
    ts = min(DW_ROWS, s)
    ns = s // ts

    def body(a_ref, b_ref, o_ref, acc_ref):
        j = pl.program_id(2)
        _accumulate(acc_ref, _mm_tn(a_ref[...], b_ref[...]), j)

        @pl.when(j == ns - 1)
        def _():
            o_ref[...] = acc_ref[...].astype(out_dtype)

    return _pcall(
        body, name=name, grid=(k // tk, n // tn, ns),
        in_specs=[pl.BlockSpec((ts, tk), lambda ik, jn, j: (j, ik)),
                  pl.BlockSpec((ts, tn), lambda ik, jn, j: (j, jn))],
        out_specs=pl.BlockSpec((tk, tn), lambda ik, jn, j: (ik, jn)),
        out_shape=jax.ShapeDtypeStruct((k, n), out_dtype),
        scratch_shapes=[pltpu.VMEM((tk, tn), F32)],
        compiler_params=pltpu.CompilerParams(dimension_semantics=("arbitrary", "arbitrary", "arbitrary"),
                                             vmem_limit_bytes=VMEM_LIMIT),
    )(a, b)


def _side_split(refs, n_in, n_out, side):
    if side is None:
        return refs[:n_in], refs[n_in:n_in + n_out], None
    si, so = len(side.ins), len(side.out_shapes)
    own_in, side_in = refs[:n_in], refs[n_in:n_in + si]
    own_out, side_out = refs[n_in + si:n_in + si + n_out], refs[n_in + si + n_out:n_in + si + n_out + so]
    return own_in, own_out, side.steps(side_in, side_out, refs[n_in + si + n_out + so:])


def _pass_on_step(n_steps):
    return max(n_steps - 2, 0)


def _side_args(side):
    if side is None:
        return [], [], [], [], []
    any_spec = pl.BlockSpec(memory_space=pl.ANY)
    return ([any_spec] * len(side.ins), [any_spec] * len(side.out_shapes), list(side.out_shapes),
            list(side.sem_shapes), list(side.ins))


class _Hosted(NamedTuple):
    results: list
    arrived: list


def _hosted(res):
    return res if isinstance(res, _Hosted) else _Hosted(res, ())


def _flash_fwd(q, k, v, side=None):
    nh, s, w = q.shape
    t = min(FLASH_TILE, s)
    nq = s // t
    n_steps = (nh // 2) * nq

    def body(*refs):
        (q_ref, k_ref, v_ref), (o_ref, lse_ref), steps = _side_split(refs, 3, 2, side)
        step = pl.program_id(0) * nq + pl.program_id(1)
        if steps:
            pl.when(step == 0)(steps[0])
            pl.when(step == _pass_on_step(n_steps))(steps[1])
        qi = pl.program_id(1)
        qs = [q_ref[0], q_ref[1]]
        below = lax.broadcasted_iota(jnp.int32, (t, t), 1) <= lax.broadcasted_iota(jnp.int32, (t, t), 0)

        def tile(j, carry, diagonal):
            sl = pl.ds(pl.multiple_of(j * t, t), t)
            out = []
            for hh in range(2):
                m, l, acc = carry[3 * hh:3 * hh + 3]
                sc = _mm_nt(qs[hh], k_ref[hh, sl, :])
                if diagonal:
                    sc = jnp.where(below, sc, -1e30)
                m_new = jnp.maximum(m, jnp.max(sc, axis=1, keepdims=True))
                p = jnp.exp(sc - m_new)
                alpha = jnp.exp(m - m_new)
                out += [m_new, alpha * l + jnp.sum(p, axis=1, keepdims=True), alpha * acc + _mm(p, v_ref[hh, sl, :])]
            return tuple(out)

        init = (jnp.full((t, 1), -1e30, F32), jnp.zeros((t, 1), F32), jnp.zeros((t, w), F32)) * 2
        carry = lax.fori_loop(0, qi, lambda j, c: tile(j, c, False), init)
        carry = tile(qi, carry, True)
        o_ref[...] = carry[2] / carry[1] + carry[5] / carry[4]
        for hh in range(2):
            lse_ref[hh] = jnp.broadcast_to(carry[3 * hh] + jnp.log(carry[3 * hh + 1]), (t, w))
        if steps:
            pl.when(step == n_steps - 1)(steps[2])

    s_in, s_out, s_shape, s_sems, s_ops = _side_args(side)
    res = _pcall(
        body, name="mla_flash_fwd" + ("_x" if side else ""), grid=(nh // 2, nq),
        in_specs=[pl.BlockSpec((2, t, w), lambda p, i: (p, i, 0)),
                  pl.BlockSpec((2, s, w), lambda p, i: (p, 0, 0)),
                  pl.BlockSpec((2, s, w), lambda p, i: (p, 0, 0))] + s_in,
        out_specs=[pl.BlockSpec((t, w), lambda p, i: (i, p)),
                   pl.BlockSpec((2, t, w), lambda p, i: (p, i, 0))] + s_out,
        out_shape=[jax.ShapeDtypeStruct((s, (nh // 2) * w), F32), jax.ShapeDtypeStruct((nh, s, w), F32)] + s_shape,
        scratch_shapes=s_sems,
        compiler_params=pltpu.CompilerParams(dimension_semantics=("arbitrary", "arbitrary"),
                                             vmem_limit_bytes=VMEM_LIMIT),
    )(q, k, v, *s_ops)
    return res[0], res[1], res[2:]


def _flash_bwd(q, k, v, o, lse, do, side=None):
    nh, s, w = q.shape
    t = min(FLASH_TILE, s)
    nq = s // t
    n_steps = (nh // 2) * nq

    def body(*refs):
        (q_ref, k_ref, v_ref, o_ref, lse_ref, do_ref), (dq_ref, dk_ref, dv_ref), steps = _side_split(refs, 6, 3, side)
        step = pl.program_id(0) * nq + pl.program_id(1)
        if steps:
            pl.when(step == 0)(steps[0])
            pl.when(step == _pass_on_step(n_steps))(steps[1])
        j = pl.program_id(1)

        @pl.when(j == 0)
        def _():
            dq_ref[...] = jnp.zeros(dq_ref.shape, F32)

        below = lax.broadcasted_iota(jnp.int32, (t, t), 1) <= lax.broadcasted_iota(jnp.int32, (t, t), 0)
        lane = lax.broadcasted_iota(jnp.int32, (t, w), 1)
        heads = [jnp.logical_and(lane >= hh * V_DIM, lane < (hh + 1) * V_DIM) for hh in range(2)]
        ks = [k_ref[0], k_ref[1]]
        vs = [v_ref[0], v_ref[1]]

        def tile(i, carry, diagonal):
            sl = pl.ds(pl.multiple_of(i * t, t), t)
            dout_all, o_all = do_ref[sl, :], o_ref[sl, :]
            out = []
            for hh in range(2):
                dk, dv = carry[2 * hh], carry[2 * hh + 1]
                qh = q_ref[hh, sl, :]
                dout = jnp.where(heads[hh], dout_all, 0.0)
                sc = _mm_nt(qh, ks[hh])
                if diagonal:
                    sc = jnp.where(below, sc, -1e30)
                p = jnp.exp(sc - lse_ref[hh, sl, 0:1])
                dp = _mm_nt(dout, vs[hh])
                ds = p * (dp - jnp.sum(dout * o_all, axis=1, keepdims=True))
                dq_ref[hh, sl, :] += _mm(ds, ks[hh])
                out += [dk + _mm_tn(ds, qh), dv + _mm_tn(p, dout)]
            return tuple(out)

        carry = tile(j, (jnp.zeros((t, w), F32),) * 4, True)
        carry = lax.fori_loop(j + 1, nq, lambda i, c: tile(i, c, False), carry)
        for hh in range(2):
            dk_ref[hh] = carry[2 * hh]
            dv_ref[hh] = jnp.where(heads[hh], carry[2 * hh + 1], 0.0)
        if steps:
            pl.when(step == n_steps - 1)(steps[2])

    s_in, s_out, s_shape, s_sems, s_ops = _side_args(side)
    res = _pcall(
        body, name="mla_flash_bwd" + ("_x" if side else ""), grid=(nh // 2, nq),
        in_specs=[pl.BlockSpec((2, s, w), lambda p, j: (p, 0, 0)),
                  pl.BlockSpec((2, t, w), lambda p, j: (p, j, 0)),
                  pl.BlockSpec((2, t, w), lambda p, j: (p, j, 0)),
                  pl.BlockSpec((s, w), lambda p, j: (0, p)),
                  pl.BlockSpec((2, s, w), lambda p, j: (p, 0, 0)),
                  pl.BlockSpec((s, w), lambda p, j: (0, p))] + s_in,
        out_specs=[pl.BlockSpec((2, s, w), lambda p, j: (p, 0, 0)),
                   pl.BlockSpec((2, t, w), lambda p, j: (p, j, 0)),
                   pl.BlockSpec((2, t, w), lambda p, j: (p, j, 0))] + s_out,
        out_shape=[jax.ShapeDtypeStruct((nh, s, w), F32)] * 3 + s_shape,
        scratch_shapes=s_sems,
        compiler_params=pltpu.CompilerParams(dimension_semantics=("arbitrary", "arbitrary"),
                                             vmem_limit_bytes=VMEM_LIMIT),
    )(q, k, v, o, lse, do, *s_ops)
    return res[0], res[1], res[2], res[3:]


def _scan(src, w_re, w_im, a_re, a_im, reverse):
    s = src.shape[0]
    nb, w = a_re.shape[0], LANES
    ch = s // 8
    assert ch & (ch - 1) == 0
    grp = 4
    tr = min(512, s)

    def cmul(ar, ai, xr, xi):
        return ar * xr - ai * xi, ar * xi + ai * xr

    def body(src_ref, wr_ref, wi_ref, ar_ref, ai_ref, xr_ref, xi_ref):
        def project(c, carry):
            rows = pl.ds(pl.multiple_of(c * tr, tr), tr)
            u = src_ref[rows, :]
            if reverse:
                br, bi = _mm_nt(u, wr_ref[...]), _mm_nt(u, wi_ref[...])
            else:
                br, bi = _mm(u, wr_ref[...]), _mm(u, wi_ref[...])
            for g in range(grp):
                xr_ref[g, rows, :] = _lanes(br, g, w)
                xi_ref[g, rows, :] = _lanes(bi, g, w)
            return carry

        lax.fori_loop(0, s // tr, project, 0)
        sub = lax.broadcasted_iota(jnp.int32, (8, w), 0)

        def shift(x, k):
            if reverse:
                return jnp.where(sub < 8 - k, pltpu.roll(x, 8 - k, 0), 0.0)
            return jnp.where(sub >= k, pltpu.roll(x, k, 0), 0.0)

        ar = [jnp.broadcast_to(ar_ref[g], (8, w)) for g in range(grp)]
        ai = [jnp.broadcast_to(ai_ref[g], (8, w)) for g in range(grp)]

        def tsl(i):
            return pl.ds(pl.multiple_of(((ch - 1 - i) if reverse else i) * 8, 8), 8)

        def local(i, carry):
            out = []
            for g in range(grp):
                xr, xi = carry[2 * g], carry[2 * g + 1]
                pr, pi = cmul(ar[g], ai[g], xr, xi)
                nr = pr + xr_ref[g, tsl(i), :]
                ni = pi + xi_ref[g, tsl(i), :]
                xr_ref[g, tsl(i), :] = nr
                xi_ref[g, tsl(i), :] = ni
                out += [nr, ni]
            return tuple(out)

        fin = lax.fori_loop(0, ch, local, (jnp.zeros((8, w), F32),) * (2 * grp))

        carry_in = []
        for g in range(grp):
            pr, pi = ar[g], ai[g]
            for _ in range(ch.bit_length() - 1):
                pr, pi = cmul(pr, pi, pr, pi)
            fr, fi = fin[2 * g], fin[2 * g + 1]
            for kk in (1, 2, 4):
                sr, si = cmul(pr, pi, shift(fr, kk), shift(fi, kk))
                fr, fi = fr + sr, fi + si
                pr, pi = cmul(pr, pi, pr, pi)
            carry_in += [shift(fr, 1), shift(fi, 1)]

        def fix(i, pw):
            out = []
            for g in range(grp):
                pr, pi = pw[2 * g], pw[2 * g + 1]
                cr, ci = cmul(pr, pi, carry_in[2 * g], carry_in[2 * g + 1])
                xr_ref[g, tsl(i), :] = xr_ref[g, tsl(i), :] + cr
                xi_ref[g, tsl(i), :] = xi_ref[g, tsl(i), :] + ci
                nr, ni = cmul(pr, pi, ar[g], ai[g])
                out += [nr, ni]
            return tuple(out)

        lax.fori_loop(0, ch, fix, tuple(x for g in range(grp) for x in (ar[g], ai[g])))

    per_j = 4 // grp
    blk = pl.BlockSpec((grp, s, w), lambda i: (i, 0, 0))
    ablk = pl.BlockSpec((grp, 1, w), lambda i: (i, 0, 0))
    sblk = pl.BlockSpec((s, w), lambda i: (0, i // per_j))
    if reverse:
        wblk = pl.BlockSpec((None, grp * w, w), lambda i: (i // per_j, i % per_j, 0))
    else:
        wblk = pl.BlockSpec((None, w, grp * w), lambda i: (i // per_j, 0, i % per_j))
    return _pcall(
        body, name="s5_scan_rev" if reverse else "s5_scan", grid=(nb // grp,),
        in_specs=[sblk, wblk, wblk, ablk, ablk], out_specs=[blk, blk],
        out_shape=[jax.ShapeDtypeStruct((nb, s, w), F32)] * 2,
        compiler_params=pltpu.CompilerParams(dimension_semantics=("arbitrary",), vmem_limit_bytes=VMEM_LIMIT),
    )(src, w_re, w_im, a_re, a_im)


class _Exchange(NamedTuple):
    ins: list
    out_shapes: list
    sem_shapes: list
    steps: Callable


def _gather_steps(ins, outs, sems):
    n = len(ins)
    send_sems, recv_sems, local_sems = sems
    x, y, c = lax.axis_index("x"), lax.axis_index("y"), lax.axis_index("c")
    me, sibling = (x, y, c), (x, y, 1 - c)
    chips = [(1 - x, y), (x, 1 - y), (1 - x, 1 - y)]

    def copy(a, k, block, to, src=None):
        dst = outs[a].at[4 * block[0] + 2 * block[1] + block[2]]
        return pltpu.make_async_remote_copy(
            src_ref=dst if src is None else src, dst_ref=dst,
            send_sem=send_sems.at[a, k], recv_sem=recv_sems.at[a, k], device_id=to, device_id_type=MESH)

    mine = [pltpu.make_async_copy(ins[a], outs[a].at[4 * x + 2 * y + c], local_sems.at[a]) for a in range(n)]
    first = []
    for a in range(n):
        first.append(copy(a, 0, me, sibling, src=ins[a]))
        first += [copy(a, 1 + j, me, (*chip, c), src=ins[a]) for j, chip in enumerate(chips)]
    passed = [copy(a, 4 + j, (*chip, c), sibling) for j, chip in enumerate(chips) for a in range(n)]

    def start():
        for cp in mine + first:
            cp.start()

    def pass_on():
        i = 0
        for j, chip in enumerate(chips):
            for a in range(n):
                copy(a, 1 + j, (*chip, c), me).wait_recv()
                passed[i].start()
                i += 1

    def finish():
        for a in range(n):
            copy(a, 0, sibling, me).wait_recv()
            for j, chip in enumerate(chips):
                copy(a, 4 + j, (*chip, 1 - c), me).wait_recv()
        for cp in first + passed:
            cp.wait_send()
        for cp in mine:
            cp.wait()

    return start, pass_on, finish


def _gather(arrs):
    n = len(arrs)
    return _Exchange(list(arrs), [jax.ShapeDtypeStruct((N_DEV,) + a.shape, a.dtype) for a in arrs],
                     [pltpu.SemaphoreType.DMA((n, 7)), pltpu.SemaphoreType.DMA((n, 7)), pltpu.SemaphoreType.DMA((n,))],
                     _gather_steps)


def _scatter_steps(ins, outs, sems):
    n = len(ins)
    send_sems, recv_sems, local_sems = sems
    x, y, c = lax.axis_index("x"), lax.axis_index("y"), lax.axis_index("c")
    me = 4 * x + 2 * y + c
    own, sent, arrivals = [], [], []
    for a in range(n):
        own.append(pltpu.make_async_copy(ins[a].at[me], outs[a].at[me], local_sems.at[a]))
        for k in range(1, N_DEV):
            px, py, pc = x ^ ((k >> 2) & 1), y ^ ((k >> 1) & 1), c ^ (k & 1)
            peer = 4 * px + 2 * py + pc
            sent.append(pltpu.make_async_remote_copy(
                src_ref=ins[a].at[peer], dst_ref=outs[a].at[me],
                send_sem=send_sems.at[a, k - 1], recv_sem=recv_sems.at[a, k - 1],
                device_id=(px, py, pc), device_id_type=MESH))
            arrivals.append(pltpu.make_async_remote_copy(
                src_ref=ins[a].at[me], dst_ref=outs[a].at[peer],
                send_sem=send_sems.at[a, k - 1], recv_sem=recv_sems.at[a, k - 1],
                device_id=(x, y, c), device_id_type=MESH))

    def start():
        for cp in own + sent:
            cp.start()

    def pass_on():
        pass

    def finish():
        for cp in arrivals:
            cp.wait_recv()
        for cp in sent:
            cp.wait_send()
        for cp in own:
            cp.wait()

    return start, pass_on, finish


def _scatter(grads):
    n = len(grads)
    return _Exchange(list(grads), [jax.ShapeDtypeStruct(g.shape, g.dtype) for g in grads],
                     [pltpu.SemaphoreType.DMA((n, N_DEV - 1)), pltpu.SemaphoreType.DMA((n, N_DEV - 1)),
                      pltpu.SemaphoreType.DMA((n,))], _scatter_steps)


def _together(a, b):
    def steps(ins, outs, sems):
        sa = a.steps(ins[:len(a.ins)], outs[:len(a.out_shapes)], sems[:len(a.sem_shapes)])
        sb = b.steps(ins[len(a.ins):], outs[len(a.out_shapes):], sems[len(a.sem_shapes):])

        def both(k):
            def run():
                sa[k]()
                sb[k]()
            return run
        return both(0), both(1), both(2)

    return _Exchange(a.ins + b.ins, a.out_shapes + b.out_shapes, a.sem_shapes + b.sem_shapes, steps)


def _run_exchange(name, ex):
    n_in, n_out = len(ex.ins), len(ex.out_shapes)

    def body(*refs):
        for step in ex.steps(refs[:n_in], refs[n_in:n_in + n_out], refs[n_in + n_out:]):
            step()

    any_spec = pl.BlockSpec(memory_space=pl.ANY)
    return _pcall(body, name=name, in_specs=[any_spec] * n_in, out_specs=[any_spec] * n_out,
                  out_shape=list(ex.out_shapes), scratch_shapes=list(ex.sem_shapes))(*ex.ins)


def _adam_math(g, w_, m_, v_):
    m_new = ADAM_B1 * m_ + (1.0 - ADAM_B1) * g
    v_new = ADAM_B2 * v_ + (1.0 - ADAM_B2) * (g * g)
    m_hat = m_new / (1.0 - ADAM_B1 ** ADAM_STEP)
    v_hat = v_new / (1.0 - ADAM_B2 ** ADAM_STEP)
    delta = -ADAM_LR * (m_hat / (jnp.sqrt(v_hat) + ADAM_EPS) + ADAM_WD * w_)
    return delta, m_new, v_new


def _adamw_weight(name, parts, w, m, v):
    nl = len(parts)

    def body(*refs):
        p_refs = refs[:nl]
        w_ref, m_ref, v_ref, g_ref, d_ref, mo_ref, vo_ref = refs[nl:]
        for l in range(nl):
            g = p_refs[l][0].astype(F32)
            for j in range(1, N_DEV):
                g = g + p_refs[l][j].astype(F32)
            g_ref[l] = g
            d_ref[l], mo_ref[l], vo_ref[l] = _adam_math(g, w_ref[l], m_ref[l], v_ref[l])

    return _pcall(
        body, name=name, out_shape=[jax.ShapeDtypeStruct(w.shape, F32)] * 4,
        compiler_params=pltpu.CompilerParams(vmem_limit_bytes=VMEM_LIMIT),
    )(*parts, w, m, v)

def _sum_sources(name, parts):
    r = parts.shape[1]

    def body(p_ref, g_ref):
        g = p_ref[0]
        for j in range(1, N_DEV):
            g = g + p_ref[j]
        g_ref[...] = g

    return _pcall(body, name=name, out_shape=jax.ShapeDtypeStruct((r, LANES), F32),
                  compiler_params=pltpu.CompilerParams(vmem_limit_bytes=VMEM_LIMIT))(parts)


def _adamw_small(name, g, w, m, v):
    n = len(g)

    def body(*refs):
        g_r, w_r, m_r, v_r = (refs[k * n:(k + 1) * n] for k in range(4))
        d_r, mo_r, vo_r = (refs[k * n:(k + 1) * n] for k in range(4, 7))
        for i in range(n):
            d_r[i][...], mo_r[i][...], vo_r[i][...] = _adam_math(g_r[i][...], w_r[i][...], m_r[i][...], v_r[i][...])

    res = _pcall(body, name=name, out_shape=[jax.ShapeDtypeStruct(a.shape, F32) for a in w] * 3,
                 compiler_params=pltpu.CompilerParams(vmem_limit_bytes=VMEM_LIMIT))(*g, *w, *m, *v)
    return res[:n], res[n:2 * n], res[2 * n:]


def _pack(arrs):
    flat = jnp.concatenate([a.reshape(-1) for a in arrs])
    flat = jnp.pad(flat, (0, (-flat.shape[0]) % (8 * LANES)))
    return flat.reshape(-1, LANES)


def _unpack(packed, shapes):
    flat = packed.reshape(-1)
    out, off = [], 0
    for shp in shapes:
        size = 1
        for d in shp:
            size *= d
        out.append(flat[off:off + size].reshape(shp))
        off += size
    return out


def _s5_params(lam_re, lam_im, log_dt, b_re, b_im, c_re, c_im):
    dt = jnp.exp(log_dt)[:, None]
    e = jnp.exp(lam_re * dt)
    ang = lam_im * dt
    a_re, a_im = e * jnp.cos(ang), e * jnp.sin(ang)
    nr, ni = a_re - 1.0, a_im
    den = lam_re * lam_re + lam_im * lam_im
    cr = ((nr * lam_re + ni * lam_im) / den)[..., None]
    ci = ((ni * lam_re - nr * lam_im) / den)[..., None]
    bb_re = cr * b_re - ci * b_im
    bb_im = cr * b_im + ci * b_re
    eye = jnp.eye(8, dtype=F32)[None, :, None, :, None]

    def bblk(bb):
        t = jnp.transpose(bb.reshape(4, 8, SSM_STATE, SSM_GROUP), (0, 3, 1, 2))
        return (eye * t[:, None]).reshape(4, 8 * SSM_GROUP, 8 * SSM_STATE)

    def cblk(cc):
        t = jnp.transpose(cc.reshape(4, 8, SSM_GROUP, SSM_STATE), (0, 3, 1, 2))
        return (eye * t[:, None]).reshape(4, 8 * SSM_STATE, 8 * SSM_GROUP)

    nb = SSM_GROUPS * SSM_STATE // LANES
    return (a_re.reshape(nb, 1, LANES), a_im.reshape(nb, 1, LANES), bblk(bb_re), bblk(bb_im),
            cblk(c_re), -cblk(c_im))


def _cat_blocks(x3, j):
    return jnp.concatenate([x3[4 * j + k] for k in range(4)], axis=-1)


def _to_chunks(a):
    s, c = a.shape
    return a.reshape(8, s // 8, c).transpose(1, 0, 2).reshape(s, c)


def _from_chunks(a):
    s, c = a.shape
    return a.reshape(s // 8, 8, c).transpose(1, 0, 2).reshape(s, c)


EARLY = ['w_in', 'w_uq', 'w_ukv']

FWD_PLAN = {
    'flash': ('late', ['ssm_w_glu', 'w_out', 'w_xq', 'w_xkv', 'w_xo', 'w_gate', 'w_up', 'w_down']),
    'ffn': ('nxt', EARLY),
}
BWD_PLAN = {
    'ffn_bwd': ('nxt', EARLY),
    'xattn_bwd': ('own', ['w_down']),
    'flash_bwd': ('own', ['w_gate', 'w_up', 'w_xq', 'w_xkv', 'w_xo']),
    'mix_in_bwd': ('own', ['ssm_w_glu', 'w_out']),
}


def _named(names, d):
    return [d[n] for n in names]


def _layer_fwd(h, memx, tabs, wl, pl_, late=None, nxt=None):
    s = h.shape[0]
    tm = min(256, s)
    cos, sin_lo, sin_hi = tabs
    sv = {}
    wl = dict(wl)
    nxt_got = {}

    def fetch(host):
        who, names = FWD_PLAN.get(host, (None, []))
        src = late if who == 'late' else nxt if who == 'nxt' else None
        return _gather(_named(names, src)) if src else None

    def landed(host, got):
        who, names = FWD_PLAN.get(host, (None, []))
        if got and who == 'late':
            wl.update(_layer_weights(dict(zip(names, got))))
        elif got:
            nxt_got.update(zip(names, got))

    def f_mix_in(h_, g, w):
        xn, _ = _rms(h_, g[...])
        return (_mm(xn, w[...]),)
    proj, = _rows("mix_in", f_mix_in, s, tm, [(h, 'r0'), (pl_['norm_mix_g'], 'f'), (wl['w_in'], 'f')],
                  [((s, D_MODEL), F32, 'r0')])

    def f_qkv(pr, cos_, slo, shi, gq, gkv, wq, wk, wv):
        cqn = _rms(pr[:, 0:Q_LORA], gq[...])[0].astype(MXU)
        kvn = _rms(pr[:, Q_LORA:Q_LORA + KV_LORA], gkv[...])[0].astype(MXU)
        krr = _rope(pr[:, 384:512], cos_, slo, shi)
        qs, ks, vs = [], [], []
        for hd in range(MLA_HEADS):
            qs.append(_rope(_mm(cqn, wq[hd]), cos_, slo, shi) * MLA_SCALE)
            ks.append(_mm(kvn, wk[hd]) + krr)
            vs.append(_mm(kvn, wv[hd]))
        return jnp.stack(qs), jnp.stack(ks), jnp.stack(vs)
    hshape = (MLA_HEADS, s, HEAD_W)
    (q, k, v), got = _hosted(_rows(
        "mla_qkv", f_qkv, s, tm,
        [(proj, 'r0'), (cos, 'r0'), (sin_lo, 'r0'), (sin_hi, 'r0'), (pl_['q_norm_g'], 'f'), (pl_['kv_norm_g'], 'f'),
         (wl['w_uq'], 'f'), (wl['w_k'], 'f'), (wl['w_v'], 'f')],
        [(hshape, MXU, 'r1')] * 3, fetch('mla_qkv')))
    landed('mla_qkv', got)

    a_out, lse, got = _flash_fwd(q, k, v, fetch('flash'))
    landed('flash', got)

    u_ch = _to_chunks(proj[:, 512:1024])

    x_re, x_im = _scan(u_ch, pl_['b_re'], pl_['b_im'], pl_['a_re'], pl_['a_im'], False)

    def f_s5_out(xr, xi, u, cre, cim, d, wglu, bglu):
        y = jnp.concatenate([_mm(_cat_blocks(xr, j), cre[j]) + _mm(_cat_blocks(xi, j), cim[j])
                             for j in range(4)], axis=-1) + d[...] * u
        z = _mm(jax.nn.gelu(y), wglu[...]) + bglu[...]
        return y, y * jax.nn.sigmoid(z)
    (y_ssm, s_out_ch), got = _hosted(_rows(
        "s5_out", f_s5_out, s, tm,
        [(x_re, 'r1'), (x_im, 'r1'), (u_ch, 'r0'), (pl_['c_re'], 'f'), (pl_['c_im'], 'f'),
         (pl_['ssm_d'], 'f'), (wl['ssm_w_glu'], 'f'), (pl_['ssm_b_glu'], 'f')],
        [((s, SSM_WIDTH), F32, 'r0')] * 2, fetch('s5_out')))
    landed('s5_out', got)
    s_out = _from_chunks(s_out_ch)

    def f_mix_out(h_, a, so, ga, gs, w):
        an = _rms(a, ga[...])[0]
        sn = _rms(so, gs[...])[0]
        return (h_ + _mm(jnp.concatenate([an, sn], axis=-1), w[...]),)
    (h1,), got = _hosted(_rows("mix_out", f_mix_out, s, tm,
                               [(h, 'r0'), (a_out, 'r0'), (s_out, 'r0'), (pl_['attn_out_g'], 'f'),
                                (pl_['ssm_out_g'], 'f'), (wl['w_out'], 'f')],
                               [((s, D_MODEL), F32, 'r0')], fetch('mix_out')))
    landed('mix_out', got)

    m_len = memx.shape[0]

    def f_memkv(mm_, g, w):
        mn = _rms(mm_, g[...])[0].astype(MXU)
        return (jnp.stack([_mm(mn, w[d]) for d in range(N_DEV)]),)
    kvm, = _rows("mem_kv", f_memkv, m_len, m_len, [(memx, 'r0'), (pl_['mem_norm_g'], 'f'), (wl['w_xkv'], 'f')],
                 [((N_DEV, m_len, X_HEAD_DIM), MXU, 'r1')])

    def f_xattn(h_, g, wq, kv_, wo):
        hn = _rms(h_, g[...])[0].astype(MXU)
        q_all = _mm(hn, wq[...]).astype(MXU)
        outs = []
        for hd in range(X_HEADS):
            p = _softmax(_mm_nt(_lanes(q_all, hd, X_HEAD_DIM), kv_[hd]) * X_SCALE)
            outs.append(_mm(p, kv_[X_HEADS + hd]).astype(MXU))
        return (h_ + _mm(jnp.concatenate(outs, axis=-1), wo[...]),)
    (h2,), got = _hosted(_rows("xattn", f_xattn, s, min(X_ROWS, s),
                               [(h1, 'r0'), (pl_['norm_x_g'], 'f'), (wl['w_xq'], 'f'), (kvm, 'f'), (wl['w_xo'], 'f')],
                               [((s, D_MODEL), F32, 'r0')], fetch('xattn')))
    landed('xattn', got)

    def f_ffn(h_, g, wg, wu, wd):
        hn = _rms(h_, g[...])[0].astype(MXU)
        y = jnp.zeros(h_.shape, F32)
        gates, ups = [], []
        for c in range(D_FF // FF_CHUNK):
            cs = pl.ds(c * FF_CHUNK, FF_CHUNK)
            gate, up = _mm(hn, wg[:, cs]), _mm(hn, wu[:, cs])
            y = y + _mm(gate * jax.nn.sigmoid(gate) * up, wd[cs, :])
            gates.append(gate)
            ups.append(up)
        return h_ + y, jnp.concatenate(gates, axis=-1), jnp.concatenate(ups, axis=-1)
    (h3, gate_f, up_f), got = _hosted(_rows(
        "ffn", f_ffn, s, tm,
        [(h2, 'r0'), (pl_['norm_ffn_g'], 'f'), (wl['w_gate'], 'f'), (wl['w_up'], 'f'), (wl['w_down'], 'f')],
        [((s, D_MODEL), F32, 'r0'), ((s, D_FF), MXU, 'r0'), ((s, D_FF), MXU, 'r0')], fetch('ffn')))
    landed('ffn', got)
    sv.update(h=h, proj=proj, q=q, k=k, v=v, a_out=a_out, lse=lse, x_re=x_re, x_im=x_im, y_ssm=y_ssm,
              s_out=s_out, h1=h1, kvm=kvm, h2=h2, u_ch=u_ch, gate=gate_f, up=up_f)
    return h3, sv, wl, nxt_got


def _layer_bwd(dh3, sv, memx, tabs, wl, pl_, nxt=None):
    s = dh3.shape[0]
    tm = min(256, s)
    cos, sin_lo, sin_hi = tabs
    gr = {}
    arrived = {}
    act_shape = (s, D_FF)

    def send(host):
        who, names = BWD_PLAN.get(host, (None, []))
        if who is None or (who == 'nxt' and not nxt):
            return None, []
        return (_scatter([nxt[n] if who == 'nxt' else _blocked(gr, n) for n in names]),
                [(who, n) for n in names])

    def f_ffn_bwd(h_, dy, gate_, up_, g, wg, wu, wd):
        hn, r = _rms(h_, g[...])
        hb = hn.astype(MXU)
        dyb = dy.astype(MXU)
        dhn = jnp.zeros(h_.shape, F32)
        acts, dgs, dus = [], [], []
        for c in range(D_FF // FF_CHUNK):
            cs = pl.ds(c * FF_CHUNK, FF_CHUNK)
            gate = _lanes(gate_, c, FF_CHUNK).astype(F32)
            up = _lanes(up_, c, FF_CHUNK).astype(F32)
            sg = jax.nn.sigmoid(gate)
            si = gate * sg
            dact = _mm_nt(dyb, wd[cs, :])
            dgate = (dact * up * (sg * (1.0 + gate * (1.0 - sg)))).astype(MXU)
            dup = (dact * si).astype(MXU)
            dhn = dhn + _mm_nt(dgate, wg[:, cs]) + _mm_nt(dup, wu[:, cs])
            acts.append((si * up).astype(MXU))
            dgs.append(dgate)
            dus.append(dup)
        dh, dg = _rms_bwd(h_, g[...], r, dhn)
        cat = lambda parts: jnp.concatenate(parts, axis=-1)
        return dy + dh, hb, cat(acts), cat(dgs), cat(dus), dg
    ex, keys = send('ffn_bwd')
    (dh2, hn_f, act, dgate, dup, gr['norm_ffn_g']), got = _hosted(_rows(
        "ffn_bwd", f_ffn_bwd, s, tm,
        [(sv['h2'], 'r0'), (dh3, 'r0'), (sv['gate'], 'r0'), (sv['up'], 'r0'), (pl_['norm_ffn_g'], 'f'),
         (wl['w_gate'], 'f'), (wl['w_up'], 'f'), (wl['w_down'], 'f')],
        [((s, D_MODEL), F32, 'r0'), ((s, D_MODEL), MXU, 'r0'), (act_shape, MXU, 'r0'), (act_shape, MXU, 'r0'),
         (act_shape, MXU, 'r0'), ((1, D_MODEL), F32, 'a')], ex))
    arrived.update(zip(keys, got))
    gr['w_gate'] = _mm_tn_call("dw_gate", hn_f, dgate, tn=FF_CHUNK)
    gr['w_up'] = _mm_tn_call("dw_up", hn_f, dup, tn=FF_CHUNK)
    gr['w_down'] = _mm_tn_call("dw_down", act, dh3, tk=FF_CHUNK)

    m_len = memx.shape[0]

    def f_xattn_bwd(h_, dy, g, wq, kv_, wo):
        hn, r = _rms(h_, g[...])
        hb = hn.astype(MXU)
        q_all = _mm(hb, wq[...]).astype(MXU)
        do_all = _mm_nt(dy, wo[...]).astype(MXU)
        dqs, ohs, dks, dvs = [], [], [], []
        for hd in range(X_HEADS):
            kh, vh = kv_[hd], kv_[X_HEADS + hd]
            qh, do = _lanes(q_all, hd, X_HEAD_DIM), _lanes(do_all, hd, X_HEAD_DIM)
            p = _softmax(_mm_nt(qh, kh) * X_SCALE)
            ohs.append(_mm(p, vh).astype(MXU))
            dvs.append(_mm_tn(p, do))
            dp = _mm_nt(do, vh)
            ds = p * (dp - jnp.sum(dp * p, axis=-1, keepdims=True)) * X_SCALE
            dqs.append(_mm(ds, kh).astype(MXU))
            dks.append(_mm_tn(ds, qh))
        dq_all = jnp.concatenate(dqs, axis=-1)
        dh, dg = _rms_bwd(h_, g[...], r, _mm_nt(dq_all, wq[...]))
        return dy + dh, hb, dq_all, jnp.concatenate(ohs, axis=-1), jnp.stack(dks + dvs), dg
    ex, keys = send('xattn_bwd')
    (dh1, hn_x, dq_x, oh_x, dkvm, gr['norm_x_g']), got = _hosted(_rows(
        "xattn_bwd", f_xattn_bwd, s, min(X_ROWS, s),
        [(sv['h1'], 'r0'), (dh2, 'r0'), (pl_['norm_x_g'], 'f'), (wl['w_xq'], 'f'), (sv['kvm'], 'f'),
         (wl['w_xo'], 'f')],
        [((s, D_MODEL), F32, 'r0'), ((s, D_MODEL), MXU, 'r0'), ((s, D_MODEL), MXU, 'r0'),
         ((s, D_MODEL), MXU, 'r0'), ((N_DEV, m_len, X_HEAD_DIM), F32, 'a'), ((1, D_MODEL), F32, 'a')], ex))
    arrived.update(zip(keys, got))
    gr['w_xq'] = _mm_tn_call("dw_xq", hn_x, dq_x)
    gr['w_xo'] = _mm_tn_call("dw_xo", oh_x, dh2)

    def f_memkv_bwd(mm_, dkv, g, w):
        mn, r = _rms(mm_, g[...])
        mb = mn.astype(MXU)
        dmn = jnp.zeros(mm_.shape, F32)
        dws = []
        for d in range(N_DEV):
            dmn = dmn + _mm_nt(dkv[d], w[d])
            dws.append(_mm_tn(mb, dkv[d]))
        _, dg = _rms_bwd(mm_, g[...], r, dmn)
        return jnp.stack(dws), dg
    gr['w_xkv'], gr['mem_norm_g'] = _rows(
        "mem_kv_bwd", f_memkv_bwd, m_len, m_len,
        [(memx, 'r0'), (dkvm, 'r1'), (pl_['mem_norm_g'], 'f'), (wl['w_xkv'], 'f')],
        [((N_DEV, D_MODEL, X_HEAD_DIM), F32, 'a'), ((1, D_MODEL), F32, 'a')])

    def f_mix_out_bwd(a, so, dy, ga, gs, w):
        dmix = _mm_nt(dy, w[...])
        an, ra = _rms(a, ga[...])
        sn, rs = _rms(so, gs[...])
        da, dga = _rms_bwd(a, ga[...], ra, dmix[:, 0:512])
        dso, dgs = _rms_bwd(so, gs[...], rs, dmix[:, 512:1024])
        return da, dso, jnp.concatenate([an, sn], axis=-1), dga, dgs
    da_out, ds_out, mixed, gr['attn_out_g'], gr['ssm_out_g'] = _rows(
        "mix_out_bwd", f_mix_out_bwd, s, tm,
        [(sv['a_out'], 'r0'), (sv['s_out'], 'r0'), (dh1, 'r0'), (pl_['attn_out_g'], 'f'), (pl_['ssm_out_g'], 'f'),
         (wl['w_out'], 'f')],
        [((s, 512), F32, 'r0'), ((s, 512), F32, 'r0'), ((s, D_MODEL), MXU, 'r0'), ((1, 512), F32, 'a'),
         ((1, 512), F32, 'a')])
    gr['w_out'] = _mm_tn_call("dw_out", mixed, dh1)

    ex, keys = send('flash_bwd')
    dq, dk, dv, got = _flash_bwd(sv['q'], sv['k'], sv['v'], sv['a_out'], sv['lse'], da_out, ex)
    arrived.update(zip(keys, got))

    def f_s5_out_bwd(xr, xi, u, y, ds, cre, cim, d, wglu, bglu):
        g, gelu_vjp = jax.vjp(jax.nn.gelu, y)
        sig = jax.nn.sigmoid(_mm(g, wglu[...]) + bglu[...])
        dz = ds * y * sig * (1.0 - sig)
        dy = ds * sig + gelu_vjp(_mm_nt(dz, wglu[...]))[0]
        dcr, dci = [], []
        for j in range(4):
            dyj = _lanes(dy, j, LANES)
            dcr.append(_mm_tn(_cat_blocks(xr, j), dyj))
            dci.append(_mm_tn(_cat_blocks(xi, j), dyj))
        return (dy, dy * d[...], jnp.stack(dcr), jnp.stack(dci),
                jnp.sum(dy * u, axis=0, keepdims=True), _mm_tn(g, dz), jnp.sum(dz, axis=0, keepdims=True))
    ex, keys = send('s5_out_bwd')
    (dy_ssm, du_dir, gr['c_re'], gr['c_im'], gr['ssm_d'], gr['ssm_w_glu'], gr['ssm_b_glu']), got = _hosted(_rows(
        "s5_out_bwd", f_s5_out_bwd, s, tm,
        [(sv['x_re'], 'r1'), (sv['x_im'], 'r1'), (sv['u_ch'], 'r0'), (sv['y_ssm'], 'r0'), (_to_chunks(ds_out), 'r0'),
         (pl_['c_re'], 'f'), (pl_['c_im'], 'f'), (pl_['ssm_d'], 'f'), (wl['ssm_w_glu'], 'f'),
         (pl_['ssm_b_glu'], 'f')],
        [((s, 512), F32, 'r0'), ((s, 512), F32, 'r0'), ((4, 512, LANES), F32, 'a'),
         ((4, 512, LANES), F32, 'a'), ((1, 512), F32, 'a'), ((512, 512), F32, 'a'), ((1, 512), F32, 'a')], ex))
    arrived.update(zip(keys, got))
    g_re, g_im = _scan(dy_ssm, pl_['c_re'], pl_['c_im'], pl_['a_re'], -pl_['a_im'], True)
    first_re = jnp.pad(sv['x_re'][:, s - 8:s - 1], ((0, 0), (1, 0), (0, 0)))
    first_im = jnp.pad(sv['x_im'][:, s - 8:s - 1], ((0, 0), (1, 0), (0, 0)))

    def f_s5_in_bwd(gre, gim, xr, xi, pr8, pi8, u, dud, f8r, f8i, bre, bim):
        first = pl.program_id(0) == 0
        xpr = jnp.concatenate([jnp.where(first, f8r[...], pr8), xr[:, :tm - 8]], axis=1)
        xpi = jnp.concatenate([jnp.where(first, f8i[...], pi8), xi[:, :tm - 8]], axis=1)
        dus, dbr, dbi = [], [], []
        for j in range(4):
            gj_r, gj_i, uj = _cat_blocks(gre, j), _cat_blocks(gim, j), _lanes(u, j, LANES)
            dus.append(_mm_nt(gj_r, bre[j]) + _mm_nt(gj_i, bim[j]))
            dbr.append(_mm_tn(uj, gj_r))
            dbi.append(_mm_tn(uj, gj_i))
        da_r = jnp.sum(gre * xpr + gim * xpi, axis=1, keepdims=True)
        da_i = jnp.sum(gim * xpr - gre * xpi, axis=1, keepdims=True)
        return dud + jnp.concatenate(dus, axis=-1), jnp.stack(dbr), jnp.stack(dbi), da_r, da_i
    ex, keys = send('s5_in_bwd')
    (du_ch, gr['b_re'], gr['b_im'], gr['a_re'], gr['a_im']), got = _hosted(_rows(
        "s5_in_bwd", f_s5_in_bwd, s, tm,
        [(g_re, 'r1'), (g_im, 'r1'), (sv['x_re'], 'r1'), (sv['x_im'], 'r1'), (sv['x_re'], 'p8'), (sv['x_im'], 'p8'),
         (sv['u_ch'], 'r0'), (du_dir, 'r0'), (first_re, 'f'), (first_im, 'f'), (pl_['b_re'], 'f'), (pl_['b_im'], 'f')],
        [((s, 512), F32, 'r0'), ((4, LANES, 512), F32, 'a'), ((4, LANES, 512), F32, 'a'),
         ((16, 1, LANES), F32, 'a'), ((16, 1, LANES), F32, 'a')], ex))
    arrived.update(zip(keys, got))
    du = _from_chunks(du_ch)

    def f_qkv_bwd(pr, cos_, slo, shi, dq_, dk_, dv_, gq, gkv, wq, wk, wv):
        cq, ckv = pr[:, 0:Q_LORA], pr[:, Q_LORA:Q_LORA + KV_LORA]
        cqn, rq = _rms(cq, gq[...])
        kvn, rkv = _rms(ckv, gkv[...])
        cqb, kvb = cqn.astype(MXU), kvn.astype(MXU)
        dcqn = jnp.zeros(cq.shape, F32)
        dkvn = jnp.zeros(ckv.shape, F32)
        dksum = jnp.zeros(dk_[0].shape, F32)
        dwq, dwk, dwv = [], [], []
        for hd in range(MLA_HEADS):
            dqp = (_rope_t(dq_[hd], cos_, slo, shi) * MLA_SCALE).astype(MXU)
            dkb, dvb = dk_[hd].astype(MXU), dv_[hd].astype(MXU)
            dwq.append(_mm_tn(cqb, dqp))
            dwk.append(_mm_tn(kvb, dkb))
            dwv.append(_mm_tn(kvb, dvb))
            dcqn = dcqn + _mm_nt(dqp, wq[hd])
            dkvn = dkvn + _mm_nt(dkb, wk[hd]) + _mm_nt(dvb, wv[hd])
            dksum = dksum + dk_[hd]
        dcq, dgq = _rms_bwd(cq, gq[...], rq, dcqn)
        dckv, dgkv = _rms_bwd(ckv, gkv[...], rkv, dkvn)
        dpa = jnp.concatenate([dcq, dckv, _rope_t(dksum, cos_, slo, shi)], axis=-1)
        return dpa, jnp.stack(dwq), jnp.stack(dwk), jnp.stack(dwv), dgq, dgkv
    ex, keys = send('mla_qkv_bwd')
    (dpa, gr['w_uq'], gr['w_k'], gr['w_v'], gr['q_norm_g'], gr['kv_norm_g']), got = _hosted(_rows(
        "mla_qkv_bwd", f_qkv_bwd, s, tm,
        [(sv['proj'], 'r0'), (cos, 'r0'), (sin_lo, 'r0'), (sin_hi, 'r0'), (dq, 'r1'), (dk, 'r1'), (dv, 'r1'),
         (pl_['q_norm_g'], 'f'), (pl_['kv_norm_g'], 'f'), (wl['w_uq'], 'f'), (wl['w_k'], 'f'), (wl['w_v'], 'f')],
        [((s, 512), F32, 'r0'), ((MLA_HEADS, Q_LORA, HEAD_W), F32, 'a'), ((MLA_HEADS, KV_LORA, HEAD_W), F32, 'a'),
         ((MLA_HEADS, KV_LORA, HEAD_W), F32, 'a'), ((1, Q_LORA), F32, 'a'), ((1, KV_LORA), F32, 'a')], ex))
    arrived.update(zip(keys, got))

    def f_mix_in_bwd(h_, dpa_, du_, dres, g, w):
        dproj = jnp.concatenate([dpa_, du_], axis=-1).astype(MXU)
        xn, r = _rms(h_, g[...])
        dh, dg = _rms_bwd(h_, g[...], r, _mm_nt(dproj, w[...]))
        return dres + dh, xn, dproj, dg
    ex, keys = send('mix_in_bwd')
    (dh0, xn, dproj, gr['norm_mix_g']), got = _hosted(_rows(
        "mix_in_bwd", f_mix_in_bwd, s, tm,
        [(sv['h'], 'r0'), (dpa, 'r0'), (du, 'r0'), (dh1, 'r0'), (pl_['norm_mix_g'], 'f'), (wl['w_in'], 'f')],
        [((s, D_MODEL), F32, 'r0'), ((s, D_MODEL), MXU, 'r0'), ((s, D_MODEL), MXU, 'r0'), ((1, D_MODEL), F32, 'a')],
        ex))
    arrived.update(zip(keys, got))
    gr['w_in'] = _mm_tn_call("dw_in", xn, dproj)
    return dh0, gr, arrived


def _layer_weights(w):
    wl = {}
    if 'w_in' in w:
        w_in = w['w_in'].reshape(D_MODEL, -1)
        z = lambda n: jnp.zeros((D_MODEL, n), w_in.dtype)
        wl['w_in'] = jnp.concatenate([w_in[:, :384], z(64), w_in[:, 384:416], z(32), w_in[:, 416:]], axis=1)
    if 'w_uq' in w:
        wl['w_uq'] = jnp.pad(w['w_uq'], ((0, 0), (0, 0), (0, HEAD_W - QK_NOPE - QK_ROPE)))
    if 'w_ukv' in w:
        wl['w_k'] = jnp.pad(w['w_ukv'][..., :QK_NOPE], ((0, 0), (0, 0), (0, HEAD_W - QK_NOPE)))
        wv = w['w_ukv'][..., QK_NOPE:]
        even = (jnp.arange(MLA_HEADS) % 2 == 0)[:, None, None]
        wl['w_v'] = jnp.concatenate([jnp.where(even, wv, 0), jnp.where(even, 0, wv)], axis=-1).astype(wv.dtype)
    if 'ssm_w_glu' in w:
        wl['ssm_w_glu'] = w['ssm_w_glu'].reshape(SSM_WIDTH, SSM_WIDTH)
    for n in ('w_out', 'w_xq', 'w_xo'):
        if n in w:
            wl[n] = w[n].reshape(D_MODEL, D_MODEL)
    if 'w_xkv' in w:
        wl['w_xkv'] = w['w_xkv']
    for n in ('w_gate', 'w_up'):
        if n in w:
            wl[n] = jnp.transpose(w[n], (1, 0, 2)).reshape(D_MODEL, D_FF)
    if 'w_down' in w:
        wl['w_down'] = w['w_down'].reshape(D_FF, D_MODEL)
    return wl


def _blocked(gr, n):
    if n == 'w_in':
        d = gr['w_in']
        out = jnp.concatenate([d[:, :384], d[:, 448:480], d[:, 512:]], axis=1).reshape(N_DEV, 128, -1)
    elif n == 'w_uq':
        out = gr['w_uq'][..., :QK_NOPE + QK_ROPE]
    elif n == 'w_ukv':
        even = (jnp.arange(MLA_HEADS) % 2 == 0)[:, None, None]
        dv = gr['w_v']
        out = jnp.concatenate([gr['w_k'][..., :QK_NOPE], jnp.where(even, dv[..., :V_DIM], dv[..., V_DIM:])], axis=-1)
    elif n == 'ssm_w_glu':
        out = gr['ssm_w_glu'].reshape(N_DEV, SSM_WIDTH // N_DEV, SSM_WIDTH)
    elif n in ('w_out', 'w_xq', 'w_xo'):
        out = gr[n].reshape(N_DEV, D_MODEL // N_DEV, D_MODEL)
    elif n in ('w_gate', 'w_up'):
        out = jnp.transpose(gr[n].reshape(D_MODEL, N_DEV, D_FF // N_DEV), (1, 0, 2))
    elif n == 'w_down':
        out = gr[n].reshape(N_DEV, D_FF // N_DEV, D_MODEL)
    else:
        out = gr[n]
    return out.astype(MXU)


def kernel(x, mem, positions, norm_mix_g, w_in, q_norm_g, w_uq, kv_norm_g, w_ukv, ssm_lambda_re, ssm_lambda_im, ssm_log_dt, ssm_b_re, ssm_b_im, ssm_c_re, ssm_c_im, ssm_d, ssm_w_glu, ssm_b_glu, attn_out_g, ssm_out_g, w_out, norm_x_g, mem_norm_g, w_xq, w_xkv, w_xo, norm_ffn_g, w_gate, w_up, w_down, final_norm_g, loss_target, m_norm_mix_g, m_w_in, m_q_norm_g, m_w_uq, m_kv_norm_g, m_w_ukv, m_ssm_lambda_re, m_ssm_lambda_im, m_ssm_log_dt, m_ssm_b_re, m_ssm_b_im, m_ssm_c_re, m_ssm_c_im, m_ssm_d, m_ssm_w_glu, m_ssm_b_glu, m_attn_out_g, m_ssm_out_g, m_w_out, m_norm_x_g, m_mem_norm_g, m_w_xq, m_w_xkv, m_w_xo, m_norm_ffn_g, m_w_gate, m_w_up, m_w_down, m_final_norm_g, v_norm_mix_g, v_w_in, v_q_norm_g, v_w_uq, v_kv_norm_g, v_w_ukv, v_ssm_lambda_re, v_ssm_lambda_im, v_ssm_log_dt, v_ssm_b_re, v_ssm_b_im, v_ssm_c_re, v_ssm_c_im, v_ssm_d, v_ssm_w_glu, v_ssm_b_glu, v_attn_out_g, v_ssm_out_g, v_w_out, v_norm_x_g, v_mem_norm_g, v_w_xq, v_w_xkv, v_w_xo, v_norm_ffn_g, v_w_gate, v_w_up, v_w_down, v_final_norm_g):
    args = dict(locals())
    W = {n: args[n] for n in WEIGHTS}
    M = {n: args['m_' + n] for n in WEIGHTS}
    V = {n: args['v_' + n] for n in WEIGHTS}
    s = x.shape[1]
    h = x[0]
    memx = mem[0]

    freqs = ROPE_THETA ** (-jnp.arange(0, QK_ROPE, 2, dtype=F32) / QK_ROPE)
    ang = positions[0].astype(F32)[:, None] * freqs
    c16, s16 = jnp.cos(ang), jnp.sin(ang)
    zeros = lambda n: jnp.zeros((s, n), F32)
    cos = jnp.concatenate([jnp.ones((s, QK_NOPE), F32), c16, c16, zeros(32)], axis=1)
    sin_lo = jnp.concatenate([zeros(QK_NOPE), -s16, zeros(ROT + 32)], axis=1)
    sin_hi = jnp.concatenate([zeros(QK_NOPE + ROT), s16, zeros(32)], axis=1)
    tabs = (cos, sin_lo, sin_hi)

    shards = [{n: W[n][l].astype(MXU) for n in SHARDED} for l in range(DEPTH)]
    gathered = dict(zip(EARLY, _run_exchange("gather_weights", _gather(_named(EARLY, shards[0])))))

    layers = []
    for l in range(DEPTH):
        wl = _layer_weights(gathered)
        s5_in = [W[n][l] for n in ('ssm_lambda_re', 'ssm_lambda_im', 'ssm_log_dt', 'ssm_b_re', 'ssm_b_im',
                                   'ssm_c_re', 'ssm_c_im')]
        (a_re, a_im, bre, bim, cre, cim), s5_vjp = jax.vjp(_s5_params, *s5_in)
        pl_ = {n: W[n][l][None] for n in ('norm_mix_g', 'q_norm_g', 'kv_norm_g', 'ssm_d', 'ssm_b_glu',
                                           'attn_out_g', 'ssm_out_g', 'norm_x_g', 'mem_norm_g', 'norm_ffn_g')}
        pl_.update(a_re=a_re, a_im=a_im, b_re=bre, b_im=bim, c_re=cre, c_im=cim)
        h, sv, wl, gathered = _layer_fwd(h, memx, tabs, wl, pl_, shards[l], shards[l + 1] if l + 1 < DEPTH else None)
        layers.append((wl, pl_, sv, s5_vjp))

    def f_loss(h_, tgt, g):
        y, r = _rms(h_, g[...])
        err = y - tgt
        part = 0.5 * jnp.sum(jnp.mean(err * err, axis=-1, keepdims=True), axis=0, keepdims=True)
        dh, dg = _rms_bwd(h_, g[...], r, err / D_MODEL)
        return dh, dg, jnp.broadcast_to(part, (8, LANES))
    dh, g_final, loss_part = _rows(
        "loss_head", f_loss, s, min(256, s), [(h, 'r0'), (loss_target[0], 'r0'), (final_norm_g[None], 'f')],
        [((s, D_MODEL), F32, 'r0'), ((1, D_MODEL), F32, 'a'), ((8, LANES), F32, 'a')])
    loss = lax.psum(loss_part[0, 0], ("x", "y", "c"))

    parts = [{} for _ in range(DEPTH)]
    g_rep = [None] * DEPTH
    blocks = None
    for l in reversed(range(DEPTH)):
        wl, pl_, sv, s5_vjp = layers[l]
        dh, gr, arrived = _layer_bwd(dh, sv, memx, tabs, wl, pl_, blocks)
        for (who, n), p in arrived.items():
            parts[l + 1 if who == 'nxt' else l][n] = p
        blocks = {n: _blocked(gr, n) for n in EARLY}
        ds5 = s5_vjp((gr['a_re'], gr['a_im'], gr['b_re'], gr['b_im'], gr['c_re'], gr['c_im']))
        rep = dict(zip(('ssm_lambda_re', 'ssm_lambda_im', 'ssm_log_dt', 'ssm_b_re', 'ssm_b_im', 'ssm_c_re',
                        'ssm_c_im'), ds5))
        for n in ('norm_mix_g', 'q_norm_g', 'kv_norm_g', 'ssm_d', 'ssm_b_glu', 'attn_out_g', 'ssm_out_g',
                  'norm_x_g', 'mem_norm_g', 'norm_ffn_g'):
            rep[n] = gr[n][0]
        g_rep[l] = rep
    grad_x = dh[None]

    rep_names = REPL_L + ['final_norm_g']
    g_loc = {n: jnp.stack([g_rep[l][n] for l in range(DEPTH)]) for n in REPL_L}
    g_loc['final_norm_g'] = g_final
    rest = [n for n in SHARDED if n not in parts[0]]
    last = _run_exchange("last_grads", _together(_scatter(_named(rest, blocks)),
                                                 _gather([_pack(_named(rep_names, g_loc))])))
    parts[0].update(zip(rest, last[:len(rest)]))

    out_sh = [{}, {}, {}, {}]
    for n in SHARDED:
        res = _adamw_weight("adamw_" + n, [parts[l][n] for l in range(DEPTH)], W[n], M[n], V[n])
        for kind, r in enumerate(res):
            out_sh[kind][n] = r

    shapes_rp = [(1,) + W[n].shape if W[n].ndim == 1 else W[n].shape for n in rep_names]
    g_rp = _unpack(_sum_sources("sum_small_grads", last[len(rest)]), shapes_rp)
    as_rows = lambda d: [d[n].reshape(shp) for n, shp in zip(rep_names, shapes_rp)]
    res_rp = (g_rp,) + _adamw_small("adamw_replicated", g_rp, as_rows(W), as_rows(M), as_rows(V))
    out_rp = [{n: a.reshape(W[n].shape) for n, a in zip(rep_names, r)} for r in res_rp]

    outs = [loss, grad_x]
    for kind in range(4):
        for n in WEIGHTS:
            outs.append(out_sh[kind][n] if n in SHARDED else out_rp[kind][n])
    return tuple(outs)
```

```python
from typing import Callable, NamedTuple

import jax
import jax.numpy as jnp
from jax import lax
from jax.experimental import pallas as pl
from jax.experimental.pallas import tpu as pltpu

F32 = jnp.float32
MXU = jnp.bfloat16

D_MODEL = 1024
MLA_HEADS = 8
QK_NOPE = 64
QK_ROPE = 32
V_DIM = 64
Q_LORA = 256
KV_LORA = 128
SSM_WIDTH = 512
SSM_GROUPS = 32
SSM_GROUP = 16
SSM_STATE = 64
X_HEADS = 4
X_HEAD_DIM = 256
D_FF = 2816
FF_CHUNK = D_FF // 2
ROPE_THETA = 10000.0
EPS = 1e-6
DEPTH = 2
N_DEV = 8
LANES = 128
HEAD_W = 128
MLA_SCALE = (QK_NOPE + QK_ROPE) ** -0.5
X_SCALE = X_HEAD_DIM ** -0.5
ADAM_LR, ADAM_B1, ADAM_B2, ADAM_EPS, ADAM_WD, ADAM_STEP = 0.001, 0.9, 0.999, 1e-08, 0.01, 10
VMEM_LIMIT = 56 * 1024 * 1024
FLASH_TILE = 512
DW_ROWS = 2048
X_ROWS = 512
ROW_TILE = 512
FFN_ROWS = 256
MESH = pl.DeviceIdType.MESH

SHARDED = ['w_in', 'w_uq', 'w_ukv', 'ssm_w_glu', 'w_out', 'w_xq', 'w_xkv', 'w_xo', 'w_gate', 'w_up', 'w_down']
REPL_L = ['norm_mix_g', 'q_norm_g', 'kv_norm_g', 'ssm_lambda_re', 'ssm_lambda_im', 'ssm_log_dt', 'ssm_b_re',
          'ssm_b_im', 'ssm_c_re', 'ssm_c_im', 'ssm_d', 'ssm_b_glu', 'attn_out_g', 'ssm_out_g', 'norm_x_g',
          'mem_norm_g', 'norm_ffn_g']
WEIGHTS = ['norm_mix_g', 'w_in', 'q_norm_g', 'w_uq', 'kv_norm_g', 'w_ukv', 'ssm_lambda_re', 'ssm_lambda_im',
           'ssm_log_dt', 'ssm_b_re', 'ssm_b_im', 'ssm_c_re', 'ssm_c_im', 'ssm_d', 'ssm_w_glu', 'ssm_b_glu',
           'attn_out_g', 'ssm_out_g', 'w_out', 'norm_x_g', 'mem_norm_g', 'w_xq', 'w_xkv', 'w_xo', 'norm_ffn_g',
           'w_gate', 'w_up', 'w_down', 'final_norm_g']


def _pcall(body, **kw):
    return pl.pallas_call(body, **kw)


def _mm(a, b):
    return jnp.dot(a.astype(MXU), b.astype(MXU), preferred_element_type=F32)


def _mm_nt(a, b):
    return lax.dot_general(a.astype(MXU), b.astype(MXU), (((1,), (1,)), ((), ())), preferred_element_type=F32)


def _mm_tn(a, b):
    return lax.dot_general(a.astype(MXU), b.astype(MXU), (((0,), (0,)), ((), ())), preferred_element_type=F32)


def _rms(x, g):
    r = lax.rsqrt(jnp.mean(x * x, axis=-1, keepdims=True) + EPS)
    return x * r * g, r


def _rms_bwd(x, g, r, dy):
    dyg = dy * g
    dx = r * dyg - x * (r * r * r) * jnp.mean(dyg * x, axis=-1, keepdims=True)
    return dx, jnp.sum(dy * x * r, axis=0, keepdims=True)


ROT = QK_ROPE // 2


def _rope(x, cos, sin_lo, sin_hi):
    return x * cos + pltpu.roll(x, HEAD_W - ROT, 1) * sin_lo + pltpu.roll(x, ROT, 1) * sin_hi


def _rope_t(g, cos, sin_lo, sin_hi):
    return g * cos + pltpu.roll(g * sin_lo, ROT, 1) + pltpu.roll(g * sin_hi, HEAD_W - ROT, 1)


def _softmax(s):
    m = jnp.max(s, axis=-1, keepdims=True)
    e = jnp.exp(s - m)
    return e / jnp.sum(e, axis=-1, keepdims=True)


def _lanes(x, j, w):
    return x[:, j * w:(j + 1) * w]


def _rows(name, fn, n, tm, ins, outs, side=None):
    def spec(shape, kind):
        nd = len(shape)
        if kind == 'p8':
            return pl.BlockSpec((shape[0], 8, shape[2]), lambda i: (0, jnp.maximum(i * (tm // 8) - 1, 0), 0))
        if kind == 'f':
            return pl.BlockSpec(shape, lambda i, _nd=nd: (0,) * _nd, pipeline_mode=pl.Buffered(1))
        if kind == 'a':
            return pl.BlockSpec(shape, lambda i, _nd=nd: (0,) * _nd)
        ax = int(kind[1])
        blk = tuple(tm if d == ax else s for d, s in enumerate(shape))
        return pl.BlockSpec(blk, lambda i, _ax=ax, _nd=nd: tuple(i if d == _ax else 0 for d in range(_nd)))

    n_in, n_out, n_steps = len(ins), len(outs), n // tm

    def body(*refs):
        in_refs, out_refs, steps = _side_split(refs, n_in, n_out, side)
        i = pl.program_id(0)
        if steps:
            pl.when(i == 0)(steps[0])
            pl.when(i == _pass_on_step(n_steps))(steps[1])
        args = [r if k == 'f' else r[...] for r, (_, k) in zip(in_refs, ins)]
        res = fn(*args)
        for r, (_, dt, k), v in zip(out_refs, outs, res):
            if k == 'a':
                _accumulate(r, v.astype(dt), i)
            else:
                r[...] = v.astype(dt)
        if steps:
            pl.when(i == n_steps - 1)(steps[2])

    s_in, s_out, s_shape, s_sems, s_ops = _side_args(side)
    res = _pcall(
        body, name=name + ("_x" if side else ""), grid=(n_steps,),
        in_specs=[spec(a.shape, k) for a, k in ins] + s_in,
        out_specs=[spec(s, k) for s, _, k in outs] + s_out,
        out_shape=[jax.ShapeDtypeStruct(s, dt) for s, dt, _ in outs] + s_shape,
        scratch_shapes=s_sems,
        compiler_params=pltpu.CompilerParams(dimension_semantics=("arbitrary",), vmem_limit_bytes=VMEM_LIMIT),
    )(*[a for a, _ in ins], *s_ops)
    return _Hosted(res[:n_out], res[n_out:]) if side else res


def _accumulate(ref, v, i):
    @pl.when(i == 0)
    def _():
        ref[...] = v

    @pl.when(i != 0)
    def _():
        ref[...] += v


def _mm_tn_call(name, a, b, tk=None, tn=None):
    out_dtype = MXU
    s, k = a.shape
    n = b.shape[1]
    tk, tn = tk or k, tn or n
    ts = min(DW_ROWS, s)
    ns = s // ts

    def body(a_ref, b_ref, o_ref, acc_ref):
        j = pl.program_id(2)
        _accumulate(acc_ref, _mm_tn(a_ref[...], b_ref[...]), j)

        @pl.when(j == ns - 1)
        def _():
            o_ref[...] = acc_ref[...].astype(out_dtype)

    return _pcall(
        body, name=name, grid=(k // tk, n // tn, ns),
        in_specs=[pl.BlockSpec((ts, tk), lambda ik, jn, j: (j, ik)),
                  pl.BlockSpec((ts, tn), lambda ik, jn, j: (j, jn))],
        out_specs=pl.BlockSpec((tk, tn), lambda ik, jn, j: (ik, jn)),
        out_shape=jax.ShapeDtypeStruct((k, n), out_dtype),
        scratch_shapes=[pltpu.VMEM((tk, tn), F32)],
        compiler_params=pltpu.CompilerParams(dimension_semantics=("arbitrary", "arbitrary", "arbitrary"),
                                             vmem_limit_bytes=VMEM_LIMIT),
    )(a, b)


def _side_split(refs, n_in, n_out, side):
    if side is None:
        return refs[:n_in], refs[n_in:n_in + n_out], None
    si, so = len(side.ins), len(side.out_shapes)
    own_in, side_in = refs[:n_in], refs[n_in:n_in + si]
    own_out, side_out = refs[n_in + si:n_in + si + n_out], refs[n_in + si + n_out:n_in + si + n_out + so]
    return own_in, own_out, side.steps(side_in, side_out, refs[n_in + si + n_out + so:])


def _pass_on_step(n_steps):
    return max(n_steps - 2, 0)


def _side_args(side):
    if side is None:
        return [], [], [], [], []
    any_spec = pl.BlockSpec(memory_space=pl.ANY)
    return ([any_spec] * len(side.ins), [any_spec] * len(side.out_shapes), list(side.out_shapes),
            list(side.sem_shapes), list(side.ins))


class _Hosted(NamedTuple):
    results: list
    arrived: list


def _hosted(res):
    return res if isinstance(res, _Hosted) else _Hosted(res, ())


def _flash_fwd(q, k, v, side=None):
    nh, s, w = q.shape
    t = min(FLASH_TILE, s)
    nq = s // t
    n_steps = (nh // 2) * nq

    def body(*refs):
        (q_ref, k_ref, v_ref), (o_ref, lse_ref), steps = _side_split(refs, 3, 2, side)
        step = pl.program_id(0) * nq + pl.program_id(1)
        if steps:
            pl.when(step == 0)(steps[0])
            pl.when(step == _pass_on_step(n_steps))(steps[1])
        qi = pl.program_id(1)
        qs = [q_ref[0], q_ref[1]]
        below = lax.broadcasted_iota(jnp.int32, (t, t), 1) <= lax.broadcasted_iota(jnp.int32, (t, t), 0)

        def tile(j, carry, diagonal):
            sl = pl.ds(pl.multiple_of(j * t, t), t)
            out = []
            for hh in range(2):
                m, l, acc = carry[3 * hh:3 * hh + 3]
                sc = _mm_nt(qs[hh], k_ref[hh, sl, :])
                if diagonal:
                    sc = jnp.where(below, sc, -1e30)
                m_new = jnp.maximum(m, jnp.max(sc, axis=1, keepdims=True))
                p = jnp.exp(sc - m_new)
                alpha = jnp.exp(m - m_new)
                out += [m_new, alpha * l + jnp.sum(p, axis=1, keepdims=True), alpha * acc + _mm(p, v_ref[hh, sl, :])]
            return tuple(out)

        init = (jnp.full((t, 1), -1e30, F32), jnp.zeros((t, 1), F32), jnp.zeros((t, w), F32)) * 2
        carry = lax.fori_loop(0, qi, lambda j, c: tile(j, c, False), init)
        carry = tile(qi, carry, True)
        o_ref[...] = carry[2] / carry[1] + carry[5] / carry[4]
        for hh in range(2):
            lse_ref[hh] = jnp.broadcast_to(carry[3 * hh] + jnp.log(carry[3 * hh + 1]), (t, w))
        if steps:
            pl.when(step == n_steps - 1)(steps[2])

    s_in, s_out, s_shape, s_sems, s_ops = _side_args(side)
    res = _pcall(
        body, name="mla_flash_fwd" + ("_x" if side else ""), grid=(nh // 2, nq),
        in_specs=[pl.BlockSpec((2, t, w), lambda p, i: (p, i, 0)),
                  pl.BlockSpec((2, s, w), lambda p, i: (p, 0, 0)),
                  pl.BlockSpec((2, s, w), lambda p, i: (p, 0, 0))] + s_in,
        out_specs=[pl.BlockSpec((t, w), lambda p, i: (i, p)),
                   pl.BlockSpec((2, t, w), lambda p, i: (p, i, 0))] + s_out,
        out_shape=[jax.ShapeDtypeStruct((s, (nh // 2) * w), F32), jax.ShapeDtypeStruct((nh, s, w), F32)] + s_shape,
        scratch_shapes=s_sems,
        compiler_params=pltpu.CompilerParams(dimension_semantics=("arbitrary", "arbitrary"),
                                             vmem_limit_bytes=VMEM_LIMIT),
    )(q, k, v, *s_ops)
    return res[0], res[1], res[2:]


def _flash_bwd(q, k, v, o, lse, do, side=None):
    nh, s, w = q.shape
    t = min(FLASH_TILE, s)
    nq = s // t
    n_steps = (nh // 2) * nq

    def body(*refs):
        (q_ref, k_ref, v_ref, o_ref, lse_ref, do_ref), (dq_ref, dk_ref, dv_ref), steps = _side_split(refs, 6, 3, side)
        step = pl.program_id(0) * nq + pl.program_id(1)
        if steps:
            pl.when(step == 0)(steps[0])
            pl.when(step == _pass_on_step(n_steps))(steps[1])
        j = pl.program_id(1)

        @pl.when(j == 0)
        def _():
            dq_ref[...] = jnp.zeros(dq_ref.shape, F32)

        below = lax.broadcasted_iota(jnp.int32, (t, t), 1) <= lax.broadcasted_iota(jnp.int32, (t, t), 0)
        lane = lax.broadcasted_iota(jnp.int32, (t, w), 1)
        heads = [jnp.logical_and(lane >= hh * V_DIM, lane < (hh + 1) * V_DIM) for hh in range(2)]
        ks = [k_ref[0], k_ref[1]]
        vs = [v_ref[0], v_ref[1]]

        def tile(i, carry, diagonal):
            sl = pl.ds(pl.multiple_of(i * t, t), t)
            dout_all, o_all = do_ref[sl, :], o_ref[sl, :]
            out = []
            for hh in range(2):
                dk, dv = carry[2 * hh], carry[2 * hh + 1]
                qh = q_ref[hh, sl, :]
                dout = jnp.where(heads[hh], dout_all, 0.0)
                sc = _mm_nt(qh, ks[hh])
                if diagonal:
                    sc = jnp.where(below, sc, -1e30)
                p = jnp.exp(sc - lse_ref[hh, sl, 0:1])
                dp = _mm_nt(dout, vs[hh])
                ds = p * (dp - jnp.sum(dout * o_all, axis=1, keepdims=True))
                dq_ref[hh, sl, :] += _mm(ds, ks[hh])
                out += [dk + _mm_tn(ds, qh), dv + _mm_tn(p, dout)]
            return tuple(out)

        carry = tile(j, (jnp.zeros((t, w), F32),) * 4, True)
        carry = lax.fori_loop(j + 1, nq, lambda i, c: tile(i, c, False), carry)
        for hh in range(2):
            dk_ref[hh] = carry[2 * hh]
            dv_ref[hh] = jnp.where(heads[hh], carry[2 * hh + 1], 0.0)
        if steps:
            pl.when(step == n_steps - 1)(steps[2])

    s_in, s_out, s_shape, s_sems, s_ops = _side_args(side)
    res = _pcall(
        body, name="mla_flash_bwd" + ("_x" if side else ""), grid=(nh // 2, nq),
        in_specs=[pl.BlockSpec((2, s, w), lambda p, j: (p, 0, 0)),
                  pl.BlockSpec((2, t, w), lambda p, j: (p, j, 0)),
                  pl.BlockSpec((2, t, w), lambda p, j: (p, j, 0)),
                  pl.BlockSpec((s, w), lambda p, j: (0, p)),
                  pl.BlockSpec((2, s, w), lambda p, j: (p, 0, 0)),
                  pl.BlockSpec((s, w), lambda p, j: (0, p))] + s_in,
        out_specs=[pl.BlockSpec((2, s, w), lambda p, j: (p, 0, 0)),
                   pl.BlockSpec((2, t, w), lambda p, j: (p, j, 0)),
                   pl.BlockSpec((2, t, w), lambda p, j: (p, j, 0))] + s_out,
        out_shape=[jax.ShapeDtypeStruct((nh, s, w), F32)] * 3 + s_shape,
        scratch_shapes=s_sems,
        compiler_params=pltpu.CompilerParams(dimension_semantics=("arbitrary", "arbitrary"),
                                             vmem_limit_bytes=VMEM_LIMIT),
    )(q, k, v, o, lse, do, *s_ops)
    return res[0], res[1], res[2], res[3:]


def _scan(src, w_re, w_im, a_re, a_im, reverse):
    s = src.shape[0]
    nb, w = a_re.shape[0], LANES
    ch = s // 8
    assert ch & (ch - 1) == 0
    grp = 4
    tr = min(512, s)

    def cmul(ar, ai, xr, xi):
        return ar * xr - ai * xi, ar * xi + ai * xr

    def body(src_ref, wr_ref, wi_ref, ar_ref, ai_ref, xr_ref, xi_ref):
        def project(c, carry):
            rows = pl.ds(pl.multiple_of(c * tr, tr), tr)
            u = src_ref[rows, :]
            if reverse:
                br, bi = _mm_nt(u, wr_ref[...]), _mm_nt(u, wi_ref[...])
            else:
                br, bi = _mm(u, wr_ref[...]), _mm(u, wi_ref[...])
            for g in range(grp):
                xr_ref[g, rows, :] = _lanes(br, g, w)
                xi_ref[g, rows, :] = _lanes(bi, g, w)
            return carry

        lax.fori_loop(0, s // tr, project, 0)
        sub = lax.broadcasted_iota(jnp.int32, (8, w), 0)

        def shift(x, k):
            if reverse:
                return jnp.where(sub < 8 - k, pltpu.roll(x, 8 - k, 0), 0.0)
            return jnp.where(sub >= k, pltpu.roll(x, k, 0), 0.0)

        ar = [jnp.broadcast_to(ar_ref[g], (8, w)) for g in range(grp)]
        ai = [jnp.broadcast_to(ai_ref[g], (8, w)) for g in range(grp)]

        def tsl(i):
            return pl.ds(pl.multiple_of(((ch - 1 - i) if reverse else i) * 8, 8), 8)

        def local(i, carry):
            out = []
            for g in range(grp):
                xr, xi = carry[2 * g], carry[2 * g + 1]
                pr, pi = cmul(ar[g], ai[g], xr, xi)
                nr = pr + xr_ref[g, tsl(i), :]
                ni = pi + xi_ref[g, tsl(i), :]
                xr_ref[g, tsl(i), :] = nr
                xi_ref[g, tsl(i), :] = ni
                out += [nr, ni]
            return tuple(out)

        fin = lax.fori_loop(0, ch, local, (jnp.zeros((8, w), F32),) * (2 * grp))

        carry_in = []
        for g in range(grp):
            pr, pi = ar[g], ai[g]
            for _ in range(ch.bit_length() - 1):
                pr, pi = cmul(pr, pi, pr, pi)
            fr, fi = fin[2 * g], fin[2 * g + 1]
            for kk in (1, 2, 4):
                sr, si = cmul(pr, pi, shift(fr, kk), shift(fi, kk))
                fr, fi = fr + sr, fi + si
                pr, pi = cmul(pr, pi, pr, pi)
            carry_in += [shift(fr, 1), shift(fi, 1)]

        def fix(i, pw):
            out = []
            for g in range(grp):
                pr, pi = pw[2 * g], pw[2 * g + 1]
                cr, ci = cmul(pr, pi, carry_in[2 * g], carry_in[2 * g + 1])
                xr_ref[g, tsl(i), :] = xr_ref[g, tsl(i), :] + cr
                xi_ref[g, tsl(i), :] = xi_ref[g, tsl(i), :] + ci
                nr, ni = cmul(pr, pi, ar[g], ai[g])
                out += [nr, ni]
            return tuple(out)

        lax.fori_loop(0, ch, fix, tuple(x for g in range(grp) for x in (ar[g], ai[g])))

    per_j = 4 // grp
    blk = pl.BlockSpec((grp, s, w), lambda i: (i, 0, 0))
    ablk = pl.BlockSpec((grp, 1, w), lambda i: (i, 0, 0))
    sblk = pl.BlockSpec((s, w), lambda i: (0, i // per_j))
    if reverse:
        wblk = pl.BlockSpec((None, grp * w, w), lambda i: (i // per_j, i % per_j, 0))
    else:
        wblk = pl.BlockSpec((None, w, grp * w), lambda i: (i // per_j, 0, i % per_j))
    return _pcall(
        body, name="s5_scan_rev" if reverse else "s5_scan", grid=(nb // grp,),
        in_specs=[sblk, wblk, wblk, ablk, ablk], out_specs=[blk, blk],
        out_shape=[jax.ShapeDtypeStruct((nb, s, w), F32)] * 2,
        compiler_params=pltpu.CompilerParams(dimension_semantics=("arbitrary",), vmem_limit_bytes=VMEM_LIMIT),
    )(src, w_re, w_im, a_re, a_im)


class _Exchange(NamedTuple):
    ins: list
    out_shapes: list
    sem_shapes: list
    steps: Callable


def _gather_steps(ins, outs, sems):
    n = len(ins)
    send_sems, recv_sems, local_sems = sems
    x, y, c = lax.axis_index("x"), lax.axis_index("y"), lax.axis_index("c")
    me, sibling = (x, y, c), (x, y, 1 - c)
    chips = [(1 - x, y), (x, 1 - y), (1 - x, 1 - y)]

    def copy(a, k, block, to, src=None):
        dst = outs[a].at[4 * block[0] + 2 * block[1] + block[2]]
        return pltpu.make_async_remote_copy(
            src_ref=dst if src is None else src, dst_ref=dst,
            send_sem=send_sems.at[a, k], recv_sem=recv_sems.at[a, k], device_id=to, device_id_type=MESH)

    mine = [pltpu.make_async_copy(ins[a], outs[a].at[4 * x + 2 * y + c], local_sems.at[a]) for a in range(n)]
    first = []
    for a in range(n):
        first.append(copy(a, 0, me, sibling, src=ins[a]))
        first += [copy(a, 1 + j, me, (*chip, c), src=ins[a]) for j, chip in enumerate(chips)]
    passed = [copy(a, 4 + j, (*chip, c), sibling) for j, chip in enumerate(chips) for a in range(n)]

    def start():
        for cp in mine + first:
            cp.start()

    def pass_on():
        i = 0
        for j, chip in enumerate(chips):
            for a in range(n):
                copy(a, 1 + j, (*chip, c), me).wait_recv()
                passed[i].start()
                i += 1

    def finish():
        for a in range(n):
            copy(a, 0, sibling, me).wait_recv()
            for j, chip in enumerate(chips):
                copy(a, 4 + j, (*chip, 1 - c), me).wait_recv()
        for cp in first + passed:
            cp.wait_send()
        for cp in mine:
            cp.wait()

    return start, pass_on, finish


def _gather(arrs):
    n = len(arrs)
    return _Exchange(list(arrs), [jax.ShapeDtypeStruct((N_DEV,) + a.shape, a.dtype) for a in arrs],
                     [pltpu.SemaphoreType.DMA((n, 7)), pltpu.SemaphoreType.DMA((n, 7)), pltpu.SemaphoreType.DMA((n,))],
                     _gather_steps)


def _scatter_steps(ins, outs, sems):
    n = len(ins)
    send_sems, recv_sems, local_sems = sems
    x, y, c = lax.axis_index("x"), lax.axis_index("y"), lax.axis_index("c")
    me = 4 * x + 2 * y + c
    own, sent, arrivals = [], [], []
    for a in range(n):
        own.append(pltpu.make_async_copy(ins[a].at[me], outs[a].at[me], local_sems.at[a]))
        for k in range(1, N_DEV):
            px, py, pc = x ^ ((k >> 2) & 1), y ^ ((k >> 1) & 1), c ^ (k & 1)
            peer = 4 * px + 2 * py + pc
            sent.append(pltpu.make_async_remote_copy(
                src_ref=ins[a].at[peer], dst_ref=outs[a].at[me],
                send_sem=send_sems.at[a, k - 1], recv_sem=recv_sems.at[a, k - 1],
                device_id=(px, py, pc), device_id_type=MESH))
            arrivals.append(pltpu.make_async_remote_copy(
                src_ref=ins[a].at[me], dst_ref=outs[a].at[peer],
                send_sem=send_sems.at[a, k - 1], recv_sem=recv_sems.at[a, k - 1],
                device_id=(x, y, c), device_id_type=MESH))

    def start():
        for cp in own + sent:
            cp.start()

    def pass_on():
        pass

    def finish():
        for cp in arrivals:
            cp.wait_recv()
        for cp in sent:
            cp.wait_send()
        for cp in own:
            cp.wait()

    return start, pass_on, finish


def _scatter(grads):
    n = len(grads)
    return _Exchange(list(grads), [jax.ShapeDtypeStruct(g.shape, g.dtype) for g in grads],
                     [pltpu.SemaphoreType.DMA((n, N_DEV - 1)), pltpu.SemaphoreType.DMA((n, N_DEV - 1)),
                      pltpu.SemaphoreType.DMA((n,))], _scatter_steps)


def _together(a, b):
    def steps(ins, outs, sems):
        sa = a.steps(ins[:len(a.ins)], outs[:len(a.out_shapes)], sems[:len(a.sem_shapes)])
        sb = b.steps(ins[len(a.ins):], outs[len(a.out_shapes):], sems[len(a.sem_shapes):])

        def both(k):
            def run():
                sa[k]()
                sb[k]()
            return run
        return both(0), both(1), both(2)

    return _Exchange(a.ins + b.ins, a.out_shapes + b.out_shapes, a.sem_shapes + b.sem_shapes, steps)


def _run_exchange(name, ex):
    n_in, n_out = len(ex.ins), len(ex.out_shapes)

    def body(*refs):
        for step in ex.steps(refs[:n_in], refs[n_in:n_in + n_out], refs[n_in + n_out:]):
            step()

    any_spec = pl.BlockSpec(memory_space=pl.ANY)
    return _pcall(body, name=name, in_specs=[any_spec] * n_in, out_specs=[any_spec] * n_out,
                  out_shape=list(ex.out_shapes), scratch_shapes=list(ex.sem_shapes))(*ex.ins)


def _adam_math(g, w_, m_, v_):
    m_new = ADAM_B1 * m_ + (1.0 - ADAM_B1) * g
    v_new = ADAM_B2 * v_ + (1.0 - ADAM_B2) * (g * g)
    m_hat = m_new / (1.0 - ADAM_B1 ** ADAM_STEP)
    v_hat = v_new / (1.0 - ADAM_B2 ** ADAM_STEP)
    delta = -ADAM_LR * (m_hat / (jnp.sqrt(v_hat) + ADAM_EPS) + ADAM_WD * w_)
    return delta, m_new, v_new


def _adamw_weight(name, parts, w, m, v):
    nl = len(parts)

    def body(*refs):
        p_refs = refs[:nl]
        w_ref, m_ref, v_ref, g_ref, d_ref, mo_ref, vo_ref = refs[nl:]
        for l in range(nl):
            g = p_refs[l][0].astype(F32)
            for j in range(1, N_DEV):
                g = g + p_refs[l][j].astype(F32)
            g_ref[l] = g
            d_ref[l], mo_ref[l], vo_ref[l] = _adam_math(g, w_ref[l], m_ref[l], v_ref[l])

    return _pcall(
        body, name=name, out_shape=[jax.ShapeDtypeStruct(w.shape, F32)] * 4,
        compiler_params=pltpu.CompilerParams(vmem_limit_bytes=VMEM_LIMIT),
    )(*parts, w, m, v)

def _sum_sources(name, parts):
    r = parts.shape[1]

    def body(p_ref, g_ref):
        g = p_ref[0]
        for j in range(1, N_DEV):
            g = g + p_ref[j]
        g_ref[...] = g

    return _pcall(body, name=name, out_shape=jax.ShapeDtypeStruct((r, LANES), F32),
                  compiler_params=pltpu.CompilerParams(vmem_limit_bytes=VMEM_LIMIT))(parts)


def _adamw_small(name, g, w, m, v):
    n = len(g)

    def body(*refs):
        g_r, w_r, m_r, v_r = (refs[k * n:(k + 1) * n] for k in range(4))
        d_r, mo_r, vo_r = (refs[k * n:(k + 1) * n] for k in range(4, 7))
        for i in range(n):
            d_r[i][...], mo_r[i][...], vo_r[i][...] = _adam_math(g_r[i][...], w_r[i][...], m_r[i][...], v_r[i][...])

    res = _pcall(body, name=name, out_shape=[jax.ShapeDtypeStruct(a.shape, F32) for a in w] * 3,
                 compiler_params=pltpu.CompilerParams(vmem_limit_bytes=VMEM_LIMIT))(*g, *w, *m, *v)
    return res[:n], res[n:2 * n], res[2 * n:]


def _pack(arrs):
    flat = jnp.concatenate([a.reshape(-1) for a in arrs])
    flat = jnp.pad(flat, (0, (-flat.shape[0]) % (8 * LANES)))
    return flat.reshape(-1, LANES)


def _unpack(packed, shapes):
    flat = packed.reshape(-1)
    out, off = [], 0
    for shp in shapes:
        size = 1
        for d in shp:
            size *= d
        out.append(flat[off:off + size].reshape(shp))
        off += size
    return out


def _s5_params(lam_re, lam_im, log_dt, b_re, b_im, c_re, c_im):
    dt = jnp.exp(log_dt)[:, None]
    e = jnp.exp(lam_re * dt)
    ang = lam_im * dt
    a_re, a_im = e * jnp.cos(ang), e * jnp.sin(ang)
    nr, ni = a_re - 1.0, a_im
    den = lam_re * lam_re + lam_im * lam_im
    cr = ((nr * lam_re + ni * lam_im) / den)[..., None]
    ci = ((ni * lam_re - nr * lam_im) / den)[..., None]
    bb_re = cr * b_re - ci * b_im
    bb_im = cr * b_im + ci * b_re
    eye = jnp.eye(8, dtype=F32)[None, :, None, :, None]

    def bblk(bb):
        t = jnp.transpose(bb.reshape(4, 8, SSM_STATE, SSM_GROUP), (0, 3, 1, 2))
        return (eye * t[:, None]).reshape(4, 8 * SSM_GROUP, 8 * SSM_STATE)

    def cblk(cc):
        t = jnp.transpose(cc.reshape(4, 8, SSM_GROUP, SSM_STATE), (0, 3, 1, 2))
        return (eye * t[:, None]).reshape(4, 8 * SSM_STATE, 8 * SSM_GROUP)

    nb = SSM_GROUPS * SSM_STATE // LANES
    return (a_re.reshape(nb, 1, LANES), a_im.reshape(nb, 1, LANES), bblk(bb_re), bblk(bb_im),
            cblk(c_re), -cblk(c_im))


def _cat_blocks(x3, j):
    return jnp.concatenate([x3[4 * j + k] for k in range(4)], axis=-1)


def _to_chunks(a):
    s, c = a.shape
    return a.reshape(8, s // 8, c).transpose(1, 0, 2).reshape(s, c)


def _from_chunks(a):
    s, c = a.shape
    return a.reshape(s // 8, 8, c).transpose(1, 0, 2).reshape(s, c)


EARLY = ['w_in', 'w_uq', 'w_ukv']

FWD_PLAN = {
    'flash': ('late', ['ssm_w_glu', 'w_out', 'w_xq', 'w_xkv', 'w_xo', 'w_gate', 'w_up', 'w_down']),
    'ffn': ('nxt', EARLY),
}
BWD_PLAN = {
    'ffn_bwd': ('nxt', EARLY),
    'xattn_bwd': ('own', ['w_down']),
    'flash_bwd': ('own', ['w_gate', 'w_up', 'w_xq', 'w_xkv', 'w_xo']),
    'mix_in_bwd': ('own', ['ssm_w_glu', 'w_out']),
}


def _named(names, d):
    return [d[n] for n in names]


def _layer_fwd(h, memx, tabs, wl, pl_, late=None, nxt=None):
    s = h.shape[0]
    tm = min(ROW_TILE, s)
    cos, sin_lo, sin_hi = tabs
    sv = {}
    wl = dict(wl)
    nxt_got = {}

    def fetch(host):
        who, names = FWD_PLAN.get(host, (None, []))
        src = late if who == 'late' else nxt if who == 'nxt' else None
        return _gather(_named(names, src)) if src else None

    def landed(host, got):
        who, names = FWD_PLAN.get(host, (None, []))
        if got and who == 'late':
            wl.update(_layer_weights(dict(zip(names, got))))
        elif got:
            nxt_got.update(zip(names, got))

    def f_mix_in(h_, g, w):
        xn, _ = _rms(h_, g[...])
        return (_mm(xn, w[...]),)
    proj, = _rows("mix_in", f_mix_in, s, tm, [(h, 'r0'), (pl_['norm_mix_g'], 'f'), (wl['w_in'], 'f')],
                  [((s, D_MODEL), F32, 'r0')])

    def f_qkv(pr, cos_, slo, shi, gq, gkv, wq, wk, wv):
        cqn = _rms(pr[:, 0:Q_LORA], gq[...])[0].astype(MXU)
        kvn = _rms(pr[:, Q_LORA:Q_LORA + KV_LORA], gkv[...])[0].astype(MXU)
        krr = _rope(pr[:, 384:512], cos_, slo, shi)
        qs, ks, vs = [], [], []
        for hd in range(MLA_HEADS):
            qs.append(_rope(_mm(cqn, wq[hd]), cos_, slo, shi) * MLA_SCALE)
            ks.append(_mm(kvn, wk[hd]) + krr)
            vs.append(_mm(kvn, wv[hd]))
        return jnp.stack(qs), jnp.stack(ks), jnp.stack(vs)
    hshape = (MLA_HEADS, s, HEAD_W)
    (q, k, v), got = _hosted(_rows(
        "mla_qkv", f_qkv, s, tm,
        [(proj, 'r0'), (cos, 'r0'), (sin_lo, 'r0'), (sin_hi, 'r0'), (pl_['q_norm_g'], 'f'), (pl_['kv_norm_g'], 'f'),
         (wl['w_uq'], 'f'), (wl['w_k'], 'f'), (wl['w_v'], 'f')],
        [(hshape, MXU, 'r1')] * 3, fetch('mla_qkv')))
    landed('mla_qkv', got)

    a_out, lse, got = _flash_fwd(q, k, v, fetch('flash'))
    landed('flash', got)

    u_ch = _to_chunks(proj[:, 512:1024])

    x_re, x_im = _scan(u_ch, pl_['b_re'], pl_['b_im'], pl_['a_re'], pl_['a_im'], False)

    def f_s5_out(xr, xi, u, cre, cim, d, wglu, bglu):
        y = jnp.concatenate([_mm(_cat_blocks(xr, j), cre[j]) + _mm(_cat_blocks(xi, j), cim[j])
                             for j in range(4)], axis=-1) + d[...] * u
        z = _mm(jax.nn.gelu(y), wglu[...]) + bglu[...]
        return y, y * jax.nn.sigmoid(z)
    (y_ssm, s_out_ch), got = _hosted(_rows(
        "s5_out", f_s5_out, s, tm,
        [(x_re, 'r1'), (x_im, 'r1'), (u_ch, 'r0'), (pl_['c_re'], 'f'), (pl_['c_im'], 'f'),
         (pl_['ssm_d'], 'f'), (wl['ssm_w_glu'], 'f'), (pl_['ssm_b_glu'], 'f')],
        [((s, SSM_WIDTH), F32, 'r0')] * 2, fetch('s5_out')))
    landed('s5_out', got)
    s_out = _from_chunks(s_out_ch)

    def f_mix_out(h_, a, so, ga, gs, w):
        an = _rms(a, ga[...])[0]
        sn = _rms(so, gs[...])[0]
        return (h_ + _mm(jnp.concatenate([an, sn], axis=-1), w[...]),)
    (h1,), got = _hosted(_rows("mix_out", f_mix_out, s, tm,
                               [(h, 'r0'), (a_out, 'r0'), (s_out, 'r0'), (pl_['attn_out_g'], 'f'),
                                (pl_['ssm_out_g'], 'f'), (wl['w_out'], 'f')],
                               [((s, D_MODEL), F32, 'r0')], fetch('mix_out')))
    landed('mix_out', got)

    m_len = memx.shape[0]

    def f_memkv(mm_, g, w):
        mn = _rms(mm_, g[...])[0].astype(MXU)
        return (jnp.stack([_mm(mn, w[d]) for d in range(N_DEV)]),)
    kvm, = _rows("mem_kv", f_memkv, m_len, m_len, [(memx, 'r0'), (pl_['mem_norm_g'], 'f'), (wl['w_xkv'], 'f')],
                 [((N_DEV, m_len, X_HEAD_DIM), MXU, 'r1')])

    def f_xattn(h_, g, wq, kv_, wo):
        hn = _rms(h_, g[...])[0].astype(MXU)
        q_all = _mm(hn, wq[...]).astype(MXU)
        outs = []
        for hd in range(X_HEADS):
            p = _softmax(_mm_nt(_lanes(q_all, hd, X_HEAD_DIM), kv_[hd]) * X_SCALE)
            outs.append(_mm(p, kv_[X_HEADS + hd]).astype(MXU))
        return (h_ + _mm(jnp.concatenate(outs, axis=-1), wo[...]),)
    (h2,), got = _hosted(_rows("xattn", f_xattn, s, min(X_ROWS, s),
                               [(h1, 'r0'), (pl_['norm_x_g'], 'f'), (wl['w_xq'], 'f'), (kvm, 'f'), (wl['w_xo'], 'f')],
                               [((s, D_MODEL), F32, 'r0')], fetch('xattn')))
    landed('xattn', got)

    def f_ffn(h_, g, wg, wu, wd):
        hn = _rms(h_, g[...])[0].astype(MXU)
        y = jnp.zeros(h_.shape, F32)
        gates, ups = [], []
        for c in range(D_FF // FF_CHUNK):
            cs = pl.ds(c * FF_CHUNK, FF_CHUNK)
            gate, up = _mm(hn, wg[:, cs]), _mm(hn, wu[:, cs])
            y = y + _mm(gate * jax.nn.sigmoid(gate) * up, wd[cs, :])
            gates.append(gate)
            ups.append(up)
        return h_ + y, jnp.concatenate(gates, axis=-1), jnp.concatenate(ups, axis=-1)
    (h3, gate_f, up_f), got = _hosted(_rows(
        "ffn", f_ffn, s, min(FFN_ROWS, s),
        [(h2, 'r0'), (pl_['norm_ffn_g'], 'f'), (wl['w_gate'], 'f'), (wl['w_up'], 'f'), (wl['w_down'], 'f')],
        [((s, D_MODEL), F32, 'r0'), ((s, D_FF), MXU, 'r0'), ((s, D_FF), MXU, 'r0')], fetch('ffn')))
    landed('ffn', got)
    sv.update(h=h, proj=proj, q=q, k=k, v=v, a_out=a_out, lse=lse, x_re=x_re, x_im=x_im, y_ssm=y_ssm,
              s_out=s_out, h1=h1, kvm=kvm, h2=h2, u_ch=u_ch, gate=gate_f, up=up_f)
    return h3, sv, wl, nxt_got


def _layer_bwd(dh3, sv, memx, tabs, wl, pl_, nxt=None):
    s = dh3.shape[0]
    tm = min(ROW_TILE, s)
    cos, sin_lo, sin_hi = tabs
    gr = {}
    arrived = {}
    act_shape = (s, D_FF)

    def send(host):
        who, names = BWD_PLAN.get(host, (None, []))
        if who is None or (who == 'nxt' and not nxt):
            return None, []
        return (_scatter([nxt[n] if who == 'nxt' else _blocked(gr, n) for n in names]),
                [(who, n) for n in names])

    def f_ffn_bwd(h_, dy, gate_, up_, g, wg, wu, wd):
        hn, r = _rms(h_, g[...])
        hb = hn.astype(MXU)
        dyb = dy.astype(MXU)
        dhn = jnp.zeros(h_.shape, F32)
        acts, dgs, dus = [], [], []
        for c in range(D_FF // FF_CHUNK):
            cs = pl.ds(c * FF_CHUNK, FF_CHUNK)
            gate = _lanes(gate_, c, FF_CHUNK).astype(F32)
            up = _lanes(up_, c, FF_CHUNK).astype(F32)
            sg = jax.nn.sigmoid(gate)
            si = gate * sg
            dact = _mm_nt(dyb, wd[cs, :])
            dgate = (dact * up * (sg * (1.0 + gate * (1.0 - sg)))).astype(MXU)
            dup = (dact * si).astype(MXU)
            dhn = dhn + _mm_nt(dgate, wg[:, cs]) + _mm_nt(dup, wu[:, cs])
            acts.append((si * up).astype(MXU))
            dgs.append(dgate)
            dus.append(dup)
        dh, dg = _rms_bwd(h_, g[...], r, dhn)
        cat = lambda parts: jnp.concatenate(parts, axis=-1)
        return dy + dh, hb, cat(acts), cat(dgs), cat(dus), dg
    ex, keys = send('ffn_bwd')
    (dh2, hn_f, act, dgate, dup, gr['norm_ffn_g']), got = _hosted(_rows(
        "ffn_bwd", f_ffn_bwd, s, min(FFN_ROWS, s),
        [(sv['h2'], 'r0'), (dh3, 'r0'), (sv['gate'], 'r0'), (sv['up'], 'r0'), (pl_['norm_ffn_g'], 'f'),
         (wl['w_gate'], 'f'), (wl['w_up'], 'f'), (wl['w_down'], 'f')],
        [((s, D_MODEL), F32, 'r0'), ((s, D_MODEL), MXU, 'r0'), (act_shape, MXU, 'r0'), (act_shape, MXU, 'r0'),
         (act_shape, MXU, 'r0'), ((1, D_MODEL), F32, 'a')], ex))
    arrived.update(zip(keys, got))
    gr['w_gate'] = _mm_tn_call("dw_gate", hn_f, dgate, tn=FF_CHUNK)
    gr['w_up'] = _mm_tn_call("dw_up", hn_f, dup, tn=FF_CHUNK)
    gr['w_down'] = _mm_tn_call("dw_down", act, dh3, tk=FF_CHUNK)

    m_len = memx.shape[0]

    def f_xattn_bwd(h_, dy, g, wq, kv_, wo):
        hn, r = _rms(h_, g[...])
        hb = hn.astype(MXU)
        q_all = _mm(hb, wq[...]).astype(MXU)
        do_all = _mm_nt(dy, wo[...]).astype(MXU)
        dqs, ohs, dks, dvs = [], [], [], []
        for hd in range(X_HEADS):
            kh, vh = kv_[hd], kv_[X_HEADS + hd]
            qh, do = _lanes(q_all, hd, X_HEAD_DIM), _lanes(do_all, hd, X_HEAD_DIM)
            p = _softmax(_mm_nt(qh, kh) * X_SCALE)
            ohs.append(_mm(p, vh).astype(MXU))
            dvs.append(_mm_tn(p, do))
            dp = _mm_nt(do, vh)
            ds = p * (dp - jnp.sum(dp * p, axis=-1, keepdims=True)) * X_SCALE
            dqs.append(_mm(ds, kh).astype(MXU))
            dks.append(_mm_tn(ds, qh))
        dq_all = jnp.concatenate(dqs, axis=-1)
        dh, dg = _rms_bwd(h_, g[...], r, _mm_nt(dq_all, wq[...]))
        return dy + dh, hb, dq_all, jnp.concatenate(ohs, axis=-1), jnp.stack(dks + dvs), dg
    ex, keys = send('xattn_bwd')
    (dh1, hn_x, dq_x, oh_x, dkvm, gr['norm_x_g']), got = _hosted(_rows(
        "xattn_bwd", f_xattn_bwd, s, min(X_ROWS, s),
        [(sv['h1'], 'r0'), (dh2, 'r0'), (pl_['norm_x_g'], 'f'), (wl['w_xq'], 'f'), (sv['kvm'], 'f'),
         (wl['w_xo'], 'f')],
        [((s, D_MODEL), F32, 'r0'), ((s, D_MODEL), MXU, 'r0'), ((s, D_MODEL), MXU, 'r0'),
         ((s, D_MODEL), MXU, 'r0'), ((N_DEV, m_len, X_HEAD_DIM), F32, 'a'), ((1, D_MODEL), F32, 'a')], ex))
    arrived.update(zip(keys, got))
    gr['w_xq'] = _mm_tn_call("dw_xq", hn_x, dq_x)
    gr['w_xo'] = _mm_tn_call("dw_xo", oh_x, dh2)

    def f_memkv_bwd(mm_, dkv, g, w):
        mn, r = _rms(mm_, g[...])
        mb = mn.astype(MXU)
        dmn = jnp.zeros(mm_.shape, F32)
        dws = []
        for d in range(N_DEV):
            dmn = dmn + _mm_nt(dkv[d], w[d])
            dws.append(_mm_tn(mb, dkv[d]))
        _, dg = _rms_bwd(mm_, g[...], r, dmn)
        return jnp.stack(dws), dg
    gr['w_xkv'], gr['mem_norm_g'] = _rows(
        "mem_kv_bwd", f_memkv_bwd, m_len, m_len,
        [(memx, 'r0'), (dkvm, 'r1'), (pl_['mem_norm_g'], 'f'), (wl['w_xkv'], 'f')],
        [((N_DEV, D_MODEL, X_HEAD_DIM), F32, 'a'), ((1, D_MODEL), F32, 'a')])

    def f_mix_out_bwd(a, so, dy, ga, gs, w):
        dmix = _mm_nt(dy, w[...])
        an, ra = _rms(a, ga[...])
        sn, rs = _rms(so, gs[...])
        da, dga = _rms_bwd(a, ga[...], ra, dmix[:, 0:512])
        dso, dgs = _rms_bwd(so, gs[...], rs, dmix[:, 512:1024])
        return da, dso, jnp.concatenate([an, sn], axis=-1), dga, dgs
    da_out, ds_out, mixed, gr['attn_out_g'], gr['ssm_out_g'] = _rows(
        "mix_out_bwd", f_mix_out_bwd, s, tm,
        [(sv['a_out'], 'r0'), (sv['s_out'], 'r0'), (dh1, 'r0'), (pl_['attn_out_g'], 'f'), (pl_['ssm_out_g'], 'f'),
         (wl['w_out'], 'f')],
        [((s, 512), F32, 'r0'), ((s, 512), F32, 'r0'), ((s, D_MODEL), MXU, 'r0'), ((1, 512), F32, 'a'),
         ((1, 512), F32, 'a')])
    gr['w_out'] = _mm_tn_call("dw_out", mixed, dh1)

    ex, keys = send('flash_bwd')
    dq, dk, dv, got = _flash_bwd(sv['q'], sv['k'], sv['v'], sv['a_out'], sv['lse'], da_out, ex)
    arrived.update(zip(keys, got))

    def f_s5_out_bwd(xr, xi, u, y, ds, cre, cim, d, wglu, bglu):
        g, gelu_vjp = jax.vjp(jax.nn.gelu, y)
        sig = jax.nn.sigmoid(_mm(g, wglu[...]) + bglu[...])
        dz = ds * y * sig * (1.0 - sig)
        dy = ds * sig + gelu_vjp(_mm_nt(dz, wglu[...]))[0]
        dcr, dci = [], []
        for j in range(4):
            dyj = _lanes(dy, j, LANES)
            dcr.append(_mm_tn(_cat_blocks(xr, j), dyj))
            dci.append(_mm_tn(_cat_blocks(xi, j), dyj))
        return (dy, dy * d[...], jnp.stack(dcr), jnp.stack(dci),
                jnp.sum(dy * u, axis=0, keepdims=True), _mm_tn(g, dz), jnp.sum(dz, axis=0, keepdims=True))
    ex, keys = send('s5_out_bwd')
    (dy_ssm, du_dir, gr['c_re'], gr['c_im'], gr['ssm_d'], gr['ssm_w_glu'], gr['ssm_b_glu']), got = _hosted(_rows(
        "s5_out_bwd", f_s5_out_bwd, s, tm,
        [(sv['x_re'], 'r1'), (sv['x_im'], 'r1'), (sv['u_ch'], 'r0'), (sv['y_ssm'], 'r0'), (_to_chunks(ds_out), 'r0'),
         (pl_['c_re'], 'f'), (pl_['c_im'], 'f'), (pl_['ssm_d'], 'f'), (wl['ssm_w_glu'], 'f'),
         (pl_['ssm_b_glu'], 'f')],
        [((s, 512), F32, 'r0'), ((s, 512), F32, 'r0'), ((4, 512, LANES), F32, 'a'),
         ((4, 512, LANES), F32, 'a'), ((1, 512), F32, 'a'), ((512, 512), F32, 'a'), ((1, 512), F32, 'a')], ex))
    arrived.update(zip(keys, got))
    g_re, g_im = _scan(dy_ssm, pl_['c_re'], pl_['c_im'], pl_['a_re'], -pl_['a_im'], True)
    first_re = jnp.pad(sv['x_re'][:, s - 8:s - 1], ((0, 0), (1, 0), (0, 0)))
    first_im = jnp.pad(sv['x_im'][:, s - 8:s - 1], ((0, 0), (1, 0), (0, 0)))

    def f_s5_in_bwd(gre, gim, xr, xi, pr8, pi8, u, dud, f8r, f8i, bre, bim):
        first = pl.program_id(0) == 0
        xpr = jnp.concatenate([jnp.where(first, f8r[...], pr8), xr[:, :tm - 8]], axis=1)
        xpi = jnp.concatenate([jnp.where(first, f8i[...], pi8), xi[:, :tm - 8]], axis=1)
        dus, dbr, dbi = [], [], []
        for j in range(4):
            gj_r, gj_i, uj = _cat_blocks(gre, j), _cat_blocks(gim, j), _lanes(u, j, LANES)
            dus.append(_mm_nt(gj_r, bre[j]) + _mm_nt(gj_i, bim[j]))
            dbr.append(_mm_tn(uj, gj_r))
            dbi.append(_mm_tn(uj, gj_i))
        da_r = jnp.sum(gre * xpr + gim * xpi, axis=1, keepdims=True)
        da_i = jnp.sum(gim * xpr - gre * xpi, axis=1, keepdims=True)
        return dud + jnp.concatenate(dus, axis=-1), jnp.stack(dbr), jnp.stack(dbi), da_r, da_i
    ex, keys = send('s5_in_bwd')
    (du_ch, gr['b_re'], gr['b_im'], gr['a_re'], gr['a_im']), got = _hosted(_rows(
        "s5_in_bwd", f_s5_in_bwd, s, tm,
        [(g_re, 'r1'), (g_im, 'r1'), (sv['x_re'], 'r1'), (sv['x_im'], 'r1'), (sv['x_re'], 'p8'), (sv['x_im'], 'p8'),
         (sv['u_ch'], 'r0'), (du_dir, 'r0'), (first_re, 'f'), (first_im, 'f'), (pl_['b_re'], 'f'), (pl_['b_im'], 'f')],
        [((s, 512), F32, 'r0'), ((4, LANES, 512), F32, 'a'), ((4, LANES, 512), F32, 'a'),
         ((16, 1, LANES), F32, 'a'), ((16, 1, LANES), F32, 'a')], ex))
    arrived.update(zip(keys, got))
    du = _from_chunks(du_ch)

    def f_qkv_bwd(pr, cos_, slo, shi, dq_, dk_, dv_, gq, gkv, wq, wk, wv):
        cq, ckv = pr[:, 0:Q_LORA], pr[:, Q_LORA:Q_LORA + KV_LORA]
        cqn, rq = _rms(cq, gq[...])
        kvn, rkv = _rms(ckv, gkv[...])
        cqb, kvb = cqn.astype(MXU), kvn.astype(MXU)
        dcqn = jnp.zeros(cq.shape, F32)
        dkvn = jnp.zeros(ckv.shape, F32)
        dksum = jnp.zeros(dk_[0].shape, F32)
        dwq, dwk, dwv = [], [], []
        for hd in range(MLA_HEADS):
            dqp = (_rope_t(dq_[hd], cos_, slo, shi) * MLA_SCALE).astype(MXU)
            dkb, dvb = dk_[hd].astype(MXU), dv_[hd].astype(MXU)
            dwq.append(_mm_tn(cqb, dqp))
            dwk.append(_mm_tn(kvb, dkb))
            dwv.append(_mm_tn(kvb, dvb))
            dcqn = dcqn + _mm_nt(dqp, wq[hd])
            dkvn = dkvn + _mm_nt(dkb, wk[hd]) + _mm_nt(dvb, wv[hd])
            dksum = dksum + dk_[hd]
        dcq, dgq = _rms_bwd(cq, gq[...], rq, dcqn)
        dckv, dgkv = _rms_bwd(ckv, gkv[...], rkv, dkvn)
        dpa = jnp.concatenate([dcq, dckv, _rope_t(dksum, cos_, slo, shi)], axis=-1)
        return dpa, jnp.stack(dwq), jnp.stack(dwk), jnp.stack(dwv), dgq, dgkv
    ex, keys = send('mla_qkv_bwd')
    (dpa, gr['w_uq'], gr['w_k'], gr['w_v'], gr['q_norm_g'], gr['kv_norm_g']), got = _hosted(_rows(
        "mla_qkv_bwd", f_qkv_bwd, s, tm,
        [(sv['proj'], 'r0'), (cos, 'r0'), (sin_lo, 'r0'), (sin_hi, 'r0'), (dq, 'r1'), (dk, 'r1'), (dv, 'r1'),
         (pl_['q_norm_g'], 'f'), (pl_['kv_norm_g'], 'f'), (wl['w_uq'], 'f'), (wl['w_k'], 'f'), (wl['w_v'], 'f')],
        [((s, 512), F32, 'r0'), ((MLA_HEADS, Q_LORA, HEAD_W), F32, 'a'), ((MLA_HEADS, KV_LORA, HEAD_W), F32, 'a'),
         ((MLA_HEADS, KV_LORA, HEAD_W), F32, 'a'), ((1, Q_LORA), F32, 'a'), ((1, KV_LORA), F32, 'a')], ex))
    arrived.update(zip(keys, got))

    def f_mix_in_bwd(h_, dpa_, du_, dres, g, w):
        dproj = jnp.concatenate([dpa_, du_], axis=-1).astype(MXU)
        xn, r = _rms(h_, g[...])
        dh, dg = _rms_bwd(h_, g[...], r, _mm_nt(dproj, w[...]))
        return dres + dh, xn, dproj, dg
    ex, keys = send('mix_in_bwd')
    (dh0, xn, dproj, gr['norm_mix_g']), got = _hosted(_rows(
        "mix_in_bwd", f_mix_in_bwd, s, tm,
        [(sv['h'], 'r0'), (dpa, 'r0'), (du, 'r0'), (dh1, 'r0'), (pl_['norm_mix_g'], 'f'), (wl['w_in'], 'f')],
        [((s, D_MODEL), F32, 'r0'), ((s, D_MODEL), MXU, 'r0'), ((s, D_MODEL), MXU, 'r0'), ((1, D_MODEL), F32, 'a')],
        ex))
    arrived.update(zip(keys, got))
    gr['w_in'] = _mm_tn_call("dw_in", xn, dproj)
    return dh0, gr, arrived


def _layer_weights(w):
    wl = {}
    if 'w_in' in w:
        w_in = w['w_in'].reshape(D_MODEL, -1)
        z = lambda n: jnp.zeros((D_MODEL, n), w_in.dtype)
        wl['w_in'] = jnp.concatenate([w_in[:, :384], z(64), w_in[:, 384:416], z(32), w_in[:, 416:]], axis=1)
    if 'w_uq' in w:
        wl['w_uq'] = jnp.pad(w['w_uq'], ((0, 0), (0, 0), (0, HEAD_W - QK_NOPE - QK_ROPE)))
    if 'w_ukv' in w:
        wl['w_k'] = jnp.pad(w['w_ukv'][..., :QK_NOPE], ((0, 0), (0, 0), (0, HEAD_W - QK_NOPE)))
        wv = w['w_ukv'][..., QK_NOPE:]
        even = (jnp.arange(MLA_HEADS) % 2 == 0)[:, None, None]
        wl['w_v'] = jnp.concatenate([jnp.where(even, wv, 0), jnp.where(even, 0, wv)], axis=-1).astype(wv.dtype)
    if 'ssm_w_glu' in w:
        wl['ssm_w_glu'] = w['ssm_w_glu'].reshape(SSM_WIDTH, SSM_WIDTH)
    for n in ('w_out', 'w_xq', 'w_xo'):
        if n in w:
            wl[n] = w[n].reshape(D_MODEL, D_MODEL)
    if 'w_xkv' in w:
        wl['w_xkv'] = w['w_xkv']
    for n in ('w_gate', 'w_up'):
        if n in w:
            wl[n] = jnp.transpose(w[n], (1, 0, 2)).reshape(D_MODEL, D_FF)
    if 'w_down' in w:
        wl['w_down'] = w['w_down'].reshape(D_FF, D_MODEL)
    return wl


def _blocked(gr, n):
    if n == 'w_in':
        d = gr['w_in']
        out = jnp.concatenate([d[:, :384], d[:, 448:480], d[:, 512:]], axis=1).reshape(N_DEV, 128, -1)
    elif n == 'w_uq':
        out = gr['w_uq'][..., :QK_NOPE + QK_ROPE]
    elif n == 'w_ukv':
        even = (jnp.arange(MLA_HEADS) % 2 == 0)[:, None, None]
        dv = gr['w_v']
        out = jnp.concatenate([gr['w_k'][..., :QK_NOPE], jnp.where(even, dv[..., :V_DIM], dv[..., V_DIM:])], axis=-1)
    elif n == 'ssm_w_glu':
        out = gr['ssm_w_glu'].reshape(N_DEV, SSM_WIDTH // N_DEV, SSM_WIDTH)
    elif n in ('w_out', 'w_xq', 'w_xo'):
        out = gr[n].reshape(N_DEV, D_MODEL // N_DEV, D_MODEL)
    elif n in ('w_gate', 'w_up'):
        out = jnp.transpose(gr[n].reshape(D_MODEL, N_DEV, D_FF // N_DEV), (1, 0, 2))
    elif n == 'w_down':
        out = gr[n].reshape(N_DEV, D_FF // N_DEV, D_MODEL)
    else:
        out = gr[n]
    return out.astype(MXU)


def kernel(x, mem, positions, norm_mix_g, w_in, q_norm_g, w_uq, kv_norm_g, w_ukv, ssm_lambda_re, ssm_lambda_im, ssm_log_dt, ssm_b_re, ssm_b_im, ssm_c_re, ssm_c_im, ssm_d, ssm_w_glu, ssm_b_glu, attn_out_g, ssm_out_g, w_out, norm_x_g, mem_norm_g, w_xq, w_xkv, w_xo, norm_ffn_g, w_gate, w_up, w_down, final_norm_g, loss_target, m_norm_mix_g, m_w_in, m_q_norm_g, m_w_uq, m_kv_norm_g, m_w_ukv, m_ssm_lambda_re, m_ssm_lambda_im, m_ssm_log_dt, m_ssm_b_re, m_ssm_b_im, m_ssm_c_re, m_ssm_c_im, m_ssm_d, m_ssm_w_glu, m_ssm_b_glu, m_attn_out_g, m_ssm_out_g, m_w_out, m_norm_x_g, m_mem_norm_g, m_w_xq, m_w_xkv, m_w_xo, m_norm_ffn_g, m_w_gate, m_w_up, m_w_down, m_final_norm_g, v_norm_mix_g, v_w_in, v_q_norm_g, v_w_uq, v_kv_norm_g, v_w_ukv, v_ssm_lambda_re, v_ssm_lambda_im, v_ssm_log_dt, v_ssm_b_re, v_ssm_b_im, v_ssm_c_re, v_ssm_c_im, v_ssm_d, v_ssm_w_glu, v_ssm_b_glu, v_attn_out_g, v_ssm_out_g, v_w_out, v_norm_x_g, v_mem_norm_g, v_w_xq, v_w_xkv, v_w_xo, v_norm_ffn_g, v_w_gate, v_w_up, v_w_down, v_final_norm_g):
    args = dict(locals())
    W = {n: args[n] for n in WEIGHTS}
    M = {n: args['m_' + n] for n in WEIGHTS}
    V = {n: args['v_' + n] for n in WEIGHTS}
    s = x.shape[1]
    h = x[0]
    memx = mem[0]

    freqs = ROPE_THETA ** (-jnp.arange(0, QK_ROPE, 2, dtype=F32) / QK_ROPE)
    ang = positions[0].astype(F32)[:, None] * freqs
    c16, s16 = jnp.cos(ang), jnp.sin(ang)
    zeros = lambda n: jnp.zeros((s, n), F32)
    cos = jnp.concatenate([jnp.ones((s, QK_NOPE), F32), c16, c16, zeros(32)], axis=1)
    sin_lo = jnp.concatenate([zeros(QK_NOPE), -s16, zeros(ROT + 32)], axis=1)
    sin_hi = jnp.concatenate([zeros(QK_NOPE + ROT), s16, zeros(32)], axis=1)
    tabs = (cos, sin_lo, sin_hi)

    shards = [{n: W[n][l].astype(MXU) for n in SHARDED} for l in range(DEPTH)]
    gathered = dict(zip(EARLY, _run_exchange("gather_weights", _gather(_named(EARLY, shards[0])))))

    layers = []
    for l in range(DEPTH):
        wl = _layer_weights(gathered)
        s5_in = [W[n][l] for n in ('ssm_lambda_re', 'ssm_lambda_im', 'ssm_log_dt', 'ssm_b_re', 'ssm_b_im',
                                   'ssm_c_re', 'ssm_c_im')]
        (a_re, a_im, bre, bim, cre, cim), s5_vjp = jax.vjp(_s5_params, *s5_in)
        pl_ = {n: W[n][l][None] for n in ('norm_mix_g', 'q_norm_g', 'kv_norm_g', 'ssm_d', 'ssm_b_glu',
                                           'attn_out_g', 'ssm_out_g', 'norm_x_g', 'mem_norm_g', 'norm_ffn_g')}
        pl_.update(a_re=a_re, a_im=a_im, b_re=bre, b_im=bim, c_re=cre, c_im=cim)
        h, sv, wl, gathered = _layer_fwd(h, memx, tabs, wl, pl_, shards[l], shards[l + 1] if l + 1 < DEPTH else None)
        layers.append((wl, pl_, sv, s5_vjp))

    def f_loss(h_, tgt, g):
        y, r = _rms(h_, g[...])
        err = y - tgt
        part = 0.5 * jnp.sum(jnp.mean(err * err, axis=-1, keepdims=True), axis=0, keepdims=True)
        dh, dg = _rms_bwd(h_, g[...], r, err / D_MODEL)
        return dh, dg, jnp.broadcast_to(part, (8, LANES))
    dh, g_final, loss_part = _rows(
        "loss_head", f_loss, s, min(ROW_TILE, s), [(h, 'r0'), (loss_target[0], 'r0'), (final_norm_g[None], 'f')],
        [((s, D_MODEL), F32, 'r0'), ((1, D_MODEL), F32, 'a'), ((8, LANES), F32, 'a')])
    loss = lax.psum(loss_part[0, 0], ("x", "y", "c"))

    parts = [{} for _ in range(DEPTH)]
    g_rep = [None] * DEPTH
    blocks = None
    for l in reversed(range(DEPTH)):
        wl, pl_, sv, s5_vjp = layers[l]
        dh, gr, arrived = _layer_bwd(dh, sv, memx, tabs, wl, pl_, blocks)
        for (who, n), p in arrived.items():
            parts[l + 1 if who == 'nxt' else l][n] = p
        blocks = {n: _blocked(gr, n) for n in EARLY}
        ds5 = s5_vjp((gr['a_re'], gr['a_im'], gr['b_re'], gr['b_im'], gr['c_re'], gr['c_im']))
        rep = dict(zip(('ssm_lambda_re', 'ssm_lambda_im', 'ssm_log_dt', 'ssm_b_re', 'ssm_b_im', 'ssm_c_re',
                        'ssm_c_im'), ds5))
        for n in ('norm_mix_g', 'q_norm_g', 'kv_norm_g', 'ssm_d', 'ssm_b_glu', 'attn_out_g', 'ssm_out_g',
                  'norm_x_g', 'mem_norm_g', 'norm_ffn_g'):
            rep[n] = gr[n][0]
        g_rep[l] = rep
    grad_x = dh[None]

    rep_names = REPL_L + ['final_norm_g']
    g_loc = {n: jnp.stack([g_rep[l][n] for l in range(DEPTH)]) for n in REPL_L}
    g_loc['final_norm_g'] = g_final
    rest = [n for n in SHARDED if n not in parts[0]]
    last = _run_exchange("last_grads", _together(_scatter(_named(rest, blocks)),
                                                 _gather([_pack(_named(rep_names, g_loc))])))
    parts[0].update(zip(rest, last[:len(rest)]))

    out_sh = [{}, {}, {}, {}]
    for n in SHARDED:
        res = _adamw_weight("adamw_" + n, [parts[l][n] for l in range(DEPTH)], W[n], M[n], V[n])
        for kind, r in enumerate(res):
            out_sh[kind][n] = r

    shapes_rp = [(1,) + W[n].shape if W[n].ndim == 1 else W[n].shape for n in rep_names]
    g_rp = _unpack(_sum_sources("sum_small_grads", last[len(rest)]), shapes_rp)
    as_rows = lambda d: [d[n].reshape(shp) for n, shp in zip(rep_names, shapes_rp)]
    res_rp = (g_rp,) + _adamw_small("adamw_replicated", g_rp, as_rows(W), as_rows(M), as_rows(V))
    out_rp = [{n: a.reshape(W[n].shape) for n, a in zip(rep_names, r)} for r in res_rp]

    outs = [loss, grad_x]
    for kind in range(4):
        for n in WEIGHTS:
            outs.append(out_sh[kind][n] if n in SHARDED else out_rp[kind][n])
    return tuple(outs)
```

```python
from typing import Callable, NamedTuple

import jax
import jax.numpy as jnp
from jax import lax
from jax.experimental import pallas as pl
from jax.experimental.pallas import tpu as pltpu

F32 = jnp.float32
MXU = jnp.bfloat16

D_MODEL = 1024
MLA_HEADS = 8
QK_NOPE = 64
QK_ROPE = 32
V_DIM = 64
Q_LORA = 256
KV_LORA = 128
SSM_WIDTH = 512
SSM_GROUPS = 32
SSM_GROUP = 16
SSM_STATE = 64
X_HEADS = 4
X_HEAD_DIM = 256
D_FF = 2816
FF_CHUNK = D_FF // 2
ROPE_THETA = 10000.0
EPS = 1e-6
DEPTH = 2
N_DEV = 8
LANES = 128
HEAD_W = 128
MLA_SCALE = (QK_NOPE + QK_ROPE) ** -0.5
X_SCALE = X_HEAD_DIM ** -0.5
ADAM_LR, ADAM_B1, ADAM_B2, ADAM_EPS, ADAM_WD, ADAM_STEP = 0.001, 0.9, 0.999, 1e-08, 0.01, 10
VMEM_LIMIT = 56 * 1024 * 1024
FLASH_TILE = 512
DW_ROWS = 2048
X_ROWS = 512
ROW_TILE = 512
FFN_ROWS = 256
MESH = pl.DeviceIdType.MESH

SHARDED = ['w_in', 'w_uq', 'w_ukv', 'ssm_w_glu', 'w_out', 'w_xq', 'w_xkv', 'w_xo', 'w_gate', 'w_up', 'w_down']
REPL_L = ['norm_mix_g', 'q_norm_g', 'kv_norm_g', 'ssm_lambda_re', 'ssm_lambda_im', 'ssm_log_dt', 'ssm_b_re',
          'ssm_b_im', 'ssm_c_re', 'ssm_c_im', 'ssm_d', 'ssm_b_glu', 'attn_out_g', 'ssm_out_g', 'norm_x_g',
          'mem_norm_g', 'norm_ffn_g']
WEIGHTS = ['norm_mix_g', 'w_in', 'q_norm_g', 'w_uq', 'kv_norm_g', 'w_ukv', 'ssm_lambda_re', 'ssm_lambda_im',
           'ssm_log_dt', 'ssm_b_re', 'ssm_b_im', 'ssm_c_re', 'ssm_c_im', 'ssm_d', 'ssm_w_glu', 'ssm_b_glu',
           'attn_out_g', 'ssm_out_g', 'w_out', 'norm_x_g', 'mem_norm_g', 'w_xq', 'w_xkv', 'w_xo', 'norm_ffn_g',
           'w_gate', 'w_up', 'w_down', 'final_norm_g']


def _pcall(body, **kw):
    return pl.pallas_call(body, **kw)


def _mm(a, b):
    return jnp.dot(a.astype(MXU), b.astype(MXU), preferred_element_type=F32)


def _mm_nt(a, b):
    return lax.dot_general(a.astype(MXU), b.astype(MXU), (((1,), (1,)), ((), ())), preferred_element_type=F32)


def _mm_tn(a, b):
    return lax.dot_general(a.astype(MXU), b.astype(MXU), (((0,), (0,)), ((), ())), preferred_element_type=F32)


def _rms(x, g):
    r = lax.rsqrt(jnp.mean(x * x, axis=-1, keepdims=True) + EPS)
    return x * r * g, r


def _rms_bwd(x, g, r, dy):
    dyg = dy * g
    dx = r * dyg - x * (r * r * r) * jnp.mean(dyg * x, axis=-1, keepdims=True)
    return dx, jnp.sum(dy * x * r, axis=0, keepdims=True)


ROT = QK_ROPE // 2


def _rope(x, cos, sin_lo, sin_hi):
    return x * cos + pltpu.roll(x, HEAD_W - ROT, 1) * sin_lo + pltpu.roll(x, ROT, 1) * sin_hi


def _rope_t(g, cos, sin_lo, sin_hi):
    return g * cos + pltpu.roll(g * sin_lo, ROT, 1) + pltpu.roll(g * sin_hi, HEAD_W - ROT, 1)


def _softmax(s):
    m = jnp.max(s, axis=-1, keepdims=True)
    e = jnp.exp(s - m)
    return e / jnp.sum(e, axis=-1, keepdims=True)


def _lanes(x, j, w):
    return x[:, j * w:(j + 1) * w]


def _rows(name, fn, n, tm, ins, outs, side=None):
    def spec(shape, kind):
        nd = len(shape)
        if kind == 'p8':
            return pl.BlockSpec((shape[0], 8, shape[2]), lambda i: (0, jnp.maximum(i * (tm // 8) - 1, 0), 0))
        if kind == 'f':
            return pl.BlockSpec(shape, lambda i, _nd=nd: (0,) * _nd, pipeline_mode=pl.Buffered(1))
        if kind == 'a':
            return pl.BlockSpec(shape, lambda i, _nd=nd: (0,) * _nd)
        ax = int(kind[1])
        blk = tuple(tm if d == ax else s for d, s in enumerate(shape))
        return pl.BlockSpec(blk, lambda i, _ax=ax, _nd=nd: tuple(i if d == _ax else 0 for d in range(_nd)))

    n_in, n_out, n_steps = len(ins), len(outs), n // tm

    def body(*refs):
        in_refs, out_refs, steps = _side_split(refs, n_in, n_out, side)
        i = pl.program_id(0)
        if steps:
            pl.when(i == 0)(steps[0])
            pl.when(i == _pass_on_step(n_steps))(steps[1])
        args = [r if k == 'f' else r[...] for r, (_, k) in zip(in_refs, ins)]
        res = fn(*args)
        for r, (_, dt, k), v in zip(out_refs, outs, res):
            if k == 'a':
                _accumulate(r, v.astype(dt), i)
            else:
                r[...] = v.astype(dt)
        if steps:
            pl.when(i == n_steps - 1)(steps[2])

    s_in, s_out, s_shape, s_sems, s_ops = _side_args(side)
    res = _pcall(
        body, name=name + ("_x" if side else ""), grid=(n_steps,),
        in_specs=[spec(a.shape, k) for a, k in ins] + s_in,
        out_specs=[spec(s, k) for s, _, k in outs] + s_out,
        out_shape=[jax.ShapeDtypeStruct(s, dt) for s, dt, _ in outs] + s_shape,
        scratch_shapes=s_sems,
        compiler_params=pltpu.CompilerParams(dimension_semantics=("arbitrary",), vmem_limit_bytes=VMEM_LIMIT),
    )(*[a for a, _ in ins], *s_ops)
    return _Hosted(res[:n_out], res[n_out:]) if side else res


def _accumulate(ref, v, i):
    @pl.when(i == 0)
    def _():
        ref[...] = v

    @pl.when(i != 0)
    def _():
        ref[...] += v


def _mm_tn_call(name, a, b, tk=None, tn=None):
    out_dtype = MXU
    s, k = a.shape
    n = b.shape[1]
    tk, tn = tk or k, tn or n
    ts = min(DW_ROWS, s)
    ns = s // ts

    def body(a_ref, b_ref, o_ref, acc_ref):
        j = pl.program_id(2)
        _accumulate(acc_ref, _mm_tn(a_ref[...], b_ref[...]), j)

        @pl.when(j == ns - 1)
        def _():
            o_ref[...] = acc_ref[...].astype(out_dtype)

    return _pcall(
        body, name=name, grid=(k // tk, n // tn, ns),
        in_specs=[pl.BlockSpec((ts, tk), lambda ik, jn, j: (j, ik)),
                  pl.BlockSpec((ts, tn), lambda ik, jn, j: (j, jn))],
        out_specs=pl.BlockSpec((tk, tn), lambda ik, jn, j: (ik, jn)),
        out_shape=jax.ShapeDtypeStruct((k, n), out_dtype),
        scratch_shapes=[pltpu.VMEM((tk, tn), F32)],
        compiler_params=pltpu.CompilerParams(dimension_semantics=("arbitrary", "arbitrary", "arbitrary"),
                                             vmem_limit_bytes=VMEM_LIMIT),
    )(a, b)


def _side_split(refs, n_in, n_out, side):
    if side is None:
        return refs[:n_in], refs[n_in:n_in + n_out], None
    si, so = len(side.ins), len(side.out_shapes)
    own_in, side_in = refs[:n_in], refs[n_in:n_in + si]
    own_out, side_out = refs[n_in + si:n_in + si + n_out], refs[n_in + si + n_out:n_in + si + n_out + so]
    return own_in, own_out, side.steps(side_in, side_out, refs[n_in + si + n_out + so:])


def _pass_on_step(n_steps):
    return max(n_steps - 2, 0)


def _side_args(side):
    if side is None:
        return [], [], [], [], []
    any_spec = pl.BlockSpec(memory_space=pl.ANY)
    return ([any_spec] * len(side.ins), [any_spec] * len(side.out_shapes), list(side.out_shapes),
            list(side.sem_shapes), list(side.ins))


class _Hosted(NamedTuple):
    results: list
    arrived: list


def _hosted(res):
    return res if isinstance(res, _Hosted) else _Hosted(res, ())


def _flash_fwd(q, k, v, side=None):
    nh, s, w = q.shape
    t = min(FLASH_TILE, s)
    nq = s // t
    n_steps = (nh // 2) * nq

    def body(*refs):
        (q_ref, k_ref, v_ref), (o_ref, lse_ref), steps = _side_split(refs, 3, 2, side)
        step = pl.program_id(0) * nq + pl.program_id(1)
        if steps:
            pl.when(step == 0)(steps[0])
            pl.when(step == _pass_on_step(n_steps))(steps[1])
        qi = pl.program_id(1)
        qs = [q_ref[0], q_ref[1]]
        below = lax.broadcasted_iota(jnp.int32, (t, t), 1) <= lax.broadcasted_iota(jnp.int32, (t, t), 0)

        def tile(j, carry, diagonal):
            sl = pl.ds(pl.multiple_of(j * t, t), t)
            out = []
            for hh in range(2):
                m, l, acc = carry[3 * hh:3 * hh + 3]
                sc = _mm_nt(qs[hh], k_ref[hh, sl, :])
                if diagonal:
                    sc = jnp.where(below, sc, -1e30)
                m_new = jnp.maximum(m, jnp.max(sc, axis=1, keepdims=True))
                p = jnp.exp(sc - m_new)
                alpha = jnp.exp(m - m_new)
                out += [m_new, alpha * l + jnp.sum(p, axis=1, keepdims=True), alpha * acc + _mm(p, v_ref[hh, sl, :])]
            return tuple(out)

        init = (jnp.full((t, 1), -1e30, F32), jnp.zeros((t, 1), F32), jnp.zeros((t, w), F32)) * 2
        carry = lax.fori_loop(0, qi, lambda j, c: tile(j, c, False), init)
        carry = tile(qi, carry, True)
        o_ref[...] = carry[2] / carry[1] + carry[5] / carry[4]
        for hh in range(2):
            lse_ref[hh] = jnp.broadcast_to(carry[3 * hh] + jnp.log(carry[3 * hh + 1]), (t, w))
        if steps:
            pl.when(step == n_steps - 1)(steps[2])

    s_in, s_out, s_shape, s_sems, s_ops = _side_args(side)
    res = _pcall(
        body, name="mla_flash_fwd" + ("_x" if side else ""), grid=(nh // 2, nq),
        in_specs=[pl.BlockSpec((2, t, w), lambda p, i: (p, i, 0)),
                  pl.BlockSpec((2, s, w), lambda p, i: (p, 0, 0)),
                  pl.BlockSpec((2, s, w), lambda p, i: (p, 0, 0))] + s_in,
        out_specs=[pl.BlockSpec((t, w), lambda p, i: (i, p)),
                   pl.BlockSpec((2, t, w), lambda p, i: (p, i, 0))] + s_out,
        out_shape=[jax.ShapeDtypeStruct((s, (nh // 2) * w), F32), jax.ShapeDtypeStruct((nh, s, w), F32)] + s_shape,
        scratch_shapes=s_sems,
        compiler_params=pltpu.CompilerParams(dimension_semantics=("arbitrary", "arbitrary"),
                                             vmem_limit_bytes=VMEM_LIMIT),
    )(q, k, v, *s_ops)
    return res[0], res[1], res[2:]


def _flash_bwd(q, k, v, o, lse, do, side=None):
    nh, s, w = q.shape
    t = min(FLASH_TILE, s)
    nq = s // t
    n_steps = (nh // 2) * nq

    def body(*refs):
        (q_ref, k_ref, v_ref, o_ref, lse_ref, do_ref), (dq_ref, dk_ref, dv_ref), steps = _side_split(refs, 6, 3, side)
        step = pl.program_id(0) * nq + pl.program_id(1)
        if steps:
            pl.when(step == 0)(steps[0])
            pl.when(step == _pass_on_step(n_steps))(steps[1])
        j = pl.program_id(1)

        @pl.when(j == 0)
        def _():
            dq_ref[...] = jnp.zeros(dq_ref.shape, F32)

        below = lax.broadcasted_iota(jnp.int32, (t, t), 1) <= lax.broadcasted_iota(jnp.int32, (t, t), 0)
        lane = lax.broadcasted_iota(jnp.int32, (t, w), 1)
        heads = [jnp.logical_and(lane >= hh * V_DIM, lane < (hh + 1) * V_DIM) for hh in range(2)]
        ks = [k_ref[0], k_ref[1]]
        vs = [v_ref[0], v_ref[1]]

        def tile(i, carry, diagonal):
            sl = pl.ds(pl.multiple_of(i * t, t), t)
            dout_all, o_all = do_ref[sl, :], o_ref[sl, :]
            out = []
            for hh in range(2):
                dk, dv = carry[2 * hh], carry[2 * hh + 1]
                qh = q_ref[hh, sl, :]
                dout = jnp.where(heads[hh], dout_all, 0.0)
                sc = _mm_nt(qh, ks[hh])
                if diagonal:
                    sc = jnp.where(below, sc, -1e30)
                p = jnp.exp(sc - lse_ref[hh, sl, 0:1])
                dp = _mm_nt(dout, vs[hh])
                ds = p * (dp - jnp.sum(dout * o_all, axis=1, keepdims=True))
                dq_ref[hh, sl, :] += _mm(ds, ks[hh])
                out += [dk + _mm_tn(ds, qh), dv + _mm_tn(p, dout)]
            return tuple(out)

        carry = tile(j, (jnp.zeros((t, w), F32),) * 4, True)
        carry = lax.fori_loop(j + 1, nq, lambda i, c: tile(i, c, False), carry)
        for hh in range(2):
            dk_ref[hh] = carry[2 * hh]
            dv_ref[hh] = jnp.where(heads[hh], carry[2 * hh + 1], 0.0)
        if steps:
            pl.when(step == n_steps - 1)(steps[2])

    s_in, s_out, s_shape, s_sems, s_ops = _side_args(side)
    res = _pcall(
        body, name="mla_flash_bwd" + ("_x" if side else ""), grid=(nh // 2, nq),
        in_specs=[pl.BlockSpec((2, s, w), lambda p, j: (p, 0, 0)),
                  pl.BlockSpec((2, t, w), lambda p, j: (p, j, 0)),
                  pl.BlockSpec((2, t, w), lambda p, j: (p, j, 0)),
                  pl.BlockSpec((s, w), lambda p, j: (0, p)),
                  pl.BlockSpec((2, s, w), lambda p, j: (p, 0, 0)),
                  pl.BlockSpec((s, w), lambda p, j: (0, p))] + s_in,
        out_specs=[pl.BlockSpec((2, s, w), lambda p, j: (p, 0, 0)),
                   pl.BlockSpec((2, t, w), lambda p, j: (p, j, 0)),
                   pl.BlockSpec((2, t, w), lambda p, j: (p, j, 0))] + s_out,
        out_shape=[jax.ShapeDtypeStruct((nh, s, w), F32)] * 3 + s_shape,
        scratch_shapes=s_sems,
        compiler_params=pltpu.CompilerParams(dimension_semantics=("arbitrary", "arbitrary"),
                                             vmem_limit_bytes=VMEM_LIMIT),
    )(q, k, v, o, lse, do, *s_ops)
    return res[0], res[1], res[2], res[3:]


def _scan(src, w_re, w_im, a_re, a_im, reverse):
    s = src.shape[0]
    nb, w = a_re.shape[0], LANES
    ch = s // 8
    assert ch & (ch - 1) == 0
    grp = 4
    tr = min(512, s)

    def cmul(ar, ai, xr, xi):
        return ar * xr - ai * xi, ar * xi + ai * xr

    def body(src_ref, wr_ref, wi_ref, ar_ref, ai_ref, xr_ref, xi_ref):
        def project(c, carry):
            rows = pl.ds(pl.multiple_of(c * tr, tr), tr)
            u = src_ref[rows, :]
            if reverse:
                br, bi = _mm_nt(u, wr_ref[...]), _mm_nt(u, wi_ref[...])
            else:
                br, bi = _mm(u, wr_ref[...]), _mm(u, wi_ref[...])
            for g in range(grp):
                xr_ref[g, rows, :] = _lanes(br, g, w)
                xi_ref[g, rows, :] = _lanes(bi, g, w)
            return carry

        lax.fori_loop(0, s // tr, project, 0)
        sub = lax.broadcasted_iota(jnp.int32, (8, w), 0)

        def shift(x, k):
            if reverse:
                return jnp.where(sub < 8 - k, pltpu.roll(x, 8 - k, 0), 0.0)
            return jnp.where(sub >= k, pltpu.roll(x, k, 0), 0.0)

        ar = [jnp.broadcast_to(ar_ref[g], (8, w)) for g in range(grp)]
        ai = [jnp.broadcast_to(ai_ref[g], (8, w)) for g in range(grp)]

        def tsl(i):
            return pl.ds(pl.multiple_of(((ch - 1 - i) if reverse else i) * 8, 8), 8)

        def local(i, carry):
            out = []
            for g in range(grp):
                xr, xi = carry[2 * g], carry[2 * g + 1]
                pr, pi = cmul(ar[g], ai[g], xr, xi)
                nr = pr + xr_ref[g, tsl(i), :]
                ni = pi + xi_ref[g, tsl(i), :]
                xr_ref[g, tsl(i), :] = nr
                xi_ref[g, tsl(i), :] = ni
                out += [nr, ni]
            return tuple(out)

        fin = lax.fori_loop(0, ch, local, (jnp.zeros((8, w), F32),) * (2 * grp))

        carry_in = []
        for g in range(grp):
            pr, pi = ar[g], ai[g]
            for _ in range(ch.bit_length() - 1):
                pr, pi = cmul(pr, pi, pr, pi)
            fr, fi = fin[2 * g], fin[2 * g + 1]
            for kk in (1, 2, 4):
                sr, si = cmul(pr, pi, shift(fr, kk), shift(fi, kk))
                fr, fi = fr + sr, fi + si
                pr, pi = cmul(pr, pi, pr, pi)
            carry_in += [shift(fr, 1), shift(fi, 1)]

        def fix(i, pw):
            out = []
            for g in range(grp):
                pr, pi = pw[2 * g], pw[2 * g + 1]
                cr, ci = cmul(pr, pi, carry_in[2 * g], carry_in[2 * g + 1])
                xr_ref[g, tsl(i), :] = xr_ref[g, tsl(i), :] + cr
                xi_ref[g, tsl(i), :] = xi_ref[g, tsl(i), :] + ci
                nr, ni = cmul(pr, pi, ar[g], ai[g])
                out += [nr, ni]
            return tuple(out)

        lax.fori_loop(0, ch, fix, tuple(x for g in range(grp) for x in (ar[g], ai[g])))

    per_j = 4 // grp
    blk = pl.BlockSpec((grp, s, w), lambda i: (i, 0, 0))
    ablk = pl.BlockSpec((grp, 1, w), lambda i: (i, 0, 0))
    sblk = pl.BlockSpec((s, w), lambda i: (0, i // per_j))
    if reverse:
        wblk = pl.BlockSpec((None, grp * w, w), lambda i: (i // per_j, i % per_j, 0))
    else:
        wblk = pl.BlockSpec((None, w, grp * w), lambda i: (i // per_j, 0, i % per_j))
    return _pcall(
        body, name="s5_scan_rev" if reverse else "s5_scan", grid=(nb // grp,),
        in_specs=[sblk, wblk, wblk, ablk, ablk], out_specs=[blk, blk],
        out_shape=[jax.ShapeDtypeStruct((nb, s, w), F32)] * 2,
        compiler_params=pltpu.CompilerParams(dimension_semantics=("arbitrary",), vmem_limit_bytes=VMEM_LIMIT),
    )(src, w_re, w_im, a_re, a_im)


class _Exchange(NamedTuple):
    ins: list
    out_shapes: list
    sem_shapes: list
    steps: Callable


def _gather_steps(ins, outs, sems):
    n = len(ins)
    send_sems, recv_sems, local_sems = sems
    x, y, c = lax.axis_index("x"), lax.axis_index("y"), lax.axis_index("c")
    me, sibling = (x, y, c), (x, y, 1 - c)
    chips = [(1 - x, y), (x, 1 - y), (1 - x, 1 - y)]

    def copy(a, k, block, to, src=None):
        dst = outs[a].at[4 * block[0] + 2 * block[1] + block[2]]
        return pltpu.make_async_remote_copy(
            src_ref=dst if src is None else src, dst_ref=dst,
            send_sem=send_sems.at[a, k], recv_sem=recv_sems.at[a, k], device_id=to, device_id_type=MESH)

    mine = [pltpu.make_async_copy(ins[a], outs[a].at[4 * x + 2 * y + c], local_sems.at[a]) for a in range(n)]
    first = []
    for a in range(n):
        first.append(copy(a, 0, me, sibling, src=ins[a]))
        first += [copy(a, 1 + j, me, (*chip, c), src=ins[a]) for j, chip in enumerate(chips)]
    passed = [copy(a, 4 + j, (*chip, c), sibling) for j, chip in enumerate(chips) for a in range(n)]

    def start():
        for cp in mine + first:
            cp.start()

    def pass_on():
        i = 0
        for j, chip in enumerate(chips):
            for a in range(n):
                copy(a, 1 + j, (*chip, c), me).wait_recv()
                passed[i].start()
                i += 1

    def finish():
        for a in range(n):
            copy(a, 0, sibling, me).wait_recv()
            for j, chip in enumerate(chips):
                copy(a, 4 + j, (*chip, 1 - c), me).wait_recv()
        for cp in first + passed:
            cp.wait_send()
        for cp in mine:
            cp.wait()

    return start, pass_on, finish


def _gather(arrs):
    n = len(arrs)
    return _Exchange(list(arrs), [jax.ShapeDtypeStruct((N_DEV,) + a.shape, a.dtype) for a in arrs],
                     [pltpu.SemaphoreType.DMA((n, 7)), pltpu.SemaphoreType.DMA((n, 7)), pltpu.SemaphoreType.DMA((n,))],
                     _gather_steps)


def _scatter_steps(ins, outs, sems):
    n = len(ins)
    send_sems, recv_sems, local_sems = sems
    x, y, c = lax.axis_index("x"), lax.axis_index("y"), lax.axis_index("c")
    me = 4 * x + 2 * y + c
    own, sent, arrivals = [], [], []
    for a in range(n):
        own.append(pltpu.make_async_copy(ins[a].at[me], outs[a].at[me], local_sems.at[a]))
        for k in range(1, N_DEV):
            px, py, pc = x ^ ((k >> 2) & 1), y ^ ((k >> 1) & 1), c ^ (k & 1)
            peer = 4 * px + 2 * py + pc
            sent.append(pltpu.make_async_remote_copy(
                src_ref=ins[a].at[peer], dst_ref=outs[a].at[me],
                send_sem=send_sems.at[a, k - 1], recv_sem=recv_sems.at[a, k - 1],
                device_id=(px, py, pc), device_id_type=MESH))
            arrivals.append(pltpu.make_async_remote_copy(
                src_ref=ins[a].at[me], dst_ref=outs[a].at[peer],
                send_sem=send_sems.at[a, k - 1], recv_sem=recv_sems.at[a, k - 1],
                device_id=(x, y, c), device_id_type=MESH))

    def start():
        for cp in own + sent:
            cp.start()

    def pass_on():
        pass

    def finish():
        for cp in arrivals:
            cp.wait_recv()
        for cp in sent:
            cp.wait_send()
        for cp in own:
            cp.wait()

    return start, pass_on, finish


def _scatter(grads):
    n = len(grads)
    return _Exchange(list(grads), [jax.ShapeDtypeStruct(g.shape, g.dtype) for g in grads],
                     [pltpu.SemaphoreType.DMA((n, N_DEV - 1)), pltpu.SemaphoreType.DMA((n, N_DEV - 1)),
                      pltpu.SemaphoreType.DMA((n,))], _scatter_steps)


def _together(a, b):
    def steps(ins, outs, sems):
        sa = a.steps(ins[:len(a.ins)], outs[:len(a.out_shapes)], sems[:len(a.sem_shapes)])
        sb = b.steps(ins[len(a.ins):], outs[len(a.out_shapes):], sems[len(a.sem_shapes):])

        def both(k):
            def run():
                sa[k]()
                sb[k]()
            return run
        return both(0), both(1), both(2)

    return _Exchange(a.ins + b.ins, a.out_shapes + b.out_shapes, a.sem_shapes + b.sem_shapes, steps)


def _run_exchange(name, ex):
    n_in, n_out = len(ex.ins), len(ex.out_shapes)

    def body(*refs):
        for step in ex.steps(refs[:n_in], refs[n_in:n_in + n_out], refs[n_in + n_out:]):
            step()

    any_spec = pl.BlockSpec(memory_space=pl.ANY)
    return _pcall(body, name=name, in_specs=[any_spec] * n_in, out_specs=[any_spec] * n_out,
                  out_shape=list(ex.out_shapes), scratch_shapes=list(ex.sem_shapes))(*ex.ins)


def _adam_math(g, w_, m_, v_):
    m_new = ADAM_B1 * m_ + (1.0 - ADAM_B1) * g
    v_new = ADAM_B2 * v_ + (1.0 - ADAM_B2) * (g * g)
    m_hat = m_new / (1.0 - ADAM_B1 ** ADAM_STEP)
    v_hat = v_new / (1.0 - ADAM_B2 ** ADAM_STEP)
    delta = -ADAM_LR * (m_hat / (jnp.sqrt(v_hat) + ADAM_EPS) + ADAM_WD * w_)
    return delta, m_new, v_new


def _adamw_weight(name, parts, w, m, v):
    nl = len(parts)

    def body(*refs):
        p_refs = refs[:nl]
        w_ref, m_ref, v_ref, g_ref, d_ref, mo_ref, vo_ref = refs[nl:]
        for l in range(nl):
            g = p_refs[l][0].astype(F32)
            for j in range(1, N_DEV):
                g = g + p_refs[l][j].astype(F32)
            g_ref[l] = g
            d_ref[l], mo_ref[l], vo_ref[l] = _adam_math(g, w_ref[l], m_ref[l], v_ref[l])

    return _pcall(
        body, name=name, out_shape=[jax.ShapeDtypeStruct(w.shape, F32)] * 4,
        compiler_params=pltpu.CompilerParams(vmem_limit_bytes=VMEM_LIMIT),
    )(*parts, w, m, v)

def _sum_sources(name, parts):
    r = parts.shape[1]

    def body(p_ref, g_ref):
        g = p_ref[0]
        for j in range(1, N_DEV):
            g = g + p_ref[j]
        g_ref[...] = g

    return _pcall(body, name=name, out_shape=jax.ShapeDtypeStruct((r, LANES), F32),
                  compiler_params=pltpu.CompilerParams(vmem_limit_bytes=VMEM_LIMIT))(parts)


def _adamw_small(name, g, w, m, v):
    n = len(g)

    def body(*refs):
        g_r, w_r, m_r, v_r = (refs[k * n:(k + 1) * n] for k in range(4))
        d_r, mo_r, vo_r = (refs[k * n:(k + 1) * n] for k in range(4, 7))
        for i in range(n):
            d_r[i][...], mo_r[i][...], vo_r[i][...] = _adam_math(g_r[i][...], w_r[i][...], m_r[i][...], v_r[i][...])

    res = _pcall(body, name=name, out_shape=[jax.ShapeDtypeStruct(a.shape, F32) for a in w] * 3,
                 compiler_params=pltpu.CompilerParams(vmem_limit_bytes=VMEM_LIMIT))(*g, *w, *m, *v)
    return res[:n], res[n:2 * n], res[2 * n:]


def _pack(arrs):
    flat = jnp.concatenate([a.reshape(-1) for a in arrs])
    flat = jnp.pad(flat, (0, (-flat.shape[0]) % (8 * LANES)))
    return flat.reshape(-1, LANES)


def _unpack(packed, shapes):
    flat = packed.reshape(-1)
    out, off = [], 0
    for shp in shapes:
        size = 1
        for d in shp:
            size *= d
        out.append(flat[off:off + size].reshape(shp))
        off += size
    return out


def _s5_params(lam_re, lam_im, log_dt, b_re, b_im, c_re, c_im):
    dt = jnp.exp(log_dt)[:, None]
    e = jnp.exp(lam_re * dt)
    ang = lam_im * dt
    a_re, a_im = e * jnp.cos(ang), e * jnp.sin(ang)
    nr, ni = a_re - 1.0, a_im
    den = lam_re * lam_re + lam_im * lam_im
    cr = ((nr * lam_re + ni * lam_im) / den)[..., None]
    ci = ((ni * lam_re - nr * lam_im) / den)[..., None]
    bb_re = cr * b_re - ci * b_im
    bb_im = cr * b_im + ci * b_re
    eye = jnp.eye(8, dtype=F32)[None, :, None, :, None]

    def bblk(bb):
        t = jnp.transpose(bb.reshape(4, 8, SSM_STATE, SSM_GROUP), (0, 3, 1, 2))
        return (eye * t[:, None]).reshape(4, 8 * SSM_GROUP, 8 * SSM_STATE)

    def cblk(cc):
        t = jnp.transpose(cc.reshape(4, 8, SSM_GROUP, SSM_STATE), (0, 3, 1, 2))
        return (eye * t[:, None]).reshape(4, 8 * SSM_STATE, 8 * SSM_GROUP)

    nb = SSM_GROUPS * SSM_STATE // LANES
    return (a_re.reshape(nb, 1, LANES), a_im.reshape(nb, 1, LANES), bblk(bb_re), bblk(bb_im),
            cblk(c_re), -cblk(c_im))


def _cat_blocks(x3, j):
    return jnp.concatenate([x3[4 * j + k] for k in range(4)], axis=-1)


def _to_chunks(a):
    s, c = a.shape
    return a.reshape(8, s // 8, c).transpose(1, 0, 2).reshape(s, c)


def _from_chunks(a):
    s, c = a.shape
    return a.reshape(s // 8, 8, c).transpose(1, 0, 2).reshape(s, c)


EARLY = ['w_in', 'w_uq', 'w_ukv']

FWD_PLAN = {
    'flash': ('late', ['ssm_w_glu', 'w_out', 'w_xq', 'w_xkv', 'w_xo', 'w_gate', 'w_up', 'w_down']),
    'ffn': ('nxt', EARLY),
}
BWD_PLAN = {
    'ffn_bwd': ('nxt', EARLY),
    'xattn_bwd': ('own', ['w_down']),
    'flash_bwd': ('own', ['w_gate', 'w_up', 'w_xq', 'w_xkv', 'w_xo']),
    'mla_qkv_bwd': ('own', ['ssm_w_glu', 'w_out']),
}


def _named(names, d):
    return [d[n] for n in names]


def _layer_fwd(h, memx, tabs, wl, pl_, late=None, nxt=None):
    s = h.shape[0]
    tm = min(ROW_TILE, s)
    cos, sin_lo, sin_hi = tabs
    sv = {}
    wl = dict(wl)
    nxt_got = {}

    def fetch(host):
        who, names = FWD_PLAN.get(host, (None, []))
        src = late if who == 'late' else nxt if who == 'nxt' else None
        return _gather(_named(names, src)) if src else None

    def landed(host, got):
        who, names = FWD_PLAN.get(host, (None, []))
        if got and who == 'late':
            wl.update(_layer_weights(dict(zip(names, got))))
        elif got:
            nxt_got.update(zip(names, got))

    def f_mix_in(h_, g, w):
        xn, _ = _rms(h_, g[...])
        pr = _mm(xn, w[...])
        return pr[:, 0:512], pr[:, 512:1024]
    proj, u_nat = _rows("mix_in", f_mix_in, s, tm, [(h, 'r0'), (pl_['norm_mix_g'], 'f'), (wl['w_in'], 'f')],
                        [((s, 512), F32, 'r0')] * 2)

    def f_qkv(pr, cos_, slo, shi, gq, gkv, wq, wk, wv):
        cqn = _rms(pr[:, 0:Q_LORA], gq[...])[0].astype(MXU)
        kvn = _rms(pr[:, Q_LORA:Q_LORA + KV_LORA], gkv[...])[0].astype(MXU)
        krr = _rope(pr[:, 384:512], cos_, slo, shi)
        qs, ks, vs = [], [], []
        for hd in range(MLA_HEADS):
            qs.append(_rope(_mm(cqn, wq[hd]), cos_, slo, shi) * MLA_SCALE)
            ks.append(_mm(kvn, wk[hd]) + krr)
            vs.append(_mm(kvn, wv[hd]))
        return jnp.stack(qs), jnp.stack(ks), jnp.stack(vs)
    hshape = (MLA_HEADS, s, HEAD_W)
    (q, k, v), got = _hosted(_rows(
        "mla_qkv", f_qkv, s, tm,
        [(proj, 'r0'), (cos, 'r0'), (sin_lo, 'r0'), (sin_hi, 'r0'), (pl_['q_norm_g'], 'f'), (pl_['kv_norm_g'], 'f'),
         (wl['w_uq'], 'f'), (wl['w_k'], 'f'), (wl['w_v'], 'f')],
        [(hshape, MXU, 'r1')] * 3, fetch('mla_qkv')))
    landed('mla_qkv', got)

    a_out, lse, got = _flash_fwd(q, k, v, fetch('flash'))
    landed('flash', got)

    u_ch = _to_chunks(u_nat)

    x_re, x_im = _scan(u_ch, pl_['b_re'], pl_['b_im'], pl_['a_re'], pl_['a_im'], False)

    def f_s5_out(xr, xi, u, cre, cim, d, wglu, bglu):
        y = jnp.concatenate([_mm(_cat_blocks(xr, j), cre[j]) + _mm(_cat_blocks(xi, j), cim[j])
                             for j in range(4)], axis=-1) + d[...] * u
        z = _mm(jax.nn.gelu(y), wglu[...]) + bglu[...]
        return y, y * jax.nn.sigmoid(z)
    (y_ssm, s_out_ch), got = _hosted(_rows(
        "s5_out", f_s5_out, s, tm,
        [(x_re, 'r1'), (x_im, 'r1'), (u_ch, 'r0'), (pl_['c_re'], 'f'), (pl_['c_im'], 'f'),
         (pl_['ssm_d'], 'f'), (wl['ssm_w_glu'], 'f'), (pl_['ssm_b_glu'], 'f')],
        [((s, SSM_WIDTH), F32, 'r0')] * 2, fetch('s5_out')))
    landed('s5_out', got)
    s_out = _from_chunks(s_out_ch)

    def f_mix_out(h_, a, so, ga, gs, w):
        an = _rms(a, ga[...])[0]
        sn = _rms(so, gs[...])[0]
        return (h_ + _mm(jnp.concatenate([an, sn], axis=-1), w[...]),)
    (h1,), got = _hosted(_rows("mix_out", f_mix_out, s, tm,
                               [(h, 'r0'), (a_out, 'r0'), (s_out, 'r0'), (pl_['attn_out_g'], 'f'),
                                (pl_['ssm_out_g'], 'f'), (wl['w_out'], 'f')],
                               [((s, D_MODEL), F32, 'r0')], fetch('mix_out')))
    landed('mix_out', got)

    m_len = memx.shape[0]

    def f_memkv(mm_, g, w):
        mn = _rms(mm_, g[...])[0].astype(MXU)
        return (jnp.stack([_mm(mn, w[d]) for d in range(N_DEV)]),)
    kvm, = _rows("mem_kv", f_memkv, m_len, m_len, [(memx, 'r0'), (pl_['mem_norm_g'], 'f'), (wl['w_xkv'], 'f')],
                 [((N_DEV, m_len, X_HEAD_DIM), MXU, 'r1')])

    def f_xattn(h_, g, wq, kv_, wo):
        hn = _rms(h_, g[...])[0].astype(MXU)
        q_all = _mm(hn, wq[...]).astype(MXU)
        outs = []
        for hd in range(X_HEADS):
            p = _softmax(_mm_nt(_lanes(q_all, hd, X_HEAD_DIM), kv_[hd]) * X_SCALE)
            outs.append(_mm(p, kv_[X_HEADS + hd]).astype(MXU))
        return (h_ + _mm(jnp.concatenate(outs, axis=-1), wo[...]),)
    (h2,), got = _hosted(_rows("xattn", f_xattn, s, min(X_ROWS, s),
                               [(h1, 'r0'), (pl_['norm_x_g'], 'f'), (wl['w_xq'], 'f'), (kvm, 'f'), (wl['w_xo'], 'f')],
                               [((s, D_MODEL), F32, 'r0')], fetch('xattn')))
    landed('xattn', got)

    def f_ffn(h_, g, wg, wu, wd):
        hn = _rms(h_, g[...])[0].astype(MXU)
        y = jnp.zeros(h_.shape, F32)
        gates, ups = [], []
        for c in range(D_FF // FF_CHUNK):
            cs = pl.ds(c * FF_CHUNK, FF_CHUNK)
            gate, up = _mm(hn, wg[:, cs]), _mm(hn, wu[:, cs])
            y = y + _mm(gate * jax.nn.sigmoid(gate) * up, wd[cs, :])
            gates.append(gate)
            ups.append(up)
        return h_ + y, jnp.concatenate(gates, axis=-1), jnp.concatenate(ups, axis=-1)
    (h3, gate_f, up_f), got = _hosted(_rows(
        "ffn", f_ffn, s, min(FFN_ROWS, s),
        [(h2, 'r0'), (pl_['norm_ffn_g'], 'f'), (wl['w_gate'], 'f'), (wl['w_up'], 'f'), (wl['w_down'], 'f')],
        [((s, D_MODEL), F32, 'r0'), ((s, D_FF), MXU, 'r0'), ((s, D_FF), MXU, 'r0')], fetch('ffn')))
    landed('ffn', got)
    sv.update(h=h, proj=proj, q=q, k=k, v=v, a_out=a_out, lse=lse, x_re=x_re, x_im=x_im, y_ssm=y_ssm,
              s_out=s_out, h1=h1, kvm=kvm, h2=h2, u_ch=u_ch, gate=gate_f, up=up_f)
    return h3, sv, wl, nxt_got


def _layer_bwd(dh3, sv, memx, tabs, wl, pl_, nxt=None):
    s = dh3.shape[0]
    tm = min(ROW_TILE, s)
    cos, sin_lo, sin_hi = tabs
    gr = {}
    arrived = {}
    act_shape = (s, D_FF)

    def send(host):
        who, names = BWD_PLAN.get(host, (None, []))
        if who is None or (who == 'nxt' and not nxt):
            return None, []
        return (_scatter([nxt[n] if who == 'nxt' else _blocked(gr, n) for n in names]),
                [(who, n) for n in names])

    def f_ffn_bwd(h_, dy, gate_, up_, g, wg, wu, wd):
        hn, r = _rms(h_, g[...])
        hb = hn.astype(MXU)
        dyb = dy.astype(MXU)
        dhn = jnp.zeros(h_.shape, F32)
        acts, dgs, dus = [], [], []
        for c in range(D_FF // FF_CHUNK):
            cs = pl.ds(c * FF_CHUNK, FF_CHUNK)
            gate = _lanes(gate_, c, FF_CHUNK).astype(F32)
            up = _lanes(up_, c, FF_CHUNK).astype(F32)
            sg = jax.nn.sigmoid(gate)
            si = gate * sg
            dact = _mm_nt(dyb, wd[cs, :])
            dgate = (dact * up * (sg * (1.0 + gate * (1.0 - sg)))).astype(MXU)
            dup = (dact * si).astype(MXU)
            dhn = dhn + _mm_nt(dgate, wg[:, cs]) + _mm_nt(dup, wu[:, cs])
            acts.append((si * up).astype(MXU))
            dgs.append(dgate)
            dus.append(dup)
        dh, dg = _rms_bwd(h_, g[...], r, dhn)
        cat = lambda parts: jnp.concatenate(parts, axis=-1)
        return dy + dh, hb, cat(acts), cat(dgs), cat(dus), dg
    ex, keys = send('ffn_bwd')
    (dh2, hn_f, act, dgate, dup, gr['norm_ffn_g']), got = _hosted(_rows(
        "ffn_bwd", f_ffn_bwd, s, min(FFN_ROWS, s),
        [(sv['h2'], 'r0'), (dh3, 'r0'), (sv['gate'], 'r0'), (sv['up'], 'r0'), (pl_['norm_ffn_g'], 'f'),
         (wl['w_gate'], 'f'), (wl['w_up'], 'f'), (wl['w_down'], 'f')],
        [((s, D_MODEL), F32, 'r0'), ((s, D_MODEL), MXU, 'r0'), (act_shape, MXU, 'r0'), (act_shape, MXU, 'r0'),
         (act_shape, MXU, 'r0'), ((1, D_MODEL), F32, 'a')], ex))
    arrived.update(zip(keys, got))
    gr['w_gate'] = _mm_tn_call("dw_gate", hn_f, dgate, tn=FF_CHUNK)
    gr['w_up'] = _mm_tn_call("dw_up", hn_f, dup, tn=FF_CHUNK)
    gr['w_down'] = _mm_tn_call("dw_down", act, dh3, tk=FF_CHUNK)

    m_len = memx.shape[0]

    def f_xattn_bwd(h_, dy, g, wq, kv_, wo):
        hn, r = _rms(h_, g[...])
        hb = hn.astype(MXU)
        q_all = _mm(hb, wq[...]).astype(MXU)
        do_all = _mm_nt(dy, wo[...]).astype(MXU)
        dqs, ohs, dks, dvs = [], [], [], []
        for hd in range(X_HEADS):
            kh, vh = kv_[hd], kv_[X_HEADS + hd]
            qh, do = _lanes(q_all, hd, X_HEAD_DIM), _lanes(do_all, hd, X_HEAD_DIM)
            p = _softmax(_mm_nt(qh, kh) * X_SCALE)
            ohs.append(_mm(p, vh).astype(MXU))
            dvs.append(_mm_tn(p, do))
            dp = _mm_nt(do, vh)
            ds = p * (dp - jnp.sum(dp * p, axis=-1, keepdims=True)) * X_SCALE
            dqs.append(_mm(ds, kh).astype(MXU))
            dks.append(_mm_tn(ds, qh))
        dq_all = jnp.concatenate(dqs, axis=-1)
        dh, dg = _rms_bwd(h_, g[...], r, _mm_nt(dq_all, wq[...]))
        return dy + dh, hb, dq_all, jnp.concatenate(ohs, axis=-1), jnp.stack(dks + dvs), dg
    ex, keys = send('xattn_bwd')
    (dh1, hn_x, dq_x, oh_x, dkvm, gr['norm_x_g']), got = _hosted(_rows(
        "xattn_bwd", f_xattn_bwd, s, min(X_ROWS, s),
        [(sv['h1'], 'r0'), (dh2, 'r0'), (pl_['norm_x_g'], 'f'), (wl['w_xq'], 'f'), (sv['kvm'], 'f'),
         (wl['w_xo'], 'f')],
        [((s, D_MODEL), F32, 'r0'), ((s, D_MODEL), MXU, 'r0'), ((s, D_MODEL), MXU, 'r0'),
         ((s, D_MODEL), MXU, 'r0'), ((N_DEV, m_len, X_HEAD_DIM), F32, 'a'), ((1, D_MODEL), F32, 'a')], ex))
    arrived.update(zip(keys, got))
    gr['w_xq'] = _mm_tn_call("dw_xq", hn_x, dq_x)
    gr['w_xo'] = _mm_tn_call("dw_xo", oh_x, dh2)

    def f_memkv_bwd(mm_, dkv, g, w):
        mn, r = _rms(mm_, g[...])
        mb = mn.astype(MXU)
        dmn = jnp.zeros(mm_.shape, F32)
        dws = []
        for d in range(N_DEV):
            dmn = dmn + _mm_nt(dkv[d], w[d])
            dws.append(_mm_tn(mb, dkv[d]))
        _, dg = _rms_bwd(mm_, g[...], r, dmn)
        return jnp.stack(dws), dg
    gr['w_xkv'], gr['mem_norm_g'] = _rows(
        "mem_kv_bwd", f_memkv_bwd, m_len, m_len,
        [(memx, 'r0'), (dkvm, 'r1'), (pl_['mem_norm_g'], 'f'), (wl['w_xkv'], 'f')],
        [((N_DEV, D_MODEL, X_HEAD_DIM), F32, 'a'), ((1, D_MODEL), F32, 'a')])

    def f_mix_out_bwd(a, so, dy, ga, gs, w):
        dmix = _mm_nt(dy, w[...])
        an, ra = _rms(a, ga[...])
        sn, rs = _rms(so, gs[...])
        da, dga = _rms_bwd(a, ga[...], ra, dmix[:, 0:512])
        dso, dgs = _rms_bwd(so, gs[...], rs, dmix[:, 512:1024])
        return da, dso, jnp.concatenate([an, sn], axis=-1), dga, dgs
    da_out, ds_out, mixed, gr['attn_out_g'], gr['ssm_out_g'] = _rows(
        "mix_out_bwd", f_mix_out_bwd, s, tm,
        [(sv['a_out'], 'r0'), (sv['s_out'], 'r0'), (dh1, 'r0'), (pl_['attn_out_g'], 'f'), (pl_['ssm_out_g'], 'f'),
         (wl['w_out'], 'f')],
        [((s, 512), F32, 'r0'), ((s, 512), F32, 'r0'), ((s, D_MODEL), MXU, 'r0'), ((1, 512), F32, 'a'),
         ((1, 512), F32, 'a')])
    gr['w_out'] = _mm_tn_call("dw_out", mixed, dh1)

    ex, keys = send('flash_bwd')
    dq, dk, dv, got = _flash_bwd(sv['q'], sv['k'], sv['v'], sv['a_out'], sv['lse'], da_out, ex)
    arrived.update(zip(keys, got))

    def f_s5_out_bwd(xr, xi, u, y, ds, cre, cim, d, wglu, bglu):
        g, gelu_vjp = jax.vjp(jax.nn.gelu, y)
        sig = jax.nn.sigmoid(_mm(g, wglu[...]) + bglu[...])
        dz = ds * y * sig * (1.0 - sig)
        dy = ds * sig + gelu_vjp(_mm_nt(dz, wglu[...]))[0]
        dcr, dci = [], []
        for j in range(4):
            dyj = _lanes(dy, j, LANES)
            dcr.append(_mm_tn(_cat_blocks(xr, j), dyj))
            dci.append(_mm_tn(_cat_blocks(xi, j), dyj))
        return (dy, dy * d[...], jnp.stack(dcr), jnp.stack(dci),
                jnp.sum(dy * u, axis=0, keepdims=True), _mm_tn(g, dz), jnp.sum(dz, axis=0, keepdims=True))
    ex, keys = send('s5_out_bwd')
    (dy_ssm, du_dir, gr['c_re'], gr['c_im'], gr['ssm_d'], gr['ssm_w_glu'], gr['ssm_b_glu']), got = _hosted(_rows(
        "s5_out_bwd", f_s5_out_bwd, s, tm,
        [(sv['x_re'], 'r1'), (sv['x_im'], 'r1'), (sv['u_ch'], 'r0'), (sv['y_ssm'], 'r0'), (_to_chunks(ds_out), 'r0'),
         (pl_['c_re'], 'f'), (pl_['c_im'], 'f'), (pl_['ssm_d'], 'f'), (wl['ssm_w_glu'], 'f'),
         (pl_['ssm_b_glu'], 'f')],
        [((s, 512), F32, 'r0'), ((s, 512), F32, 'r0'), ((4, 512, LANES), F32, 'a'),
         ((4, 512, LANES), F32, 'a'), ((1, 512), F32, 'a'), ((512, 512), F32, 'a'), ((1, 512), F32, 'a')], ex))
    arrived.update(zip(keys, got))
    g_re, g_im = _scan(dy_ssm, pl_['c_re'], pl_['c_im'], pl_['a_re'], -pl_['a_im'], True)
    first_re = jnp.pad(sv['x_re'][:, s - 8:s - 1], ((0, 0), (1, 0), (0, 0)))
    first_im = jnp.pad(sv['x_im'][:, s - 8:s - 1], ((0, 0), (1, 0), (0, 0)))

    def f_s5_in_bwd(gre, gim, xr, xi, pr8, pi8, u, dud, f8r, f8i, bre, bim):
        first = pl.program_id(0) == 0
        xpr = jnp.concatenate([jnp.where(first, f8r[...], pr8), xr[:, :tm - 8]], axis=1)
        xpi = jnp.concatenate([jnp.where(first, f8i[...], pi8), xi[:, :tm - 8]], axis=1)
        dus, dbr, dbi = [], [], []
        for j in range(4):
            gj_r, gj_i, uj = _cat_blocks(gre, j), _cat_blocks(gim, j), _lanes(u, j, LANES)
            dus.append(_mm_nt(gj_r, bre[j]) + _mm_nt(gj_i, bim[j]))
            dbr.append(_mm_tn(uj, gj_r))
            dbi.append(_mm_tn(uj, gj_i))
        da_r = jnp.sum(gre * xpr + gim * xpi, axis=1, keepdims=True)
        da_i = jnp.sum(gim * xpr - gre * xpi, axis=1, keepdims=True)
        return dud + jnp.concatenate(dus, axis=-1), jnp.stack(dbr), jnp.stack(dbi), da_r, da_i
    ex, keys = send('s5_in_bwd')
    (du_ch, gr['b_re'], gr['b_im'], gr['a_re'], gr['a_im']), got = _hosted(_rows(
        "s5_in_bwd", f_s5_in_bwd, s, tm,
        [(g_re, 'r1'), (g_im, 'r1'), (sv['x_re'], 'r1'), (sv['x_im'], 'r1'), (sv['x_re'], 'p8'), (sv['x_im'], 'p8'),
         (sv['u_ch'], 'r0'), (du_dir, 'r0'), (first_re, 'f'), (first_im, 'f'), (pl_['b_re'], 'f'), (pl_['b_im'], 'f')],
        [((s, 512), F32, 'r0'), ((4, LANES, 512), F32, 'a'), ((4, LANES, 512), F32, 'a'),
         ((16, 1, LANES), F32, 'a'), ((16, 1, LANES), F32, 'a')], ex))
    arrived.update(zip(keys, got))
    du = _from_chunks(du_ch)

    def f_qkv_bwd(pr, cos_, slo, shi, dq_, dk_, dv_, gq, gkv, wq, wk, wv):
        cq, ckv = pr[:, 0:Q_LORA], pr[:, Q_LORA:Q_LORA + KV_LORA]
        cqn, rq = _rms(cq, gq[...])
        kvn, rkv = _rms(ckv, gkv[...])
        cqb, kvb = cqn.astype(MXU), kvn.astype(MXU)
        dcqn = jnp.zeros(cq.shape, F32)
        dkvn = jnp.zeros(ckv.shape, F32)
        dksum = jnp.zeros(dk_[0].shape, F32)
        dwq, dwk, dwv = [], [], []
        for hd in range(MLA_HEADS):
            dqp = (_rope_t(dq_[hd], cos_, slo, shi) * MLA_SCALE).astype(MXU)
            dkb, dvb = dk_[hd].astype(MXU), dv_[hd].astype(MXU)
            dwq.append(_mm_tn(cqb, dqp))
            dwk.append(_mm_tn(kvb, dkb))
            dwv.append(_mm_tn(kvb, dvb))
            dcqn = dcqn + _mm_nt(dqp, wq[hd])
            dkvn = dkvn + _mm_nt(dkb, wk[hd]) + _mm_nt(dvb, wv[hd])
            dksum = dksum + dk_[hd]
        dcq, dgq = _rms_bwd(cq, gq[...], rq, dcqn)
        dckv, dgkv = _rms_bwd(ckv, gkv[...], rkv, dkvn)
        dpa = jnp.concatenate([dcq, dckv, _rope_t(dksum, cos_, slo, shi)], axis=-1)
        return dpa, jnp.stack(dwq), jnp.stack(dwk), jnp.stack(dwv), dgq, dgkv
    ex, keys = send('mla_qkv_bwd')
    (dpa, gr['w_uq'], gr['w_k'], gr['w_v'], gr['q_norm_g'], gr['kv_norm_g']), got = _hosted(_rows(
        "mla_qkv_bwd", f_qkv_bwd, s, tm,
        [(sv['proj'], 'r0'), (cos, 'r0'), (sin_lo, 'r0'), (sin_hi, 'r0'), (dq, 'r1'), (dk, 'r1'), (dv, 'r1'),
         (pl_['q_norm_g'], 'f'), (pl_['kv_norm_g'], 'f'), (wl['w_uq'], 'f'), (wl['w_k'], 'f'), (wl['w_v'], 'f')],
        [((s, 512), F32, 'r0'), ((MLA_HEADS, Q_LORA, HEAD_W), F32, 'a'), ((MLA_HEADS, KV_LORA, HEAD_W), F32, 'a'),
         ((MLA_HEADS, KV_LORA, HEAD_W), F32, 'a'), ((1, Q_LORA), F32, 'a'), ((1, KV_LORA), F32, 'a')], ex))
    arrived.update(zip(keys, got))

    def f_mix_in_bwd(h_, dpa_, du_, dres, g, w):
        dproj = jnp.concatenate([dpa_, du_], axis=-1).astype(MXU)
        xn, r = _rms(h_, g[...])
        dh, dg = _rms_bwd(h_, g[...], r, _mm_nt(dproj, w[...]))
        return dres + dh, xn, dproj, dg
    ex, keys = send('mix_in_bwd')
    (dh0, xn, dproj, gr['norm_mix_g']), got = _hosted(_rows(
        "mix_in_bwd", f_mix_in_bwd, s, tm,
        [(sv['h'], 'r0'), (dpa, 'r0'), (du, 'r0'), (dh1, 'r0'), (pl_['norm_mix_g'], 'f'), (wl['w_in'], 'f')],
        [((s, D_MODEL), F32, 'r0'), ((s, D_MODEL), MXU, 'r0'), ((s, D_MODEL), MXU, 'r0'), ((1, D_MODEL), F32, 'a')],
        ex))
    arrived.update(zip(keys, got))
    gr['w_in'] = _mm_tn_call("dw_in", xn, dproj)
    return dh0, gr, arrived


def _layer_weights(w):
    wl = {}
    if 'w_in' in w:
        w_in = w['w_in'].reshape(D_MODEL, -1)
        z = lambda n: jnp.zeros((D_MODEL, n), w_in.dtype)
        wl['w_in'] = jnp.concatenate([w_in[:, :384], z(64), w_in[:, 384:416], z(32), w_in[:, 416:]], axis=1)
    if 'w_uq' in w:
        wl['w_uq'] = jnp.pad(w['w_uq'], ((0, 0), (0, 0), (0, HEAD_W - QK_NOPE - QK_ROPE)))
    if 'w_ukv' in w:
        wl['w_k'] = jnp.pad(w['w_ukv'][..., :QK_NOPE], ((0, 0), (0, 0), (0, HEAD_W - QK_NOPE)))
        wv = w['w_ukv'][..., QK_NOPE:]
        even = (jnp.arange(MLA_HEADS) % 2 == 0)[:, None, None]
        wl['w_v'] = jnp.concatenate([jnp.where(even, wv, 0), jnp.where(even, 0, wv)], axis=-1).astype(wv.dtype)
    if 'ssm_w_glu' in w:
        wl['ssm_w_glu'] = w['ssm_w_glu'].reshape(SSM_WIDTH, SSM_WIDTH)
    for n in ('w_out', 'w_xq', 'w_xo'):
        if n in w:
            wl[n] = w[n].reshape(D_MODEL, D_MODEL)
    if 'w_xkv' in w:
        wl['w_xkv'] = w['w_xkv']
    for n in ('w_gate', 'w_up'):
        if n in w:
            wl[n] = jnp.transpose(w[n], (1, 0, 2)).reshape(D_MODEL, D_FF)
    if 'w_down' in w:
        wl['w_down'] = w['w_down'].reshape(D_FF, D_MODEL)
    return wl


def _blocked(gr, n):
    if n == 'w_in':
        d = gr['w_in']
        out = jnp.concatenate([d[:, :384], d[:, 448:480], d[:, 512:]], axis=1).reshape(N_DEV, 128, -1)
    elif n == 'w_uq':
        out = gr['w_uq'][..., :QK_NOPE + QK_ROPE]
    elif n == 'w_ukv':
        even = (jnp.arange(MLA_HEADS) % 2 == 0)[:, None, None]
        dv = gr['w_v']
        out = jnp.concatenate([gr['w_k'][..., :QK_NOPE], jnp.where(even, dv[..., :V_DIM], dv[..., V_DIM:])], axis=-1)
    elif n == 'ssm_w_glu':
        out = gr['ssm_w_glu'].reshape(N_DEV, SSM_WIDTH // N_DEV, SSM_WIDTH)
    elif n in ('w_out', 'w_xq', 'w_xo'):
        out = gr[n].reshape(N_DEV, D_MODEL // N_DEV, D_MODEL)
    elif n in ('w_gate', 'w_up'):
        out = jnp.transpose(gr[n].reshape(D_MODEL, N_DEV, D_FF // N_DEV), (1, 0, 2))
    elif n == 'w_down':
        out = gr[n].reshape(N_DEV, D_FF // N_DEV, D_MODEL)
    else:
        out = gr[n]
    return out.astype(MXU)


def kernel(x, mem, positions, norm_mix_g, w_in, q_norm_g, w_uq, kv_norm_g, w_ukv, ssm_lambda_re, ssm_lambda_im, ssm_log_dt, ssm_b_re, ssm_b_im, ssm_c_re, ssm_c_im, ssm_d, ssm_w_glu, ssm_b_glu, attn_out_g, ssm_out_g, w_out, norm_x_g, mem_norm_g, w_xq, w_xkv, w_xo, norm_ffn_g, w_gate, w_up, w_down, final_norm_g, loss_target, m_norm_mix_g, m_w_in, m_q_norm_g, m_w_uq, m_kv_norm_g, m_w_ukv, m_ssm_lambda_re, m_ssm_lambda_im, m_ssm_log_dt, m_ssm_b_re, m_ssm_b_im, m_ssm_c_re, m_ssm_c_im, m_ssm_d, m_ssm_w_glu, m_ssm_b_glu, m_attn_out_g, m_ssm_out_g, m_w_out, m_norm_x_g, m_mem_norm_g, m_w_xq, m_w_xkv, m_w_xo, m_norm_ffn_g, m_w_gate, m_w_up, m_w_down, m_final_norm_g, v_norm_mix_g, v_w_in, v_q_norm_g, v_w_uq, v_kv_norm_g, v_w_ukv, v_ssm_lambda_re, v_ssm_lambda_im, v_ssm_log_dt, v_ssm_b_re, v_ssm_b_im, v_ssm_c_re, v_ssm_c_im, v_ssm_d, v_ssm_w_glu, v_ssm_b_glu, v_attn_out_g, v_ssm_out_g, v_w_out, v_norm_x_g, v_mem_norm_g, v_w_xq, v_w_xkv, v_w_xo, v_norm_ffn_g, v_w_gate, v_w_up, v_w_down, v_final_norm_g):
    args = dict(locals())
    W = {n: args[n] for n in WEIGHTS}
    M = {n: args['m_' + n] for n in WEIGHTS}
    V = {n: args['v_' + n] for n in WEIGHTS}
    s = x.shape[1]
    h = x[0]
    memx = mem[0]

    freqs = ROPE_THETA ** (-jnp.arange(0, QK_ROPE, 2, dtype=F32) / QK_ROPE)
    ang = positions[0].astype(F32)[:, None] * freqs
    c16, s16 = jnp.cos(ang), jnp.sin(ang)
    zeros = lambda n: jnp.zeros((s, n), F32)
    cos = jnp.concatenate([jnp.ones((s, QK_NOPE), F32), c16, c16, zeros(32)], axis=1)
    sin_lo = jnp.concatenate([zeros(QK_NOPE), -s16, zeros(ROT + 32)], axis=1)
    sin_hi = jnp.concatenate([zeros(QK_NOPE + ROT), s16, zeros(32)], axis=1)
    tabs = (cos, sin_lo, sin_hi)

    shards = [{n: W[n][l].astype(MXU) for n in SHARDED} for l in range(DEPTH)]
    gathered = dict(zip(EARLY, _run_exchange("gather_weights", _gather(_named(EARLY, shards[0])))))

    layers = []
    for l in range(DEPTH):
        wl = _layer_weights(gathered)
        s5_in = [W[n][l] for n in ('ssm_lambda_re', 'ssm_lambda_im', 'ssm_log_dt', 'ssm_b_re', 'ssm_b_im',
                                   'ssm_c_re', 'ssm_c_im')]
        (a_re, a_im, bre, bim, cre, cim), s5_vjp = jax.vjp(_s5_params, *s5_in)
        pl_ = {n: W[n][l][None] for n in ('norm_mix_g', 'q_norm_g', 'kv_norm_g', 'ssm_d', 'ssm_b_glu',
                                           'attn_out_g', 'ssm_out_g', 'norm_x_g', 'mem_norm_g', 'norm_ffn_g')}
        pl_.update(a_re=a_re, a_im=a_im, b_re=bre, b_im=bim, c_re=cre, c_im=cim)
        h, sv, wl, gathered = _layer_fwd(h, memx, tabs, wl, pl_, shards[l], shards[l + 1] if l + 1 < DEPTH else None)
        layers.append((wl, pl_, sv, s5_vjp))

    def f_loss(h_, tgt, g):
        y, r = _rms(h_, g[...])
        err = y - tgt
        part = 0.5 * jnp.sum(jnp.mean(err * err, axis=-1, keepdims=True), axis=0, keepdims=True)
        dh, dg = _rms_bwd(h_, g[...], r, err / D_MODEL)
        return dh, dg, jnp.broadcast_to(part, (8, LANES))
    dh, g_final, loss_part = _rows(
        "loss_head", f_loss, s, min(ROW_TILE, s), [(h, 'r0'), (loss_target[0], 'r0'), (final_norm_g[None], 'f')],
        [((s, D_MODEL), F32, 'r0'), ((1, D_MODEL), F32, 'a'), ((8, LANES), F32, 'a')])
    loss = lax.psum(loss_part[0, 0], ("x", "y", "c"))

    parts = [{} for _ in range(DEPTH)]
    g_rep = [None] * DEPTH
    blocks = None
    for l in reversed(range(DEPTH)):
        wl, pl_, sv, s5_vjp = layers[l]
        dh, gr, arrived = _layer_bwd(dh, sv, memx, tabs, wl, pl_, blocks)
        for (who, n), p in arrived.items():
            parts[l + 1 if who == 'nxt' else l][n] = p
        blocks = {n: _blocked(gr, n) for n in EARLY}
        ds5 = s5_vjp((gr['a_re'], gr['a_im'], gr['b_re'], gr['b_im'], gr['c_re'], gr['c_im']))
        rep = dict(zip(('ssm_lambda_re', 'ssm_lambda_im', 'ssm_log_dt', 'ssm_b_re', 'ssm_b_im', 'ssm_c_re',
                        'ssm_c_im'), ds5))
        for n in ('norm_mix_g', 'q_norm_g', 'kv_norm_g', 'ssm_d', 'ssm_b_glu', 'attn_out_g', 'ssm_out_g',
                  'norm_x_g', 'mem_norm_g', 'norm_ffn_g'):
            rep[n] = gr[n][0]
        g_rep[l] = rep
    grad_x = dh[None]

    rep_names = REPL_L + ['final_norm_g']
    g_loc = {n: jnp.stack([g_rep[l][n] for l in range(DEPTH)]) for n in REPL_L}
    g_loc['final_norm_g'] = g_final
    rest = [n for n in SHARDED if n not in parts[0]]
    last = _run_exchange("last_grads", _together(_scatter(_named(rest, blocks)),
                                                 _gather([_pack(_named(rep_names, g_loc))])))
    parts[0].update(zip(rest, last[:len(rest)]))

    out_sh = [{}, {}, {}, {}]
    for n in SHARDED:
        res = _adamw_weight("adamw_" + n, [parts[l][n] for l in range(DEPTH)], W[n], M[n], V[n])
        for kind, r in enumerate(res):
            out_sh[kind][n] = r

    shapes_rp = [(1,) + W[n].shape if W[n].ndim == 1 else W[n].shape for n in rep_names]
    g_rp = _unpack(_sum_sources("sum_small_grads", last[len(rest)]), shapes_rp)
    as_rows = lambda d: [d[n].reshape(shp) for n, shp in zip(rep_names, shapes_rp)]
    res_rp = (g_rp,) + _adamw_small("adamw_replicated", g_rp, as_rows(W), as_rows(M), as_rows(V))
    out_rp = [{n: a.reshape(W[n].shape) for n, a in zip(rep_names, r)} for r in res_rp]

    outs = [loss, grad_x]
    for kind in range(4):
        for n in WEIGHTS:
            outs.append(out_sh[kind][n] if n in SHARDED else out_rp[kind][n])
    return tuple(outs)
```

```python
from typing import Callable, NamedTuple

import jax
import jax.numpy as jnp
from jax import lax
from jax.experimental import pallas as pl
from jax.experimental.pallas import tpu as pltpu

F32 = jnp.float32
MXU = jnp.bfloat16

D_MODEL = 1024
MLA_HEADS = 8
QK_NOPE = 64
QK_ROPE = 32
V_DIM = 64
Q_LORA = 256
KV_LORA = 128
SSM_WIDTH = 512
SSM_GROUPS = 32
SSM_GROUP = 16
SSM_STATE = 64
X_HEADS = 4
X_HEAD_DIM = 256
D_FF = 2816
FF_CHUNK = D_FF // 2
FF_FWD_CHUNK = D_FF
ROPE_THETA = 10000.0
EPS = 1e-6
DEPTH = 2
N_DEV = 8
LANES = 128
HEAD_W = 128
MLA_SCALE = (QK_NOPE + QK_ROPE) ** -0.5
X_SCALE = X_HEAD_DIM ** -0.5
ADAM_LR, ADAM_B1, ADAM_B2, ADAM_EPS, ADAM_WD, ADAM_STEP = 0.001, 0.9, 0.999, 1e-08, 0.01, 10
VMEM_LIMIT = 56 * 1024 * 1024
FLASH_TILE = 512
DW_ROWS = 2048
X_ROWS = 512
ROW_TILE = 512
FFN_ROWS = 256
MESH = pl.DeviceIdType.MESH

SHARDED = ['w_in', 'w_uq', 'w_ukv', 'ssm_w_glu', 'w_out', 'w_xq', 'w_xkv', 'w_xo', 'w_gate', 'w_up', 'w_down']
REPL_L = ['norm_mix_g', 'q_norm_g', 'kv_norm_g', 'ssm_lambda_re', 'ssm_lambda_im', 'ssm_log_dt', 'ssm_b_re',
          'ssm_b_im', 'ssm_c_re', 'ssm_c_im', 'ssm_d', 'ssm_b_glu', 'attn_out_g', 'ssm_out_g', 'norm_x_g',
          'mem_norm_g', 'norm_ffn_g']
WEIGHTS = ['norm_mix_g', 'w_in', 'q_norm_g', 'w_uq', 'kv_norm_g', 'w_ukv', 'ssm_lambda_re', 'ssm_lambda_im',
           'ssm_log_dt', 'ssm_b_re', 'ssm_b_im', 'ssm_c_re', 'ssm_c_im', 'ssm_d', 'ssm_w_glu', 'ssm_b_glu',
           'attn_out_g', 'ssm_out_g', 'w_out', 'norm_x_g', 'mem_norm_g', 'w_xq', 'w_xkv', 'w_xo', 'norm_ffn_g',
           'w_gate', 'w_up', 'w_down', 'final_norm_g']


def _pcall(body, **kw):
    return pl.pallas_call(body, **kw)


def _mm(a, b):
    return jnp.dot(a.astype(MXU), b.astype(MXU), preferred_element_type=F32)


def _mm_nt(a, b):
    return lax.dot_general(a.astype(MXU), b.astype(MXU), (((1,), (1,)), ((), ())), preferred_element_type=F32)


def _mm_tn(a, b):
    return lax.dot_general(a.astype(MXU), b.astype(MXU), (((0,), (0,)), ((), ())), preferred_element_type=F32)


def _rms(x, g):
    r = lax.rsqrt(jnp.mean(x * x, axis=-1, keepdims=True) + EPS)
    return x * r * g, r


def _rms_bwd(x, g, r, dy):
    dyg = dy * g
    dx = r * dyg - x * (r * r * r) * jnp.mean(dyg * x, axis=-1, keepdims=True)
    return dx, jnp.sum(dy * x * r, axis=0, keepdims=True)


ROT = QK_ROPE // 2


def _rope(x, cos, sin_lo, sin_hi):
    return x * cos + pltpu.roll(x, HEAD_W - ROT, 1) * sin_lo + pltpu.roll(x, ROT, 1) * sin_hi


def _rope_t(g, cos, sin_lo, sin_hi):
    return g * cos + pltpu.roll(g * sin_lo, ROT, 1) + pltpu.roll(g * sin_hi, HEAD_W - ROT, 1)


def _softmax(s):
    m = jnp.max(s, axis=-1, keepdims=True)
    e = jnp.exp(s - m)
    return e / jnp.sum(e, axis=-1, keepdims=True)


def _lanes(x, j, w):
    return x[:, j * w:(j + 1) * w]


def _rows(name, fn, n, tm, ins, outs, side=None):
    def spec(shape, kind):
        nd = len(shape)
        if kind == 'p8':
            return pl.BlockSpec((shape[0], 8, shape[2]), lambda i: (0, jnp.maximum(i * (tm // 8) - 1, 0), 0))
        if kind == 'f':
            return pl.BlockSpec(shape, lambda i, _nd=nd: (0,) * _nd, pipeline_mode=pl.Buffered(1))
        if kind == 'a':
            return pl.BlockSpec(shape, lambda i, _nd=nd: (0,) * _nd)
        ax = int(kind[1])
        blk = tuple(tm if d == ax else s for d, s in enumerate(shape))
        return pl.BlockSpec(blk, lambda i, _ax=ax, _nd=nd: tuple(i if d == _ax else 0 for d in range(_nd)))

    n_in, n_out, n_steps = len(ins), len(outs), n // tm

    def body(*refs):
        in_refs, out_refs, steps = _side_split(refs, n_in, n_out, side)
        i = pl.program_id(0)
        if steps:
            pl.when(i == 0)(steps[0])
            pl.when(i == _pass_on_step(n_steps))(steps[1])
        args = [r if k == 'f' else r[...] for r, (_, k) in zip(in_refs, ins)]
        res = fn(*args)
        for r, (_, dt, k), v in zip(out_refs, outs, res):
            if k == 'a':
                _accumulate(r, v.astype(dt), i)
            else:
                r[...] = v.astype(dt)
        if steps:
            pl.when(i == n_steps - 1)(steps[2])

    s_in, s_out, s_shape, s_sems, s_ops = _side_args(side)
    res = _pcall(
        body, name=name + ("_x" if side else ""), grid=(n_steps,),
        in_specs=[spec(a.shape, k) for a, k in ins] + s_in,
        out_specs=[spec(s, k) for s, _, k in outs] + s_out,
        out_shape=[jax.ShapeDtypeStruct(s, dt) for s, dt, _ in outs] + s_shape,
        scratch_shapes=s_sems,
        compiler_params=pltpu.CompilerParams(dimension_semantics=("arbitrary",), vmem_limit_bytes=VMEM_LIMIT),
    )(*[a for a, _ in ins], *s_ops)
    return _Hosted(res[:n_out], res[n_out:]) if side else res


def _accumulate(ref, v, i):
    @pl.when(i == 0)
    def _():
        ref[...] = v

    @pl.when(i != 0)
    def _():
        ref[...] += v


def _mm_tn_call(name, a, b, tk=None, tn=None):
    out_dtype = MXU
    s, k = a.shape
    n = b.shape[1]
    tk, tn = tk or k, tn or n
    ts = min(DW_ROWS, s)
    ns = s // ts

    def body(a_ref, b_ref, o_ref, acc_ref):
        j = pl.program_id(2)
        _accumulate(acc_ref, _mm_tn(a_ref[...], b_ref[...]), j)

        @pl.when(j == ns - 1)
        def _():
            o_ref[...] = acc_ref[...].astype(out_dtype)

    return _pcall(
        body, name=name, grid=(k // tk, n // tn, ns),
        in_specs=[pl.BlockSpec((ts, tk), lambda ik, jn, j: (j, ik)),
                  pl.BlockSpec((ts, tn), lambda ik, jn, j: (j, jn))],
        out_specs=pl.BlockSpec((tk, tn), lambda ik, jn, j: (ik, jn)),
        out_shape=jax.ShapeDtypeStruct((k, n), out_dtype),
        scratch_shapes=[pltpu.VMEM((tk, tn), F32)],
        compiler_params=pltpu.CompilerParams(dimension_semantics=("arbitrary", "arbitrary", "arbitrary"),
                                             vmem_limit_bytes=VMEM_LIMIT),
    )(a, b)


def _side_split(refs, n_in, n_out, side):
    if side is None:
        return refs[:n_in], refs[n_in:n_in + n_out], None
    si, so = len(side.ins), len(side.out_shapes)
    own_in, side_in = refs[:n_in], refs[n_in:n_in + si]
    own_out, side_out = refs[n_in + si:n_in + si + n_out], refs[n_in + si + n_out:n_in + si + n_out + so]
    return own_in, own_out, side.steps(side_in, side_out, refs[n_in + si + n_out + so:])


def _pass_on_step(n_steps):
    return max(n_steps - 2, 0)


def _side_args(side):
    if side is None:
        return [], [], [], [], []
    any_spec = pl.BlockSpec(memory_space=pl.ANY)
    return ([any_spec] * len(side.ins), [any_spec] * len(side.out_shapes), list(side.out_shapes),
            list(side.sem_shapes), list(side.ins))


class _Hosted(NamedTuple):
    results: list
    arrived: list


def _hosted(res):
    return res if isinstance(res, _Hosted) else _Hosted(res, ())


def _flash_fwd(q, k, v, side=None):
    nh, s, w = q.shape
    t = min(FLASH_TILE, s)
    nq = s // t
    n_steps = (nh // 2) * nq

    def body(*refs):
        (q_ref, k_ref, v_ref), (o_ref, lse_ref), steps = _side_split(refs, 3, 2, side)
        step = pl.program_id(0) * nq + pl.program_id(1)
        if steps:
            pl.when(step == 0)(steps[0])
            pl.when(step == _pass_on_step(n_steps))(steps[1])
        qi = pl.program_id(1)
        qs = [q_ref[0], q_ref[1]]
        below = lax.broadcasted_iota(jnp.int32, (t, t), 1) <= lax.broadcasted_iota(jnp.int32, (t, t), 0)

        def tile(j, carry, diagonal):
            sl = pl.ds(pl.multiple_of(j * t, t), t)
            out = []
            for hh in range(2):
                m, l, acc = carry[3 * hh:3 * hh + 3]
                sc = _mm_nt(qs[hh], k_ref[hh, sl, :])
                if diagonal:
                    sc = jnp.where(below, sc, -1e30)
                m_new = jnp.maximum(m, jnp.max(sc, axis=1, keepdims=True))
                p = jnp.exp(sc - m_new)
                alpha = jnp.exp(m - m_new)
                out += [m_new, alpha * l + jnp.sum(p, axis=1, keepdims=True), alpha * acc + _mm(p, v_ref[hh, sl, :])]
            return tuple(out)

        init = (jnp.full((t, 1), -1e30, F32), jnp.zeros((t, 1), F32), jnp.zeros((t, w), F32)) * 2
        carry = lax.fori_loop(0, qi, lambda j, c: tile(j, c, False), init)
        carry = tile(qi, carry, True)
        o_ref[...] = carry[2] / carry[1] + carry[5] / carry[4]
        for hh in range(2):
            lse_ref[hh] = jnp.broadcast_to(carry[3 * hh] + jnp.log(carry[3 * hh + 1]), (t, w))
        if steps:
            pl.when(step == n_steps - 1)(steps[2])

    s_in, s_out, s_shape, s_sems, s_ops = _side_args(side)
    res = _pcall(
        body, name="mla_flash_fwd" + ("_x" if side else ""), grid=(nh // 2, nq),
        in_specs=[pl.BlockSpec((2, t, w), lambda p, i: (p, i, 0)),
                  pl.BlockSpec((2, s, w), lambda p, i: (p, 0, 0)),
                  pl.BlockSpec((2, s, w), lambda p, i: (p, 0, 0))] + s_in,
        out_specs=[pl.BlockSpec((t, w), lambda p, i: (i, p)),
                   pl.BlockSpec((2, t, w), lambda p, i: (p, i, 0))] + s_out,
        out_shape=[jax.ShapeDtypeStruct((s, (nh // 2) * w), F32), jax.ShapeDtypeStruct((nh, s, w), F32)] + s_shape,
        scratch_shapes=s_sems,
        compiler_params=pltpu.CompilerParams(dimension_semantics=("arbitrary", "arbitrary"),
                                             vmem_limit_bytes=VMEM_LIMIT),
    )(q, k, v, *s_ops)
    return res[0], res[1], res[2:]


def _flash_bwd(q, k, v, o, lse, do, side=None):
    nh, s, w = q.shape
    t = min(FLASH_TILE, s)
    nq = s // t
    n_steps = (nh // 2) * nq

    def body(*refs):
        (q_ref, k_ref, v_ref, o_ref, lse_ref, do_ref), (dq_ref, dk_ref, dv_ref), steps = _side_split(refs, 6, 3, side)
        step = pl.program_id(0) * nq + pl.program_id(1)
        if steps:
            pl.when(step == 0)(steps[0])
            pl.when(step == _pass_on_step(n_steps))(steps[1])
        j = pl.program_id(1)

        @pl.when(j == 0)
        def _():
            dq_ref[...] = jnp.zeros(dq_ref.shape, F32)

        below = lax.broadcasted_iota(jnp.int32, (t, t), 1) <= lax.broadcasted_iota(jnp.int32, (t, t), 0)
        lane = lax.broadcasted_iota(jnp.int32, (t, w), 1)
        heads = [jnp.logical_and(lane >= hh * V_DIM, lane < (hh + 1) * V_DIM) for hh in range(2)]
        ks = [k_ref[0], k_ref[1]]
        vs = [v_ref[0], v_ref[1]]

        def tile(i, carry, diagonal):
            sl = pl.ds(pl.multiple_of(i * t, t), t)
            dout_all, o_all = do_ref[sl, :], o_ref[sl, :]
            out = []
            for hh in range(2):
                dk, dv = carry[2 * hh], carry[2 * hh + 1]
                qh = q_ref[hh, sl, :]
                dout = jnp.where(heads[hh], dout_all, 0.0)
                sc = _mm_nt(qh, ks[hh])
                if diagonal:
                    sc = jnp.where(below, sc, -1e30)
                p = jnp.exp(sc - lse_ref[hh, sl, 0:1])
                dp = _mm_nt(dout, vs[hh])
                ds = p * (dp - jnp.sum(dout * o_all, axis=1, keepdims=True))
                dq_ref[hh, sl, :] += _mm(ds, ks[hh])
                out += [dk + _mm_tn(ds, qh), dv + _mm_tn(p, dout)]
            return tuple(out)

        carry = tile(j, (jnp.zeros((t, w), F32),) * 4, True)
        carry = lax.fori_loop(j + 1, nq, lambda i, c: tile(i, c, False), carry)
        for hh in range(2):
            dk_ref[hh] = carry[2 * hh]
            dv_ref[hh] = jnp.where(heads[hh], carry[2 * hh + 1], 0.0)
        if steps:
            pl.when(step == n_steps - 1)(steps[2])

    s_in, s_out, s_shape, s_sems, s_ops = _side_args(side)
    res = _pcall(
        body, name="mla_flash_bwd" + ("_x" if side else ""), grid=(nh // 2, nq),
        in_specs=[pl.BlockSpec((2, s, w), lambda p, j: (p, 0, 0)),
                  pl.BlockSpec((2, t, w), lambda p, j: (p, j, 0)),
                  pl.BlockSpec((2, t, w), lambda p, j: (p, j, 0)),
                  pl.BlockSpec((s, w), lambda p, j: (0, p)),
                  pl.BlockSpec((2, s, w), lambda p, j: (p, 0, 0)),
                  pl.BlockSpec((s, w), lambda p, j: (0, p))] + s_in,
        out_specs=[pl.BlockSpec((2, s, w), lambda p, j: (p, 0, 0)),
                   pl.BlockSpec((2, t, w), lambda p, j: (p, j, 0)),
                   pl.BlockSpec((2, t, w), lambda p, j: (p, j, 0))] + s_out,
        out_shape=[jax.ShapeDtypeStruct((nh, s, w), F32)] * 3 + s_shape,
        scratch_shapes=s_sems,
        compiler_params=pltpu.CompilerParams(dimension_semantics=("arbitrary", "arbitrary"),
                                             vmem_limit_bytes=VMEM_LIMIT),
    )(q, k, v, o, lse, do, *s_ops)
    return res[0], res[1], res[2], res[3:]


def _scan(src, w_re, w_im, a_re, a_im, reverse):
    s = src.shape[0]
    nb, w = a_re.shape[0], LANES
    ch = s // 8
    assert ch & (ch - 1) == 0
    grp = 4
    tr = min(512, s)

    def cmul(ar, ai, xr, xi):
        return ar * xr - ai * xi, ar * xi + ai * xr

    def body(src_ref, wr_ref, wi_ref, ar_ref, ai_ref, xr_ref, xi_ref):
        def project(c, carry):
            rows = pl.ds(pl.multiple_of(c * tr, tr), tr)
            u = src_ref[rows, :]
            if reverse:
                br, bi = _mm_nt(u, wr_ref[...]), _mm_nt(u, wi_ref[...])
            else:
                br, bi = _mm(u, wr_ref[...]), _mm(u, wi_ref[...])
            for g in range(grp):
                xr_ref[g, rows, :] = _lanes(br, g, w)
                xi_ref[g, rows, :] = _lanes(bi, g, w)
            return carry

        lax.fori_loop(0, s // tr, project, 0)
        sub = lax.broadcasted_iota(jnp.int32, (8, w), 0)

        def shift(x, k):
            if reverse:
                return jnp.where(sub < 8 - k, pltpu.roll(x, 8 - k, 0), 0.0)
            return jnp.where(sub >= k, pltpu.roll(x, k, 0), 0.0)

        ar = [jnp.broadcast_to(ar_ref[g], (8, w)) for g in range(grp)]
        ai = [jnp.broadcast_to(ai_ref[g], (8, w)) for g in range(grp)]

        def tsl(i):
            return pl.ds(pl.multiple_of(((ch - 1 - i) if reverse else i) * 8, 8), 8)

        def local(i, carry):
            out = []
            for g in range(grp):
                xr, xi = carry[2 * g], carry[2 * g + 1]
                pr, pi = cmul(ar[g], ai[g], xr, xi)
                nr = pr + xr_ref[g, tsl(i), :]
                ni = pi + xi_ref[g, tsl(i), :]
                xr_ref[g, tsl(i), :] = nr
                xi_ref[g, tsl(i), :] = ni
                out += [nr, ni]
            return tuple(out)

        fin = lax.fori_loop(0, ch, local, (jnp.zeros((8, w), F32),) * (2 * grp))

        carry_in = []
        for g in range(grp):
            pr, pi = ar[g], ai[g]
            for _ in range(ch.bit_length() - 1):
                pr, pi = cmul(pr, pi, pr, pi)
            fr, fi = fin[2 * g], fin[2 * g + 1]
            for kk in (1, 2, 4):
                sr, si = cmul(pr, pi, shift(fr, kk), shift(fi, kk))
                fr, fi = fr + sr, fi + si
                pr, pi = cmul(pr, pi, pr, pi)
            carry_in += [shift(fr, 1), shift(fi, 1)]

        def fix(i, pw):
            out = []
            for g in range(grp):
                pr, pi = pw[2 * g], pw[2 * g + 1]
                cr, ci = cmul(pr, pi, carry_in[2 * g], carry_in[2 * g + 1])
                xr_ref[g, tsl(i), :] = xr_ref[g, tsl(i), :] + cr
                xi_ref[g, tsl(i), :] = xi_ref[g, tsl(i), :] + ci
                nr, ni = cmul(pr, pi, ar[g], ai[g])
                out += [nr, ni]
            return tuple(out)

        lax.fori_loop(0, ch, fix, tuple(x for g in range(grp) for x in (ar[g], ai[g])))

    per_j = 4 // grp
    blk = pl.BlockSpec((grp, s, w), lambda i: (i, 0, 0))
    ablk = pl.BlockSpec((grp, 1, w), lambda i: (i, 0, 0))
    sblk = pl.BlockSpec((s, w), lambda i: (0, i // per_j))
    if reverse:
        wblk = pl.BlockSpec((None, grp * w, w), lambda i: (i // per_j, i % per_j, 0))
    else:
        wblk = pl.BlockSpec((None, w, grp * w), lambda i: (i // per_j, 0, i % per_j))
    return _pcall(
        body, name="s5_scan_rev" if reverse else "s5_scan", grid=(nb // grp,),
        in_specs=[sblk, wblk, wblk, ablk, ablk], out_specs=[blk, blk],
        out_shape=[jax.ShapeDtypeStruct((nb, s, w), F32)] * 2,
        compiler_params=pltpu.CompilerParams(dimension_semantics=("arbitrary",), vmem_limit_bytes=VMEM_LIMIT),
    )(src, w_re, w_im, a_re, a_im)


class _Exchange(NamedTuple):
    ins: list
    out_shapes: list
    sem_shapes: list
    steps: Callable


def _gather_steps(ins, outs, sems):
    n = len(ins)
    send_sems, recv_sems, local_sems = sems
    x, y, c = lax.axis_index("x"), lax.axis_index("y"), lax.axis_index("c")
    me, sibling = (x, y, c), (x, y, 1 - c)
    chips = [(1 - x, y), (x, 1 - y), (1 - x, 1 - y)]

    def copy(a, k, block, to, src=None):
        dst = outs[a].at[4 * block[0] + 2 * block[1] + block[2]]
        return pltpu.make_async_remote_copy(
            src_ref=dst if src is None else src, dst_ref=dst,
            send_sem=send_sems.at[a, k], recv_sem=recv_sems.at[a, k], device_id=to, device_id_type=MESH)

    mine = [pltpu.make_async_copy(ins[a], outs[a].at[4 * x + 2 * y + c], local_sems.at[a]) for a in range(n)]
    first = []
    for a in range(n):
        first.append(copy(a, 0, me, sibling, src=ins[a]))
        first += [copy(a, 1 + j, me, (*chip, c), src=ins[a]) for j, chip in enumerate(chips)]
    passed = [copy(a, 4 + j, (*chip, c), sibling) for j, chip in enumerate(chips) for a in range(n)]

    def start():
        for cp in mine + first:
            cp.start()

    def pass_on():
        i = 0
        for j, chip in enumerate(chips):
            for a in range(n):
                copy(a, 1 + j, (*chip, c), me).wait_recv()
                passed[i].start()
                i += 1

    def finish():
        for a in range(n):
            copy(a, 0, sibling, me).wait_recv()
            for j, chip in enumerate(chips):
                copy(a, 4 + j, (*chip, 1 - c), me).wait_recv()
        for cp in first + passed:
            cp.wait_send()
        for cp in mine:
            cp.wait()

    return start, pass_on, finish


def _gather(arrs):
    n = len(arrs)
    return _Exchange(list(arrs), [jax.ShapeDtypeStruct((N_DEV,) + a.shape, a.dtype) for a in arrs],
                     [pltpu.SemaphoreType.DMA((n, 7)), pltpu.SemaphoreType.DMA((n, 7)), pltpu.SemaphoreType.DMA((n,))],
                     _gather_steps)


def _scatter_steps(ins, outs, sems):
    n = len(ins)
    send_sems, recv_sems, local_sems = sems
    x, y, c = lax.axis_index("x"), lax.axis_index("y"), lax.axis_index("c")
    me = 4 * x + 2 * y + c
    own, sent, arrivals = [], [], []
    for a in range(n):
        own.append(pltpu.make_async_copy(ins[a].at[me], outs[a].at[me], local_sems.at[a]))
        for k in range(1, N_DEV):
            px, py, pc = x ^ ((k >> 2) & 1), y ^ ((k >> 1) & 1), c ^ (k & 1)
            peer = 4 * px + 2 * py + pc
            sent.append(pltpu.make_async_remote_copy(
                src_ref=ins[a].at[peer], dst_ref=outs[a].at[me],
                send_sem=send_sems.at[a, k - 1], recv_sem=recv_sems.at[a, k - 1],
                device_id=(px, py, pc), device_id_type=MESH))
            arrivals.append(pltpu.make_async_remote_copy(
                src_ref=ins[a].at[me], dst_ref=outs[a].at[peer],
                send_sem=send_sems.at[a, k - 1], recv_sem=recv_sems.at[a, k - 1],
                device_id=(x, y, c), device_id_type=MESH))

    def start():
        for cp in own + sent:
            cp.start()

    def pass_on():
        pass

    def finish():
        for cp in arrivals:
            cp.wait_recv()
        for cp in sent:
            cp.wait_send()
        for cp in own:
            cp.wait()

    return start, pass_on, finish


def _scatter(grads):
    n = len(grads)
    return _Exchange(list(grads), [jax.ShapeDtypeStruct(g.shape, g.dtype) for g in grads],
                     [pltpu.SemaphoreType.DMA((n, N_DEV - 1)), pltpu.SemaphoreType.DMA((n, N_DEV - 1)),
                      pltpu.SemaphoreType.DMA((n,))], _scatter_steps)


def _together(a, b):
    def steps(ins, outs, sems):
        sa = a.steps(ins[:len(a.ins)], outs[:len(a.out_shapes)], sems[:len(a.sem_shapes)])
        sb = b.steps(ins[len(a.ins):], outs[len(a.out_shapes):], sems[len(a.sem_shapes):])

        def both(k):
            def run():
                sa[k]()
                sb[k]()
            return run
        return both(0), both(1), both(2)

    return _Exchange(a.ins + b.ins, a.out_shapes + b.out_shapes, a.sem_shapes + b.sem_shapes, steps)


def _run_exchange(name, ex):
    n_in, n_out = len(ex.ins), len(ex.out_shapes)

    def body(*refs):
        for step in ex.steps(refs[:n_in], refs[n_in:n_in + n_out], refs[n_in + n_out:]):
            step()

    any_spec = pl.BlockSpec(memory_space=pl.ANY)
    return _pcall(body, name=name, in_specs=[any_spec] * n_in, out_specs=[any_spec] * n_out,
                  out_shape=list(ex.out_shapes), scratch_shapes=list(ex.sem_shapes))(*ex.ins)


def _adam_math(g, w_, m_, v_):
    m_new = ADAM_B1 * m_ + (1.0 - ADAM_B1) * g
    v_new = ADAM_B2 * v_ + (1.0 - ADAM_B2) * (g * g)
    m_hat = m_new / (1.0 - ADAM_B1 ** ADAM_STEP)
    v_hat = v_new / (1.0 - ADAM_B2 ** ADAM_STEP)
    delta = -ADAM_LR * (m_hat / (jnp.sqrt(v_hat) + ADAM_EPS) + ADAM_WD * w_)
    return delta, m_new, v_new


def _adamw_weight(name, parts, w, m, v):
    nl = len(parts)

    def body(*refs):
        p_refs = refs[:nl]
        w_ref, m_ref, v_ref, g_ref, d_ref, mo_ref, vo_ref = refs[nl:]
        for l in range(nl):
            g = p_refs[l][0].astype(F32)
            for j in range(1, N_DEV):
                g = g + p_refs[l][j].astype(F32)
            g_ref[l] = g
            d_ref[l], mo_ref[l], vo_ref[l] = _adam_math(g, w_ref[l], m_ref[l], v_ref[l])

    return _pcall(
        body, name=name, out_shape=[jax.ShapeDtypeStruct(w.shape, F32)] * 4,
        compiler_params=pltpu.CompilerParams(vmem_limit_bytes=VMEM_LIMIT),
    )(*parts, w, m, v)

def _sum_sources(name, parts):
    r = parts.shape[1]

    def body(p_ref, g_ref):
        g = p_ref[0]
        for j in range(1, N_DEV):
            g = g + p_ref[j]
        g_ref[...] = g

    return _pcall(body, name=name, out_shape=jax.ShapeDtypeStruct((r, LANES), F32),
                  compiler_params=pltpu.CompilerParams(vmem_limit_bytes=VMEM_LIMIT))(parts)


def _adamw_small(name, g, w, m, v):
    n = len(g)

    def body(*refs):
        g_r, w_r, m_r, v_r = (refs[k * n:(k + 1) * n] for k in range(4))
        d_r, mo_r, vo_r = (refs[k * n:(k + 1) * n] for k in range(4, 7))
        for i in range(n):
            d_r[i][...], mo_r[i][...], vo_r[i][...] = _adam_math(g_r[i][...], w_r[i][...], m_r[i][...], v_r[i][...])

    res = _pcall(body, name=name, out_shape=[jax.ShapeDtypeStruct(a.shape, F32) for a in w] * 3,
                 compiler_params=pltpu.CompilerParams(vmem_limit_bytes=VMEM_LIMIT))(*g, *w, *m, *v)
    return res[:n], res[n:2 * n], res[2 * n:]


def _pack(arrs):
    flat = jnp.concatenate([a.reshape(-1) for a in arrs])
    flat = jnp.pad(flat, (0, (-flat.shape[0]) % (8 * LANES)))
    return flat.reshape(-1, LANES)


def _unpack(packed, shapes):
    flat = packed.reshape(-1)
    out, off = [], 0
    for shp in shapes:
        size = 1
        for d in shp:
            size *= d
        out.append(flat[off:off + size].reshape(shp))
        off += size
    return out


def _s5_params(lam_re, lam_im, log_dt, b_re, b_im, c_re, c_im):
    dt = jnp.exp(log_dt)[:, None]
    e = jnp.exp(lam_re * dt)
    ang = lam_im * dt
    a_re, a_im = e * jnp.cos(ang), e * jnp.sin(ang)
    nr, ni = a_re - 1.0, a_im
    den = lam_re * lam_re + lam_im * lam_im
    cr = ((nr * lam_re + ni * lam_im) / den)[..., None]
    ci = ((ni * lam_re - nr * lam_im) / den)[..., None]
    bb_re = cr * b_re - ci * b_im
    bb_im = cr * b_im + ci * b_re
    eye = jnp.eye(8, dtype=F32)[None, :, None, :, None]

    def bblk(bb):
        t = jnp.transpose(bb.reshape(4, 8, SSM_STATE, SSM_GROUP), (0, 3, 1, 2))
        return (eye * t[:, None]).reshape(4, 8 * SSM_GROUP, 8 * SSM_STATE)

    def cblk(cc):
        t = jnp.transpose(cc.reshape(4, 8, SSM_GROUP, SSM_STATE), (0, 3, 1, 2))
        return (eye * t[:, None]).reshape(4, 8 * SSM_STATE, 8 * SSM_GROUP)

    nb = SSM_GROUPS * SSM_STATE // LANES
    return (a_re.reshape(nb, 1, LANES), a_im.reshape(nb, 1, LANES), bblk(bb_re), bblk(bb_im),
            cblk(c_re), -cblk(c_im))


def _cat_blocks(x3, j):
    return jnp.concatenate([x3[4 * j + k] for k in range(4)], axis=-1)


def _to_chunks(a):
    s, c = a.shape
    return a.reshape(8, s // 8, c).transpose(1, 0, 2).reshape(s, c)


def _from_chunks(a):
    s, c = a.shape
    return a.reshape(s // 8, 8, c).transpose(1, 0, 2).reshape(s, c)


EARLY = ['w_in', 'w_uq', 'w_ukv']

FWD_PLAN = {
    'flash': ('late', ['ssm_w_glu', 'w_out', 'w_xq', 'w_xkv', 'w_xo', 'w_gate', 'w_up', 'w_down']),
    'ffn': ('nxt', EARLY),
}
BWD_PLAN = {
    'ffn_bwd': ('nxt', EARLY),
    'xattn_bwd': ('own', ['w_down']),
    'flash_bwd': ('own', ['w_gate', 'w_up', 'w_xq', 'w_xkv', 'w_xo']),
    'mla_qkv_bwd': ('own', ['ssm_w_glu', 'w_out']),
}


def _named(names, d):
    return [d[n] for n in names]


def _layer_fwd(h, memx, tabs, wl, pl_, late=None, nxt=None):
    s = h.shape[0]
    tm = min(ROW_TILE, s)
    cos, sin_lo, sin_hi = tabs
    sv = {}
    wl = dict(wl)
    nxt_got = {}

    def fetch(host):
        who, names = FWD_PLAN.get(host, (None, []))
        src = late if who == 'late' else nxt if who == 'nxt' else None
        return _gather(_named(names, src)) if src else None

    def landed(host, got):
        who, names = FWD_PLAN.get(host, (None, []))
        if got and who == 'late':
            wl.update(_layer_weights(dict(zip(names, got))))
        elif got:
            nxt_got.update(zip(names, got))

    def f_mix_in(h_, g, w):
        xn, _ = _rms(h_, g[...])
        pr = _mm(xn, w[...])
        return pr[:, 0:512], pr[:, 512:1024]
    proj, u_nat = _rows("mix_in", f_mix_in, s, tm, [(h, 'r0'), (pl_['norm_mix_g'], 'f'), (wl['w_in'], 'f')],
                        [((s, 512), F32, 'r0')] * 2)

    def f_qkv(pr, cos_, slo, shi, gq, gkv, wq, wk, wv):
        cqn = _rms(pr[:, 0:Q_LORA], gq[...])[0].astype(MXU)
        kvn = _rms(pr[:, Q_LORA:Q_LORA + KV_LORA], gkv[...])[0].astype(MXU)
        krr = _rope(pr[:, 384:512], cos_, slo, shi)
        qs, ks, vs = [], [], []
        for hd in range(MLA_HEADS):
            qs.append(_rope(_mm(cqn, wq[hd]), cos_, slo, shi) * MLA_SCALE)
            ks.append(_mm(kvn, wk[hd]) + krr)
            vs.append(_mm(kvn, wv[hd]))
        return jnp.stack(qs), jnp.stack(ks), jnp.stack(vs)
    hshape = (MLA_HEADS, s, HEAD_W)
    (q, k, v), got = _hosted(_rows(
        "mla_qkv", f_qkv, s, tm,
        [(proj, 'r0'), (cos, 'r0'), (sin_lo, 'r0'), (sin_hi, 'r0'), (pl_['q_norm_g'], 'f'), (pl_['kv_norm_g'], 'f'),
         (wl['w_uq'], 'f'), (wl['w_k'], 'f'), (wl['w_v'], 'f')],
        [(hshape, MXU, 'r1')] * 3, fetch('mla_qkv')))
    landed('mla_qkv', got)

    a_out, lse, got = _flash_fwd(q, k, v, fetch('flash'))
    landed('flash', got)

    u_ch = _to_chunks(u_nat)

    x_re, x_im = _scan(u_ch, pl_['b_re'], pl_['b_im'], pl_['a_re'], pl_['a_im'], False)

    def f_s5_out(xr, xi, u, cre, cim, d, wglu, bglu):
        y = jnp.concatenate([_mm(_cat_blocks(xr, j), cre[j]) + _mm(_cat_blocks(xi, j), cim[j])
                             for j in range(4)], axis=-1) + d[...] * u
        z = _mm(jax.nn.gelu(y), wglu[...]) + bglu[...]
        return y, y * jax.nn.sigmoid(z)
    (y_ssm, s_out_ch), got = _hosted(_rows(
        "s5_out", f_s5_out, s, tm,
        [(x_re, 'r1'), (x_im, 'r1'), (u_ch, 'r0'), (pl_['c_re'], 'f'), (pl_['c_im'], 'f'),
         (pl_['ssm_d'], 'f'), (wl['ssm_w_glu'], 'f'), (pl_['ssm_b_glu'], 'f')],
        [((s, SSM_WIDTH), F32, 'r0')] * 2, fetch('s5_out')))
    landed('s5_out', got)
    s_out = _from_chunks(s_out_ch)

    def f_mix_out(h_, a, so, ga, gs, w):
        an = _rms(a, ga[...])[0]
        sn = _rms(so, gs[...])[0]
        return (h_ + _mm(jnp.concatenate([an, sn], axis=-1), w[...]),)
    (h1,), got = _hosted(_rows("mix_out", f_mix_out, s, tm,
                               [(h, 'r0'), (a_out, 'r0'), (s_out, 'r0'), (pl_['attn_out_g'], 'f'),
                                (pl_['ssm_out_g'], 'f'), (wl['w_out'], 'f')],
                               [((s, D_MODEL), F32, 'r0')], fetch('mix_out')))
    landed('mix_out', got)

    m_len = memx.shape[0]

    def f_memkv(mm_, g, w):
        mn = _rms(mm_, g[...])[0].astype(MXU)
        return (jnp.stack([_mm(mn, w[d]) for d in range(N_DEV)]),)
    kvm, = _rows("mem_kv", f_memkv, m_len, m_len, [(memx, 'r0'), (pl_['mem_norm_g'], 'f'), (wl['w_xkv'], 'f')],
                 [((N_DEV, m_len, X_HEAD_DIM), MXU, 'r1')])

    def f_xattn(h_, g, wq, kv_, wo):
        hn = _rms(h_, g[...])[0].astype(MXU)
        q_all = _mm(hn, wq[...]).astype(MXU)
        outs = []
        for hd in range(X_HEADS):
            p = _softmax(_mm_nt(_lanes(q_all, hd, X_HEAD_DIM), kv_[hd]) * X_SCALE)
            outs.append(_mm(p, kv_[X_HEADS + hd]).astype(MXU))
        return (h_ + _mm(jnp.concatenate(outs, axis=-1), wo[...]),)
    (h2,), got = _hosted(_rows("xattn", f_xattn, s, min(X_ROWS, s),
                               [(h1, 'r0'), (pl_['norm_x_g'], 'f'), (wl['w_xq'], 'f'), (kvm, 'f'), (wl['w_xo'], 'f')],
                               [((s, D_MODEL), F32, 'r0')], fetch('xattn')))
    landed('xattn', got)

    def f_ffn(h_, g, wg, wu, wd):
        hn = _rms(h_, g[...])[0].astype(MXU)
        y = jnp.zeros(h_.shape, F32)
        gates, ups = [], []
        for c in range(D_FF // FF_FWD_CHUNK):
            cs = pl.ds(c * FF_FWD_CHUNK, FF_FWD_CHUNK)
            gate, up = _mm(hn, wg[:, cs]), _mm(hn, wu[:, cs])
            y = y + _mm(gate * jax.nn.sigmoid(gate) * up, wd[cs, :])
            gates.append(gate)
            ups.append(up)
        return h_ + y, jnp.concatenate(gates, axis=-1), jnp.concatenate(ups, axis=-1)
    (h3, gate_f, up_f), got = _hosted(_rows(
        "ffn", f_ffn, s, min(FFN_ROWS, s),
        [(h2, 'r0'), (pl_['norm_ffn_g'], 'f'), (wl['w_gate'], 'f'), (wl['w_up'], 'f'), (wl['w_down'], 'f')],
        [((s, D_MODEL), F32, 'r0'), ((s, D_FF), MXU, 'r0'), ((s, D_FF), MXU, 'r0')], fetch('ffn')))
    landed('ffn', got)
    sv.update(h=h, proj=proj, q=q, k=k, v=v, a_out=a_out, lse=lse, x_re=x_re, x_im=x_im, y_ssm=y_ssm,
              s_out=s_out, h1=h1, kvm=kvm, h2=h2, u_ch=u_ch, gate=gate_f, up=up_f)
    return h3, sv, wl, nxt_got


def _layer_bwd(dh3, sv, memx, tabs, wl, pl_, nxt=None):
    s = dh3.shape[0]
    tm = min(ROW_TILE, s)
    cos, sin_lo, sin_hi = tabs
    gr = {}
    arrived = {}
    act_shape = (s, D_FF)

    def send(host):
        who, names = BWD_PLAN.get(host, (None, []))
        if who is None or (who == 'nxt' and not nxt):
            return None, []
        return (_scatter([nxt[n] if who == 'nxt' else _blocked(gr, n) for n in names]),
                [(who, n) for n in names])

    def f_ffn_bwd(h_, dy, gate_, up_, g, wg, wu, wd):
        hn, r = _rms(h_, g[...])
        hb = hn.astype(MXU)
        dyb = dy.astype(MXU)
        dhn = jnp.zeros(h_.shape, F32)
        acts, dgs, dus = [], [], []
        for c in range(D_FF // FF_CHUNK):
            cs = pl.ds(c * FF_CHUNK, FF_CHUNK)
            gate = _lanes(gate_, c, FF_CHUNK).astype(F32)
            up = _lanes(up_, c, FF_CHUNK).astype(F32)
            sg = jax.nn.sigmoid(gate)
            si = gate * sg
            dact = _mm_nt(dyb, wd[cs, :])
            dgate = (dact * up * (sg * (1.0 + gate * (1.0 - sg)))).astype(MXU)
            dup = (dact * si).astype(MXU)
            dhn = dhn + _mm_nt(dgate, wg[:, cs]) + _mm_nt(dup, wu[:, cs])
            acts.append((si * up).astype(MXU))
            dgs.append(dgate)
            dus.append(dup)
        dh, dg = _rms_bwd(h_, g[...], r, dhn)
        cat = lambda parts: jnp.concatenate(parts, axis=-1)
        return dy + dh, hb, cat(acts), cat(dgs), cat(dus), dg
    ex, keys = send('ffn_bwd')
    (dh2, hn_f, act, dgate, dup, gr['norm_ffn_g']), got = _hosted(_rows(
        "ffn_bwd", f_ffn_bwd, s, min(FFN_ROWS, s),
        [(sv['h2'], 'r0'), (dh3, 'r0'), (sv['gate'], 'r0'), (sv['up'], 'r0'), (pl_['norm_ffn_g'], 'f'),
         (wl['w_gate'], 'f'), (wl['w_up'], 'f'), (wl['w_down'], 'f')],
        [((s, D_MODEL), F32, 'r0'), ((s, D_MODEL), MXU, 'r0'), (act_shape, MXU, 'r0'), (act_shape, MXU, 'r0'),
         (act_shape, MXU, 'r0'), ((1, D_MODEL), F32, 'a')], ex))
    arrived.update(zip(keys, got))
    gr['w_gate'] = _mm_tn_call("dw_gate", hn_f, dgate, tn=FF_CHUNK)
    gr['w_up'] = _mm_tn_call("dw_up", hn_f, dup, tn=FF_CHUNK)
    gr['w_down'] = _mm_tn_call("dw_down", act, dh3, tk=FF_CHUNK)

    m_len = memx.shape[0]

    def f_xattn_bwd(h_, dy, g, wq, kv_, wo):
        hn, r = _rms(h_, g[...])
        hb = hn.astype(MXU)
        q_all = _mm(hb, wq[...]).astype(MXU)
        do_all = _mm_nt(dy, wo[...]).astype(MXU)
        dqs, ohs, dks, dvs = [], [], [], []
        for hd in range(X_HEADS):
            kh, vh = kv_[hd], kv_[X_HEADS + hd]
            qh, do = _lanes(q_all, hd, X_HEAD_DIM), _lanes(do_all, hd, X_HEAD_DIM)
            p = _softmax(_mm_nt(qh, kh) * X_SCALE)
            ohs.append(_mm(p, vh).astype(MXU))
            dvs.append(_mm_tn(p, do))
            dp = _mm_nt(do, vh)
            ds = p * (dp - jnp.sum(dp * p, axis=-1, keepdims=True)) * X_SCALE
            dqs.append(_mm(ds, kh).astype(MXU))
            dks.append(_mm_tn(ds, qh))
        dq_all = jnp.concatenate(dqs, axis=-1)
        dh, dg = _rms_bwd(h_, g[...], r, _mm_nt(dq_all, wq[...]))
        return dy + dh, hb, dq_all, jnp.concatenate(ohs, axis=-1), jnp.stack(dks + dvs), dg
    ex, keys = send('xattn_bwd')
    (dh1, hn_x, dq_x, oh_x, dkvm, gr['norm_x_g']), got = _hosted(_rows(
        "xattn_bwd", f_xattn_bwd, s, min(X_ROWS, s),
        [(sv['h1'], 'r0'), (dh2, 'r0'), (pl_['norm_x_g'], 'f'), (wl['w_xq'], 'f'), (sv['kvm'], 'f'),
         (wl['w_xo'], 'f')],
        [((s, D_MODEL), F32, 'r0'), ((s, D_MODEL), MXU, 'r0'), ((s, D_MODEL), MXU, 'r0'),
         ((s, D_MODEL), MXU, 'r0'), ((N_DEV, m_len, X_HEAD_DIM), F32, 'a'), ((1, D_MODEL), F32, 'a')], ex))
    arrived.update(zip(keys, got))
    gr['w_xq'] = _mm_tn_call("dw_xq", hn_x, dq_x)
    gr['w_xo'] = _mm_tn_call("dw_xo", oh_x, dh2)

    def f_memkv_bwd(mm_, dkv, g, w):
        mn, r = _rms(mm_, g[...])
        mb = mn.astype(MXU)
        dmn = jnp.zeros(mm_.shape, F32)
        dws = []
        for d in range(N_DEV):
            dmn = dmn + _mm_nt(dkv[d], w[d])
            dws.append(_mm_tn(mb, dkv[d]))
        _, dg = _rms_bwd(mm_, g[...], r, dmn)
        return jnp.stack(dws), dg
    gr['w_xkv'], gr['mem_norm_g'] = _rows(
        "mem_kv_bwd", f_memkv_bwd, m_len, m_len,
        [(memx, 'r0'), (dkvm, 'r1'), (pl_['mem_norm_g'], 'f'), (wl['w_xkv'], 'f')],
        [((N_DEV, D_MODEL, X_HEAD_DIM), F32, 'a'), ((1, D_MODEL), F32, 'a')])

    def f_mix_out_bwd(a, so, dy, ga, gs, w):
        dmix = _mm_nt(dy, w[...])
        an, ra = _rms(a, ga[...])
        sn, rs = _rms(so, gs[...])
        da, dga = _rms_bwd(a, ga[...], ra, dmix[:, 0:512])
        dso, dgs = _rms_bwd(so, gs[...], rs, dmix[:, 512:1024])
        return da, dso, jnp.concatenate([an, sn], axis=-1), dga, dgs
    da_out, ds_out, mixed, gr['attn_out_g'], gr['ssm_out_g'] = _rows(
        "mix_out_bwd", f_mix_out_bwd, s, tm,
        [(sv['a_out'], 'r0'), (sv['s_out'], 'r0'), (dh1, 'r0'), (pl_['attn_out_g'], 'f'), (pl_['ssm_out_g'], 'f'),
         (wl['w_out'], 'f')],
        [((s, 512), F32, 'r0'), ((s, 512), F32, 'r0'), ((s, D_MODEL), MXU, 'r0'), ((1, 512), F32, 'a'),
         ((1, 512), F32, 'a')])
    gr['w_out'] = _mm_tn_call("dw_out", mixed, dh1)

    ex, keys = send('flash_bwd')
    dq, dk, dv, got = _flash_bwd(sv['q'], sv['k'], sv['v'], sv['a_out'], sv['lse'], da_out, ex)
    arrived.update(zip(keys, got))

    def f_s5_out_bwd(xr, xi, u, y, ds, cre, cim, d, wglu, bglu):
        g, gelu_vjp = jax.vjp(jax.nn.gelu, y)
        sig = jax.nn.sigmoid(_mm(g, wglu[...]) + bglu[...])
        dz = ds * y * sig * (1.0 - sig)
        dy = ds * sig + gelu_vjp(_mm_nt(dz, wglu[...]))[0]
        dcr, dci = [], []
        for j in range(4):
            dyj = _lanes(dy, j, LANES)
            dcr.append(_mm_tn(_cat_blocks(xr, j), dyj))
            dci.append(_mm_tn(_cat_blocks(xi, j), dyj))
        return (dy, dy * d[...], jnp.stack(dcr), jnp.stack(dci),
                jnp.sum(dy * u, axis=0, keepdims=True), _mm_tn(g, dz), jnp.sum(dz, axis=0, keepdims=True))
    ex, keys = send('s5_out_bwd')
    (dy_ssm, du_dir, gr['c_re'], gr['c_im'], gr['ssm_d'], gr['ssm_w_glu'], gr['ssm_b_glu']), got = _hosted(_rows(
        "s5_out_bwd", f_s5_out_bwd, s, tm,
        [(sv['x_re'], 'r1'), (sv['x_im'], 'r1'), (sv['u_ch'], 'r0'), (sv['y_ssm'], 'r0'), (_to_chunks(ds_out), 'r0'),
         (pl_['c_re'], 'f'), (pl_['c_im'], 'f'), (pl_['ssm_d'], 'f'), (wl['ssm_w_glu'], 'f'),
         (pl_['ssm_b_glu'], 'f')],
        [((s, 512), F32, 'r0'), ((s, 512), F32, 'r0'), ((4, 512, LANES), F32, 'a'),
         ((4, 512, LANES), F32, 'a'), ((1, 512), F32, 'a'), ((512, 512), F32, 'a'), ((1, 512), F32, 'a')], ex))
    arrived.update(zip(keys, got))
    g_re, g_im = _scan(dy_ssm, pl_['c_re'], pl_['c_im'], pl_['a_re'], -pl_['a_im'], True)
    first_re = jnp.pad(sv['x_re'][:, s - 8:s - 1], ((0, 0), (1, 0), (0, 0)))
    first_im = jnp.pad(sv['x_im'][:, s - 8:s - 1], ((0, 0), (1, 0), (0, 0)))

    def f_s5_in_bwd(gre, gim, xr, xi, pr8, pi8, u, dud, f8r, f8i, bre, bim):
        first = pl.program_id(0) == 0
        xpr = jnp.concatenate([jnp.where(first, f8r[...], pr8), xr[:, :tm - 8]], axis=1)
        xpi = jnp.concatenate([jnp.where(first, f8i[...], pi8), xi[:, :tm - 8]], axis=1)
        dus, dbr, dbi = [], [], []
        for j in range(4):
            gj_r, gj_i, uj = _cat_blocks(gre, j), _cat_blocks(gim, j), _lanes(u, j, LANES)
            dus.append(_mm_nt(gj_r, bre[j]) + _mm_nt(gj_i, bim[j]))
            dbr.append(_mm_tn(uj, gj_r))
            dbi.append(_mm_tn(uj, gj_i))
        da_r = jnp.sum(gre * xpr + gim * xpi, axis=1, keepdims=True)
        da_i = jnp.sum(gim * xpr - gre * xpi, axis=1, keepdims=True)
        return dud + jnp.concatenate(dus, axis=-1), jnp.stack(dbr), jnp.stack(dbi), da_r, da_i
    ex, keys = send('s5_in_bwd')
    (du_ch, gr['b_re'], gr['b_im'], gr['a_re'], gr['a_im']), got = _hosted(_rows(
        "s5_in_bwd", f_s5_in_bwd, s, tm,
        [(g_re, 'r1'), (g_im, 'r1'), (sv['x_re'], 'r1'), (sv['x_im'], 'r1'), (sv['x_re'], 'p8'), (sv['x_im'], 'p8'),
         (sv['u_ch'], 'r0'), (du_dir, 'r0'), (first_re, 'f'), (first_im, 'f'), (pl_['b_re'], 'f'), (pl_['b_im'], 'f')],
        [((s, 512), F32, 'r0'), ((4, LANES, 512), F32, 'a'), ((4, LANES, 512), F32, 'a'),
         ((16, 1, LANES), F32, 'a'), ((16, 1, LANES), F32, 'a')], ex))
    arrived.update(zip(keys, got))
    du = _from_chunks(du_ch)

    def f_qkv_bwd(pr, cos_, slo, shi, dq_, dk_, dv_, gq, gkv, wq, wk, wv):
        cq, ckv = pr[:, 0:Q_LORA], pr[:, Q_LORA:Q_LORA + KV_LORA]
        cqn, rq = _rms(cq, gq[...])
        kvn, rkv = _rms(ckv, gkv[...])
        cqb, kvb = cqn.astype(MXU), kvn.astype(MXU)
        dcqn = jnp.zeros(cq.shape, F32)
        dkvn = jnp.zeros(ckv.shape, F32)
        dksum = jnp.zeros(dk_[0].shape, F32)
        dwq, dwk, dwv = [], [], []
        for hd in range(MLA_HEADS):
            dqp = (_rope_t(dq_[hd], cos_, slo, shi) * MLA_SCALE).astype(MXU)
            dkb, dvb = dk_[hd].astype(MXU), dv_[hd].astype(MXU)
            dwq.append(_mm_tn(cqb, dqp))
            dwk.append(_mm_tn(kvb, dkb))
            dwv.append(_mm_tn(kvb, dvb))
            dcqn = dcqn + _mm_nt(dqp, wq[hd])
            dkvn = dkvn + _mm_nt(dkb, wk[hd]) + _mm_nt(dvb, wv[hd])
            dksum = dksum + dk_[hd]
        dcq, dgq = _rms_bwd(cq, gq[...], rq, dcqn)
        dckv, dgkv = _rms_bwd(ckv, gkv[...], rkv, dkvn)
        dpa = jnp.concatenate([dcq, dckv, _rope_t(dksum, cos_, slo, shi)], axis=-1)
        return dpa, jnp.stack(dwq), jnp.stack(dwk), jnp.stack(dwv), dgq, dgkv
    ex, keys = send('mla_qkv_bwd')
    (dpa, gr['w_uq'], gr['w_k'], gr['w_v'], gr['q_norm_g'], gr['kv_norm_g']), got = _hosted(_rows(
        "mla_qkv_bwd", f_qkv_bwd, s, tm,
        [(sv['proj'], 'r0'), (cos, 'r0'), (sin_lo, 'r0'), (sin_hi, 'r0'), (dq, 'r1'), (dk, 'r1'), (dv, 'r1'),
         (pl_['q_norm_g'], 'f'), (pl_['kv_norm_g'], 'f'), (wl['w_uq'], 'f'), (wl['w_k'], 'f'), (wl['w_v'], 'f')],
        [((s, 512), F32, 'r0'), ((MLA_HEADS, Q_LORA, HEAD_W), F32, 'a'), ((MLA_HEADS, KV_LORA, HEAD_W), F32, 'a'),
         ((MLA_HEADS, KV_LORA, HEAD_W), F32, 'a'), ((1, Q_LORA), F32, 'a'), ((1, KV_LORA), F32, 'a')], ex))
    arrived.update(zip(keys, got))

    def f_mix_in_bwd(h_, dpa_, du_, dres, g, w):
        dproj = jnp.concatenate([dpa_, du_], axis=-1).astype(MXU)
        xn, r = _rms(h_, g[...])
        dh, dg = _rms_bwd(h_, g[...], r, _mm_nt(dproj, w[...]))
        return dres + dh, xn, dproj, dg
    ex, keys = send('mix_in_bwd')
    (dh0, xn, dproj, gr['norm_mix_g']), got = _hosted(_rows(
        "mix_in_bwd", f_mix_in_bwd, s, tm,
        [(sv['h'], 'r0'), (dpa, 'r0'), (du, 'r0'), (dh1, 'r0'), (pl_['norm_mix_g'], 'f'), (wl['w_in'], 'f')],
        [((s, D_MODEL), F32, 'r0'), ((s, D_MODEL), MXU, 'r0'), ((s, D_MODEL), MXU, 'r0'), ((1, D_MODEL), F32, 'a')],
        ex))
    arrived.update(zip(keys, got))
    gr['w_in'] = _mm_tn_call("dw_in", xn, dproj)
    return dh0, gr, arrived


def _layer_weights(w):
    wl = {}
    if 'w_in' in w:
        w_in = w['w_in'].reshape(D_MODEL, -1)
        z = lambda n: jnp.zeros((D_MODEL, n), w_in.dtype)
        wl['w_in'] = jnp.concatenate([w_in[:, :384], z(64), w_in[:, 384:416], z(32), w_in[:, 416:]], axis=1)
    if 'w_uq' in w:
        wl['w_uq'] = jnp.pad(w['w_uq'], ((0, 0), (0, 0), (0, HEAD_W - QK_NOPE - QK_ROPE)))
    if 'w_ukv' in w:
        wl['w_k'] = jnp.pad(w['w_ukv'][..., :QK_NOPE], ((0, 0), (0, 0), (0, HEAD_W - QK_NOPE)))
        wv = w['w_ukv'][..., QK_NOPE:]
        even = (jnp.arange(MLA_HEADS) % 2 == 0)[:, None, None]
        wl['w_v'] = jnp.concatenate([jnp.where(even, wv, 0), jnp.where(even, 0, wv)], axis=-1).astype(wv.dtype)
    if 'ssm_w_glu' in w:
        wl['ssm_w_glu'] = w['ssm_w_glu'].reshape(SSM_WIDTH, SSM_WIDTH)
    for n in ('w_out', 'w_xq', 'w_xo'):
        if n in w:
            wl[n] = w[n].reshape(D_MODEL, D_MODEL)
    if 'w_xkv' in w:
        wl['w_xkv'] = w['w_xkv']
    for n in ('w_gate', 'w_up'):
        if n in w:
            wl[n] = jnp.transpose(w[n], (1, 0, 2)).reshape(D_MODEL, D_FF)
    if 'w_down' in w:
        wl['w_down'] = w['w_down'].reshape(D_FF, D_MODEL)
    return wl


def _blocked(gr, n):
    if n == 'w_in':
        d = gr['w_in']
        out = jnp.concatenate([d[:, :384], d[:, 448:480], d[:, 512:]], axis=1).reshape(N_DEV, 128, -1)
    elif n == 'w_uq':
        out = gr['w_uq'][..., :QK_NOPE + QK_ROPE]
    elif n == 'w_ukv':
        even = (jnp.arange(MLA_HEADS) % 2 == 0)[:, None, None]
        dv = gr['w_v']
        out = jnp.concatenate([gr['w_k'][..., :QK_NOPE], jnp.where(even, dv[..., :V_DIM], dv[..., V_DIM:])], axis=-1)
    elif n == 'ssm_w_glu':
        out = gr['ssm_w_glu'].reshape(N_DEV, SSM_WIDTH // N_DEV, SSM_WIDTH)
    elif n in ('w_out', 'w_xq', 'w_xo'):
        out = gr[n].reshape(N_DEV, D_MODEL // N_DEV, D_MODEL)
    elif n in ('w_gate', 'w_up'):
        out = jnp.transpose(gr[n].reshape(D_MODEL, N_DEV, D_FF // N_DEV), (1, 0, 2))
    elif n == 'w_down':
        out = gr[n].reshape(N_DEV, D_FF // N_DEV, D_MODEL)
    else:
        out = gr[n]
    return out.astype(MXU)


def kernel(x, mem, positions, norm_mix_g, w_in, q_norm_g, w_uq, kv_norm_g, w_ukv, ssm_lambda_re, ssm_lambda_im, ssm_log_dt, ssm_b_re, ssm_b_im, ssm_c_re, ssm_c_im, ssm_d, ssm_w_glu, ssm_b_glu, attn_out_g, ssm_out_g, w_out, norm_x_g, mem_norm_g, w_xq, w_xkv, w_xo, norm_ffn_g, w_gate, w_up, w_down, final_norm_g, loss_target, m_norm_mix_g, m_w_in, m_q_norm_g, m_w_uq, m_kv_norm_g, m_w_ukv, m_ssm_lambda_re, m_ssm_lambda_im, m_ssm_log_dt, m_ssm_b_re, m_ssm_b_im, m_ssm_c_re, m_ssm_c_im, m_ssm_d, m_ssm_w_glu, m_ssm_b_glu, m_attn_out_g, m_ssm_out_g, m_w_out, m_norm_x_g, m_mem_norm_g, m_w_xq, m_w_xkv, m_w_xo, m_norm_ffn_g, m_w_gate, m_w_up, m_w_down, m_final_norm_g, v_norm_mix_g, v_w_in, v_q_norm_g, v_w_uq, v_kv_norm_g, v_w_ukv, v_ssm_lambda_re, v_ssm_lambda_im, v_ssm_log_dt, v_ssm_b_re, v_ssm_b_im, v_ssm_c_re, v_ssm_c_im, v_ssm_d, v_ssm_w_glu, v_ssm_b_glu, v_attn_out_g, v_ssm_out_g, v_w_out, v_norm_x_g, v_mem_norm_g, v_w_xq, v_w_xkv, v_w_xo, v_norm_ffn_g, v_w_gate, v_w_up, v_w_down, v_final_norm_g):
    args = dict(locals())
    W = {n: args[n] for n in WEIGHTS}
    M = {n: args['m_' + n] for n in WEIGHTS}
    V = {n: args['v_' + n] for n in WEIGHTS}
    s = x.shape[1]
    h = x[0]
    memx = mem[0]

    freqs = ROPE_THETA ** (-jnp.arange(0, QK_ROPE, 2, dtype=F32) / QK_ROPE)
    ang = positions[0].astype(F32)[:, None] * freqs
    c16, s16 = jnp.cos(ang), jnp.sin(ang)
    zeros = lambda n: jnp.zeros((s, n), F32)
    cos = jnp.concatenate([jnp.ones((s, QK_NOPE), F32), c16, c16, zeros(32)], axis=1)
    sin_lo = jnp.concatenate([zeros(QK_NOPE), -s16, zeros(ROT + 32)], axis=1)
    sin_hi = jnp.concatenate([zeros(QK_NOPE + ROT), s16, zeros(32)], axis=1)
    tabs = (cos, sin_lo, sin_hi)

    shards = [{n: W[n][l].astype(MXU) for n in SHARDED} for l in range(DEPTH)]
    gathered = dict(zip(EARLY, _run_exchange("gather_weights", _gather(_named(EARLY, shards[0])))))

    layers = []
    for l in range(DEPTH):
        wl = _layer_weights(gathered)
        s5_in = [W[n][l] for n in ('ssm_lambda_re', 'ssm_lambda_im', 'ssm_log_dt', 'ssm_b_re', 'ssm_b_im',
                                   'ssm_c_re', 'ssm_c_im')]
        (a_re, a_im, bre, bim, cre, cim), s5_vjp = jax.vjp(_s5_params, *s5_in)
        pl_ = {n: W[n][l][None] for n in ('norm_mix_g', 'q_norm_g', 'kv_norm_g', 'ssm_d', 'ssm_b_glu',
                                           'attn_out_g', 'ssm_out_g', 'norm_x_g', 'mem_norm_g', 'norm_ffn_g')}
        pl_.update(a_re=a_re, a_im=a_im, b_re=bre, b_im=bim, c_re=cre, c_im=cim)
        h, sv, wl, gathered = _layer_fwd(h, memx, tabs, wl, pl_, shards[l], shards[l + 1] if l + 1 < DEPTH else None)
        layers.append((wl, pl_, sv, s5_vjp))

    def f_loss(h_, tgt, g):
        y, r = _rms(h_, g[...])
        err = y - tgt
        part = 0.5 * jnp.sum(jnp.mean(err * err, axis=-1, keepdims=True), axis=0, keepdims=True)
        dh, dg = _rms_bwd(h_, g[...], r, err / D_MODEL)
        return dh, dg, jnp.broadcast_to(part, (8, LANES))
    dh, g_final, loss_part = _rows(
        "loss_head", f_loss, s, min(ROW_TILE, s), [(h, 'r0'), (loss_target[0], 'r0'), (final_norm_g[None], 'f')],
        [((s, D_MODEL), F32, 'r0'), ((1, D_MODEL), F32, 'a'), ((8, LANES), F32, 'a')])
    loss = lax.psum(loss_part[0, 0], ("x", "y", "c"))

    parts = [{} for _ in range(DEPTH)]
    g_rep = [None] * DEPTH
    blocks = None
    for l in reversed(range(DEPTH)):
        wl, pl_, sv, s5_vjp = layers[l]
        dh, gr, arrived = _layer_bwd(dh, sv, memx, tabs, wl, pl_, blocks)
        for (who, n), p in arrived.items():
            parts[l + 1 if who == 'nxt' else l][n] = p
        blocks = {n: _blocked(gr, n) for n in EARLY}
        ds5 = s5_vjp((gr['a_re'], gr['a_im'], gr['b_re'], gr['b_im'], gr['c_re'], gr['c_im']))
        rep = dict(zip(('ssm_lambda_re', 'ssm_lambda_im', 'ssm_log_dt', 'ssm_b_re', 'ssm_b_im', 'ssm_c_re',
                        'ssm_c_im'), ds5))
        for n in ('norm_mix_g', 'q_norm_g', 'kv_norm_g', 'ssm_d', 'ssm_b_glu', 'attn_out_g', 'ssm_out_g',
                  'norm_x_g', 'mem_norm_g', 'norm_ffn_g'):
            rep[n] = gr[n][0]
        g_rep[l] = rep
    grad_x = dh[None]

    rep_names = REPL_L + ['final_norm_g']
    g_loc = {n: jnp.stack([g_rep[l][n] for l in range(DEPTH)]) for n in REPL_L}
    g_loc['final_norm_g'] = g_final
    rest = [n for n in SHARDED if n not in parts[0]]
    last = _run_exchange("last_grads", _together(_scatter(_named(rest, blocks)),
                                                 _gather([_pack(_named(rep_names, g_loc))])))
    parts[0].update(zip(rest, last[:len(rest)]))

    out_sh = [{}, {}, {}, {}]
    for n in SHARDED:
        res = _adamw_weight("adamw_" + n, [parts[l][n] for l in range(DEPTH)], W[n], M[n], V[n])
        for kind, r in enumerate(res):
            out_sh[kind][n] = r

    shapes_rp = [(1,) + W[n].shape if W[n].ndim == 1 else W[n].shape for n in rep_names]
    g_rp = _unpack(_sum_sources("sum_small_grads", last[len(rest)]), shapes_rp)
    as_rows = lambda d: [d[n].reshape(shp) for n, shp in zip(rep_names, shapes_rp)]
    res_rp = (g_rp,) + _adamw_small("adamw_replicated", g_rp, as_rows(W), as_rows(M), as_rows(V))
    out_rp = [{n: a.reshape(W[n].shape) for n, a in zip(rep_names, r)} for r in res_rp]

    outs = [loss, grad_x]
    for kind in range(4):
        for n in WEIGHTS:
            outs.append(out_sh[kind][n] if n in SHARDED else out_rp[kind][n])
    return tuple(outs)
```

```python
from typing import Callable, NamedTuple

import jax
import jax.numpy as jnp
from jax import lax
from jax.experimental import pallas as pl
from jax.experimental.pallas import tpu as pltpu

F32 = jnp.float32
MXU = jnp.bfloat16

D_MODEL = 1024
MLA_HEADS = 8
QK_NOPE = 64
QK_ROPE = 32
V_DIM = 64
Q_LORA = 256
KV_LORA = 128
SSM_WIDTH = 512
SSM_GROUPS = 32
SSM_GROUP = 16
SSM_STATE = 64
X_HEADS = 4
X_HEAD_DIM = 256
D_FF = 2816
FF_CHUNK = D_FF // 2
FF_FWD_CHUNK = D_FF
ROPE_THETA = 10000.0
EPS = 1e-6
DEPTH = 2
N_DEV = 8
LANES = 128
HEAD_W = 128
MLA_SCALE = (QK_NOPE + QK_ROPE) ** -0.5
X_SCALE = X_HEAD_DIM ** -0.5
ADAM_LR, ADAM_B1, ADAM_B2, ADAM_EPS, ADAM_WD, ADAM_STEP = 0.001, 0.9, 0.999, 1e-08, 0.01, 10
VMEM_LIMIT = 56 * 1024 * 1024
FLASH_TILE = 512
DW_ROWS = 2048
X_ROWS = 512
ROW_TILE = 512
FFN_ROWS = 256
MESH = pl.DeviceIdType.MESH

SHARDED = ['w_in', 'w_uq', 'w_ukv', 'ssm_w_glu', 'w_out', 'w_xq', 'w_xkv', 'w_xo', 'w_gate', 'w_up', 'w_down']
REPL_L = ['norm_mix_g', 'q_norm_g', 'kv_norm_g', 'ssm_lambda_re', 'ssm_lambda_im', 'ssm_log_dt', 'ssm_b_re',
          'ssm_b_im', 'ssm_c_re', 'ssm_c_im', 'ssm_d', 'ssm_b_glu', 'attn_out_g', 'ssm_out_g', 'norm_x_g',
          'mem_norm_g', 'norm_ffn_g']
WEIGHTS = ['norm_mix_g', 'w_in', 'q_norm_g', 'w_uq', 'kv_norm_g', 'w_ukv', 'ssm_lambda_re', 'ssm_lambda_im',
           'ssm_log_dt', 'ssm_b_re', 'ssm_b_im', 'ssm_c_re', 'ssm_c_im', 'ssm_d', 'ssm_w_glu', 'ssm_b_glu',
           'attn_out_g', 'ssm_out_g', 'w_out', 'norm_x_g', 'mem_norm_g', 'w_xq', 'w_xkv', 'w_xo', 'norm_ffn_g',
           'w_gate', 'w_up', 'w_down', 'final_norm_g']


def _pcall(body, **kw):
    return pl.pallas_call(body, **kw)


def _mm(a, b):
    return jnp.dot(a.astype(MXU), b.astype(MXU), preferred_element_type=F32)


def _mm_nt(a, b):
    return lax.dot_general(a.astype(MXU), b.astype(MXU), (((1,), (1,)), ((), ())), preferred_element_type=F32)


def _mm_tn(a, b):
    return lax.dot_general(a.astype(MXU), b.astype(MXU), (((0,), (0,)), ((), ())), preferred_element_type=F32)


def _rms(x, g):
    r = lax.rsqrt(jnp.mean(x * x, axis=-1, keepdims=True) + EPS)
    return x * r * g, r


def _rms_bwd(x, g, r, dy):
    dyg = dy * g
    dx = r * dyg - x * (r * r * r) * jnp.mean(dyg * x, axis=-1, keepdims=True)
    return dx, jnp.sum(dy * x * r, axis=0, keepdims=True)


ROT = QK_ROPE // 2


def _rope(x, cos, sin_lo, sin_hi):
    return x * cos + pltpu.roll(x, HEAD_W - ROT, 1) * sin_lo + pltpu.roll(x, ROT, 1) * sin_hi


def _rope_t(g, cos, sin_lo, sin_hi):
    return g * cos + pltpu.roll(g * sin_lo, ROT, 1) + pltpu.roll(g * sin_hi, HEAD_W - ROT, 1)


def _softmax(s):
    m = jnp.max(s, axis=-1, keepdims=True)
    e = jnp.exp(s - m)
    return e / jnp.sum(e, axis=-1, keepdims=True)


def _lanes(x, j, w):
    return x[:, j * w:(j + 1) * w]


def _rows(name, fn, n, tm, ins, outs, side=None):
    def spec(shape, kind):
        nd = len(shape)
        if kind == 'p16':
            return pl.BlockSpec((shape[0], 16, shape[2]), lambda i: (0, jnp.maximum(i * (tm // 16) - 1, 0), 0))
        if kind == 'f':
            return pl.BlockSpec(shape, lambda i, _nd=nd: (0,) * _nd, pipeline_mode=pl.Buffered(1))
        if kind == 'a':
            return pl.BlockSpec(shape, lambda i, _nd=nd: (0,) * _nd)
        ax = int(kind[1])
        blk = tuple(tm if d == ax else s for d, s in enumerate(shape))
        return pl.BlockSpec(blk, lambda i, _ax=ax, _nd=nd: tuple(i if d == _ax else 0 for d in range(_nd)))

    n_in, n_out, n_steps = len(ins), len(outs), n // tm

    def body(*refs):
        in_refs, out_refs, steps = _side_split(refs, n_in, n_out, side)
        i = pl.program_id(0)
        if steps:
            pl.when(i == 0)(steps[0])
            pl.when(i == _pass_on_step(n_steps))(steps[1])
        args = [r if k == 'f' else r[...] for r, (_, k) in zip(in_refs, ins)]
        res = fn(*args)
        for r, (_, dt, k), v in zip(out_refs, outs, res):
            if k == 'a':
                _accumulate(r, v.astype(dt), i)
            else:
                r[...] = v.astype(dt)
        if steps:
            pl.when(i == n_steps - 1)(steps[2])

    s_in, s_out, s_shape, s_sems, s_ops = _side_args(side)
    res = _pcall(
        body, name=name + ("_x" if side else ""), grid=(n_steps,),
        in_specs=[spec(a.shape, k) for a, k in ins] + s_in,
        out_specs=[spec(s, k) for s, _, k in outs] + s_out,
        out_shape=[jax.ShapeDtypeStruct(s, dt) for s, dt, _ in outs] + s_shape,
        scratch_shapes=s_sems,
        compiler_params=pltpu.CompilerParams(dimension_semantics=("arbitrary",), vmem_limit_bytes=VMEM_LIMIT),
    )(*[a for a, _ in ins], *s_ops)
    return _Hosted(res[:n_out], res[n_out:]) if side else res


def _accumulate(ref, v, i):
    @pl.when(i == 0)
    def _():
        ref[...] = v

    @pl.when(i != 0)
    def _():
        ref[...] += v


def _mm_tn_call(name, a, b, tk=None, tn=None):
    out_dtype = MXU
    s, k = a.shape
    n = b.shape[1]
    tk, tn = tk or k, tn or n
    ts = min(DW_ROWS, s)
    ns = s // ts

    def body(a_ref, b_ref, o_ref, acc_ref):
        j = pl.program_id(2)
        _accumulate(acc_ref, _mm_tn(a_ref[...], b_ref[...]), j)

        @pl.when(j == ns - 1)
        def _():
            o_ref[...] = acc_ref[...].astype(out_dtype)

    return _pcall(
        body, name=name, grid=(k // tk, n // tn, ns),
        in_specs=[pl.BlockSpec((ts, tk), lambda ik, jn, j: (j, ik)),
                  pl.BlockSpec((ts, tn), lambda ik, jn, j: (j, jn))],
        out_specs=pl.BlockSpec((tk, tn), lambda ik, jn, j: (ik, jn)),
        out_shape=jax.ShapeDtypeStruct((k, n), out_dtype),
        scratch_shapes=[pltpu.VMEM((tk, tn), F32)],
        compiler_params=pltpu.CompilerParams(dimension_semantics=("arbitrary", "arbitrary", "arbitrary"),
                                             vmem_limit_bytes=VMEM_LIMIT),
    )(a, b)


def _side_split(refs, n_in, n_out, side):
    if side is None:
        return refs[:n_in], refs[n_in:n_in + n_out], None
    si, so = len(side.ins), len(side.out_shapes)
    own_in, side_in = refs[:n_in], refs[n_in:n_in + si]
    own_out, side_out = refs[n_in + si:n_in + si + n_out], refs[n_in + si + n_out:n_in + si + n_out + so]
    return own_in, own_out, side.steps(side_in, side_out, refs[n_in + si + n_out + so:])


def _pass_on_step(n_steps):
    return max(n_steps - 2, 0)


def _side_args(side):
    if side is None:
        return [], [], [], [], []
    any_spec = pl.BlockSpec(memory_space=pl.ANY)
    return ([any_spec] * len(side.ins), [any_spec] * len(side.out_shapes), list(side.out_shapes),
            list(side.sem_shapes), list(side.ins))


class _Hosted(NamedTuple):
    results: list
    arrived: list


def _hosted(res):
    return res if isinstance(res, _Hosted) else _Hosted(res, ())


def _flash_fwd(q, k, v, side=None):
    nh, s, w = q.shape
    t = min(FLASH_TILE, s)
    nq = s // t
    n_steps = (nh // 2) * nq

    def body(*refs):
        (q_ref, k_ref, v_ref), (o_ref, lse_ref), steps = _side_split(refs, 3, 2, side)
        step = pl.program_id(0) * nq + pl.program_id(1)
        if steps:
            pl.when(step == 0)(steps[0])
            pl.when(step == _pass_on_step(n_steps))(steps[1])
        qi = pl.program_id(1)
        qs = [q_ref[0], q_ref[1]]
        below = lax.broadcasted_iota(jnp.int32, (t, t), 1) <= lax.broadcasted_iota(jnp.int32, (t, t), 0)

        def tile(j, carry, diagonal):
            sl = pl.ds(pl.multiple_of(j * t, t), t)
            out = []
            for hh in range(2):
                m, l, acc = carry[3 * hh:3 * hh + 3]
                sc = _mm_nt(qs[hh], k_ref[hh, sl, :])
                if diagonal:
                    sc = jnp.where(below, sc, -1e30)
                m_new = jnp.maximum(m, jnp.max(sc, axis=1, keepdims=True))
                p = jnp.exp(sc - m_new)
                alpha = jnp.exp(m - m_new)
                out += [m_new, alpha * l + jnp.sum(p, axis=1, keepdims=True), alpha * acc + _mm(p, v_ref[hh, sl, :])]
            return tuple(out)

        init = (jnp.full((t, 1), -1e30, F32), jnp.zeros((t, 1), F32), jnp.zeros((t, w), F32)) * 2
        carry = lax.fori_loop(0, qi, lambda j, c: tile(j, c, False), init)
        carry = tile(qi, carry, True)
        o_ref[...] = carry[2] / carry[1] + carry[5] / carry[4]
        for hh in range(2):
            lse_ref[hh] = jnp.broadcast_to(carry[3 * hh] + jnp.log(carry[3 * hh + 1]), (t, w))
        if steps:
            pl.when(step == n_steps - 1)(steps[2])

    s_in, s_out, s_shape, s_sems, s_ops = _side_args(side)
    res = _pcall(
        body, name="mla_flash_fwd" + ("_x" if side else ""), grid=(nh // 2, nq),
        in_specs=[pl.BlockSpec((2, t, w), lambda p, i: (p, i, 0)),
                  pl.BlockSpec((2, s, w), lambda p, i: (p, 0, 0)),
                  pl.BlockSpec((2, s, w), lambda p, i: (p, 0, 0))] + s_in,
        out_specs=[pl.BlockSpec((t, w), lambda p, i: (i, p)),
                   pl.BlockSpec((2, t, w), lambda p, i: (p, i, 0))] + s_out,
        out_shape=[jax.ShapeDtypeStruct((s, (nh // 2) * w), F32), jax.ShapeDtypeStruct((nh, s, w), F32)] + s_shape,
        scratch_shapes=s_sems,
        compiler_params=pltpu.CompilerParams(dimension_semantics=("arbitrary", "arbitrary"),
                                             vmem_limit_bytes=VMEM_LIMIT),
    )(q, k, v, *s_ops)
    return res[0], res[1], res[2:]


def _flash_bwd(q, k, v, o, lse, do, side=None):
    nh, s, w = q.shape
    t = min(FLASH_TILE, s)
    nq = s // t
    n_steps = (nh // 2) * nq

    def body(*refs):
        (q_ref, k_ref, v_ref, o_ref, lse_ref, do_ref), (dq_ref, dk_ref, dv_ref), steps = _side_split(refs, 6, 3, side)
        step = pl.program_id(0) * nq + pl.program_id(1)
        if steps:
            pl.when(step == 0)(steps[0])
            pl.when(step == _pass_on_step(n_steps))(steps[1])
        j = pl.program_id(1)

        @pl.when(j == 0)
        def _():
            dq_ref[...] = jnp.zeros(dq_ref.shape, F32)

        below = lax.broadcasted_iota(jnp.int32, (t, t), 1) <= lax.broadcasted_iota(jnp.int32, (t, t), 0)
        lane = lax.broadcasted_iota(jnp.int32, (t, w), 1)
        heads = [jnp.logical_and(lane >= hh * V_DIM, lane < (hh + 1) * V_DIM) for hh in range(2)]
        ks = [k_ref[0], k_ref[1]]
        vs = [v_ref[0], v_ref[1]]

        def tile(i, carry, diagonal):
            sl = pl.ds(pl.multiple_of(i * t, t), t)
            dout_all, o_all = do_ref[sl, :], o_ref[sl, :]
            out = []
            for hh in range(2):
                dk, dv = carry[2 * hh], carry[2 * hh + 1]
                qh = q_ref[hh, sl, :]
                dout = jnp.where(heads[hh], dout_all, 0.0)
                sc = _mm_nt(qh, ks[hh])
                if diagonal:
                    sc = jnp.where(below, sc, -1e30)
                p = jnp.exp(sc - lse_ref[hh, sl, 0:1])
                dp = _mm_nt(dout, vs[hh])
                ds = p * (dp - jnp.sum(dout * o_all, axis=1, keepdims=True))
                dq_ref[hh, sl, :] += _mm(ds, ks[hh])
                out += [dk + _mm_tn(ds, qh), dv + _mm_tn(p, dout)]
            return tuple(out)

        carry = tile(j, (jnp.zeros((t, w), F32),) * 4, True)
        carry = lax.fori_loop(j + 1, nq, lambda i, c: tile(i, c, False), carry)
        for hh in range(2):
            dk_ref[hh] = carry[2 * hh]
            dv_ref[hh] = jnp.where(heads[hh], carry[2 * hh + 1], 0.0)
        if steps:
            pl.when(step == n_steps - 1)(steps[2])

    s_in, s_out, s_shape, s_sems, s_ops = _side_args(side)
    res = _pcall(
        body, name="mla_flash_bwd" + ("_x" if side else ""), grid=(nh // 2, nq),
        in_specs=[pl.BlockSpec((2, s, w), lambda p, j: (p, 0, 0)),
                  pl.BlockSpec((2, t, w), lambda p, j: (p, j, 0)),
                  pl.BlockSpec((2, t, w), lambda p, j: (p, j, 0)),
                  pl.BlockSpec((s, w), lambda p, j: (0, p)),
                  pl.BlockSpec((2, s, w), lambda p, j: (p, 0, 0)),
                  pl.BlockSpec((s, w), lambda p, j: (0, p))] + s_in,
        out_specs=[pl.BlockSpec((2, s, w), lambda p, j: (p, 0, 0)),
                   pl.BlockSpec((2, t, w), lambda p, j: (p, j, 0)),
                   pl.BlockSpec((2, t, w), lambda p, j: (p, j, 0))] + s_out,
        out_shape=[jax.ShapeDtypeStruct((nh, s, w), F32)] * 3 + s_shape,
        scratch_shapes=s_sems,
        compiler_params=pltpu.CompilerParams(dimension_semantics=("arbitrary", "arbitrary"),
                                             vmem_limit_bytes=VMEM_LIMIT),
    )(q, k, v, o, lse, do, *s_ops)
    return res[0], res[1], res[2], res[3:]


def _scan(src, w_re, w_im, a_re, a_im, reverse):
    s = src.shape[0]
    nb, w = a_re.shape[0], LANES
    ch = s // 8
    assert ch & (ch - 1) == 0
    grp = 4
    tr = min(512, s)

    def cmul(ar, ai, xr, xi):
        return ar * xr - ai * xi, ar * xi + ai * xr

    def body(src_ref, wr_ref, wi_ref, ar_ref, ai_ref, xr_out, xi_out, xr_ref, xi_ref):
        def project(c, carry):
            rows = pl.ds(pl.multiple_of(c * tr, tr), tr)
            u = src_ref[rows, :]
            if reverse:
                br, bi = _mm_nt(u, wr_ref[...]), _mm_nt(u, wi_ref[...])
            else:
                br, bi = _mm(u, wr_ref[...]), _mm(u, wi_ref[...])
            for g in range(grp):
                xr_ref[g, rows, :] = _lanes(br, g, w)
                xi_ref[g, rows, :] = _lanes(bi, g, w)
            return carry

        lax.fori_loop(0, s // tr, project, 0)
        sub = lax.broadcasted_iota(jnp.int32, (8, w), 0)

        def shift(x, k):
            if reverse:
                return jnp.where(sub < 8 - k, pltpu.roll(x, 8 - k, 0), 0.0)
            return jnp.where(sub >= k, pltpu.roll(x, k, 0), 0.0)

        ar = [jnp.broadcast_to(ar_ref[g], (8, w)) for g in range(grp)]
        ai = [jnp.broadcast_to(ai_ref[g], (8, w)) for g in range(grp)]

        def tsl(i):
            return pl.ds(pl.multiple_of(((ch - 1 - i) if reverse else i) * 8, 8), 8)

        def local(i, carry):
            out = []
            for g in range(grp):
                xr, xi = carry[2 * g], carry[2 * g + 1]
                pr, pi = cmul(ar[g], ai[g], xr, xi)
                nr = pr + xr_ref[g, tsl(i), :]
                ni = pi + xi_ref[g, tsl(i), :]
                xr_ref[g, tsl(i), :] = nr
                xi_ref[g, tsl(i), :] = ni
                out += [nr, ni]
            return tuple(out)

        fin = lax.fori_loop(0, ch, local, (jnp.zeros((8, w), F32),) * (2 * grp))

        carry_in = []
        for g in range(grp):
            pr, pi = ar[g], ai[g]
            for _ in range(ch.bit_length() - 1):
                pr, pi = cmul(pr, pi, pr, pi)
            fr, fi = fin[2 * g], fin[2 * g + 1]
            for kk in (1, 2, 4):
                sr, si = cmul(pr, pi, shift(fr, kk), shift(fi, kk))
                fr, fi = fr + sr, fi + si
                pr, pi = cmul(pr, pi, pr, pi)
            carry_in += [shift(fr, 1), shift(fi, 1)]

        def fix(ii, pw):
            i0, i1 = 2 * ii, 2 * ii + 1
            blk16 = pl.ds(pl.multiple_of(((ch // 2 - 1 - ii) if reverse else ii) * 16, 16), 16)
            out = []
            for g in range(grp):
                p0r, p0i = pw[2 * g], pw[2 * g + 1]
                p1r, p1i = cmul(p0r, p0i, ar[g], ai[g])
                c0r, c0i = cmul(p0r, p0i, carry_in[2 * g], carry_in[2 * g + 1])
                c1r, c1i = cmul(p1r, p1i, carry_in[2 * g], carry_in[2 * g + 1])
                v0r, v0i = xr_ref[g, tsl(i0), :] + c0r, xi_ref[g, tsl(i0), :] + c0i
                v1r, v1i = xr_ref[g, tsl(i1), :] + c1r, xi_ref[g, tsl(i1), :] + c1i
                pair_r, pair_i = ([v1r, v0r], [v1i, v0i]) if reverse else ([v0r, v1r], [v0i, v1i])
                xr_out[g, blk16, :] = jnp.concatenate(pair_r, axis=0).astype(xr_out.dtype)
                xi_out[g, blk16, :] = jnp.concatenate(pair_i, axis=0).astype(xi_out.dtype)
                nr, ni = cmul(p1r, p1i, ar[g], ai[g])
                out += [nr, ni]
            return tuple(out)

        lax.fori_loop(0, ch // 2, fix, tuple(x for g in range(grp) for x in (ar[g], ai[g])))

    per_j = 4 // grp
    blk = pl.BlockSpec((grp, s, w), lambda i: (i, 0, 0))
    ablk = pl.BlockSpec((grp, 1, w), lambda i: (i, 0, 0))
    sblk = pl.BlockSpec((s, w), lambda i: (0, i // per_j))
    if reverse:
        wblk = pl.BlockSpec((None, grp * w, w), lambda i: (i // per_j, i % per_j, 0))
    else:
        wblk = pl.BlockSpec((None, w, grp * w), lambda i: (i // per_j, 0, i % per_j))
    return _pcall(
        body, name="s5_scan_rev" if reverse else "s5_scan", grid=(nb // grp,),
        in_specs=[sblk, wblk, wblk, ablk, ablk], out_specs=[blk, blk],
        out_shape=[jax.ShapeDtypeStruct((nb, s, w), MXU)] * 2,
        scratch_shapes=[pltpu.VMEM((grp, s, w), F32)] * 2,
        compiler_params=pltpu.CompilerParams(dimension_semantics=("arbitrary",), vmem_limit_bytes=VMEM_LIMIT),
    )(src, w_re, w_im, a_re, a_im)


class _Exchange(NamedTuple):
    ins: list
    out_shapes: list
    sem_shapes: list
    steps: Callable


def _gather_steps(ins, outs, sems):
    n = len(ins)
    send_sems, recv_sems, local_sems = sems
    x, y, c = lax.axis_index("x"), lax.axis_index("y"), lax.axis_index("c")
    me, sibling = (x, y, c), (x, y, 1 - c)
    chips = [(1 - x, y), (x, 1 - y), (1 - x, 1 - y)]

    def copy(a, k, block, to, src=None):
        dst = outs[a].at[4 * block[0] + 2 * block[1] + block[2]]
        return pltpu.make_async_remote_copy(
            src_ref=dst if src is None else src, dst_ref=dst,
            send_sem=send_sems.at[a, k], recv_sem=recv_sems.at[a, k], device_id=to, device_id_type=MESH)

    mine = [pltpu.make_async_copy(ins[a], outs[a].at[4 * x + 2 * y + c], local_sems.at[a]) for a in range(n)]
    first = []
    for a in range(n):
        first.append(copy(a, 0, me, sibling, src=ins[a]))
        first += [copy(a, 1 + j, me, (*chip, c), src=ins[a]) for j, chip in enumerate(chips)]
    passed = [copy(a, 4 + j, (*chip, c), sibling) for j, chip in enumerate(chips) for a in range(n)]

    def start():
        for cp in mine + first:
            cp.start()

    def pass_on():
        i = 0
        for j, chip in enumerate(chips):
            for a in range(n):
                copy(a, 1 + j, (*chip, c), me).wait_recv()
                passed[i].start()
                i += 1

    def finish():
        for a in range(n):
            copy(a, 0, sibling, me).wait_recv()
            for j, chip in enumerate(chips):
                copy(a, 4 + j, (*chip, 1 - c), me).wait_recv()
        for cp in first + passed:
            cp.wait_send()
        for cp in mine:
            cp.wait()

    return start, pass_on, finish


def _gather(arrs):
    n = len(arrs)
    return _Exchange(list(arrs), [jax.ShapeDtypeStruct((N_DEV,) + a.shape, a.dtype) for a in arrs],
                     [pltpu.SemaphoreType.DMA((n, 7)), pltpu.SemaphoreType.DMA((n, 7)), pltpu.SemaphoreType.DMA((n,))],
                     _gather_steps)


def _scatter_steps(ins, outs, sems):
    n = len(ins)
    send_sems, recv_sems, local_sems = sems
    x, y, c = lax.axis_index("x"), lax.axis_index("y"), lax.axis_index("c")
    me = 4 * x + 2 * y + c
    own, sent, arrivals = [], [], []
    for a in range(n):
        own.append(pltpu.make_async_copy(ins[a].at[me], outs[a].at[me], local_sems.at[a]))
        for k in range(1, N_DEV):
            px, py, pc = x ^ ((k >> 2) & 1), y ^ ((k >> 1) & 1), c ^ (k & 1)
            peer = 4 * px + 2 * py + pc
            sent.append(pltpu.make_async_remote_copy(
                src_ref=ins[a].at[peer], dst_ref=outs[a].at[me],
                send_sem=send_sems.at[a, k - 1], recv_sem=recv_sems.at[a, k - 1],
                device_id=(px, py, pc), device_id_type=MESH))
            arrivals.append(pltpu.make_async_remote_copy(
                src_ref=ins[a].at[me], dst_ref=outs[a].at[peer],
                send_sem=send_sems.at[a, k - 1], recv_sem=recv_sems.at[a, k - 1],
                device_id=(x, y, c), device_id_type=MESH))

    def start():
        for cp in own + sent:
            cp.start()

    def pass_on():
        pass

    def finish():
        for cp in arrivals:
            cp.wait_recv()
        for cp in sent:
            cp.wait_send()
        for cp in own:
            cp.wait()

    return start, pass_on, finish


def _scatter(grads):
    n = len(grads)
    return _Exchange(list(grads), [jax.ShapeDtypeStruct(g.shape, g.dtype) for g in grads],
                     [pltpu.SemaphoreType.DMA((n, N_DEV - 1)), pltpu.SemaphoreType.DMA((n, N_DEV - 1)),
                      pltpu.SemaphoreType.DMA((n,))], _scatter_steps)


def _together(a, b):
    def steps(ins, outs, sems):
        sa = a.steps(ins[:len(a.ins)], outs[:len(a.out_shapes)], sems[:len(a.sem_shapes)])
        sb = b.steps(ins[len(a.ins):], outs[len(a.out_shapes):], sems[len(a.sem_shapes):])

        def both(k):
            def run():
                sa[k]()
                sb[k]()
            return run
        return both(0), both(1), both(2)

    return _Exchange(a.ins + b.ins, a.out_shapes + b.out_shapes, a.sem_shapes + b.sem_shapes, steps)


def _run_exchange(name, ex):
    n_in, n_out = len(ex.ins), len(ex.out_shapes)

    def body(*refs):
        for step in ex.steps(refs[:n_in], refs[n_in:n_in + n_out], refs[n_in + n_out:]):
            step()

    any_spec = pl.BlockSpec(memory_space=pl.ANY)
    return _pcall(body, name=name, in_specs=[any_spec] * n_in, out_specs=[any_spec] * n_out,
                  out_shape=list(ex.out_shapes), scratch_shapes=list(ex.sem_shapes))(*ex.ins)


def _adam_math(g, w_, m_, v_):
    m_new = ADAM_B1 * m_ + (1.0 - ADAM_B1) * g
    v_new = ADAM_B2 * v_ + (1.0 - ADAM_B2) * (g * g)
    m_hat = m_new / (1.0 - ADAM_B1 ** ADAM_STEP)
    v_hat = v_new / (1.0 - ADAM_B2 ** ADAM_STEP)
    delta = -ADAM_LR * (m_hat / (jnp.sqrt(v_hat) + ADAM_EPS) + ADAM_WD * w_)
    return delta, m_new, v_new


def _adamw_weight(name, parts, w, m, v):
    nl = len(parts)

    def body(*refs):
        p_refs = refs[:nl]
        w_ref, m_ref, v_ref, g_ref, d_ref, mo_ref, vo_ref = refs[nl:]
        for l in range(nl):
            g = p_refs[l][0].astype(F32)
            for j in range(1, N_DEV):
                g = g + p_refs[l][j].astype(F32)
            g_ref[l] = g
            d_ref[l], mo_ref[l], vo_ref[l] = _adam_math(g, w_ref[l], m_ref[l], v_ref[l])

    return _pcall(
        body, name=name, out_shape=[jax.ShapeDtypeStruct(w.shape, F32)] * 4,
        compiler_params=pltpu.CompilerParams(vmem_limit_bytes=VMEM_LIMIT),
    )(*parts, w, m, v)

def _sum_sources(name, parts):
    r = parts.shape[1]

    def body(p_ref, g_ref):
        g = p_ref[0]
        for j in range(1, N_DEV):
            g = g + p_ref[j]
        g_ref[...] = g

    return _pcall(body, name=name, out_shape=jax.ShapeDtypeStruct((r, LANES), F32),
                  compiler_params=pltpu.CompilerParams(vmem_limit_bytes=VMEM_LIMIT))(parts)


def _adamw_small(name, g, w, m, v):
    n = len(g)

    def body(*refs):
        g_r, w_r, m_r, v_r = (refs[k * n:(k + 1) * n] for k in range(4))
        d_r, mo_r, vo_r = (refs[k * n:(k + 1) * n] for k in range(4, 7))
        for i in range(n):
            d_r[i][...], mo_r[i][...], vo_r[i][...] = _adam_math(g_r[i][...], w_r[i][...], m_r[i][...], v_r[i][...])

    res = _pcall(body, name=name, out_shape=[jax.ShapeDtypeStruct(a.shape, F32) for a in w] * 3,
                 compiler_params=pltpu.CompilerParams(vmem_limit_bytes=VMEM_LIMIT))(*g, *w, *m, *v)
    return res[:n], res[n:2 * n], res[2 * n:]


def _pack(arrs):
    flat = jnp.concatenate([a.reshape(-1) for a in arrs])
    flat = jnp.pad(flat, (0, (-flat.shape[0]) % (8 * LANES)))
    return flat.reshape(-1, LANES)


def _unpack(packed, shapes):
    flat = packed.reshape(-1)
    out, off = [], 0
    for shp in shapes:
        size = 1
        for d in shp:
            size *= d
        out.append(flat[off:off + size].reshape(shp))
        off += size
    return out


def _s5_params(lam_re, lam_im, log_dt, b_re, b_im, c_re, c_im):
    dt = jnp.exp(log_dt)[:, None]
    e = jnp.exp(lam_re * dt)
    ang = lam_im * dt
    a_re, a_im = e * jnp.cos(ang), e * jnp.sin(ang)
    nr, ni = a_re - 1.0, a_im
    den = lam_re * lam_re + lam_im * lam_im
    cr = ((nr * lam_re + ni * lam_im) / den)[..., None]
    ci = ((ni * lam_re - nr * lam_im) / den)[..., None]
    bb_re = cr * b_re - ci * b_im
    bb_im = cr * b_im + ci * b_re
    eye = jnp.eye(8, dtype=F32)[None, :, None, :, None]

    def bblk(bb):
        t = jnp.transpose(bb.reshape(4, 8, SSM_STATE, SSM_GROUP), (0, 3, 1, 2))
        return (eye * t[:, None]).reshape(4, 8 * SSM_GROUP, 8 * SSM_STATE)

    def cblk(cc):
        t = jnp.transpose(cc.reshape(4, 8, SSM_GROUP, SSM_STATE), (0, 3, 1, 2))
        return (eye * t[:, None]).reshape(4, 8 * SSM_STATE, 8 * SSM_GROUP)

    nb = SSM_GROUPS * SSM_STATE // LANES
    return (a_re.reshape(nb, 1, LANES), a_im.reshape(nb, 1, LANES), bblk(bb_re), bblk(bb_im),
            cblk(c_re), -cblk(c_im))


def _cat_blocks(x3, j):
    return jnp.concatenate([x3[4 * j + k] for k in range(4)], axis=-1)


def _to_chunks(a):
    s, c = a.shape
    return a.reshape(8, s // 8, c).transpose(1, 0, 2).reshape(s, c)


def _from_chunks(a):
    s, c = a.shape
    return a.reshape(s // 8, 8, c).transpose(1, 0, 2).reshape(s, c)


EARLY = ['w_in', 'w_uq', 'w_ukv']

FWD_PLAN = {
    'flash': ('late', ['ssm_w_glu', 'w_out', 'w_xq', 'w_xkv', 'w_xo', 'w_gate', 'w_up', 'w_down']),
    'ffn': ('nxt', EARLY),
}
BWD_PLAN = {
    'ffn_bwd': ('nxt', EARLY),
    'xattn_bwd': ('own', ['w_down']),
    'flash_bwd': ('own', ['w_gate', 'w_up', 'w_xq', 'w_xkv', 'w_xo']),
    'mla_qkv_bwd': ('own', ['ssm_w_glu', 'w_out']),
}


def _named(names, d):
    return [d[n] for n in names]


def _layer_fwd(h, memx, tabs, wl, pl_, late=None, nxt=None):
    s = h.shape[0]
    tm = min(ROW_TILE, s)
    cos, sin_lo, sin_hi = tabs
    sv = {}
    wl = dict(wl)
    nxt_got = {}

    def fetch(host):
        who, names = FWD_PLAN.get(host, (None, []))
        src = late if who == 'late' else nxt if who == 'nxt' else None
        return _gather(_named(names, src)) if src else None

    def landed(host, got):
        who, names = FWD_PLAN.get(host, (None, []))
        if got and who == 'late':
            wl.update(_layer_weights(dict(zip(names, got))))
        elif got:
            nxt_got.update(zip(names, got))

    def f_mix_in(h_, g, w):
        xn, _ = _rms(h_, g[...])
        pr = _mm(xn, w[...])
        return pr[:, 0:512], pr[:, 512:1024]
    proj, u_nat = _rows("mix_in", f_mix_in, s, tm, [(h, 'r0'), (pl_['norm_mix_g'], 'f'), (wl['w_in'], 'f')],
                        [((s, 512), F32, 'r0')] * 2)

    def f_qkv(pr, cos_, slo, shi, gq, gkv, wq, wk, wv):
        cqn = _rms(pr[:, 0:Q_LORA], gq[...])[0].astype(MXU)
        kvn = _rms(pr[:, Q_LORA:Q_LORA + KV_LORA], gkv[...])[0].astype(MXU)
        krr = _rope(pr[:, 384:512], cos_, slo, shi)
        qs, ks, vs = [], [], []
        for hd in range(MLA_HEADS):
            qs.append(_rope(_mm(cqn, wq[hd]), cos_, slo, shi) * MLA_SCALE)
            ks.append(_mm(kvn, wk[hd]) + krr)
            vs.append(_mm(kvn, wv[hd]))
        return jnp.stack(qs), jnp.stack(ks), jnp.stack(vs)
    hshape = (MLA_HEADS, s, HEAD_W)
    (q, k, v), got = _hosted(_rows(
        "mla_qkv", f_qkv, s, tm,
        [(proj, 'r0'), (cos, 'r0'), (sin_lo, 'r0'), (sin_hi, 'r0'), (pl_['q_norm_g'], 'f'), (pl_['kv_norm_g'], 'f'),
         (wl['w_uq'], 'f'), (wl['w_k'], 'f'), (wl['w_v'], 'f')],
        [(hshape, MXU, 'r1')] * 3, fetch('mla_qkv')))
    landed('mla_qkv', got)

    a_out, lse, got = _flash_fwd(q, k, v, fetch('flash'))
    landed('flash', got)

    u_ch = _to_chunks(u_nat)

    x_re, x_im = _scan(u_ch, pl_['b_re'], pl_['b_im'], pl_['a_re'], pl_['a_im'], False)

    def f_s5_out(xr, xi, u, cre, cim, d, wglu, bglu):
        y = jnp.concatenate([_mm(_cat_blocks(xr, j), cre[j]) + _mm(_cat_blocks(xi, j), cim[j])
                             for j in range(4)], axis=-1) + d[...] * u
        z = _mm(jax.nn.gelu(y), wglu[...]) + bglu[...]
        return y, y * jax.nn.sigmoid(z)
    (y_ssm, s_out_ch), got = _hosted(_rows(
        "s5_out", f_s5_out, s, tm,
        [(x_re, 'r1'), (x_im, 'r1'), (u_ch, 'r0'), (pl_['c_re'], 'f'), (pl_['c_im'], 'f'),
         (pl_['ssm_d'], 'f'), (wl['ssm_w_glu'], 'f'), (pl_['ssm_b_glu'], 'f')],
        [((s, SSM_WIDTH), F32, 'r0')] * 2, fetch('s5_out')))
    landed('s5_out', got)
    s_out = _from_chunks(s_out_ch)

    def f_mix_out(h_, a, so, ga, gs, w):
        an = _rms(a, ga[...])[0]
        sn = _rms(so, gs[...])[0]
        return (h_ + _mm(jnp.concatenate([an, sn], axis=-1), w[...]),)
    (h1,), got = _hosted(_rows("mix_out", f_mix_out, s, tm,
                               [(h, 'r0'), (a_out, 'r0'), (s_out, 'r0'), (pl_['attn_out_g'], 'f'),
                                (pl_['ssm_out_g'], 'f'), (wl['w_out'], 'f')],
                               [((s, D_MODEL), F32, 'r0')], fetch('mix_out')))
    landed('mix_out', got)

    m_len = memx.shape[0]

    def f_memkv(mm_, g, w):
        mn = _rms(mm_, g[...])[0].astype(MXU)
        return (jnp.stack([_mm(mn, w[d]) for d in range(N_DEV)]),)
    kvm, = _rows("mem_kv", f_memkv, m_len, m_len, [(memx, 'r0'), (pl_['mem_norm_g'], 'f'), (wl['w_xkv'], 'f')],
                 [((N_DEV, m_len, X_HEAD_DIM), MXU, 'r1')])

    def f_xattn(h_, g, wq, kv_, wo):
        hn = _rms(h_, g[...])[0].astype(MXU)
        q_all = _mm(hn, wq[...]).astype(MXU)
        outs = []
        for hd in range(X_HEADS):
            p = _softmax(_mm_nt(_lanes(q_all, hd, X_HEAD_DIM), kv_[hd]) * X_SCALE)
            outs.append(_mm(p, kv_[X_HEADS + hd]).astype(MXU))
        return (h_ + _mm(jnp.concatenate(outs, axis=-1), wo[...]),)
    (h2,), got = _hosted(_rows("xattn", f_xattn, s, min(X_ROWS, s),
                               [(h1, 'r0'), (pl_['norm_x_g'], 'f'), (wl['w_xq'], 'f'), (kvm, 'f'), (wl['w_xo'], 'f')],
                               [((s, D_MODEL), F32, 'r0')], fetch('xattn')))
    landed('xattn', got)

    def f_ffn(h_, g, wg, wu, wd):
        hn = _rms(h_, g[...])[0].astype(MXU)
        y = jnp.zeros(h_.shape, F32)
        gates, ups = [], []
        for c in range(D_FF // FF_FWD_CHUNK):
            cs = pl.ds(c * FF_FWD_CHUNK, FF_FWD_CHUNK)
            gate, up = _mm(hn, wg[:, cs]), _mm(hn, wu[:, cs])
            y = y + _mm(gate * jax.nn.sigmoid(gate) * up, wd[cs, :])
            gates.append(gate)
            ups.append(up)
        return h_ + y, jnp.concatenate(gates, axis=-1), jnp.concatenate(ups, axis=-1)
    (h3, gate_f, up_f), got = _hosted(_rows(
        "ffn", f_ffn, s, min(FFN_ROWS, s),
        [(h2, 'r0'), (pl_['norm_ffn_g'], 'f'), (wl['w_gate'], 'f'), (wl['w_up'], 'f'), (wl['w_down'], 'f')],
        [((s, D_MODEL), F32, 'r0'), ((s, D_FF), MXU, 'r0'), ((s, D_FF), MXU, 'r0')], fetch('ffn')))
    landed('ffn', got)
    sv.update(h=h, proj=proj, q=q, k=k, v=v, a_out=a_out, lse=lse, x_re=x_re, x_im=x_im, y_ssm=y_ssm,
              s_out=s_out, h1=h1, kvm=kvm, h2=h2, u_ch=u_ch, gate=gate_f, up=up_f)
    return h3, sv, wl, nxt_got


def _layer_bwd(dh3, sv, memx, tabs, wl, pl_, nxt=None):
    s = dh3.shape[0]
    tm = min(ROW_TILE, s)
    cos, sin_lo, sin_hi = tabs
    gr = {}
    arrived = {}
    act_shape = (s, D_FF)

    def send(host):
        who, names = BWD_PLAN.get(host, (None, []))
        if who is None or (who == 'nxt' and not nxt):
            return None, []
        return (_scatter([nxt[n] if who == 'nxt' else _blocked(gr, n) for n in names]),
                [(who, n) for n in names])

    def f_ffn_bwd(h_, dy, gate_, up_, g, wg, wu, wd):
        hn, r = _rms(h_, g[...])
        hb = hn.astype(MXU)
        dyb = dy.astype(MXU)
        dhn = jnp.zeros(h_.shape, F32)
        acts, dgs, dus = [], [], []
        for c in range(D_FF // FF_CHUNK):
            cs = pl.ds(c * FF_CHUNK, FF_CHUNK)
            gate = _lanes(gate_, c, FF_CHUNK).astype(F32)
            up = _lanes(up_, c, FF_CHUNK).astype(F32)
            sg = jax.nn.sigmoid(gate)
            si = gate * sg
            dact = _mm_nt(dyb, wd[cs, :])
            dgate = (dact * up * (sg * (1.0 + gate * (1.0 - sg)))).astype(MXU)
            dup = (dact * si).astype(MXU)
            dhn = dhn + _mm_nt(dgate, wg[:, cs]) + _mm_nt(dup, wu[:, cs])
            acts.append((si * up).astype(MXU))
            dgs.append(dgate)
            dus.append(dup)
        dh, dg = _rms_bwd(h_, g[...], r, dhn)
        cat = lambda parts: jnp.concatenate(parts, axis=-1)
        return dy + dh, hb, cat(acts), cat(dgs), cat(dus), dg
    ex, keys = send('ffn_bwd')
    (dh2, hn_f, act, dgate, dup, gr['norm_ffn_g']), got = _hosted(_rows(
        "ffn_bwd", f_ffn_bwd, s, min(FFN_ROWS, s),
        [(sv['h2'], 'r0'), (dh3, 'r0'), (sv['gate'], 'r0'), (sv['up'], 'r0'), (pl_['norm_ffn_g'], 'f'),
         (wl['w_gate'], 'f'), (wl['w_up'], 'f'), (wl['w_down'], 'f')],
        [((s, D_MODEL), F32, 'r0'), ((s, D_MODEL), MXU, 'r0'), (act_shape, MXU, 'r0'), (act_shape, MXU, 'r0'),
         (act_shape, MXU, 'r0'), ((1, D_MODEL), F32, 'a')], ex))
    arrived.update(zip(keys, got))
    gr['w_gate'] = _mm_tn_call("dw_gate", hn_f, dgate, tn=FF_CHUNK)
    gr['w_up'] = _mm_tn_call("dw_up", hn_f, dup, tn=FF_CHUNK)
    gr['w_down'] = _mm_tn_call("dw_down", act, dh3, tk=FF_CHUNK)

    m_len = memx.shape[0]

    def f_xattn_bwd(h_, dy, g, wq, kv_, wo):
        hn, r = _rms(h_, g[...])
        hb = hn.astype(MXU)
        q_all = _mm(hb, wq[...]).astype(MXU)
        do_all = _mm_nt(dy, wo[...]).astype(MXU)
        dqs, ohs, dks, dvs = [], [], [], []
        for hd in range(X_HEADS):
            kh, vh = kv_[hd], kv_[X_HEADS + hd]
            qh, do = _lanes(q_all, hd, X_HEAD_DIM), _lanes(do_all, hd, X_HEAD_DIM)
            p = _softmax(_mm_nt(qh, kh) * X_SCALE)
            ohs.append(_mm(p, vh).astype(MXU))
            dvs.append(_mm_tn(p, do))
            dp = _mm_nt(do, vh)
            ds = p * (dp - jnp.sum(dp * p, axis=-1, keepdims=True)) * X_SCALE
            dqs.append(_mm(ds, kh).astype(MXU))
            dks.append(_mm_tn(ds, qh))
        dq_all = jnp.concatenate(dqs, axis=-1)
        dh, dg = _rms_bwd(h_, g[...], r, _mm_nt(dq_all, wq[...]))
        return dy + dh, hb, dq_all, jnp.concatenate(ohs, axis=-1), jnp.stack(dks + dvs), dg
    ex, keys = send('xattn_bwd')
    (dh1, hn_x, dq_x, oh_x, dkvm, gr['norm_x_g']), got = _hosted(_rows(
        "xattn_bwd", f_xattn_bwd, s, min(X_ROWS, s),
        [(sv['h1'], 'r0'), (dh2, 'r0'), (pl_['norm_x_g'], 'f'), (wl['w_xq'], 'f'), (sv['kvm'], 'f'),
         (wl['w_xo'], 'f')],
        [((s, D_MODEL), F32, 'r0'), ((s, D_MODEL), MXU, 'r0'), ((s, D_MODEL), MXU, 'r0'),
         ((s, D_MODEL), MXU, 'r0'), ((N_DEV, m_len, X_HEAD_DIM), F32, 'a'), ((1, D_MODEL), F32, 'a')], ex))
    arrived.update(zip(keys, got))
    gr['w_xq'] = _mm_tn_call("dw_xq", hn_x, dq_x)
    gr['w_xo'] = _mm_tn_call("dw_xo", oh_x, dh2)

    def f_memkv_bwd(mm_, dkv, g, w):
        mn, r = _rms(mm_, g[...])
        mb = mn.astype(MXU)
        dmn = jnp.zeros(mm_.shape, F32)
        dws = []
        for d in range(N_DEV):
            dmn = dmn + _mm_nt(dkv[d], w[d])
            dws.append(_mm_tn(mb, dkv[d]))
        _, dg = _rms_bwd(mm_, g[...], r, dmn)
        return jnp.stack(dws), dg
    gr['w_xkv'], gr['mem_norm_g'] = _rows(
        "mem_kv_bwd", f_memkv_bwd, m_len, m_len,
        [(memx, 'r0'), (dkvm, 'r1'), (pl_['mem_norm_g'], 'f'), (wl['w_xkv'], 'f')],
        [((N_DEV, D_MODEL, X_HEAD_DIM), F32, 'a'), ((1, D_MODEL), F32, 'a')])

    def f_mix_out_bwd(a, so, dy, ga, gs, w):
        dmix = _mm_nt(dy, w[...])
        an, ra = _rms(a, ga[...])
        sn, rs = _rms(so, gs[...])
        da, dga = _rms_bwd(a, ga[...], ra, dmix[:, 0:512])
        dso, dgs = _rms_bwd(so, gs[...], rs, dmix[:, 512:1024])
        return da, dso, jnp.concatenate([an, sn], axis=-1), dga, dgs
    da_out, ds_out, mixed, gr['attn_out_g'], gr['ssm_out_g'] = _rows(
        "mix_out_bwd", f_mix_out_bwd, s, tm,
        [(sv['a_out'], 'r0'), (sv['s_out'], 'r0'), (dh1, 'r0'), (pl_['attn_out_g'], 'f'), (pl_['ssm_out_g'], 'f'),
         (wl['w_out'], 'f')],
        [((s, 512), F32, 'r0'), ((s, 512), F32, 'r0'), ((s, D_MODEL), MXU, 'r0'), ((1, 512), F32, 'a'),
         ((1, 512), F32, 'a')])
    gr['w_out'] = _mm_tn_call("dw_out", mixed, dh1)

    ex, keys = send('flash_bwd')
    dq, dk, dv, got = _flash_bwd(sv['q'], sv['k'], sv['v'], sv['a_out'], sv['lse'], da_out, ex)
    arrived.update(zip(keys, got))

    def f_s5_out_bwd(xr, xi, u, y, ds, cre, cim, d, wglu, bglu):
        g, gelu_vjp = jax.vjp(jax.nn.gelu, y)
        sig = jax.nn.sigmoid(_mm(g, wglu[...]) + bglu[...])
        dz = ds * y * sig * (1.0 - sig)
        dy = ds * sig + gelu_vjp(_mm_nt(dz, wglu[...]))[0]
        dcr, dci = [], []
        for j in range(4):
            dyj = _lanes(dy, j, LANES)
            dcr.append(_mm_tn(_cat_blocks(xr, j), dyj))
            dci.append(_mm_tn(_cat_blocks(xi, j), dyj))
        return (dy, dy * d[...], jnp.stack(dcr), jnp.stack(dci),
                jnp.sum(dy * u, axis=0, keepdims=True), _mm_tn(g, dz), jnp.sum(dz, axis=0, keepdims=True))
    ex, keys = send('s5_out_bwd')
    (dy_ssm, du_dir, gr['c_re'], gr['c_im'], gr['ssm_d'], gr['ssm_w_glu'], gr['ssm_b_glu']), got = _hosted(_rows(
        "s5_out_bwd", f_s5_out_bwd, s, tm,
        [(sv['x_re'], 'r1'), (sv['x_im'], 'r1'), (sv['u_ch'], 'r0'), (sv['y_ssm'], 'r0'), (_to_chunks(ds_out), 'r0'),
         (pl_['c_re'], 'f'), (pl_['c_im'], 'f'), (pl_['ssm_d'], 'f'), (wl['ssm_w_glu'], 'f'),
         (pl_['ssm_b_glu'], 'f')],
        [((s, 512), F32, 'r0'), ((s, 512), F32, 'r0'), ((4, 512, LANES), F32, 'a'),
         ((4, 512, LANES), F32, 'a'), ((1, 512), F32, 'a'), ((512, 512), F32, 'a'), ((1, 512), F32, 'a')], ex))
    arrived.update(zip(keys, got))
    g_re, g_im = _scan(dy_ssm, pl_['c_re'], pl_['c_im'], pl_['a_re'], -pl_['a_im'], True)
    first_re = jnp.pad(sv['x_re'][:, s - 8:s - 1].astype(F32), ((0, 0), (1, 0), (0, 0)))
    first_im = jnp.pad(sv['x_im'][:, s - 8:s - 1].astype(F32), ((0, 0), (1, 0), (0, 0)))

    def f_s5_in_bwd(gre, gim, xr, xi, pr16, pi16, u, dud, f8r, f8i, bre, bim):
        first = pl.program_id(0) == 0
        xr32, xi32 = xr.astype(F32), xi.astype(F32)
        xpr = jnp.concatenate([jnp.where(first, f8r[...], pr16.astype(F32)[:, 8:16]), xr32[:, :tm - 8]], axis=1)
        xpi = jnp.concatenate([jnp.where(first, f8i[...], pi16.astype(F32)[:, 8:16]), xi32[:, :tm - 8]], axis=1)
        gre32, gim32 = gre.astype(F32), gim.astype(F32)
        dus, dbr, dbi = [], [], []
        for j in range(4):
            gj_r, gj_i, uj = _cat_blocks(gre, j), _cat_blocks(gim, j), _lanes(u, j, LANES)
            dus.append(_mm_nt(gj_r, bre[j]) + _mm_nt(gj_i, bim[j]))
            dbr.append(_mm_tn(uj, gj_r))
            dbi.append(_mm_tn(uj, gj_i))
        da_r = jnp.sum(gre32 * xpr + gim32 * xpi, axis=1, keepdims=True)
        da_i = jnp.sum(gim32 * xpr - gre32 * xpi, axis=1, keepdims=True)
        return dud + jnp.concatenate(dus, axis=-1), jnp.stack(dbr), jnp.stack(dbi), da_r, da_i
    ex, keys = send('s5_in_bwd')
    (du_ch, gr['b_re'], gr['b_im'], gr['a_re'], gr['a_im']), got = _hosted(_rows(
        "s5_in_bwd", f_s5_in_bwd, s, tm,
        [(g_re, 'r1'), (g_im, 'r1'), (sv['x_re'], 'r1'), (sv['x_im'], 'r1'), (sv['x_re'], 'p16'), (sv['x_im'], 'p16'),
         (sv['u_ch'], 'r0'), (du_dir, 'r0'), (first_re, 'f'), (first_im, 'f'), (pl_['b_re'], 'f'), (pl_['b_im'], 'f')],
        [((s, 512), F32, 'r0'), ((4, LANES, 512), F32, 'a'), ((4, LANES, 512), F32, 'a'),
         ((16, 1, LANES), F32, 'a'), ((16, 1, LANES), F32, 'a')], ex))
    arrived.update(zip(keys, got))
    du = _from_chunks(du_ch)

    def f_qkv_bwd(pr, cos_, slo, shi, dq_, dk_, dv_, gq, gkv, wq, wk, wv):
        cq, ckv = pr[:, 0:Q_LORA], pr[:, Q_LORA:Q_LORA + KV_LORA]
        cqn, rq = _rms(cq, gq[...])
        kvn, rkv = _rms(ckv, gkv[...])
        cqb, kvb = cqn.astype(MXU), kvn.astype(MXU)
        dcqn = jnp.zeros(cq.shape, F32)
        dkvn = jnp.zeros(ckv.shape, F32)
        dksum = jnp.zeros(dk_[0].shape, F32)
        dwq, dwk, dwv = [], [], []
        for hd in range(MLA_HEADS):
            dqp = (_rope_t(dq_[hd], cos_, slo, shi) * MLA_SCALE).astype(MXU)
            dkb, dvb = dk_[hd].astype(MXU), dv_[hd].astype(MXU)
            dwq.append(_mm_tn(cqb, dqp))
            dwk.append(_mm_tn(kvb, dkb))
            dwv.append(_mm_tn(kvb, dvb))
            dcqn = dcqn + _mm_nt(dqp, wq[hd])
            dkvn = dkvn + _mm_nt(dkb, wk[hd]) + _mm_nt(dvb, wv[hd])
            dksum = dksum + dk_[hd]
        dcq, dgq = _rms_bwd(cq, gq[...], rq, dcqn)
        dckv, dgkv = _rms_bwd(ckv, gkv[...], rkv, dkvn)
        dpa = jnp.concatenate([dcq, dckv, _rope_t(dksum, cos_, slo, shi)], axis=-1)
        return dpa, jnp.stack(dwq), jnp.stack(dwk), jnp.stack(dwv), dgq, dgkv
    ex, keys = send('mla_qkv_bwd')
    (dpa, gr['w_uq'], gr['w_k'], gr['w_v'], gr['q_norm_g'], gr['kv_norm_g']), got = _hosted(_rows(
        "mla_qkv_bwd", f_qkv_bwd, s, tm,
        [(sv['proj'], 'r0'), (cos, 'r0'), (sin_lo, 'r0'), (sin_hi, 'r0'), (dq, 'r1'), (dk, 'r1'), (dv, 'r1'),
         (pl_['q_norm_g'], 'f'), (pl_['kv_norm_g'], 'f'), (wl['w_uq'], 'f'), (wl['w_k'], 'f'), (wl['w_v'], 'f')],
        [((s, 512), F32, 'r0'), ((MLA_HEADS, Q_LORA, HEAD_W), F32, 'a'), ((MLA_HEADS, KV_LORA, HEAD_W), F32, 'a'),
         ((MLA_HEADS, KV_LORA, HEAD_W), F32, 'a'), ((1, Q_LORA), F32, 'a'), ((1, KV_LORA), F32, 'a')], ex))
    arrived.update(zip(keys, got))

    def f_mix_in_bwd(h_, dpa_, du_, dres, g, w):
        dproj = jnp.concatenate([dpa_, du_], axis=-1).astype(MXU)
        xn, r = _rms(h_, g[...])
        dh, dg = _rms_bwd(h_, g[...], r, _mm_nt(dproj, w[...]))
        return dres + dh, xn, dproj, dg
    ex, keys = send('mix_in_bwd')
    (dh0, xn, dproj, gr['norm_mix_g']), got = _hosted(_rows(
        "mix_in_bwd", f_mix_in_bwd, s, tm,
        [(sv['h'], 'r0'), (dpa, 'r0'), (du, 'r0'), (dh1, 'r0'), (pl_['norm_mix_g'], 'f'), (wl['w_in'], 'f')],
        [((s, D_MODEL), F32, 'r0'), ((s, D_MODEL), MXU, 'r0'), ((s, D_MODEL), MXU, 'r0'), ((1, D_MODEL), F32, 'a')],
        ex))
    arrived.update(zip(keys, got))
    gr['w_in'] = _mm_tn_call("dw_in", xn, dproj)
    return dh0, gr, arrived


def _layer_weights(w):
    wl = {}
    if 'w_in' in w:
        w_in = w['w_in'].reshape(D_MODEL, -1)
        z = lambda n: jnp.zeros((D_MODEL, n), w_in.dtype)
        wl['w_in'] = jnp.concatenate([w_in[:, :384], z(64), w_in[:, 384:416], z(32), w_in[:, 416:]], axis=1)
    if 'w_uq' in w:
        wl['w_uq'] = jnp.pad(w['w_uq'], ((0, 0), (0, 0), (0, HEAD_W - QK_NOPE - QK_ROPE)))
    if 'w_ukv' in w:
        wl['w_k'] = jnp.pad(w['w_ukv'][..., :QK_NOPE], ((0, 0), (0, 0), (0, HEAD_W - QK_NOPE)))
        wv = w['w_ukv'][..., QK_NOPE:]
        even = (jnp.arange(MLA_HEADS) % 2 == 0)[:, None, None]
        wl['w_v'] = jnp.concatenate([jnp.where(even, wv, 0), jnp.where(even, 0, wv)], axis=-1).astype(wv.dtype)
    if 'ssm_w_glu' in w:
        wl['ssm_w_glu'] = w['ssm_w_glu'].reshape(SSM_WIDTH, SSM_WIDTH)
    for n in ('w_out', 'w_xq', 'w_xo'):
        if n in w:
            wl[n] = w[n].reshape(D_MODEL, D_MODEL)
    if 'w_xkv' in w:
        wl['w_xkv'] = w['w_xkv']
    for n in ('w_gate', 'w_up'):
        if n in w:
            wl[n] = jnp.transpose(w[n], (1, 0, 2)).reshape(D_MODEL, D_FF)
    if 'w_down' in w:
        wl['w_down'] = w['w_down'].reshape(D_FF, D_MODEL)
    return wl


def _blocked(gr, n):
    if n == 'w_in':
        d = gr['w_in']
        out = jnp.concatenate([d[:, :384], d[:, 448:480], d[:, 512:]], axis=1).reshape(N_DEV, 128, -1)
    elif n == 'w_uq':
        out = gr['w_uq'][..., :QK_NOPE + QK_ROPE]
    elif n == 'w_ukv':
        even = (jnp.arange(MLA_HEADS) % 2 == 0)[:, None, None]
        dv = gr['w_v']
        out = jnp.concatenate([gr['w_k'][..., :QK_NOPE], jnp.where(even, dv[..., :V_DIM], dv[..., V_DIM:])], axis=-1)
    elif n == 'ssm_w_glu':
        out = gr['ssm_w_glu'].reshape(N_DEV, SSM_WIDTH // N_DEV, SSM_WIDTH)
    elif n in ('w_out', 'w_xq', 'w_xo'):
        out = gr[n].reshape(N_DEV, D_MODEL // N_DEV, D_MODEL)
    elif n in ('w_gate', 'w_up'):
        out = jnp.transpose(gr[n].reshape(D_MODEL, N_DEV, D_FF // N_DEV), (1, 0, 2))
    elif n == 'w_down':
        out = gr[n].reshape(N_DEV, D_FF // N_DEV, D_MODEL)
    else:
        out = gr[n]
    return out.astype(MXU)


def kernel(x, mem, positions, norm_mix_g, w_in, q_norm_g, w_uq, kv_norm_g, w_ukv, ssm_lambda_re, ssm_lambda_im, ssm_log_dt, ssm_b_re, ssm_b_im, ssm_c_re, ssm_c_im, ssm_d, ssm_w_glu, ssm_b_glu, attn_out_g, ssm_out_g, w_out, norm_x_g, mem_norm_g, w_xq, w_xkv, w_xo, norm_ffn_g, w_gate, w_up, w_down, final_norm_g, loss_target, m_norm_mix_g, m_w_in, m_q_norm_g, m_w_uq, m_kv_norm_g, m_w_ukv, m_ssm_lambda_re, m_ssm_lambda_im, m_ssm_log_dt, m_ssm_b_re, m_ssm_b_im, m_ssm_c_re, m_ssm_c_im, m_ssm_d, m_ssm_w_glu, m_ssm_b_glu, m_attn_out_g, m_ssm_out_g, m_w_out, m_norm_x_g, m_mem_norm_g, m_w_xq, m_w_xkv, m_w_xo, m_norm_ffn_g, m_w_gate, m_w_up, m_w_down, m_final_norm_g, v_norm_mix_g, v_w_in, v_q_norm_g, v_w_uq, v_kv_norm_g, v_w_ukv, v_ssm_lambda_re, v_ssm_lambda_im, v_ssm_log_dt, v_ssm_b_re, v_ssm_b_im, v_ssm_c_re, v_ssm_c_im, v_ssm_d, v_ssm_w_glu, v_ssm_b_glu, v_attn_out_g, v_ssm_out_g, v_w_out, v_norm_x_g, v_mem_norm_g, v_w_xq, v_w_xkv, v_w_xo, v_norm_ffn_g, v_w_gate, v_w_up, v_w_down, v_final_norm_g):
    args = dict(locals())
    W = {n: args[n] for n in WEIGHTS}
    M = {n: args['m_' + n] for n in WEIGHTS}
    V = {n: args['v_' + n] for n in WEIGHTS}
    s = x.shape[1]
    h = x[0]
    memx = mem[0]

    freqs = ROPE_THETA ** (-jnp.arange(0, QK_ROPE, 2, dtype=F32) / QK_ROPE)
    ang = positions[0].astype(F32)[:, None] * freqs
    c16, s16 = jnp.cos(ang), jnp.sin(ang)
    zeros = lambda n: jnp.zeros((s, n), F32)
    cos = jnp.concatenate([jnp.ones((s, QK_NOPE), F32), c16, c16, zeros(32)], axis=1)
    sin_lo = jnp.concatenate([zeros(QK_NOPE), -s16, zeros(ROT + 32)], axis=1)
    sin_hi = jnp.concatenate([zeros(QK_NOPE + ROT), s16, zeros(32)], axis=1)
    tabs = (cos, sin_lo, sin_hi)

    shards = [{n: W[n][l].astype(MXU) for n in SHARDED} for l in range(DEPTH)]
    gathered = dict(zip(EARLY, _run_exchange("gather_weights", _gather(_named(EARLY, shards[0])))))

    layers = []
    for l in range(DEPTH):
        wl = _layer_weights(gathered)
        s5_in = [W[n][l] for n in ('ssm_lambda_re', 'ssm_lambda_im', 'ssm_log_dt', 'ssm_b_re', 'ssm_b_im',
                                   'ssm_c_re', 'ssm_c_im')]
        (a_re, a_im, bre, bim, cre, cim), s5_vjp = jax.vjp(_s5_params, *s5_in)
        pl_ = {n: W[n][l][None] for n in ('norm_mix_g', 'q_norm_g', 'kv_norm_g', 'ssm_d', 'ssm_b_glu',
                                           'attn_out_g', 'ssm_out_g', 'norm_x_g', 'mem_norm_g', 'norm_ffn_g')}
        pl_.update(a_re=a_re, a_im=a_im, b_re=bre, b_im=bim, c_re=cre, c_im=cim)
        h, sv, wl, gathered = _layer_fwd(h, memx, tabs, wl, pl_, shards[l], shards[l + 1] if l + 1 < DEPTH else None)
        layers.append((wl, pl_, sv, s5_vjp))

    def f_loss(h_, tgt, g):
        y, r = _rms(h_, g[...])
        err = y - tgt
        part = 0.5 * jnp.sum(jnp.mean(err * err, axis=-1, keepdims=True), axis=0, keepdims=True)
        dh, dg = _rms_bwd(h_, g[...], r, err / D_MODEL)
        return dh, dg, jnp.broadcast_to(part, (8, LANES))
    dh, g_final, loss_part = _rows(
        "loss_head", f_loss, s, min(ROW_TILE, s), [(h, 'r0'), (loss_target[0], 'r0'), (final_norm_g[None], 'f')],
        [((s, D_MODEL), F32, 'r0'), ((1, D_MODEL), F32, 'a'), ((8, LANES), F32, 'a')])
    loss = lax.psum(loss_part[0, 0], ("x", "y", "c"))

    parts = [{} for _ in range(DEPTH)]
    g_rep = [None] * DEPTH
    blocks = None
    for l in reversed(range(DEPTH)):
        wl, pl_, sv, s5_vjp = layers[l]
        dh, gr, arrived = _layer_bwd(dh, sv, memx, tabs, wl, pl_, blocks)
        for (who, n), p in arrived.items():
            parts[l + 1 if who == 'nxt' else l][n] = p
        blocks = {n: _blocked(gr, n) for n in EARLY}
        ds5 = s5_vjp((gr['a_re'], gr['a_im'], gr['b_re'], gr['b_im'], gr['c_re'], gr['c_im']))
        rep = dict(zip(('ssm_lambda_re', 'ssm_lambda_im', 'ssm_log_dt', 'ssm_b_re', 'ssm_b_im', 'ssm_c_re',
                        'ssm_c_im'), ds5))
        for n in ('norm_mix_g', 'q_norm_g', 'kv_norm_g', 'ssm_d', 'ssm_b_glu', 'attn_out_g', 'ssm_out_g',
                  'norm_x_g', 'mem_norm_g', 'norm_ffn_g'):
            rep[n] = gr[n][0]
        g_rep[l] = rep
    grad_x = dh[None]

    rep_names = REPL_L + ['final_norm_g']
    g_loc = {n: jnp.stack([g_rep[l][n] for l in range(DEPTH)]) for n in REPL_L}
    g_loc['final_norm_g'] = g_final
    rest = [n for n in SHARDED if n not in parts[0]]
    last = _run_exchange("last_grads", _together(_scatter(_named(rest, blocks)),
                                                 _gather([_pack(_named(rep_names, g_loc))])))
    parts[0].update(zip(rest, last[:len(rest)]))

    out_sh = [{}, {}, {}, {}]
    for n in SHARDED:
        res = _adamw_weight("adamw_" + n, [parts[l][n] for l in range(DEPTH)], W[n], M[n], V[n])
        for kind, r in enumerate(res):
            out_sh[kind][n] = r

    shapes_rp = [(1,) + W[n].shape if W[n].ndim == 1 else W[n].shape for n in rep_names]
    g_rp = _unpack(_sum_sources("sum_small_grads", last[len(rest)]), shapes_rp)
    as_rows = lambda d: [d[n].reshape(shp) for n, shp in zip(rep_names, shapes_rp)]
    res_rp = (g_rp,) + _adamw_small("adamw_replicated", g_rp, as_rows(W), as_rows(M), as_rows(V))
    out_rp = [{n: a.reshape(W[n].shape) for n, a in zip(rep_names, r)} for r in res_rp]

    outs = [loss, grad_x]
    for kind in range(4):
        for n in WEIGHTS:
            outs.append(out_sh[kind][n] if n in SHARDED else out_rp[kind][n])
    return tuple(outs)
```

```python
from typing import Callable, NamedTuple

import jax
import jax.numpy as jnp
from jax import lax
from jax.experimental import pallas as pl
from jax.experimental.pallas import tpu as pltpu

F32 = jnp.float32
MXU = jnp.bfloat16

D_MODEL = 1024
MLA_HEADS = 8
QK_NOPE = 64
QK_ROPE = 32
V_DIM = 64
Q_LORA = 256
KV_LORA = 128
SSM_WIDTH = 512
SSM_GROUPS = 32
SSM_GROUP = 16
SSM_STATE = 64
X_HEADS = 4
X_HEAD_DIM = 256
D_FF = 2816
FF_CHUNK = D_FF // 2
FF_FWD_CHUNK = D_FF
ROPE_THETA = 10000.0
EPS = 1e-6
DEPTH = 2
N_DEV = 8
LANES = 128
HEAD_W = 128
MLA_SCALE = (QK_NOPE + QK_ROPE) ** -0.5
X_SCALE = X_HEAD_DIM ** -0.5
ADAM_LR, ADAM_B1, ADAM_B2, ADAM_EPS, ADAM_WD, ADAM_STEP = 0.001, 0.9, 0.999, 1e-08, 0.01, 10
VMEM_LIMIT = 56 * 1024 * 1024
FLASH_TILE = 512
DW_ROWS = 2048
X_ROWS = 512
ROW_TILE = 512
FFN_ROWS = 256
MESH = pl.DeviceIdType.MESH

SHARDED = ['w_in', 'w_uq', 'w_ukv', 'ssm_w_glu', 'w_out', 'w_xq', 'w_xkv', 'w_xo', 'w_gate', 'w_up', 'w_down']
REPL_L = ['norm_mix_g', 'q_norm_g', 'kv_norm_g', 'ssm_lambda_re', 'ssm_lambda_im', 'ssm_log_dt', 'ssm_b_re',
          'ssm_b_im', 'ssm_c_re', 'ssm_c_im', 'ssm_d', 'ssm_b_glu', 'attn_out_g', 'ssm_out_g', 'norm_x_g',
          'mem_norm_g', 'norm_ffn_g']
WEIGHTS = ['norm_mix_g', 'w_in', 'q_norm_g', 'w_uq', 'kv_norm_g', 'w_ukv', 'ssm_lambda_re', 'ssm_lambda_im',
           'ssm_log_dt', 'ssm_b_re', 'ssm_b_im', 'ssm_c_re', 'ssm_c_im', 'ssm_d', 'ssm_w_glu', 'ssm_b_glu',
           'attn_out_g', 'ssm_out_g', 'w_out', 'norm_x_g', 'mem_norm_g', 'w_xq', 'w_xkv', 'w_xo', 'norm_ffn_g',
           'w_gate', 'w_up', 'w_down', 'final_norm_g']


def _pcall(body, **kw):
    return pl.pallas_call(body, **kw)


def _mm(a, b):
    return jnp.dot(a.astype(MXU), b.astype(MXU), preferred_element_type=F32)


def _mm_nt(a, b):
    return lax.dot_general(a.astype(MXU), b.astype(MXU), (((1,), (1,)), ((), ())), preferred_element_type=F32)


def _mm_tn(a, b):
    return lax.dot_general(a.astype(MXU), b.astype(MXU), (((0,), (0,)), ((), ())), preferred_element_type=F32)


def _rms(x, g):
    r = lax.rsqrt(jnp.mean(x * x, axis=-1, keepdims=True) + EPS)
    return x * r * g, r


def _rms_bwd(x, g, r, dy):
    dyg = dy * g
    dx = r * dyg - x * (r * r * r) * jnp.mean(dyg * x, axis=-1, keepdims=True)
    return dx, jnp.sum(dy * x * r, axis=0, keepdims=True)


ROT = QK_ROPE // 2


def _rope(x, cos, sin_lo, sin_hi):
    return x * cos + pltpu.roll(x, HEAD_W - ROT, 1) * sin_lo + pltpu.roll(x, ROT, 1) * sin_hi


def _rope_t(g, cos, sin_lo, sin_hi):
    return g * cos + pltpu.roll(g * sin_lo, ROT, 1) + pltpu.roll(g * sin_hi, HEAD_W - ROT, 1)


def _softmax(s):
    m = jnp.max(s, axis=-1, keepdims=True)
    e = jnp.exp(s - m)
    return e / jnp.sum(e, axis=-1, keepdims=True)


def _lanes(x, j, w):
    return x[:, j * w:(j + 1) * w]


def _rows(name, fn, n, tm, ins, outs, side=None):
    def spec(shape, kind):
        nd = len(shape)
        if kind == 'p16':
            return pl.BlockSpec((shape[0], 16, shape[2]), lambda i: (0, jnp.maximum(i * (tm // 16) - 1, 0), 0))
        if kind == 'f':
            return pl.BlockSpec(shape, lambda i, _nd=nd: (0,) * _nd, pipeline_mode=pl.Buffered(1))
        if kind == 'a':
            return pl.BlockSpec(shape, lambda i, _nd=nd: (0,) * _nd)
        ax = int(kind[1])
        blk = tuple(tm if d == ax else s for d, s in enumerate(shape))
        return pl.BlockSpec(blk, lambda i, _ax=ax, _nd=nd: tuple(i if d == _ax else 0 for d in range(_nd)))

    n_in, n_out, n_steps = len(ins), len(outs), n // tm

    def body(*refs):
        in_refs, out_refs, steps = _side_split(refs, n_in, n_out, side)
        i = pl.program_id(0)
        if steps:
            pl.when(i == 0)(steps[0])
            pl.when(i == _pass_on_step(n_steps))(steps[1])
        args = [r if k == 'f' else r[...] for r, (_, k) in zip(in_refs, ins)]
        res = fn(*args)
        for r, (_, dt, k), v in zip(out_refs, outs, res):
            if k == 'a':
                _accumulate(r, v.astype(dt), i)
            else:
                r[...] = v.astype(dt)
        if steps:
            pl.when(i == n_steps - 1)(steps[2])

    s_in, s_out, s_shape, s_sems, s_ops = _side_args(side)
    res = _pcall(
        body, name=name + ("_x" if side else ""), grid=(n_steps,),
        in_specs=[spec(a.shape, k) for a, k in ins] + s_in,
        out_specs=[spec(s, k) for s, _, k in outs] + s_out,
        out_shape=[jax.ShapeDtypeStruct(s, dt) for s, dt, _ in outs] + s_shape,
        scratch_shapes=s_sems,
        compiler_params=pltpu.CompilerParams(dimension_semantics=("arbitrary",), vmem_limit_bytes=VMEM_LIMIT),
    )(*[a for a, _ in ins], *s_ops)
    return _Hosted(res[:n_out], res[n_out:]) if side else res


def _accumulate(ref, v, i):
    @pl.when(i == 0)
    def _():
        ref[...] = v

    @pl.when(i != 0)
    def _():
        ref[...] += v


def _mm_tn_call(name, a, b, tk=None, tn=None):
    out_dtype = MXU
    s, k = a.shape
    n = b.shape[1]
    tk, tn = tk or k, tn or n
    ts = min(DW_ROWS, s)
    ns = s // ts

    def body(a_ref, b_ref, o_ref, acc_ref):
        j = pl.program_id(2)
        _accumulate(acc_ref, _mm_tn(a_ref[...], b_ref[...]), j)

        @pl.when(j == ns - 1)
        def _():
            o_ref[...] = acc_ref[...].astype(out_dtype)

    return _pcall(
        body, name=name, grid=(k // tk, n // tn, ns),
        in_specs=[pl.BlockSpec((ts, tk), lambda ik, jn, j: (j, ik)),
                  pl.BlockSpec((ts, tn), lambda ik, jn, j: (j, jn))],
        out_specs=pl.BlockSpec((tk, tn), lambda ik, jn, j: (ik, jn)),
        out_shape=jax.ShapeDtypeStruct((k, n), out_dtype),
        scratch_shapes=[pltpu.VMEM((tk, tn), F32)],
        compiler_params=pltpu.CompilerParams(dimension_semantics=("arbitrary", "arbitrary", "arbitrary"),
                                             vmem_limit_bytes=VMEM_LIMIT),
    )(a, b)


def _side_split(refs, n_in, n_out, side):
    if side is None:
        return refs[:n_in], refs[n_in:n_in + n_out], None
    si, so = len(side.ins), len(side.out_shapes)
    own_in, side_in = refs[:n_in], refs[n_in:n_in + si]
    own_out, side_out = refs[n_in + si:n_in + si + n_out], refs[n_in + si + n_out:n_in + si + n_out + so]
    return own_in, own_out, side.steps(side_in, side_out, refs[n_in + si + n_out + so:])


def _pass_on_step(n_steps):
    return max(n_steps - 2, 0)


def _side_args(side):
    if side is None:
        return [], [], [], [], []
    any_spec = pl.BlockSpec(memory_space=pl.ANY)
    return ([any_spec] * len(side.ins), [any_spec] * len(side.out_shapes), list(side.out_shapes),
            list(side.sem_shapes), list(side.ins))


class _Hosted(NamedTuple):
    results: list
    arrived: list


def _hosted(res):
    return res if isinstance(res, _Hosted) else _Hosted(res, ())


def _flash_fwd(q, k, v, side=None):
    nh, s, w = q.shape
    t = min(FLASH_TILE, s)
    nq = s // t
    n_steps = (nh // 2) * nq

    def body(*refs):
        (q_ref, k_ref, v_ref), (o_ref, lse_ref), steps = _side_split(refs, 3, 2, side)
        step = pl.program_id(0) * nq + pl.program_id(1)
        if steps:
            pl.when(step == 0)(steps[0])
            pl.when(step == _pass_on_step(n_steps))(steps[1])
        qi = pl.program_id(1)
        qs = [q_ref[0], q_ref[1]]
        below = lax.broadcasted_iota(jnp.int32, (t, t), 1) <= lax.broadcasted_iota(jnp.int32, (t, t), 0)

        def tile(j, carry, diagonal):
            sl = pl.ds(pl.multiple_of(j * t, t), t)
            out = []
            for hh in range(2):
                m, l, acc = carry[3 * hh:3 * hh + 3]
                sc = _mm_nt(qs[hh], k_ref[hh, sl, :])
                if diagonal:
                    sc = jnp.where(below, sc, -1e30)
                m_new = jnp.maximum(m, jnp.max(sc, axis=1, keepdims=True))
                p = jnp.exp(sc - m_new)
                alpha = jnp.exp(m - m_new)
                out += [m_new, alpha * l + jnp.sum(p, axis=1, keepdims=True), alpha * acc + _mm(p, v_ref[hh, sl, :])]
            return tuple(out)

        init = (jnp.full((t, 1), -1e30, F32), jnp.zeros((t, 1), F32), jnp.zeros((t, w), F32)) * 2
        carry = lax.fori_loop(0, qi, lambda j, c: tile(j, c, False), init)
        carry = tile(qi, carry, True)
        o_ref[...] = carry[2] / carry[1] + carry[5] / carry[4]
        for hh in range(2):
            lse_ref[hh] = jnp.broadcast_to(carry[3 * hh] + jnp.log(carry[3 * hh + 1]), (t, w))
        if steps:
            pl.when(step == n_steps - 1)(steps[2])

    s_in, s_out, s_shape, s_sems, s_ops = _side_args(side)
    res = _pcall(
        body, name="mla_flash_fwd" + ("_x" if side else ""), grid=(nh // 2, nq),
        in_specs=[pl.BlockSpec((2, t, w), lambda p, i: (p, i, 0)),
                  pl.BlockSpec((2, s, w), lambda p, i: (p, 0, 0)),
                  pl.BlockSpec((2, s, w), lambda p, i: (p, 0, 0))] + s_in,
        out_specs=[pl.BlockSpec((t, w), lambda p, i: (i, p)),
                   pl.BlockSpec((2, t, w), lambda p, i: (p, i, 0))] + s_out,
        out_shape=[jax.ShapeDtypeStruct((s, (nh // 2) * w), F32), jax.ShapeDtypeStruct((nh, s, w), F32)] + s_shape,
        scratch_shapes=s_sems,
        compiler_params=pltpu.CompilerParams(dimension_semantics=("arbitrary", "arbitrary"),
                                             vmem_limit_bytes=VMEM_LIMIT),
    )(q, k, v, *s_ops)
    return res[0], res[1], res[2:]


def _flash_bwd(q, k, v, o, lse, do, side=None):
    nh, s, w = q.shape
    t = min(FLASH_TILE, s)
    nq = s // t
    n_steps = (nh // 2) * nq

    def body(*refs):
        (q_ref, k_ref, v_ref, o_ref, lse_ref, do_ref), (dq_ref, dk_ref, dv_ref), steps = _side_split(refs, 6, 3, side)
        step = pl.program_id(0) * nq + pl.program_id(1)
        if steps:
            pl.when(step == 0)(steps[0])
            pl.when(step == _pass_on_step(n_steps))(steps[1])
        j = pl.program_id(1)

        @pl.when(j == 0)
        def _():
            dq_ref[...] = jnp.zeros(dq_ref.shape, F32)

        below = lax.broadcasted_iota(jnp.int32, (t, t), 1) <= lax.broadcasted_iota(jnp.int32, (t, t), 0)
        lane = lax.broadcasted_iota(jnp.int32, (t, w), 1)
        heads = [jnp.logical_and(lane >= hh * V_DIM, lane < (hh + 1) * V_DIM) for hh in range(2)]
        ks = [k_ref[0], k_ref[1]]
        vs = [v_ref[0], v_ref[1]]

        def tile(i, carry, diagonal):
            sl = pl.ds(pl.multiple_of(i * t, t), t)
            dout_all, o_all = do_ref[sl, :], o_ref[sl, :]
            out = []
            for hh in range(2):
                dk, dv = carry[2 * hh], carry[2 * hh + 1]
                qh = q_ref[hh, sl, :]
                dout = jnp.where(heads[hh], dout_all, 0.0)
                sc = _mm_nt(qh, ks[hh])
                if diagonal:
                    sc = jnp.where(below, sc, -1e30)
                p = jnp.exp(sc - lse_ref[hh, sl, 0:1])
                dp = _mm_nt(dout, vs[hh])
                ds = p * (dp - jnp.sum(dout * o_all, axis=1, keepdims=True))
                dq_ref[hh, sl, :] += _mm(ds, ks[hh])
                out += [dk + _mm_tn(ds, qh), dv + _mm_tn(p, dout)]
            return tuple(out)

        carry = tile(j, (jnp.zeros((t, w), F32),) * 4, True)
        carry = lax.fori_loop(j + 1, nq, lambda i, c: tile(i, c, False), carry)
        for hh in range(2):
            dk_ref[hh] = carry[2 * hh].astype(MXU)
            dv_ref[hh] = jnp.where(heads[hh], carry[2 * hh + 1], 0.0).astype(MXU)
        if steps:
            pl.when(step == n_steps - 1)(steps[2])

    s_in, s_out, s_shape, s_sems, s_ops = _side_args(side)
    res = _pcall(
        body, name="mla_flash_bwd" + ("_x" if side else ""), grid=(nh // 2, nq),
        in_specs=[pl.BlockSpec((2, s, w), lambda p, j: (p, 0, 0)),
                  pl.BlockSpec((2, t, w), lambda p, j: (p, j, 0)),
                  pl.BlockSpec((2, t, w), lambda p, j: (p, j, 0)),
                  pl.BlockSpec((s, w), lambda p, j: (0, p)),
                  pl.BlockSpec((2, s, w), lambda p, j: (p, 0, 0)),
                  pl.BlockSpec((s, w), lambda p, j: (0, p))] + s_in,
        out_specs=[pl.BlockSpec((2, s, w), lambda p, j: (p, 0, 0)),
                   pl.BlockSpec((2, t, w), lambda p, j: (p, j, 0)),
                   pl.BlockSpec((2, t, w), lambda p, j: (p, j, 0))] + s_out,
        out_shape=[jax.ShapeDtypeStruct((nh, s, w), F32)] + [jax.ShapeDtypeStruct((nh, s, w), MXU)] * 2 + s_shape,
        scratch_shapes=s_sems,
        compiler_params=pltpu.CompilerParams(dimension_semantics=("arbitrary", "arbitrary"),
                                             vmem_limit_bytes=VMEM_LIMIT),
    )(q, k, v, o, lse, do, *s_ops)
    return res[0], res[1], res[2], res[3:]


def _scan(src, w_re, w_im, a_re, a_im, reverse):
    s = src.shape[0]
    nb, w = a_re.shape[0], LANES
    ch = s // 8
    assert ch & (ch - 1) == 0
    grp = 4
    tr = min(512, s)

    def cmul(ar, ai, xr, xi):
        return ar * xr - ai * xi, ar * xi + ai * xr

    def body(src_ref, wr_ref, wi_ref, ar_ref, ai_ref, xr_out, xi_out, xr_ref, xi_ref):
        def project(c, carry):
            rows = pl.ds(pl.multiple_of(c * tr, tr), tr)
            u = src_ref[rows, :]
            if reverse:
                br, bi = _mm_nt(u, wr_ref[...]), _mm_nt(u, wi_ref[...])
            else:
                br, bi = _mm(u, wr_ref[...]), _mm(u, wi_ref[...])
            for g in range(grp):
                xr_ref[g, rows, :] = _lanes(br, g, w)
                xi_ref[g, rows, :] = _lanes(bi, g, w)
            return carry

        lax.fori_loop(0, s // tr, project, 0)
        sub = lax.broadcasted_iota(jnp.int32, (8, w), 0)

        def shift(x, k):
            if reverse:
                return jnp.where(sub < 8 - k, pltpu.roll(x, 8 - k, 0), 0.0)
            return jnp.where(sub >= k, pltpu.roll(x, k, 0), 0.0)

        ar = [jnp.broadcast_to(ar_ref[g], (8, w)) for g in range(grp)]
        ai = [jnp.broadcast_to(ai_ref[g], (8, w)) for g in range(grp)]

        def tsl(i):
            return pl.ds(pl.multiple_of(((ch - 1 - i) if reverse else i) * 8, 8), 8)

        def local(i, carry):
            out = []
            for g in range(grp):
                xr, xi = carry[2 * g], carry[2 * g + 1]
                pr, pi = cmul(ar[g], ai[g], xr, xi)
                nr = pr + xr_ref[g, tsl(i), :]
                ni = pi + xi_ref[g, tsl(i), :]
                xr_ref[g, tsl(i), :] = nr
                xi_ref[g, tsl(i), :] = ni
                out += [nr, ni]
            return tuple(out)

        fin = lax.fori_loop(0, ch, local, (jnp.zeros((8, w), F32),) * (2 * grp))

        carry_in = []
        for g in range(grp):
            pr, pi = ar[g], ai[g]
            for _ in range(ch.bit_length() - 1):
                pr, pi = cmul(pr, pi, pr, pi)
            fr, fi = fin[2 * g], fin[2 * g + 1]
            for kk in (1, 2, 4):
                sr, si = cmul(pr, pi, shift(fr, kk), shift(fi, kk))
                fr, fi = fr + sr, fi + si
                pr, pi = cmul(pr, pi, pr, pi)
            carry_in += [shift(fr, 1), shift(fi, 1)]

        def fix(ii, pw):
            i0, i1 = 2 * ii, 2 * ii + 1
            blk16 = pl.ds(pl.multiple_of(((ch // 2 - 1 - ii) if reverse else ii) * 16, 16), 16)
            out = []
            for g in range(grp):
                p0r, p0i = pw[2 * g], pw[2 * g + 1]
                p1r, p1i = cmul(p0r, p0i, ar[g], ai[g])
                c0r, c0i = cmul(p0r, p0i, carry_in[2 * g], carry_in[2 * g + 1])
                c1r, c1i = cmul(p1r, p1i, carry_in[2 * g], carry_in[2 * g + 1])
                v0r, v0i = xr_ref[g, tsl(i0), :] + c0r, xi_ref[g, tsl(i0), :] + c0i
                v1r, v1i = xr_ref[g, tsl(i1), :] + c1r, xi_ref[g, tsl(i1), :] + c1i
                pair_r, pair_i = ([v1r, v0r], [v1i, v0i]) if reverse else ([v0r, v1r], [v0i, v1i])
                xr_out[g, blk16, :] = jnp.concatenate(pair_r, axis=0).astype(xr_out.dtype)
                xi_out[g, blk16, :] = jnp.concatenate(pair_i, axis=0).astype(xi_out.dtype)
                nr, ni = cmul(p1r, p1i, ar[g], ai[g])
                out += [nr, ni]
            return tuple(out)

        lax.fori_loop(0, ch // 2, fix, tuple(x for g in range(grp) for x in (ar[g], ai[g])))

    per_j = 4 // grp
    blk = pl.BlockSpec((grp, s, w), lambda i: (i, 0, 0))
    ablk = pl.BlockSpec((grp, 1, w), lambda i: (i, 0, 0))
    sblk = pl.BlockSpec((s, w), lambda i: (0, i // per_j))
    if reverse:
        wblk = pl.BlockSpec((None, grp * w, w), lambda i: (i // per_j, i % per_j, 0))
    else:
        wblk = pl.BlockSpec((None, w, grp * w), lambda i: (i // per_j, 0, i % per_j))
    return _pcall(
        body, name="s5_scan_rev" if reverse else "s5_scan", grid=(nb // grp,),
        in_specs=[sblk, wblk, wblk, ablk, ablk], out_specs=[blk, blk],
        out_shape=[jax.ShapeDtypeStruct((nb, s, w), MXU)] * 2,
        scratch_shapes=[pltpu.VMEM((grp, s, w), F32)] * 2,
        compiler_params=pltpu.CompilerParams(dimension_semantics=("arbitrary",), vmem_limit_bytes=VMEM_LIMIT),
    )(src, w_re, w_im, a_re, a_im)


class _Exchange(NamedTuple):
    ins: list
    out_shapes: list
    sem_shapes: list
    steps: Callable


def _gather_steps(ins, outs, sems):
    n = len(ins)
    send_sems, recv_sems, local_sems = sems
    x, y, c = lax.axis_index("x"), lax.axis_index("y"), lax.axis_index("c")
    me, sibling = (x, y, c), (x, y, 1 - c)
    chips = [(1 - x, y), (x, 1 - y), (1 - x, 1 - y)]

    def copy(a, k, block, to, src=None):
        dst = outs[a].at[4 * block[0] + 2 * block[1] + block[2]]
        return pltpu.make_async_remote_copy(
            src_ref=dst if src is None else src, dst_ref=dst,
            send_sem=send_sems.at[a, k], recv_sem=recv_sems.at[a, k], device_id=to, device_id_type=MESH)

    mine = [pltpu.make_async_copy(ins[a], outs[a].at[4 * x + 2 * y + c], local_sems.at[a]) for a in range(n)]
    first = []
    for a in range(n):
        first.append(copy(a, 0, me, sibling, src=ins[a]))
        first += [copy(a, 1 + j, me, (*chip, c), src=ins[a]) for j, chip in enumerate(chips)]
    passed = [copy(a, 4 + j, (*chip, c), sibling) for j, chip in enumerate(chips) for a in range(n)]

    def start():
        for cp in mine + first:
            cp.start()

    def pass_on():
        i = 0
        for j, chip in enumerate(chips):
            for a in range(n):
                copy(a, 1 + j, (*chip, c), me).wait_recv()
                passed[i].start()
                i += 1

    def finish():
        for a in range(n):
            copy(a, 0, sibling, me).wait_recv()
            for j, chip in enumerate(chips):
                copy(a, 4 + j, (*chip, 1 - c), me).wait_recv()
        for cp in first + passed:
            cp.wait_send()
        for cp in mine:
            cp.wait()

    return start, pass_on, finish


def _gather(arrs):
    n = len(arrs)
    return _Exchange(list(arrs), [jax.ShapeDtypeStruct((N_DEV,) + a.shape, a.dtype) for a in arrs],
                     [pltpu.SemaphoreType.DMA((n, 7)), pltpu.SemaphoreType.DMA((n, 7)), pltpu.SemaphoreType.DMA((n,))],
                     _gather_steps)


def _scatter_steps(ins, outs, sems):
    n = len(ins)
    send_sems, recv_sems, local_sems = sems
    x, y, c = lax.axis_index("x"), lax.axis_index("y"), lax.axis_index("c")
    me = 4 * x + 2 * y + c
    own, sent, arrivals = [], [], []
    for a in range(n):
        own.append(pltpu.make_async_copy(ins[a].at[me], outs[a].at[me], local_sems.at[a]))
        for k in range(1, N_DEV):
            px, py, pc = x ^ ((k >> 2) & 1), y ^ ((k >> 1) & 1), c ^ (k & 1)
            peer = 4 * px + 2 * py + pc
            sent.append(pltpu.make_async_remote_copy(
                src_ref=ins[a].at[peer], dst_ref=outs[a].at[me],
                send_sem=send_sems.at[a, k - 1], recv_sem=recv_sems.at[a, k - 1],
                device_id=(px, py, pc), device_id_type=MESH))
            arrivals.append(pltpu.make_async_remote_copy(
                src_ref=ins[a].at[me], dst_ref=outs[a].at[peer],
                send_sem=send_sems.at[a, k - 1], recv_sem=recv_sems.at[a, k - 1],
                device_id=(x, y, c), device_id_type=MESH))

    def start():
        for cp in own + sent:
            cp.start()

    def pass_on():
        pass

    def finish():
        for cp in arrivals:
            cp.wait_recv()
        for cp in sent:
            cp.wait_send()
        for cp in own:
            cp.wait()

    return start, pass_on, finish


def _scatter(grads):
    n = len(grads)
    return _Exchange(list(grads), [jax.ShapeDtypeStruct(g.shape, g.dtype) for g in grads],
                     [pltpu.SemaphoreType.DMA((n, N_DEV - 1)), pltpu.SemaphoreType.DMA((n, N_DEV - 1)),
                      pltpu.SemaphoreType.DMA((n,))], _scatter_steps)


def _together(a, b):
    def steps(ins, outs, sems):
        sa = a.steps(ins[:len(a.ins)], outs[:len(a.out_shapes)], sems[:len(a.sem_shapes)])
        sb = b.steps(ins[len(a.ins):], outs[len(a.out_shapes):], sems[len(a.sem_shapes):])

        def both(k):
            def run():
                sa[k]()
                sb[k]()
            return run
        return both(0), both(1), both(2)

    return _Exchange(a.ins + b.ins, a.out_shapes + b.out_shapes, a.sem_shapes + b.sem_shapes, steps)


def _run_exchange(name, ex):
    n_in, n_out = len(ex.ins), len(ex.out_shapes)

    def body(*refs):
        for step in ex.steps(refs[:n_in], refs[n_in:n_in + n_out], refs[n_in + n_out:]):
            step()

    any_spec = pl.BlockSpec(memory_space=pl.ANY)
    return _pcall(body, name=name, in_specs=[any_spec] * n_in, out_specs=[any_spec] * n_out,
                  out_shape=list(ex.out_shapes), scratch_shapes=list(ex.sem_shapes))(*ex.ins)


def _adam_math(g, w_, m_, v_):
    m_new = ADAM_B1 * m_ + (1.0 - ADAM_B1) * g
    v_new = ADAM_B2 * v_ + (1.0 - ADAM_B2) * (g * g)
    m_hat = m_new / (1.0 - ADAM_B1 ** ADAM_STEP)
    v_hat = v_new / (1.0 - ADAM_B2 ** ADAM_STEP)
    delta = -ADAM_LR * (m_hat / (jnp.sqrt(v_hat) + ADAM_EPS) + ADAM_WD * w_)
    return delta, m_new, v_new


def _adamw_weight(name, parts, w, m, v):
    nl = len(parts)

    def body(*refs):
        p_refs = refs[:nl]
        w_ref, m_ref, v_ref, g_ref, d_ref, mo_ref, vo_ref = refs[nl:]
        for l in range(nl):
            g = p_refs[l][0].astype(F32)
            for j in range(1, N_DEV):
                g = g + p_refs[l][j].astype(F32)
            g_ref[l] = g
            d_ref[l], mo_ref[l], vo_ref[l] = _adam_math(g, w_ref[l], m_ref[l], v_ref[l])

    return _pcall(
        body, name=name, out_shape=[jax.ShapeDtypeStruct(w.shape, F32)] * 4,
        compiler_params=pltpu.CompilerParams(vmem_limit_bytes=VMEM_LIMIT),
    )(*parts, w, m, v)

def _sum_sources(name, parts):
    r = parts.shape[1]

    def body(p_ref, g_ref):
        g = p_ref[0]
        for j in range(1, N_DEV):
            g = g + p_ref[j]
        g_ref[...] = g

    return _pcall(body, name=name, out_shape=jax.ShapeDtypeStruct((r, LANES), F32),
                  compiler_params=pltpu.CompilerParams(vmem_limit_bytes=VMEM_LIMIT))(parts)


def _adamw_small(name, g, w, m, v):
    n = len(g)

    def body(*refs):
        g_r, w_r, m_r, v_r = (refs[k * n:(k + 1) * n] for k in range(4))
        d_r, mo_r, vo_r = (refs[k * n:(k + 1) * n] for k in range(4, 7))
        for i in range(n):
            d_r[i][...], mo_r[i][...], vo_r[i][...] = _adam_math(g_r[i][...], w_r[i][...], m_r[i][...], v_r[i][...])

    res = _pcall(body, name=name, out_shape=[jax.ShapeDtypeStruct(a.shape, F32) for a in w] * 3,
                 compiler_params=pltpu.CompilerParams(vmem_limit_bytes=VMEM_LIMIT))(*g, *w, *m, *v)
    return res[:n], res[n:2 * n], res[2 * n:]


def _pack(arrs):
    flat = jnp.concatenate([a.reshape(-1) for a in arrs])
    flat = jnp.pad(flat, (0, (-flat.shape[0]) % (8 * LANES)))
    return flat.reshape(-1, LANES)


def _unpack(packed, shapes):
    flat = packed.reshape(-1)
    out, off = [], 0
    for shp in shapes:
        size = 1
        for d in shp:
            size *= d
        out.append(flat[off:off + size].reshape(shp))
        off += size
    return out


def _s5_params(lam_re, lam_im, log_dt, b_re, b_im, c_re, c_im):
    dt = jnp.exp(log_dt)[:, None]
    e = jnp.exp(lam_re * dt)
    ang = lam_im * dt
    a_re, a_im = e * jnp.cos(ang), e * jnp.sin(ang)
    nr, ni = a_re - 1.0, a_im
    den = lam_re * lam_re + lam_im * lam_im
    cr = ((nr * lam_re + ni * lam_im) / den)[..., None]
    ci = ((ni * lam_re - nr * lam_im) / den)[..., None]
    bb_re = cr * b_re - ci * b_im
    bb_im = cr * b_im + ci * b_re
    eye = jnp.eye(8, dtype=F32)[None, :, None, :, None]

    def bblk(bb):
        t = jnp.transpose(bb.reshape(4, 8, SSM_STATE, SSM_GROUP), (0, 3, 1, 2))
        return (eye * t[:, None]).reshape(4, 8 * SSM_GROUP, 8 * SSM_STATE)

    def cblk(cc):
        t = jnp.transpose(cc.reshape(4, 8, SSM_GROUP, SSM_STATE), (0, 3, 1, 2))
        return (eye * t[:, None]).reshape(4, 8 * SSM_STATE, 8 * SSM_GROUP)

    nb = SSM_GROUPS * SSM_STATE // LANES
    return (a_re.reshape(nb, 1, LANES), a_im.reshape(nb, 1, LANES), bblk(bb_re), bblk(bb_im),
            cblk(c_re), -cblk(c_im))


def _cat_blocks(x3, j):
    return jnp.concatenate([x3[4 * j + k] for k in range(4)], axis=-1)


def _to_chunks(a):
    s, c = a.shape
    return a.reshape(8, s // 8, c).transpose(1, 0, 2).reshape(s, c)


def _from_chunks(a):
    s, c = a.shape
    return a.reshape(s // 8, 8, c).transpose(1, 0, 2).reshape(s, c)


EARLY = ['w_in', 'w_uq', 'w_ukv']

FWD_PLAN = {
    'flash': ('late', ['ssm_w_glu', 'w_out', 'w_xq', 'w_xkv', 'w_xo', 'w_gate', 'w_up', 'w_down']),
    'ffn': ('nxt', EARLY),
}
BWD_PLAN = {
    'ffn_bwd': ('nxt', EARLY),
    'xattn_bwd': ('own', ['w_down']),
    'flash_bwd': ('own', ['w_gate', 'w_up', 'w_xq', 'w_xkv', 'w_xo']),
    'mla_qkv_bwd': ('own', ['ssm_w_glu', 'w_out']),
}


def _named(names, d):
    return [d[n] for n in names]


def _layer_fwd(h, memx, tabs, wl, pl_, late=None, nxt=None):
    s = h.shape[0]
    tm = min(ROW_TILE, s)
    cos, sin_lo, sin_hi = tabs
    sv = {}
    wl = dict(wl)
    nxt_got = {}

    def fetch(host):
        who, names = FWD_PLAN.get(host, (None, []))
        src = late if who == 'late' else nxt if who == 'nxt' else None
        return _gather(_named(names, src)) if src else None

    def landed(host, got):
        who, names = FWD_PLAN.get(host, (None, []))
        if got and who == 'late':
            wl.update(_layer_weights(dict(zip(names, got))))
        elif got:
            nxt_got.update(zip(names, got))

    def f_mix_in(h_, g, w):
        xn, _ = _rms(h_, g[...])
        pr = _mm(xn, w[...])
        return pr[:, 0:512], pr[:, 512:1024]
    proj, u_nat = _rows("mix_in", f_mix_in, s, tm, [(h, 'r0'), (pl_['norm_mix_g'], 'f'), (wl['w_in'], 'f')],
                        [((s, 512), F32, 'r0')] * 2)

    def f_qkv(pr, cos_, slo, shi, gq, gkv, wq, wk, wv):
        cqn = _rms(pr[:, 0:Q_LORA], gq[...])[0].astype(MXU)
        kvn = _rms(pr[:, Q_LORA:Q_LORA + KV_LORA], gkv[...])[0].astype(MXU)
        krr = _rope(pr[:, 384:512], cos_, slo, shi)
        qs, ks, vs = [], [], []
        for hd in range(MLA_HEADS):
            qs.append(_rope(_mm(cqn, wq[hd]), cos_, slo, shi) * MLA_SCALE)
            ks.append(_mm(kvn, wk[hd]) + krr)
            vs.append(_mm(kvn, wv[hd]))
        return jnp.stack(qs), jnp.stack(ks), jnp.stack(vs)
    hshape = (MLA_HEADS, s, HEAD_W)
    (q, k, v), got = _hosted(_rows(
        "mla_qkv", f_qkv, s, tm,
        [(proj, 'r0'), (cos, 'r0'), (sin_lo, 'r0'), (sin_hi, 'r0'), (pl_['q_norm_g'], 'f'), (pl_['kv_norm_g'], 'f'),
         (wl['w_uq'], 'f'), (wl['w_k'], 'f'), (wl['w_v'], 'f')],
        [(hshape, MXU, 'r1')] * 3, fetch('mla_qkv')))
    landed('mla_qkv', got)

    a_out, lse, got = _flash_fwd(q, k, v, fetch('flash'))
    landed('flash', got)

    u_ch = _to_chunks(u_nat)

    x_re, x_im = _scan(u_ch, pl_['b_re'], pl_['b_im'], pl_['a_re'], pl_['a_im'], False)

    def f_s5_out(xr, xi, u, cre, cim, d, wglu, bglu):
        y = jnp.concatenate([_mm(_cat_blocks(xr, j), cre[j]) + _mm(_cat_blocks(xi, j), cim[j])
                             for j in range(4)], axis=-1) + d[...] * u
        z = _mm(jax.nn.gelu(y), wglu[...]) + bglu[...]
        return y, y * jax.nn.sigmoid(z)
    (y_ssm, s_out_ch), got = _hosted(_rows(
        "s5_out", f_s5_out, s, tm,
        [(x_re, 'r1'), (x_im, 'r1'), (u_ch, 'r0'), (pl_['c_re'], 'f'), (pl_['c_im'], 'f'),
         (pl_['ssm_d'], 'f'), (wl['ssm_w_glu'], 'f'), (pl_['ssm_b_glu'], 'f')],
        [((s, SSM_WIDTH), F32, 'r0')] * 2, fetch('s5_out')))
    landed('s5_out', got)
    s_out = _from_chunks(s_out_ch)

    def f_mix_out(h_, a, so, ga, gs, w):
        an = _rms(a, ga[...])[0]
        sn = _rms(so, gs[...])[0]
        return (h_ + _mm(jnp.concatenate([an, sn], axis=-1), w[...]),)
    (h1,), got = _hosted(_rows("mix_out", f_mix_out, s, tm,
                               [(h, 'r0'), (a_out, 'r0'), (s_out, 'r0'), (pl_['attn_out_g'], 'f'),
                                (pl_['ssm_out_g'], 'f'), (wl['w_out'], 'f')],
                               [((s, D_MODEL), F32, 'r0')], fetch('mix_out')))
    landed('mix_out', got)

    m_len = memx.shape[0]

    def f_memkv(mm_, g, w):
        mn = _rms(mm_, g[...])[0].astype(MXU)
        return (jnp.stack([_mm(mn, w[d]) for d in range(N_DEV)]),)
    kvm, = _rows("mem_kv", f_memkv, m_len, m_len, [(memx, 'r0'), (pl_['mem_norm_g'], 'f'), (wl['w_xkv'], 'f')],
                 [((N_DEV, m_len, X_HEAD_DIM), MXU, 'r1')])

    def f_xattn(h_, g, wq, kv_, wo):
        hn = _rms(h_, g[...])[0].astype(MXU)
        q_all = _mm(hn, wq[...]).astype(MXU)
        outs = []
        for hd in range(X_HEADS):
            p = _softmax(_mm_nt(_lanes(q_all, hd, X_HEAD_DIM), kv_[hd]) * X_SCALE)
            outs.append(_mm(p, kv_[X_HEADS + hd]).astype(MXU))
        return (h_ + _mm(jnp.concatenate(outs, axis=-1), wo[...]),)
    (h2,), got = _hosted(_rows("xattn", f_xattn, s, min(X_ROWS, s),
                               [(h1, 'r0'), (pl_['norm_x_g'], 'f'), (wl['w_xq'], 'f'), (kvm, 'f'), (wl['w_xo'], 'f')],
                               [((s, D_MODEL), F32, 'r0')], fetch('xattn')))
    landed('xattn', got)

    def f_ffn(h_, g, wg, wu, wd):
        hn = _rms(h_, g[...])[0].astype(MXU)
        y = jnp.zeros(h_.shape, F32)
        gates, ups = [], []
        for c in range(D_FF // FF_FWD_CHUNK):
            cs = pl.ds(c * FF_FWD_CHUNK, FF_FWD_CHUNK)
            gate, up = _mm(hn, wg[:, cs]), _mm(hn, wu[:, cs])
            y = y + _mm(gate * jax.nn.sigmoid(gate) * up, wd[cs, :])
            gates.append(gate)
            ups.append(up)
        return h_ + y, jnp.concatenate(gates, axis=-1), jnp.concatenate(ups, axis=-1)
    (h3, gate_f, up_f), got = _hosted(_rows(
        "ffn", f_ffn, s, min(FFN_ROWS, s),
        [(h2, 'r0'), (pl_['norm_ffn_g'], 'f'), (wl['w_gate'], 'f'), (wl['w_up'], 'f'), (wl['w_down'], 'f')],
        [((s, D_MODEL), F32, 'r0'), ((s, D_FF), MXU, 'r0'), ((s, D_FF), MXU, 'r0')], fetch('ffn')))
    landed('ffn', got)
    sv.update(h=h, proj=proj, q=q, k=k, v=v, a_out=a_out, lse=lse, x_re=x_re, x_im=x_im, y_ssm=y_ssm,
              s_out=s_out, h1=h1, kvm=kvm, h2=h2, u_ch=u_ch, gate=gate_f, up=up_f)
    return h3, sv, wl, nxt_got


def _layer_bwd(dh3, sv, memx, tabs, wl, pl_, nxt=None):
    s = dh3.shape[0]
    tm = min(ROW_TILE, s)
    cos, sin_lo, sin_hi = tabs
    gr = {}
    arrived = {}
    act_shape = (s, D_FF)

    def send(host):
        who, names = BWD_PLAN.get(host, (None, []))
        if who is None or (who == 'nxt' and not nxt):
            return None, []
        return (_scatter([nxt[n] if who == 'nxt' else _blocked(gr, n) for n in names]),
                [(who, n) for n in names])

    def f_ffn_bwd(h_, dy, gate_, up_, g, wg, wu, wd):
        hn, r = _rms(h_, g[...])
        hb = hn.astype(MXU)
        dyb = dy.astype(MXU)
        dhn = jnp.zeros(h_.shape, F32)
        acts, dgs, dus = [], [], []
        for c in range(D_FF // FF_CHUNK):
            cs = pl.ds(c * FF_CHUNK, FF_CHUNK)
            gate = _lanes(gate_, c, FF_CHUNK).astype(F32)
            up = _lanes(up_, c, FF_CHUNK).astype(F32)
            sg = jax.nn.sigmoid(gate)
            si = gate * sg
            dact = _mm_nt(dyb, wd[cs, :])
            dgate = (dact * up * (sg * (1.0 + gate * (1.0 - sg)))).astype(MXU)
            dup = (dact * si).astype(MXU)
            dhn = dhn + _mm_nt(dgate, wg[:, cs]) + _mm_nt(dup, wu[:, cs])
            acts.append((si * up).astype(MXU))
            dgs.append(dgate)
            dus.append(dup)
        dh, dg = _rms_bwd(h_, g[...], r, dhn)
        cat = lambda parts: jnp.concatenate(parts, axis=-1)
        return dy + dh, hb, cat(acts), cat(dgs), cat(dus), dg
    ex, keys = send('ffn_bwd')
    (dh2, hn_f, act, dgate, dup, gr['norm_ffn_g']), got = _hosted(_rows(
        "ffn_bwd", f_ffn_bwd, s, min(FFN_ROWS, s),
        [(sv['h2'], 'r0'), (dh3, 'r0'), (sv['gate'], 'r0'), (sv['up'], 'r0'), (pl_['norm_ffn_g'], 'f'),
         (wl['w_gate'], 'f'), (wl['w_up'], 'f'), (wl['w_down'], 'f')],
        [((s, D_MODEL), F32, 'r0'), ((s, D_MODEL), MXU, 'r0'), (act_shape, MXU, 'r0'), (act_shape, MXU, 'r0'),
         (act_shape, MXU, 'r0'), ((1, D_MODEL), F32, 'a')], ex))
    arrived.update(zip(keys, got))
    gr['w_gate'] = _mm_tn_call("dw_gate", hn_f, dgate, tn=FF_CHUNK)
    gr['w_up'] = _mm_tn_call("dw_up", hn_f, dup, tn=FF_CHUNK)
    gr['w_down'] = _mm_tn_call("dw_down", act, dh3, tk=FF_CHUNK)

    m_len = memx.shape[0]

    def f_xattn_bwd(h_, dy, g, wq, kv_, wo):
        hn, r = _rms(h_, g[...])
        hb = hn.astype(MXU)
        q_all = _mm(hb, wq[...]).astype(MXU)
        do_all = _mm_nt(dy, wo[...]).astype(MXU)
        dqs, ohs, dks, dvs = [], [], [], []
        for hd in range(X_HEADS):
            kh, vh = kv_[hd], kv_[X_HEADS + hd]
            qh, do = _lanes(q_all, hd, X_HEAD_DIM), _lanes(do_all, hd, X_HEAD_DIM)
            p = _softmax(_mm_nt(qh, kh) * X_SCALE)
            ohs.append(_mm(p, vh).astype(MXU))
            dvs.append(_mm_tn(p, do))
            dp = _mm_nt(do, vh)
            ds = p * (dp - jnp.sum(dp * p, axis=-1, keepdims=True)) * X_SCALE
            dqs.append(_mm(ds, kh).astype(MXU))
            dks.append(_mm_tn(ds, qh))
        dq_all = jnp.concatenate(dqs, axis=-1)
        dh, dg = _rms_bwd(h_, g[...], r, _mm_nt(dq_all, wq[...]))
        return dy + dh, hb, dq_all, jnp.concatenate(ohs, axis=-1), jnp.stack(dks + dvs), dg
    ex, keys = send('xattn_bwd')
    (dh1, hn_x, dq_x, oh_x, dkvm, gr['norm_x_g']), got = _hosted(_rows(
        "xattn_bwd", f_xattn_bwd, s, min(X_ROWS, s),
        [(sv['h1'], 'r0'), (dh2, 'r0'), (pl_['norm_x_g'], 'f'), (wl['w_xq'], 'f'), (sv['kvm'], 'f'),
         (wl['w_xo'], 'f')],
        [((s, D_MODEL), F32, 'r0'), ((s, D_MODEL), MXU, 'r0'), ((s, D_MODEL), MXU, 'r0'),
         ((s, D_MODEL), MXU, 'r0'), ((N_DEV, m_len, X_HEAD_DIM), F32, 'a'), ((1, D_MODEL), F32, 'a')], ex))
    arrived.update(zip(keys, got))
    gr['w_xq'] = _mm_tn_call("dw_xq", hn_x, dq_x)
    gr['w_xo'] = _mm_tn_call("dw_xo", oh_x, dh2)

    def f_memkv_bwd(mm_, dkv, g, w):
        mn, r = _rms(mm_, g[...])
        mb = mn.astype(MXU)
        dmn = jnp.zeros(mm_.shape, F32)
        dws = []
        for d in range(N_DEV):
            dmn = dmn + _mm_nt(dkv[d], w[d])
            dws.append(_mm_tn(mb, dkv[d]))
        _, dg = _rms_bwd(mm_, g[...], r, dmn)
        return jnp.stack(dws), dg
    gr['w_xkv'], gr['mem_norm_g'] = _rows(
        "mem_kv_bwd", f_memkv_bwd, m_len, m_len,
        [(memx, 'r0'), (dkvm, 'r1'), (pl_['mem_norm_g'], 'f'), (wl['w_xkv'], 'f')],
        [((N_DEV, D_MODEL, X_HEAD_DIM), F32, 'a'), ((1, D_MODEL), F32, 'a')])

    def f_mix_out_bwd(a, so, dy, ga, gs, w):
        dmix = _mm_nt(dy, w[...])
        an, ra = _rms(a, ga[...])
        sn, rs = _rms(so, gs[...])
        da, dga = _rms_bwd(a, ga[...], ra, dmix[:, 0:512])
        dso, dgs = _rms_bwd(so, gs[...], rs, dmix[:, 512:1024])
        return da, dso, jnp.concatenate([an, sn], axis=-1), dga, dgs
    da_out, ds_out, mixed, gr['attn_out_g'], gr['ssm_out_g'] = _rows(
        "mix_out_bwd", f_mix_out_bwd, s, tm,
        [(sv['a_out'], 'r0'), (sv['s_out'], 'r0'), (dh1, 'r0'), (pl_['attn_out_g'], 'f'), (pl_['ssm_out_g'], 'f'),
         (wl['w_out'], 'f')],
        [((s, 512), F32, 'r0'), ((s, 512), F32, 'r0'), ((s, D_MODEL), MXU, 'r0'), ((1, 512), F32, 'a'),
         ((1, 512), F32, 'a')])
    gr['w_out'] = _mm_tn_call("dw_out", mixed, dh1)

    ex, keys = send('flash_bwd')
    dq, dk, dv, got = _flash_bwd(sv['q'], sv['k'], sv['v'], sv['a_out'], sv['lse'], da_out, ex)
    arrived.update(zip(keys, got))

    def f_s5_out_bwd(xr, xi, u, y, ds, cre, cim, d, wglu, bglu):
        g, gelu_vjp = jax.vjp(jax.nn.gelu, y)
        sig = jax.nn.sigmoid(_mm(g, wglu[...]) + bglu[...])
        dz = ds * y * sig * (1.0 - sig)
        dy = ds * sig + gelu_vjp(_mm_nt(dz, wglu[...]))[0]
        dcr, dci = [], []
        for j in range(4):
            dyj = _lanes(dy, j, LANES)
            dcr.append(_mm_tn(_cat_blocks(xr, j), dyj))
            dci.append(_mm_tn(_cat_blocks(xi, j), dyj))
        return (dy, dy * d[...], jnp.stack(dcr), jnp.stack(dci),
                jnp.sum(dy * u, axis=0, keepdims=True), _mm_tn(g, dz), jnp.sum(dz, axis=0, keepdims=True))
    ex, keys = send('s5_out_bwd')
    (dy_ssm, du_dir, gr['c_re'], gr['c_im'], gr['ssm_d'], gr['ssm_w_glu'], gr['ssm_b_glu']), got = _hosted(_rows(
        "s5_out_bwd", f_s5_out_bwd, s, tm,
        [(sv['x_re'], 'r1'), (sv['x_im'], 'r1'), (sv['u_ch'], 'r0'), (sv['y_ssm'], 'r0'), (_to_chunks(ds_out), 'r0'),
         (pl_['c_re'], 'f'), (pl_['c_im'], 'f'), (pl_['ssm_d'], 'f'), (wl['ssm_w_glu'], 'f'),
         (pl_['ssm_b_glu'], 'f')],
        [((s, 512), F32, 'r0'), ((s, 512), F32, 'r0'), ((4, 512, LANES), F32, 'a'),
         ((4, 512, LANES), F32, 'a'), ((1, 512), F32, 'a'), ((512, 512), F32, 'a'), ((1, 512), F32, 'a')], ex))
    arrived.update(zip(keys, got))
    g_re, g_im = _scan(dy_ssm, pl_['c_re'], pl_['c_im'], pl_['a_re'], -pl_['a_im'], True)
    first_re = jnp.pad(sv['x_re'][:, s - 8:s - 1].astype(F32), ((0, 0), (1, 0), (0, 0)))
    first_im = jnp.pad(sv['x_im'][:, s - 8:s - 1].astype(F32), ((0, 0), (1, 0), (0, 0)))

    def f_s5_in_bwd(gre, gim, xr, xi, pr16, pi16, u, dud, f8r, f8i, bre, bim):
        first = pl.program_id(0) == 0
        xr32, xi32 = xr.astype(F32), xi.astype(F32)
        xpr = jnp.concatenate([jnp.where(first, f8r[...], pr16.astype(F32)[:, 8:16]), xr32[:, :tm - 8]], axis=1)
        xpi = jnp.concatenate([jnp.where(first, f8i[...], pi16.astype(F32)[:, 8:16]), xi32[:, :tm - 8]], axis=1)
        gre32, gim32 = gre.astype(F32), gim.astype(F32)
        dus, dbr, dbi = [], [], []
        for j in range(4):
            gj_r, gj_i, uj = _cat_blocks(gre, j), _cat_blocks(gim, j), _lanes(u, j, LANES)
            dus.append(_mm_nt(gj_r, bre[j]) + _mm_nt(gj_i, bim[j]))
            dbr.append(_mm_tn(uj, gj_r))
            dbi.append(_mm_tn(uj, gj_i))
        da_r = jnp.sum(gre32 * xpr + gim32 * xpi, axis=1, keepdims=True)
        da_i = jnp.sum(gim32 * xpr - gre32 * xpi, axis=1, keepdims=True)
        return dud + jnp.concatenate(dus, axis=-1), jnp.stack(dbr), jnp.stack(dbi), da_r, da_i
    ex, keys = send('s5_in_bwd')
    (du_ch, gr['b_re'], gr['b_im'], gr['a_re'], gr['a_im']), got = _hosted(_rows(
        "s5_in_bwd", f_s5_in_bwd, s, tm,
        [(g_re, 'r1'), (g_im, 'r1'), (sv['x_re'], 'r1'), (sv['x_im'], 'r1'), (sv['x_re'], 'p16'), (sv['x_im'], 'p16'),
         (sv['u_ch'], 'r0'), (du_dir, 'r0'), (first_re, 'f'), (first_im, 'f'), (pl_['b_re'], 'f'), (pl_['b_im'], 'f')],
        [((s, 512), F32, 'r0'), ((4, LANES, 512), F32, 'a'), ((4, LANES, 512), F32, 'a'),
         ((16, 1, LANES), F32, 'a'), ((16, 1, LANES), F32, 'a')], ex))
    arrived.update(zip(keys, got))
    du = _from_chunks(du_ch)

    def f_qkv_bwd(pr, cos_, slo, shi, dq_, dk_, dv_, gq, gkv, wq, wk, wv):
        cq, ckv = pr[:, 0:Q_LORA], pr[:, Q_LORA:Q_LORA + KV_LORA]
        cqn, rq = _rms(cq, gq[...])
        kvn, rkv = _rms(ckv, gkv[...])
        cqb, kvb = cqn.astype(MXU), kvn.astype(MXU)
        dcqn = jnp.zeros(cq.shape, F32)
        dkvn = jnp.zeros(ckv.shape, F32)
        dksum = jnp.zeros(dk_[0].shape, F32)
        dwq, dwk, dwv = [], [], []
        for hd in range(MLA_HEADS):
            dqp = (_rope_t(dq_[hd], cos_, slo, shi) * MLA_SCALE).astype(MXU)
            dkb, dvb = dk_[hd].astype(MXU), dv_[hd].astype(MXU)
            dwq.append(_mm_tn(cqb, dqp))
            dwk.append(_mm_tn(kvb, dkb))
            dwv.append(_mm_tn(kvb, dvb))
            dcqn = dcqn + _mm_nt(dqp, wq[hd])
            dkvn = dkvn + _mm_nt(dkb, wk[hd]) + _mm_nt(dvb, wv[hd])
            dksum = dksum + dk_[hd]
        dcq, dgq = _rms_bwd(cq, gq[...], rq, dcqn)
        dckv, dgkv = _rms_bwd(ckv, gkv[...], rkv, dkvn)
        dpa = jnp.concatenate([dcq, dckv, _rope_t(dksum, cos_, slo, shi)], axis=-1)
        return dpa, jnp.stack(dwq), jnp.stack(dwk), jnp.stack(dwv), dgq, dgkv
    ex, keys = send('mla_qkv_bwd')
    (dpa, gr['w_uq'], gr['w_k'], gr['w_v'], gr['q_norm_g'], gr['kv_norm_g']), got = _hosted(_rows(
        "mla_qkv_bwd", f_qkv_bwd, s, tm,
        [(sv['proj'], 'r0'), (cos, 'r0'), (sin_lo, 'r0'), (sin_hi, 'r0'), (dq, 'r1'), (dk, 'r1'), (dv, 'r1'),
         (pl_['q_norm_g'], 'f'), (pl_['kv_norm_g'], 'f'), (wl['w_uq'], 'f'), (wl['w_k'], 'f'), (wl['w_v'], 'f')],
        [((s, 512), F32, 'r0'), ((MLA_HEADS, Q_LORA, HEAD_W), F32, 'a'), ((MLA_HEADS, KV_LORA, HEAD_W), F32, 'a'),
         ((MLA_HEADS, KV_LORA, HEAD_W), F32, 'a'), ((1, Q_LORA), F32, 'a'), ((1, KV_LORA), F32, 'a')], ex))
    arrived.update(zip(keys, got))

    def f_mix_in_bwd(h_, dpa_, du_, dres, g, w):
        dproj = jnp.concatenate([dpa_, du_], axis=-1).astype(MXU)
        xn, r = _rms(h_, g[...])
        dh, dg = _rms_bwd(h_, g[...], r, _mm_nt(dproj, w[...]))
        return dres + dh, xn, dproj, dg
    ex, keys = send('mix_in_bwd')
    (dh0, xn, dproj, gr['norm_mix_g']), got = _hosted(_rows(
        "mix_in_bwd", f_mix_in_bwd, s, tm,
        [(sv['h'], 'r0'), (dpa, 'r0'), (du, 'r0'), (dh1, 'r0'), (pl_['norm_mix_g'], 'f'), (wl['w_in'], 'f')],
        [((s, D_MODEL), F32, 'r0'), ((s, D_MODEL), MXU, 'r0'), ((s, D_MODEL), MXU, 'r0'), ((1, D_MODEL), F32, 'a')],
        ex))
    arrived.update(zip(keys, got))
    gr['w_in'] = _mm_tn_call("dw_in", xn, dproj)
    return dh0, gr, arrived


def _layer_weights(w):
    wl = {}
    if 'w_in' in w:
        w_in = w['w_in'].reshape(D_MODEL, -1)
        z = lambda n: jnp.zeros((D_MODEL, n), w_in.dtype)
        wl['w_in'] = jnp.concatenate([w_in[:, :384], z(64), w_in[:, 384:416], z(32), w_in[:, 416:]], axis=1)
    if 'w_uq' in w:
        wl['w_uq'] = jnp.pad(w['w_uq'], ((0, 0), (0, 0), (0, HEAD_W - QK_NOPE - QK_ROPE)))
    if 'w_ukv' in w:
        wl['w_k'] = jnp.pad(w['w_ukv'][..., :QK_NOPE], ((0, 0), (0, 0), (0, HEAD_W - QK_NOPE)))
        wv = w['w_ukv'][..., QK_NOPE:]
        even = (jnp.arange(MLA_HEADS) % 2 == 0)[:, None, None]
        wl['w_v'] = jnp.concatenate([jnp.where(even, wv, 0), jnp.where(even, 0, wv)], axis=-1).astype(wv.dtype)
    if 'ssm_w_glu' in w:
        wl['ssm_w_glu'] = w['ssm_w_glu'].reshape(SSM_WIDTH, SSM_WIDTH)
    for n in ('w_out', 'w_xq', 'w_xo'):
        if n in w:
            wl[n] = w[n].reshape(D_MODEL, D_MODEL)
    if 'w_xkv' in w:
        wl['w_xkv'] = w['w_xkv']
    for n in ('w_gate', 'w_up'):
        if n in w:
            wl[n] = jnp.transpose(w[n], (1, 0, 2)).reshape(D_MODEL, D_FF)
    if 'w_down' in w:
        wl['w_down'] = w['w_down'].reshape(D_FF, D_MODEL)
    return wl


def _blocked(gr, n):
    if n == 'w_in':
        d = gr['w_in']
        out = jnp.concatenate([d[:, :384], d[:, 448:480], d[:, 512:]], axis=1).reshape(N_DEV, 128, -1)
    elif n == 'w_uq':
        out = gr['w_uq'][..., :QK_NOPE + QK_ROPE]
    elif n == 'w_ukv':
        even = (jnp.arange(MLA_HEADS) % 2 == 0)[:, None, None]
        dv = gr['w_v']
        out = jnp.concatenate([gr['w_k'][..., :QK_NOPE], jnp.where(even, dv[..., :V_DIM], dv[..., V_DIM:])], axis=-1)
    elif n == 'ssm_w_glu':
        out = gr['ssm_w_glu'].reshape(N_DEV, SSM_WIDTH // N_DEV, SSM_WIDTH)
    elif n in ('w_out', 'w_xq', 'w_xo'):
        out = gr[n].reshape(N_DEV, D_MODEL // N_DEV, D_MODEL)
    elif n in ('w_gate', 'w_up'):
        out = jnp.transpose(gr[n].reshape(D_MODEL, N_DEV, D_FF // N_DEV), (1, 0, 2))
    elif n == 'w_down':
        out = gr[n].reshape(N_DEV, D_FF // N_DEV, D_MODEL)
    else:
        out = gr[n]
    return out.astype(MXU)


def kernel(x, mem, positions, norm_mix_g, w_in, q_norm_g, w_uq, kv_norm_g, w_ukv, ssm_lambda_re, ssm_lambda_im, ssm_log_dt, ssm_b_re, ssm_b_im, ssm_c_re, ssm_c_im, ssm_d, ssm_w_glu, ssm_b_glu, attn_out_g, ssm_out_g, w_out, norm_x_g, mem_norm_g, w_xq, w_xkv, w_xo, norm_ffn_g, w_gate, w_up, w_down, final_norm_g, loss_target, m_norm_mix_g, m_w_in, m_q_norm_g, m_w_uq, m_kv_norm_g, m_w_ukv, m_ssm_lambda_re, m_ssm_lambda_im, m_ssm_log_dt, m_ssm_b_re, m_ssm_b_im, m_ssm_c_re, m_ssm_c_im, m_ssm_d, m_ssm_w_glu, m_ssm_b_glu, m_attn_out_g, m_ssm_out_g, m_w_out, m_norm_x_g, m_mem_norm_g, m_w_xq, m_w_xkv, m_w_xo, m_norm_ffn_g, m_w_gate, m_w_up, m_w_down, m_final_norm_g, v_norm_mix_g, v_w_in, v_q_norm_g, v_w_uq, v_kv_norm_g, v_w_ukv, v_ssm_lambda_re, v_ssm_lambda_im, v_ssm_log_dt, v_ssm_b_re, v_ssm_b_im, v_ssm_c_re, v_ssm_c_im, v_ssm_d, v_ssm_w_glu, v_ssm_b_glu, v_attn_out_g, v_ssm_out_g, v_w_out, v_norm_x_g, v_mem_norm_g, v_w_xq, v_w_xkv, v_w_xo, v_norm_ffn_g, v_w_gate, v_w_up, v_w_down, v_final_norm_g):
    args = dict(locals())
    W = {n: args[n] for n in WEIGHTS}
    M = {n: args['m_' + n] for n in WEIGHTS}
    V = {n: args['v_' + n] for n in WEIGHTS}
    s = x.shape[1]
    h = x[0]
    memx = mem[0]

    freqs = ROPE_THETA ** (-jnp.arange(0, QK_ROPE, 2, dtype=F32) / QK_ROPE)
    ang = positions[0].astype(F32)[:, None] * freqs
    c16, s16 = jnp.cos(ang), jnp.sin(ang)
    zeros = lambda n: jnp.zeros((s, n), F32)
    cos = jnp.concatenate([jnp.ones((s, QK_NOPE), F32), c16, c16, zeros(32)], axis=1)
    sin_lo = jnp.concatenate([zeros(QK_NOPE), -s16, zeros(ROT + 32)], axis=1)
    sin_hi = jnp.concatenate([zeros(QK_NOPE + ROT), s16, zeros(32)], axis=1)
    tabs = (cos, sin_lo, sin_hi)

    shards = [{n: W[n][l].astype(MXU) for n in SHARDED} for l in range(DEPTH)]
    gathered = dict(zip(EARLY, _run_exchange("gather_weights", _gather(_named(EARLY, shards[0])))))

    layers = []
    for l in range(DEPTH):
        wl = _layer_weights(gathered)
        s5_in = [W[n][l] for n in ('ssm_lambda_re', 'ssm_lambda_im', 'ssm_log_dt', 'ssm_b_re', 'ssm_b_im',
                                   'ssm_c_re', 'ssm_c_im')]
        (a_re, a_im, bre, bim, cre, cim), s5_vjp = jax.vjp(_s5_params, *s5_in)
        pl_ = {n: W[n][l][None] for n in ('norm_mix_g', 'q_norm_g', 'kv_norm_g', 'ssm_d', 'ssm_b_glu',
                                           'attn_out_g', 'ssm_out_g', 'norm_x_g', 'mem_norm_g', 'norm_ffn_g')}
        pl_.update(a_re=a_re, a_im=a_im, b_re=bre, b_im=bim, c_re=cre, c_im=cim)
        h, sv, wl, gathered = _layer_fwd(h, memx, tabs, wl, pl_, shards[l], shards[l + 1] if l + 1 < DEPTH else None)
        layers.append((wl, pl_, sv, s5_vjp))

    def f_loss(h_, tgt, g):
        y, r = _rms(h_, g[...])
        err = y - tgt
        part = 0.5 * jnp.sum(jnp.mean(err * err, axis=-1, keepdims=True), axis=0, keepdims=True)
        dh, dg = _rms_bwd(h_, g[...], r, err / D_MODEL)
        return dh, dg, jnp.broadcast_to(part, (8, LANES))
    dh, g_final, loss_part = _rows(
        "loss_head", f_loss, s, min(ROW_TILE, s), [(h, 'r0'), (loss_target[0], 'r0'), (final_norm_g[None], 'f')],
        [((s, D_MODEL), F32, 'r0'), ((1, D_MODEL), F32, 'a'), ((8, LANES), F32, 'a')])
    loss = lax.psum(loss_part[0, 0], ("x", "y", "c"))

    parts = [{} for _ in range(DEPTH)]
    g_rep = [None] * DEPTH
    blocks = None
    for l in reversed(range(DEPTH)):
        wl, pl_, sv, s5_vjp = layers[l]
        dh, gr, arrived = _layer_bwd(dh, sv, memx, tabs, wl, pl_, blocks)
        for (who, n), p in arrived.items():
            parts[l + 1 if who == 'nxt' else l][n] = p
        blocks = {n: _blocked(gr, n) for n in EARLY}
        ds5 = s5_vjp((gr['a_re'], gr['a_im'], gr['b_re'], gr['b_im'], gr['c_re'], gr['c_im']))
        rep = dict(zip(('ssm_lambda_re', 'ssm_lambda_im', 'ssm_log_dt', 'ssm_b_re', 'ssm_b_im', 'ssm_c_re',
                        'ssm_c_im'), ds5))
        for n in ('norm_mix_g', 'q_norm_g', 'kv_norm_g', 'ssm_d', 'ssm_b_glu', 'attn_out_g', 'ssm_out_g',
                  'norm_x_g', 'mem_norm_g', 'norm_ffn_g'):
            rep[n] = gr[n][0]
        g_rep[l] = rep
    grad_x = dh[None]

    rep_names = REPL_L + ['final_norm_g']
    g_loc = {n: jnp.stack([g_rep[l][n] for l in range(DEPTH)]) for n in REPL_L}
    g_loc['final_norm_g'] = g_final
    rest = [n for n in SHARDED if n not in parts[0]]
    last = _run_exchange("last_grads", _together(_scatter(_named(rest, blocks)),
                                                 _gather([_pack(_named(rep_names, g_loc))])))
    parts[0].update(zip(rest, last[:len(rest)]))

    out_sh = [{}, {}, {}, {}]
    for n in SHARDED:
        res = _adamw_weight("adamw_" + n, [parts[l][n] for l in range(DEPTH)], W[n], M[n], V[n])
        for kind, r in enumerate(res):
            out_sh[kind][n] = r

    shapes_rp = [(1,) + W[n].shape if W[n].ndim == 1 else W[n].shape for n in rep_names]
    g_rp = _unpack(_sum_sources("sum_small_grads", last[len(rest)]), shapes_rp)
    as_rows = lambda d: [d[n].reshape(shp) for n, shp in zip(rep_names, shapes_rp)]
    res_rp = (g_rp,) + _adamw_small("adamw_replicated", g_rp, as_rows(W), as_rows(M), as_rows(V))
    out_rp = [{n: a.reshape(W[n].shape) for n, a in zip(rep_names, r)} for r in res_rp]

    outs = [loss, grad_x]
    for kind in range(4):
        for n in WEIGHTS:
            outs.append(out_sh[kind][n] if n in SHARDED else out_rp[kind][n])
    return tuple(outs)
```

```python
from typing import Callable, NamedTuple

import jax
import jax.numpy as jnp
from jax import lax
from jax.experimental import pallas as pl
from jax.experimental.pallas import tpu as pltpu

F32 = jnp.float32
MXU = jnp.bfloat16

D_MODEL = 1024
MLA_HEADS = 8
QK_NOPE = 64
QK_ROPE = 32
V_DIM = 64
Q_LORA = 256
KV_LORA = 128
SSM_WIDTH = 512
SSM_GROUPS = 32
SSM_GROUP = 16
SSM_STATE = 64
X_HEADS = 4
X_HEAD_DIM = 256
D_FF = 2816
FF_CHUNK = D_FF // 2
FF_FWD_CHUNK = D_FF
ROPE_THETA = 10000.0
EPS = 1e-6
DEPTH = 2
N_DEV = 8
LANES = 128
HEAD_W = 128
MLA_SCALE = (QK_NOPE + QK_ROPE) ** -0.5
X_SCALE = X_HEAD_DIM ** -0.5
ADAM_LR, ADAM_B1, ADAM_B2, ADAM_EPS, ADAM_WD, ADAM_STEP = 0.001, 0.9, 0.999, 1e-08, 0.01, 10
VMEM_LIMIT = 56 * 1024 * 1024
FLASH_TILE = 512
DW_ROWS = 2048
X_ROWS = 512
ROW_TILE = 512
FFN_ROWS = 256
MESH = pl.DeviceIdType.MESH

SHARDED = ['w_in', 'w_uq', 'w_ukv', 'ssm_w_glu', 'w_out', 'w_xq', 'w_xkv', 'w_xo', 'w_gate', 'w_up', 'w_down']
REPL_L = ['norm_mix_g', 'q_norm_g', 'kv_norm_g', 'ssm_lambda_re', 'ssm_lambda_im', 'ssm_log_dt', 'ssm_b_re',
          'ssm_b_im', 'ssm_c_re', 'ssm_c_im', 'ssm_d', 'ssm_b_glu', 'attn_out_g', 'ssm_out_g', 'norm_x_g',
          'mem_norm_g', 'norm_ffn_g']
WEIGHTS = ['norm_mix_g', 'w_in', 'q_norm_g', 'w_uq', 'kv_norm_g', 'w_ukv', 'ssm_lambda_re', 'ssm_lambda_im',
           'ssm_log_dt', 'ssm_b_re', 'ssm_b_im', 'ssm_c_re', 'ssm_c_im', 'ssm_d', 'ssm_w_glu', 'ssm_b_glu',
           'attn_out_g', 'ssm_out_g', 'w_out', 'norm_x_g', 'mem_norm_g', 'w_xq', 'w_xkv', 'w_xo', 'norm_ffn_g',
           'w_gate', 'w_up', 'w_down', 'final_norm_g']


def _pcall(body, **kw):
    return pl.pallas_call(body, **kw)


def _mm(a, b):
    return jnp.dot(a.astype(MXU), b.astype(MXU), preferred_element_type=F32)


def _mm_nt(a, b):
    return lax.dot_general(a.astype(MXU), b.astype(MXU), (((1,), (1,)), ((), ())), preferred_element_type=F32)


def _mm_tn(a, b):
    return lax.dot_general(a.astype(MXU), b.astype(MXU), (((0,), (0,)), ((), ())), preferred_element_type=F32)


def _rms(x, g):
    r = lax.rsqrt(jnp.mean(x * x, axis=-1, keepdims=True) + EPS)
    return x * r * g, r


def _rms_bwd(x, g, r, dy):
    dyg = dy * g
    dx = r * dyg - x * (r * r * r) * jnp.mean(dyg * x, axis=-1, keepdims=True)
    return dx, jnp.sum(dy * x * r, axis=0, keepdims=True)


ROT = QK_ROPE // 2


def _rope(x, cos, sin_lo, sin_hi):
    return x * cos + pltpu.roll(x, HEAD_W - ROT, 1) * sin_lo + pltpu.roll(x, ROT, 1) * sin_hi


def _rope_t(g, cos, sin_lo, sin_hi):
    return g * cos + pltpu.roll(g * sin_lo, ROT, 1) + pltpu.roll(g * sin_hi, HEAD_W - ROT, 1)


def _softmax(s):
    m = jnp.max(s, axis=-1, keepdims=True)
    e = jnp.exp(s - m)
    return e / jnp.sum(e, axis=-1, keepdims=True)


def _lanes(x, j, w):
    return x[:, j * w:(j + 1) * w]


def _rows(name, fn, n, tm, ins, outs, side=None):
    def spec(shape, kind):
        nd = len(shape)
        if kind == 'p16':
            return pl.BlockSpec((shape[0], 16, shape[2]), lambda i: (0, jnp.maximum(i * (tm // 16) - 1, 0), 0))
        if kind == 'f':
            return pl.BlockSpec(shape, lambda i, _nd=nd: (0,) * _nd, pipeline_mode=pl.Buffered(1))
        if kind == 'a':
            return pl.BlockSpec(shape, lambda i, _nd=nd: (0,) * _nd)
        ax = int(kind[1])
        blk = tuple(tm if d == ax else s for d, s in enumerate(shape))
        return pl.BlockSpec(blk, lambda i, _ax=ax, _nd=nd: tuple(i if d == _ax else 0 for d in range(_nd)))

    n_in, n_out, n_steps = len(ins), len(outs), n // tm

    def body(*refs):
        in_refs, out_refs, steps = _side_split(refs, n_in, n_out, side)
        i = pl.program_id(0)
        if steps:
            pl.when(i == 0)(steps[0])
            pl.when(i == _pass_on_step(n_steps))(steps[1])
        args = [r if k == 'f' else r[...] for r, (_, k) in zip(in_refs, ins)]
        res = fn(*args)
        for r, (_, dt, k), v in zip(out_refs, outs, res):
            if k == 'a':
                _accumulate(r, v.astype(dt), i)
            else:
                r[...] = v.astype(dt)
        if steps:
            pl.when(i == n_steps - 1)(steps[2])

    s_in, s_out, s_shape, s_sems, s_ops = _side_args(side)
    res = _pcall(
        body, name=name + ("_x" if side else ""), grid=(n_steps,),
        in_specs=[spec(a.shape, k) for a, k in ins] + s_in,
        out_specs=[spec(s, k) for s, _, k in outs] + s_out,
        out_shape=[jax.ShapeDtypeStruct(s, dt) for s, dt, _ in outs] + s_shape,
        scratch_shapes=s_sems,
        compiler_params=pltpu.CompilerParams(dimension_semantics=("arbitrary",), vmem_limit_bytes=VMEM_LIMIT),
    )(*[a for a, _ in ins], *s_ops)
    return _Hosted(res[:n_out], res[n_out:]) if side else res


def _accumulate(ref, v, i):
    @pl.when(i == 0)
    def _():
        ref[...] = v

    @pl.when(i != 0)
    def _():
        ref[...] += v


def _mm_tn_call(name, a, b, tk=None, tn=None):
    out_dtype = MXU
    s, k = a.shape
    n = b.shape[1]
    tk, tn = tk or k, tn or n
    ts = min(DW_ROWS, s)
    ns = s // ts

    def body(a_ref, b_ref, o_ref, acc_ref):
        j = pl.program_id(2)
        _accumulate(acc_ref, _mm_tn(a_ref[...], b_ref[...]), j)

        @pl.when(j == ns - 1)
        def _():
            o_ref[...] = acc_ref[...].astype(out_dtype)

    return _pcall(
        body, name=name, grid=(k // tk, n // tn, ns),
        in_specs=[pl.BlockSpec((ts, tk), lambda ik, jn, j: (j, ik)),
                  pl.BlockSpec((ts, tn), lambda ik, jn, j: (j, jn))],
        out_specs=pl.BlockSpec((tk, tn), lambda ik, jn, j: (ik, jn)),
        out_shape=jax.ShapeDtypeStruct((k, n), out_dtype),
        scratch_shapes=[pltpu.VMEM((tk, tn), F32)],
        compiler_params=pltpu.CompilerParams(dimension_semantics=("arbitrary", "arbitrary", "arbitrary"),
                                             vmem_limit_bytes=VMEM_LIMIT),
    )(a, b)


def _side_split(refs, n_in, n_out, side):
    if side is None:
        return refs[:n_in], refs[n_in:n_in + n_out], None
    si, so = len(side.ins), len(side.out_shapes)
    own_in, side_in = refs[:n_in], refs[n_in:n_in + si]
    own_out, side_out = refs[n_in + si:n_in + si + n_out], refs[n_in + si + n_out:n_in + si + n_out + so]
    return own_in, own_out, side.steps(side_in, side_out, refs[n_in + si + n_out + so:])


def _pass_on_step(n_steps):
    return max(n_steps - 2, 0)


def _side_args(side):
    if side is None:
        return [], [], [], [], []
    any_spec = pl.BlockSpec(memory_space=pl.ANY)
    return ([any_spec] * len(side.ins), [any_spec] * len(side.out_shapes), list(side.out_shapes),
            list(side.sem_shapes), list(side.ins))


class _Hosted(NamedTuple):
    results: list
    arrived: list


def _hosted(res):
    return res if isinstance(res, _Hosted) else _Hosted(res, ())


def _flash_fwd(q, k, v, side=None):
    nh, s, w = q.shape
    t = min(FLASH_TILE, s)
    nq = s // t
    n_steps = (nh // 2) * nq

    def body(*refs):
        (q_ref, k_ref, v_ref), (o_ref, lse_ref), steps = _side_split(refs, 3, 2, side)
        step = pl.program_id(0) * nq + pl.program_id(1)
        if steps:
            pl.when(step == 0)(steps[0])
            pl.when(step == _pass_on_step(n_steps))(steps[1])
        qi = pl.program_id(1)
        qs = [q_ref[0], q_ref[1]]
        below = lax.broadcasted_iota(jnp.int32, (t, t), 1) <= lax.broadcasted_iota(jnp.int32, (t, t), 0)

        def tile(j, carry, diagonal):
            sl = pl.ds(pl.multiple_of(j * t, t), t)
            out = []
            for hh in range(2):
                m, l, acc = carry[3 * hh:3 * hh + 3]
                sc = _mm_nt(qs[hh], k_ref[hh, sl, :])
                if diagonal:
                    sc = jnp.where(below, sc, -1e30)
                m_new = jnp.maximum(m, jnp.max(sc, axis=1, keepdims=True))
                p = jnp.exp(sc - m_new).astype(MXU)
                alpha = jnp.exp(m - m_new)
                out += [m_new, alpha * l + _mm(p, ones), alpha * acc + _mm(p, v_ref[hh, sl, :])]
            return tuple(out)

        ones = jnp.ones((t, w), MXU)
        init = (jnp.full((t, 1), -1e30, F32), jnp.zeros((t, w), F32), jnp.zeros((t, w), F32)) * 2
        carry = lax.fori_loop(0, qi, lambda j, c: tile(j, c, False), init)
        carry = tile(qi, carry, True)
        o_ref[...] = carry[2] / carry[1] + carry[5] / carry[4]
        for hh in range(2):
            lse_ref[hh] = carry[3 * hh] + jnp.log(carry[3 * hh + 1])
        if steps:
            pl.when(step == n_steps - 1)(steps[2])

    s_in, s_out, s_shape, s_sems, s_ops = _side_args(side)
    res = _pcall(
        body, name="mla_flash_fwd" + ("_x" if side else ""), grid=(nh // 2, nq),
        in_specs=[pl.BlockSpec((2, t, w), lambda p, i: (p, i, 0)),
                  pl.BlockSpec((2, s, w), lambda p, i: (p, 0, 0)),
                  pl.BlockSpec((2, s, w), lambda p, i: (p, 0, 0))] + s_in,
        out_specs=[pl.BlockSpec((t, w), lambda p, i: (i, p)),
                   pl.BlockSpec((2, t, w), lambda p, i: (p, i, 0))] + s_out,
        out_shape=[jax.ShapeDtypeStruct((s, (nh // 2) * w), F32), jax.ShapeDtypeStruct((nh, s, w), F32)] + s_shape,
        scratch_shapes=s_sems,
        compiler_params=pltpu.CompilerParams(dimension_semantics=("arbitrary", "arbitrary"),
                                             vmem_limit_bytes=VMEM_LIMIT),
    )(q, k, v, *s_ops)
    return res[0], res[1], res[2:]


def _flash_bwd(q, k, v, o, lse, do, side=None):
    nh, s, w = q.shape
    t = min(FLASH_TILE, s)
    nq = s // t
    n_steps = (nh // 2) * nq

    def body(*refs):
        (q_ref, k_ref, v_ref, o_ref, lse_ref, do_ref), (dq_ref, dk_ref, dv_ref), steps = _side_split(refs, 6, 3, side)
        step = pl.program_id(0) * nq + pl.program_id(1)
        if steps:
            pl.when(step == 0)(steps[0])
            pl.when(step == _pass_on_step(n_steps))(steps[1])
        j = pl.program_id(1)

        @pl.when(j == 0)
        def _():
            dq_ref[...] = jnp.zeros(dq_ref.shape, F32)

        below = lax.broadcasted_iota(jnp.int32, (t, t), 1) <= lax.broadcasted_iota(jnp.int32, (t, t), 0)
        lane = lax.broadcasted_iota(jnp.int32, (t, w), 1)
        heads = [jnp.logical_and(lane >= hh * V_DIM, lane < (hh + 1) * V_DIM) for hh in range(2)]
        ks = [k_ref[0], k_ref[1]]
        vs = [v_ref[0], v_ref[1]]

        def tile(i, carry, diagonal):
            sl = pl.ds(pl.multiple_of(i * t, t), t)
            dout_all, o_all = do_ref[sl, :], o_ref[sl, :]
            out = []
            for hh in range(2):
                dk, dv = carry[2 * hh], carry[2 * hh + 1]
                qh = q_ref[hh, sl, :]
                dout = jnp.where(heads[hh], dout_all, 0.0)
                sc = _mm_nt(qh, ks[hh])
                if diagonal:
                    sc = jnp.where(below, sc, -1e30)
                p = jnp.exp(sc - lse_ref[hh, sl, 0:1])
                dp = _mm_nt(dout, vs[hh])
                ds = p * (dp - jnp.sum(dout * o_all, axis=1, keepdims=True))
                dq_ref[hh, sl, :] += _mm(ds, ks[hh])
                out += [dk + _mm_tn(ds, qh), dv + _mm_tn(p, dout)]
            return tuple(out)

        carry = tile(j, (jnp.zeros((t, w), F32),) * 4, True)
        carry = lax.fori_loop(j + 1, nq, lambda i, c: tile(i, c, False), carry)
        for hh in range(2):
            dk_ref[hh] = carry[2 * hh].astype(MXU)
            dv_ref[hh] = jnp.where(heads[hh], carry[2 * hh + 1], 0.0).astype(MXU)
        if steps:
            pl.when(step == n_steps - 1)(steps[2])

    s_in, s_out, s_shape, s_sems, s_ops = _side_args(side)
    res = _pcall(
        body, name="mla_flash_bwd" + ("_x" if side else ""), grid=(nh // 2, nq),
        in_specs=[pl.BlockSpec((2, s, w), lambda p, j: (p, 0, 0)),
                  pl.BlockSpec((2, t, w), lambda p, j: (p, j, 0)),
                  pl.BlockSpec((2, t, w), lambda p, j: (p, j, 0)),
                  pl.BlockSpec((s, w), lambda p, j: (0, p)),
                  pl.BlockSpec((2, s, w), lambda p, j: (p, 0, 0)),
                  pl.BlockSpec((s, w), lambda p, j: (0, p))] + s_in,
        out_specs=[pl.BlockSpec((2, s, w), lambda p, j: (p, 0, 0)),
                   pl.BlockSpec((2, t, w), lambda p, j: (p, j, 0)),
                   pl.BlockSpec((2, t, w), lambda p, j: (p, j, 0))] + s_out,
        out_shape=[jax.ShapeDtypeStruct((nh, s, w), F32)] + [jax.ShapeDtypeStruct((nh, s, w), MXU)] * 2 + s_shape,
        scratch_shapes=s_sems,
        compiler_params=pltpu.CompilerParams(dimension_semantics=("arbitrary", "arbitrary"),
                                             vmem_limit_bytes=VMEM_LIMIT),
    )(q, k, v, o, lse, do, *s_ops)
    return res[0], res[1], res[2], res[3:]


def _scan(src, w_re, w_im, a_re, a_im, reverse):
    s = src.shape[0]
    nb, w = a_re.shape[0], LANES
    ch = s // 8
    assert ch & (ch - 1) == 0
    grp = 4
    tr = min(512, s)

    def cmul(ar, ai, xr, xi):
        return ar * xr - ai * xi, ar * xi + ai * xr

    def body(src_ref, wr_ref, wi_ref, ar_ref, ai_ref, xr_out, xi_out, xr_ref, xi_ref):
        def project(c, carry):
            rows = pl.ds(pl.multiple_of(c * tr, tr), tr)
            u = src_ref[rows, :]
            if reverse:
                br, bi = _mm_nt(u, wr_ref[...]), _mm_nt(u, wi_ref[...])
            else:
                br, bi = _mm(u, wr_ref[...]), _mm(u, wi_ref[...])
            for g in range(grp):
                xr_ref[g, rows, :] = _lanes(br, g, w)
                xi_ref[g, rows, :] = _lanes(bi, g, w)
            return carry

        lax.fori_loop(0, s // tr, project, 0)
        sub = lax.broadcasted_iota(jnp.int32, (8, w), 0)

        def shift(x, k):
            if reverse:
                return jnp.where(sub < 8 - k, pltpu.roll(x, 8 - k, 0), 0.0)
            return jnp.where(sub >= k, pltpu.roll(x, k, 0), 0.0)

        ar = [jnp.broadcast_to(ar_ref[g], (8, w)) for g in range(grp)]
        ai = [jnp.broadcast_to(ai_ref[g], (8, w)) for g in range(grp)]

        def tsl(i):
            return pl.ds(pl.multiple_of(((ch - 1 - i) if reverse else i) * 8, 8), 8)

        def local(i, carry):
            out = []
            for g in range(grp):
                xr, xi = carry[2 * g], carry[2 * g + 1]
                pr, pi = cmul(ar[g], ai[g], xr, xi)
                nr = pr + xr_ref[g, tsl(i), :]
                ni = pi + xi_ref[g, tsl(i), :]
                xr_ref[g, tsl(i), :] = nr
                xi_ref[g, tsl(i), :] = ni
                out += [nr, ni]
            return tuple(out)

        fin = lax.fori_loop(0, ch, local, (jnp.zeros((8, w), F32),) * (2 * grp))

        carry_in = []
        for g in range(grp):
            pr, pi = ar[g], ai[g]
            for _ in range(ch.bit_length() - 1):
                pr, pi = cmul(pr, pi, pr, pi)
            fr, fi = fin[2 * g], fin[2 * g + 1]
            for kk in (1, 2, 4):
                sr, si = cmul(pr, pi, shift(fr, kk), shift(fi, kk))
                fr, fi = fr + sr, fi + si
                pr, pi = cmul(pr, pi, pr, pi)
            carry_in += [shift(fr, 1), shift(fi, 1)]

        def fix(ii, pw):
            i0, i1 = 2 * ii, 2 * ii + 1
            blk16 = pl.ds(pl.multiple_of(((ch // 2 - 1 - ii) if reverse else ii) * 16, 16), 16)
            out = []
            for g in range(grp):
                p0r, p0i = pw[2 * g], pw[2 * g + 1]
                p1r, p1i = cmul(p0r, p0i, ar[g], ai[g])
                c0r, c0i = cmul(p0r, p0i, carry_in[2 * g], carry_in[2 * g + 1])
                c1r, c1i = cmul(p1r, p1i, carry_in[2 * g], carry_in[2 * g + 1])
                v0r, v0i = xr_ref[g, tsl(i0), :] + c0r, xi_ref[g, tsl(i0), :] + c0i
                v1r, v1i = xr_ref[g, tsl(i1), :] + c1r, xi_ref[g, tsl(i1), :] + c1i
                pair_r, pair_i = ([v1r, v0r], [v1i, v0i]) if reverse else ([v0r, v1r], [v0i, v1i])
                xr_out[g, blk16, :] = jnp.concatenate(pair_r, axis=0).astype(xr_out.dtype)
                xi_out[g, blk16, :] = jnp.concatenate(pair_i, axis=0).astype(xi_out.dtype)
                nr, ni = cmul(p1r, p1i, ar[g], ai[g])
                out += [nr, ni]
            return tuple(out)

        lax.fori_loop(0, ch // 2, fix, tuple(x for g in range(grp) for x in (ar[g], ai[g])))

    per_j = 4 // grp
    blk = pl.BlockSpec((grp, s, w), lambda i: (i, 0, 0))
    ablk = pl.BlockSpec((grp, 1, w), lambda i: (i, 0, 0))
    sblk = pl.BlockSpec((s, w), lambda i: (0, i // per_j))
    if reverse:
        wblk = pl.BlockSpec((None, grp * w, w), lambda i: (i // per_j, i % per_j, 0))
    else:
        wblk = pl.BlockSpec((None, w, grp * w), lambda i: (i // per_j, 0, i % per_j))
    return _pcall(
        body, name="s5_scan_rev" if reverse else "s5_scan", grid=(nb // grp,),
        in_specs=[sblk, wblk, wblk, ablk, ablk], out_specs=[blk, blk],
        out_shape=[jax.ShapeDtypeStruct((nb, s, w), MXU)] * 2,
        scratch_shapes=[pltpu.VMEM((grp, s, w), F32)] * 2,
        compiler_params=pltpu.CompilerParams(dimension_semantics=("arbitrary",), vmem_limit_bytes=VMEM_LIMIT),
    )(src, w_re, w_im, a_re, a_im)


class _Exchange(NamedTuple):
    ins: list
    out_shapes: list
    sem_shapes: list
    steps: Callable


def _gather_steps(ins, outs, sems):
    n = len(ins)
    send_sems, recv_sems, local_sems = sems
    x, y, c = lax.axis_index("x"), lax.axis_index("y"), lax.axis_index("c")
    me, sibling = (x, y, c), (x, y, 1 - c)
    chips = [(1 - x, y), (x, 1 - y), (1 - x, 1 - y)]

    def copy(a, k, block, to, src=None):
        dst = outs[a].at[4 * block[0] + 2 * block[1] + block[2]]
        return pltpu.make_async_remote_copy(
            src_ref=dst if src is None else src, dst_ref=dst,
            send_sem=send_sems.at[a, k], recv_sem=recv_sems.at[a, k], device_id=to, device_id_type=MESH)

    mine = [pltpu.make_async_copy(ins[a], outs[a].at[4 * x + 2 * y + c], local_sems.at[a]) for a in range(n)]
    first = []
    for a in range(n):
        first.append(copy(a, 0, me, sibling, src=ins[a]))
        first += [copy(a, 1 + j, me, (*chip, c), src=ins[a]) for j, chip in enumerate(chips)]
    passed = [copy(a, 4 + j, (*chip, c), sibling) for j, chip in enumerate(chips) for a in range(n)]

    def start():
        for cp in mine + first:
            cp.start()

    def pass_on():
        i = 0
        for j, chip in enumerate(chips):
            for a in range(n):
                copy(a, 1 + j, (*chip, c), me).wait_recv()
                passed[i].start()
                i += 1

    def finish():
        for a in range(n):
            copy(a, 0, sibling, me).wait_recv()
            for j, chip in enumerate(chips):
                copy(a, 4 + j, (*chip, 1 - c), me).wait_recv()
        for cp in first + passed:
            cp.wait_send()
        for cp in mine:
            cp.wait()

    return start, pass_on, finish


def _gather(arrs):
    n = len(arrs)
    return _Exchange(list(arrs), [jax.ShapeDtypeStruct((N_DEV,) + a.shape, a.dtype) for a in arrs],
                     [pltpu.SemaphoreType.DMA((n, 7)), pltpu.SemaphoreType.DMA((n, 7)), pltpu.SemaphoreType.DMA((n,))],
                     _gather_steps)


def _scatter_steps(ins, outs, sems):
    n = len(ins)
    send_sems, recv_sems, local_sems = sems
    x, y, c = lax.axis_index("x"), lax.axis_index("y"), lax.axis_index("c")
    me = 4 * x + 2 * y + c
    own, sent, arrivals = [], [], []
    for a in range(n):
        own.append(pltpu.make_async_copy(ins[a].at[me], outs[a].at[me], local_sems.at[a]))
        for k in range(1, N_DEV):
            px, py, pc = x ^ ((k >> 2) & 1), y ^ ((k >> 1) & 1), c ^ (k & 1)
            peer = 4 * px + 2 * py + pc
            sent.append(pltpu.make_async_remote_copy(
                src_ref=ins[a].at[peer], dst_ref=outs[a].at[me],
                send_sem=send_sems.at[a, k - 1], recv_sem=recv_sems.at[a, k - 1],
                device_id=(px, py, pc), device_id_type=MESH))
            arrivals.append(pltpu.make_async_remote_copy(
                src_ref=ins[a].at[me], dst_ref=outs[a].at[peer],
                send_sem=send_sems.at[a, k - 1], recv_sem=recv_sems.at[a, k - 1],
                device_id=(x, y, c), device_id_type=MESH))

    def start():
        for cp in own + sent:
            cp.start()

    def pass_on():
        pass

    def finish():
        for cp in arrivals:
            cp.wait_recv()
        for cp in sent:
            cp.wait_send()
        for cp in own:
            cp.wait()

    return start, pass_on, finish


def _scatter(grads):
    n = len(grads)
    return _Exchange(list(grads), [jax.ShapeDtypeStruct(g.shape, g.dtype) for g in grads],
                     [pltpu.SemaphoreType.DMA((n, N_DEV - 1)), pltpu.SemaphoreType.DMA((n, N_DEV - 1)),
                      pltpu.SemaphoreType.DMA((n,))], _scatter_steps)


def _together(a, b):
    def steps(ins, outs, sems):
        sa = a.steps(ins[:len(a.ins)], outs[:len(a.out_shapes)], sems[:len(a.sem_shapes)])
        sb = b.steps(ins[len(a.ins):], outs[len(a.out_shapes):], sems[len(a.sem_shapes):])

        def both(k):
            def run():
                sa[k]()
                sb[k]()
            return run
        return both(0), both(1), both(2)

    return _Exchange(a.ins + b.ins, a.out_shapes + b.out_shapes, a.sem_shapes + b.sem_shapes, steps)


def _run_exchange(name, ex):
    n_in, n_out = len(ex.ins), len(ex.out_shapes)

    def body(*refs):
        for step in ex.steps(refs[:n_in], refs[n_in:n_in + n_out], refs[n_in + n_out:]):
            step()

    any_spec = pl.BlockSpec(memory_space=pl.ANY)
    return _pcall(body, name=name, in_specs=[any_spec] * n_in, out_specs=[any_spec] * n_out,
                  out_shape=list(ex.out_shapes), scratch_shapes=list(ex.sem_shapes))(*ex.ins)


def _adam_math(g, w_, m_, v_):
    m_new = ADAM_B1 * m_ + (1.0 - ADAM_B1) * g
    v_new = ADAM_B2 * v_ + (1.0 - ADAM_B2) * (g * g)
    m_hat = m_new / (1.0 - ADAM_B1 ** ADAM_STEP)
    v_hat = v_new / (1.0 - ADAM_B2 ** ADAM_STEP)
    delta = -ADAM_LR * (m_hat / (jnp.sqrt(v_hat) + ADAM_EPS) + ADAM_WD * w_)
    return delta, m_new, v_new


def _adamw_weight(name, parts, w, m, v):
    nl = len(parts)

    def body(*refs):
        p_refs = refs[:nl]
        w_ref, m_ref, v_ref, g_ref, d_ref, mo_ref, vo_ref = refs[nl:]
        for l in range(nl):
            g = p_refs[l][0].astype(F32)
            for j in range(1, N_DEV):
                g = g + p_refs[l][j].astype(F32)
            g_ref[l] = g
            d_ref[l], mo_ref[l], vo_ref[l] = _adam_math(g, w_ref[l], m_ref[l], v_ref[l])

    return _pcall(
        body, name=name, out_shape=[jax.ShapeDtypeStruct(w.shape, F32)] * 4,
        compiler_params=pltpu.CompilerParams(vmem_limit_bytes=VMEM_LIMIT),
    )(*parts, w, m, v)

def _sum_sources(name, parts):
    r = parts.shape[1]

    def body(p_ref, g_ref):
        g = p_ref[0]
        for j in range(1, N_DEV):
            g = g + p_ref[j]
        g_ref[...] = g

    return _pcall(body, name=name, out_shape=jax.ShapeDtypeStruct((r, LANES), F32),
                  compiler_params=pltpu.CompilerParams(vmem_limit_bytes=VMEM_LIMIT))(parts)


def _adamw_small(name, g, w, m, v):
    n = len(g)

    def body(*refs):
        g_r, w_r, m_r, v_r = (refs[k * n:(k + 1) * n] for k in range(4))
        d_r, mo_r, vo_r = (refs[k * n:(k + 1) * n] for k in range(4, 7))
        for i in range(n):
            d_r[i][...], mo_r[i][...], vo_r[i][...] = _adam_math(g_r[i][...], w_r[i][...], m_r[i][...], v_r[i][...])

    res = _pcall(body, name=name, out_shape=[jax.ShapeDtypeStruct(a.shape, F32) for a in w] * 3,
                 compiler_params=pltpu.CompilerParams(vmem_limit_bytes=VMEM_LIMIT))(*g, *w, *m, *v)
    return res[:n], res[n:2 * n], res[2 * n:]


def _pack(arrs):
    flat = jnp.concatenate([a.reshape(-1) for a in arrs])
    flat = jnp.pad(flat, (0, (-flat.shape[0]) % (8 * LANES)))
    return flat.reshape(-1, LANES)


def _unpack(packed, shapes):
    flat = packed.reshape(-1)
    out, off = [], 0
    for shp in shapes:
        size = 1
        for d in shp:
            size *= d
        out.append(flat[off:off + size].reshape(shp))
        off += size
    return out


def _s5_params(lam_re, lam_im, log_dt, b_re, b_im, c_re, c_im):
    dt = jnp.exp(log_dt)[:, None]
    e = jnp.exp(lam_re * dt)
    ang = lam_im * dt
    a_re, a_im = e * jnp.cos(ang), e * jnp.sin(ang)
    nr, ni = a_re - 1.0, a_im
    den = lam_re * lam_re + lam_im * lam_im
    cr = ((nr * lam_re + ni * lam_im) / den)[..., None]
    ci = ((ni * lam_re - nr * lam_im) / den)[..., None]
    bb_re = cr * b_re - ci * b_im
    bb_im = cr * b_im + ci * b_re
    eye = jnp.eye(8, dtype=F32)[None, :, None, :, None]

    def bblk(bb):
        t = jnp.transpose(bb.reshape(4, 8, SSM_STATE, SSM_GROUP), (0, 3, 1, 2))
        return (eye * t[:, None]).reshape(4, 8 * SSM_GROUP, 8 * SSM_STATE)

    def cblk(cc):
        t = jnp.transpose(cc.reshape(4, 8, SSM_GROUP, SSM_STATE), (0, 3, 1, 2))
        return (eye * t[:, None]).reshape(4, 8 * SSM_STATE, 8 * SSM_GROUP)

    nb = SSM_GROUPS * SSM_STATE // LANES
    return (a_re.reshape(nb, 1, LANES), a_im.reshape(nb, 1, LANES), bblk(bb_re), bblk(bb_im),
            cblk(c_re), -cblk(c_im))


def _cat_blocks(x3, j):
    return jnp.concatenate([x3[4 * j + k] for k in range(4)], axis=-1)


def _to_chunks(a):
    s, c = a.shape
    return a.reshape(8, s // 8, c).transpose(1, 0, 2).reshape(s, c)


def _from_chunks(a):
    s, c = a.shape
    return a.reshape(s // 8, 8, c).transpose(1, 0, 2).reshape(s, c)


EARLY = ['w_in', 'w_uq', 'w_ukv']

FWD_PLAN = {
    'flash': ('late', ['ssm_w_glu', 'w_out', 'w_xq', 'w_xkv', 'w_xo', 'w_gate', 'w_up', 'w_down']),
    'ffn': ('nxt', EARLY),
}
BWD_PLAN = {
    'ffn_bwd': ('nxt', EARLY),
    'xattn_bwd': ('own', ['w_down']),
    'flash_bwd': ('own', ['w_gate', 'w_up', 'w_xq', 'w_xkv', 'w_xo']),
    'mla_qkv_bwd': ('own', ['ssm_w_glu', 'w_out']),
}


def _named(names, d):
    return [d[n] for n in names]


def _layer_fwd(h, memx, tabs, wl, pl_, late=None, nxt=None):
    s = h.shape[0]
    tm = min(ROW_TILE, s)
    cos, sin_lo, sin_hi = tabs
    sv = {}
    wl = dict(wl)
    nxt_got = {}

    def fetch(host):
        who, names = FWD_PLAN.get(host, (None, []))
        src = late if who == 'late' else nxt if who == 'nxt' else None
        return _gather(_named(names, src)) if src else None

    def landed(host, got):
        who, names = FWD_PLAN.get(host, (None, []))
        if got and who == 'late':
            wl.update(_layer_weights(dict(zip(names, got))))
        elif got:
            nxt_got.update(zip(names, got))

    def f_mix_in(h_, g, w):
        xn, _ = _rms(h_, g[...])
        pr = _mm(xn, w[...])
        return pr[:, 0:512], pr[:, 512:1024]
    proj, u_nat = _rows("mix_in", f_mix_in, s, tm, [(h, 'r0'), (pl_['norm_mix_g'], 'f'), (wl['w_in'], 'f')],
                        [((s, 512), F32, 'r0')] * 2)

    def f_qkv(pr, cos_, slo, shi, gq, gkv, wq, wk, wv):
        cqn = _rms(pr[:, 0:Q_LORA], gq[...])[0].astype(MXU)
        kvn = _rms(pr[:, Q_LORA:Q_LORA + KV_LORA], gkv[...])[0].astype(MXU)
        krr = _rope(pr[:, 384:512], cos_, slo, shi)
        qs, ks, vs = [], [], []
        for hd in range(MLA_HEADS):
            qs.append(_rope(_mm(cqn, wq[hd]), cos_, slo, shi) * MLA_SCALE)
            ks.append(_mm(kvn, wk[hd]) + krr)
            vs.append(_mm(kvn, wv[hd]))
        return jnp.stack(qs), jnp.stack(ks), jnp.stack(vs)
    hshape = (MLA_HEADS, s, HEAD_W)
    (q, k, v), got = _hosted(_rows(
        "mla_qkv", f_qkv, s, tm,
        [(proj, 'r0'), (cos, 'r0'), (sin_lo, 'r0'), (sin_hi, 'r0'), (pl_['q_norm_g'], 'f'), (pl_['kv_norm_g'], 'f'),
         (wl['w_uq'], 'f'), (wl['w_k'], 'f'), (wl['w_v'], 'f')],
        [(hshape, MXU, 'r1')] * 3, fetch('mla_qkv')))
    landed('mla_qkv', got)

    a_out, lse, got = _flash_fwd(q, k, v, fetch('flash'))
    landed('flash', got)

    u_ch = _to_chunks(u_nat)

    x_re, x_im = _scan(u_ch, pl_['b_re'], pl_['b_im'], pl_['a_re'], pl_['a_im'], False)

    def f_s5_out(xr, xi, u, cre, cim, d, wglu, bglu):
        y = jnp.concatenate([_mm(_cat_blocks(xr, j), cre[j]) + _mm(_cat_blocks(xi, j), cim[j])
                             for j in range(4)], axis=-1) + d[...] * u
        z = _mm(jax.nn.gelu(y), wglu[...]) + bglu[...]
        return y, y * jax.nn.sigmoid(z)
    (y_ssm, s_out_ch), got = _hosted(_rows(
        "s5_out", f_s5_out, s, tm,
        [(x_re, 'r1'), (x_im, 'r1'), (u_ch, 'r0'), (pl_['c_re'], 'f'), (pl_['c_im'], 'f'),
         (pl_['ssm_d'], 'f'), (wl['ssm_w_glu'], 'f'), (pl_['ssm_b_glu'], 'f')],
        [((s, SSM_WIDTH), F32, 'r0')] * 2, fetch('s5_out')))
    landed('s5_out', got)
    s_out = _from_chunks(s_out_ch)

    def f_mix_out(h_, a, so, ga, gs, w):
        an = _rms(a, ga[...])[0]
        sn = _rms(so, gs[...])[0]
        return (h_ + _mm(jnp.concatenate([an, sn], axis=-1), w[...]),)
    (h1,), got = _hosted(_rows("mix_out", f_mix_out, s, tm,
                               [(h, 'r0'), (a_out, 'r0'), (s_out, 'r0'), (pl_['attn_out_g'], 'f'),
                                (pl_['ssm_out_g'], 'f'), (wl['w_out'], 'f')],
                               [((s, D_MODEL), F32, 'r0')], fetch('mix_out')))
    landed('mix_out', got)

    m_len = memx.shape[0]

    def f_memkv(mm_, g, w):
        mn = _rms(mm_, g[...])[0].astype(MXU)
        return (jnp.stack([_mm(mn, w[d]) for d in range(N_DEV)]),)
    kvm, = _rows("mem_kv", f_memkv, m_len, m_len, [(memx, 'r0'), (pl_['mem_norm_g'], 'f'), (wl['w_xkv'], 'f')],
                 [((N_DEV, m_len, X_HEAD_DIM), MXU, 'r1')])

    def f_xattn(h_, g, wq, kv_, wo):
        hn = _rms(h_, g[...])[0].astype(MXU)
        q_all = _mm(hn, wq[...]).astype(MXU)
        outs = []
        for hd in range(X_HEADS):
            p = _softmax(_mm_nt(_lanes(q_all, hd, X_HEAD_DIM), kv_[hd]) * X_SCALE)
            outs.append(_mm(p, kv_[X_HEADS + hd]).astype(MXU))
        return (h_ + _mm(jnp.concatenate(outs, axis=-1), wo[...]),)
    (h2,), got = _hosted(_rows("xattn", f_xattn, s, min(X_ROWS, s),
                               [(h1, 'r0'), (pl_['norm_x_g'], 'f'), (wl['w_xq'], 'f'), (kvm, 'f'), (wl['w_xo'], 'f')],
                               [((s, D_MODEL), F32, 'r0')], fetch('xattn')))
    landed('xattn', got)

    def f_ffn(h_, g, wg, wu, wd):
        hn = _rms(h_, g[...])[0].astype(MXU)
        y = jnp.zeros(h_.shape, F32)
        gates, ups = [], []
        for c in range(D_FF // FF_FWD_CHUNK):
            cs = pl.ds(c * FF_FWD_CHUNK, FF_FWD_CHUNK)
            gate, up = _mm(hn, wg[:, cs]), _mm(hn, wu[:, cs])
            y = y + _mm(gate * jax.nn.sigmoid(gate) * up, wd[cs, :])
            gates.append(gate)
            ups.append(up)
        return h_ + y, jnp.concatenate(gates, axis=-1), jnp.concatenate(ups, axis=-1)
    (h3, gate_f, up_f), got = _hosted(_rows(
        "ffn", f_ffn, s, min(FFN_ROWS, s),
        [(h2, 'r0'), (pl_['norm_ffn_g'], 'f'), (wl['w_gate'], 'f'), (wl['w_up'], 'f'), (wl['w_down'], 'f')],
        [((s, D_MODEL), F32, 'r0'), ((s, D_FF), MXU, 'r0'), ((s, D_FF), MXU, 'r0')], fetch('ffn')))
    landed('ffn', got)
    sv.update(h=h, proj=proj, q=q, k=k, v=v, a_out=a_out, lse=lse, x_re=x_re, x_im=x_im, y_ssm=y_ssm,
              s_out=s_out, h1=h1, kvm=kvm, h2=h2, u_ch=u_ch, gate=gate_f, up=up_f)
    return h3, sv, wl, nxt_got


def _layer_bwd(dh3, sv, memx, tabs, wl, pl_, nxt=None):
    s = dh3.shape[0]
    tm = min(ROW_TILE, s)
    cos, sin_lo, sin_hi = tabs
    gr = {}
    arrived = {}
    act_shape = (s, D_FF)

    def send(host):
        who, names = BWD_PLAN.get(host, (None, []))
        if who is None or (who == 'nxt' and not nxt):
            return None, []
        return (_scatter([nxt[n] if who == 'nxt' else _blocked(gr, n) for n in names]),
                [(who, n) for n in names])

    def f_ffn_bwd(h_, dy, gate_, up_, g, wg, wu, wd):
        hn, r = _rms(h_, g[...])
        hb = hn.astype(MXU)
        dyb = dy.astype(MXU)
        dhn = jnp.zeros(h_.shape, F32)
        acts, dgs, dus = [], [], []
        for c in range(D_FF // FF_CHUNK):
            cs = pl.ds(c * FF_CHUNK, FF_CHUNK)
            gate = _lanes(gate_, c, FF_CHUNK).astype(F32)
            up = _lanes(up_, c, FF_CHUNK).astype(F32)
            sg = jax.nn.sigmoid(gate)
            si = gate * sg
            dact = _mm_nt(dyb, wd[cs, :])
            dgate = (dact * up * (sg * (1.0 + gate * (1.0 - sg)))).astype(MXU)
            dup = (dact * si).astype(MXU)
            dhn = dhn + _mm_nt(dgate, wg[:, cs]) + _mm_nt(dup, wu[:, cs])
            acts.append((si * up).astype(MXU))
            dgs.append(dgate)
            dus.append(dup)
        dh, dg = _rms_bwd(h_, g[...], r, dhn)
        cat = lambda parts: jnp.concatenate(parts, axis=-1)
        return dy + dh, hb, cat(acts), cat(dgs), cat(dus), dg
    ex, keys = send('ffn_bwd')
    (dh2, hn_f, act, dgate, dup, gr['norm_ffn_g']), got = _hosted(_rows(
        "ffn_bwd", f_ffn_bwd, s, min(FFN_ROWS, s),
        [(sv['h2'], 'r0'), (dh3, 'r0'), (sv['gate'], 'r0'), (sv['up'], 'r0'), (pl_['norm_ffn_g'], 'f'),
         (wl['w_gate'], 'f'), (wl['w_up'], 'f'), (wl['w_down'], 'f')],
        [((s, D_MODEL), F32, 'r0'), ((s, D_MODEL), MXU, 'r0'), (act_shape, MXU, 'r0'), (act_shape, MXU, 'r0'),
         (act_shape, MXU, 'r0'), ((1, D_MODEL), F32, 'a')], ex))
    arrived.update(zip(keys, got))
    gr['w_gate'] = _mm_tn_call("dw_gate", hn_f, dgate, tn=FF_CHUNK)
    gr['w_up'] = _mm_tn_call("dw_up", hn_f, dup, tn=FF_CHUNK)
    gr['w_down'] = _mm_tn_call("dw_down", act, dh3, tk=FF_CHUNK)

    m_len = memx.shape[0]

    def f_xattn_bwd(h_, dy, g, wq, kv_, wo):
        hn, r = _rms(h_, g[...])
        hb = hn.astype(MXU)
        q_all = _mm(hb, wq[...]).astype(MXU)
        do_all = _mm_nt(dy, wo[...]).astype(MXU)
        dqs, ohs, dks, dvs = [], [], [], []
        for hd in range(X_HEADS):
            kh, vh = kv_[hd], kv_[X_HEADS + hd]
            qh, do = _lanes(q_all, hd, X_HEAD_DIM), _lanes(do_all, hd, X_HEAD_DIM)
            p = _softmax(_mm_nt(qh, kh) * X_SCALE)
            ohs.append(_mm(p, vh).astype(MXU))
            dvs.append(_mm_tn(p, do))
            dp = _mm_nt(do, vh)
            ds = p * (dp - jnp.sum(dp * p, axis=-1, keepdims=True)) * X_SCALE
            dqs.append(_mm(ds, kh).astype(MXU))
            dks.append(_mm_tn(ds, qh))
        dq_all = jnp.concatenate(dqs, axis=-1)
        dh, dg = _rms_bwd(h_, g[...], r, _mm_nt(dq_all, wq[...]))
        return dy + dh, hb, dq_all, jnp.concatenate(ohs, axis=-1), jnp.stack(dks + dvs), dg
    ex, keys = send('xattn_bwd')
    (dh1, hn_x, dq_x, oh_x, dkvm, gr['norm_x_g']), got = _hosted(_rows(
        "xattn_bwd", f_xattn_bwd, s, min(X_ROWS, s),
        [(sv['h1'], 'r0'), (dh2, 'r0'), (pl_['norm_x_g'], 'f'), (wl['w_xq'], 'f'), (sv['kvm'], 'f'),
         (wl['w_xo'], 'f')],
        [((s, D_MODEL), F32, 'r0'), ((s, D_MODEL), MXU, 'r0'), ((s, D_MODEL), MXU, 'r0'),
         ((s, D_MODEL), MXU, 'r0'), ((N_DEV, m_len, X_HEAD_DIM), F32, 'a'), ((1, D_MODEL), F32, 'a')], ex))
    arrived.update(zip(keys, got))
    gr['w_xq'] = _mm_tn_call("dw_xq", hn_x, dq_x)
    gr['w_xo'] = _mm_tn_call("dw_xo", oh_x, dh2)

    def f_memkv_bwd(mm_, dkv, g, w):
        mn, r = _rms(mm_, g[...])
        mb = mn.astype(MXU)
        dmn = jnp.zeros(mm_.shape, F32)
        dws = []
        for d in range(N_DEV):
            dmn = dmn + _mm_nt(dkv[d], w[d])
            dws.append(_mm_tn(mb, dkv[d]))
        _, dg = _rms_bwd(mm_, g[...], r, dmn)
        return jnp.stack(dws), dg
    gr['w_xkv'], gr['mem_norm_g'] = _rows(
        "mem_kv_bwd", f_memkv_bwd, m_len, m_len,
        [(memx, 'r0'), (dkvm, 'r1'), (pl_['mem_norm_g'], 'f'), (wl['w_xkv'], 'f')],
        [((N_DEV, D_MODEL, X_HEAD_DIM), F32, 'a'), ((1, D_MODEL), F32, 'a')])

    def f_mix_out_bwd(a, so, dy, ga, gs, w):
        dmix = _mm_nt(dy, w[...])
        an, ra = _rms(a, ga[...])
        sn, rs = _rms(so, gs[...])
        da, dga = _rms_bwd(a, ga[...], ra, dmix[:, 0:512])
        dso, dgs = _rms_bwd(so, gs[...], rs, dmix[:, 512:1024])
        return da, dso, jnp.concatenate([an, sn], axis=-1), dga, dgs
    da_out, ds_out, mixed, gr['attn_out_g'], gr['ssm_out_g'] = _rows(
        "mix_out_bwd", f_mix_out_bwd, s, tm,
        [(sv['a_out'], 'r0'), (sv['s_out'], 'r0'), (dh1, 'r0'), (pl_['attn_out_g'], 'f'), (pl_['ssm_out_g'], 'f'),
         (wl['w_out'], 'f')],
        [((s, 512), F32, 'r0'), ((s, 512), F32, 'r0'), ((s, D_MODEL), MXU, 'r0'), ((1, 512), F32, 'a'),
         ((1, 512), F32, 'a')])
    gr['w_out'] = _mm_tn_call("dw_out", mixed, dh1)

    ex, keys = send('flash_bwd')
    dq, dk, dv, got = _flash_bwd(sv['q'], sv['k'], sv['v'], sv['a_out'], sv['lse'], da_out, ex)
    arrived.update(zip(keys, got))

    def f_s5_out_bwd(xr, xi, u, y, ds, cre, cim, d, wglu, bglu):
        g, gelu_vjp = jax.vjp(jax.nn.gelu, y)
        sig = jax.nn.sigmoid(_mm(g, wglu[...]) + bglu[...])
        dz = ds * y * sig * (1.0 - sig)
        dy = ds * sig + gelu_vjp(_mm_nt(dz, wglu[...]))[0]
        dcr, dci = [], []
        for j in range(4):
            dyj = _lanes(dy, j, LANES)
            dcr.append(_mm_tn(_cat_blocks(xr, j), dyj))
            dci.append(_mm_tn(_cat_blocks(xi, j), dyj))
        return (dy, dy * d[...], jnp.stack(dcr), jnp.stack(dci),
                jnp.sum(dy * u, axis=0, keepdims=True), _mm_tn(g, dz), jnp.sum(dz, axis=0, keepdims=True))
    ex, keys = send('s5_out_bwd')
    (dy_ssm, du_dir, gr['c_re'], gr['c_im'], gr['ssm_d'], gr['ssm_w_glu'], gr['ssm_b_glu']), got = _hosted(_rows(
        "s5_out_bwd", f_s5_out_bwd, s, tm,
        [(sv['x_re'], 'r1'), (sv['x_im'], 'r1'), (sv['u_ch'], 'r0'), (sv['y_ssm'], 'r0'), (_to_chunks(ds_out), 'r0'),
         (pl_['c_re'], 'f'), (pl_['c_im'], 'f'), (pl_['ssm_d'], 'f'), (wl['ssm_w_glu'], 'f'),
         (pl_['ssm_b_glu'], 'f')],
        [((s, 512), F32, 'r0'), ((s, 512), F32, 'r0'), ((4, 512, LANES), F32, 'a'),
         ((4, 512, LANES), F32, 'a'), ((1, 512), F32, 'a'), ((512, 512), F32, 'a'), ((1, 512), F32, 'a')], ex))
    arrived.update(zip(keys, got))
    g_re, g_im = _scan(dy_ssm, pl_['c_re'], pl_['c_im'], pl_['a_re'], -pl_['a_im'], True)
    first_re = jnp.pad(sv['x_re'][:, s - 8:s - 1].astype(F32), ((0, 0), (1, 0), (0, 0)))
    first_im = jnp.pad(sv['x_im'][:, s - 8:s - 1].astype(F32), ((0, 0), (1, 0), (0, 0)))

    def f_s5_in_bwd(gre, gim, xr, xi, pr16, pi16, u, dud, f8r, f8i, bre, bim):
        first = pl.program_id(0) == 0
        xr32, xi32 = xr.astype(F32), xi.astype(F32)
        xpr = jnp.concatenate([jnp.where(first, f8r[...], pr16.astype(F32)[:, 8:16]), xr32[:, :tm - 8]], axis=1)
        xpi = jnp.concatenate([jnp.where(first, f8i[...], pi16.astype(F32)[:, 8:16]), xi32[:, :tm - 8]], axis=1)
        gre32, gim32 = gre.astype(F32), gim.astype(F32)
        dus, dbr, dbi = [], [], []
        for j in range(4):
            gj_r, gj_i, uj = _cat_blocks(gre, j), _cat_blocks(gim, j), _lanes(u, j, LANES)
            dus.append(_mm_nt(gj_r, bre[j]) + _mm_nt(gj_i, bim[j]))
            dbr.append(_mm_tn(uj, gj_r))
            dbi.append(_mm_tn(uj, gj_i))
        da_r = jnp.sum(gre32 * xpr + gim32 * xpi, axis=1, keepdims=True)
        da_i = jnp.sum(gim32 * xpr - gre32 * xpi, axis=1, keepdims=True)
        return dud + jnp.concatenate(dus, axis=-1), jnp.stack(dbr), jnp.stack(dbi), da_r, da_i
    ex, keys = send('s5_in_bwd')
    (du_ch, gr['b_re'], gr['b_im'], gr['a_re'], gr['a_im']), got = _hosted(_rows(
        "s5_in_bwd", f_s5_in_bwd, s, tm,
        [(g_re, 'r1'), (g_im, 'r1'), (sv['x_re'], 'r1'), (sv['x_im'], 'r1'), (sv['x_re'], 'p16'), (sv['x_im'], 'p16'),
         (sv['u_ch'], 'r0'), (du_dir, 'r0'), (first_re, 'f'), (first_im, 'f'), (pl_['b_re'], 'f'), (pl_['b_im'], 'f')],
        [((s, 512), F32, 'r0'), ((4, LANES, 512), F32, 'a'), ((4, LANES, 512), F32, 'a'),
         ((16, 1, LANES), F32, 'a'), ((16, 1, LANES), F32, 'a')], ex))
    arrived.update(zip(keys, got))
    du = _from_chunks(du_ch)

    def f_qkv_bwd(pr, cos_, slo, shi, dq_, dk_, dv_, gq, gkv, wq, wk, wv):
        cq, ckv = pr[:, 0:Q_LORA], pr[:, Q_LORA:Q_LORA + KV_LORA]
        cqn, rq = _rms(cq, gq[...])
        kvn, rkv = _rms(ckv, gkv[...])
        cqb, kvb = cqn.astype(MXU), kvn.astype(MXU)
        dcqn = jnp.zeros(cq.shape, F32)
        dkvn = jnp.zeros(ckv.shape, F32)
        dksum = jnp.zeros(dk_[0].shape, F32)
        dwq, dwk, dwv = [], [], []
        for hd in range(MLA_HEADS):
            dqp = (_rope_t(dq_[hd], cos_, slo, shi) * MLA_SCALE).astype(MXU)
            dkb, dvb = dk_[hd].astype(MXU), dv_[hd].astype(MXU)
            dwq.append(_mm_tn(cqb, dqp))
            dwk.append(_mm_tn(kvb, dkb))
            dwv.append(_mm_tn(kvb, dvb))
            dcqn = dcqn + _mm_nt(dqp, wq[hd])
            dkvn = dkvn + _mm_nt(dkb, wk[hd]) + _mm_nt(dvb, wv[hd])
            dksum = dksum + dk_[hd]
        dcq, dgq = _rms_bwd(cq, gq[...], rq, dcqn)
        dckv, dgkv = _rms_bwd(ckv, gkv[...], rkv, dkvn)
        dpa = jnp.concatenate([dcq, dckv, _rope_t(dksum, cos_, slo, shi)], axis=-1)
        return dpa, jnp.stack(dwq), jnp.stack(dwk), jnp.stack(dwv), dgq, dgkv
    ex, keys = send('mla_qkv_bwd')
    (dpa, gr['w_uq'], gr['w_k'], gr['w_v'], gr['q_norm_g'], gr['kv_norm_g']), got = _hosted(_rows(
        "mla_qkv_bwd", f_qkv_bwd, s, tm,
        [(sv['proj'], 'r0'), (cos, 'r0'), (sin_lo, 'r0'), (sin_hi, 'r0'), (dq, 'r1'), (dk, 'r1'), (dv, 'r1'),
         (pl_['q_norm_g'], 'f'), (pl_['kv_norm_g'], 'f'), (wl['w_uq'], 'f'), (wl['w_k'], 'f'), (wl['w_v'], 'f')],
        [((s, 512), F32, 'r0'), ((MLA_HEADS, Q_LORA, HEAD_W), F32, 'a'), ((MLA_HEADS, KV_LORA, HEAD_W), F32, 'a'),
         ((MLA_HEADS, KV_LORA, HEAD_W), F32, 'a'), ((1, Q_LORA), F32, 'a'), ((1, KV_LORA), F32, 'a')], ex))
    arrived.update(zip(keys, got))

    def f_mix_in_bwd(h_, dpa_, du_, dres, g, w):
        dproj = jnp.concatenate([dpa_, du_], axis=-1).astype(MXU)
        xn, r = _rms(h_, g[...])
        dh, dg = _rms_bwd(h_, g[...], r, _mm_nt(dproj, w[...]))
        return dres + dh, xn, dproj, dg
    ex, keys = send('mix_in_bwd')
    (dh0, xn, dproj, gr['norm_mix_g']), got = _hosted(_rows(
        "mix_in_bwd", f_mix_in_bwd, s, tm,
        [(sv['h'], 'r0'), (dpa, 'r0'), (du, 'r0'), (dh1, 'r0'), (pl_['norm_mix_g'], 'f'), (wl['w_in'], 'f')],
        [((s, D_MODEL), F32, 'r0'), ((s, D_MODEL), MXU, 'r0'), ((s, D_MODEL), MXU, 'r0'), ((1, D_MODEL), F32, 'a')],
        ex))
    arrived.update(zip(keys, got))
    gr['w_in'] = _mm_tn_call("dw_in", xn, dproj)
    return dh0, gr, arrived


def _layer_weights(w):
    wl = {}
    if 'w_in' in w:
        w_in = w['w_in'].reshape(D_MODEL, -1)
        z = lambda n: jnp.zeros((D_MODEL, n), w_in.dtype)
        wl['w_in'] = jnp.concatenate([w_in[:, :384], z(64), w_in[:, 384:416], z(32), w_in[:, 416:]], axis=1)
    if 'w_uq' in w:
        wl['w_uq'] = jnp.pad(w['w_uq'], ((0, 0), (0, 0), (0, HEAD_W - QK_NOPE - QK_ROPE)))
    if 'w_ukv' in w:
        wl['w_k'] = jnp.pad(w['w_ukv'][..., :QK_NOPE], ((0, 0), (0, 0), (0, HEAD_W - QK_NOPE)))
        wv = w['w_ukv'][..., QK_NOPE:]
        even = (jnp.arange(MLA_HEADS) % 2 == 0)[:, None, None]
        wl['w_v'] = jnp.concatenate([jnp.where(even, wv, 0), jnp.where(even, 0, wv)], axis=-1).astype(wv.dtype)
    if 'ssm_w_glu' in w:
        wl['ssm_w_glu'] = w['ssm_w_glu'].reshape(SSM_WIDTH, SSM_WIDTH)
    for n in ('w_out', 'w_xq', 'w_xo'):
        if n in w:
            wl[n] = w[n].reshape(D_MODEL, D_MODEL)
    if 'w_xkv' in w:
        wl['w_xkv'] = w['w_xkv']
    for n in ('w_gate', 'w_up'):
        if n in w:
            wl[n] = jnp.transpose(w[n], (1, 0, 2)).reshape(D_MODEL, D_FF)
    if 'w_down' in w:
        wl['w_down'] = w['w_down'].reshape(D_FF, D_MODEL)
    return wl


def _blocked(gr, n):
    if n == 'w_in':
        d = gr['w_in']
        out = jnp.concatenate([d[:, :384], d[:, 448:480], d[:, 512:]], axis=1).reshape(N_DEV, 128, -1)
    elif n == 'w_uq':
        out = gr['w_uq'][..., :QK_NOPE + QK_ROPE]
    elif n == 'w_ukv':
        even = (jnp.arange(MLA_HEADS) % 2 == 0)[:, None, None]
        dv = gr['w_v']
        out = jnp.concatenate([gr['w_k'][..., :QK_NOPE], jnp.where(even, dv[..., :V_DIM], dv[..., V_DIM:])], axis=-1)
    elif n == 'ssm_w_glu':
        out = gr['ssm_w_glu'].reshape(N_DEV, SSM_WIDTH // N_DEV, SSM_WIDTH)
    elif n in ('w_out', 'w_xq', 'w_xo'):
        out = gr[n].reshape(N_DEV, D_MODEL // N_DEV, D_MODEL)
    elif n in ('w_gate', 'w_up'):
        out = jnp.transpose(gr[n].reshape(D_MODEL, N_DEV, D_FF // N_DEV), (1, 0, 2))
    elif n == 'w_down':
        out = gr[n].reshape(N_DEV, D_FF // N_DEV, D_MODEL)
    else:
        out = gr[n]
    return out.astype(MXU)


def kernel(x, mem, positions, norm_mix_g, w_in, q_norm_g, w_uq, kv_norm_g, w_ukv, ssm_lambda_re, ssm_lambda_im, ssm_log_dt, ssm_b_re, ssm_b_im, ssm_c_re, ssm_c_im, ssm_d, ssm_w_glu, ssm_b_glu, attn_out_g, ssm_out_g, w_out, norm_x_g, mem_norm_g, w_xq, w_xkv, w_xo, norm_ffn_g, w_gate, w_up, w_down, final_norm_g, loss_target, m_norm_mix_g, m_w_in, m_q_norm_g, m_w_uq, m_kv_norm_g, m_w_ukv, m_ssm_lambda_re, m_ssm_lambda_im, m_ssm_log_dt, m_ssm_b_re, m_ssm_b_im, m_ssm_c_re, m_ssm_c_im, m_ssm_d, m_ssm_w_glu, m_ssm_b_glu, m_attn_out_g, m_ssm_out_g, m_w_out, m_norm_x_g, m_mem_norm_g, m_w_xq, m_w_xkv, m_w_xo, m_norm_ffn_g, m_w_gate, m_w_up, m_w_down, m_final_norm_g, v_norm_mix_g, v_w_in, v_q_norm_g, v_w_uq, v_kv_norm_g, v_w_ukv, v_ssm_lambda_re, v_ssm_lambda_im, v_ssm_log_dt, v_ssm_b_re, v_ssm_b_im, v_ssm_c_re, v_ssm_c_im, v_ssm_d, v_ssm_w_glu, v_ssm_b_glu, v_attn_out_g, v_ssm_out_g, v_w_out, v_norm_x_g, v_mem_norm_g, v_w_xq, v_w_xkv, v_w_xo, v_norm_ffn_g, v_w_gate, v_w_up, v_w_down, v_final_norm_g):
    args = dict(locals())
    W = {n: args[n] for n in WEIGHTS}
    M = {n: args['m_' + n] for n in WEIGHTS}
    V = {n: args['v_' + n] for n in WEIGHTS}
    s = x.shape[1]
    h = x[0]
    memx = mem[0]

    freqs = ROPE_THETA ** (-jnp.arange(0, QK_ROPE, 2, dtype=F32) / QK_ROPE)
    ang = positions[0].astype(F32)[:, None] * freqs
    c16, s16 = jnp.cos(ang), jnp.sin(ang)
    zeros = lambda n: jnp.zeros((s, n), F32)
    cos = jnp.concatenate([jnp.ones((s, QK_NOPE), F32), c16, c16, zeros(32)], axis=1)
    sin_lo = jnp.concatenate([zeros(QK_NOPE), -s16, zeros(ROT + 32)], axis=1)
    sin_hi = jnp.concatenate([zeros(QK_NOPE + ROT), s16, zeros(32)], axis=1)
    tabs = (cos, sin_lo, sin_hi)

    shards = [{n: W[n][l].astype(MXU) for n in SHARDED} for l in range(DEPTH)]
    gathered = dict(zip(EARLY, _run_exchange("gather_weights", _gather(_named(EARLY, shards[0])))))

    layers = []
    for l in range(DEPTH):
        wl = _layer_weights(gathered)
        s5_in = [W[n][l] for n in ('ssm_lambda_re', 'ssm_lambda_im', 'ssm_log_dt', 'ssm_b_re', 'ssm_b_im',
                                   'ssm_c_re', 'ssm_c_im')]
        (a_re, a_im, bre, bim, cre, cim), s5_vjp = jax.vjp(_s5_params, *s5_in)
        pl_ = {n: W[n][l][None] for n in ('norm_mix_g', 'q_norm_g', 'kv_norm_g', 'ssm_d', 'ssm_b_glu',
                                           'attn_out_g', 'ssm_out_g', 'norm_x_g', 'mem_norm_g', 'norm_ffn_g')}
        pl_.update(a_re=a_re, a_im=a_im, b_re=bre, b_im=bim, c_re=cre, c_im=cim)
        h, sv, wl, gathered = _layer_fwd(h, memx, tabs, wl, pl_, shards[l], shards[l + 1] if l + 1 < DEPTH else None)
        layers.append((wl, pl_, sv, s5_vjp))

    def f_loss(h_, tgt, g):
        y, r = _rms(h_, g[...])
        err = y - tgt
        part = 0.5 * jnp.sum(jnp.mean(err * err, axis=-1, keepdims=True), axis=0, keepdims=True)
        dh, dg = _rms_bwd(h_, g[...], r, err / D_MODEL)
        return dh, dg, jnp.broadcast_to(part, (8, LANES))
    dh, g_final, loss_part = _rows(
        "loss_head", f_loss, s, min(ROW_TILE, s), [(h, 'r0'), (loss_target[0], 'r0'), (final_norm_g[None], 'f')],
        [((s, D_MODEL), F32, 'r0'), ((1, D_MODEL), F32, 'a'), ((8, LANES), F32, 'a')])
    loss = lax.psum(loss_part[0, 0], ("x", "y", "c"))

    parts = [{} for _ in range(DEPTH)]
    g_rep = [None] * DEPTH
    blocks = None
    for l in reversed(range(DEPTH)):
        wl, pl_, sv, s5_vjp = layers[l]
        dh, gr, arrived = _layer_bwd(dh, sv, memx, tabs, wl, pl_, blocks)
        for (who, n), p in arrived.items():
            parts[l + 1 if who == 'nxt' else l][n] = p
        blocks = {n: _blocked(gr, n) for n in EARLY}
        ds5 = s5_vjp((gr['a_re'], gr['a_im'], gr['b_re'], gr['b_im'], gr['c_re'], gr['c_im']))
        rep = dict(zip(('ssm_lambda_re', 'ssm_lambda_im', 'ssm_log_dt', 'ssm_b_re', 'ssm_b_im', 'ssm_c_re',
                        'ssm_c_im'), ds5))
        for n in ('norm_mix_g', 'q_norm_g', 'kv_norm_g', 'ssm_d', 'ssm_b_glu', 'attn_out_g', 'ssm_out_g',
                  'norm_x_g', 'mem_norm_g', 'norm_ffn_g'):
            rep[n] = gr[n][0]
        g_rep[l] = rep
    grad_x = dh[None]

    rep_names = REPL_L + ['final_norm_g']
    g_loc = {n: jnp.stack([g_rep[l][n] for l in range(DEPTH)]) for n in REPL_L}
    g_loc['final_norm_g'] = g_final
    rest = [n for n in SHARDED if n not in parts[0]]
    last = _run_exchange("last_grads", _together(_scatter(_named(rest, blocks)),
                                                 _gather([_pack(_named(rep_names, g_loc))])))
    parts[0].update(zip(rest, last[:len(rest)]))

    out_sh = [{}, {}, {}, {}]
    for n in SHARDED:
        res = _adamw_weight("adamw_" + n, [parts[l][n] for l in range(DEPTH)], W[n], M[n], V[n])
        for kind, r in enumerate(res):
            out_sh[kind][n] = r

    shapes_rp = [(1,) + W[n].shape if W[n].ndim == 1 else W[n].shape for n in rep_names]
    g_rp = _unpack(_sum_sources("sum_small_grads", last[len(rest)]), shapes_rp)
    as_rows = lambda d: [d[n].reshape(shp) for n, shp in zip(rep_names, shapes_rp)]
    res_rp = (g_rp,) + _adamw_small("adamw_replicated", g_rp, as_rows(W), as_rows(M), as_rows(V))
    out_rp = [{n: a.reshape(W[n].shape) for n, a in zip(rep_names, r)} for r in res_rp]

    outs = [loss, grad_x]
    for kind in range(4):
        for n in WEIGHTS:
            outs.append(out_sh[kind][n] if n in SHARDED else out_rp[kind][n])
    return tuple(outs)
```
